```python
import jax, jax.numpy as jnp
from jax import lax
import numpy as np

D_MODEL = 1024
BATCH = 32
SEQ = 2048
DEPTH = 1

CHUNK = 128
A_GROUPS = 8
A_GROUP_DIM = D_MODEL // A_GROUPS
A_WIDTH = A_GROUPS * A_GROUP_DIM
B_HEADS = 8
B_HEAD_DIM = D_MODEL // B_HEADS
B_WIDTH = B_HEADS * B_HEAD_DIM
Q_BLOCK = 128
EPS = 1e-6

IN_WIDTHS = (A_WIDTH, A_WIDTH, A_WIDTH, B_WIDTH, B_WIDTH, B_WIDTH, B_WIDTH, D_MODEL, D_MODEL)
IN_PROJ_WIDTH = sum(IN_WIDTHS)
SPLIT_POINTS = tuple(int(p) for p in np.cumsum(IN_WIDTHS)[:-1])

kernel_name = "hybrid_gmlp_stickbreaking_gated_block"


def rms_norm(x, gain):
    xf = x.astype(jnp.float32)
    y = xf * lax.rsqrt(jnp.mean(xf * xf, axis=-1, keepdims=True) + EPS)
    return (y * gain.astype(jnp.float32)).astype(x.dtype)


def chunked_spatial_gating(u, v, w_s, b_s):
    bsz, seq, _ = u.shape
    n_chunks = seq // CHUNK
    vr = v.reshape(bsz, n_chunks, CHUNK, A_GROUPS, A_GROUP_DIM)
    causal = jnp.tril(jnp.ones((CHUNK, CHUNK), dtype=bool))
    w = jnp.where(causal[None], w_s, 0.0).astype(v.dtype)
    mixed = jnp.einsum('gts,bnsgc->bntgc', w, vr)
    mixed = mixed + b_s.T.astype(v.dtype)[None, None, :, :, None]
    return u * mixed.reshape(bsz, seq, A_WIDTH)


def stick_breaking_attention(q, k, v):
    bsz, seq, n_heads, head_dim = q.shape
    n_blocks = seq // Q_BLOCK
    scale = head_dim ** -0.5
    q_blocks = q.reshape(bsz, n_blocks, Q_BLOCK, n_heads, head_dim).transpose(1, 0, 2, 3, 4)
    key_pos = jnp.arange(seq)

    def one_block(args):
        q_blk, blk_idx = args
        logits = jnp.einsum('bthd,bshd->bhts', q_blk, k).astype(jnp.float32) * scale
        q_pos = blk_idx * Q_BLOCK + jnp.arange(Q_BLOCK)
        causal = key_pos[None, :] < q_pos[:, None]
        log_beta = jax.nn.log_sigmoid(logits)
        log_one_minus = jnp.where(causal, log_beta - logits, 0.0)
        suffix = lax.cumsum(log_one_minus, axis=3, reverse=True) - log_one_minus
        weights = jnp.where(causal, jnp.exp(log_beta + suffix), 0.0)
        return jnp.einsum('bhts,bshd->bthd', weights.astype(v.dtype), v)

    out = lax.map(one_block, (q_blocks, jnp.arange(n_blocks)))
    return out.transpose(1, 0, 2, 3, 4).reshape(bsz, seq, n_heads, head_dim)


def _fwd_setup_inputs(seed: int = 0) -> dict:
    key = jax.random.key(seed)
    ks = jax.random.split(key, 11)
    f32 = jnp.float32
    x = jax.random.normal(ks[0], (BATCH, SEQ, D_MODEL), f32)
    norm_in = 1.0 + 0.02 * jax.random.normal(ks[1], (DEPTH, D_MODEL), f32)
    w_in = jax.random.normal(ks[2], (DEPTH, D_MODEL, IN_PROJ_WIDTH), f32) * D_MODEL ** -0.5
    norm_v = 1.0 + 0.02 * jax.random.normal(ks[3], (DEPTH, A_WIDTH), f32)
    w_s = jax.random.normal(ks[4], (DEPTH, A_GROUPS, CHUNK, CHUNK), f32) * (0.5 * CHUNK ** -0.5)
    b_s = 1.0 + 0.02 * jax.random.normal(ks[5], (DEPTH, A_GROUPS, CHUNK), f32)
    w_o_gmlp = jax.random.normal(ks[6], (DEPTH, A_WIDTH, D_MODEL), f32) * A_WIDTH ** -0.5
    w_o_sb = jax.random.normal(ks[7], (DEPTH, B_WIDTH, D_MODEL), f32) * B_WIDTH ** -0.5
    w_out = jax.random.normal(ks[8], (DEPTH, D_MODEL, D_MODEL), f32) * D_MODEL ** -0.5
    norm_final = 1.0 + 0.02 * jax.random.normal(ks[9], (D_MODEL,), f32)
    return {"x": x, "norm_in": norm_in, "w_in": w_in, "norm_v": norm_v, "w_s": w_s,
            "b_s": b_s, "w_o_gmlp": w_o_gmlp, "w_o_sb": w_o_sb, "w_out": w_out,
            "norm_final": norm_final}


def _fwd_reference(x, norm_in, w_in, norm_v, w_s, b_s, w_o_gmlp, w_o_sb, w_out, norm_final):
    bsz, seq, _ = x.shape
    for layer in range(DEPTH):
        h = rms_norm(x, norm_in[layer])
        proj = jnp.einsum('bsd,de->bse', h, w_in[layer])
        u_a, v_a, z_a, q_b, k_b, v_b, z_b, gate_a, gate_b = jnp.split(proj, SPLIT_POINTS, axis=-1)

        u_a = jax.nn.gelu(u_a)
        v_a = rms_norm(jax.nn.gelu(v_a), norm_v[layer])
        y_a = chunked_spatial_gating(u_a, v_a, w_s[layer], b_s[layer]) * jax.nn.silu(z_a)

        heads = lambda t: t.reshape(bsz, seq, B_HEADS, B_HEAD_DIM)
        y_b = stick_breaking_attention(heads(q_b), heads(k_b), heads(v_b)).reshape(bsz, seq, B_WIDTH)
        y_b = y_b * jax.nn.silu(z_b)

        p_a = jnp.einsum('bse,ed->bsd', y_a, w_o_gmlp[layer])
        p_b = jnp.einsum('bse,ed->bsd', y_b, w_o_sb[layer])
        merged = jax.nn.sigmoid(gate_a) * p_a + jax.nn.sigmoid(gate_b) * p_b
        x = x + jnp.einsum('bsd,de->bse', merged, w_out[layer])
    return rms_norm(x, norm_final)


import jax as _jax
import jax.numpy as _jnp

TWIN_FORMAT = 'train_step'
FWD_PARAMS = ['x', 'norm_in', 'w_in', 'norm_v', 'w_s', 'b_s', 'w_o_gmlp', 'w_o_sb', 'w_out', 'norm_final']
TWIN_WEIGHTS = ['norm_in', 'w_in', 'norm_v', 'w_s', 'b_s', 'w_o_gmlp', 'w_o_sb', 'w_out', 'norm_final']
TWIN_DIFF_INPUT = 'x'
TWIN_INPUTS = ['x', 'norm_in', 'w_in', 'norm_v', 'w_s', 'b_s', 'w_o_gmlp', 'w_o_sb', 'w_out', 'norm_final', 'loss_target', 'm_norm_in', 'm_w_in', 'm_norm_v', 'm_w_s', 'm_b_s', 'm_w_o_gmlp', 'm_w_o_sb', 'm_w_out', 'm_norm_final', 'v_norm_in', 'v_w_in', 'v_norm_v', 'v_w_s', 'v_b_s', 'v_w_o_gmlp', 'v_w_o_sb', 'v_w_out', 'v_norm_final']
TWIN_OUTPUTS = ['loss', 'grad_x', 'grad_norm_in', 'grad_w_in', 'grad_norm_v', 'grad_w_s', 'grad_b_s', 'grad_w_o_gmlp', 'grad_w_o_sb', 'grad_w_out', 'grad_norm_final', 'delta_norm_in', 'delta_w_in', 'delta_norm_v', 'delta_w_s', 'delta_b_s', 'delta_w_o_gmlp', 'delta_w_o_sb', 'delta_w_out', 'delta_norm_final', 'new_m_norm_in', 'new_m_w_in', 'new_m_norm_v', 'new_m_w_s', 'new_m_b_s', 'new_m_w_o_gmlp', 'new_m_w_o_sb', 'new_m_w_out', 'new_m_norm_final', 'new_v_norm_in', 'new_v_w_in', 'new_v_norm_v', 'new_v_w_s', 'new_v_b_s', 'new_v_w_o_gmlp', 'new_v_w_o_sb', 'new_v_w_out', 'new_v_norm_final']
TWIN_LEAF_KINDS = {'loss': 'loss', 'grad_x': 'grad_x', 'grad_norm_in': 'grad_w', 'grad_w_in': 'grad_w', 'grad_norm_v': 'grad_w', 'grad_w_s': 'grad_w', 'grad_b_s': 'grad_w', 'grad_w_o_gmlp': 'grad_w', 'grad_w_o_sb': 'grad_w', 'grad_w_out': 'grad_w', 'grad_norm_final': 'grad_w', 'delta_norm_in': 'delta_w', 'delta_w_in': 'delta_w', 'delta_norm_v': 'delta_w', 'delta_w_s': 'delta_w', 'delta_b_s': 'delta_w', 'delta_w_o_gmlp': 'delta_w', 'delta_w_o_sb': 'delta_w', 'delta_w_out': 'delta_w', 'delta_norm_final': 'delta_w', 'new_m_norm_in': 'new_m', 'new_m_w_in': 'new_m', 'new_m_norm_v': 'new_m', 'new_m_w_s': 'new_m', 'new_m_b_s': 'new_m', 'new_m_w_o_gmlp': 'new_m', 'new_m_w_o_sb': 'new_m', 'new_m_w_out': 'new_m', 'new_m_norm_final': 'new_m', 'new_v_norm_in': 'new_v', 'new_v_w_in': 'new_v', 'new_v_norm_v': 'new_v', 'new_v_w_s': 'new_v', 'new_v_b_s': 'new_v', 'new_v_w_o_gmlp': 'new_v', 'new_v_w_o_sb': 'new_v', 'new_v_w_out': 'new_v', 'new_v_norm_final': 'new_v'}


def _forward(args):
    return _fwd_reference(*[args[k] for k in FWD_PARAMS])


def _output_shape():
    out = _jax.eval_shape(lambda: _forward(_fwd_setup_inputs(0)))
    return out.shape, out.dtype

N_MICROBATCH = 1
ADAM_LR = 0.001
ADAM_B1 = 0.9
ADAM_B2 = 0.999
ADAM_EPS = 1e-08
ADAM_WD = 0.01
ADAM_STEP = 10
PER_EXAMPLE_BATCH_AXIS = {'x': 0, 'loss_target': 0}
SHARED_INPUTS = []
_WEIGHT_DTYPES = {'norm_in': _jnp.float32, 'w_in': _jnp.float32, 'norm_v': _jnp.float32, 'w_s': _jnp.float32, 'b_s': _jnp.float32, 'w_o_gmlp': _jnp.float32, 'w_o_sb': _jnp.float32, 'w_out': _jnp.float32, 'norm_final': _jnp.float32}
MOMENT_SCALE = {'norm_in': 1.165209e-01, 'w_in': 3.902654e-02, 'norm_v': 1.802063e-02, 'w_s': 3.547865e-02, 'b_s': 5.183438e-02, 'w_o_gmlp': 5.373648e-02, 'w_o_sb': 5.044601e-02, 'w_out': 7.346040e-02, 'norm_final': 6.387293e+01}


def _to_microbatches(a, axis):
    t = _jnp.moveaxis(a, axis, 0)
    t = t.reshape((N_MICROBATCH, t.shape[0] // N_MICROBATCH) + t.shape[1:])
    return _jnp.moveaxis(t, 1, axis + 1)


def setup_inputs(seed: int = 0) -> dict:
    inp = _fwd_setup_inputs(seed)
    key = _jax.random.fold_in(_jax.random.key(seed), 7919)
    shape, _ = _output_shape()
    out = dict(inp)
    out["loss_target"] = _jax.random.normal(_jax.random.fold_in(key, 0), shape, _jnp.float32)
    for i, name in enumerate(TWIN_WEIGHTS):
        w = inp[name].astype(_jnp.float32)
        if MOMENT_SCALE is None:
            s = _jnp.sqrt(_jnp.mean(_jnp.square(w)) + 1e-30)
        else:
            s = MOMENT_SCALE[name]
        km, kv = _jax.random.split(_jax.random.fold_in(key, i + 1))
        out[name] = w
        out["m_" + name] = s * _jax.random.normal(km, w.shape, _jnp.float32)
        out["v_" + name] = (s * s) * _jax.random.uniform(kv, w.shape, _jnp.float32, 0.5, 1.5)
    if N_MICROBATCH > 1:
        for name, axis in PER_EXAMPLE_BATCH_AXIS.items():
            out[name] = _to_microbatches(out[name], axis)
    return {'x': out['x'], 'norm_in': out['norm_in'], 'w_in': out['w_in'], 'norm_v': out['norm_v'], 'w_s': out['w_s'], 'b_s': out['b_s'], 'w_o_gmlp': out['w_o_gmlp'], 'w_o_sb': out['w_o_sb'], 'w_out': out['w_out'], 'norm_final': out['norm_final'], 'loss_target': out['loss_target'], 'm_norm_in': out['m_norm_in'], 'm_w_in': out['m_w_in'], 'm_norm_v': out['m_norm_v'], 'm_w_s': out['m_w_s'], 'm_b_s': out['m_b_s'], 'm_w_o_gmlp': out['m_w_o_gmlp'], 'm_w_o_sb': out['m_w_o_sb'], 'm_w_out': out['m_w_out'], 'm_norm_final': out['m_norm_final'], 'v_norm_in': out['v_norm_in'], 'v_w_in': out['v_w_in'], 'v_norm_v': out['v_norm_v'], 'v_w_s': out['v_w_s'], 'v_b_s': out['v_b_s'], 'v_w_o_gmlp': out['v_w_o_gmlp'], 'v_w_o_sb': out['v_w_o_sb'], 'v_w_out': out['v_w_out'], 'v_norm_final': out['v_norm_final']}


def _loss(weights, diff, rest, loss_target):
    with _jax.named_scope("forward"):
        args = {**rest, TWIN_DIFF_INPUT: diff, **{k: w.astype(_WEIGHT_DTYPES[k]) for k, w in weights.items()}}
        y = _forward(args)
    with _jax.named_scope("loss_head"):
        err = _jnp.square(y.astype(_jnp.float32) - loss_target)
        return 0.5 * _jnp.sum(_jnp.mean(err, axis=-1)) if err.ndim else 0.5 * err


def _adamw(w, g, m, v):
    m = ADAM_B1 * m + (1.0 - ADAM_B1) * g
    v = ADAM_B2 * v + (1.0 - ADAM_B2) * _jnp.square(g)
    m_hat = m / (1.0 - ADAM_B1 ** ADAM_STEP)
    v_hat = v / (1.0 - ADAM_B2 ** ADAM_STEP)
    delta = -ADAM_LR * (m_hat / (_jnp.sqrt(v_hat) + ADAM_EPS) + ADAM_WD * w)
    return delta, m, v


def reference(x, norm_in, w_in, norm_v, w_s, b_s, w_o_gmlp, w_o_sb, w_out, norm_final, loss_target, m_norm_in, m_w_in, m_norm_v, m_w_s, m_b_s, m_w_o_gmlp, m_w_o_sb, m_w_out, m_norm_final, v_norm_in, v_w_in, v_norm_v, v_w_s, v_b_s, v_w_o_gmlp, v_w_o_sb, v_w_out, v_norm_final):
    given = dict(x=x, norm_in=norm_in, w_in=w_in, norm_v=norm_v, w_s=w_s, b_s=b_s, w_o_gmlp=w_o_gmlp, w_o_sb=w_o_sb, w_out=w_out, norm_final=norm_final, loss_target=loss_target, m_norm_in=m_norm_in, m_w_in=m_w_in, m_norm_v=m_norm_v, m_w_s=m_w_s, m_b_s=m_b_s, m_w_o_gmlp=m_w_o_gmlp, m_w_o_sb=m_w_o_sb, m_w_out=m_w_out, m_norm_final=m_norm_final, v_norm_in=v_norm_in, v_w_in=v_w_in, v_norm_v=v_norm_v, v_w_s=v_w_s, v_b_s=v_b_s, v_w_o_gmlp=v_w_o_gmlp, v_w_o_sb=v_w_o_sb, v_w_out=v_w_out, v_norm_final=v_norm_final)
    weights = {n: given[n] for n in TWIN_WEIGHTS}
    shared = {n: given[n] for n in SHARED_INPUTS}
    per_example = {n: given[n] for n in ['x']}
    grad_fn = _jax.value_and_grad(_loss, argnums=(0, 1))

    def one_microbatch(ex, loss_target):
        ex = dict(ex)
        diff = ex.pop(TWIN_DIFF_INPUT)
        return grad_fn(weights, diff, {**shared, **ex}, loss_target)

    if N_MICROBATCH == 1:
        loss, (grad_w, grad_x) = one_microbatch(per_example, given["loss_target"])
    else:
        def body(carry, xs):
            loss_sum, grad_sum = carry
            l_k, (gw_k, gx_k) = one_microbatch(xs[0], xs[1])
            with _jax.named_scope("update"):
                return (loss_sum + l_k, _jax.tree.map(_jnp.add, grad_sum, gw_k)), gx_k

        init = (_jnp.zeros((), _jnp.float32), _jax.tree.map(_jnp.zeros_like, weights))
        (loss, grad_w), grad_x = _jax.lax.scan(body, init, (per_example, given["loss_target"]))
    with _jax.named_scope("update"):
        delta_w, new_m, new_v = {}, {}, {}
        for n in TWIN_WEIGHTS:
            delta_w[n], new_m[n], new_v[n] = _adamw(weights[n], grad_w[n], given["m_" + n], given["v_" + n])
    return (loss, grad_x, *[grad_w[n] for n in TWIN_WEIGHTS], *[delta_w[n] for n in TWIN_WEIGHTS],
            *[new_m[n] for n in TWIN_WEIGHTS], *[new_v[n] for n in TWIN_WEIGHTS])
```

```python
import functools
import math

import jax
import jax.numpy as jnp
from jax import lax
from jax.experimental import pallas as pl
from jax.experimental.pallas import tpu as pltpu

F32 = jnp.float32
BF16 = jnp.bfloat16
EPS = 1e-6
HEADS = 8
N_SPLIT = 9
N_CHIP = 4
MESH = pl.DeviceIdType.MESH

ADAM_LR = 0.001
ADAM_B1 = 0.9
ADAM_B2 = 0.999
ADAM_EPS = 1e-08
ADAM_WD = 0.01
ADAM_STEP = 10

VMEM_LIMIT = 56 * 2 ** 20
TM = 256
TQ = 128
TK = 128
TKW = 512

NT = (((1,), (1,)), ((), ()))
TN = (((0,), (0,)), ((), ()))


def _params(sem):
    return pltpu.CompilerParams(dimension_semantics=sem, vmem_limit_bytes=VMEM_LIMIT)


def _resident(shape):
    nd = len(shape)
    return pl.BlockSpec(shape, lambda *_: (0,) * nd, pipeline_mode=pl.Buffered(1))


def _const(shape):
    nd = len(shape)
    return pl.BlockSpec(shape, lambda *_: (0,) * nd)


def _segments(d, ncol):
    segs = []
    edges = sorted({j * ncol for j in range(N_CHIP + 1)} | {s * d for s in range(N_SPLIT + 1)})
    for lo, hi in zip(edges[:-1], edges[1:]):
        segs.append((lo // ncol, lo % ncol, lo // d, lo % d, hi - lo))
    return segs


def _sigmoid(x):
    return 1.0 / (1.0 + jnp.exp(-x))


_GELU_C = math.sqrt(2.0 / math.pi)


def _gelu(x):
    return 0.5 * x * (1.0 + jnp.tanh(_GELU_C * (x + 0.044715 * (x * x * x))))


def _gelu_and_grad(x):
    x2 = x * x
    th = jnp.tanh(_GELU_C * (x + 0.044715 * (x2 * x)))
    val = 0.5 * x * (1.0 + th)
    grad = 0.5 * (1.0 + th) + 0.5 * x * (1.0 - th * th) * (_GELU_C * (1.0 + 3.0 * 0.044715 * x2))
    return val, grad


def _split_bf16(a):
    hi = a.astype(BF16)
    lo = (a - hi.astype(F32)).astype(BF16)
    return hi, lo


def _in_proj_fwd(x2d, g_in, wg):
    t, d = x2d.shape
    ncol = wg.shape[2]
    segs = _segments(d, ncol)

    def body(x_ref, g_ref, w_ref, apre_ref, qkv_ref, gpre_ref, ht_ref):
        x = x_ref[...]
        r = lax.rsqrt(jnp.mean(x * x, axis=-1, keepdims=True) + EPS)
        hf = x * r * g_ref[...]
        h = hf.astype(BF16)
        ht_ref[...] = hf.T.astype(BF16)
        outs = (apre_ref, qkv_ref, gpre_ref)
        for chip, c0, grp, s0, width in segs:
            res = jnp.dot(h, w_ref[chip, :, c0:c0 + width], preferred_element_type=F32)
            o = outs[grp // 3]
            off = (grp % 3) * d + s0
            o[:, off:off + width] = res.astype(o.dtype)

    return pl.pallas_call(
        body, name="in_proj_fwd",
        grid=(t // TM,),
        in_specs=[pl.BlockSpec((TM, d), lambda i: (i, 0)), _const((1, d)), _resident(wg.shape)],
        out_specs=[pl.BlockSpec((TM, 3 * d), lambda i: (i, 0)), pl.BlockSpec((TM, 3 * d), lambda i: (i, 0)),
                   pl.BlockSpec((TM, 3 * d), lambda i: (i, 0)), pl.BlockSpec((d, TM), lambda i: (0, i))],
        out_shape=[jax.ShapeDtypeStruct((t, 3 * d), F32), jax.ShapeDtypeStruct((t, 3 * d), BF16),
                   jax.ShapeDtypeStruct((t, 3 * d), F32), jax.ShapeDtypeStruct((d, t), BF16)],
        compiler_params=_params(("arbitrary",)),
    )(x2d, g_in, wg)


def _branch_a_fwd(a_pre, g_v, wm, b_t):
    t, d3 = a_pre.shape
    d = d3 // 3
    ng, chunk, _ = wm.shape
    cw = d // ng

    def body(a_ref, gv_ref, wm_ref, bt_ref, ya_ref):
        ua = _gelu(a_ref[:, 0:d])
        vg = _gelu(a_ref[:, d:2 * d])
        za = a_ref[:, 2 * d:3 * d]
        rv = lax.rsqrt(jnp.mean(vg * vg, axis=-1, keepdims=True) + EPS)
        va = (vg * rv * gv_ref[...]).astype(BF16)
        gate = ua * (za * _sigmoid(za))
        for g in range(ng):
            sl = slice(g * cw, (g + 1) * cw)
            mixed = jnp.dot(wm_ref[g], va[:, sl], preferred_element_type=F32) + bt_ref[:, g:g + 1]
            ya_ref[:, sl] = (gate[:, sl] * mixed).astype(BF16)

    return pl.pallas_call(
        body, name="branch_a_fwd",
        grid=(t // chunk,),
        in_specs=[pl.BlockSpec((chunk, d3), lambda i: (i, 0)), _const((1, d)), _const(wm.shape), _const(b_t.shape)],
        out_specs=pl.BlockSpec((chunk, d), lambda i: (i, 0)),
        out_shape=jax.ShapeDtypeStruct((t, d), BF16),
        compiler_params=_params(("arbitrary",)),
    )(a_pre, g_v, wm, b_t)


def _branch_a_bwd(a_pre, dya, g_v, wm, wm_t, b_t):
    t, d3 = a_pre.shape
    d = d3 // 3
    ng, chunk, _ = wm.shape
    cw = d // ng
    nsteps = t // chunk

    def body(a_ref, dya_ref, gv_ref, wm_ref, wmt_ref, bt_ref, da_ref, gws_ref, gbt_ref, gnv_ref, db_acc):
        i = pl.program_id(0)

        @pl.when(i == 0)
        def _():
            gws_ref[...] = jnp.zeros_like(gws_ref)
            gnv_ref[...] = jnp.zeros_like(gnv_ref)
            db_acc[...] = jnp.zeros_like(db_acc)

        ua, dgelu_u = _gelu_and_grad(a_ref[:, 0:d])
        vg, dgelu_v = _gelu_and_grad(a_ref[:, d:2 * d])
        za = a_ref[:, 2 * d:3 * d]
        sig = _sigmoid(za)
        sz = za * sig
        dsz = sig * (1.0 + za * (1.0 - sig))
        rv = lax.rsqrt(jnp.mean(vg * vg, axis=-1, keepdims=True) + EPS)
        nv = vg * rv
        gv = gv_ref[...]
        va = (nv * gv).astype(BF16)
        dya = dya_ref[...]
        dmix = dya * ua * sz
        db_acc[...] += dmix
        dmix_b = dmix.astype(BF16)
        t_gate = dya * sz
        t_z = dya * ua * dsz
        dva_parts = []
        for g in range(ng):
            sl = slice(g * cw, (g + 1) * cw)
            mixed = jnp.dot(wm_ref[g], va[:, sl], preferred_element_type=F32) + bt_ref[:, g:g + 1]
            da_ref[:, sl] = (t_gate[:, sl] * mixed * dgelu_u[:, sl]).astype(BF16)
            da_ref[:, 2 * d + g * cw:2 * d + (g + 1) * cw] = (t_z[:, sl] * mixed).astype(BF16)
            gws_ref[g] += lax.dot_general(dmix_b[:, sl], va[:, sl], NT, preferred_element_type=F32)
            dva_parts.append(jnp.dot(wmt_ref[g], dmix_b[:, sl], preferred_element_type=F32))
        dva = jnp.concatenate(dva_parts, axis=1)
        gnv_ref[...] += jnp.sum(dva * nv, axis=0, keepdims=True)
        dnv = dva * gv
        dvg = rv * (dnv - nv * jnp.mean(dnv * nv, axis=-1, keepdims=True))
        da_ref[:, d:2 * d] = (dvg * dgelu_v).astype(BF16)

        @pl.when(i == nsteps - 1)
        def _():
            acc = db_acc[...]
            for g in range(ng):
                gbt_ref[:, g:g + 1] = jnp.sum(acc[:, g * cw:(g + 1) * cw], axis=1, keepdims=True)

    return pl.pallas_call(
        body, name="branch_a_bwd",
        grid=(nsteps,),
        in_specs=[pl.BlockSpec((chunk, d3), lambda i: (i, 0)), pl.BlockSpec((chunk, d), lambda i: (i, 0)),
                  _const((1, d)), _const(wm.shape), _const(wm_t.shape), _const(b_t.shape)],
        out_specs=[pl.BlockSpec((chunk, d3), lambda i: (i, 0)), _const(wm.shape), _const(b_t.shape), _const((1, d))],
        out_shape=[jax.ShapeDtypeStruct((t, d3), BF16), jax.ShapeDtypeStruct(wm.shape, F32),
                   jax.ShapeDtypeStruct(b_t.shape, F32), jax.ShapeDtypeStruct((1, d), F32)],
        scratch_shapes=[pltpu.VMEM((chunk, d), F32)],
        compiler_params=_params(("arbitrary",)),
    )(a_pre, dya, g_v, wm, wm_t, b_t)


def _tri(rows_gt_cols):
    r = lax.broadcasted_iota(jnp.int32, (TK, TK), 0)
    c = lax.broadcasted_iota(jnp.int32, (TK, TK), 1)
    return (r > c) if rows_gt_cols else (r < c)


def _sb_block(q, k, scale, carry, causal):
    z = lax.dot_general(q, k, NT, preferred_element_type=F32) * scale
    sp = jnp.maximum(z, 0.0) + jnp.log(1.0 + jnp.exp(-jnp.abs(z)))
    lom = -sp
    if causal is not None:
        lom = jnp.where(causal, lom, 0.0)
    upper = _tri(True).astype(BF16)
    hi, lo = _split_bf16(lom)
    sx = jnp.dot(hi, upper, preferred_element_type=F32) + jnp.dot(lo, upper, preferred_element_type=F32)
    log_beta = z - sp
    a = jnp.exp(log_beta + sx + carry)
    if causal is not None:
        a = jnp.where(causal, a, 0.0)
    return a, log_beta, carry + sx[:, 0:1] + lom[:, 0:1]


def _attn_fwd(qkv, g_pre, bsz, seq):
    t, d3 = qkv.shape
    d = d3 // 3
    hd = d // HEADS
    nq = seq // TQ
    scale = hd ** -0.5
    assert TQ == TK

    def body(q_ref, k_ref, v_ref, zb_ref, o_ref, yb_ref):
        i = pl.program_id(2)
        q = q_ref[...]
        causal = _tri(True)

        def step(kb, carry, acc, mask):
            rows = pl.ds(pl.multiple_of(kb * TK, TK), TK)
            a, _, carry = _sb_block(q, k_ref[rows, :], scale, carry, mask)
            acc = acc + jnp.dot(a.astype(BF16), v_ref[rows, :], preferred_element_type=F32)
            return carry, acc

        carry, acc = step(i, jnp.zeros((TQ, 1), F32), jnp.zeros((TQ, hd), F32), causal)
        carry, acc = lax.fori_loop(0, i, lambda n, c: step(i - 1 - n, c[0], c[1], None), (carry, acc))
        o_ref[...] = acc
        zb = zb_ref[...]
        yb_ref[...] = (acc * (zb * _sigmoid(zb))).astype(BF16)

    row_blk = lambda b, h, i: (b * nq + i, h)
    return pl.pallas_call(
        body, name="attn_fwd",
        grid=(bsz, HEADS, nq),
        in_specs=[pl.BlockSpec((TQ, hd), row_blk),
                  pl.BlockSpec((seq, hd), lambda b, h, i: (b, HEADS + h)),
                  pl.BlockSpec((seq, hd), lambda b, h, i: (b, 2 * HEADS + h)),
                  pl.BlockSpec((TQ, hd), row_blk)],
        out_specs=[pl.BlockSpec((TQ, hd), row_blk), pl.BlockSpec((TQ, hd), row_blk)],
        out_shape=[jax.ShapeDtypeStruct((t, d), F32), jax.ShapeDtypeStruct((t, d), BF16)],
        compiler_params=_params(("arbitrary", "arbitrary", "arbitrary")),
    )(qkv, qkv, qkv, g_pre)


def _attn_bwd(qkv, g_pre, o, dyb, bsz, seq):
    t, d3 = qkv.shape
    d = d3 // 3
    hd = d // HEADS
    nq = seq // TQ
    scale = hd ** -0.5

    def body(q_ref, k_ref, v_ref, zb_ref, o_ref, dyb_ref, dq_ref, dk_ref, dv_ref, dzb_ref,
             g_s, beta_s, dk_acc, dv_acc):
        i = pl.program_id(2)

        @pl.when(i == 0)
        def _():
            dk_acc[...] = jnp.zeros_like(dk_acc)
            dv_acc[...] = jnp.zeros_like(dv_acc)

        q = q_ref[...]
        zb = zb_ref[...]
        sig = _sigmoid(zb)
        dyb_t = dyb_ref[...]
        do = (dyb_t * (zb * sig)).astype(BF16)
        dzb_ref[...] = (dyb_t * o_ref[...] * (sig * (1.0 + zb * (1.0 - sig)))).astype(BF16)
        causal = _tri(True)

        def sweep(kb, carry, mask):
            rows = pl.ds(pl.multiple_of(kb * TK, TK), TK)
            a, log_beta, carry = _sb_block(q, k_ref[rows, :], scale, carry, mask)
            beta = jnp.exp(log_beta)
            if mask is not None:
                beta = jnp.where(mask, beta, 0.0)
            da = lax.dot_general(do, v_ref[rows, :], NT, preferred_element_type=F32)
            g_s[kb] = a * da
            beta_s[kb] = beta
            dv_acc[rows, :] += lax.dot_general(a.astype(BF16), do, TN, preferred_element_type=F32)
            return carry

        carry = sweep(i, jnp.zeros((TQ, 1), F32), causal)
        lax.fori_loop(0, i, lambda n, c: sweep(i - 1 - n, c, None), carry)

        lower = _tri(False).astype(BF16)

        def back(kb, c):
            p_carry, dq = c
            rows = pl.ds(pl.multiple_of(kb * TK, TK), TK)
            g = g_s[kb]
            beta = beta_s[kb]
            hi, lo = _split_bf16(g)
            px = jnp.dot(hi, lower, preferred_element_type=F32) + jnp.dot(lo, lower, preferred_element_type=F32)
            dz = ((g * (1.0 - beta) - (p_carry + px) * beta) * scale).astype(BF16)
            dq = dq + jnp.dot(dz, k_ref[rows, :], preferred_element_type=F32)
            dk_acc[rows, :] += lax.dot_general(dz, q, TN, preferred_element_type=F32)
            return p_carry + px[:, TK - 1:TK] + g[:, TK - 1:TK], dq

        _, dq = lax.fori_loop(0, i + 1, back, (jnp.zeros((TQ, 1), F32), jnp.zeros((TQ, hd), F32)))
        dq_ref[...] = dq.astype(BF16)

        @pl.when(i == nq - 1)
        def _():
            dk_ref[...] = dk_acc[...].astype(BF16)
            dv_ref[...] = dv_acc[...].astype(BF16)

    row_blk = lambda b, h, i: (b * nq + i, h)
    seq_blk = lambda b, h, i: (b, h)
    out = jax.ShapeDtypeStruct((t, d), BF16)
    return pl.pallas_call(
        body, name="attn_bwd",
        grid=(bsz, HEADS, nq),
        in_specs=[pl.BlockSpec((TQ, hd), row_blk),
                  pl.BlockSpec((seq, hd), lambda b, h, i: (b, HEADS + h)),
                  pl.BlockSpec((seq, hd), lambda b, h, i: (b, 2 * HEADS + h)),
                  pl.BlockSpec((TQ, hd), row_blk), pl.BlockSpec((TQ, hd), row_blk), pl.BlockSpec((TQ, hd), row_blk)],
        out_specs=[pl.BlockSpec((TQ, hd), row_blk), pl.BlockSpec((seq, hd), seq_blk),
                   pl.BlockSpec((seq, hd), seq_blk), pl.BlockSpec((TQ, hd), row_blk)],
        out_shape=[out, out, out, out],
        scratch_shapes=[pltpu.VMEM((nq, TQ, TK), F32), pltpu.VMEM((nq, TQ, TK), F32),
                        pltpu.VMEM((seq, hd), F32), pltpu.VMEM((seq, hd), F32)],
        compiler_params=_params(("arbitrary", "arbitrary", "arbitrary")),
    )(qkv, qkv, qkv, g_pre, o, dyb)


def _out_proj(ya, yb, g_pre, x2d, tgt, w_og, w_osb, w_out, g_f):
    t, d = x2d.shape

    def body(ya_ref, yb_ref, ga_ref, gb_ref, x_ref, tgt_ref, wog_ref, wosb_ref, wout_ref, gf_ref,
             dya_ref, dyb_ref, dg_ref, dx2_ref, loss_ref, gnf_ref, gwog_ref, gwosb_ref, gwout_ref):
        @pl.when(pl.program_id(0) == 0)
        def _():
            loss_ref[...] = jnp.zeros_like(loss_ref)
            gnf_ref[...] = jnp.zeros_like(gnf_ref)
            gwog_ref[...] = jnp.zeros_like(gwog_ref)
            gwosb_ref[...] = jnp.zeros_like(gwosb_ref)
            gwout_ref[...] = jnp.zeros_like(gwout_ref)

        ya = ya_ref[...]
        yb = yb_ref[...]
        pa = jnp.dot(ya, wog_ref[...], preferred_element_type=F32)
        pb = jnp.dot(yb, wosb_ref[...], preferred_element_type=F32)
        sga = _sigmoid(ga_ref[...])
        sgb = _sigmoid(gb_ref[...])
        merged = (sga * pa + sgb * pb).astype(BF16)
        x2 = x_ref[...] + jnp.dot(merged, wout_ref[...], preferred_element_type=F32)
        r2 = lax.rsqrt(jnp.mean(x2 * x2, axis=-1, keepdims=True) + EPS)
        n2 = x2 * r2
        gf = gf_ref[...]
        err = n2 * gf - tgt_ref[...]
        loss_ref[...] += 0.5 * jnp.sum(jnp.sum(err * err, axis=-1, keepdims=True), axis=0, keepdims=True) / d
        dy = err * (1.0 / d)
        gnf_ref[...] += jnp.sum(dy * n2, axis=0, keepdims=True)
        dn = dy * gf
        dx2 = r2 * (dn - n2 * jnp.mean(dn * n2, axis=-1, keepdims=True))
        dx2_ref[...] = dx2
        dx2_b = dx2.astype(BF16)
        dmerged = lax.dot_general(dx2_b, wout_ref[...], NT, preferred_element_type=F32)
        gwout_ref[...] += lax.dot_general(merged, dx2_b, TN, preferred_element_type=F32)
        dg_ref[:, 0:d] = (dmerged * pa * (sga * (1.0 - sga))).astype(BF16)
        dg_ref[:, d:2 * d] = (dmerged * pb * (sgb * (1.0 - sgb))).astype(BF16)
        dpa = (dmerged * sga).astype(BF16)
        dpb = (dmerged * sgb).astype(BF16)
        dya_ref[...] = lax.dot_general(dpa, wog_ref[...], NT, preferred_element_type=F32)
        dyb_ref[...] = lax.dot_general(dpb, wosb_ref[...], NT, preferred_element_type=F32)
        gwog_ref[...] += lax.dot_general(ya, dpa, TN, preferred_element_type=F32)
        gwosb_ref[...] += lax.dot_general(yb, dpb, TN, preferred_element_type=F32)

    row = lambda i: (i, 0)
    return pl.pallas_call(
        body, name="out_proj",
        grid=(t // TM,),
        in_specs=[pl.BlockSpec((TM, d), row), pl.BlockSpec((TM, d), row),
                  pl.BlockSpec((TM, d), lambda i: (i, 1)), pl.BlockSpec((TM, d), lambda i: (i, 2)),
                  pl.BlockSpec((TM, d), row), pl.BlockSpec((TM, d), row),
                  _resident((d, d)), _resident((d, d)), _resident((d, d)), _const((1, d))],
        out_specs=[pl.BlockSpec((TM, d), row), pl.BlockSpec((TM, d), row), pl.BlockSpec((TM, 2 * d), row),
                   pl.BlockSpec((TM, d), row), _const((1, 1)), _const((1, d)),
                   _const((d, d)), _const((d, d)), _const((d, d))],
        out_shape=[jax.ShapeDtypeStruct((t, d), F32), jax.ShapeDtypeStruct((t, d), F32),
                   jax.ShapeDtypeStruct((t, 2 * d), BF16), jax.ShapeDtypeStruct((t, d), F32),
                   jax.ShapeDtypeStruct((1, 1), F32), jax.ShapeDtypeStruct((1, d), F32),
                   jax.ShapeDtypeStruct((d, d), F32), jax.ShapeDtypeStruct((d, d), F32),
                   jax.ShapeDtypeStruct((d, d), F32)],
        compiler_params=_params(("arbitrary",)),
    )(ya, yb, g_pre, g_pre, x2d, tgt, w_og, w_osb, w_out, g_f)


def _dproj_pieces(d):
    return [(0, 0, 3), (1, 3, 1), (2, 4, 1), (3, 5, 1), (4, 6, 1), (5, 7, 2)]


def _in_proj_bwd_x(pieces, wg, x2d, g_in, dx2):
    t, d = x2d.shape
    ncol = wg.shape[2]
    segs = _segments(d, ncol)
    layout = _dproj_pieces(d)

    def body(da_ref, dq_ref, dk_ref, dv_ref, dzb_ref, dg_ref, w_ref, x_ref, g_ref, dx2_ref, gx_ref, gn_ref):
        @pl.when(pl.program_id(0) == 0)
        def _():
            gn_ref[...] = jnp.zeros_like(gn_ref)

        refs = (da_ref, dq_ref, dk_ref, dv_ref, dzb_ref, dg_ref)
        dh = jnp.zeros((TM, d), F32)
        for chip, c0, grp, s0, width in segs:
            piece, first, _ = next(p for p in layout if p[1] <= grp < p[1] + p[2])
            off = (grp - first) * d + s0
            dh = dh + lax.dot_general(refs[piece][:, off:off + width], w_ref[chip, :, c0:c0 + width], NT,
                                      preferred_element_type=F32)
        x = x_ref[...]
        r = lax.rsqrt(jnp.mean(x * x, axis=-1, keepdims=True) + EPS)
        n = x * r
        gn_ref[...] += jnp.sum(dh * n, axis=0, keepdims=True)
        dn = dh * g_ref[...]
        gx_ref[...] = dx2_ref[...] + r * (dn - n * jnp.mean(dn * n, axis=-1, keepdims=True))

    row = lambda i: (i, 0)
    return pl.pallas_call(
        body, name="in_proj_bwd_x",
        grid=(t // TM,),
        in_specs=[pl.BlockSpec((TM, p.shape[1]), row) for p in pieces]
        + [_resident(wg.shape), pl.BlockSpec((TM, d), row), _const((1, d)), pl.BlockSpec((TM, d), row)],
        out_specs=[pl.BlockSpec((TM, d), row), _const((1, d))],
        out_shape=[jax.ShapeDtypeStruct((t, d), F32), jax.ShapeDtypeStruct((1, d), F32)],
        compiler_params=_params(("arbitrary",)),
    )(*pieces, wg, x2d, g_in, dx2)


def _in_proj_bwd_w(h_t, pieces):
    d, t = h_t.shape
    nk = t // TKW
    layout = _dproj_pieces(d)

    def body(ht_ref, da_ref, dq_ref, dk_ref, dv_ref, dzb_ref, dg_ref, gw_ref, acc):
        s = pl.program_id(0)
        i = pl.program_id(1)

        @pl.when(i == 0)
        def _():
            acc[...] = jnp.zeros_like(acc)

        refs = (da_ref, dq_ref, dk_ref, dv_ref, dzb_ref, dg_ref)
        for piece, first, count in layout:
            for j in range(count):
                @pl.when(s == first + j)
                def _(piece=piece, j=j):
                    acc[...] += jnp.dot(ht_ref[...], refs[piece][:, j * d:(j + 1) * d], preferred_element_type=F32)

        @pl.when(i == nk - 1)
        def _():
            gw_ref[...] = acc[...]

    def piece_spec(p, first, count):
        width = count * d
        return pl.BlockSpec((TKW, width), lambda s, i: (jnp.where((s >= first) & (s < first + count), i, 0), 0))

    return pl.pallas_call(
        body, name="in_proj_bwd_w",
        grid=(N_SPLIT, nk),
        in_specs=[pl.BlockSpec((d, TKW), lambda s, i: (0, i))] + [piece_spec(*p) for p in layout],
        out_specs=pl.BlockSpec((d, d), lambda s, i: (0, s)),
        out_shape=jax.ShapeDtypeStruct((d, N_SPLIT * d), F32),
        scratch_shapes=[pltpu.VMEM((d, d), F32)],
        compiler_params=_params(("arbitrary", "arbitrary")),
    )(h_t, *pieces)


def _local_step(x, tgt, norm_in, wg, norm_v, w_s, b_s, w_og, w_osb, w_out, norm_final):
    bsz, seq, d = x.shape
    t = bsz * seq
    x2d = x.reshape(t, d)
    tgt2d = tgt.reshape(t, d)
    chunk = w_s.shape[-1]
    causal = jnp.tril(jnp.ones((chunk, chunk), dtype=bool))
    wm = jnp.where(causal[None], w_s, 0.0).astype(BF16)
    wm_t = jnp.swapaxes(wm, 1, 2)
    b_t = b_s.T

    a_pre, qkv, g_pre, h_t = _in_proj_fwd(x2d, norm_in, wg)
    ya = _branch_a_fwd(a_pre, norm_v, wm, b_t)
    o, yb = _attn_fwd(qkv, g_pre, bsz, seq)
    dya, dyb, dg, dx2, loss, g_nf, g_wog, g_wosb, g_wout = _out_proj(
        ya, yb, g_pre, x2d, tgt2d, w_og, w_osb, w_out, norm_final.reshape(1, d))
    dq, dk, dv, dzb = _attn_bwd(qkv, g_pre, o, dyb, bsz, seq)
    d_a, g_ws, g_bt, g_nv = _branch_a_bwd(a_pre, dya, norm_v, wm, wm_t, b_t)
    pieces = (d_a, dq, dk, dv, dzb, dg)
    gx, g_nin = _in_proj_bwd_x(pieces, wg, x2d, norm_in, dx2)
    g_win = _in_proj_bwd_w(h_t, pieces)
    g_ws = jnp.where(causal[None], g_ws, 0.0)
    return (loss, gx.reshape(bsz, seq, d), g_nin, g_win, g_nv, g_ws, g_bt.T, g_wog, g_wosb, g_wout, g_nf)


def _row_tile(rows):
    return next(r for r in (128, 64, 32, 16, 8) if rows % r == 0)


def _cast_bf16(a):
    rows, cols = a.shape
    tr = _row_tile(rows)

    def body(a_ref, o_ref):
        o_ref[...] = a_ref[...].astype(BF16)

    return pl.pallas_call(
        body, name="cast_bf16", grid=(rows // tr,),
        in_specs=[pl.BlockSpec((tr, cols), lambda i: (i, 0))],
        out_specs=pl.BlockSpec((tr, cols), lambda i: (i, 0)),
        out_shape=jax.ShapeDtypeStruct(a.shape, BF16),
        compiler_params=_params(("arbitrary",)),
    )(a)


def _cast_chunks_bf16(g):
    rows, cols4 = g.shape
    cols = cols4 // N_CHIP
    tr = _row_tile(rows)

    def body(g_ref, o_ref):
        o_ref[0] = g_ref[...].astype(BF16)

    return pl.pallas_call(
        body, name="cast_chunks_bf16", grid=(N_CHIP, rows // tr),
        in_specs=[pl.BlockSpec((tr, cols), lambda j, i: (i, j))],
        out_specs=pl.BlockSpec((1, tr, cols), lambda j, i: (j, i, 0)),
        out_shape=jax.ShapeDtypeStruct((N_CHIP, rows, cols), BF16),
        compiler_params=_params(("arbitrary", "arbitrary")),
    )(g)


def _add_received(own, recv):
    rows, cols = own.shape
    tr = _row_tile(rows)

    def body(own_ref, recv_ref, o_ref):
        s = own_ref[...]
        for k in range(N_CHIP - 1):
            s = s + recv_ref[k].astype(F32)
        o_ref[...] = s

    return pl.pallas_call(
        body, name="add_received", grid=(rows // tr,),
        in_specs=[pl.BlockSpec((tr, cols), lambda i: (i, 0)), pl.BlockSpec((N_CHIP - 1, tr, cols), lambda i: (0, i, 0))],
        out_specs=pl.BlockSpec((tr, cols), lambda i: (i, 0)),
        out_shape=jax.ShapeDtypeStruct(own.shape, F32),
        compiler_params=_params(("arbitrary",)),
    )(own, recv)


def _adamw(w, m, v, g_parts):
    rows, cols = w.shape
    tr = _row_tile(rows)
    n_parts = len(g_parts)
    c1 = 1.0 - ADAM_B1 ** ADAM_STEP
    c2 = 1.0 - ADAM_B2 ** ADAM_STEP

    def body(*refs):
        w_ref, m_ref, v_ref = refs[:3]
        part_refs = refs[3:3 + n_parts]
        g_ref, d_ref, nm_ref, nv_ref = refs[3 + n_parts:]
        g = part_refs[0][...]
        for p in part_refs[1:]:
            g = g + p[...]
        new_m = ADAM_B1 * m_ref[...] + (1.0 - ADAM_B1) * g
        new_v = ADAM_B2 * v_ref[...] + (1.0 - ADAM_B2) * (g * g)
        m_hat = new_m / c1
        v_hat = new_v / c2
        g_ref[...] = g
        d_ref[...] = -ADAM_LR * (m_hat / (jnp.sqrt(v_hat) + ADAM_EPS) + ADAM_WD * w_ref[...])
        nm_ref[...] = new_m
        nv_ref[...] = new_v

    spec = pl.BlockSpec((tr, cols), lambda i: (i, 0))
    out = jax.ShapeDtypeStruct(w.shape, F32)
    return pl.pallas_call(
        body, name="adamw", grid=(rows // tr,),
        in_specs=[spec] * (3 + n_parts), out_specs=[spec] * 4, out_shape=[out] * 4,
        compiler_params=_params(("arbitrary",)),
    )(w, m, v, *g_parts)


ANY = pl.BlockSpec(memory_space=pl.ANY)


def _place():
    x, y, c = lax.axis_index("x"), lax.axis_index("y"), lax.axis_index("c")
    other_chips = [(1 - x, y), (x, 1 - y), (1 - x, 1 - y)]
    return x, y, c, other_chips


def _all_gather_halves(halves):
    n = len(halves)

    def body(*refs):
        ins, outs = refs[:n], refs[n:2 * n]
        send_sems, recv_sems, local_sems = refs[2 * n:]
        x, y, c, chips = _place()
        sibling = (x, y, 1 - c)
        my_chip = 2 * x + y

        def copy(a, k, block, to, src=None):
            return pltpu.make_async_remote_copy(
                src_ref=outs[a].at[block] if src is None else src, dst_ref=outs[a].at[block],
                send_sem=send_sems.at[a, k], recv_sem=recv_sems.at[a, k], device_id=to, device_id_type=MESH)

        local = [pltpu.make_async_copy(ins[a], outs[a].at[pl.ds(2 * my_chip, 2)], local_sems.at[a]) for a in range(n)]
        for cp in local:
            cp.start()
        started = []
        for k, (px, py) in enumerate(chips):
            for a in range(n):
                cp = copy(a, k, 2 * my_chip + c, (px, py, c), src=ins[a].at[c])
                cp.start()
                started.append(cp)
        for k, (px, py) in enumerate(chips):
            for a in range(n):
                block = 2 * (2 * px + py) + c
                copy(a, k, block, (x, y, c)).wait_recv()
                cp = copy(a, 3 + k, block, sibling)
                cp.start()
                started.append(cp)
        for k, (px, py) in enumerate(chips):
            for a in range(n):
                copy(a, 3 + k, 2 * (2 * px + py) + 1 - c, (x, y, c)).wait_recv()
        for cp in started:
            cp.wait_send()
        for cp in local:
            cp.wait()

    return pl.pallas_call(
        body, name="all_gather_weights",
        in_specs=[ANY] * n, out_specs=[ANY] * n,
        out_shape=[jax.ShapeDtypeStruct((2 * N_CHIP,) + h.shape[1:], h.dtype) for h in halves],
        scratch_shapes=[pltpu.SemaphoreType.DMA((n, 6)), pltpu.SemaphoreType.DMA((n, 6)), pltpu.SemaphoreType.DMA((n,))],
    )(*halves)


def _scatter_chunks(chunks):
    n = len(chunks)

    def body(*refs):
        ins, outs = refs[:n], refs[n:2 * n]
        send_sems, recv_sems = refs[2 * n:]
        x, y, c, chips = _place()
        copies = []
        for k, (px, py) in enumerate(chips):
            for a in range(n):
                cp = pltpu.make_async_remote_copy(
                    src_ref=ins[a].at[2 * px + py], dst_ref=outs[a].at[k],
                    send_sem=send_sems.at[a, k], recv_sem=recv_sems.at[a, k], device_id=(px, py, c), device_id_type=MESH)
                cp.start()
                copies.append(cp)
        for cp in copies:
            cp.wait()

    return pl.pallas_call(
        body, name="scatter_grad_chunks",
        in_specs=[ANY] * n, out_specs=[ANY] * n,
        out_shape=[jax.ShapeDtypeStruct((N_CHIP - 1,) + ch.shape[1:], ch.dtype) for ch in chunks],
        scratch_shapes=[pltpu.SemaphoreType.DMA((n, 3)), pltpu.SemaphoreType.DMA((n, 3))],
    )(*chunks)


def _swap_with_sibling(arrs):
    n = len(arrs)

    def body(*refs):
        ins, outs = refs[:n], refs[n:2 * n]
        send_sems, recv_sems = refs[2 * n:]
        x, y, c, _ = _place()
        copies = [pltpu.make_async_remote_copy(
            src_ref=ins[a], dst_ref=outs[a], send_sem=send_sems.at[a], recv_sem=recv_sems.at[a],
            device_id=(x, y, 1 - c), device_id_type=MESH) for a in range(n)]
        for cp in copies:
            cp.start()
        for cp in copies:
            cp.wait()

    return pl.pallas_call(
        body, name="swap_with_sibling",
        in_specs=[ANY] * n, out_specs=[ANY] * n,
        out_shape=[jax.ShapeDtypeStruct(a.shape, a.dtype) for a in arrs],
        scratch_shapes=[pltpu.SemaphoreType.DMA((n,)), pltpu.SemaphoreType.DMA((n,))],
    )(*arrs)


def _all_reduce_small(pack):
    rows, cols = pack.shape
    n_dev = 2 * N_CHIP
    flips = [(dx, dy, dc) for dx in (0, 1) for dy in (0, 1) for dc in (0, 1)][1:]

    def body(pack_ref, out_ref, slots, send_sems, recv_sems):
        x, y, c, _ = _place()
        me = 4 * x + 2 * y + c
        copies = []
        for k, (dx, dy, dc) in enumerate(flips):
            peer = (1 - x if dx else x, 1 - y if dy else y, 1 - c if dc else c)
            cp = pltpu.make_async_remote_copy(
                src_ref=pack_ref, dst_ref=slots.at[me], send_sem=send_sems.at[k], recv_sem=recv_sems.at[k],
                device_id=peer, device_id_type=MESH)
            cp.start()
            copies.append(cp)
        slots[me] = pack_ref[...]
        for cp in copies:
            cp.wait_recv()
        total = slots[0]
        for i in range(1, n_dev):
            total = total + slots[i]
        out_ref[...] = total
        for cp in copies:
            cp.wait_send()

    vmem = pl.BlockSpec(memory_space=pltpu.VMEM)
    return pl.pallas_call(
        body, name="all_reduce_small",
        in_specs=[vmem], out_specs=vmem,
        out_shape=jax.ShapeDtypeStruct(pack.shape, F32),
        scratch_shapes=[pltpu.VMEM((n_dev, rows, cols), F32), pltpu.SemaphoreType.DMA((n_dev - 1,)),
                        pltpu.SemaphoreType.DMA((n_dev - 1,))],
        compiler_params=pltpu.CompilerParams(vmem_limit_bytes=VMEM_LIMIT),
    )(pack)


def _pack_small(vec_nin, vec_nv, b_s, vec_nf, w_s):
    d = vec_nin.shape[-1]
    slabs = [jnp.pad(v.reshape(1, d), ((0, 7), (0, 0))) for v in (vec_nin, vec_nv, b_s, vec_nf)]
    return jnp.concatenate(slabs + [w_s.reshape(-1, d)], axis=0)


def _unpack_small(pack, w_s_shape, b_s_shape):
    d = pack.shape[-1]
    return (pack[0:1], pack[8:9], pack[16].reshape(b_s_shape), pack[24], pack[32:].reshape(w_s_shape))


def kernel(x, norm_in, w_in, norm_v, w_s, b_s, w_o_gmlp, w_o_sb, w_out, norm_final, loss_target, m_norm_in, m_w_in, m_norm_v, m_w_s, m_b_s, m_w_o_gmlp, m_w_o_sb, m_w_out, m_norm_final, v_norm_in, v_w_in, v_norm_v, v_w_s, v_b_s, v_w_o_gmlp, v_w_o_sb, v_w_out, v_norm_final):
    d = x.shape[-1]
    ncol = w_in.shape[-1]
    nrow = w_o_gmlp.shape[-2]
    chip = 2 * lax.axis_index("x") + lax.axis_index("y")

    shards = [w_in[0], w_o_gmlp[0], w_o_sb[0], w_out[0]]
    halves = [_cast_bf16(s).reshape(2, s.shape[0] // 2, s.shape[1]) for s in shards]
    wg, w_og, w_osb, w_o = _all_gather_halves(halves)
    wg = wg.reshape(N_CHIP, d, ncol)

    (loss, grad_x, g_nin, g_win, g_nv, g_ws, g_bs, g_wog, g_wosb, g_wout, g_nf) = _local_step(
        x, loss_target, norm_in, wg, norm_v, w_s[0], b_s[0],
        w_og.reshape(d, d), w_osb.reshape(d, d), w_o.reshape(d, d), norm_final)

    mats = [g_wog, g_wosb, g_wout]
    chunks = [_cast_chunks_bf16(g_win)] + [_cast_bf16(g).reshape(N_CHIP, nrow, d) for g in mats]
    received = _scatter_chunks(chunks)
    own = [lax.dynamic_slice(g_win, (0, chip * ncol), (d, ncol))] + [
        lax.dynamic_slice(g, (chip * nrow, 0), (nrow, d)) for g in mats]
    sums = [_add_received(o, r) for o, r in zip(own, received)]
    sibling_sums = _swap_with_sibling(sums)
    stats = []
    for w, m, v, mine, theirs in zip(shards, [m_w_in[0], m_w_o_gmlp[0], m_w_o_sb[0], m_w_out[0]],
                                     [v_w_in[0], v_w_o_gmlp[0], v_w_o_sb[0], v_w_out[0]], sums, sibling_sums):
        stats.append(_adamw(w, m, v, [mine, theirs]))
    (gw_in, dw_in, nm_in, nv_in), (gw_og, dw_og, nm_og, nv_og), (gw_osb, dw_osb, nm_osb, nv_osb), \
        (gw_out, dw_out, nm_out, nv_out) = stats

    g_small = _all_reduce_small(_pack_small(g_nin, g_nv, g_bs, g_nf, g_ws))
    gs, ds, ms, vs = _adamw(
        _pack_small(norm_in, norm_v, b_s[0], norm_final, w_s[0]),
        _pack_small(m_norm_in, m_norm_v, m_b_s[0], m_norm_final, m_w_s[0]),
        _pack_small(v_norm_in, v_norm_v, v_b_s[0], v_norm_final, v_w_s[0]), [g_small])

    def small(pack):
        nin, nv, bs, nf, ws = _unpack_small(pack, w_s.shape, b_s.shape)
        return nin, nv, ws, bs, nf

    loss = lax.psum(loss[0, 0], ("x", "y", "c"))
    out = []
    for small_pack, win, wog, wosb, wout in ((gs, gw_in, gw_og, gw_osb, gw_out), (ds, dw_in, dw_og, dw_osb, dw_out),
                                             (ms, nm_in, nm_og, nm_osb, nm_out), (vs, nv_in, nv_og, nv_osb, nv_out)):
        nin, nv, ws, bs, nf = small(small_pack)
        out += [nin, win[None], nv, ws, bs, wog[None], wosb[None], wout[None], nf]
    return (loss, grad_x, *out)
```

```python
import functools
import math

import jax
import jax.numpy as jnp
from jax import lax
from jax.experimental import pallas as pl
from jax.experimental.pallas import tpu as pltpu

F32 = jnp.float32
BF16 = jnp.bfloat16
EPS = 1e-6
HEADS = 8
N_SPLIT = 9
N_CHIP = 4
MESH = pl.DeviceIdType.MESH

ADAM_LR = 0.001
ADAM_B1 = 0.9
ADAM_B2 = 0.999
ADAM_EPS = 1e-08
ADAM_WD = 0.01
ADAM_STEP = 10

VMEM_LIMIT = 56 * 2 ** 20
TM = 256
ATT_T = 256
ATT_HP = 2
TKW = 512

NT = (((1,), (1,)), ((), ()))
TN = (((0,), (0,)), ((), ()))


def _params(sem):
    return pltpu.CompilerParams(dimension_semantics=sem, vmem_limit_bytes=VMEM_LIMIT)


def _resident(shape):
    nd = len(shape)
    return pl.BlockSpec(shape, lambda *_: (0,) * nd, pipeline_mode=pl.Buffered(1))


def _const(shape):
    nd = len(shape)
    return pl.BlockSpec(shape, lambda *_: (0,) * nd)


def _segments(d, ncol):
    segs = []
    edges = sorted({j * ncol for j in range(N_CHIP + 1)} | {s * d for s in range(N_SPLIT + 1)})
    for lo, hi in zip(edges[:-1], edges[1:]):
        segs.append((lo // ncol, lo % ncol, lo // d, lo % d, hi - lo))
    return segs


def _sigmoid(x):
    return 1.0 / (1.0 + jnp.exp(-x))


_GELU_C = math.sqrt(2.0 / math.pi)


def _gelu(x):
    return 0.5 * x * (1.0 + jnp.tanh(_GELU_C * (x + 0.044715 * (x * x * x))))


def _gelu_and_grad(x):
    x2 = x * x
    th = jnp.tanh(_GELU_C * (x + 0.044715 * (x2 * x)))
    val = 0.5 * x * (1.0 + th)
    grad = 0.5 * (1.0 + th) + 0.5 * x * (1.0 - th * th) * (_GELU_C * (1.0 + 3.0 * 0.044715 * x2))
    return val, grad


def _split_bf16(a):
    hi = a.astype(BF16)
    lo = (a - hi.astype(F32)).astype(BF16)
    return hi, lo


def _in_proj_fwd(x2d, g_in, wg):
    t, d = x2d.shape
    ncol = wg.shape[2]
    segs = _segments(d, ncol)

    def body(x_ref, g_ref, w_ref, apre_ref, qkv_ref, gpre_ref, ht_ref):
        x = x_ref[...]
        r = lax.rsqrt(jnp.mean(x * x, axis=-1, keepdims=True) + EPS)
        hf = x * r * g_ref[...]
        h = hf.astype(BF16)
        ht_ref[...] = hf.T.astype(BF16)
        outs = (apre_ref, qkv_ref, gpre_ref)
        for chip, c0, grp, s0, width in segs:
            res = jnp.dot(h, w_ref[chip, :, c0:c0 + width], preferred_element_type=F32)
            o = outs[grp // 3]
            off = (grp % 3) * d + s0
            o[:, off:off + width] = res.astype(o.dtype)

    return pl.pallas_call(
        body, name="in_proj_fwd",
        grid=(t // TM,),
        in_specs=[pl.BlockSpec((TM, d), lambda i: (i, 0)), _const((1, d)), _resident(wg.shape)],
        out_specs=[pl.BlockSpec((TM, 3 * d), lambda i: (i, 0)), pl.BlockSpec((TM, 3 * d), lambda i: (i, 0)),
                   pl.BlockSpec((TM, 3 * d), lambda i: (i, 0)), pl.BlockSpec((d, TM), lambda i: (0, i))],
        out_shape=[jax.ShapeDtypeStruct((t, 3 * d), F32), jax.ShapeDtypeStruct((t, 3 * d), BF16),
                   jax.ShapeDtypeStruct((t, 3 * d), F32), jax.ShapeDtypeStruct((d, t), BF16)],
        compiler_params=_params(("arbitrary",)),
    )(x2d, g_in, wg)


def _branch_a_fwd(a_pre, g_v, wm, b_t):
    t, d3 = a_pre.shape
    d = d3 // 3
    ng, chunk, _ = wm.shape
    cw = d // ng

    def body(a_ref, gv_ref, wm_ref, bt_ref, ya_ref):
        ua = _gelu(a_ref[:, 0:d])
        vg = _gelu(a_ref[:, d:2 * d])
        za = a_ref[:, 2 * d:3 * d]
        rv = lax.rsqrt(jnp.mean(vg * vg, axis=-1, keepdims=True) + EPS)
        va = (vg * rv * gv_ref[...]).astype(BF16)
        gate = ua * (za * _sigmoid(za))
        for g in range(ng):
            sl = slice(g * cw, (g + 1) * cw)
            mixed = jnp.dot(wm_ref[g], va[:, sl], preferred_element_type=F32) + bt_ref[:, g:g + 1]
            ya_ref[:, sl] = (gate[:, sl] * mixed).astype(BF16)

    return pl.pallas_call(
        body, name="branch_a_fwd",
        grid=(t // chunk,),
        in_specs=[pl.BlockSpec((chunk, d3), lambda i: (i, 0)), _const((1, d)), _const(wm.shape), _const(b_t.shape)],
        out_specs=pl.BlockSpec((chunk, d), lambda i: (i, 0)),
        out_shape=jax.ShapeDtypeStruct((t, d), BF16),
        compiler_params=_params(("arbitrary",)),
    )(a_pre, g_v, wm, b_t)


def _branch_a_bwd(a_pre, dya, g_v, wm, wm_t, b_t):
    t, d3 = a_pre.shape
    d = d3 // 3
    ng, chunk, _ = wm.shape
    cw = d // ng
    nsteps = t // chunk

    def body(a_ref, dya_ref, gv_ref, wm_ref, wmt_ref, bt_ref, da_ref, gws_ref, gbt_ref, gnv_ref, db_acc):
        i = pl.program_id(0)

        @pl.when(i == 0)
        def _():
            gws_ref[...] = jnp.zeros_like(gws_ref)
            gnv_ref[...] = jnp.zeros_like(gnv_ref)
            db_acc[...] = jnp.zeros_like(db_acc)

        ua, dgelu_u = _gelu_and_grad(a_ref[:, 0:d])
        vg, dgelu_v = _gelu_and_grad(a_ref[:, d:2 * d])
        za = a_ref[:, 2 * d:3 * d]
        sig = _sigmoid(za)
        sz = za * sig
        dsz = sig * (1.0 + za * (1.0 - sig))
        rv = lax.rsqrt(jnp.mean(vg * vg, axis=-1, keepdims=True) + EPS)
        nv = vg * rv
        gv = gv_ref[...]
        va = (nv * gv).astype(BF16)
        dya = dya_ref[...]
        dmix = dya * ua * sz
        db_acc[...] += dmix
        dmix_b = dmix.astype(BF16)
        t_gate = dya * sz
        t_z = dya * ua * dsz
        dva_parts = []
        for g in range(ng):
            sl = slice(g * cw, (g + 1) * cw)
            mixed = jnp.dot(wm_ref[g], va[:, sl], preferred_element_type=F32) + bt_ref[:, g:g + 1]
            da_ref[:, sl] = (t_gate[:, sl] * mixed * dgelu_u[:, sl]).astype(BF16)
            da_ref[:, 2 * d + g * cw:2 * d + (g + 1) * cw] = (t_z[:, sl] * mixed).astype(BF16)
            gws_ref[g] += lax.dot_general(dmix_b[:, sl], va[:, sl], NT, preferred_element_type=F32)
            dva_parts.append(jnp.dot(wmt_ref[g], dmix_b[:, sl], preferred_element_type=F32))
        dva = jnp.concatenate(dva_parts, axis=1)
        gnv_ref[...] += jnp.sum(dva * nv, axis=0, keepdims=True)
        dnv = dva * gv
        dvg = rv * (dnv - nv * jnp.mean(dnv * nv, axis=-1, keepdims=True))
        da_ref[:, d:2 * d] = (dvg * dgelu_v).astype(BF16)

        @pl.when(i == nsteps - 1)
        def _():
            acc = db_acc[...]
            for g in range(ng):
                gbt_ref[:, g:g + 1] = jnp.sum(acc[:, g * cw:(g + 1) * cw], axis=1, keepdims=True)

    return pl.pallas_call(
        body, name="branch_a_bwd",
        grid=(nsteps,),
        in_specs=[pl.BlockSpec((chunk, d3), lambda i: (i, 0)), pl.BlockSpec((chunk, d), lambda i: (i, 0)),
                  _const((1, d)), _const(wm.shape), _const(wm_t.shape), _const(b_t.shape)],
        out_specs=[pl.BlockSpec((chunk, d3), lambda i: (i, 0)), _const(wm.shape), _const(b_t.shape), _const((1, d))],
        out_shape=[jax.ShapeDtypeStruct((t, d3), BF16), jax.ShapeDtypeStruct(wm.shape, F32),
                   jax.ShapeDtypeStruct(b_t.shape, F32), jax.ShapeDtypeStruct((1, d), F32)],
        scratch_shapes=[pltpu.VMEM((chunk, d), F32)],
        compiler_params=_params(("arbitrary",)),
    )(a_pre, dya, g_v, wm, wm_t, b_t)


def _tri(n, rows_gt_cols):
    r = lax.broadcasted_iota(jnp.int32, (n, n), 0)
    c = lax.broadcasted_iota(jnp.int32, (n, n), 1)
    return (r > c) if rows_gt_cols else (r < c)


def _twice(tri):
    t = tri.astype(BF16)
    return jnp.concatenate([t, t], axis=0)


def _cumsum_mm(a, tri2):
    hi, lo = _split_bf16(a)
    return jnp.dot(jnp.concatenate([hi, lo], axis=1), tri2, preferred_element_type=F32)


def _sb_block(q, k, scale, upper2, causal):
    z = lax.dot_general(q, k, NT, preferred_element_type=F32) * scale
    sp = jnp.maximum(z, 0.0) + jnp.log(1.0 + jnp.exp(-jnp.abs(z)))
    lom = -sp
    if causal is not None:
        lom = jnp.where(causal, lom, 0.0)
    sx = _cumsum_mm(lom, upper2)
    return z - sp, sx, sx[:, 0:1] + lom[:, 0:1]


def _attn_specs(d, seq, nq):
    hp_w = ATT_HP * (d // HEADS)
    n_hp = d // hp_w
    row_blk = pl.BlockSpec((ATT_T, hp_w), lambda b, h, i: (b * nq + i, h))
    k_blk = pl.BlockSpec((seq, hp_w), lambda b, h, i: (b, n_hp + h))
    v_blk = pl.BlockSpec((seq, hp_w), lambda b, h, i: (b, 2 * n_hp + h))
    seq_blk = pl.BlockSpec((seq, hp_w), lambda b, h, i: (b, h))
    return row_blk, k_blk, v_blk, seq_blk, n_hp


def _attn_fwd(qkv, g_pre, bsz, seq):
    t, d3 = qkv.shape
    d = d3 // 3
    hd = d // HEADS
    nq = seq // ATT_T
    scale = hd ** -0.5
    row_blk, k_blk, v_blk, _, n_hp = _attn_specs(d, seq, nq)

    def body(q_ref, k_ref, v_ref, zb_ref, o_ref, yb_ref):
        i = pl.program_id(2)
        causal = _tri(ATT_T, True)
        upper2 = _twice(causal)

        def step(kb, state, mask):
            rows = pl.ds(pl.multiple_of(kb * ATT_T, ATT_T), ATT_T)
            new = []
            for h in range(ATT_HP):
                cols = slice(h * hd, (h + 1) * hd)
                carry, acc = state[h]
                log_beta, sx, total = _sb_block(q_ref[:, cols], k_ref[rows, cols], scale, upper2, mask)
                a = jnp.exp(log_beta + sx + carry)
                if mask is not None:
                    a = jnp.where(mask, a, 0.0)
                acc = acc + jnp.dot(a.astype(BF16), v_ref[rows, cols], preferred_element_type=F32)
                new.append((carry + total, acc))
            return tuple(new)

        init = tuple((jnp.zeros((ATT_T, 1), F32), jnp.zeros((ATT_T, hd), F32)) for _ in range(ATT_HP))
        state = step(i, init, causal)
        state = lax.fori_loop(0, i, lambda n, s: step(i - 1 - n, s, None), state)
        for h in range(ATT_HP):
            cols = slice(h * hd, (h + 1) * hd)
            acc = state[h][1]
            zb = zb_ref[:, cols]
            o_ref[:, cols] = acc
            yb_ref[:, cols] = (acc * (zb * _sigmoid(zb))).astype(BF16)

    return pl.pallas_call(
        body, name="attn_fwd",
        grid=(bsz, n_hp, nq),
        in_specs=[row_blk, k_blk, v_blk, row_blk],
        out_specs=[row_blk, row_blk],
        out_shape=[jax.ShapeDtypeStruct((t, d), F32), jax.ShapeDtypeStruct((t, d), BF16)],
        compiler_params=_params(("arbitrary", "arbitrary", "arbitrary")),
    )(qkv, qkv, qkv, g_pre)


def _attn_bwd(qkv, g_pre, o, dyb, bsz, seq):
    t, d3 = qkv.shape
    d = d3 // 3
    hd = d // HEADS
    nq = seq // ATT_T
    scale = hd ** -0.5
    row_blk, k_blk, v_blk, seq_blk, n_hp = _attn_specs(d, seq, nq)

    def body(q_ref, k_ref, v_ref, zb_ref, o_ref, dyb_ref, dq_ref, dk_ref, dv_ref, dzb_ref,
             g_s, beta_s, dkt_acc, dvt_acc):
        i = pl.program_id(2)

        @pl.when(i == 0)
        def _():
            dkt_acc[...] = jnp.zeros_like(dkt_acc)
            dvt_acc[...] = jnp.zeros_like(dvt_acc)

        causal = _tri(ATT_T, True)
        upper2 = _twice(causal)
        lower2 = _twice(_tri(ATT_T, False))
        zb = zb_ref[...]
        sig = _sigmoid(zb)
        dyb_t = dyb_ref[...]
        do_f = dyb_t * (zb * sig)
        do = do_f.astype(BF16)
        do_t = do_f.T.astype(BF16)
        q_t = q_ref[...].astype(F32).T.astype(BF16)
        dzb_ref[...] = (dyb_t * o_ref[...] * (sig * (1.0 + zb * (1.0 - sig)))).astype(BF16)

        def sweep(kb, carries, mask):
            rows = pl.ds(pl.multiple_of(kb * ATT_T, ATT_T), ATT_T)
            new = []
            for h in range(ATT_HP):
                cols = slice(h * hd, (h + 1) * hd)
                log_beta, sx, total = _sb_block(q_ref[:, cols], k_ref[rows, cols], scale, upper2, mask)
                a = jnp.exp(log_beta + sx + carries[h])
                beta = jnp.exp(log_beta)
                if mask is not None:
                    a = jnp.where(mask, a, 0.0)
                    beta = jnp.where(mask, beta, 0.0)
                da = lax.dot_general(do[:, cols], v_ref[rows, cols], NT, preferred_element_type=F32)
                g_s[h, kb] = a * da
                beta_s[h, kb] = beta
                dvt_acc[kb, cols, :] += jnp.dot(do_t[cols, :], a.astype(BF16), preferred_element_type=F32)
                new.append(carries[h] + total)
            return tuple(new)

        carries = sweep(i, tuple(jnp.zeros((ATT_T, 1), F32) for _ in range(ATT_HP)), causal)
        lax.fori_loop(0, i, lambda n, c: sweep(i - 1 - n, c, None), carries)

        def back(kb, state):
            rows = pl.ds(pl.multiple_of(kb * ATT_T, ATT_T), ATT_T)
            new = []
            for h in range(ATT_HP):
                cols = slice(h * hd, (h + 1) * hd)
                p_carry, dq = state[h]
                g = g_s[h, kb]
                beta = beta_s[h, kb]
                px = _cumsum_mm(g, lower2)
                dz = ((g * (1.0 - beta) - (p_carry + px) * beta) * scale).astype(BF16)
                dq = dq + jnp.dot(dz, k_ref[rows, cols], preferred_element_type=F32)
                dkt_acc[kb, cols, :] += jnp.dot(q_t[cols, :], dz, preferred_element_type=F32)
                new.append((p_carry + px[:, ATT_T - 1:ATT_T] + g[:, ATT_T - 1:ATT_T], dq))
            return tuple(new)

        init = tuple((jnp.zeros((ATT_T, 1), F32), jnp.zeros((ATT_T, hd), F32)) for _ in range(ATT_HP))
        state = lax.fori_loop(0, i + 1, back, init)
        for h in range(ATT_HP):
            dq_ref[:, h * hd:(h + 1) * hd] = state[h][1].astype(BF16)

        @pl.when(i == nq - 1)
        def _():
            for kb in range(nq):
                dk_ref[kb * ATT_T:(kb + 1) * ATT_T, :] = dkt_acc[kb].T.astype(BF16)
                dv_ref[kb * ATT_T:(kb + 1) * ATT_T, :] = dvt_acc[kb].T.astype(BF16)

    out = jax.ShapeDtypeStruct((t, d), BF16)
    hp_w = ATT_HP * hd
    return pl.pallas_call(
        body, name="attn_bwd",
        grid=(bsz, n_hp, nq),
        in_specs=[row_blk, k_blk, v_blk, row_blk, row_blk, row_blk],
        out_specs=[row_blk, seq_blk, seq_blk, row_blk],
        out_shape=[out, out, out, out],
        scratch_shapes=[pltpu.VMEM((ATT_HP, nq, ATT_T, ATT_T), F32), pltpu.VMEM((ATT_HP, nq, ATT_T, ATT_T), F32),
                        pltpu.VMEM((nq, hp_w, ATT_T), F32), pltpu.VMEM((nq, hp_w, ATT_T), F32)],
        compiler_params=_params(("arbitrary", "arbitrary", "arbitrary")),
    )(qkv, qkv, qkv, g_pre, o, dyb)


def _out_proj(ya, yb, g_pre, x2d, tgt, w_og, w_osb, w_out, g_f):
    t, d = x2d.shape

    def body(ya_ref, yb_ref, ga_ref, gb_ref, x_ref, tgt_ref, wog_ref, wosb_ref, wout_ref, gf_ref,
             dya_ref, dyb_ref, dg_ref, dx2_ref, loss_ref, gnf_ref, gwog_ref, gwosb_ref, gwout_ref):
        @pl.when(pl.program_id(0) == 0)
        def _():
            loss_ref[...] = jnp.zeros_like(loss_ref)
            gnf_ref[...] = jnp.zeros_like(gnf_ref)
            gwog_ref[...] = jnp.zeros_like(gwog_ref)
            gwosb_ref[...] = jnp.zeros_like(gwosb_ref)
            gwout_ref[...] = jnp.zeros_like(gwout_ref)

        ya = ya_ref[...]
        yb = yb_ref[...]
        pa = jnp.dot(ya, wog_ref[...], preferred_element_type=F32)
        pb = jnp.dot(yb, wosb_ref[...], preferred_element_type=F32)
        sga = _sigmoid(ga_ref[...])
        sgb = _sigmoid(gb_ref[...])
        merged = (sga * pa + sgb * pb).astype(BF16)
        x2 = x_ref[...] + jnp.dot(merged, wout_ref[...], preferred_element_type=F32)
        r2 = lax.rsqrt(jnp.mean(x2 * x2, axis=-1, keepdims=True) + EPS)
        n2 = x2 * r2
        gf = gf_ref[...]
        err = n2 * gf - tgt_ref[...]
        loss_ref[...] += 0.5 * jnp.sum(jnp.sum(err * err, axis=-1, keepdims=True), axis=0, keepdims=True) / d
        dy = err * (1.0 / d)
        gnf_ref[...] += jnp.sum(dy * n2, axis=0, keepdims=True)
        dn = dy * gf
        dx2 = r2 * (dn - n2 * jnp.mean(dn * n2, axis=-1, keepdims=True))
        dx2_ref[...] = dx2
        dx2_b = dx2.astype(BF16)
        dmerged = lax.dot_general(dx2_b, wout_ref[...], NT, preferred_element_type=F32)
        gwout_ref[...] += lax.dot_general(merged, dx2_b, TN, preferred_element_type=F32)
        dg_ref[:, 0:d] = (dmerged * pa * (sga * (1.0 - sga))).astype(BF16)
        dg_ref[:, d:2 * d] = (dmerged * pb * (sgb * (1.0 - sgb))).astype(BF16)
        dpa = (dmerged * sga).astype(BF16)
        dpb = (dmerged * sgb).astype(BF16)
        dya_ref[...] = lax.dot_general(dpa, wog_ref[...], NT, preferred_element_type=F32)
        dyb_ref[...] = lax.dot_general(dpb, wosb_ref[...], NT, preferred_element_type=F32)
        gwog_ref[...] += lax.dot_general(ya, dpa, TN, preferred_element_type=F32)
        gwosb_ref[...] += lax.dot_general(yb, dpb, TN, preferred_element_type=F32)

    row = lambda i: (i, 0)
    return pl.pallas_call(
        body, name="out_proj",
        grid=(t // TM,),
        in_specs=[pl.BlockSpec((TM, d), row), pl.BlockSpec((TM, d), row),
                  pl.BlockSpec((TM, d), lambda i: (i, 1)), pl.BlockSpec((TM, d), lambda i: (i, 2)),
                  pl.BlockSpec((TM, d), row), pl.BlockSpec((TM, d), row),
                  _resident((d, d)), _resident((d, d)), _resident((d, d)), _const((1, d))],
        out_specs=[pl.BlockSpec((TM, d), row), pl.BlockSpec((TM, d), row), pl.BlockSpec((TM, 2 * d), row),
                   pl.BlockSpec((TM, d), row), _const((1, 1)), _const((1, d)),
                   _const((d, d)), _const((d, d)), _const((d, d))],
        out_shape=[jax.ShapeDtypeStruct((t, d), F32), jax.ShapeDtypeStruct((t, d), F32),
                   jax.ShapeDtypeStruct((t, 2 * d), BF16), jax.ShapeDtypeStruct((t, d), F32),
                   jax.ShapeDtypeStruct((1, 1), F32), jax.ShapeDtypeStruct((1, d), F32),
                   jax.ShapeDtypeStruct((d, d), F32), jax.ShapeDtypeStruct((d, d), F32),
                   jax.ShapeDtypeStruct((d, d), F32)],
        compiler_params=_params(("arbitrary",)),
    )(ya, yb, g_pre, g_pre, x2d, tgt, w_og, w_osb, w_out, g_f)


def _dproj_pieces(d):
    return [(0, 0, 3), (1, 3, 1), (2, 4, 1), (3, 5, 1), (4, 6, 1), (5, 7, 2)]


def _in_proj_bwd_x(pieces, wg, x2d, g_in, dx2):
    t, d = x2d.shape
    ncol = wg.shape[2]
    segs = _segments(d, ncol)
    layout = _dproj_pieces(d)

    def body(da_ref, dq_ref, dk_ref, dv_ref, dzb_ref, dg_ref, w_ref, x_ref, g_ref, dx2_ref, gx_ref, gn_ref):
        @pl.when(pl.program_id(0) == 0)
        def _():
            gn_ref[...] = jnp.zeros_like(gn_ref)

        refs = (da_ref, dq_ref, dk_ref, dv_ref, dzb_ref, dg_ref)
        dh = jnp.zeros((TM, d), F32)
        for chip, c0, grp, s0, width in segs:
            piece, first, _ = next(p for p in layout if p[1] <= grp < p[1] + p[2])
            off = (grp - first) * d + s0
            dh = dh + lax.dot_general(refs[piece][:, off:off + width], w_ref[chip, :, c0:c0 + width], NT,
                                      preferred_element_type=F32)
        x = x_ref[...]
        r = lax.rsqrt(jnp.mean(x * x, axis=-1, keepdims=True) + EPS)
        n = x * r
        gn_ref[...] += jnp.sum(dh * n, axis=0, keepdims=True)
        dn = dh * g_ref[...]
        gx_ref[...] = dx2_ref[...] + r * (dn - n * jnp.mean(dn * n, axis=-1, keepdims=True))

    row = lambda i: (i, 0)
    return pl.pallas_call(
        body, name="in_proj_bwd_x",
        grid=(t // TM,),
        in_specs=[pl.BlockSpec((TM, p.shape[1]), row) for p in pieces]
        + [_resident(wg.shape), pl.BlockSpec((TM, d), row), _const((1, d)), pl.BlockSpec((TM, d), row)],
        out_specs=[pl.BlockSpec((TM, d), row), _const((1, d))],
        out_shape=[jax.ShapeDtypeStruct((t, d), F32), jax.ShapeDtypeStruct((1, d), F32)],
        compiler_params=_params(("arbitrary",)),
    )(*pieces, wg, x2d, g_in, dx2)


def _in_proj_bwd_w(h_t, pieces):
    d, t = h_t.shape
    nk = t // TKW
    layout = _dproj_pieces(d)

    def body(ht_ref, da_ref, dq_ref, dk_ref, dv_ref, dzb_ref, dg_ref, gw_ref, acc):
        s = pl.program_id(0)
        i = pl.program_id(1)

        @pl.when(i == 0)
        def _():
            acc[...] = jnp.zeros_like(acc)

        refs = (da_ref, dq_ref, dk_ref, dv_ref, dzb_ref, dg_ref)
        for piece, first, count in layout:
            for j in range(count):
                @pl.when(s == first + j)
                def _(piece=piece, j=j):
                    acc[...] += jnp.dot(ht_ref[...], refs[piece][:, j * d:(j + 1) * d], preferred_element_type=F32)

        @pl.when(i == nk - 1)
        def _():
            gw_ref[...] = acc[...]

    def piece_spec(p, first, count):
        width = count * d
        return pl.BlockSpec((TKW, width), lambda s, i: (jnp.where((s >= first) & (s < first + count), i, 0), 0))

    return pl.pallas_call(
        body, name="in_proj_bwd_w",
        grid=(N_SPLIT, nk),
        in_specs=[pl.BlockSpec((d, TKW), lambda s, i: (0, i))] + [piece_spec(*p) for p in layout],
        out_specs=pl.BlockSpec((d, d), lambda s, i: (0, s)),
        out_shape=jax.ShapeDtypeStruct((d, N_SPLIT * d), F32),
        scratch_shapes=[pltpu.VMEM((d, d), F32)],
        compiler_params=_params(("arbitrary", "arbitrary")),
    )(h_t, *pieces)


def _local_step(x, tgt, norm_in, wg, norm_v, w_s, b_s, w_og, w_osb, w_out, norm_final):
    bsz, seq, d = x.shape
    t = bsz * seq
    x2d = x.reshape(t, d)
    tgt2d = tgt.reshape(t, d)
    chunk = w_s.shape[-1]
    causal = jnp.tril(jnp.ones((chunk, chunk), dtype=bool))
    wm = jnp.where(causal[None], w_s, 0.0).astype(BF16)
    wm_t = jnp.swapaxes(wm, 1, 2)
    b_t = b_s.T

    a_pre, qkv, g_pre, h_t = _in_proj_fwd(x2d, norm_in, wg)
    ya = _branch_a_fwd(a_pre, norm_v, wm, b_t)
    o, yb = _attn_fwd(qkv, g_pre, bsz, seq)
    dya, dyb, dg, dx2, loss, g_nf, g_wog, g_wosb, g_wout = _out_proj(
        ya, yb, g_pre, x2d, tgt2d, w_og, w_osb, w_out, norm_final.reshape(1, d))
    dq, dk, dv, dzb = _attn_bwd(qkv, g_pre, o, dyb, bsz, seq)
    d_a, g_ws, g_bt, g_nv = _branch_a_bwd(a_pre, dya, norm_v, wm, wm_t, b_t)
    pieces = (d_a, dq, dk, dv, dzb, dg)
    gx, g_nin = _in_proj_bwd_x(pieces, wg, x2d, norm_in, dx2)
    g_win = _in_proj_bwd_w(h_t, pieces)
    g_ws = jnp.where(causal[None], g_ws, 0.0)
    return (loss, gx.reshape(bsz, seq, d), g_nin, g_win, g_nv, g_ws, g_bt.T, g_wog, g_wosb, g_wout, g_nf)


def _row_tile(rows):
    return next(r for r in (128, 64, 32, 16, 8) if rows % r == 0)


def _cast_bf16(a):
    rows, cols = a.shape
    tr = _row_tile(rows)

    def body(a_ref, o_ref):
        o_ref[...] = a_ref[...].astype(BF16)

    return pl.pallas_call(
        body, name="cast_bf16", grid=(rows // tr,),
        in_specs=[pl.BlockSpec((tr, cols), lambda i: (i, 0))],
        out_specs=pl.BlockSpec((tr, cols), lambda i: (i, 0)),
        out_shape=jax.ShapeDtypeStruct(a.shape, BF16),
        compiler_params=_params(("arbitrary",)),
    )(a)


def _cast_chunks_bf16(g):
    rows, cols4 = g.shape
    cols = cols4 // N_CHIP
    tr = _row_tile(rows)

    def body(g_ref, o_ref):
        o_ref[0] = g_ref[...].astype(BF16)

    return pl.pallas_call(
        body, name="cast_chunks_bf16", grid=(N_CHIP, rows // tr),
        in_specs=[pl.BlockSpec((tr, cols), lambda j, i: (i, j))],
        out_specs=pl.BlockSpec((1, tr, cols), lambda j, i: (j, i, 0)),
        out_shape=jax.ShapeDtypeStruct((N_CHIP, rows, cols), BF16),
        compiler_params=_params(("arbitrary", "arbitrary")),
    )(g)


def _add_received(own, recv):
    rows, cols = own.shape
    tr = _row_tile(rows)

    def body(own_ref, recv_ref, o_ref):
        s = own_ref[...]
        for k in range(N_CHIP - 1):
            s = s + recv_ref[k].astype(F32)
        o_ref[...] = s

    return pl.pallas_call(
        body, name="add_received", grid=(rows // tr,),
        in_specs=[pl.BlockSpec((tr, cols), lambda i: (i, 0)), pl.BlockSpec((N_CHIP - 1, tr, cols), lambda i: (0, i, 0))],
        out_specs=pl.BlockSpec((tr, cols), lambda i: (i, 0)),
        out_shape=jax.ShapeDtypeStruct(own.shape, F32),
        compiler_params=_params(("arbitrary",)),
    )(own, recv)


def _adamw(w, m, v, g_parts):
    rows, cols = w.shape
    tr = _row_tile(rows)
    n_parts = len(g_parts)
    c1 = 1.0 - ADAM_B1 ** ADAM_STEP
    c2 = 1.0 - ADAM_B2 ** ADAM_STEP

    def body(*refs):
        w_ref, m_ref, v_ref = refs[:3]
        part_refs = refs[3:3 + n_parts]
        g_ref, d_ref, nm_ref, nv_ref = refs[3 + n_parts:]
        g = part_refs[0][...]
        for p in part_refs[1:]:
            g = g + p[...]
        new_m = ADAM_B1 * m_ref[...] + (1.0 - ADAM_B1) * g
        new_v = ADAM_B2 * v_ref[...] + (1.0 - ADAM_B2) * (g * g)
        m_hat = new_m / c1
        v_hat = new_v / c2
        g_ref[...] = g
        d_ref[...] = -ADAM_LR * (m_hat / (jnp.sqrt(v_hat) + ADAM_EPS) + ADAM_WD * w_ref[...])
        nm_ref[...] = new_m
        nv_ref[...] = new_v

    spec = pl.BlockSpec((tr, cols), lambda i: (i, 0))
    out = jax.ShapeDtypeStruct(w.shape, F32)
    return pl.pallas_call(
        body, name="adamw", grid=(rows // tr,),
        in_specs=[spec] * (3 + n_parts), out_specs=[spec] * 4, out_shape=[out] * 4,
        compiler_params=_params(("arbitrary",)),
    )(w, m, v, *g_parts)


ANY = pl.BlockSpec(memory_space=pl.ANY)


def _place():
    x, y, c = lax.axis_index("x"), lax.axis_index("y"), lax.axis_index("c")
    other_chips = [(1 - x, y), (x, 1 - y), (1 - x, 1 - y)]
    return x, y, c, other_chips


def _all_gather_halves(halves):
    n = len(halves)

    def body(*refs):
        ins, outs = refs[:n], refs[n:2 * n]
        send_sems, recv_sems, local_sems = refs[2 * n:]
        x, y, c, chips = _place()
        sibling = (x, y, 1 - c)
        my_chip = 2 * x + y

        def copy(a, k, block, to, src=None):
            return pltpu.make_async_remote_copy(
                src_ref=outs[a].at[block] if src is None else src, dst_ref=outs[a].at[block],
                send_sem=send_sems.at[a, k], recv_sem=recv_sems.at[a, k], device_id=to, device_id_type=MESH)

        local = [pltpu.make_async_copy(ins[a], outs[a].at[pl.ds(2 * my_chip, 2)], local_sems.at[a]) for a in range(n)]
        for cp in local:
            cp.start()
        started = []
        for k, (px, py) in enumerate(chips):
            for a in range(n):
                cp = copy(a, k, 2 * my_chip + c, (px, py, c), src=ins[a].at[c])
                cp.start()
                started.append(cp)
        for k, (px, py) in enumerate(chips):
            for a in range(n):
                block = 2 * (2 * px + py) + c
                copy(a, k, block, (x, y, c)).wait_recv()
                cp = copy(a, 3 + k, block, sibling)
                cp.start()
                started.append(cp)
        for k, (px, py) in enumerate(chips):
            for a in range(n):
                copy(a, 3 + k, 2 * (2 * px + py) + 1 - c, (x, y, c)).wait_recv()
        for cp in started:
            cp.wait_send()
        for cp in local:
            cp.wait()

    return pl.pallas_call(
        body, name="all_gather_weights",
        in_specs=[ANY] * n, out_specs=[ANY] * n,
        out_shape=[jax.ShapeDtypeStruct((2 * N_CHIP,) + h.shape[1:], h.dtype) for h in halves],
        scratch_shapes=[pltpu.SemaphoreType.DMA((n, 6)), pltpu.SemaphoreType.DMA((n, 6)), pltpu.SemaphoreType.DMA((n,))],
    )(*halves)


def _scatter_chunks(chunks):
    n = len(chunks)

    def body(*refs):
        ins, outs = refs[:n], refs[n:2 * n]
        send_sems, recv_sems = refs[2 * n:]
        x, y, c, chips = _place()
        copies = []
        for k, (px, py) in enumerate(chips):
            for a in range(n):
                cp = pltpu.make_async_remote_copy(
                    src_ref=ins[a].at[2 * px + py], dst_ref=outs[a].at[k],
                    send_sem=send_sems.at[a, k], recv_sem=recv_sems.at[a, k], device_id=(px, py, c), device_id_type=MESH)
                cp.start()
                copies.append(cp)
        for cp in copies:
            cp.wait()

    return pl.pallas_call(
        body, name="scatter_grad_chunks",
        in_specs=[ANY] * n, out_specs=[ANY] * n,
        out_shape=[jax.ShapeDtypeStruct((N_CHIP - 1,) + ch.shape[1:], ch.dtype) for ch in chunks],
        scratch_shapes=[pltpu.SemaphoreType.DMA((n, 3)), pltpu.SemaphoreType.DMA((n, 3))],
    )(*chunks)


def _swap_with_sibling(arrs):
    n = len(arrs)

    def body(*refs):
        ins, outs = refs[:n], refs[n:2 * n]
        send_sems, recv_sems = refs[2 * n:]
        x, y, c, _ = _place()
        copies = [pltpu.make_async_remote_copy(
            src_ref=ins[a], dst_ref=outs[a], send_sem=send_sems.at[a], recv_sem=recv_sems.at[a],
            device_id=(x, y, 1 - c), device_id_type=MESH) for a in range(n)]
        for cp in copies:
            cp.start()
        for cp in copies:
            cp.wait()

    return pl.pallas_call(
        body, name="swap_with_sibling",
        in_specs=[ANY] * n, out_specs=[ANY] * n,
        out_shape=[jax.ShapeDtypeStruct(a.shape, a.dtype) for a in arrs],
        scratch_shapes=[pltpu.SemaphoreType.DMA((n,)), pltpu.SemaphoreType.DMA((n,))],
    )(*arrs)


def _all_reduce_small(pack):
    rows, cols = pack.shape
    n_dev = 2 * N_CHIP
    flips = [(dx, dy, dc) for dx in (0, 1) for dy in (0, 1) for dc in (0, 1)][1:]

    def body(pack_ref, out_ref, slots, send_sems, recv_sems):
        x, y, c, _ = _place()
        me = 4 * x + 2 * y + c
        copies = []
        for k, (dx, dy, dc) in enumerate(flips):
            peer = (1 - x if dx else x, 1 - y if dy else y, 1 - c if dc else c)
            cp = pltpu.make_async_remote_copy(
                src_ref=pack_ref, dst_ref=slots.at[me], send_sem=send_sems.at[k], recv_sem=recv_sems.at[k],
                device_id=peer, device_id_type=MESH)
            cp.start()
            copies.append(cp)
        slots[me] = pack_ref[...]
        for cp in copies:
            cp.wait_recv()
        total = slots[0]
        for i in range(1, n_dev):
            total = total + slots[i]
        out_ref[...] = total
        for cp in copies:
            cp.wait_send()

    vmem = pl.BlockSpec(memory_space=pltpu.VMEM)
    return pl.pallas_call(
        body, name="all_reduce_small",
        in_specs=[vmem], out_specs=vmem,
        out_shape=jax.ShapeDtypeStruct(pack.shape, F32),
        scratch_shapes=[pltpu.VMEM((n_dev, rows, cols), F32), pltpu.SemaphoreType.DMA((n_dev - 1,)),
                        pltpu.SemaphoreType.DMA((n_dev - 1,))],
        compiler_params=pltpu.CompilerParams(vmem_limit_bytes=VMEM_LIMIT),
    )(pack)


def _pack_small(vec_nin, vec_nv, b_s, vec_nf, w_s):
    d = vec_nin.shape[-1]
    slabs = [jnp.pad(v.reshape(1, d), ((0, 7), (0, 0))) for v in (vec_nin, vec_nv, b_s, vec_nf)]
    return jnp.concatenate(slabs + [w_s.reshape(-1, d)], axis=0)


def _unpack_small(pack, w_s_shape, b_s_shape):
    d = pack.shape[-1]
    return (pack[0:1], pack[8:9], pack[16].reshape(b_s_shape), pack[24], pack[32:].reshape(w_s_shape))


def kernel(x, norm_in, w_in, norm_v, w_s, b_s, w_o_gmlp, w_o_sb, w_out, norm_final, loss_target, m_norm_in, m_w_in, m_norm_v, m_w_s, m_b_s, m_w_o_gmlp, m_w_o_sb, m_w_out, m_norm_final, v_norm_in, v_w_in, v_norm_v, v_w_s, v_b_s, v_w_o_gmlp, v_w_o_sb, v_w_out, v_norm_final):
    d = x.shape[-1]
    ncol = w_in.shape[-1]
    nrow = w_o_gmlp.shape[-2]
    chip = 2 * lax.axis_index("x") + lax.axis_index("y")

    shards = [w_in[0], w_o_gmlp[0], w_o_sb[0], w_out[0]]
    halves = [_cast_bf16(s).reshape(2, s.shape[0] // 2, s.shape[1]) for s in shards]
    wg, w_og, w_osb, w_o = _all_gather_halves(halves)
    wg = wg.reshape(N_CHIP, d, ncol)

    (loss, grad_x, g_nin, g_win, g_nv, g_ws, g_bs, g_wog, g_wosb, g_wout, g_nf) = _local_step(
        x, loss_target, norm_in, wg, norm_v, w_s[0], b_s[0],
        w_og.reshape(d, d), w_osb.reshape(d, d), w_o.reshape(d, d), norm_final)

    mats = [g_wog, g_wosb, g_wout]
    chunks = [_cast_chunks_bf16(g_win)] + [_cast_bf16(g).reshape(N_CHIP, nrow, d) for g in mats]
    received = _scatter_chunks(chunks)
    own = [lax.dynamic_slice(g_win, (0, chip * ncol), (d, ncol))] + [
        lax.dynamic_slice(g, (chip * nrow, 0), (nrow, d)) for g in mats]
    sums = [_add_received(o, r) for o, r in zip(own, received)]
    sibling_sums = _swap_with_sibling(sums)
    stats = []
    for w, m, v, mine, theirs in zip(shards, [m_w_in[0], m_w_o_gmlp[0], m_w_o_sb[0], m_w_out[0]],
                                     [v_w_in[0], v_w_o_gmlp[0], v_w_o_sb[0], v_w_out[0]], sums, sibling_sums):
        stats.append(_adamw(w, m, v, [mine, theirs]))
    (gw_in, dw_in, nm_in, nv_in), (gw_og, dw_og, nm_og, nv_og), (gw_osb, dw_osb, nm_osb, nv_osb), \
        (gw_out, dw_out, nm_out, nv_out) = stats

    g_small = _all_reduce_small(_pack_small(g_nin, g_nv, g_bs, g_nf, g_ws))
    gs, ds, ms, vs = _adamw(
        _pack_small(norm_in, norm_v, b_s[0], norm_final, w_s[0]),
        _pack_small(m_norm_in, m_norm_v, m_b_s[0], m_norm_final, m_w_s[0]),
        _pack_small(v_norm_in, v_norm_v, v_b_s[0], v_norm_final, v_w_s[0]), [g_small])

    def small(pack):
        nin, nv, bs, nf, ws = _unpack_small(pack, w_s.shape, b_s.shape)
        return nin, nv, ws, bs, nf

    loss = lax.psum(loss[0, 0], ("x", "y", "c"))
    out = []
    for small_pack, win, wog, wosb, wout in ((gs, gw_in, gw_og, gw_osb, gw_out), (ds, dw_in, dw_og, dw_osb, dw_out),
                                             (ms, nm_in, nm_og, nm_osb, nm_out), (vs, nv_in, nv_og, nv_osb, nv_out)):
        nin, nv, ws, bs, nf = small(small_pack)
        out += [nin, win[None], nv, ws, bs, wog[None], wosb[None], wout[None], nf]
    return (loss, grad_x, *out)
```

```python
import functools
import math

import jax
import jax.numpy as jnp
from jax import lax
from jax.experimental import pallas as pl
from jax.experimental.pallas import tpu as pltpu

F32 = jnp.float32
BF16 = jnp.bfloat16
EPS = 1e-6
HEADS = 8
N_SPLIT = 9
N_CHIP = 4
MESH = pl.DeviceIdType.MESH

ADAM_LR = 0.001
ADAM_B1 = 0.9
ADAM_B2 = 0.999
ADAM_EPS = 1e-08
ADAM_WD = 0.01
ADAM_STEP = 10

VMEM_LIMIT = 56 * 2 ** 20
TM = 256
ATT_T = 256
ATT_HP = 4
TKW = 512

NT = (((1,), (1,)), ((), ()))
TN = (((0,), (0,)), ((), ()))


def _params(sem):
    return pltpu.CompilerParams(dimension_semantics=sem, vmem_limit_bytes=VMEM_LIMIT)


def _resident(shape):
    nd = len(shape)
    return pl.BlockSpec(shape, lambda *_: (0,) * nd, pipeline_mode=pl.Buffered(1))


def _const(shape):
    nd = len(shape)
    return pl.BlockSpec(shape, lambda *_: (0,) * nd)


def _segments(d, ncol):
    segs = []
    edges = sorted({j * ncol for j in range(N_CHIP + 1)} | {s * d for s in range(N_SPLIT + 1)})
    for lo, hi in zip(edges[:-1], edges[1:]):
        segs.append((lo // ncol, lo % ncol, lo // d, lo % d, hi - lo))
    return segs


def _sigmoid(x):
    return 1.0 / (1.0 + jnp.exp(-x))


_GELU_C = math.sqrt(2.0 / math.pi)


def _gelu(x):
    return 0.5 * x * (1.0 + jnp.tanh(_GELU_C * (x + 0.044715 * (x * x * x))))


def _gelu_and_grad(x):
    x2 = x * x
    th = jnp.tanh(_GELU_C * (x + 0.044715 * (x2 * x)))
    val = 0.5 * x * (1.0 + th)
    grad = 0.5 * (1.0 + th) + 0.5 * x * (1.0 - th * th) * (_GELU_C * (1.0 + 3.0 * 0.044715 * x2))
    return val, grad


def _split_bf16(a):
    hi = a.astype(BF16)
    lo = (a - hi.astype(F32)).astype(BF16)
    return hi, lo


def _in_proj_fwd(x2d, g_in, wg):
    t, d = x2d.shape
    ncol = wg.shape[2]
    segs = _segments(d, ncol)

    def body(x_ref, g_ref, w_ref, apre_ref, qkv_ref, gpre_ref, ht_ref):
        x = x_ref[...]
        r = lax.rsqrt(jnp.mean(x * x, axis=-1, keepdims=True) + EPS)
        hf = x * r * g_ref[...]
        h = hf.astype(BF16)
        ht_ref[...] = hf.T.astype(BF16)
        outs = (apre_ref, qkv_ref, gpre_ref)
        for chip, c0, grp, s0, width in segs:
            res = jnp.dot(h, w_ref[chip, :, c0:c0 + width], preferred_element_type=F32)
            o = outs[grp // 3]
            off = (grp % 3) * d + s0
            o[:, off:off + width] = res.astype(o.dtype)

    return pl.pallas_call(
        body, name="in_proj_fwd",
        grid=(t // TM,),
        in_specs=[pl.BlockSpec((TM, d), lambda i: (i, 0)), _const((1, d)), _resident(wg.shape)],
        out_specs=[pl.BlockSpec((TM, 3 * d), lambda i: (i, 0)), pl.BlockSpec((TM, 3 * d), lambda i: (i, 0)),
                   pl.BlockSpec((TM, 3 * d), lambda i: (i, 0)), pl.BlockSpec((d, TM), lambda i: (0, i))],
        out_shape=[jax.ShapeDtypeStruct((t, 3 * d), F32), jax.ShapeDtypeStruct((t, 3 * d), BF16),
                   jax.ShapeDtypeStruct((t, 3 * d), F32), jax.ShapeDtypeStruct((d, t), BF16)],
        compiler_params=_params(("arbitrary",)),
    )(x2d, g_in, wg)


def _branch_a_fwd(a_pre, g_v, wm, b_t):
    t, d3 = a_pre.shape
    d = d3 // 3
    ng, chunk, _ = wm.shape
    cw = d // ng

    def body(a_ref, gv_ref, wm_ref, bt_ref, ya_ref):
        ua = _gelu(a_ref[:, 0:d])
        vg = _gelu(a_ref[:, d:2 * d])
        za = a_ref[:, 2 * d:3 * d]
        rv = lax.rsqrt(jnp.mean(vg * vg, axis=-1, keepdims=True) + EPS)
        va = (vg * rv * gv_ref[...]).astype(BF16)
        gate = ua * (za * _sigmoid(za))
        for g in range(ng):
            sl = slice(g * cw, (g + 1) * cw)
            mixed = jnp.dot(wm_ref[g], va[:, sl], preferred_element_type=F32) + bt_ref[:, g:g + 1]
            ya_ref[:, sl] = (gate[:, sl] * mixed).astype(BF16)

    return pl.pallas_call(
        body, name="branch_a_fwd",
        grid=(t // chunk,),
        in_specs=[pl.BlockSpec((chunk, d3), lambda i: (i, 0)), _const((1, d)), _const(wm.shape), _const(b_t.shape)],
        out_specs=pl.BlockSpec((chunk, d), lambda i: (i, 0)),
        out_shape=jax.ShapeDtypeStruct((t, d), BF16),
        compiler_params=_params(("arbitrary",)),
    )(a_pre, g_v, wm, b_t)


def _branch_a_bwd(a_pre, dya, g_v, wm, wm_t, b_t):
    t, d3 = a_pre.shape
    d = d3 // 3
    ng, chunk, _ = wm.shape
    cw = d // ng
    nsteps = t // chunk

    def body(a_ref, dya_ref, gv_ref, wm_ref, wmt_ref, bt_ref, da_ref, gws_ref, gbt_ref, gnv_ref, db_acc):
        i = pl.program_id(0)

        @pl.when(i == 0)
        def _():
            gws_ref[...] = jnp.zeros_like(gws_ref)
            gnv_ref[...] = jnp.zeros_like(gnv_ref)
            db_acc[...] = jnp.zeros_like(db_acc)

        ua, dgelu_u = _gelu_and_grad(a_ref[:, 0:d])
        vg, dgelu_v = _gelu_and_grad(a_ref[:, d:2 * d])
        za = a_ref[:, 2 * d:3 * d]
        sig = _sigmoid(za)
        sz = za * sig
        dsz = sig * (1.0 + za * (1.0 - sig))
        rv = lax.rsqrt(jnp.mean(vg * vg, axis=-1, keepdims=True) + EPS)
        nv = vg * rv
        gv = gv_ref[...]
        va = (nv * gv).astype(BF16)
        dya = dya_ref[...]
        dmix = dya * ua * sz
        db_acc[...] += dmix
        dmix_b = dmix.astype(BF16)
        t_gate = dya * sz
        t_z = dya * ua * dsz
        dva_parts = []
        for g in range(ng):
            sl = slice(g * cw, (g + 1) * cw)
            mixed = jnp.dot(wm_ref[g], va[:, sl], preferred_element_type=F32) + bt_ref[:, g:g + 1]
            da_ref[:, sl] = (t_gate[:, sl] * mixed * dgelu_u[:, sl]).astype(BF16)
            da_ref[:, 2 * d + g * cw:2 * d + (g + 1) * cw] = (t_z[:, sl] * mixed).astype(BF16)
            gws_ref[g] += lax.dot_general(dmix_b[:, sl], va[:, sl], NT, preferred_element_type=F32)
            dva_parts.append(jnp.dot(wmt_ref[g], dmix_b[:, sl], preferred_element_type=F32))
        dva = jnp.concatenate(dva_parts, axis=1)
        gnv_ref[...] += jnp.sum(dva * nv, axis=0, keepdims=True)
        dnv = dva * gv
        dvg = rv * (dnv - nv * jnp.mean(dnv * nv, axis=-1, keepdims=True))
        da_ref[:, d:2 * d] = (dvg * dgelu_v).astype(BF16)

        @pl.when(i == nsteps - 1)
        def _():
            acc = db_acc[...]
            for g in range(ng):
                gbt_ref[:, g:g + 1] = jnp.sum(acc[:, g * cw:(g + 1) * cw], axis=1, keepdims=True)

    return pl.pallas_call(
        body, name="branch_a_bwd",
        grid=(nsteps,),
        in_specs=[pl.BlockSpec((chunk, d3), lambda i: (i, 0)), pl.BlockSpec((chunk, d), lambda i: (i, 0)),
                  _const((1, d)), _const(wm.shape), _const(wm_t.shape), _const(b_t.shape)],
        out_specs=[pl.BlockSpec((chunk, d3), lambda i: (i, 0)), _const(wm.shape), _const(b_t.shape), _const((1, d))],
        out_shape=[jax.ShapeDtypeStruct((t, d3), BF16), jax.ShapeDtypeStruct(wm.shape, F32),
                   jax.ShapeDtypeStruct(b_t.shape, F32), jax.ShapeDtypeStruct((1, d), F32)],
        scratch_shapes=[pltpu.VMEM((chunk, d), F32)],
        compiler_params=_params(("arbitrary",)),
    )(a_pre, dya, g_v, wm, wm_t, b_t)


def _tri(n, rows_gt_cols):
    r = lax.broadcasted_iota(jnp.int32, (n, n), 0)
    c = lax.broadcasted_iota(jnp.int32, (n, n), 1)
    return (r > c) if rows_gt_cols else (r < c)


def _twice(tri):
    t = tri.astype(BF16)
    return jnp.concatenate([t, t], axis=0)


def _cumsum_mm(a, tri2):
    hi, lo = _split_bf16(a)
    return jnp.dot(jnp.concatenate([hi, lo], axis=1), tri2, preferred_element_type=F32)


LOG2E = 1.4426950408889634
_SIGN = 0x80000000


def _sb_block(q, k, scale, upper2, causal):
    z2 = lax.dot_general(q, k, NT, preferred_element_type=F32) * (scale * LOG2E)
    neg_abs = lax.bitcast_convert_type(lax.bitcast_convert_type(z2, jnp.uint32) | jnp.uint32(_SIGN), F32)
    l2 = jnp.log(1.0 + jnp.exp2(neg_abs)) * LOG2E
    log_beta = jnp.minimum(z2, 0.0) - l2
    lom = log_beta - z2
    if causal is not None:
        lom = jnp.where(causal, lom, 0.0)
    sx = _cumsum_mm(lom, upper2)
    return log_beta, sx, sx[:, 0:1] + lom[:, 0:1]


def _attn_specs(d, seq, nq):
    hp_w = ATT_HP * (d // HEADS)
    n_hp = d // hp_w
    row_blk = pl.BlockSpec((ATT_T, hp_w), lambda b, h, i: (b * nq + i, h))
    k_blk = pl.BlockSpec((seq, hp_w), lambda b, h, i: (b, n_hp + h))
    v_blk = pl.BlockSpec((seq, hp_w), lambda b, h, i: (b, 2 * n_hp + h))
    seq_blk = pl.BlockSpec((seq, hp_w), lambda b, h, i: (b, h))
    return row_blk, k_blk, v_blk, seq_blk, n_hp


def _attn_fwd(qkv, g_pre, bsz, seq):
    t, d3 = qkv.shape
    d = d3 // 3
    hd = d // HEADS
    nq = seq // ATT_T
    scale = hd ** -0.5
    row_blk, k_blk, v_blk, _, n_hp = _attn_specs(d, seq, nq)

    def body(q_ref, k_ref, v_ref, zb_ref, o_ref, yb_ref):
        i = pl.program_id(2)
        causal = _tri(ATT_T, True)
        upper2 = _twice(causal)

        def step(kb, state, mask):
            rows = pl.ds(pl.multiple_of(kb * ATT_T, ATT_T), ATT_T)
            heads = [slice(h * hd, (h + 1) * hd) for h in range(ATT_HP)]
            scores = [_sb_block(q_ref[:, cols], k_ref[rows, cols], scale, upper2, mask) for cols in heads]
            new = []
            for cols, (carry, acc), (log_beta, sx, total) in zip(heads, state, scores):
                a = jnp.exp2(log_beta + sx + carry)
                if mask is not None:
                    a = jnp.where(mask, a, 0.0)
                acc = acc + jnp.dot(a.astype(BF16), v_ref[rows, cols], preferred_element_type=F32)
                new.append((carry + total, acc))
            return tuple(new)

        init = tuple((jnp.zeros((ATT_T, 1), F32), jnp.zeros((ATT_T, hd), F32)) for _ in range(ATT_HP))
        state = step(i, init, causal)
        state = lax.fori_loop(0, i, lambda n, s: step(i - 1 - n, s, None), state)
        for h in range(ATT_HP):
            cols = slice(h * hd, (h + 1) * hd)
            acc = state[h][1]
            zb = zb_ref[:, cols]
            o_ref[:, cols] = acc
            yb_ref[:, cols] = (acc * (zb * _sigmoid(zb))).astype(BF16)

    return pl.pallas_call(
        body, name="attn_fwd",
        grid=(bsz, n_hp, nq),
        in_specs=[row_blk, k_blk, v_blk, row_blk],
        out_specs=[row_blk, row_blk],
        out_shape=[jax.ShapeDtypeStruct((t, d), F32), jax.ShapeDtypeStruct((t, d), BF16)],
        compiler_params=_params(("arbitrary", "arbitrary", "arbitrary")),
    )(qkv, qkv, qkv, g_pre)


def _attn_bwd(qkv, g_pre, o, dyb, bsz, seq):
    t, d3 = qkv.shape
    d = d3 // 3
    hd = d // HEADS
    nq = seq // ATT_T
    scale = hd ** -0.5
    row_blk, k_blk, v_blk, seq_blk, n_hp = _attn_specs(d, seq, nq)

    def body(q_ref, k_ref, v_ref, zb_ref, o_ref, dyb_ref, dq_ref, dk_ref, dv_ref, dzb_ref,
             g_s, beta_s, dkt_acc, dvt_acc):
        i = pl.program_id(2)

        @pl.when(i == 0)
        def _():
            dkt_acc[...] = jnp.zeros_like(dkt_acc)
            dvt_acc[...] = jnp.zeros_like(dvt_acc)

        causal = _tri(ATT_T, True)
        upper2 = _twice(causal)
        lower2 = _twice(~causal)
        zb = zb_ref[...]
        sig = _sigmoid(zb)
        dyb_t = dyb_ref[...]
        do_f = dyb_t * (zb * sig)
        do = do_f.astype(BF16)
        do_t = do_f.T.astype(BF16)
        q_t = q_ref[...].astype(F32).T.astype(BF16)
        dzb_ref[...] = (dyb_t * o_ref[...] * (sig * (1.0 + zb * (1.0 - sig)))).astype(BF16)

        def sweep(kb, carries, mask):
            rows = pl.ds(pl.multiple_of(kb * ATT_T, ATT_T), ATT_T)
            heads = [slice(h * hd, (h + 1) * hd) for h in range(ATT_HP)]
            scores = [_sb_block(q_ref[:, cols], k_ref[rows, cols], scale, upper2, mask) for cols in heads]
            das = [lax.dot_general(do[:, cols], v_ref[rows, cols], NT, preferred_element_type=F32) for cols in heads]
            new = []
            for h, (cols, carry, (log_beta, sx, total), da) in enumerate(zip(heads, carries, scores, das)):
                a = jnp.exp2(log_beta + sx + carry)
                beta = jnp.exp2(log_beta)
                if mask is not None:
                    a = jnp.where(mask, a, 0.0)
                    beta = jnp.where(mask, beta, 0.0)
                g_s[h, kb] = a * da
                beta_s[h, kb] = beta
                dvt_acc[kb, cols, :] += jnp.dot(do_t[cols, :], a.astype(BF16), preferred_element_type=F32)
                new.append(carry + total)
            return tuple(new)

        carries = sweep(i, tuple(jnp.zeros((ATT_T, 1), F32) for _ in range(ATT_HP)), causal)
        lax.fori_loop(0, i, lambda n, c: sweep(i - 1 - n, c, None), carries)

        def back(kb, state):
            rows = pl.ds(pl.multiple_of(kb * ATT_T, ATT_T), ATT_T)
            heads = [slice(h * hd, (h + 1) * hd) for h in range(ATT_HP)]
            sums = [_cumsum_mm(g_s[h, kb], lower2) for h in range(ATT_HP)]
            new = []
            for h, (cols, (p_carry, dq), px) in enumerate(zip(heads, state, sums)):
                dz = ((g_s[h, kb] - (p_carry + px) * beta_s[h, kb]) * scale).astype(BF16)
                dq = dq + jnp.dot(dz, k_ref[rows, cols], preferred_element_type=F32)
                dkt_acc[kb, cols, :] += jnp.dot(q_t[cols, :], dz, preferred_element_type=F32)
                new.append((p_carry + px[:, ATT_T - 1:ATT_T], dq))
            return tuple(new)

        init = tuple((jnp.zeros((ATT_T, 1), F32), jnp.zeros((ATT_T, hd), F32)) for _ in range(ATT_HP))
        state = lax.fori_loop(0, i + 1, back, init)
        for h in range(ATT_HP):
            dq_ref[:, h * hd:(h + 1) * hd] = state[h][1].astype(BF16)

        @pl.when(i == nq - 1)
        def _():
            for kb in range(nq):
                dk_ref[kb * ATT_T:(kb + 1) * ATT_T, :] = dkt_acc[kb].T.astype(BF16)
                dv_ref[kb * ATT_T:(kb + 1) * ATT_T, :] = dvt_acc[kb].T.astype(BF16)

    out = jax.ShapeDtypeStruct((t, d), BF16)
    hp_w = ATT_HP * hd
    return pl.pallas_call(
        body, name="attn_bwd",
        grid=(bsz, n_hp, nq),
        in_specs=[row_blk, k_blk, v_blk, row_blk, row_blk, row_blk],
        out_specs=[row_blk, seq_blk, seq_blk, row_blk],
        out_shape=[out, out, out, out],
        scratch_shapes=[pltpu.VMEM((ATT_HP, nq, ATT_T, ATT_T), F32), pltpu.VMEM((ATT_HP, nq, ATT_T, ATT_T), F32),
                        pltpu.VMEM((nq, hp_w, ATT_T), F32), pltpu.VMEM((nq, hp_w, ATT_T), F32)],
        compiler_params=_params(("arbitrary", "arbitrary", "arbitrary")),
    )(qkv, qkv, qkv, g_pre, o, dyb)


def _out_proj(ya, yb, g_pre, x2d, tgt, w_og, w_osb, w_out, g_f):
    t, d = x2d.shape

    def body(ya_ref, yb_ref, ga_ref, gb_ref, x_ref, tgt_ref, wog_ref, wosb_ref, wout_ref, gf_ref,
             dya_ref, dyb_ref, dg_ref, dx2_ref, loss_ref, gnf_ref, gwog_ref, gwosb_ref, gwout_ref):
        @pl.when(pl.program_id(0) == 0)
        def _():
            loss_ref[...] = jnp.zeros_like(loss_ref)
            gnf_ref[...] = jnp.zeros_like(gnf_ref)
            gwog_ref[...] = jnp.zeros_like(gwog_ref)
            gwosb_ref[...] = jnp.zeros_like(gwosb_ref)
            gwout_ref[...] = jnp.zeros_like(gwout_ref)

        ya = ya_ref[...]
        yb = yb_ref[...]
        pa = jnp.dot(ya, wog_ref[...], preferred_element_type=F32)
        pb = jnp.dot(yb, wosb_ref[...], preferred_element_type=F32)
        sga = _sigmoid(ga_ref[...])
        sgb = _sigmoid(gb_ref[...])
        merged = (sga * pa + sgb * pb).astype(BF16)
        x2 = x_ref[...] + jnp.dot(merged, wout_ref[...], preferred_element_type=F32)
        r2 = lax.rsqrt(jnp.mean(x2 * x2, axis=-1, keepdims=True) + EPS)
        n2 = x2 * r2
        gf = gf_ref[...]
        err = n2 * gf - tgt_ref[...]
        loss_ref[...] += 0.5 * jnp.sum(jnp.sum(err * err, axis=-1, keepdims=True), axis=0, keepdims=True) / d
        dy = err * (1.0 / d)
        gnf_ref[...] += jnp.sum(dy * n2, axis=0, keepdims=True)
        dn = dy * gf
        dx2 = r2 * (dn - n2 * jnp.mean(dn * n2, axis=-1, keepdims=True))
        dx2_ref[...] = dx2
        dx2_b = dx2.astype(BF16)
        dmerged = lax.dot_general(dx2_b, wout_ref[...], NT, preferred_element_type=F32)
        gwout_ref[...] += lax.dot_general(merged, dx2_b, TN, preferred_element_type=F32)
        dg_ref[:, 0:d] = (dmerged * pa * (sga * (1.0 - sga))).astype(BF16)
        dg_ref[:, d:2 * d] = (dmerged * pb * (sgb * (1.0 - sgb))).astype(BF16)
        dpa = (dmerged * sga).astype(BF16)
        dpb = (dmerged * sgb).astype(BF16)
        dya_ref[...] = lax.dot_general(dpa, wog_ref[...], NT, preferred_element_type=F32)
        dyb_ref[...] = lax.dot_general(dpb, wosb_ref[...], NT, preferred_element_type=F32)
        gwog_ref[...] += lax.dot_general(ya, dpa, TN, preferred_element_type=F32)
        gwosb_ref[...] += lax.dot_general(yb, dpb, TN, preferred_element_type=F32)

    row = lambda i: (i, 0)
    return pl.pallas_call(
        body, name="out_proj",
        grid=(t // TM,),
        in_specs=[pl.BlockSpec((TM, d), row), pl.BlockSpec((TM, d), row),
                  pl.BlockSpec((TM, d), lambda i: (i, 1)), pl.BlockSpec((TM, d), lambda i: (i, 2)),
                  pl.BlockSpec((TM, d), row), pl.BlockSpec((TM, d), row),
                  _resident((d, d)), _resident((d, d)), _resident((d, d)), _const((1, d))],
        out_specs=[pl.BlockSpec((TM, d), row), pl.BlockSpec((TM, d), row), pl.BlockSpec((TM, 2 * d), row),
                   pl.BlockSpec((TM, d), row), _const((1, 1)), _const((1, d)),
                   _const((d, d)), _const((d, d)), _const((d, d))],
        out_shape=[jax.ShapeDtypeStruct((t, d), F32), jax.ShapeDtypeStruct((t, d), F32),
                   jax.ShapeDtypeStruct((t, 2 * d), BF16), jax.ShapeDtypeStruct((t, d), F32),
                   jax.ShapeDtypeStruct((1, 1), F32), jax.ShapeDtypeStruct((1, d), F32),
                   jax.ShapeDtypeStruct((d, d), F32), jax.ShapeDtypeStruct((d, d), F32),
                   jax.ShapeDtypeStruct((d, d), F32)],
        compiler_params=_params(("arbitrary",)),
    )(ya, yb, g_pre, g_pre, x2d, tgt, w_og, w_osb, w_out, g_f)


def _dproj_pieces(d):
    return [(0, 0, 3), (1, 3, 1), (2, 4, 1), (3, 5, 1), (4, 6, 1), (5, 7, 2)]


def _in_proj_bwd_x(pieces, wg, x2d, g_in, dx2):
    t, d = x2d.shape
    ncol = wg.shape[2]
    segs = _segments(d, ncol)
    layout = _dproj_pieces(d)

    def body(da_ref, dq_ref, dk_ref, dv_ref, dzb_ref, dg_ref, w_ref, x_ref, g_ref, dx2_ref, gx_ref, gn_ref):
        @pl.when(pl.program_id(0) == 0)
        def _():
            gn_ref[...] = jnp.zeros_like(gn_ref)

        refs = (da_ref, dq_ref, dk_ref, dv_ref, dzb_ref, dg_ref)
        dh = jnp.zeros((TM, d), F32)
        for chip, c0, grp, s0, width in segs:
            piece, first, _ = next(p for p in layout if p[1] <= grp < p[1] + p[2])
            off = (grp - first) * d + s0
            dh = dh + lax.dot_general(refs[piece][:, off:off + width], w_ref[chip, :, c0:c0 + width], NT,
                                      preferred_element_type=F32)
        x = x_ref[...]
        r = lax.rsqrt(jnp.mean(x * x, axis=-1, keepdims=True) + EPS)
        n = x * r
        gn_ref[...] += jnp.sum(dh * n, axis=0, keepdims=True)
        dn = dh * g_ref[...]
        gx_ref[...] = dx2_ref[...] + r * (dn - n * jnp.mean(dn * n, axis=-1, keepdims=True))

    row = lambda i: (i, 0)
    return pl.pallas_call(
        body, name="in_proj_bwd_x",
        grid=(t // TM,),
        in_specs=[pl.BlockSpec((TM, p.shape[1]), row) for p in pieces]
        + [_resident(wg.shape), pl.BlockSpec((TM, d), row), _const((1, d)), pl.BlockSpec((TM, d), row)],
        out_specs=[pl.BlockSpec((TM, d), row), _const((1, d))],
        out_shape=[jax.ShapeDtypeStruct((t, d), F32), jax.ShapeDtypeStruct((1, d), F32)],
        compiler_params=_params(("arbitrary",)),
    )(*pieces, wg, x2d, g_in, dx2)


def _in_proj_bwd_w(h_t, pieces):
    d, t = h_t.shape
    nk = t // TKW
    layout = _dproj_pieces(d)

    def body(ht_ref, da_ref, dq_ref, dk_ref, dv_ref, dzb_ref, dg_ref, gw_ref, acc):
        s = pl.program_id(0)
        i = pl.program_id(1)

        @pl.when(i == 0)
        def _():
            acc[...] = jnp.zeros_like(acc)

        refs = (da_ref, dq_ref, dk_ref, dv_ref, dzb_ref, dg_ref)
        for piece, first, count in layout:
            for j in range(count):
                @pl.when(s == first + j)
                def _(piece=piece, j=j):
                    acc[...] += jnp.dot(ht_ref[...], refs[piece][:, j * d:(j + 1) * d], preferred_element_type=F32)

        @pl.when(i == nk - 1)
        def _():
            gw_ref[...] = acc[...]

    def piece_spec(p, first, count):
        width = count * d
        return pl.BlockSpec((TKW, width), lambda s, i: (jnp.where((s >= first) & (s < first + count), i, 0), 0))

    return pl.pallas_call(
        body, name="in_proj_bwd_w",
        grid=(N_SPLIT, nk),
        in_specs=[pl.BlockSpec((d, TKW), lambda s, i: (0, i))] + [piece_spec(*p) for p in layout],
        out_specs=pl.BlockSpec((d, d), lambda s, i: (0, s)),
        out_shape=jax.ShapeDtypeStruct((d, N_SPLIT * d), F32),
        scratch_shapes=[pltpu.VMEM((d, d), F32)],
        compiler_params=_params(("arbitrary", "arbitrary")),
    )(h_t, *pieces)


def _local_step(x, tgt, norm_in, wg, norm_v, w_s, b_s, w_og, w_osb, w_out, norm_final):
    bsz, seq, d = x.shape
    t = bsz * seq
    x2d = x.reshape(t, d)
    tgt2d = tgt.reshape(t, d)
    chunk = w_s.shape[-1]
    causal = jnp.tril(jnp.ones((chunk, chunk), dtype=bool))
    wm = jnp.where(causal[None], w_s, 0.0).astype(BF16)
    wm_t = jnp.swapaxes(wm, 1, 2)
    b_t = b_s.T

    a_pre, qkv, g_pre, h_t = _in_proj_fwd(x2d, norm_in, wg)
    ya = _branch_a_fwd(a_pre, norm_v, wm, b_t)
    o, yb = _attn_fwd(qkv, g_pre, bsz, seq)
    dya, dyb, dg, dx2, loss, g_nf, g_wog, g_wosb, g_wout = _out_proj(
        ya, yb, g_pre, x2d, tgt2d, w_og, w_osb, w_out, norm_final.reshape(1, d))
    dq, dk, dv, dzb = _attn_bwd(qkv, g_pre, o, dyb, bsz, seq)
    d_a, g_ws, g_bt, g_nv = _branch_a_bwd(a_pre, dya, norm_v, wm, wm_t, b_t)
    pieces = (d_a, dq, dk, dv, dzb, dg)
    gx, g_nin = _in_proj_bwd_x(pieces, wg, x2d, norm_in, dx2)
    g_win = _in_proj_bwd_w(h_t, pieces)
    g_ws = jnp.where(causal[None], g_ws, 0.0)
    return (loss, gx.reshape(bsz, seq, d), g_nin, g_win, g_nv, g_ws, g_bt.T, g_wog, g_wosb, g_wout, g_nf)


def _row_tile(rows):
    return next(r for r in (128, 64, 32, 16, 8) if rows % r == 0)


def _cast_bf16(a):
    rows, cols = a.shape
    tr = _row_tile(rows)

    def body(a_ref, o_ref):
        o_ref[...] = a_ref[...].astype(BF16)

    return pl.pallas_call(
        body, name="cast_bf16", grid=(rows // tr,),
        in_specs=[pl.BlockSpec((tr, cols), lambda i: (i, 0))],
        out_specs=pl.BlockSpec((tr, cols), lambda i: (i, 0)),
        out_shape=jax.ShapeDtypeStruct(a.shape, BF16),
        compiler_params=_params(("arbitrary",)),
    )(a)


def _cast_chunks_bf16(g):
    rows, cols4 = g.shape
    cols = cols4 // N_CHIP
    tr = _row_tile(rows)

    def body(g_ref, o_ref):
        o_ref[0] = g_ref[...].astype(BF16)

    return pl.pallas_call(
        body, name="cast_chunks_bf16", grid=(N_CHIP, rows // tr),
        in_specs=[pl.BlockSpec((tr, cols), lambda j, i: (i, j))],
        out_specs=pl.BlockSpec((1, tr, cols), lambda j, i: (j, i, 0)),
        out_shape=jax.ShapeDtypeStruct((N_CHIP, rows, cols), BF16),
        compiler_params=_params(("arbitrary", "arbitrary")),
    )(g)


def _add_received(own, recv):
    rows, cols = own.shape
    tr = _row_tile(rows)

    def body(own_ref, recv_ref, o_ref):
        s = own_ref[...]
        for k in range(N_CHIP - 1):
            s = s + recv_ref[k].astype(F32)
        o_ref[...] = s

    return pl.pallas_call(
        body, name="add_received", grid=(rows // tr,),
        in_specs=[pl.BlockSpec((tr, cols), lambda i: (i, 0)), pl.BlockSpec((N_CHIP - 1, tr, cols), lambda i: (0, i, 0))],
        out_specs=pl.BlockSpec((tr, cols), lambda i: (i, 0)),
        out_shape=jax.ShapeDtypeStruct(own.shape, F32),
        compiler_params=_params(("arbitrary",)),
    )(own, recv)


def _adamw(w, m, v, g_parts):
    rows, cols = w.shape
    tr = _row_tile(rows)
    n_parts = len(g_parts)
    c1 = 1.0 - ADAM_B1 ** ADAM_STEP
    c2 = 1.0 - ADAM_B2 ** ADAM_STEP

    def body(*refs):
        w_ref, m_ref, v_ref = refs[:3]
        part_refs = refs[3:3 + n_parts]
        g_ref, d_ref, nm_ref, nv_ref = refs[3 + n_parts:]
        g = part_refs[0][...]
        for p in part_refs[1:]:
            g = g + p[...]
        new_m = ADAM_B1 * m_ref[...] + (1.0 - ADAM_B1) * g
        new_v = ADAM_B2 * v_ref[...] + (1.0 - ADAM_B2) * (g * g)
        m_hat = new_m / c1
        v_hat = new_v / c2
        g_ref[...] = g
        d_ref[...] = -ADAM_LR * (m_hat / (jnp.sqrt(v_hat) + ADAM_EPS) + ADAM_WD * w_ref[...])
        nm_ref[...] = new_m
        nv_ref[...] = new_v

    spec = pl.BlockSpec((tr, cols), lambda i: (i, 0))
    out = jax.ShapeDtypeStruct(w.shape, F32)
    return pl.pallas_call(
        body, name="adamw", grid=(rows // tr,),
        in_specs=[spec] * (3 + n_parts), out_specs=[spec] * 4, out_shape=[out] * 4,
        compiler_params=_params(("arbitrary",)),
    )(w, m, v, *g_parts)


ANY = pl.BlockSpec(memory_space=pl.ANY)


def _place():
    x, y, c = lax.axis_index("x"), lax.axis_index("y"), lax.axis_index("c")
    other_chips = [(1 - x, y), (x, 1 - y), (1 - x, 1 - y)]
    return x, y, c, other_chips


def _all_gather_halves(halves):
    n = len(halves)

    def body(*refs):
        ins, outs = refs[:n], refs[n:2 * n]
        send_sems, recv_sems, local_sems = refs[2 * n:]
        x, y, c, chips = _place()
        sibling = (x, y, 1 - c)
        my_chip = 2 * x + y

        def copy(a, k, block, to, src=None):
            return pltpu.make_async_remote_copy(
                src_ref=outs[a].at[block] if src is None else src, dst_ref=outs[a].at[block],
                send_sem=send_sems.at[a, k], recv_sem=recv_sems.at[a, k], device_id=to, device_id_type=MESH)

        local = [pltpu.make_async_copy(ins[a], outs[a].at[pl.ds(2 * my_chip, 2)], local_sems.at[a]) for a in range(n)]
        for cp in local:
            cp.start()
        started = []
        for k, (px, py) in enumerate(chips):
            for a in range(n):
                cp = copy(a, k, 2 * my_chip + c, (px, py, c), src=ins[a].at[c])
                cp.start()
                started.append(cp)
        for k, (px, py) in enumerate(chips):
            for a in range(n):
                block = 2 * (2 * px + py) + c
                copy(a, k, block, (x, y, c)).wait_recv()
                cp = copy(a, 3 + k, block, sibling)
                cp.start()
                started.append(cp)
        for k, (px, py) in enumerate(chips):
            for a in range(n):
                copy(a, 3 + k, 2 * (2 * px + py) + 1 - c, (x, y, c)).wait_recv()
        for cp in started:
            cp.wait_send()
        for cp in local:
            cp.wait()

    return pl.pallas_call(
        body, name="all_gather_weights",
        in_specs=[ANY] * n, out_specs=[ANY] * n,
        out_shape=[jax.ShapeDtypeStruct((2 * N_CHIP,) + h.shape[1:], h.dtype) for h in halves],
        scratch_shapes=[pltpu.SemaphoreType.DMA((n, 6)), pltpu.SemaphoreType.DMA((n, 6)), pltpu.SemaphoreType.DMA((n,))],
    )(*halves)


def _scatter_chunks(chunks):
    n = len(chunks)

    def body(*refs):
        ins, outs = refs[:n], refs[n:2 * n]
        send_sems, recv_sems = refs[2 * n:]
        x, y, c, chips = _place()
        copies = []
        for k, (px, py) in enumerate(chips):
            for a in range(n):
                cp = pltpu.make_async_remote_copy(
                    src_ref=ins[a].at[2 * px + py], dst_ref=outs[a].at[k],
                    send_sem=send_sems.at[a, k], recv_sem=recv_sems.at[a, k], device_id=(px, py, c), device_id_type=MESH)
                cp.start()
                copies.append(cp)
        for cp in copies:
            cp.wait()

    return pl.pallas_call(
        body, name="scatter_grad_chunks",
        in_specs=[ANY] * n, out_specs=[ANY] * n,
        out_shape=[jax.ShapeDtypeStruct((N_CHIP - 1,) + ch.shape[1:], ch.dtype) for ch in chunks],
        scratch_shapes=[pltpu.SemaphoreType.DMA((n, 3)), pltpu.SemaphoreType.DMA((n, 3))],
    )(*chunks)


def _swap_with_sibling(arrs):
    n = len(arrs)

    def body(*refs):
        ins, outs = refs[:n], refs[n:2 * n]
        send_sems, recv_sems = refs[2 * n:]
        x, y, c, _ = _place()
        copies = [pltpu.make_async_remote_copy(
            src_ref=ins[a], dst_ref=outs[a], send_sem=send_sems.at[a], recv_sem=recv_sems.at[a],
            device_id=(x, y, 1 - c), device_id_type=MESH) for a in range(n)]
        for cp in copies:
            cp.start()
        for cp in copies:
            cp.wait()

    return pl.pallas_call(
        body, name="swap_with_sibling",
        in_specs=[ANY] * n, out_specs=[ANY] * n,
        out_shape=[jax.ShapeDtypeStruct(a.shape, a.dtype) for a in arrs],
        scratch_shapes=[pltpu.SemaphoreType.DMA((n,)), pltpu.SemaphoreType.DMA((n,))],
    )(*arrs)


def _all_reduce_small(pack):
    rows, cols = pack.shape
    n_dev = 2 * N_CHIP
    flips = [(dx, dy, dc) for dx in (0, 1) for dy in (0, 1) for dc in (0, 1)][1:]

    def body(pack_ref, out_ref, slots, send_sems, recv_sems):
        x, y, c, _ = _place()
        me = 4 * x + 2 * y + c
        copies = []
        for k, (dx, dy, dc) in enumerate(flips):
            peer = (1 - x if dx else x, 1 - y if dy else y, 1 - c if dc else c)
            cp = pltpu.make_async_remote_copy(
                src_ref=pack_ref, dst_ref=slots.at[me], send_sem=send_sems.at[k], recv_sem=recv_sems.at[k],
                device_id=peer, device_id_type=MESH)
            cp.start()
            copies.append(cp)
        slots[me] = pack_ref[...]
        for cp in copies:
            cp.wait_recv()
        total = slots[0]
        for i in range(1, n_dev):
            total = total + slots[i]
        out_ref[...] = total
        for cp in copies:
            cp.wait_send()

    vmem = pl.BlockSpec(memory_space=pltpu.VMEM)
    return pl.pallas_call(
        body, name="all_reduce_small",
        in_specs=[vmem], out_specs=vmem,
        out_shape=jax.ShapeDtypeStruct(pack.shape, F32),
        scratch_shapes=[pltpu.VMEM((n_dev, rows, cols), F32), pltpu.SemaphoreType.DMA((n_dev - 1,)),
                        pltpu.SemaphoreType.DMA((n_dev - 1,))],
        compiler_params=pltpu.CompilerParams(vmem_limit_bytes=VMEM_LIMIT),
    )(pack)


def _pack_small(vec_nin, vec_nv, b_s, vec_nf, w_s):
    d = vec_nin.shape[-1]
    slabs = [jnp.pad(v.reshape(1, d), ((0, 7), (0, 0))) for v in (vec_nin, vec_nv, b_s, vec_nf)]
    return jnp.concatenate(slabs + [w_s.reshape(-1, d)], axis=0)


def _unpack_small(pack, w_s_shape, b_s_shape):
    d = pack.shape[-1]
    return (pack[0:1], pack[8:9], pack[16].reshape(b_s_shape), pack[24], pack[32:].reshape(w_s_shape))


def kernel(x, norm_in, w_in, norm_v, w_s, b_s, w_o_gmlp, w_o_sb, w_out, norm_final, loss_target, m_norm_in, m_w_in, m_norm_v, m_w_s, m_b_s, m_w_o_gmlp, m_w_o_sb, m_w_out, m_norm_final, v_norm_in, v_w_in, v_norm_v, v_w_s, v_b_s, v_w_o_gmlp, v_w_o_sb, v_w_out, v_norm_final):
    d = x.shape[-1]
    ncol = w_in.shape[-1]
    nrow = w_o_gmlp.shape[-2]
    chip = 2 * lax.axis_index("x") + lax.axis_index("y")

    shards = [w_in[0], w_o_gmlp[0], w_o_sb[0], w_out[0]]
    halves = [_cast_bf16(s).reshape(2, s.shape[0] // 2, s.shape[1]) for s in shards]
    wg, w_og, w_osb, w_o = _all_gather_halves(halves)
    wg = wg.reshape(N_CHIP, d, ncol)

    (loss, grad_x, g_nin, g_win, g_nv, g_ws, g_bs, g_wog, g_wosb, g_wout, g_nf) = _local_step(
        x, loss_target, norm_in, wg, norm_v, w_s[0], b_s[0],
        w_og.reshape(d, d), w_osb.reshape(d, d), w_o.reshape(d, d), norm_final)

    mats = [g_wog, g_wosb, g_wout]
    chunks = [_cast_chunks_bf16(g_win)] + [_cast_bf16(g).reshape(N_CHIP, nrow, d) for g in mats]
    received = _scatter_chunks(chunks)
    own = [lax.dynamic_slice(g_win, (0, chip * ncol), (d, ncol))] + [
        lax.dynamic_slice(g, (chip * nrow, 0), (nrow, d)) for g in mats]
    sums = [_add_received(o, r) for o, r in zip(own, received)]
    sibling_sums = _swap_with_sibling(sums)
    stats = []
    for w, m, v, mine, theirs in zip(shards, [m_w_in[0], m_w_o_gmlp[0], m_w_o_sb[0], m_w_out[0]],
                                     [v_w_in[0], v_w_o_gmlp[0], v_w_o_sb[0], v_w_out[0]], sums, sibling_sums):
        stats.append(_adamw(w, m, v, [mine, theirs]))
    (gw_in, dw_in, nm_in, nv_in), (gw_og, dw_og, nm_og, nv_og), (gw_osb, dw_osb, nm_osb, nv_osb), \
        (gw_out, dw_out, nm_out, nv_out) = stats

    g_small = _all_reduce_small(_pack_small(g_nin, g_nv, g_bs, g_nf, g_ws))
    gs, ds, ms, vs = _adamw(
        _pack_small(norm_in, norm_v, b_s[0], norm_final, w_s[0]),
        _pack_small(m_norm_in, m_norm_v, m_b_s[0], m_norm_final, m_w_s[0]),
        _pack_small(v_norm_in, v_norm_v, v_b_s[0], v_norm_final, v_w_s[0]), [g_small])

    def small(pack):
        nin, nv, bs, nf, ws = _unpack_small(pack, w_s.shape, b_s.shape)
        return nin, nv, ws, bs, nf

    loss = lax.psum(loss[0, 0], ("x", "y", "c"))
    out = []
    for small_pack, win, wog, wosb, wout in ((gs, gw_in, gw_og, gw_osb, gw_out), (ds, dw_in, dw_og, dw_osb, dw_out),
                                             (ms, nm_in, nm_og, nm_osb, nm_out), (vs, nv_in, nv_og, nv_osb, nv_out)):
        nin, nv, ws, bs, nf = small(small_pack)
        out += [nin, win[None], nv, ws, bs, wog[None], wosb[None], wout[None], nf]
    return (loss, grad_x, *out)
```

```python
import functools
import math

import jax
import jax.numpy as jnp
from jax import lax
from jax.experimental import pallas as pl
from jax.experimental.pallas import tpu as pltpu

F32 = jnp.float32
BF16 = jnp.bfloat16
EPS = 1e-6
HEADS = 8
N_SPLIT = 9
N_CHIP = 4
MESH = pl.DeviceIdType.MESH

ADAM_LR = 0.001
ADAM_B1 = 0.9
ADAM_B2 = 0.999
ADAM_EPS = 1e-08
ADAM_WD = 0.01
ADAM_STEP = 10

VMEM_LIMIT = 56 * 2 ** 20
TM = 256
ATT_T = 256
ATT_HP = 4
TKW = 512

NT = (((1,), (1,)), ((), ()))
TN = (((0,), (0,)), ((), ()))


def _params(sem):
    return pltpu.CompilerParams(dimension_semantics=sem, vmem_limit_bytes=VMEM_LIMIT)


def _resident(shape):
    nd = len(shape)
    return pl.BlockSpec(shape, lambda *_: (0,) * nd, pipeline_mode=pl.Buffered(1))


def _const(shape):
    nd = len(shape)
    return pl.BlockSpec(shape, lambda *_: (0,) * nd)


def _segments(d, ncol):
    segs = []
    edges = sorted({j * ncol for j in range(N_CHIP + 1)} | {s * d for s in range(N_SPLIT + 1)})
    for lo, hi in zip(edges[:-1], edges[1:]):
        segs.append((lo // ncol, lo % ncol, lo // d, lo % d, hi - lo))
    return segs


def _sigmoid(x):
    return 1.0 / (1.0 + jnp.exp(-x))


_GELU_C = math.sqrt(2.0 / math.pi)


def _gelu(x):
    return 0.5 * x * (1.0 + jnp.tanh(_GELU_C * (x + 0.044715 * (x * x * x))))


def _gelu_and_grad(x):
    x2 = x * x
    th = jnp.tanh(_GELU_C * (x + 0.044715 * (x2 * x)))
    val = 0.5 * x * (1.0 + th)
    grad = 0.5 * (1.0 + th) + 0.5 * x * (1.0 - th * th) * (_GELU_C * (1.0 + 3.0 * 0.044715 * x2))
    return val, grad


def _split_bf16(a):
    hi = a.astype(BF16)
    lo = (a - hi.astype(F32)).astype(BF16)
    return hi, lo


def _in_proj_fwd(x2d, g_in, wg):
    t, d = x2d.shape
    ncol = wg.shape[2]
    segs = _segments(d, ncol)

    def body(x_ref, g_ref, w_ref, apre_ref, qkv_ref, gpre_ref, ht_ref):
        x = x_ref[...]
        r = lax.rsqrt(jnp.mean(x * x, axis=-1, keepdims=True) + EPS)
        hf = x * r * g_ref[...]
        h = hf.astype(BF16)
        ht_ref[...] = hf.T.astype(BF16)
        outs = (apre_ref, qkv_ref, gpre_ref)
        for chip, c0, grp, s0, width in segs:
            res = jnp.dot(h, w_ref[chip, :, c0:c0 + width], preferred_element_type=F32)
            o = outs[grp // 3]
            off = (grp % 3) * d + s0
            o[:, off:off + width] = res.astype(o.dtype)

    return pl.pallas_call(
        body, name="in_proj_fwd",
        grid=(t // TM,),
        in_specs=[pl.BlockSpec((TM, d), lambda i: (i, 0)), _const((1, d)), _resident(wg.shape)],
        out_specs=[pl.BlockSpec((TM, 3 * d), lambda i: (i, 0)), pl.BlockSpec((TM, 3 * d), lambda i: (i, 0)),
                   pl.BlockSpec((TM, 3 * d), lambda i: (i, 0)), pl.BlockSpec((d, TM), lambda i: (0, i))],
        out_shape=[jax.ShapeDtypeStruct((t, 3 * d), F32), jax.ShapeDtypeStruct((t, 3 * d), BF16),
                   jax.ShapeDtypeStruct((t, 3 * d), F32), jax.ShapeDtypeStruct((d, t), BF16)],
        compiler_params=_params(("arbitrary",)),
    )(x2d, g_in, wg)


def _branch_a_fwd(a_pre, g_v, wm, b_t):
    t, d3 = a_pre.shape
    d = d3 // 3
    ng, chunk, _ = wm.shape
    cw = d // ng

    def body(a_ref, gv_ref, wm_ref, bt_ref, ya_ref):
        ua = _gelu(a_ref[:, 0:d])
        vg = _gelu(a_ref[:, d:2 * d])
        za = a_ref[:, 2 * d:3 * d]
        rv = lax.rsqrt(jnp.mean(vg * vg, axis=-1, keepdims=True) + EPS)
        va = (vg * rv * gv_ref[...]).astype(BF16)
        gate = ua * (za * _sigmoid(za))
        for g in range(ng):
            sl = slice(g * cw, (g + 1) * cw)
            mixed = jnp.dot(wm_ref[g], va[:, sl], preferred_element_type=F32) + bt_ref[:, g:g + 1]
            ya_ref[:, sl] = (gate[:, sl] * mixed).astype(BF16)

    return pl.pallas_call(
        body, name="branch_a_fwd",
        grid=(t // chunk,),
        in_specs=[pl.BlockSpec((chunk, d3), lambda i: (i, 0)), _const((1, d)), _const(wm.shape), _const(b_t.shape)],
        out_specs=pl.BlockSpec((chunk, d), lambda i: (i, 0)),
        out_shape=jax.ShapeDtypeStruct((t, d), BF16),
        compiler_params=_params(("arbitrary",)),
    )(a_pre, g_v, wm, b_t)


def _branch_a_bwd(a_pre, dya, g_v, wm, wm_t, b_t):
    t, d3 = a_pre.shape
    d = d3 // 3
    ng, chunk, _ = wm.shape
    cw = d // ng
    nsteps = t // chunk

    def body(a_ref, dya_ref, gv_ref, wm_ref, wmt_ref, bt_ref, da_ref, gws_ref, gbt_ref, gnv_ref, db_acc):
        i = pl.program_id(0)

        @pl.when(i == 0)
        def _():
            gws_ref[...] = jnp.zeros_like(gws_ref)
            gnv_ref[...] = jnp.zeros_like(gnv_ref)
            db_acc[...] = jnp.zeros_like(db_acc)

        ua, dgelu_u = _gelu_and_grad(a_ref[:, 0:d])
        vg, dgelu_v = _gelu_and_grad(a_ref[:, d:2 * d])
        za = a_ref[:, 2 * d:3 * d]
        sig = _sigmoid(za)
        sz = za * sig
        dsz = sig * (1.0 + za * (1.0 - sig))
        rv = lax.rsqrt(jnp.mean(vg * vg, axis=-1, keepdims=True) + EPS)
        nv = vg * rv
        gv = gv_ref[...]
        va = (nv * gv).astype(BF16)
        dya = dya_ref[...]
        dmix = dya * ua * sz
        db_acc[...] += dmix
        dmix_b = dmix.astype(BF16)
        t_gate = dya * sz
        t_z = dya * ua * dsz
        dva_parts = []
        for g in range(ng):
            sl = slice(g * cw, (g + 1) * cw)
            mixed = jnp.dot(wm_ref[g], va[:, sl], preferred_element_type=F32) + bt_ref[:, g:g + 1]
            da_ref[:, sl] = (t_gate[:, sl] * mixed * dgelu_u[:, sl]).astype(BF16)
            da_ref[:, 2 * d + g * cw:2 * d + (g + 1) * cw] = (t_z[:, sl] * mixed).astype(BF16)
            gws_ref[g] += lax.dot_general(dmix_b[:, sl], va[:, sl], NT, preferred_element_type=F32)
            dva_parts.append(jnp.dot(wmt_ref[g], dmix_b[:, sl], preferred_element_type=F32))
        dva = jnp.concatenate(dva_parts, axis=1)
        gnv_ref[...] += jnp.sum(dva * nv, axis=0, keepdims=True)
        dnv = dva * gv
        dvg = rv * (dnv - nv * jnp.mean(dnv * nv, axis=-1, keepdims=True))
        da_ref[:, d:2 * d] = (dvg * dgelu_v).astype(BF16)

        @pl.when(i == nsteps - 1)
        def _():
            acc = db_acc[...]
            for g in range(ng):
                gbt_ref[:, g:g + 1] = jnp.sum(acc[:, g * cw:(g + 1) * cw], axis=1, keepdims=True)

    return pl.pallas_call(
        body, name="branch_a_bwd",
        grid=(nsteps,),
        in_specs=[pl.BlockSpec((chunk, d3), lambda i: (i, 0)), pl.BlockSpec((chunk, d), lambda i: (i, 0)),
                  _const((1, d)), _const(wm.shape), _const(wm_t.shape), _const(b_t.shape)],
        out_specs=[pl.BlockSpec((chunk, d3), lambda i: (i, 0)), _const(wm.shape), _const(b_t.shape), _const((1, d))],
        out_shape=[jax.ShapeDtypeStruct((t, d3), BF16), jax.ShapeDtypeStruct(wm.shape, F32),
                   jax.ShapeDtypeStruct(b_t.shape, F32), jax.ShapeDtypeStruct((1, d), F32)],
        scratch_shapes=[pltpu.VMEM((chunk, d), F32)],
        compiler_params=_params(("arbitrary",)),
    )(a_pre, dya, g_v, wm, wm_t, b_t)


def _tri(n, rows_gt_cols):
    r = lax.broadcasted_iota(jnp.int32, (n, n), 0)
    c = lax.broadcasted_iota(jnp.int32, (n, n), 1)
    return (r > c) if rows_gt_cols else (r < c)


def _twice(tri):
    t = tri.astype(BF16)
    return jnp.concatenate([t, t], axis=0)


def _cumsum_mm(a, tri2):
    hi, lo = _split_bf16(a)
    return jnp.dot(jnp.concatenate([hi, lo], axis=1), tri2, preferred_element_type=F32)


LOG2E = 1.4426950408889634
_SIGN = 0x80000000


def _sb_block(q, k, scale, upper2, causal):
    z2 = lax.dot_general(q, k, NT, preferred_element_type=F32) * (scale * LOG2E)
    neg_abs = lax.bitcast_convert_type(lax.bitcast_convert_type(z2, jnp.uint32) | jnp.uint32(_SIGN), F32)
    l2 = jnp.log(1.0 + jnp.exp2(neg_abs)) * LOG2E
    log_beta = jnp.minimum(z2, 0.0) - l2
    lom = log_beta - z2
    if causal is not None:
        lom = jnp.where(causal, lom, 0.0)
    sx = _cumsum_mm(lom, upper2)
    return log_beta, sx, sx[:, 0:1] + lom[:, 0:1]


def _attn_specs(d, seq, nq):
    hp_w = ATT_HP * (d // HEADS)
    n_hp = d // hp_w
    row_blk = pl.BlockSpec((ATT_T, hp_w), lambda b, h, i: (b * nq + i, h))
    k_blk = pl.BlockSpec((seq, hp_w), lambda b, h, i: (b, n_hp + h))
    v_blk = pl.BlockSpec((seq, hp_w), lambda b, h, i: (b, 2 * n_hp + h))
    seq_blk = pl.BlockSpec((seq, hp_w), lambda b, h, i: (b, h))
    return row_blk, k_blk, v_blk, seq_blk, n_hp


def _attn_fwd(qkv, g_pre, bsz, seq):
    t, d3 = qkv.shape
    d = d3 // 3
    hd = d // HEADS
    nq = seq // ATT_T
    scale = hd ** -0.5
    row_blk, k_blk, v_blk, _, n_hp = _attn_specs(d, seq, nq)

    def body(q_ref, k_ref, v_ref, zb_ref, o_ref, yb_ref):
        i = pl.program_id(2)
        causal = _tri(ATT_T, True)
        upper2 = _twice(causal)

        def step(kb, state, mask):
            rows = pl.ds(pl.multiple_of(kb * ATT_T, ATT_T), ATT_T)
            heads = [slice(h * hd, (h + 1) * hd) for h in range(ATT_HP)]
            scores = [_sb_block(q_ref[:, cols], k_ref[rows, cols], scale, upper2, mask) for cols in heads]
            new = []
            for cols, (carry, acc), (log_beta, sx, total) in zip(heads, state, scores):
                a = jnp.exp2(log_beta + sx + carry)
                if mask is not None:
                    a = jnp.where(mask, a, 0.0)
                acc = acc + jnp.dot(a.astype(BF16), v_ref[rows, cols], preferred_element_type=F32)
                new.append((carry + total, acc))
            return tuple(new)

        init = tuple((jnp.zeros((ATT_T, 1), F32), jnp.zeros((ATT_T, hd), F32)) for _ in range(ATT_HP))
        state = step(i, init, causal)
        state = lax.fori_loop(0, i, lambda n, s: step(i - 1 - n, s, None), state)
        for h in range(ATT_HP):
            cols = slice(h * hd, (h + 1) * hd)
            acc = state[h][1]
            zb = zb_ref[:, cols]
            o_ref[:, cols] = acc
            yb_ref[:, cols] = (acc * (zb * _sigmoid(zb))).astype(BF16)

    return pl.pallas_call(
        body, name="attn_fwd",
        grid=(bsz, n_hp, nq),
        in_specs=[row_blk, k_blk, v_blk, row_blk],
        out_specs=[row_blk, row_blk],
        out_shape=[jax.ShapeDtypeStruct((t, d), F32), jax.ShapeDtypeStruct((t, d), BF16)],
        compiler_params=_params(("arbitrary", "arbitrary", "arbitrary")),
    )(qkv, qkv, qkv, g_pre)


def _attn_bwd(qkv, g_pre, o, dyb, bsz, seq):
    t, d3 = qkv.shape
    d = d3 // 3
    hd = d // HEADS
    nq = seq // ATT_T
    scale = hd ** -0.5
    row_blk, k_blk, v_blk, seq_blk, n_hp = _attn_specs(d, seq, nq)

    def body(q_ref, k_ref, v_ref, zb_ref, o_ref, dyb_ref, dq_ref, dk_ref, dv_ref, dzb_ref,
             g_s, beta_s, dkt_acc, dvt_acc):
        i = pl.program_id(2)

        @pl.when(i == 0)
        def _():
            dkt_acc[...] = jnp.zeros_like(dkt_acc)
            dvt_acc[...] = jnp.zeros_like(dvt_acc)

        causal = _tri(ATT_T, True)
        upper2 = _twice(causal)
        lower2 = _twice(~causal)
        zb = zb_ref[...]
        sig = _sigmoid(zb)
        dyb_t = dyb_ref[...]
        do_f = dyb_t * (zb * sig)
        do = do_f.astype(BF16)
        do_t = do_f.T.astype(BF16)
        q_t = q_ref[...].astype(F32).T.astype(BF16)
        dzb_ref[...] = (dyb_t * o_ref[...] * (sig * (1.0 + zb * (1.0 - sig)))).astype(BF16)

        def sweep(kb, carries, mask):
            rows = pl.ds(pl.multiple_of(kb * ATT_T, ATT_T), ATT_T)
            heads = [slice(h * hd, (h + 1) * hd) for h in range(ATT_HP)]
            scores = [_sb_block(q_ref[:, cols], k_ref[rows, cols], scale, upper2, mask) for cols in heads]
            das = [lax.dot_general(do[:, cols], v_ref[rows, cols], NT, preferred_element_type=F32) for cols in heads]
            new = []
            for h, (cols, carry, (log_beta, sx, total), da) in enumerate(zip(heads, carries, scores, das)):
                a = jnp.exp2(log_beta + sx + carry)
                beta = jnp.exp2(log_beta)
                if mask is not None:
                    a = jnp.where(mask, a, 0.0)
                    beta = jnp.where(mask, beta, 0.0)
                g_s[h, kb] = a * da
                beta_s[h, kb] = beta
                dvt_acc[kb, cols, :] += jnp.dot(do_t[cols, :], a.astype(BF16), preferred_element_type=F32)
                new.append(carry + total)
            return tuple(new)

        carries = sweep(i, tuple(jnp.zeros((ATT_T, 1), F32) for _ in range(ATT_HP)), causal)
        lax.fori_loop(0, i, lambda n, c: sweep(i - 1 - n, c, None), carries)

        def back(kb, state):
            rows = pl.ds(pl.multiple_of(kb * ATT_T, ATT_T), ATT_T)
            heads = [slice(h * hd, (h + 1) * hd) for h in range(ATT_HP)]
            sums = [_cumsum_mm(g_s[h, kb], lower2) for h in range(ATT_HP)]
            new = []
            for h, (cols, (p_carry, dq), px) in enumerate(zip(heads, state, sums)):
                dz = ((g_s[h, kb] - (p_carry + px) * beta_s[h, kb]) * scale).astype(BF16)
                dq = dq + jnp.dot(dz, k_ref[rows, cols], preferred_element_type=F32)
                dkt_acc[kb, cols, :] += jnp.dot(q_t[cols, :], dz, preferred_element_type=F32)
                new.append((p_carry + px[:, ATT_T - 1:ATT_T], dq))
            return tuple(new)

        init = tuple((jnp.zeros((ATT_T, 1), F32), jnp.zeros((ATT_T, hd), F32)) for _ in range(ATT_HP))
        state = lax.fori_loop(0, i + 1, back, init)
        for h in range(ATT_HP):
            dq_ref[:, h * hd:(h + 1) * hd] = state[h][1].astype(BF16)

        @pl.when(i == nq - 1)
        def _():
            for kb in range(nq):
                dk_ref[kb * ATT_T:(kb + 1) * ATT_T, :] = dkt_acc[kb].T.astype(BF16)
                dv_ref[kb * ATT_T:(kb + 1) * ATT_T, :] = dvt_acc[kb].T.astype(BF16)

    out = jax.ShapeDtypeStruct((t, d), BF16)
    hp_w = ATT_HP * hd
    return pl.pallas_call(
        body, name="attn_bwd",
        grid=(bsz, n_hp, nq),
        in_specs=[row_blk, k_blk, v_blk, row_blk, row_blk, row_blk],
        out_specs=[row_blk, seq_blk, seq_blk, row_blk],
        out_shape=[out, out, out, out],
        scratch_shapes=[pltpu.VMEM((ATT_HP, nq, ATT_T, ATT_T), F32), pltpu.VMEM((ATT_HP, nq, ATT_T, ATT_T), F32),
                        pltpu.VMEM((nq, hp_w, ATT_T), F32), pltpu.VMEM((nq, hp_w, ATT_T), F32)],
        compiler_params=_params(("arbitrary", "arbitrary", "arbitrary")),
    )(qkv, qkv, qkv, g_pre, o, dyb)


def _out_proj(ya, yb, g_pre, x2d, tgt, w_og, w_osb, w_out, g_f):
    t, d = x2d.shape

    def body(ya_ref, yb_ref, ga_ref, gb_ref, x_ref, tgt_ref, wog_ref, wosb_ref, wout_ref, gf_ref,
             dya_ref, dyb_ref, dg_ref, dx2_ref, loss_ref, gnf_ref, gwog_ref, gwosb_ref, gwout_ref):
        @pl.when(pl.program_id(0) == 0)
        def _():
            loss_ref[...] = jnp.zeros_like(loss_ref)
            gnf_ref[...] = jnp.zeros_like(gnf_ref)
            gwog_ref[...] = jnp.zeros_like(gwog_ref)
            gwosb_ref[...] = jnp.zeros_like(gwosb_ref)
            gwout_ref[...] = jnp.zeros_like(gwout_ref)

        ya = ya_ref[...]
        yb = yb_ref[...]
        pa = jnp.dot(ya, wog_ref[...], preferred_element_type=F32)
        pb = jnp.dot(yb, wosb_ref[...], preferred_element_type=F32)
        sga = _sigmoid(ga_ref[...])
        sgb = _sigmoid(gb_ref[...])
        merged = (sga * pa + sgb * pb).astype(BF16)
        x2 = x_ref[...] + jnp.dot(merged, wout_ref[...], preferred_element_type=F32)
        r2 = lax.rsqrt(jnp.mean(x2 * x2, axis=-1, keepdims=True) + EPS)
        n2 = x2 * r2
        gf = gf_ref[...]
        err = n2 * gf - tgt_ref[...]
        loss_ref[...] += 0.5 * jnp.sum(jnp.sum(err * err, axis=-1, keepdims=True), axis=0, keepdims=True) / d
        dy = err * (1.0 / d)
        gnf_ref[...] += jnp.sum(dy * n2, axis=0, keepdims=True)
        dn = dy * gf
        dx2 = r2 * (dn - n2 * jnp.mean(dn * n2, axis=-1, keepdims=True))
        dx2_ref[...] = dx2
        dx2_b = dx2.astype(BF16)
        dmerged = lax.dot_general(dx2_b, wout_ref[...], NT, preferred_element_type=F32)
        gwout_ref[...] += lax.dot_general(merged, dx2_b, TN, preferred_element_type=F32)
        dg_ref[:, 0:d] = (dmerged * pa * (sga * (1.0 - sga))).astype(BF16)
        dg_ref[:, d:2 * d] = (dmerged * pb * (sgb * (1.0 - sgb))).astype(BF16)
        dpa = (dmerged * sga).astype(BF16)
        dpb = (dmerged * sgb).astype(BF16)
        dya_ref[...] = lax.dot_general(dpa, wog_ref[...], NT, preferred_element_type=F32)
        dyb_ref[...] = lax.dot_general(dpb, wosb_ref[...], NT, preferred_element_type=F32)
        gwog_ref[...] += lax.dot_general(ya, dpa, TN, preferred_element_type=F32)
        gwosb_ref[...] += lax.dot_general(yb, dpb, TN, preferred_element_type=F32)

    row = lambda i: (i, 0)
    return pl.pallas_call(
        body, name="out_proj",
        grid=(t // TM,),
        in_specs=[pl.BlockSpec((TM, d), row), pl.BlockSpec((TM, d), row),
                  pl.BlockSpec((TM, d), lambda i: (i, 1)), pl.BlockSpec((TM, d), lambda i: (i, 2)),
                  pl.BlockSpec((TM, d), row), pl.BlockSpec((TM, d), row),
                  _resident((d, d)), _resident((d, d)), _resident((d, d)), _const((1, d))],
        out_specs=[pl.BlockSpec((TM, d), row), pl.BlockSpec((TM, d), row), pl.BlockSpec((TM, 2 * d), row),
                   pl.BlockSpec((TM, d), row), _const((1, 1)), _const((1, d)),
                   _const((d, d)), _const((d, d)), _const((d, d))],
        out_shape=[jax.ShapeDtypeStruct((t, d), F32), jax.ShapeDtypeStruct((t, d), F32),
                   jax.ShapeDtypeStruct((t, 2 * d), BF16), jax.ShapeDtypeStruct((t, d), F32),
                   jax.ShapeDtypeStruct((1, 1), F32), jax.ShapeDtypeStruct((1, d), F32),
                   jax.ShapeDtypeStruct((d, d), F32), jax.ShapeDtypeStruct((d, d), F32),
                   jax.ShapeDtypeStruct((d, d), F32)],
        compiler_params=_params(("arbitrary",)),
    )(ya, yb, g_pre, g_pre, x2d, tgt, w_og, w_osb, w_out, g_f)


def _dproj_pieces(d):
    return [(0, 0, 3), (1, 3, 1), (2, 4, 1), (3, 5, 1), (4, 6, 1), (5, 7, 2)]


def _in_proj_bwd_x(pieces, wg, x2d, g_in, dx2, gw16):
    t, d = x2d.shape
    ncol = wg.shape[2]
    segs = _segments(d, ncol)
    layout = _dproj_pieces(d)
    nsteps = t // TM

    def body(da_ref, dq_ref, dk_ref, dv_ref, dzb_ref, dg_ref, w_ref, x_ref, g_ref, dx2_ref, gw16_ref,
             gx_ref, gn_ref, recv_ref, send_sems, recv_sems):
        x_pos, y_pos, c_pos, chips = _place()

        def share(k, chunk):
            px, py = chips[k]
            return pltpu.make_async_remote_copy(
                src_ref=gw16_ref.at[:, chunk * ncol:(chunk + 1) * ncol], dst_ref=recv_ref.at[k],
                send_sem=send_sems.at[k], recv_sem=recv_sems.at[k], device_id=(px, py, c_pos), device_id_type=MESH)

        @pl.when(pl.program_id(0) == 0)
        def _():
            gn_ref[...] = jnp.zeros_like(gn_ref)
            for k, (px, py) in enumerate(chips):
                for chunk in range(N_CHIP):
                    @pl.when(2 * px + py == chunk)
                    def _(k=k, chunk=chunk):
                        share(k, chunk).start()

        @pl.when(pl.program_id(0) == nsteps - 1)
        def _():
            for k in range(N_CHIP - 1):
                share(k, 0).wait()

        refs = (da_ref, dq_ref, dk_ref, dv_ref, dzb_ref, dg_ref)
        dh = jnp.zeros((TM, d), F32)
        for chip, c0, grp, s0, width in segs:
            piece, first, _ = next(p for p in layout if p[1] <= grp < p[1] + p[2])
            off = (grp - first) * d + s0
            dh = dh + lax.dot_general(refs[piece][:, off:off + width], w_ref[chip, :, c0:c0 + width], NT,
                                      preferred_element_type=F32)
        x = x_ref[...]
        r = lax.rsqrt(jnp.mean(x * x, axis=-1, keepdims=True) + EPS)
        n = x * r
        gn_ref[...] += jnp.sum(dh * n, axis=0, keepdims=True)
        dn = dh * g_ref[...]
        gx_ref[...] = dx2_ref[...] + r * (dn - n * jnp.mean(dn * n, axis=-1, keepdims=True))

    row = lambda i: (i, 0)
    return pl.pallas_call(
        body, name="in_proj_bwd_x",
        grid=(t // TM,),
        in_specs=[pl.BlockSpec((TM, p.shape[1]), row) for p in pieces]
        + [_resident(wg.shape), pl.BlockSpec((TM, d), row), _const((1, d)), pl.BlockSpec((TM, d), row), ANY],
        out_specs=[pl.BlockSpec((TM, d), row), _const((1, d)), ANY],
        out_shape=[jax.ShapeDtypeStruct((t, d), F32), jax.ShapeDtypeStruct((1, d), F32),
                   jax.ShapeDtypeStruct((N_CHIP - 1, d, ncol), BF16)],
        scratch_shapes=[pltpu.SemaphoreType.DMA((N_CHIP - 1,)), pltpu.SemaphoreType.DMA((N_CHIP - 1,))],
        compiler_params=_params(("arbitrary",)),
    )(*pieces, wg, x2d, g_in, dx2, gw16)


def _in_proj_bwd_w(h_t, pieces, mats16, pack):
    d, t = h_t.shape
    nk = t // TKW
    layout = _dproj_pieces(d)
    n_mats = len(mats16)
    n_dev = 2 * N_CHIP
    flips = [(dx, dy, dc) for dx in (0, 1) for dy in (0, 1) for dc in (0, 1)][1:]

    def body(ht_ref, da_ref, dq_ref, dk_ref, dv_ref, dzb_ref, dg_ref, *rest):
        mat_refs, pack_ref = rest[:n_mats], rest[n_mats]
        gw_ref, gw16_ref = rest[n_mats + 1:n_mats + 3]
        recv_refs, slots_ref = rest[n_mats + 3:2 * n_mats + 3], rest[2 * n_mats + 3]
        acc, mat_send, mat_recv, pack_send, pack_recv, own_sem = rest[2 * n_mats + 4:]
        s = pl.program_id(0)
        i = pl.program_id(1)
        x_pos, y_pos, c_pos, chips = _place()
        me = 4 * x_pos + 2 * y_pos + c_pos

        def exchanges():
            cps = []
            for k, (px, py) in enumerate(chips):
                for a in range(n_mats):
                    cps.append(pltpu.make_async_remote_copy(
                        src_ref=mat_refs[a].at[2 * px + py], dst_ref=recv_refs[a].at[k],
                        send_sem=mat_send.at[a, k], recv_sem=mat_recv.at[a, k],
                        device_id=(px, py, c_pos), device_id_type=MESH))
            for k, (dx, dy, dc) in enumerate(flips):
                peer = (1 - x_pos if dx else x_pos, 1 - y_pos if dy else y_pos, 1 - c_pos if dc else c_pos)
                cps.append(pltpu.make_async_remote_copy(
                    src_ref=pack_ref, dst_ref=slots_ref.at[me], send_sem=pack_send.at[k], recv_sem=pack_recv.at[k],
                    device_id=peer, device_id_type=MESH))
            return cps, pltpu.make_async_copy(pack_ref, slots_ref.at[me], own_sem)

        @pl.when((s == 0) & (i == 0))
        def _():
            cps, own = exchanges()
            own.start()
            for cp in cps:
                cp.start()

        @pl.when((s == N_SPLIT - 1) & (i == nk - 1))
        def _():
            cps, own = exchanges()
            own.wait()
            for cp in cps:
                cp.wait()

        @pl.when(i == 0)
        def _():
            acc[...] = jnp.zeros_like(acc)

        refs = (da_ref, dq_ref, dk_ref, dv_ref, dzb_ref, dg_ref)
        for piece, first, count in layout:
            for j in range(count):
                @pl.when(s == first + j)
                def _(piece=piece, j=j):
                    acc[...] += jnp.dot(ht_ref[...], refs[piece][:, j * d:(j + 1) * d], preferred_element_type=F32)

        @pl.when(i == nk - 1)
        def _():
            gw_ref[...] = acc[...]
            gw16_ref[...] = acc[...].astype(BF16)

    def piece_spec(p, first, count):
        width = count * d
        return pl.BlockSpec((TKW, width), lambda s, i: (jnp.where((s >= first) & (s < first + count), i, 0), 0))

    col_blk = pl.BlockSpec((d, d), lambda s, i: (0, s))
    outs = pl.pallas_call(
        body, name="in_proj_bwd_w",
        grid=(N_SPLIT, nk),
        in_specs=[pl.BlockSpec((d, TKW), lambda s, i: (0, i))] + [piece_spec(*p) for p in layout] + [ANY] * (n_mats + 1),
        out_specs=[col_blk, col_blk] + [ANY] * (n_mats + 1),
        out_shape=[jax.ShapeDtypeStruct((d, N_SPLIT * d), F32), jax.ShapeDtypeStruct((d, N_SPLIT * d), BF16)]
        + [jax.ShapeDtypeStruct((N_CHIP - 1,) + m.shape[1:], BF16) for m in mats16]
        + [jax.ShapeDtypeStruct((n_dev,) + pack.shape, F32)],
        scratch_shapes=[pltpu.VMEM((d, d), F32),
                        pltpu.SemaphoreType.DMA((n_mats, N_CHIP - 1)), pltpu.SemaphoreType.DMA((n_mats, N_CHIP - 1)),
                        pltpu.SemaphoreType.DMA((n_dev - 1,)), pltpu.SemaphoreType.DMA((n_dev - 1,)),
                        pltpu.SemaphoreType.DMA],
        compiler_params=_params(("arbitrary", "arbitrary")),
    )(h_t, *pieces, *mats16, pack)
    return outs[0], outs[1], outs[2:2 + n_mats], outs[2 + n_mats]


def _local_step(x2d, tgt2d, bsz, seq, norm_in, wg, norm_v, w_s, b_s, w_og, w_osb, w_out, norm_final):
    d = x2d.shape[1]
    chunk = w_s.shape[-1]
    causal = jnp.tril(jnp.ones((chunk, chunk), dtype=bool))
    wm = jnp.where(causal[None], w_s, 0.0).astype(BF16)
    wm_t = jnp.swapaxes(wm, 1, 2)
    b_t = b_s.T

    a_pre, qkv, g_pre, h_t = _in_proj_fwd(x2d, norm_in, wg)
    ya = _branch_a_fwd(a_pre, norm_v, wm, b_t)
    o, yb = _attn_fwd(qkv, g_pre, bsz, seq)
    dya, dyb, dg, dx2, loss, g_nf, g_wog, g_wosb, g_wout = _out_proj(
        ya, yb, g_pre, x2d, tgt2d, w_og, w_osb, w_out, norm_final.reshape(1, d))
    dq, dk, dv, dzb = _attn_bwd(qkv, g_pre, o, dyb, bsz, seq)
    d_a, g_ws, g_bt, g_nv = _branch_a_bwd(a_pre, dya, norm_v, wm, wm_t, b_t)
    g_ws = jnp.where(causal[None], g_ws, 0.0)
    return loss, h_t, (d_a, dq, dk, dv, dzb, dg), dx2, g_nv, g_ws, g_bt.T, g_wog, g_wosb, g_wout, g_nf


def _row_tile(rows):
    return next(r for r in (128, 64, 32, 16, 8) if rows % r == 0)


def _cast_bf16(a):
    rows, cols = a.shape
    tr = _row_tile(rows)

    def body(a_ref, o_ref):
        o_ref[...] = a_ref[...].astype(BF16)

    return pl.pallas_call(
        body, name="cast_bf16", grid=(rows // tr,),
        in_specs=[pl.BlockSpec((tr, cols), lambda i: (i, 0))],
        out_specs=pl.BlockSpec((tr, cols), lambda i: (i, 0)),
        out_shape=jax.ShapeDtypeStruct(a.shape, BF16),
        compiler_params=_params(("arbitrary",)),
    )(a)


def _add_received(full, recv, chip, by_cols):
    _, rows, cols = recv.shape
    tr = _row_tile(rows)
    nb = rows // tr

    def body(chip_ref, own_ref, recv_ref, o_ref):
        s = own_ref[...]
        for k in range(N_CHIP - 1):
            s = s + recv_ref[k].astype(F32)
        o_ref[...] = s

    own_map = (lambda i, chip_ref: (i, chip_ref[0])) if by_cols else (lambda i, chip_ref: (chip_ref[0] * nb + i, 0))
    return pl.pallas_call(
        body, name="add_received",
        grid_spec=pltpu.PrefetchScalarGridSpec(
            num_scalar_prefetch=1, grid=(nb,),
            in_specs=[pl.BlockSpec((tr, cols), own_map),
                      pl.BlockSpec((N_CHIP - 1, tr, cols), lambda i, chip_ref: (0, i, 0))],
            out_specs=pl.BlockSpec((tr, cols), lambda i, chip_ref: (i, 0))),
        out_shape=jax.ShapeDtypeStruct((rows, cols), F32),
        compiler_params=_params(("arbitrary",)),
    )(chip.reshape(1).astype(jnp.int32), full, recv)


def _adamw_math(w, m, v, g):
    new_m = ADAM_B1 * m + (1.0 - ADAM_B1) * g
    new_v = ADAM_B2 * v + (1.0 - ADAM_B2) * (g * g)
    m_hat = new_m / (1.0 - ADAM_B1 ** ADAM_STEP)
    v_hat = new_v / (1.0 - ADAM_B2 ** ADAM_STEP)
    return -ADAM_LR * (m_hat / (jnp.sqrt(v_hat) + ADAM_EPS) + ADAM_WD * w), new_m, new_v


def _adamw(w, m, v, g_parts):
    rows, cols = w.shape
    tr = _row_tile(rows)
    n_parts = len(g_parts)

    def body(*refs):
        w_ref, m_ref, v_ref = refs[:3]
        part_refs = refs[3:3 + n_parts]
        g_ref, d_ref, nm_ref, nv_ref = refs[3 + n_parts:]
        g = part_refs[0][...]
        for p in part_refs[1:]:
            g = g + p[...]
        g_ref[...] = g
        d_ref[...], nm_ref[...], nv_ref[...] = _adamw_math(w_ref[...], m_ref[...], v_ref[...], g)

    spec = pl.BlockSpec((tr, cols), lambda i: (i, 0))
    out = jax.ShapeDtypeStruct(w.shape, F32)
    return pl.pallas_call(
        body, name="adamw", grid=(rows // tr,),
        in_specs=[spec] * (3 + n_parts), out_specs=[spec] * 4, out_shape=[out] * 4,
        compiler_params=_params(("arbitrary",)),
    )(w, m, v, *g_parts)


def _adamw_small(w, m, v, slots_head, slots_tail):
    n_dev, p0, _ = slots_head.shape

    def body(w_ref, m_ref, v_ref, head_ref, tail_ref, g_ref, d_ref, nm_ref, nv_ref):
        for ref, rows in ((head_ref, slice(0, p0)), (tail_ref, slice(p0, w.shape[0]))):
            g = ref[0]
            for i in range(1, n_dev):
                g = g + ref[i]
            g_ref[rows, :] = g
            d_ref[rows, :], nm_ref[rows, :], nv_ref[rows, :] = _adamw_math(w_ref[rows, :], m_ref[rows, :], v_ref[rows, :], g)

    vmem = pl.BlockSpec(memory_space=pltpu.VMEM)
    out = jax.ShapeDtypeStruct(w.shape, F32)
    return pl.pallas_call(
        body, name="adamw_small", in_specs=[vmem] * 5, out_specs=[vmem] * 4, out_shape=[out] * 4,
        compiler_params=pltpu.CompilerParams(vmem_limit_bytes=VMEM_LIMIT),
    )(w, m, v, slots_head, slots_tail)


ANY = pl.BlockSpec(memory_space=pl.ANY)


def _place():
    x, y, c = lax.axis_index("x"), lax.axis_index("y"), lax.axis_index("c")
    other_chips = [(1 - x, y), (x, 1 - y), (1 - x, 1 - y)]
    return x, y, c, other_chips


def _all_gather_halves(halves):
    n = len(halves)

    def body(*refs):
        ins, outs = refs[:n], refs[n:2 * n]
        send_sems, recv_sems, local_sems = refs[2 * n:]
        x, y, c, chips = _place()
        sibling = (x, y, 1 - c)
        my_chip = 2 * x + y

        def copy(a, k, block, to, src=None):
            return pltpu.make_async_remote_copy(
                src_ref=outs[a].at[block] if src is None else src, dst_ref=outs[a].at[block],
                send_sem=send_sems.at[a, k], recv_sem=recv_sems.at[a, k], device_id=to, device_id_type=MESH)

        local = [pltpu.make_async_copy(ins[a], outs[a].at[pl.ds(2 * my_chip, 2)], local_sems.at[a]) for a in range(n)]
        for cp in local:
            cp.start()
        started = []
        for k, (px, py) in enumerate(chips):
            for a in range(n):
                cp = copy(a, k, 2 * my_chip + c, (px, py, c), src=ins[a].at[c])
                cp.start()
                started.append(cp)
        for k, (px, py) in enumerate(chips):
            for a in range(n):
                block = 2 * (2 * px + py) + c
                copy(a, k, block, (x, y, c)).wait_recv()
                cp = copy(a, 3 + k, block, sibling)
                cp.start()
                started.append(cp)
        for k, (px, py) in enumerate(chips):
            for a in range(n):
                copy(a, 3 + k, 2 * (2 * px + py) + 1 - c, (x, y, c)).wait_recv()
        for cp in started:
            cp.wait_send()
        for cp in local:
            cp.wait()

    return pl.pallas_call(
        body, name="all_gather_weights",
        in_specs=[ANY] * n, out_specs=[ANY] * n,
        out_shape=[jax.ShapeDtypeStruct((2 * N_CHIP,) + h.shape[1:], h.dtype) for h in halves],
        scratch_shapes=[pltpu.SemaphoreType.DMA((n, 6)), pltpu.SemaphoreType.DMA((n, 6)), pltpu.SemaphoreType.DMA((n,))],
    )(*halves)


def _swap_and_gather(arrs, pack):
    n = len(arrs)
    n_dev = 2 * N_CHIP
    flips = [(dx, dy, dc) for dx in (0, 1) for dy in (0, 1) for dc in (0, 1)][1:]

    def body(*refs):
        ins, pack_ref = refs[:n], refs[n]
        outs, slots_ref = refs[n + 1:2 * n + 1], refs[2 * n + 1]
        send_sems, recv_sems, pack_send, pack_recv, own_sem = refs[2 * n + 2:]
        x, y, c, _ = _place()
        me = 4 * x + 2 * y + c
        own = pltpu.make_async_copy(pack_ref, slots_ref.at[me], own_sem)
        own.start()
        copies = []
        for k, (dx, dy, dc) in enumerate(flips):
            peer = (1 - x if dx else x, 1 - y if dy else y, 1 - c if dc else c)
            copies.append(pltpu.make_async_remote_copy(
                src_ref=pack_ref, dst_ref=slots_ref.at[me], send_sem=pack_send.at[k], recv_sem=pack_recv.at[k],
                device_id=peer, device_id_type=MESH))
        copies += [pltpu.make_async_remote_copy(
            src_ref=ins[a], dst_ref=outs[a], send_sem=send_sems.at[a], recv_sem=recv_sems.at[a],
            device_id=(x, y, 1 - c), device_id_type=MESH) for a in range(n)]
        for cp in copies:
            cp.start()
        for cp in copies:
            cp.wait()
        own.wait()

    return pl.pallas_call(
        body, name="swap_and_gather",
        in_specs=[ANY] * (n + 1), out_specs=[ANY] * (n + 1),
        out_shape=[jax.ShapeDtypeStruct(a.shape, a.dtype) for a in arrs] + [jax.ShapeDtypeStruct((n_dev,) + pack.shape, F32)],
        scratch_shapes=[pltpu.SemaphoreType.DMA((n,)), pltpu.SemaphoreType.DMA((n,)),
                        pltpu.SemaphoreType.DMA((n_dev - 1,)), pltpu.SemaphoreType.DMA((n_dev - 1,)),
                        pltpu.SemaphoreType.DMA],
    )(*arrs, pack)


SLAB = 8


def _slab(vec, d):
    return jnp.pad(vec.reshape(1, d), ((0, SLAB - 1), (0, 0)))


def _pack_tail(vec_nv, b_s, vec_nf, w_s):
    d = vec_nv.shape[-1]
    return jnp.concatenate([_slab(v, d) for v in (vec_nv, b_s, vec_nf)] + [w_s.reshape(-1, d)], axis=0)


def _pack_small(vec_nin, vec_nv, b_s, vec_nf, w_s):
    return jnp.concatenate([_slab(vec_nin, vec_nin.shape[-1]), _pack_tail(vec_nv, b_s, vec_nf, w_s)], axis=0)


def _unpack_small(pack, w_s_shape, b_s_shape):
    return (pack[0:1], pack[SLAB:SLAB + 1], pack[2 * SLAB].reshape(b_s_shape), pack[3 * SLAB],
            pack[4 * SLAB:].reshape(w_s_shape))


def kernel(x, norm_in, w_in, norm_v, w_s, b_s, w_o_gmlp, w_o_sb, w_out, norm_final, loss_target, m_norm_in, m_w_in, m_norm_v, m_w_s, m_b_s, m_w_o_gmlp, m_w_o_sb, m_w_out, m_norm_final, v_norm_in, v_w_in, v_norm_v, v_w_s, v_b_s, v_w_o_gmlp, v_w_o_sb, v_w_out, v_norm_final):
    d = x.shape[-1]
    ncol = w_in.shape[-1]
    nrow = w_o_gmlp.shape[-2]
    chip = 2 * lax.axis_index("x") + lax.axis_index("y")

    shards = [w_in[0], w_o_gmlp[0], w_o_sb[0], w_out[0]]
    halves = [_cast_bf16(s).reshape(2, s.shape[0] // 2, s.shape[1]) for s in shards]
    wg, w_og, w_osb, w_o = _all_gather_halves(halves)
    wg = wg.reshape(N_CHIP, d, ncol)

    bsz, seq, _ = x.shape
    x2d = x.reshape(bsz * seq, d)
    loss, h_t, pieces, dx2, g_nv, g_ws, g_bs, g_wog, g_wosb, g_wout, g_nf = _local_step(
        x2d, loss_target.reshape(bsz * seq, d), bsz, seq, norm_in, wg, norm_v, w_s[0], b_s[0],
        w_og.reshape(d, d), w_osb.reshape(d, d), w_o.reshape(d, d), norm_final)

    mats = [g_wog, g_wosb, g_wout]
    mats16 = [_cast_bf16(g).reshape(N_CHIP, nrow, d) for g in mats]
    g_win, g_win16, recv_mats, slots_tail = _in_proj_bwd_w(h_t, pieces, mats16, _pack_tail(g_nv, g_bs, g_nf, g_ws))
    grad_x, g_nin, recv_win = _in_proj_bwd_x(pieces, wg, x2d, norm_in, dx2, g_win16)
    grad_x = grad_x.reshape(bsz, seq, d)

    sums = [_add_received(g_win, recv_win, chip, True)] + [
        _add_received(g, r, chip, False) for g, r in zip(mats, recv_mats)]
    *sibling_sums, slots_head = _swap_and_gather(sums, _slab(g_nin, d))
    stats = []
    for w, m, v, mine, theirs in zip(shards, [m_w_in[0], m_w_o_gmlp[0], m_w_o_sb[0], m_w_out[0]],
                                     [v_w_in[0], v_w_o_gmlp[0], v_w_o_sb[0], v_w_out[0]], sums, sibling_sums):
        stats.append(_adamw(w, m, v, [mine, theirs]))
    (gw_in, dw_in, nm_in, nv_in), (gw_og, dw_og, nm_og, nv_og), (gw_osb, dw_osb, nm_osb, nv_osb), \
        (gw_out, dw_out, nm_out, nv_out) = stats

    gs, ds, ms, vs = _adamw_small(
        _pack_small(norm_in, norm_v, b_s[0], norm_final, w_s[0]),
        _pack_small(m_norm_in, m_norm_v, m_b_s[0], m_norm_final, m_w_s[0]),
        _pack_small(v_norm_in, v_norm_v, v_b_s[0], v_norm_final, v_w_s[0]), slots_head, slots_tail)

    def small(pack):
        nin, nv, bs, nf, ws = _unpack_small(pack, w_s.shape, b_s.shape)
        return nin, nv, ws, bs, nf

    loss = lax.psum(loss[0, 0], ("x", "y", "c"))
    out = []
    for small_pack, win, wog, wosb, wout in ((gs, gw_in, gw_og, gw_osb, gw_out), (ds, dw_in, dw_og, dw_osb, dw_out),
                                             (ms, nm_in, nm_og, nm_osb, nm_out), (vs, nv_in, nv_og, nv_osb, nv_out)):
        nin, nv, ws, bs, nf = small(small_pack)
        out += [nin, win[None], nv, ws, bs, wog[None], wosb[None], wout[None], nf]
    return (loss, grad_x, *out)
```

```python
import functools
import math

import jax
import jax.numpy as jnp
from jax import lax
from jax.experimental import pallas as pl
from jax.experimental.pallas import tpu as pltpu

F32 = jnp.float32
BF16 = jnp.bfloat16
EPS = 1e-6
HEADS = 8
N_SPLIT = 9
N_CHIP = 4
MESH = pl.DeviceIdType.MESH

ADAM_LR = 0.001
ADAM_B1 = 0.9
ADAM_B2 = 0.999
ADAM_EPS = 1e-08
ADAM_WD = 0.01
ADAM_STEP = 10

VMEM_LIMIT = 56 * 2 ** 20
TM = 256
ATT_T = 256
ATT_HP = 4
TKW = 512

NT = (((1,), (1,)), ((), ()))
TN = (((0,), (0,)), ((), ()))


def _params(sem):
    return pltpu.CompilerParams(dimension_semantics=sem, vmem_limit_bytes=VMEM_LIMIT)


def _resident(shape):
    nd = len(shape)
    return pl.BlockSpec(shape, lambda *_: (0,) * nd, pipeline_mode=pl.Buffered(1))


def _const(shape):
    nd = len(shape)
    return pl.BlockSpec(shape, lambda *_: (0,) * nd)


def _segments(d, ncol):
    segs = []
    edges = sorted({j * ncol for j in range(N_CHIP + 1)} | {s * d for s in range(N_SPLIT + 1)})
    for lo, hi in zip(edges[:-1], edges[1:]):
        segs.append((lo // ncol, lo % ncol, lo // d, lo % d, hi - lo))
    return segs


def _sigmoid(x):
    return 1.0 / (1.0 + jnp.exp(-x))


_GELU_C = math.sqrt(2.0 / math.pi)


def _gelu(x):
    return 0.5 * x * (1.0 + jnp.tanh(_GELU_C * (x + 0.044715 * (x * x * x))))


def _gelu_and_grad(x):
    x2 = x * x
    th = jnp.tanh(_GELU_C * (x + 0.044715 * (x2 * x)))
    val = 0.5 * x * (1.0 + th)
    grad = 0.5 * (1.0 + th) + 0.5 * x * (1.0 - th * th) * (_GELU_C * (1.0 + 3.0 * 0.044715 * x2))
    return val, grad


def _split_bf16(a):
    hi = a.astype(BF16)
    lo = (a - hi.astype(F32)).astype(BF16)
    return hi, lo


def _in_proj_fwd(x2d, g_in, wg):
    t, d = x2d.shape
    ncol = wg.shape[2]
    segs = _segments(d, ncol)

    def body(x_ref, g_ref, w_ref, apre_ref, qkv_ref, gpre_ref, ht_ref):
        x = x_ref[...]
        r = lax.rsqrt(jnp.mean(x * x, axis=-1, keepdims=True) + EPS)
        hf = x * r * g_ref[...]
        h = hf.astype(BF16)
        ht_ref[...] = hf.T.astype(BF16)
        outs = (apre_ref, qkv_ref, gpre_ref)
        for chip, c0, grp, s0, width in segs:
            res = jnp.dot(h, w_ref[chip, :, c0:c0 + width], preferred_element_type=F32)
            o = outs[grp // 3]
            off = (grp % 3) * d + s0
            o[:, off:off + width] = res.astype(o.dtype)

    return pl.pallas_call(
        body, name="in_proj_fwd",
        grid=(t // TM,),
        in_specs=[pl.BlockSpec((TM, d), lambda i: (i, 0)), _const((1, d)), _resident(wg.shape)],
        out_specs=[pl.BlockSpec((TM, 3 * d), lambda i: (i, 0)), pl.BlockSpec((TM, 3 * d), lambda i: (i, 0)),
                   pl.BlockSpec((TM, 3 * d), lambda i: (i, 0)), pl.BlockSpec((d, TM), lambda i: (0, i))],
        out_shape=[jax.ShapeDtypeStruct((t, 3 * d), F32), jax.ShapeDtypeStruct((t, 3 * d), BF16),
                   jax.ShapeDtypeStruct((t, 3 * d), F32), jax.ShapeDtypeStruct((d, t), BF16)],
        compiler_params=_params(("arbitrary",)),
    )(x2d, g_in, wg)


def _branch_a_fwd(a_pre, g_v, wm, b_t):
    t, d3 = a_pre.shape
    d = d3 // 3
    ng, chunk, _ = wm.shape
    cw = d // ng

    def body(a_ref, gv_ref, wm_ref, bt_ref, ya_ref):
        ua = _gelu(a_ref[:, 0:d])
        vg = _gelu(a_ref[:, d:2 * d])
        za = a_ref[:, 2 * d:3 * d]
        rv = lax.rsqrt(jnp.mean(vg * vg, axis=-1, keepdims=True) + EPS)
        va = (vg * rv * gv_ref[...]).astype(BF16)
        gate = ua * (za * _sigmoid(za))
        for g in range(ng):
            sl = slice(g * cw, (g + 1) * cw)
            mixed = jnp.dot(wm_ref[g], va[:, sl], preferred_element_type=F32) + bt_ref[:, g:g + 1]
            ya_ref[:, sl] = (gate[:, sl] * mixed).astype(BF16)

    return pl.pallas_call(
        body, name="branch_a_fwd",
        grid=(t // chunk,),
        in_specs=[pl.BlockSpec((chunk, d3), lambda i: (i, 0)), _const((1, d)), _const(wm.shape), _const(b_t.shape)],
        out_specs=pl.BlockSpec((chunk, d), lambda i: (i, 0)),
        out_shape=jax.ShapeDtypeStruct((t, d), BF16),
        compiler_params=_params(("arbitrary",)),
    )(a_pre, g_v, wm, b_t)


def _branch_a_bwd(a_pre, dya, g_v, wm, wm_t, b_t):
    t, d3 = a_pre.shape
    d = d3 // 3
    ng, chunk, _ = wm.shape
    cw = d // ng
    nsteps = t // chunk

    def body(a_ref, dya_ref, gv_ref, wm_ref, wmt_ref, bt_ref, da_ref, gws_ref, gbt_ref, gnv_ref, db_acc):
        i = pl.program_id(0)

        @pl.when(i == 0)
        def _():
            gws_ref[...] = jnp.zeros_like(gws_ref)
            gnv_ref[...] = jnp.zeros_like(gnv_ref)
            db_acc[...] = jnp.zeros_like(db_acc)

        ua, dgelu_u = _gelu_and_grad(a_ref[:, 0:d])
        vg, dgelu_v = _gelu_and_grad(a_ref[:, d:2 * d])
        za = a_ref[:, 2 * d:3 * d]
        sig = _sigmoid(za)
        sz = za * sig
        dsz = sig * (1.0 + za * (1.0 - sig))
        rv = lax.rsqrt(jnp.mean(vg * vg, axis=-1, keepdims=True) + EPS)
        nv = vg * rv
        gv = gv_ref[...]
        va = (nv * gv).astype(BF16)
        dya = dya_ref[...]
        dmix = dya * ua * sz
        db_acc[...] += dmix
        dmix_b = dmix.astype(BF16)
        t_gate = dya * sz
        t_z = dya * ua * dsz
        dva_parts = []
        for g in range(ng):
            sl = slice(g * cw, (g + 1) * cw)
            mixed = jnp.dot(wm_ref[g], va[:, sl], preferred_element_type=F32) + bt_ref[:, g:g + 1]
            da_ref[:, sl] = (t_gate[:, sl] * mixed * dgelu_u[:, sl]).astype(BF16)
            da_ref[:, 2 * d + g * cw:2 * d + (g + 1) * cw] = (t_z[:, sl] * mixed).astype(BF16)
            gws_ref[g] += lax.dot_general(dmix_b[:, sl], va[:, sl], NT, preferred_element_type=F32)
            dva_parts.append(jnp.dot(wmt_ref[g], dmix_b[:, sl], preferred_element_type=F32))
        dva = jnp.concatenate(dva_parts, axis=1)
        gnv_ref[...] += jnp.sum(dva * nv, axis=0, keepdims=True)
        dnv = dva * gv
        dvg = rv * (dnv - nv * jnp.mean(dnv * nv, axis=-1, keepdims=True))
        da_ref[:, d:2 * d] = (dvg * dgelu_v).astype(BF16)

        @pl.when(i == nsteps - 1)
        def _():
            acc = db_acc[...]
            for g in range(ng):
                gbt_ref[:, g:g + 1] = jnp.sum(acc[:, g * cw:(g + 1) * cw], axis=1, keepdims=True)

    return pl.pallas_call(
        body, name="branch_a_bwd",
        grid=(nsteps,),
        in_specs=[pl.BlockSpec((chunk, d3), lambda i: (i, 0)), pl.BlockSpec((chunk, d), lambda i: (i, 0)),
                  _const((1, d)), _const(wm.shape), _const(wm_t.shape), _const(b_t.shape)],
        out_specs=[pl.BlockSpec((chunk, d3), lambda i: (i, 0)), _const(wm.shape), _const(b_t.shape), _const((1, d))],
        out_shape=[jax.ShapeDtypeStruct((t, d3), BF16), jax.ShapeDtypeStruct(wm.shape, F32),
                   jax.ShapeDtypeStruct(b_t.shape, F32), jax.ShapeDtypeStruct((1, d), F32)],
        scratch_shapes=[pltpu.VMEM((chunk, d), F32)],
        compiler_params=_params(("arbitrary",)),
    )(a_pre, dya, g_v, wm, wm_t, b_t)


def _tri(n, rows_gt_cols):
    r = lax.broadcasted_iota(jnp.int32, (n, n), 0)
    c = lax.broadcasted_iota(jnp.int32, (n, n), 1)
    return (r > c) if rows_gt_cols else (r < c)


def _twice(tri):
    t = tri.astype(BF16)
    return jnp.concatenate([t, t], axis=0)


def _cumsum_mm(a, tri2):
    hi, lo = _split_bf16(a)
    return jnp.dot(jnp.concatenate([hi, lo], axis=1), tri2, preferred_element_type=F32)


LOG2E = 1.4426950408889634
_SIGN = 0x80000000


def _sb_block(q, k, scale, upper2, causal):
    z2 = lax.dot_general(q, k, NT, preferred_element_type=F32) * (scale * LOG2E)
    neg_abs = lax.bitcast_convert_type(lax.bitcast_convert_type(z2, jnp.uint32) | jnp.uint32(_SIGN), F32)
    l2 = jnp.log(1.0 + jnp.exp2(neg_abs)) * LOG2E
    log_beta = jnp.minimum(z2, 0.0) - l2
    lom = log_beta - z2
    if causal is not None:
        lom = jnp.where(causal, lom, 0.0)
    sx = _cumsum_mm(lom, upper2)
    return log_beta, sx, sx[:, 0:1] + lom[:, 0:1]


DEAD_LOG2 = -150.0


def _max_carry(carries):
    return jnp.max(functools.reduce(jnp.maximum, carries))


def _attn_specs(d, seq, nq):
    hp_w = ATT_HP * (d // HEADS)
    n_hp = d // hp_w
    row_blk = pl.BlockSpec((ATT_T, hp_w), lambda b, h, i: (b * nq + i, h))
    k_blk = pl.BlockSpec((seq, hp_w), lambda b, h, i: (b, n_hp + h))
    v_blk = pl.BlockSpec((seq, hp_w), lambda b, h, i: (b, 2 * n_hp + h))
    seq_blk = pl.BlockSpec((seq, hp_w), lambda b, h, i: (b, h))
    return row_blk, k_blk, v_blk, seq_blk, n_hp


def _attn_fwd(qkv, g_pre, bsz, seq):
    t, d3 = qkv.shape
    d = d3 // 3
    hd = d // HEADS
    nq = seq // ATT_T
    scale = hd ** -0.5
    row_blk, k_blk, v_blk, _, n_hp = _attn_specs(d, seq, nq)

    def body(q_ref, k_ref, v_ref, zb_ref, o_ref, yb_ref):
        i = pl.program_id(2)
        causal = _tri(ATT_T, True)
        upper2 = _twice(causal)

        def step(kb, state, mask):
            rows = pl.ds(pl.multiple_of(kb * ATT_T, ATT_T), ATT_T)
            heads = [slice(h * hd, (h + 1) * hd) for h in range(ATT_HP)]
            scores = [_sb_block(q_ref[:, cols], k_ref[rows, cols], scale, upper2, mask) for cols in heads]
            new = []
            for cols, (carry, acc), (log_beta, sx, total) in zip(heads, state, scores):
                a = jnp.exp2(log_beta + sx + carry)
                if mask is not None:
                    a = jnp.where(mask, a, 0.0)
                acc = acc + jnp.dot(a.astype(BF16), v_ref[rows, cols], preferred_element_type=F32)
                new.append((carry + total, acc))
            return tuple(new)

        init = tuple((jnp.zeros((ATT_T, 1), F32), jnp.zeros((ATT_T, hd), F32)) for _ in range(ATT_HP))
        state = step(i, init, causal)
        def more(c):
            new = step(c[0], c[1], None)
            return c[0] - 1, new, _max_carry([s[0] for s in new])

        _, state, _ = lax.while_loop(lambda c: (c[0] >= 0) & (c[2] > DEAD_LOG2), more,
                                     (i - 1, state, _max_carry([s[0] for s in state])))
        for h in range(ATT_HP):
            cols = slice(h * hd, (h + 1) * hd)
            acc = state[h][1]
            zb = zb_ref[:, cols]
            o_ref[:, cols] = acc
            yb_ref[:, cols] = (acc * (zb * _sigmoid(zb))).astype(BF16)

    return pl.pallas_call(
        body, name="attn_fwd",
        grid=(bsz, n_hp, nq),
        in_specs=[row_blk, k_blk, v_blk, row_blk],
        out_specs=[row_blk, row_blk],
        out_shape=[jax.ShapeDtypeStruct((t, d), F32), jax.ShapeDtypeStruct((t, d), BF16)],
        compiler_params=_params(("arbitrary", "arbitrary", "arbitrary")),
    )(qkv, qkv, qkv, g_pre)


def _attn_bwd(qkv, g_pre, o, dyb, bsz, seq):
    t, d3 = qkv.shape
    d = d3 // 3
    hd = d // HEADS
    nq = seq // ATT_T
    scale = hd ** -0.5
    row_blk, k_blk, v_blk, seq_blk, n_hp = _attn_specs(d, seq, nq)

    def body(q_ref, k_ref, v_ref, zb_ref, o_ref, dyb_ref, dq_ref, dk_ref, dv_ref, dzb_ref,
             g_s, beta_s, dkt_acc, dvt_acc):
        i = pl.program_id(2)

        @pl.when(i == 0)
        def _():
            dkt_acc[...] = jnp.zeros_like(dkt_acc)
            dvt_acc[...] = jnp.zeros_like(dvt_acc)

        causal = _tri(ATT_T, True)
        upper2 = _twice(causal)
        lower2 = _twice(~causal)
        zb = zb_ref[...]
        sig = _sigmoid(zb)
        dyb_t = dyb_ref[...]
        do_f = dyb_t * (zb * sig)
        do = do_f.astype(BF16)
        do_t = do_f.T.astype(BF16)
        q_t = q_ref[...].astype(F32).T.astype(BF16)
        dzb_ref[...] = (dyb_t * o_ref[...] * (sig * (1.0 + zb * (1.0 - sig)))).astype(BF16)

        def sweep(kb, carries, mask):
            rows = pl.ds(pl.multiple_of(kb * ATT_T, ATT_T), ATT_T)
            heads = [slice(h * hd, (h + 1) * hd) for h in range(ATT_HP)]
            scores = [_sb_block(q_ref[:, cols], k_ref[rows, cols], scale, upper2, mask) for cols in heads]
            das = [lax.dot_general(do[:, cols], v_ref[rows, cols], NT, preferred_element_type=F32) for cols in heads]
            new = []
            for h, (cols, carry, (log_beta, sx, total), da) in enumerate(zip(heads, carries, scores, das)):
                a = jnp.exp2(log_beta + sx + carry)
                beta = jnp.exp2(log_beta)
                if mask is not None:
                    a = jnp.where(mask, a, 0.0)
                    beta = jnp.where(mask, beta, 0.0)
                g_s[h, kb] = a * da
                beta_s[h, kb] = beta
                dvt_acc[kb, cols, :] += jnp.dot(do_t[cols, :], a.astype(BF16), preferred_element_type=F32)
                new.append(carry + total)
            return tuple(new)

        carries = sweep(i, tuple(jnp.zeros((ATT_T, 1), F32) for _ in range(ATT_HP)), causal)

        def more(c):
            new = sweep(c[0], c[1], None)
            return c[0] - 1, new, _max_carry(new)

        last, _, _ = lax.while_loop(lambda c: (c[0] >= 0) & (c[2] > DEAD_LOG2), more, (i - 1, carries, _max_carry(carries)))
        first_kb = last + 1

        def back(kb, state):
            rows = pl.ds(pl.multiple_of(kb * ATT_T, ATT_T), ATT_T)
            heads = [slice(h * hd, (h + 1) * hd) for h in range(ATT_HP)]
            sums = [_cumsum_mm(g_s[h, kb], lower2) for h in range(ATT_HP)]
            new = []
            for h, (cols, (p_carry, dq), px) in enumerate(zip(heads, state, sums)):
                dz = ((g_s[h, kb] - (p_carry + px) * beta_s[h, kb]) * scale).astype(BF16)
                dq = dq + jnp.dot(dz, k_ref[rows, cols], preferred_element_type=F32)
                dkt_acc[kb, cols, :] += jnp.dot(q_t[cols, :], dz, preferred_element_type=F32)
                new.append((p_carry + px[:, ATT_T - 1:ATT_T], dq))
            return tuple(new)

        init = tuple((jnp.zeros((ATT_T, 1), F32), jnp.zeros((ATT_T, hd), F32)) for _ in range(ATT_HP))
        state = lax.fori_loop(first_kb, i + 1, back, init)
        for h in range(ATT_HP):
            dq_ref[:, h * hd:(h + 1) * hd] = state[h][1].astype(BF16)

        @pl.when(i == nq - 1)
        def _():
            for kb in range(nq):
                dk_ref[kb * ATT_T:(kb + 1) * ATT_T, :] = dkt_acc[kb].T.astype(BF16)
                dv_ref[kb * ATT_T:(kb + 1) * ATT_T, :] = dvt_acc[kb].T.astype(BF16)

    out = jax.ShapeDtypeStruct((t, d), BF16)
    hp_w = ATT_HP * hd
    return pl.pallas_call(
        body, name="attn_bwd",
        grid=(bsz, n_hp, nq),
        in_specs=[row_blk, k_blk, v_blk, row_blk, row_blk, row_blk],
        out_specs=[row_blk, seq_blk, seq_blk, row_blk],
        out_shape=[out, out, out, out],
        scratch_shapes=[pltpu.VMEM((ATT_HP, nq, ATT_T, ATT_T), F32), pltpu.VMEM((ATT_HP, nq, ATT_T, ATT_T), F32),
                        pltpu.VMEM((nq, hp_w, ATT_T), F32), pltpu.VMEM((nq, hp_w, ATT_T), F32)],
        compiler_params=_params(("arbitrary", "arbitrary", "arbitrary")),
    )(qkv, qkv, qkv, g_pre, o, dyb)


def _out_proj(ya, yb, g_pre, x2d, tgt, w_og, w_osb, w_out, g_f):
    t, d = x2d.shape

    def body(ya_ref, yb_ref, ga_ref, gb_ref, x_ref, tgt_ref, wog_ref, wosb_ref, wout_ref, gf_ref,
             dya_ref, dyb_ref, dg_ref, dx2_ref, loss_ref, gnf_ref, gwog_ref, gwosb_ref, gwout_ref):
        @pl.when(pl.program_id(0) == 0)
        def _():
            loss_ref[...] = jnp.zeros_like(loss_ref)
            gnf_ref[...] = jnp.zeros_like(gnf_ref)
            gwog_ref[...] = jnp.zeros_like(gwog_ref)
            gwosb_ref[...] = jnp.zeros_like(gwosb_ref)
            gwout_ref[...] = jnp.zeros_like(gwout_ref)

        ya = ya_ref[...]
        yb = yb_ref[...]
        pa = jnp.dot(ya, wog_ref[...], preferred_element_type=F32)
        pb = jnp.dot(yb, wosb_ref[...], preferred_element_type=F32)
        sga = _sigmoid(ga_ref[...])
        sgb = _sigmoid(gb_ref[...])
        merged = (sga * pa + sgb * pb).astype(BF16)
        x2 = x_ref[...] + jnp.dot(merged, wout_ref[...], preferred_element_type=F32)
        r2 = lax.rsqrt(jnp.mean(x2 * x2, axis=-1, keepdims=True) + EPS)
        n2 = x2 * r2
        gf = gf_ref[...]
        err = n2 * gf - tgt_ref[...]
        loss_ref[...] += 0.5 * jnp.sum(jnp.sum(err * err, axis=-1, keepdims=True), axis=0, keepdims=True) / d
        dy = err * (1.0 / d)
        gnf_ref[...] += jnp.sum(dy * n2, axis=0, keepdims=True)
        dn = dy * gf
        dx2 = r2 * (dn - n2 * jnp.mean(dn * n2, axis=-1, keepdims=True))
        dx2_ref[...] = dx2
        dx2_b = dx2.astype(BF16)
        dmerged = lax.dot_general(dx2_b, wout_ref[...], NT, preferred_element_type=F32)
        gwout_ref[...] += lax.dot_general(merged, dx2_b, TN, preferred_element_type=F32)
        dg_ref[:, 0:d] = (dmerged * pa * (sga * (1.0 - sga))).astype(BF16)
        dg_ref[:, d:2 * d] = (dmerged * pb * (sgb * (1.0 - sgb))).astype(BF16)
        dpa = (dmerged * sga).astype(BF16)
        dpb = (dmerged * sgb).astype(BF16)
        dya_ref[...] = lax.dot_general(dpa, wog_ref[...], NT, preferred_element_type=F32)
        dyb_ref[...] = lax.dot_general(dpb, wosb_ref[...], NT, preferred_element_type=F32)
        gwog_ref[...] += lax.dot_general(ya, dpa, TN, preferred_element_type=F32)
        gwosb_ref[...] += lax.dot_general(yb, dpb, TN, preferred_element_type=F32)

    row = lambda i: (i, 0)
    return pl.pallas_call(
        body, name="out_proj",
        grid=(t // TM,),
        in_specs=[pl.BlockSpec((TM, d), row), pl.BlockSpec((TM, d), row),
                  pl.BlockSpec((TM, d), lambda i: (i, 1)), pl.BlockSpec((TM, d), lambda i: (i, 2)),
                  pl.BlockSpec((TM, d), row), pl.BlockSpec((TM, d), row),
                  _resident((d, d)), _resident((d, d)), _resident((d, d)), _const((1, d))],
        out_specs=[pl.BlockSpec((TM, d), row), pl.BlockSpec((TM, d), row), pl.BlockSpec((TM, 2 * d), row),
                   pl.BlockSpec((TM, d), row), _const((1, 1)), _const((1, d)),
                   _const((d, d)), _const((d, d)), _const((d, d))],
        out_shape=[jax.ShapeDtypeStruct((t, d), F32), jax.ShapeDtypeStruct((t, d), F32),
                   jax.ShapeDtypeStruct((t, 2 * d), BF16), jax.ShapeDtypeStruct((t, d), F32),
                   jax.ShapeDtypeStruct((1, 1), F32), jax.ShapeDtypeStruct((1, d), F32),
                   jax.ShapeDtypeStruct((d, d), F32), jax.ShapeDtypeStruct((d, d), F32),
                   jax.ShapeDtypeStruct((d, d), F32)],
        compiler_params=_params(("arbitrary",)),
    )(ya, yb, g_pre, g_pre, x2d, tgt, w_og, w_osb, w_out, g_f)


def _dproj_pieces(d):
    return [(0, 0, 3), (1, 3, 1), (2, 4, 1), (3, 5, 1), (4, 6, 1), (5, 7, 2)]


def _in_proj_bwd_x(pieces, wg, x2d, g_in, dx2, gw16):
    t, d = x2d.shape
    ncol = wg.shape[2]
    segs = _segments(d, ncol)
    layout = _dproj_pieces(d)
    nsteps = t // TM

    def body(da_ref, dq_ref, dk_ref, dv_ref, dzb_ref, dg_ref, w_ref, x_ref, g_ref, dx2_ref, gw16_ref,
             gx_ref, gn_ref, recv_ref, send_sems, recv_sems):
        x_pos, y_pos, c_pos, chips = _place()

        def share(k, chunk):
            px, py = chips[k]
            return pltpu.make_async_remote_copy(
                src_ref=gw16_ref.at[:, chunk * ncol:(chunk + 1) * ncol], dst_ref=recv_ref.at[k],
                send_sem=send_sems.at[k], recv_sem=recv_sems.at[k], device_id=(px, py, c_pos), device_id_type=MESH)

        @pl.when(pl.program_id(0) == 0)
        def _():
            gn_ref[...] = jnp.zeros_like(gn_ref)
            for k, (px, py) in enumerate(chips):
                for chunk in range(N_CHIP):
                    @pl.when(2 * px + py == chunk)
                    def _(k=k, chunk=chunk):
                        share(k, chunk).start()

        @pl.when(pl.program_id(0) == nsteps - 1)
        def _():
            for k in range(N_CHIP - 1):
                share(k, 0).wait()

        refs = (da_ref, dq_ref, dk_ref, dv_ref, dzb_ref, dg_ref)
        dh = jnp.zeros((TM, d), F32)
        for chip, c0, grp, s0, width in segs:
            piece, first, _ = next(p for p in layout if p[1] <= grp < p[1] + p[2])
            off = (grp - first) * d + s0
            dh = dh + lax.dot_general(refs[piece][:, off:off + width], w_ref[chip, :, c0:c0 + width], NT,
                                      preferred_element_type=F32)
        x = x_ref[...]
        r = lax.rsqrt(jnp.mean(x * x, axis=-1, keepdims=True) + EPS)
        n = x * r
        gn_ref[...] += jnp.sum(dh * n, axis=0, keepdims=True)
        dn = dh * g_ref[...]
        gx_ref[...] = dx2_ref[...] + r * (dn - n * jnp.mean(dn * n, axis=-1, keepdims=True))

    row = lambda i: (i, 0)
    return pl.pallas_call(
        body, name="in_proj_bwd_x",
        grid=(t // TM,),
        in_specs=[pl.BlockSpec((TM, p.shape[1]), row) for p in pieces]
        + [_resident(wg.shape), pl.BlockSpec((TM, d), row), _const((1, d)), pl.BlockSpec((TM, d), row), ANY],
        out_specs=[pl.BlockSpec((TM, d), row), _const((1, d)), ANY],
        out_shape=[jax.ShapeDtypeStruct((t, d), F32), jax.ShapeDtypeStruct((1, d), F32),
                   jax.ShapeDtypeStruct((N_CHIP - 1, d, ncol), BF16)],
        scratch_shapes=[pltpu.SemaphoreType.DMA((N_CHIP - 1,)), pltpu.SemaphoreType.DMA((N_CHIP - 1,))],
        compiler_params=_params(("arbitrary",)),
    )(*pieces, wg, x2d, g_in, dx2, gw16)


def _in_proj_bwd_w(h_t, pieces, mats16, pack):
    d, t = h_t.shape
    nk = t // TKW
    layout = _dproj_pieces(d)
    n_mats = len(mats16)
    n_dev = 2 * N_CHIP
    flips = [(dx, dy, dc) for dx in (0, 1) for dy in (0, 1) for dc in (0, 1)][1:]

    def body(ht_ref, da_ref, dq_ref, dk_ref, dv_ref, dzb_ref, dg_ref, *rest):
        mat_refs, pack_ref = rest[:n_mats], rest[n_mats]
        gw_ref, gw16_ref = rest[n_mats + 1:n_mats + 3]
        recv_refs, slots_ref = rest[n_mats + 3:2 * n_mats + 3], rest[2 * n_mats + 3]
        acc, mat_send, mat_recv, pack_send, pack_recv, own_sem = rest[2 * n_mats + 4:]
        s = pl.program_id(0)
        i = pl.program_id(1)
        x_pos, y_pos, c_pos, chips = _place()
        me = 4 * x_pos + 2 * y_pos + c_pos

        def exchanges():
            cps = []
            for k, (px, py) in enumerate(chips):
                for a in range(n_mats):
                    cps.append(pltpu.make_async_remote_copy(
                        src_ref=mat_refs[a].at[2 * px + py], dst_ref=recv_refs[a].at[k],
                        send_sem=mat_send.at[a, k], recv_sem=mat_recv.at[a, k],
                        device_id=(px, py, c_pos), device_id_type=MESH))
            for k, (dx, dy, dc) in enumerate(flips):
                peer = (1 - x_pos if dx else x_pos, 1 - y_pos if dy else y_pos, 1 - c_pos if dc else c_pos)
                cps.append(pltpu.make_async_remote_copy(
                    src_ref=pack_ref, dst_ref=slots_ref.at[me], send_sem=pack_send.at[k], recv_sem=pack_recv.at[k],
                    device_id=peer, device_id_type=MESH))
            return cps, pltpu.make_async_copy(pack_ref, slots_ref.at[me], own_sem)

        @pl.when((s == 0) & (i == 0))
        def _():
            cps, own = exchanges()
            own.start()
            for cp in cps:
                cp.start()

        @pl.when((s == N_SPLIT - 1) & (i == nk - 1))
        def _():
            cps, own = exchanges()
            own.wait()
            for cp in cps:
                cp.wait()

        @pl.when(i == 0)
        def _():
            acc[...] = jnp.zeros_like(acc)

        refs = (da_ref, dq_ref, dk_ref, dv_ref, dzb_ref, dg_ref)
        for piece, first, count in layout:
            for j in range(count):
                @pl.when(s == first + j)
                def _(piece=piece, j=j):
                    acc[...] += jnp.dot(ht_ref[...], refs[piece][:, j * d:(j + 1) * d], preferred_element_type=F32)

        @pl.when(i == nk - 1)
        def _():
            gw_ref[...] = acc[...]
            gw16_ref[...] = acc[...].astype(BF16)

    def piece_spec(p, first, count):
        width = count * d
        return pl.BlockSpec((TKW, width), lambda s, i: (jnp.where((s >= first) & (s < first + count), i, 0), 0))

    col_blk = pl.BlockSpec((d, d), lambda s, i: (0, s))
    outs = pl.pallas_call(
        body, name="in_proj_bwd_w",
        grid=(N_SPLIT, nk),
        in_specs=[pl.BlockSpec((d, TKW), lambda s, i: (0, i))] + [piece_spec(*p) for p in layout] + [ANY] * (n_mats + 1),
        out_specs=[col_blk, col_blk] + [ANY] * (n_mats + 1),
        out_shape=[jax.ShapeDtypeStruct((d, N_SPLIT * d), F32), jax.ShapeDtypeStruct((d, N_SPLIT * d), BF16)]
        + [jax.ShapeDtypeStruct((N_CHIP - 1,) + m.shape[1:], BF16) for m in mats16]
        + [jax.ShapeDtypeStruct((n_dev,) + pack.shape, F32)],
        scratch_shapes=[pltpu.VMEM((d, d), F32),
                        pltpu.SemaphoreType.DMA((n_mats, N_CHIP - 1)), pltpu.SemaphoreType.DMA((n_mats, N_CHIP - 1)),
                        pltpu.SemaphoreType.DMA((n_dev - 1,)), pltpu.SemaphoreType.DMA((n_dev - 1,)),
                        pltpu.SemaphoreType.DMA],
        compiler_params=_params(("arbitrary", "arbitrary")),
    )(h_t, *pieces, *mats16, pack)
    return outs[0], outs[1], outs[2:2 + n_mats], outs[2 + n_mats]


def _local_step(x2d, tgt2d, bsz, seq, norm_in, wg, norm_v, w_s, b_s, w_og, w_osb, w_out, norm_final):
    d = x2d.shape[1]
    chunk = w_s.shape[-1]
    causal = jnp.tril(jnp.ones((chunk, chunk), dtype=bool))
    wm = jnp.where(causal[None], w_s, 0.0).astype(BF16)
    wm_t = jnp.swapaxes(wm, 1, 2)
    b_t = b_s.T

    a_pre, qkv, g_pre, h_t = _in_proj_fwd(x2d, norm_in, wg)
    ya = _branch_a_fwd(a_pre, norm_v, wm, b_t)
    o, yb = _attn_fwd(qkv, g_pre, bsz, seq)
    dya, dyb, dg, dx2, loss, g_nf, g_wog, g_wosb, g_wout = _out_proj(
        ya, yb, g_pre, x2d, tgt2d, w_og, w_osb, w_out, norm_final.reshape(1, d))
    dq, dk, dv, dzb = _attn_bwd(qkv, g_pre, o, dyb, bsz, seq)
    d_a, g_ws, g_bt, g_nv = _branch_a_bwd(a_pre, dya, norm_v, wm, wm_t, b_t)
    g_ws = jnp.where(causal[None], g_ws, 0.0)
    return loss, h_t, (d_a, dq, dk, dv, dzb, dg), dx2, g_nv, g_ws, g_bt.T, g_wog, g_wosb, g_wout, g_nf


def _row_tile(rows):
    return next(r for r in (128, 64, 32, 16, 8) if rows % r == 0)


def _cast_bf16(a):
    rows, cols = a.shape
    tr = _row_tile(rows)

    def body(a_ref, o_ref):
        o_ref[...] = a_ref[...].astype(BF16)

    return pl.pallas_call(
        body, name="cast_bf16", grid=(rows // tr,),
        in_specs=[pl.BlockSpec((tr, cols), lambda i: (i, 0))],
        out_specs=pl.BlockSpec((tr, cols), lambda i: (i, 0)),
        out_shape=jax.ShapeDtypeStruct(a.shape, BF16),
        compiler_params=_params(("arbitrary",)),
    )(a)


def _add_received(full, recv, chip, by_cols):
    _, rows, cols = recv.shape
    tr = _row_tile(rows)
    nb = rows // tr

    def body(chip_ref, own_ref, recv_ref, o_ref):
        s = own_ref[...]
        for k in range(N_CHIP - 1):
            s = s + recv_ref[k].astype(F32)
        o_ref[...] = s

    own_map = (lambda i, chip_ref: (i, chip_ref[0])) if by_cols else (lambda i, chip_ref: (chip_ref[0] * nb + i, 0))
    return pl.pallas_call(
        body, name="add_received",
        grid_spec=pltpu.PrefetchScalarGridSpec(
            num_scalar_prefetch=1, grid=(nb,),
            in_specs=[pl.BlockSpec((tr, cols), own_map),
                      pl.BlockSpec((N_CHIP - 1, tr, cols), lambda i, chip_ref: (0, i, 0))],
            out_specs=pl.BlockSpec((tr, cols), lambda i, chip_ref: (i, 0))),
        out_shape=jax.ShapeDtypeStruct((rows, cols), F32),
        compiler_params=_params(("arbitrary",)),
    )(chip.reshape(1).astype(jnp.int32), full, recv)


def _adamw_math(w, m, v, g):
    new_m = ADAM_B1 * m + (1.0 - ADAM_B1) * g
    new_v = ADAM_B2 * v + (1.0 - ADAM_B2) * (g * g)
    m_hat = new_m / (1.0 - ADAM_B1 ** ADAM_STEP)
    v_hat = new_v / (1.0 - ADAM_B2 ** ADAM_STEP)
    return -ADAM_LR * (m_hat / (jnp.sqrt(v_hat) + ADAM_EPS) + ADAM_WD * w), new_m, new_v


def _adamw(w, m, v, g_parts):
    rows, cols = w.shape
    tr = _row_tile(rows)
    n_parts = len(g_parts)

    def body(*refs):
        w_ref, m_ref, v_ref = refs[:3]
        part_refs = refs[3:3 + n_parts]
        g_ref, d_ref, nm_ref, nv_ref = refs[3 + n_parts:]
        g = part_refs[0][...]
        for p in part_refs[1:]:
            g = g + p[...]
        g_ref[...] = g
        d_ref[...], nm_ref[...], nv_ref[...] = _adamw_math(w_ref[...], m_ref[...], v_ref[...], g)

    spec = pl.BlockSpec((tr, cols), lambda i: (i, 0))
    out = jax.ShapeDtypeStruct(w.shape, F32)
    return pl.pallas_call(
        body, name="adamw", grid=(rows // tr,),
        in_specs=[spec] * (3 + n_parts), out_specs=[spec] * 4, out_shape=[out] * 4,
        compiler_params=_params(("arbitrary",)),
    )(w, m, v, *g_parts)


def _adamw_small(w, m, v, slots_head, slots_tail):
    n_dev, p0, _ = slots_head.shape

    def body(w_ref, m_ref, v_ref, head_ref, tail_ref, g_ref, d_ref, nm_ref, nv_ref):
        for ref, rows in ((head_ref, slice(0, p0)), (tail_ref, slice(p0, w.shape[0]))):
            g = ref[0]
            for i in range(1, n_dev):
                g = g + ref[i]
            g_ref[rows, :] = g
            d_ref[rows, :], nm_ref[rows, :], nv_ref[rows, :] = _adamw_math(w_ref[rows, :], m_ref[rows, :], v_ref[rows, :], g)

    vmem = pl.BlockSpec(memory_space=pltpu.VMEM)
    out = jax.ShapeDtypeStruct(w.shape, F32)
    return pl.pallas_call(
        body, name="adamw_small", in_specs=[vmem] * 5, out_specs=[vmem] * 4, out_shape=[out] * 4,
        compiler_params=pltpu.CompilerParams(vmem_limit_bytes=VMEM_LIMIT),
    )(w, m, v, slots_head, slots_tail)


ANY = pl.BlockSpec(memory_space=pl.ANY)


def _place():
    x, y, c = lax.axis_index("x"), lax.axis_index("y"), lax.axis_index("c")
    other_chips = [(1 - x, y), (x, 1 - y), (1 - x, 1 - y)]
    return x, y, c, other_chips


def _all_gather_halves(halves):
    n = len(halves)

    def body(*refs):
        ins, outs = refs[:n], refs[n:2 * n]
        send_sems, recv_sems, local_sems = refs[2 * n:]
        x, y, c, chips = _place()
        sibling = (x, y, 1 - c)
        my_chip = 2 * x + y

        def copy(a, k, block, to, src=None):
            return pltpu.make_async_remote_copy(
                src_ref=outs[a].at[block] if src is None else src, dst_ref=outs[a].at[block],
                send_sem=send_sems.at[a, k], recv_sem=recv_sems.at[a, k], device_id=to, device_id_type=MESH)

        local = [pltpu.make_async_copy(ins[a], outs[a].at[pl.ds(2 * my_chip, 2)], local_sems.at[a]) for a in range(n)]
        for cp in local:
            cp.start()
        started = []
        for k, (px, py) in enumerate(chips):
            for a in range(n):
                cp = copy(a, k, 2 * my_chip + c, (px, py, c), src=ins[a].at[c])
                cp.start()
                started.append(cp)
        for k, (px, py) in enumerate(chips):
            for a in range(n):
                block = 2 * (2 * px + py) + c
                copy(a, k, block, (x, y, c)).wait_recv()
                cp = copy(a, 3 + k, block, sibling)
                cp.start()
                started.append(cp)
        for k, (px, py) in enumerate(chips):
            for a in range(n):
                copy(a, 3 + k, 2 * (2 * px + py) + 1 - c, (x, y, c)).wait_recv()
        for cp in started:
            cp.wait_send()
        for cp in local:
            cp.wait()

    return pl.pallas_call(
        body, name="all_gather_weights",
        in_specs=[ANY] * n, out_specs=[ANY] * n,
        out_shape=[jax.ShapeDtypeStruct((2 * N_CHIP,) + h.shape[1:], h.dtype) for h in halves],
        scratch_shapes=[pltpu.SemaphoreType.DMA((n, 6)), pltpu.SemaphoreType.DMA((n, 6)), pltpu.SemaphoreType.DMA((n,))],
    )(*halves)


def _swap_and_gather(arrs, pack):
    n = len(arrs)
    n_dev = 2 * N_CHIP
    flips = [(dx, dy, dc) for dx in (0, 1) for dy in (0, 1) for dc in (0, 1)][1:]

    def body(*refs):
        ins, pack_ref = refs[:n], refs[n]
        outs, slots_ref = refs[n + 1:2 * n + 1], refs[2 * n + 1]
        send_sems, recv_sems, pack_send, pack_recv, own_sem = refs[2 * n + 2:]
        x, y, c, _ = _place()
        me = 4 * x + 2 * y + c
        own = pltpu.make_async_copy(pack_ref, slots_ref.at[me], own_sem)
        own.start()
        copies = []
        for k, (dx, dy, dc) in enumerate(flips):
            peer = (1 - x if dx else x, 1 - y if dy else y, 1 - c if dc else c)
            copies.append(pltpu.make_async_remote_copy(
                src_ref=pack_ref, dst_ref=slots_ref.at[me], send_sem=pack_send.at[k], recv_sem=pack_recv.at[k],
                device_id=peer, device_id_type=MESH))
        copies += [pltpu.make_async_remote_copy(
            src_ref=ins[a], dst_ref=outs[a], send_sem=send_sems.at[a], recv_sem=recv_sems.at[a],
            device_id=(x, y, 1 - c), device_id_type=MESH) for a in range(n)]
        for cp in copies:
            cp.start()
        for cp in copies:
            cp.wait()
        own.wait()

    return pl.pallas_call(
        body, name="swap_and_gather",
        in_specs=[ANY] * (n + 1), out_specs=[ANY] * (n + 1),
        out_shape=[jax.ShapeDtypeStruct(a.shape, a.dtype) for a in arrs] + [jax.ShapeDtypeStruct((n_dev,) + pack.shape, F32)],
        scratch_shapes=[pltpu.SemaphoreType.DMA((n,)), pltpu.SemaphoreType.DMA((n,)),
                        pltpu.SemaphoreType.DMA((n_dev - 1,)), pltpu.SemaphoreType.DMA((n_dev - 1,)),
                        pltpu.SemaphoreType.DMA],
    )(*arrs, pack)


SLAB = 8


def _slab(vec, d):
    return jnp.pad(vec.reshape(1, d), ((0, SLAB - 1), (0, 0)))


def _pack_tail(vec_nv, b_s, vec_nf, w_s):
    d = vec_nv.shape[-1]
    return jnp.concatenate([_slab(v, d) for v in (vec_nv, b_s, vec_nf)] + [w_s.reshape(-1, d)], axis=0)


def _pack_small(vec_nin, vec_nv, b_s, vec_nf, w_s):
    return jnp.concatenate([_slab(vec_nin, vec_nin.shape[-1]), _pack_tail(vec_nv, b_s, vec_nf, w_s)], axis=0)


def _unpack_small(pack, w_s_shape, b_s_shape):
    return (pack[0:1], pack[SLAB:SLAB + 1], pack[2 * SLAB].reshape(b_s_shape), pack[3 * SLAB],
            pack[4 * SLAB:].reshape(w_s_shape))


def kernel(x, norm_in, w_in, norm_v, w_s, b_s, w_o_gmlp, w_o_sb, w_out, norm_final, loss_target, m_norm_in, m_w_in, m_norm_v, m_w_s, m_b_s, m_w_o_gmlp, m_w_o_sb, m_w_out, m_norm_final, v_norm_in, v_w_in, v_norm_v, v_w_s, v_b_s, v_w_o_gmlp, v_w_o_sb, v_w_out, v_norm_final):
    d = x.shape[-1]
    ncol = w_in.shape[-1]
    nrow = w_o_gmlp.shape[-2]
    chip = 2 * lax.axis_index("x") + lax.axis_index("y")

    shards = [w_in[0], w_o_gmlp[0], w_o_sb[0], w_out[0]]
    halves = [_cast_bf16(s).reshape(2, s.shape[0] // 2, s.shape[1]) for s in shards]
    wg, w_og, w_osb, w_o = _all_gather_halves(halves)
    wg = wg.reshape(N_CHIP, d, ncol)

    bsz, seq, _ = x.shape
    x2d = x.reshape(bsz * seq, d)
    loss, h_t, pieces, dx2, g_nv, g_ws, g_bs, g_wog, g_wosb, g_wout, g_nf = _local_step(
        x2d, loss_target.reshape(bsz * seq, d), bsz, seq, norm_in, wg, norm_v, w_s[0], b_s[0],
        w_og.reshape(d, d), w_osb.reshape(d, d), w_o.reshape(d, d), norm_final)

    mats = [g_wog, g_wosb, g_wout]
    mats16 = [_cast_bf16(g).reshape(N_CHIP, nrow, d) for g in mats]
    g_win, g_win16, recv_mats, slots_tail = _in_proj_bwd_w(h_t, pieces, mats16, _pack_tail(g_nv, g_bs, g_nf, g_ws))
    grad_x, g_nin, recv_win = _in_proj_bwd_x(pieces, wg, x2d, norm_in, dx2, g_win16)
    grad_x = grad_x.reshape(bsz, seq, d)

    sums = [_add_received(g_win, recv_win, chip, True)] + [
        _add_received(g, r, chip, False) for g, r in zip(mats, recv_mats)]
    *sibling_sums, slots_head = _swap_and_gather(sums, _slab(g_nin, d))
    stats = []
    for w, m, v, mine, theirs in zip(shards, [m_w_in[0], m_w_o_gmlp[0], m_w_o_sb[0], m_w_out[0]],
                                     [v_w_in[0], v_w_o_gmlp[0], v_w_o_sb[0], v_w_out[0]], sums, sibling_sums):
        stats.append(_adamw(w, m, v, [mine, theirs]))
    (gw_in, dw_in, nm_in, nv_in), (gw_og, dw_og, nm_og, nv_og), (gw_osb, dw_osb, nm_osb, nv_osb), \
        (gw_out, dw_out, nm_out, nv_out) = stats

    gs, ds, ms, vs = _adamw_small(
        _pack_small(norm_in, norm_v, b_s[0], norm_final, w_s[0]),
        _pack_small(m_norm_in, m_norm_v, m_b_s[0], m_norm_final, m_w_s[0]),
        _pack_small(v_norm_in, v_norm_v, v_b_s[0], v_norm_final, v_w_s[0]), slots_head, slots_tail)

    def small(pack):
        nin, nv, bs, nf, ws = _unpack_small(pack, w_s.shape, b_s.shape)
        return nin, nv, ws, bs, nf

    loss = lax.psum(loss[0, 0], ("x", "y", "c"))
    out = []
    for small_pack, win, wog, wosb, wout in ((gs, gw_in, gw_og, gw_osb, gw_out), (ds, dw_in, dw_og, dw_osb, dw_out),
                                             (ms, nm_in, nm_og, nm_osb, nm_out), (vs, nv_in, nv_og, nv_osb, nv_out)):
        nin, nv, ws, bs, nf = small(small_pack)
        out += [nin, win[None], nv, ws, bs, wog[None], wosb[None], wout[None], nf]
    return (loss, grad_x, *out)
```

```python
import functools
import math

import jax
import jax.numpy as jnp
from jax import lax
from jax.experimental import pallas as pl
from jax.experimental.pallas import tpu as pltpu

F32 = jnp.float32
BF16 = jnp.bfloat16
EPS = 1e-6
HEADS = 8
N_SPLIT = 9
N_CHIP = 4
MESH = pl.DeviceIdType.MESH

ADAM_LR = 0.001
ADAM_B1 = 0.9
ADAM_B2 = 0.999
ADAM_EPS = 1e-08
ADAM_WD = 0.01
ADAM_STEP = 10

VMEM_LIMIT = 56 * 2 ** 20
TM = 256
ATT_T = 256
ATT_HP = 4
TKW = 1024
CHUNKS_PER_STEP = 4
CAST_STEPS = 8

NT = (((1,), (1,)), ((), ()))
TN = (((0,), (0,)), ((), ()))


def _params(sem):
    return pltpu.CompilerParams(dimension_semantics=sem, vmem_limit_bytes=VMEM_LIMIT)


def _resident(shape):
    nd = len(shape)
    return pl.BlockSpec(shape, lambda *_: (0,) * nd, pipeline_mode=pl.Buffered(1))


def _const(shape):
    nd = len(shape)
    return pl.BlockSpec(shape, lambda *_: (0,) * nd)


def _segments(d, ncol):
    segs = []
    edges = sorted({j * ncol for j in range(N_CHIP + 1)} | {s * d for s in range(N_SPLIT + 1)})
    for lo, hi in zip(edges[:-1], edges[1:]):
        segs.append((lo // ncol, lo % ncol, lo // d, lo % d, hi - lo))
    return segs


def _sigmoid(x):
    return 1.0 / (1.0 + jnp.exp(-x))


_GELU_C = math.sqrt(2.0 / math.pi)


def _gelu(x):
    return 0.5 * x * (1.0 + jnp.tanh(_GELU_C * (x + 0.044715 * (x * x * x))))


def _gelu_and_grad(x):
    x2 = x * x
    th = jnp.tanh(_GELU_C * (x + 0.044715 * (x2 * x)))
    val = 0.5 * x * (1.0 + th)
    grad = 0.5 * (1.0 + th) + 0.5 * x * (1.0 - th * th) * (_GELU_C * (1.0 + 3.0 * 0.044715 * x2))
    return val, grad


def _split_bf16(a):
    hi = a.astype(BF16)
    lo = (a - hi.astype(F32)).astype(BF16)
    return hi, lo


def _in_proj_fwd(x2d, g_in, wg):
    t, d = x2d.shape
    ncol = wg.shape[2]
    segs = _segments(d, ncol)

    def body(x_ref, g_ref, w_ref, apre_ref, qkv_ref, gpre_ref, ht_ref):
        x = x_ref[...]
        r = lax.rsqrt(jnp.mean(x * x, axis=-1, keepdims=True) + EPS)
        hf = x * r * g_ref[...]
        h = hf.astype(BF16)
        ht_ref[...] = hf.T.astype(BF16)
        outs = (apre_ref, qkv_ref, gpre_ref)
        for chip, c0, grp, s0, width in segs:
            res = jnp.dot(h, w_ref[chip, :, c0:c0 + width], preferred_element_type=F32)
            o = outs[grp // 3]
            off = (grp % 3) * d + s0
            o[:, off:off + width] = res.astype(o.dtype)

    return pl.pallas_call(
        body, name="in_proj_fwd",
        grid=(t // TM,),
        in_specs=[pl.BlockSpec((TM, d), lambda i: (i, 0)), _const((1, d)), _resident(wg.shape)],
        out_specs=[pl.BlockSpec((TM, 3 * d), lambda i: (i, 0)), pl.BlockSpec((TM, 3 * d), lambda i: (i, 0)),
                   pl.BlockSpec((TM, 3 * d), lambda i: (i, 0)), pl.BlockSpec((d, TM), lambda i: (0, i))],
        out_shape=[jax.ShapeDtypeStruct((t, 3 * d), F32), jax.ShapeDtypeStruct((t, 3 * d), BF16),
                   jax.ShapeDtypeStruct((t, 3 * d), F32), jax.ShapeDtypeStruct((d, t), BF16)],
        compiler_params=_params(("arbitrary",)),
    )(x2d, g_in, wg)


def _branch_a_fwd(a_pre, g_v, wm, b_t):
    t, d3 = a_pre.shape
    d = d3 // 3
    ng, chunk, _ = wm.shape
    cw = d // ng

    per_step = CHUNKS_PER_STEP if t % (CHUNKS_PER_STEP * chunk) == 0 else 1

    def body(a_ref, gv_ref, wm_ref, bt_ref, ya_ref):
        for n in range(per_step):
            rows = slice(n * chunk, (n + 1) * chunk)
            ua = _gelu(a_ref[rows, 0:d])
            vg = _gelu(a_ref[rows, d:2 * d])
            za = a_ref[rows, 2 * d:3 * d]
            rv = lax.rsqrt(jnp.mean(vg * vg, axis=-1, keepdims=True) + EPS)
            va = (vg * rv * gv_ref[...]).astype(BF16)
            gate = ua * (za * _sigmoid(za))
            for g in range(ng):
                sl = slice(g * cw, (g + 1) * cw)
                mixed = jnp.dot(wm_ref[g], va[:, sl], preferred_element_type=F32) + bt_ref[:, g:g + 1]
                ya_ref[rows, sl] = (gate[:, sl] * mixed).astype(BF16)

    tile = per_step * chunk
    return pl.pallas_call(
        body, name="branch_a_fwd",
        grid=(t // tile,),
        in_specs=[pl.BlockSpec((tile, d3), lambda i: (i, 0)), _const((1, d)), _const(wm.shape), _const(b_t.shape)],
        out_specs=pl.BlockSpec((tile, d), lambda i: (i, 0)),
        out_shape=jax.ShapeDtypeStruct((t, d), BF16),
        compiler_params=_params(("arbitrary",)),
    )(a_pre, g_v, wm, b_t)


def _branch_a_bwd(a_pre, dya, g_v, wm, wm_t, b_t):
    t, d3 = a_pre.shape
    d = d3 // 3
    ng, chunk, _ = wm.shape
    cw = d // ng
    nsteps = t // chunk

    def body(a_ref, dya_ref, gv_ref, wm_ref, wmt_ref, bt_ref, da_ref, gws_ref, gbt_ref, gnv_ref, db_acc):
        i = pl.program_id(0)

        @pl.when(i == 0)
        def _():
            gws_ref[...] = jnp.zeros_like(gws_ref)
            gnv_ref[...] = jnp.zeros_like(gnv_ref)
            db_acc[...] = jnp.zeros_like(db_acc)

        ua, dgelu_u = _gelu_and_grad(a_ref[:, 0:d])
        vg, dgelu_v = _gelu_and_grad(a_ref[:, d:2 * d])
        za = a_ref[:, 2 * d:3 * d]
        sig = _sigmoid(za)
        sz = za * sig
        dsz = sig * (1.0 + za * (1.0 - sig))
        rv = lax.rsqrt(jnp.mean(vg * vg, axis=-1, keepdims=True) + EPS)
        nv = vg * rv
        gv = gv_ref[...]
        va = (nv * gv).astype(BF16)
        dya = dya_ref[...]
        dmix = dya * ua * sz
        db_acc[...] += dmix
        dmix_b = dmix.astype(BF16)
        t_gate = dya * sz
        t_z = dya * ua * dsz
        dva_parts = []
        for g in range(ng):
            sl = slice(g * cw, (g + 1) * cw)
            mixed = jnp.dot(wm_ref[g], va[:, sl], preferred_element_type=F32) + bt_ref[:, g:g + 1]
            da_ref[:, sl] = (t_gate[:, sl] * mixed * dgelu_u[:, sl]).astype(BF16)
            da_ref[:, 2 * d + g * cw:2 * d + (g + 1) * cw] = (t_z[:, sl] * mixed).astype(BF16)
            gws_ref[g] += lax.dot_general(dmix_b[:, sl], va[:, sl], NT, preferred_element_type=F32)
            dva_parts.append(jnp.dot(wmt_ref[g], dmix_b[:, sl], preferred_element_type=F32))
        dva = jnp.concatenate(dva_parts, axis=1)
        gnv_ref[...] += jnp.sum(dva * nv, axis=0, keepdims=True)
        dnv = dva * gv
        dvg = rv * (dnv - nv * jnp.mean(dnv * nv, axis=-1, keepdims=True))
        da_ref[:, d:2 * d] = (dvg * dgelu_v).astype(BF16)

        @pl.when(i == nsteps - 1)
        def _():
            acc = db_acc[...]
            for g in range(ng):
                gbt_ref[:, g:g + 1] = jnp.sum(acc[:, g * cw:(g + 1) * cw], axis=1, keepdims=True)

    return pl.pallas_call(
        body, name="branch_a_bwd",
        grid=(nsteps,),
        in_specs=[pl.BlockSpec((chunk, d3), lambda i: (i, 0)), pl.BlockSpec((chunk, d), lambda i: (i, 0)),
                  _const((1, d)), _const(wm.shape), _const(wm_t.shape), _const(b_t.shape)],
        out_specs=[pl.BlockSpec((chunk, d3), lambda i: (i, 0)), _const(wm.shape), _const(b_t.shape), _const((1, d))],
        out_shape=[jax.ShapeDtypeStruct((t, d3), BF16), jax.ShapeDtypeStruct(wm.shape, F32),
                   jax.ShapeDtypeStruct(b_t.shape, F32), jax.ShapeDtypeStruct((1, d), F32)],
        scratch_shapes=[pltpu.VMEM((chunk, d), F32)],
        compiler_params=_params(("arbitrary",)),
    )(a_pre, dya, g_v, wm, wm_t, b_t)


def _tri(n, rows_gt_cols):
    r = lax.broadcasted_iota(jnp.int32, (n, n), 0)
    c = lax.broadcasted_iota(jnp.int32, (n, n), 1)
    return (r > c) if rows_gt_cols else (r < c)


def _twice(tri):
    t = tri.astype(BF16)
    return jnp.concatenate([t, t], axis=0)


def _cumsum_mm(a, tri2):
    hi, lo = _split_bf16(a)
    return jnp.dot(jnp.concatenate([hi, lo], axis=1), tri2, preferred_element_type=F32)


LOG2E = 1.4426950408889634
_SIGN = 0x80000000


def _sb_block(q, k, scale, upper2, causal):
    z2 = lax.dot_general(q, k, NT, preferred_element_type=F32) * (scale * LOG2E)
    neg_abs = lax.bitcast_convert_type(lax.bitcast_convert_type(z2, jnp.uint32) | jnp.uint32(_SIGN), F32)
    l2 = jnp.log(1.0 + jnp.exp2(neg_abs)) * LOG2E
    log_beta = jnp.minimum(z2, 0.0) - l2
    lom = log_beta - z2
    if causal is not None:
        lom = jnp.where(causal, lom, 0.0)
    sx = _cumsum_mm(lom, upper2)
    return log_beta, sx, sx[:, 0:1] + lom[:, 0:1]


DEAD_LOG2 = -150.0


def _max_carry(carries):
    return jnp.max(functools.reduce(jnp.maximum, carries))


def _attn_specs(d, seq, nq):
    hp_w = ATT_HP * (d // HEADS)
    n_hp = d // hp_w
    row_blk = pl.BlockSpec((ATT_T, hp_w), lambda b, h, i: (b * nq + i, h))
    k_blk = pl.BlockSpec((seq, hp_w), lambda b, h, i: (b, n_hp + h))
    v_blk = pl.BlockSpec((seq, hp_w), lambda b, h, i: (b, 2 * n_hp + h))
    seq_blk = pl.BlockSpec((seq, hp_w), lambda b, h, i: (b, h))
    return row_blk, k_blk, v_blk, seq_blk, n_hp


def _attn_fwd(qkv, g_pre, bsz, seq):
    t, d3 = qkv.shape
    d = d3 // 3
    hd = d // HEADS
    nq = seq // ATT_T
    scale = hd ** -0.5
    row_blk, k_blk, v_blk, _, n_hp = _attn_specs(d, seq, nq)

    def body(q_ref, k_ref, v_ref, zb_ref, o_ref, yb_ref):
        i = pl.program_id(2)
        causal = _tri(ATT_T, True)
        upper2 = _twice(causal)

        def step(kb, state, mask):
            rows = pl.ds(pl.multiple_of(kb * ATT_T, ATT_T), ATT_T)
            heads = [slice(h * hd, (h + 1) * hd) for h in range(ATT_HP)]
            scores = [_sb_block(q_ref[:, cols], k_ref[rows, cols], scale, upper2, mask) for cols in heads]
            new = []
            for cols, (carry, acc), (log_beta, sx, total) in zip(heads, state, scores):
                a = jnp.exp2(log_beta + sx + carry)
                if mask is not None:
                    a = jnp.where(mask, a, 0.0)
                acc = acc + jnp.dot(a.astype(BF16), v_ref[rows, cols], preferred_element_type=F32)
                new.append((carry + total, acc))
            return tuple(new)

        init = tuple((jnp.zeros((ATT_T, 1), F32), jnp.zeros((ATT_T, hd), F32)) for _ in range(ATT_HP))
        state = step(i, init, causal)
        def more(c):
            new = step(c[0], c[1], None)
            return c[0] - 1, new, _max_carry([s[0] for s in new])

        _, state, _ = lax.while_loop(lambda c: (c[0] >= 0) & (c[2] > DEAD_LOG2), more,
                                     (i - 1, state, _max_carry([s[0] for s in state])))
        for h in range(ATT_HP):
            cols = slice(h * hd, (h + 1) * hd)
            acc = state[h][1]
            zb = zb_ref[:, cols]
            o_ref[:, cols] = acc
            yb_ref[:, cols] = (acc * (zb * _sigmoid(zb))).astype(BF16)

    return pl.pallas_call(
        body, name="attn_fwd",
        grid=(bsz, n_hp, nq),
        in_specs=[row_blk, k_blk, v_blk, row_blk],
        out_specs=[row_blk, row_blk],
        out_shape=[jax.ShapeDtypeStruct((t, d), F32), jax.ShapeDtypeStruct((t, d), BF16)],
        compiler_params=_params(("arbitrary", "arbitrary", "arbitrary")),
    )(qkv, qkv, qkv, g_pre)


def _attn_bwd(qkv, g_pre, o, dyb, bsz, seq):
    t, d3 = qkv.shape
    d = d3 // 3
    hd = d // HEADS
    nq = seq // ATT_T
    scale = hd ** -0.5
    row_blk, k_blk, v_blk, seq_blk, n_hp = _attn_specs(d, seq, nq)

    def body(q_ref, k_ref, v_ref, zb_ref, o_ref, dyb_ref, dq_ref, dk_ref, dv_ref, dzb_ref,
             g_s, beta_s, dkt_acc, dvt_acc):
        i = pl.program_id(2)

        @pl.when(i == 0)
        def _():
            dkt_acc[...] = jnp.zeros_like(dkt_acc)
            dvt_acc[...] = jnp.zeros_like(dvt_acc)

        causal = _tri(ATT_T, True)
        upper2 = _twice(causal)
        lower2 = _twice(~causal)
        zb = zb_ref[...]
        sig = _sigmoid(zb)
        dyb_t = dyb_ref[...]
        do_f = dyb_t * (zb * sig)
        do = do_f.astype(BF16)
        do_t = do_f.T.astype(BF16)
        q_t = q_ref[...].astype(F32).T.astype(BF16)
        dzb_ref[...] = (dyb_t * o_ref[...] * (sig * (1.0 + zb * (1.0 - sig)))).astype(BF16)

        def sweep(kb, carries, mask):
            rows = pl.ds(pl.multiple_of(kb * ATT_T, ATT_T), ATT_T)
            heads = [slice(h * hd, (h + 1) * hd) for h in range(ATT_HP)]
            scores = [_sb_block(q_ref[:, cols], k_ref[rows, cols], scale, upper2, mask) for cols in heads]
            das = [lax.dot_general(do[:, cols], v_ref[rows, cols], NT, preferred_element_type=F32) for cols in heads]
            new = []
            for h, (cols, carry, (log_beta, sx, total), da) in enumerate(zip(heads, carries, scores, das)):
                a = jnp.exp2(log_beta + sx + carry)
                beta = jnp.exp2(log_beta)
                if mask is not None:
                    a = jnp.where(mask, a, 0.0)
                    beta = jnp.where(mask, beta, 0.0)
                g_s[h, kb] = a * da
                beta_s[h, kb] = beta
                dvt_acc[kb, cols, :] += jnp.dot(do_t[cols, :], a.astype(BF16), preferred_element_type=F32)
                new.append(carry + total)
            return tuple(new)

        carries = sweep(i, tuple(jnp.zeros((ATT_T, 1), F32) for _ in range(ATT_HP)), causal)

        def more(c):
            new = sweep(c[0], c[1], None)
            return c[0] - 1, new, _max_carry(new)

        last, _, _ = lax.while_loop(lambda c: (c[0] >= 0) & (c[2] > DEAD_LOG2), more, (i - 1, carries, _max_carry(carries)))
        first_kb = last + 1

        def back(kb, state):
            rows = pl.ds(pl.multiple_of(kb * ATT_T, ATT_T), ATT_T)
            heads = [slice(h * hd, (h + 1) * hd) for h in range(ATT_HP)]
            sums = [_cumsum_mm(g_s[h, kb], lower2) for h in range(ATT_HP)]
            new = []
            for h, (cols, (p_carry, dq), px) in enumerate(zip(heads, state, sums)):
                dz = ((g_s[h, kb] - (p_carry + px) * beta_s[h, kb]) * scale).astype(BF16)
                dq = dq + jnp.dot(dz, k_ref[rows, cols], preferred_element_type=F32)
                dkt_acc[kb, cols, :] += jnp.dot(q_t[cols, :], dz, preferred_element_type=F32)
                new.append((p_carry + px[:, ATT_T - 1:ATT_T], dq))
            return tuple(new)

        init = tuple((jnp.zeros((ATT_T, 1), F32), jnp.zeros((ATT_T, hd), F32)) for _ in range(ATT_HP))
        state = lax.fori_loop(first_kb, i + 1, back, init)
        for h in range(ATT_HP):
            dq_ref[:, h * hd:(h + 1) * hd] = state[h][1].astype(BF16)

        @pl.when(i == nq - 1)
        def _():
            for kb in range(nq):
                dk_ref[kb * ATT_T:(kb + 1) * ATT_T, :] = dkt_acc[kb].T.astype(BF16)
                dv_ref[kb * ATT_T:(kb + 1) * ATT_T, :] = dvt_acc[kb].T.astype(BF16)

    out = jax.ShapeDtypeStruct((t, d), BF16)
    hp_w = ATT_HP * hd
    return pl.pallas_call(
        body, name="attn_bwd",
        grid=(bsz, n_hp, nq),
        in_specs=[row_blk, k_blk, v_blk, row_blk, row_blk, row_blk],
        out_specs=[row_blk, seq_blk, seq_blk, row_blk],
        out_shape=[out, out, out, out],
        scratch_shapes=[pltpu.VMEM((ATT_HP, nq, ATT_T, ATT_T), F32), pltpu.VMEM((ATT_HP, nq, ATT_T, ATT_T), F32),
                        pltpu.VMEM((nq, hp_w, ATT_T), F32), pltpu.VMEM((nq, hp_w, ATT_T), F32)],
        compiler_params=_params(("arbitrary", "arbitrary", "arbitrary")),
    )(qkv, qkv, qkv, g_pre, o, dyb)


def _out_proj(ya, yb, g_pre, x2d, tgt, w_og, w_osb, w_out, g_f):
    t, d = x2d.shape

    def body(ya_ref, yb_ref, ga_ref, gb_ref, x_ref, tgt_ref, wog_ref, wosb_ref, wout_ref, gf_ref,
             dya_ref, dyb_ref, dg_ref, dx2_ref, loss_ref, gnf_ref, gwog_ref, gwosb_ref, gwout_ref):
        @pl.when(pl.program_id(0) == 0)
        def _():
            loss_ref[...] = jnp.zeros_like(loss_ref)
            gnf_ref[...] = jnp.zeros_like(gnf_ref)
            gwog_ref[...] = jnp.zeros_like(gwog_ref)
            gwosb_ref[...] = jnp.zeros_like(gwosb_ref)
            gwout_ref[...] = jnp.zeros_like(gwout_ref)

        ya = ya_ref[...]
        yb = yb_ref[...]
        pa = jnp.dot(ya, wog_ref[...], preferred_element_type=F32)
        pb = jnp.dot(yb, wosb_ref[...], preferred_element_type=F32)
        sga = _sigmoid(ga_ref[...])
        sgb = _sigmoid(gb_ref[...])
        merged = (sga * pa + sgb * pb).astype(BF16)
        x2 = x_ref[...] + jnp.dot(merged, wout_ref[...], preferred_element_type=F32)
        r2 = lax.rsqrt(jnp.mean(x2 * x2, axis=-1, keepdims=True) + EPS)
        n2 = x2 * r2
        gf = gf_ref[...]
        err = n2 * gf - tgt_ref[...]
        loss_ref[...] += 0.5 * jnp.sum(jnp.sum(err * err, axis=-1, keepdims=True), axis=0, keepdims=True) / d
        dy = err * (1.0 / d)
        gnf_ref[...] += jnp.sum(dy * n2, axis=0, keepdims=True)
        dn = dy * gf
        dx2 = r2 * (dn - n2 * jnp.mean(dn * n2, axis=-1, keepdims=True))
        dx2_ref[...] = dx2
        dx2_b = dx2.astype(BF16)
        dmerged = lax.dot_general(dx2_b, wout_ref[...], NT, preferred_element_type=F32)
        gwout_ref[...] += lax.dot_general(merged, dx2_b, TN, preferred_element_type=F32)
        dg_ref[:, 0:d] = (dmerged * pa * (sga * (1.0 - sga))).astype(BF16)
        dg_ref[:, d:2 * d] = (dmerged * pb * (sgb * (1.0 - sgb))).astype(BF16)
        dpa = (dmerged * sga).astype(BF16)
        dpb = (dmerged * sgb).astype(BF16)
        dya_ref[...] = lax.dot_general(dpa, wog_ref[...], NT, preferred_element_type=F32)
        dyb_ref[...] = lax.dot_general(dpb, wosb_ref[...], NT, preferred_element_type=F32)
        gwog_ref[...] += lax.dot_general(ya, dpa, TN, preferred_element_type=F32)
        gwosb_ref[...] += lax.dot_general(yb, dpb, TN, preferred_element_type=F32)

    row = lambda i: (i, 0)
    return pl.pallas_call(
        body, name="out_proj",
        grid=(t // TM,),
        in_specs=[pl.BlockSpec((TM, d), row), pl.BlockSpec((TM, d), row),
                  pl.BlockSpec((TM, d), lambda i: (i, 1)), pl.BlockSpec((TM, d), lambda i: (i, 2)),
                  pl.BlockSpec((TM, d), row), pl.BlockSpec((TM, d), row),
                  _resident((d, d)), _resident((d, d)), _resident((d, d)), _const((1, d))],
        out_specs=[pl.BlockSpec((TM, d), row), pl.BlockSpec((TM, d), row), pl.BlockSpec((TM, 2 * d), row),
                   pl.BlockSpec((TM, d), row), _const((1, 1)), _const((1, d)),
                   _const((d, d)), _const((d, d)), _const((d, d))],
        out_shape=[jax.ShapeDtypeStruct((t, d), F32), jax.ShapeDtypeStruct((t, d), F32),
                   jax.ShapeDtypeStruct((t, 2 * d), BF16), jax.ShapeDtypeStruct((t, d), F32),
                   jax.ShapeDtypeStruct((1, 1), F32), jax.ShapeDtypeStruct((1, d), F32),
                   jax.ShapeDtypeStruct((d, d), F32), jax.ShapeDtypeStruct((d, d), F32),
                   jax.ShapeDtypeStruct((d, d), F32)],
        compiler_params=_params(("arbitrary",)),
    )(ya, yb, g_pre, g_pre, x2d, tgt, w_og, w_osb, w_out, g_f)


def _dproj_pieces(d):
    return [(0, 0, 3), (1, 3, 1), (2, 4, 1), (3, 5, 1), (4, 6, 1), (5, 7, 2)]


def _in_proj_bwd_x(pieces, wg, x2d, g_in, dx2, gw16):
    t, d = x2d.shape
    ncol = wg.shape[2]
    segs = _segments(d, ncol)
    layout = _dproj_pieces(d)
    nsteps = t // TM

    def body(da_ref, dq_ref, dk_ref, dv_ref, dzb_ref, dg_ref, w_ref, x_ref, g_ref, dx2_ref, gw16_ref,
             gx_ref, gn_ref, recv_ref, send_sems, recv_sems):
        x_pos, y_pos, c_pos, chips = _place()

        def share(k, chunk):
            px, py = chips[k]
            return pltpu.make_async_remote_copy(
                src_ref=gw16_ref.at[:, chunk * ncol:(chunk + 1) * ncol], dst_ref=recv_ref.at[k],
                send_sem=send_sems.at[k], recv_sem=recv_sems.at[k], device_id=(px, py, c_pos), device_id_type=MESH)

        @pl.when(pl.program_id(0) == 0)
        def _():
            gn_ref[...] = jnp.zeros_like(gn_ref)
            for k, (px, py) in enumerate(chips):
                for chunk in range(N_CHIP):
                    @pl.when(2 * px + py == chunk)
                    def _(k=k, chunk=chunk):
                        share(k, chunk).start()

        @pl.when(pl.program_id(0) == nsteps - 1)
        def _():
            for k in range(N_CHIP - 1):
                share(k, 0).wait()

        refs = (da_ref, dq_ref, dk_ref, dv_ref, dzb_ref, dg_ref)
        dh = jnp.zeros((TM, d), F32)
        for chip, c0, grp, s0, width in segs:
            piece, first, _ = next(p for p in layout if p[1] <= grp < p[1] + p[2])
            off = (grp - first) * d + s0
            dh = dh + lax.dot_general(refs[piece][:, off:off + width], w_ref[chip, :, c0:c0 + width], NT,
                                      preferred_element_type=F32)
        x = x_ref[...]
        r = lax.rsqrt(jnp.mean(x * x, axis=-1, keepdims=True) + EPS)
        n = x * r
        gn_ref[...] += jnp.sum(dh * n, axis=0, keepdims=True)
        dn = dh * g_ref[...]
        gx_ref[...] = dx2_ref[...] + r * (dn - n * jnp.mean(dn * n, axis=-1, keepdims=True))

    row = lambda i: (i, 0)
    return pl.pallas_call(
        body, name="in_proj_bwd_x",
        grid=(t // TM,),
        in_specs=[pl.BlockSpec((TM, p.shape[1]), row) for p in pieces]
        + [_resident(wg.shape), pl.BlockSpec((TM, d), row), _const((1, d)), pl.BlockSpec((TM, d), row), ANY],
        out_specs=[pl.BlockSpec((TM, d), row), _const((1, d)), ANY],
        out_shape=[jax.ShapeDtypeStruct((t, d), F32), jax.ShapeDtypeStruct((1, d), F32),
                   jax.ShapeDtypeStruct((N_CHIP - 1, d, ncol), BF16)],
        scratch_shapes=[pltpu.SemaphoreType.DMA((N_CHIP - 1,)), pltpu.SemaphoreType.DMA((N_CHIP - 1,))],
        compiler_params=_params(("arbitrary",)),
    )(*pieces, wg, x2d, g_in, dx2, gw16)


def _in_proj_bwd_w(h_t, pieces, mats16, pack):
    d, t = h_t.shape
    nk = t // TKW
    layout = _dproj_pieces(d)
    n_mats = len(mats16)
    n_dev = 2 * N_CHIP
    flips = [(dx, dy, dc) for dx in (0, 1) for dy in (0, 1) for dc in (0, 1)][1:]

    def body(ht_ref, da_ref, dq_ref, dk_ref, dv_ref, dzb_ref, dg_ref, *rest):
        mat_refs, pack_ref = rest[:n_mats], rest[n_mats]
        gw_ref, gw16_ref = rest[n_mats + 1:n_mats + 3]
        recv_refs, slots_ref = rest[n_mats + 3:2 * n_mats + 3], rest[2 * n_mats + 3]
        acc, mat_send, mat_recv, pack_send, pack_recv, own_sem = rest[2 * n_mats + 4:]
        s = pl.program_id(0)
        i = pl.program_id(1)
        x_pos, y_pos, c_pos, chips = _place()
        me = 4 * x_pos + 2 * y_pos + c_pos

        def exchanges():
            cps = []
            for k, (px, py) in enumerate(chips):
                for a in range(n_mats):
                    cps.append(pltpu.make_async_remote_copy(
                        src_ref=mat_refs[a].at[2 * px + py], dst_ref=recv_refs[a].at[k],
                        send_sem=mat_send.at[a, k], recv_sem=mat_recv.at[a, k],
                        device_id=(px, py, c_pos), device_id_type=MESH))
            for k, (dx, dy, dc) in enumerate(flips):
                peer = (1 - x_pos if dx else x_pos, 1 - y_pos if dy else y_pos, 1 - c_pos if dc else c_pos)
                cps.append(pltpu.make_async_remote_copy(
                    src_ref=pack_ref, dst_ref=slots_ref.at[me], send_sem=pack_send.at[k], recv_sem=pack_recv.at[k],
                    device_id=peer, device_id_type=MESH))
            return cps, pltpu.make_async_copy(pack_ref, slots_ref.at[me], own_sem)

        @pl.when((s == 0) & (i == 0))
        def _():
            cps, own = exchanges()
            own.start()
            for cp in cps:
                cp.start()

        @pl.when((s == N_SPLIT - 1) & (i == nk - 1))
        def _():
            cps, own = exchanges()
            own.wait()
            for cp in cps:
                cp.wait()

        @pl.when(i == 0)
        def _():
            acc[...] = jnp.zeros_like(acc)

        refs = (da_ref, dq_ref, dk_ref, dv_ref, dzb_ref, dg_ref)
        for piece, first, count in layout:
            @pl.when((s >= first) & (s < first + count))
            def _(piece=piece):
                acc[...] += jnp.dot(ht_ref[...], refs[piece][...], preferred_element_type=F32)

        @pl.when(i == nk - 1)
        def _():
            gw_ref[...] = acc[...]
            gw16_ref[...] = acc[...].astype(BF16)

    def piece_spec(p, first, count):
        def index(s, i):
            mine = (s >= first) & (s < first + count)
            return jnp.where(mine, i, 0), jnp.where(mine, s - first, 0)
        return pl.BlockSpec((TKW, d), index)

    col_blk = pl.BlockSpec((d, d), lambda s, i: (0, s))
    outs = pl.pallas_call(
        body, name="in_proj_bwd_w",
        grid=(N_SPLIT, nk),
        in_specs=[pl.BlockSpec((d, TKW), lambda s, i: (0, i))] + [piece_spec(*p) for p in layout] + [ANY] * (n_mats + 1),
        out_specs=[col_blk, col_blk] + [ANY] * (n_mats + 1),
        out_shape=[jax.ShapeDtypeStruct((d, N_SPLIT * d), F32), jax.ShapeDtypeStruct((d, N_SPLIT * d), BF16)]
        + [jax.ShapeDtypeStruct((N_CHIP - 1,) + m.shape[1:], BF16) for m in mats16]
        + [jax.ShapeDtypeStruct((n_dev,) + pack.shape, F32)],
        scratch_shapes=[pltpu.VMEM((d, d), F32),
                        pltpu.SemaphoreType.DMA((n_mats, N_CHIP - 1)), pltpu.SemaphoreType.DMA((n_mats, N_CHIP - 1)),
                        pltpu.SemaphoreType.DMA((n_dev - 1,)), pltpu.SemaphoreType.DMA((n_dev - 1,)),
                        pltpu.SemaphoreType.DMA],
        compiler_params=_params(("arbitrary", "arbitrary")),
    )(h_t, *pieces, *mats16, pack)
    return outs[0], outs[1], outs[2:2 + n_mats], outs[2 + n_mats]


def _local_step(x2d, tgt2d, bsz, seq, norm_in, wg, norm_v, w_s, b_s, w_og, w_osb, w_out, norm_final):
    d = x2d.shape[1]
    chunk = w_s.shape[-1]
    causal = jnp.tril(jnp.ones((chunk, chunk), dtype=bool))
    wm = jnp.where(causal[None], w_s, 0.0).astype(BF16)
    wm_t = jnp.swapaxes(wm, 1, 2)
    b_t = b_s.T

    a_pre, qkv, g_pre, h_t = _in_proj_fwd(x2d, norm_in, wg)
    ya = _branch_a_fwd(a_pre, norm_v, wm, b_t)
    o, yb = _attn_fwd(qkv, g_pre, bsz, seq)
    dya, dyb, dg, dx2, loss, g_nf, g_wog, g_wosb, g_wout = _out_proj(
        ya, yb, g_pre, x2d, tgt2d, w_og, w_osb, w_out, norm_final.reshape(1, d))
    dq, dk, dv, dzb = _attn_bwd(qkv, g_pre, o, dyb, bsz, seq)
    d_a, g_ws, g_bt, g_nv = _branch_a_bwd(a_pre, dya, norm_v, wm, wm_t, b_t)
    g_ws = jnp.where(causal[None], g_ws, 0.0)
    return loss, h_t, (d_a, dq, dk, dv, dzb, dg), dx2, g_nv, g_ws, g_bt.T, g_wog, g_wosb, g_wout, g_nf


def _row_tile(rows):
    return next(r for r in (128, 64, 32, 16, 8) if rows % r == 0)


def _cast_bf16(arrs):
    n = len(arrs)

    def body(*refs):
        for a_ref, o_ref in zip(refs[:n], refs[n:]):
            o_ref[...] = a_ref[...].astype(BF16)

    specs = [pl.BlockSpec((a.shape[0] // CAST_STEPS, a.shape[1]), lambda i: (i, 0)) for a in arrs]
    return pl.pallas_call(
        body, name="cast_bf16", grid=(CAST_STEPS,),
        in_specs=specs, out_specs=specs,
        out_shape=[jax.ShapeDtypeStruct(a.shape, BF16) for a in arrs],
        compiler_params=_params(("arbitrary",)),
    )(*arrs)


def _add_received(full, recv, chip, by_cols):
    _, rows, cols = recv.shape
    tr = _row_tile(rows)
    nb = rows // tr

    def body(chip_ref, own_ref, recv_ref, o_ref):
        s = own_ref[...]
        for k in range(N_CHIP - 1):
            s = s + recv_ref[k].astype(F32)
        o_ref[...] = s

    own_map = (lambda i, chip_ref: (i, chip_ref[0])) if by_cols else (lambda i, chip_ref: (chip_ref[0] * nb + i, 0))
    return pl.pallas_call(
        body, name="add_received",
        grid_spec=pltpu.PrefetchScalarGridSpec(
            num_scalar_prefetch=1, grid=(nb,),
            in_specs=[pl.BlockSpec((tr, cols), own_map),
                      pl.BlockSpec((N_CHIP - 1, tr, cols), lambda i, chip_ref: (0, i, 0))],
            out_specs=pl.BlockSpec((tr, cols), lambda i, chip_ref: (i, 0))),
        out_shape=jax.ShapeDtypeStruct((rows, cols), F32),
        compiler_params=_params(("arbitrary",)),
    )(chip.reshape(1).astype(jnp.int32), full, recv)


def _adamw_math(w, m, v, g):
    new_m = ADAM_B1 * m + (1.0 - ADAM_B1) * g
    new_v = ADAM_B2 * v + (1.0 - ADAM_B2) * (g * g)
    m_hat = new_m / (1.0 - ADAM_B1 ** ADAM_STEP)
    v_hat = new_v / (1.0 - ADAM_B2 ** ADAM_STEP)
    return -ADAM_LR * (m_hat / (jnp.sqrt(v_hat) + ADAM_EPS) + ADAM_WD * w), new_m, new_v


def _adamw(w, m, v, g_parts):
    rows, cols = w.shape
    tr = _row_tile(rows)
    n_parts = len(g_parts)

    def body(*refs):
        w_ref, m_ref, v_ref = refs[:3]
        part_refs = refs[3:3 + n_parts]
        g_ref, d_ref, nm_ref, nv_ref = refs[3 + n_parts:]
        g = part_refs[0][...]
        for p in part_refs[1:]:
            g = g + p[...]
        g_ref[...] = g
        d_ref[...], nm_ref[...], nv_ref[...] = _adamw_math(w_ref[...], m_ref[...], v_ref[...], g)

    spec = pl.BlockSpec((tr, cols), lambda i: (i, 0))
    out = jax.ShapeDtypeStruct(w.shape, F32)
    return pl.pallas_call(
        body, name="adamw", grid=(rows // tr,),
        in_specs=[spec] * (3 + n_parts), out_specs=[spec] * 4, out_shape=[out] * 4,
        compiler_params=_params(("arbitrary",)),
    )(w, m, v, *g_parts)


def _adamw_small(w, m, v, slots_head, slots_tail):
    n_dev, p0, _ = slots_head.shape

    def body(w_ref, m_ref, v_ref, head_ref, tail_ref, g_ref, d_ref, nm_ref, nv_ref):
        for ref, rows in ((head_ref, slice(0, p0)), (tail_ref, slice(p0, w.shape[0]))):
            g = ref[0]
            for i in range(1, n_dev):
                g = g + ref[i]
            g_ref[rows, :] = g
            d_ref[rows, :], nm_ref[rows, :], nv_ref[rows, :] = _adamw_math(w_ref[rows, :], m_ref[rows, :], v_ref[rows, :], g)

    vmem = pl.BlockSpec(memory_space=pltpu.VMEM)
    out = jax.ShapeDtypeStruct(w.shape, F32)
    return pl.pallas_call(
        body, name="adamw_small", in_specs=[vmem] * 5, out_specs=[vmem] * 4, out_shape=[out] * 4,
        compiler_params=pltpu.CompilerParams(vmem_limit_bytes=VMEM_LIMIT),
    )(w, m, v, slots_head, slots_tail)


ANY = pl.BlockSpec(memory_space=pl.ANY)


def _place():
    x, y, c = lax.axis_index("x"), lax.axis_index("y"), lax.axis_index("c")
    other_chips = [(1 - x, y), (x, 1 - y), (1 - x, 1 - y)]
    return x, y, c, other_chips


def _all_gather_halves(halves):
    n = len(halves)

    def body(*refs):
        ins, outs = refs[:n], refs[n:2 * n]
        send_sems, recv_sems, local_sems = refs[2 * n:]
        x, y, c, chips = _place()
        sibling = (x, y, 1 - c)
        my_chip = 2 * x + y

        def copy(a, k, block, to, src=None):
            return pltpu.make_async_remote_copy(
                src_ref=outs[a].at[block] if src is None else src, dst_ref=outs[a].at[block],
                send_sem=send_sems.at[a, k], recv_sem=recv_sems.at[a, k], device_id=to, device_id_type=MESH)

        local = [pltpu.make_async_copy(ins[a], outs[a].at[pl.ds(2 * my_chip, 2)], local_sems.at[a]) for a in range(n)]
        for cp in local:
            cp.start()
        started = []
        for k, (px, py) in enumerate(chips):
            for a in range(n):
                cp = copy(a, k, 2 * my_chip + c, (px, py, c), src=ins[a].at[c])
                cp.start()
                started.append(cp)
        for k, (px, py) in enumerate(chips):
            for a in range(n):
                block = 2 * (2 * px + py) + c
                copy(a, k, block, (x, y, c)).wait_recv()
                cp = copy(a, 3 + k, block, sibling)
                cp.start()
                started.append(cp)
        for k, (px, py) in enumerate(chips):
            for a in range(n):
                copy(a, 3 + k, 2 * (2 * px + py) + 1 - c, (x, y, c)).wait_recv()
        for cp in started:
            cp.wait_send()
        for cp in local:
            cp.wait()

    return pl.pallas_call(
        body, name="all_gather_weights",
        in_specs=[ANY] * n, out_specs=[ANY] * n,
        out_shape=[jax.ShapeDtypeStruct((2 * N_CHIP,) + h.shape[1:], h.dtype) for h in halves],
        scratch_shapes=[pltpu.SemaphoreType.DMA((n, 6)), pltpu.SemaphoreType.DMA((n, 6)), pltpu.SemaphoreType.DMA((n,))],
    )(*halves)


def _swap_and_gather(arrs, pack):
    n = len(arrs)
    n_dev = 2 * N_CHIP
    flips = [(dx, dy, dc) for dx in (0, 1) for dy in (0, 1) for dc in (0, 1)][1:]

    def body(*refs):
        ins, pack_ref = refs[:n], refs[n]
        outs, slots_ref = refs[n + 1:2 * n + 1], refs[2 * n + 1]
        send_sems, recv_sems, pack_send, pack_recv, own_sem = refs[2 * n + 2:]
        x, y, c, _ = _place()
        me = 4 * x + 2 * y + c
        own = pltpu.make_async_copy(pack_ref, slots_ref.at[me], own_sem)
        own.start()
        copies = []
        for k, (dx, dy, dc) in enumerate(flips):
            peer = (1 - x if dx else x, 1 - y if dy else y, 1 - c if dc else c)
            copies.append(pltpu.make_async_remote_copy(
                src_ref=pack_ref, dst_ref=slots_ref.at[me], send_sem=pack_send.at[k], recv_sem=pack_recv.at[k],
                device_id=peer, device_id_type=MESH))
        copies += [pltpu.make_async_remote_copy(
            src_ref=ins[a], dst_ref=outs[a], send_sem=send_sems.at[a], recv_sem=recv_sems.at[a],
            device_id=(x, y, 1 - c), device_id_type=MESH) for a in range(n)]
        for cp in copies:
            cp.start()
        for cp in copies:
            cp.wait()
        own.wait()

    return pl.pallas_call(
        body, name="swap_and_gather",
        in_specs=[ANY] * (n + 1), out_specs=[ANY] * (n + 1),
        out_shape=[jax.ShapeDtypeStruct(a.shape, a.dtype) for a in arrs] + [jax.ShapeDtypeStruct((n_dev,) + pack.shape, F32)],
        scratch_shapes=[pltpu.SemaphoreType.DMA((n,)), pltpu.SemaphoreType.DMA((n,)),
                        pltpu.SemaphoreType.DMA((n_dev - 1,)), pltpu.SemaphoreType.DMA((n_dev - 1,)),
                        pltpu.SemaphoreType.DMA],
    )(*arrs, pack)


SLAB = 8


def _slab(vec, d):
    return jnp.pad(vec.reshape(1, d), ((0, SLAB - 1), (0, 0)))


def _pack_tail(vec_nv, b_s, vec_nf, w_s):
    d = vec_nv.shape[-1]
    return jnp.concatenate([_slab(v, d) for v in (vec_nv, b_s, vec_nf)] + [w_s.reshape(-1, d)], axis=0)


def _pack_small(vec_nin, vec_nv, b_s, vec_nf, w_s):
    return jnp.concatenate([_slab(vec_nin, vec_nin.shape[-1]), _pack_tail(vec_nv, b_s, vec_nf, w_s)], axis=0)


def _unpack_small(pack, w_s_shape, b_s_shape):
    return (pack[0:1], pack[SLAB:SLAB + 1], pack[2 * SLAB].reshape(b_s_shape), pack[3 * SLAB],
            pack[4 * SLAB:].reshape(w_s_shape))


def kernel(x, norm_in, w_in, norm_v, w_s, b_s, w_o_gmlp, w_o_sb, w_out, norm_final, loss_target, m_norm_in, m_w_in, m_norm_v, m_w_s, m_b_s, m_w_o_gmlp, m_w_o_sb, m_w_out, m_norm_final, v_norm_in, v_w_in, v_norm_v, v_w_s, v_b_s, v_w_o_gmlp, v_w_o_sb, v_w_out, v_norm_final):
    d = x.shape[-1]
    ncol = w_in.shape[-1]
    nrow = w_o_gmlp.shape[-2]
    chip = 2 * lax.axis_index("x") + lax.axis_index("y")

    shards = [w_in[0], w_o_gmlp[0], w_o_sb[0], w_out[0]]
    halves = [s16.reshape(2, s16.shape[0] // 2, s16.shape[1]) for s16 in _cast_bf16(shards)]
    wg, w_og, w_osb, w_o = _all_gather_halves(halves)
    wg = wg.reshape(N_CHIP, d, ncol)

    bsz, seq, _ = x.shape
    x2d = x.reshape(bsz * seq, d)
    loss, h_t, pieces, dx2, g_nv, g_ws, g_bs, g_wog, g_wosb, g_wout, g_nf = _local_step(
        x2d, loss_target.reshape(bsz * seq, d), bsz, seq, norm_in, wg, norm_v, w_s[0], b_s[0],
        w_og.reshape(d, d), w_osb.reshape(d, d), w_o.reshape(d, d), norm_final)

    mats = [g_wog, g_wosb, g_wout]
    mats16 = [g16.reshape(N_CHIP, nrow, d) for g16 in _cast_bf16(mats)]
    g_win, g_win16, recv_mats, slots_tail = _in_proj_bwd_w(h_t, pieces, mats16, _pack_tail(g_nv, g_bs, g_nf, g_ws))
    grad_x, g_nin, recv_win = _in_proj_bwd_x(pieces, wg, x2d, norm_in, dx2, g_win16)
    grad_x = grad_x.reshape(bsz, seq, d)

    sums = [_add_received(g_win, recv_win, chip, True)] + [
        _add_received(g, r, chip, False) for g, r in zip(mats, recv_mats)]
    *sibling_sums, slots_head = _swap_and_gather(sums, _slab(g_nin, d))
    stats = []
    for w, m, v, mine, theirs in zip(shards, [m_w_in[0], m_w_o_gmlp[0], m_w_o_sb[0], m_w_out[0]],
                                     [v_w_in[0], v_w_o_gmlp[0], v_w_o_sb[0], v_w_out[0]], sums, sibling_sums):
        stats.append(_adamw(w, m, v, [mine, theirs]))
    (gw_in, dw_in, nm_in, nv_in), (gw_og, dw_og, nm_og, nv_og), (gw_osb, dw_osb, nm_osb, nv_osb), \
        (gw_out, dw_out, nm_out, nv_out) = stats

    gs, ds, ms, vs = _adamw_small(
        _pack_small(norm_in, norm_v, b_s[0], norm_final, w_s[0]),
        _pack_small(m_norm_in, m_norm_v, m_b_s[0], m_norm_final, m_w_s[0]),
        _pack_small(v_norm_in, v_norm_v, v_b_s[0], v_norm_final, v_w_s[0]), slots_head, slots_tail)

    def small(pack):
        nin, nv, bs, nf, ws = _unpack_small(pack, w_s.shape, b_s.shape)
        return nin, nv, ws, bs, nf

    loss = lax.psum(loss[0, 0], ("x", "y", "c"))
    out = []
    for small_pack, win, wog, wosb, wout in ((gs, gw_in, gw_og, gw_osb, gw_out), (ds, dw_in, dw_og, dw_osb, dw_out),
                                             (ms, nm_in, nm_og, nm_osb, nm_out), (vs, nv_in, nv_og, nv_osb, nv_out)):
        nin, nv, ws, bs, nf = small(small_pack)
        out += [nin, win[None], nv, ws, bs, wog[None], wosb[None], wout[None], nf]
    return (loss, grad_x, *out)
```

```python
import functools
import math

import jax
import jax.numpy as jnp
from jax import lax
from jax.experimental import pallas as pl
from jax.experimental.pallas import tpu as pltpu

F32 = jnp.float32
BF16 = jnp.bfloat16
EPS = 1e-6
HEADS = 8
N_SPLIT = 9
N_CHIP = 4
MESH = pl.DeviceIdType.MESH

ADAM_LR = 0.001
ADAM_B1 = 0.9
ADAM_B2 = 0.999
ADAM_EPS = 1e-08
ADAM_WD = 0.01
ADAM_STEP = 10

VMEM_LIMIT = 56 * 2 ** 20
TM = 256
ATT_T = 256
ATT_HP = 4
TKW = 1024
CHUNKS_PER_STEP = 4
CAST_STEPS = 8

NT = (((1,), (1,)), ((), ()))
TN = (((0,), (0,)), ((), ()))


def _params(sem):
    return pltpu.CompilerParams(dimension_semantics=sem, vmem_limit_bytes=VMEM_LIMIT)


def _resident(shape):
    nd = len(shape)
    return pl.BlockSpec(shape, lambda *_: (0,) * nd, pipeline_mode=pl.Buffered(1))


def _const(shape):
    nd = len(shape)
    return pl.BlockSpec(shape, lambda *_: (0,) * nd)


def _segments(d, ncol):
    segs = []
    edges = sorted({j * ncol for j in range(N_CHIP + 1)} | {s * d for s in range(N_SPLIT + 1)})
    for lo, hi in zip(edges[:-1], edges[1:]):
        segs.append((lo // ncol, lo % ncol, lo // d, lo % d, hi - lo))
    return segs


def _sigmoid(x):
    return 1.0 / (1.0 + jnp.exp(-x))


_GELU_C = math.sqrt(2.0 / math.pi)


def _gelu(x):
    return 0.5 * x * (1.0 + jnp.tanh(_GELU_C * (x + 0.044715 * (x * x * x))))


def _gelu_and_grad(x):
    x2 = x * x
    th = jnp.tanh(_GELU_C * (x + 0.044715 * (x2 * x)))
    val = 0.5 * x * (1.0 + th)
    grad = 0.5 * (1.0 + th) + 0.5 * x * (1.0 - th * th) * (_GELU_C * (1.0 + 3.0 * 0.044715 * x2))
    return val, grad


def _split_bf16(a):
    hi = a.astype(BF16)
    lo = (a - hi.astype(F32)).astype(BF16)
    return hi, lo


def _in_proj_fwd(x2d, g_in, halves):
    t, d = x2d.shape
    n = len(halves)
    ncol = halves[0].shape[2]
    n_row = t // TM
    assert halves[0].shape[1] * 2 == d

    def body(order_ref, x_ref, g_ref, *rest):
        ins = rest[:n]
        proj_ref, ht_ref = rest[n:n + 2]
        outs = rest[n + 2:2 * n + 2]
        wbuf, send_sems, recv_sems, local_sems, load_sem = rest[2 * n + 2:]
        phase = pl.program_id(0)
        i = pl.program_id(1)
        x_pos, y_pos, c_pos, chips = _place()
        sibling = (x_pos, y_pos, 1 - c_pos)
        me = (x_pos, y_pos, c_pos)
        my_chip = 2 * x_pos + y_pos

        def copy(a, k, block, to, src=None):
            return pltpu.make_async_remote_copy(
                src_ref=outs[a].at[block] if src is None else src, dst_ref=outs[a].at[block],
                send_sem=send_sems.at[a, k], recv_sem=recv_sems.at[a, k], device_id=to, device_id_type=MESH)

        def local(a):
            return pltpu.make_async_copy(ins[a], outs[a].at[pl.ds(2 * my_chip, 2)], local_sems.at[a])

        def load(src, first, slot):
            for half in range(2):
                cp = pltpu.make_async_copy(src.at[first + half], wbuf.at[slot, pl.ds(half * (d // 2), d // 2)], load_sem)
                cp.start()
                cp.wait()

        @pl.when((phase == 0) & (i == 0))
        def _():
            for a in range(n):
                local(a).start()
            for k, (px, py) in enumerate(chips):
                for a in range(n):
                    copy(a, k, 2 * my_chip + c_pos, (px, py, c_pos), src=ins[a].at[c_pos]).start()
            load(ins[0], 0, 0)

        for k, (px, py) in enumerate(chips):
            @pl.when((phase == k + 1) & (i == 0))
            def _(k=k, px=px, py=py):
                theirs = 2 * (2 * px + py)
                for a in range(n):
                    copy(a, k, theirs + c_pos, me).wait_recv()
                    copy(a, 3 + k, theirs + c_pos, sibling).start()
                for a in range(n):
                    copy(a, 3 + k, theirs + 1 - c_pos, me).wait_recv()
                load(outs[0], theirs, (k + 1) % 2)

        x = x_ref[...]
        r = lax.rsqrt(jnp.mean(x * x, axis=-1, keepdims=True) + EPS)
        hf = x * r * g_ref[...]
        h = hf.astype(BF16)

        @pl.when(phase == 0)
        def _():
            ht_ref[...] = hf.T.astype(BF16)

        for slot in range(2):
            @pl.when(phase % 2 == slot)
            def _(slot=slot):
                proj_ref[...] = jnp.dot(h, wbuf[slot], preferred_element_type=F32)

        @pl.when((phase == N_CHIP - 1) & (i == n_row - 1))
        def _():
            for a in range(n):
                for k in range(2 * (N_CHIP - 1)):
                    copy(a, k, 0, me).wait_send()
                local(a).wait()

    x_pos, y_pos = lax.axis_index("x"), lax.axis_index("y")
    order = jnp.stack([2 * x_pos + y_pos, 2 * (1 - x_pos) + y_pos, 2 * x_pos + 1 - y_pos,
                       2 * (1 - x_pos) + 1 - y_pos]).astype(jnp.int32)
    last = n_row - 1
    outs = pl.pallas_call(
        body, name="in_proj_fwd",
        grid_spec=pltpu.PrefetchScalarGridSpec(
            num_scalar_prefetch=1, grid=(N_CHIP, n_row),
            in_specs=[pl.BlockSpec((TM, d), lambda p, i, order: (i, 0)),
                      pl.BlockSpec((1, d), lambda p, i, order: (0, 0))] + [ANY] * n,
            out_specs=[pl.BlockSpec((TM, ncol), lambda p, i, order: (i, order[p])),
                       pl.BlockSpec((d, TM), lambda p, i, order: (0, jnp.where(p == 0, i, last)))] + [ANY] * n,
            scratch_shapes=[pltpu.VMEM((2, d, ncol), BF16),
                            pltpu.SemaphoreType.DMA((n, 2 * (N_CHIP - 1))), pltpu.SemaphoreType.DMA((n, 2 * (N_CHIP - 1))),
                            pltpu.SemaphoreType.DMA((n,)), pltpu.SemaphoreType.DMA]),
        out_shape=[jax.ShapeDtypeStruct((t, N_CHIP * ncol), F32), jax.ShapeDtypeStruct((d, t), BF16)]
        + [jax.ShapeDtypeStruct((2 * N_CHIP,) + hv.shape[1:], BF16) for hv in halves],
        compiler_params=_params(("arbitrary", "arbitrary")),
    )(order, x2d, g_in, *halves)
    return outs[0], outs[1], outs[2:]


def _branch_a_fwd(a_pre, g_v, wm, b_t):
    t = a_pre.shape[0]
    d = g_v.shape[1]
    d3 = 3 * d
    ng, chunk, _ = wm.shape
    cw = d // ng

    per_step = CHUNKS_PER_STEP if t % (CHUNKS_PER_STEP * chunk) == 0 else 1

    def body(a_ref, gv_ref, wm_ref, bt_ref, ya_ref):
        for n in range(per_step):
            rows = slice(n * chunk, (n + 1) * chunk)
            ua = _gelu(a_ref[rows, 0:d])
            vg = _gelu(a_ref[rows, d:2 * d])
            za = a_ref[rows, 2 * d:3 * d]
            rv = lax.rsqrt(jnp.mean(vg * vg, axis=-1, keepdims=True) + EPS)
            va = (vg * rv * gv_ref[...]).astype(BF16)
            gate = ua * (za * _sigmoid(za))
            for g in range(ng):
                sl = slice(g * cw, (g + 1) * cw)
                mixed = jnp.dot(wm_ref[g], va[:, sl], preferred_element_type=F32) + bt_ref[:, g:g + 1]
                ya_ref[rows, sl] = (gate[:, sl] * mixed).astype(BF16)

    tile = per_step * chunk
    return pl.pallas_call(
        body, name="branch_a_fwd",
        grid=(t // tile,),
        in_specs=[pl.BlockSpec((tile, d3), lambda i: (i, 0)), _const((1, d)), _const(wm.shape), _const(b_t.shape)],
        out_specs=pl.BlockSpec((tile, d), lambda i: (i, 0)),
        out_shape=jax.ShapeDtypeStruct((t, d), BF16),
        compiler_params=_params(("arbitrary",)),
    )(a_pre, g_v, wm, b_t)


def _branch_a_bwd(a_pre, dya, g_v, wm, wm_t, b_t):
    t = a_pre.shape[0]
    d = g_v.shape[1]
    d3 = 3 * d
    ng, chunk, _ = wm.shape
    cw = d // ng
    nsteps = t // chunk

    def body(a_ref, dya_ref, gv_ref, wm_ref, wmt_ref, bt_ref, da_ref, gws_ref, gbt_ref, gnv_ref, db_acc):
        i = pl.program_id(0)

        @pl.when(i == 0)
        def _():
            gws_ref[...] = jnp.zeros_like(gws_ref)
            gnv_ref[...] = jnp.zeros_like(gnv_ref)
            db_acc[...] = jnp.zeros_like(db_acc)

        ua, dgelu_u = _gelu_and_grad(a_ref[:, 0:d])
        vg, dgelu_v = _gelu_and_grad(a_ref[:, d:2 * d])
        za = a_ref[:, 2 * d:3 * d]
        sig = _sigmoid(za)
        sz = za * sig
        dsz = sig * (1.0 + za * (1.0 - sig))
        rv = lax.rsqrt(jnp.mean(vg * vg, axis=-1, keepdims=True) + EPS)
        nv = vg * rv
        gv = gv_ref[...]
        va = (nv * gv).astype(BF16)
        dya = dya_ref[...]
        dmix = dya * ua * sz
        db_acc[...] += dmix
        dmix_b = dmix.astype(BF16)
        t_gate = dya * sz
        t_z = dya * ua * dsz
        dva_parts = []
        for g in range(ng):
            sl = slice(g * cw, (g + 1) * cw)
            mixed = jnp.dot(wm_ref[g], va[:, sl], preferred_element_type=F32) + bt_ref[:, g:g + 1]
            da_ref[:, sl] = (t_gate[:, sl] * mixed * dgelu_u[:, sl]).astype(BF16)
            da_ref[:, 2 * d + g * cw:2 * d + (g + 1) * cw] = (t_z[:, sl] * mixed).astype(BF16)
            gws_ref[g] += lax.dot_general(dmix_b[:, sl], va[:, sl], NT, preferred_element_type=F32)
            dva_parts.append(jnp.dot(wmt_ref[g], dmix_b[:, sl], preferred_element_type=F32))
        dva = jnp.concatenate(dva_parts, axis=1)
        gnv_ref[...] += jnp.sum(dva * nv, axis=0, keepdims=True)
        dnv = dva * gv
        dvg = rv * (dnv - nv * jnp.mean(dnv * nv, axis=-1, keepdims=True))
        da_ref[:, d:2 * d] = (dvg * dgelu_v).astype(BF16)

        @pl.when(i == nsteps - 1)
        def _():
            acc = db_acc[...]
            for g in range(ng):
                gbt_ref[:, g:g + 1] = jnp.sum(acc[:, g * cw:(g + 1) * cw], axis=1, keepdims=True)

    return pl.pallas_call(
        body, name="branch_a_bwd",
        grid=(nsteps,),
        in_specs=[pl.BlockSpec((chunk, d3), lambda i: (i, 0)), pl.BlockSpec((chunk, d), lambda i: (i, 0)),
                  _const((1, d)), _const(wm.shape), _const(wm_t.shape), _const(b_t.shape)],
        out_specs=[pl.BlockSpec((chunk, d3), lambda i: (i, 0)), _const(wm.shape), _const(b_t.shape), _const((1, d))],
        out_shape=[jax.ShapeDtypeStruct((t, d3), BF16), jax.ShapeDtypeStruct(wm.shape, F32),
                   jax.ShapeDtypeStruct(b_t.shape, F32), jax.ShapeDtypeStruct((1, d), F32)],
        scratch_shapes=[pltpu.VMEM((chunk, d), F32)],
        compiler_params=_params(("arbitrary",)),
    )(a_pre, dya, g_v, wm, wm_t, b_t)


def _tri(n, rows_gt_cols):
    r = lax.broadcasted_iota(jnp.int32, (n, n), 0)
    c = lax.broadcasted_iota(jnp.int32, (n, n), 1)
    return (r > c) if rows_gt_cols else (r < c)


def _twice(tri):
    t = tri.astype(BF16)
    return jnp.concatenate([t, t], axis=0)


def _cumsum_mm(a, tri2):
    hi, lo = _split_bf16(a)
    return jnp.dot(jnp.concatenate([hi, lo], axis=1), tri2, preferred_element_type=F32)


LOG2E = 1.4426950408889634
_SIGN = 0x80000000


def _sb_block(q, k, scale, upper2, causal):
    z2 = lax.dot_general(q, k, NT, preferred_element_type=F32) * (scale * LOG2E)
    neg_abs = lax.bitcast_convert_type(lax.bitcast_convert_type(z2, jnp.uint32) | jnp.uint32(_SIGN), F32)
    l2 = jnp.log(1.0 + jnp.exp2(neg_abs)) * LOG2E
    log_beta = jnp.minimum(z2, 0.0) - l2
    lom = log_beta - z2
    if causal is not None:
        lom = jnp.where(causal, lom, 0.0)
    sx = _cumsum_mm(lom, upper2)
    return log_beta, sx, sx[:, 0:1] + lom[:, 0:1]


DEAD_LOG2 = -150.0


def _max_carry(carries):
    return jnp.max(functools.reduce(jnp.maximum, carries))


Q_GROUP, K_GROUP, V_GROUP, ZB_GROUP, GA_GROUP, GB_GROUP = 3, 4, 5, 6, 7, 8


def _attn_specs(d, seq, nq):
    hp_w = ATT_HP * (d // HEADS)
    n_hp = d // hp_w
    row_blk = lambda group: pl.BlockSpec((ATT_T, hp_w), lambda b, h, i: (b * nq + i, group * n_hp + h))
    seq_blk = lambda group: pl.BlockSpec((seq, hp_w), lambda b, h, i: (b, group * n_hp + h))
    return row_blk, seq_blk, n_hp


def _attn_fwd(proj, d, bsz, seq):
    t = proj.shape[0]
    hd = d // HEADS
    nq = seq // ATT_T
    scale = hd ** -0.5
    row_blk, seq_blk, n_hp = _attn_specs(d, seq, nq)

    def body(q_ref, k32_ref, v32_ref, zb_ref, o_ref, yb_ref, q16_ref, k_ref, v_ref):
        i = pl.program_id(2)
        causal = _tri(ATT_T, True)
        upper2 = _twice(causal)

        @pl.when(i == 0)
        def _():
            k_ref[...] = k32_ref[...].astype(BF16)
            v_ref[...] = v32_ref[...].astype(BF16)

        q16_ref[...] = q_ref[...].astype(BF16)

        def step(kb, state, mask):
            rows = pl.ds(pl.multiple_of(kb * ATT_T, ATT_T), ATT_T)
            heads = [slice(h * hd, (h + 1) * hd) for h in range(ATT_HP)]
            scores = [_sb_block(q16_ref[:, cols], k_ref[rows, cols], scale, upper2, mask) for cols in heads]
            new = []
            for cols, (carry, acc), (log_beta, sx, total) in zip(heads, state, scores):
                a = jnp.exp2(log_beta + sx + carry)
                if mask is not None:
                    a = jnp.where(mask, a, 0.0)
                acc = acc + jnp.dot(a.astype(BF16), v_ref[rows, cols], preferred_element_type=F32)
                new.append((carry + total, acc))
            return tuple(new)

        init = tuple((jnp.zeros((ATT_T, 1), F32), jnp.zeros((ATT_T, hd), F32)) for _ in range(ATT_HP))
        state = step(i, init, causal)
        def more(c):
            new = step(c[0], c[1], None)
            return c[0] - 1, new, _max_carry([s[0] for s in new])

        _, state, _ = lax.while_loop(lambda c: (c[0] >= 0) & (c[2] > DEAD_LOG2), more,
                                     (i - 1, state, _max_carry([s[0] for s in state])))
        for h in range(ATT_HP):
            cols = slice(h * hd, (h + 1) * hd)
            acc = state[h][1]
            zb = zb_ref[:, cols]
            o_ref[:, cols] = acc
            yb_ref[:, cols] = (acc * (zb * _sigmoid(zb))).astype(BF16)

    return pl.pallas_call(
        body, name="attn_fwd",
        grid=(bsz, n_hp, nq),
        in_specs=[row_blk(Q_GROUP), seq_blk(K_GROUP), seq_blk(V_GROUP), row_blk(ZB_GROUP)],
        out_specs=[row_blk(0), row_blk(0), row_blk(0), seq_blk(0), seq_blk(0)],
        out_shape=[jax.ShapeDtypeStruct((t, d), F32)] + [jax.ShapeDtypeStruct((t, d), BF16)] * 4,
        compiler_params=_params(("arbitrary", "arbitrary", "arbitrary")),
    )(proj, proj, proj, proj)


def _attn_bwd(q16, k16, v16, proj, o, dyb, bsz, seq):
    t, d = q16.shape
    hd = d // HEADS
    nq = seq // ATT_T
    scale = hd ** -0.5
    row_blk, seq_blk, n_hp = _attn_specs(d, seq, nq)

    def body(q_ref, k_ref, v_ref, zb_ref, o_ref, dyb_ref, dq_ref, dk_ref, dv_ref, dzb_ref,
             g_s, beta_s, dkt_acc, dvt_acc):
        i = pl.program_id(2)

        @pl.when(i == 0)
        def _():
            dkt_acc[...] = jnp.zeros_like(dkt_acc)
            dvt_acc[...] = jnp.zeros_like(dvt_acc)

        causal = _tri(ATT_T, True)
        upper2 = _twice(causal)
        lower2 = _twice(~causal)
        zb = zb_ref[...]
        sig = _sigmoid(zb)
        dyb_t = dyb_ref[...]
        do_f = dyb_t * (zb * sig)
        do = do_f.astype(BF16)
        do_t = do_f.T.astype(BF16)
        q_t = q_ref[...].astype(F32).T.astype(BF16)
        dzb_ref[...] = (dyb_t * o_ref[...] * (sig * (1.0 + zb * (1.0 - sig)))).astype(BF16)

        def sweep(kb, carries, mask):
            rows = pl.ds(pl.multiple_of(kb * ATT_T, ATT_T), ATT_T)
            heads = [slice(h * hd, (h + 1) * hd) for h in range(ATT_HP)]
            scores = [_sb_block(q_ref[:, cols], k_ref[rows, cols], scale, upper2, mask) for cols in heads]
            das = [lax.dot_general(do[:, cols], v_ref[rows, cols], NT, preferred_element_type=F32) for cols in heads]
            new = []
            for h, (cols, carry, (log_beta, sx, total), da) in enumerate(zip(heads, carries, scores, das)):
                a = jnp.exp2(log_beta + sx + carry)
                beta = jnp.exp2(log_beta)
                if mask is not None:
                    a = jnp.where(mask, a, 0.0)
                    beta = jnp.where(mask, beta, 0.0)
                g_s[h, kb] = a * da
                beta_s[h, kb] = beta
                dvt_acc[kb, cols, :] += jnp.dot(do_t[cols, :], a.astype(BF16), preferred_element_type=F32)
                new.append(carry + total)
            return tuple(new)

        carries = sweep(i, tuple(jnp.zeros((ATT_T, 1), F32) for _ in range(ATT_HP)), causal)

        def more(c):
            new = sweep(c[0], c[1], None)
            return c[0] - 1, new, _max_carry(new)

        last, _, _ = lax.while_loop(lambda c: (c[0] >= 0) & (c[2] > DEAD_LOG2), more, (i - 1, carries, _max_carry(carries)))
        first_kb = last + 1

        def back(kb, state):
            rows = pl.ds(pl.multiple_of(kb * ATT_T, ATT_T), ATT_T)
            heads = [slice(h * hd, (h + 1) * hd) for h in range(ATT_HP)]
            sums = [_cumsum_mm(g_s[h, kb], lower2) for h in range(ATT_HP)]
            new = []
            for h, (cols, (p_carry, dq), px) in enumerate(zip(heads, state, sums)):
                dz = ((g_s[h, kb] - (p_carry + px) * beta_s[h, kb]) * scale).astype(BF16)
                dq = dq + jnp.dot(dz, k_ref[rows, cols], preferred_element_type=F32)
                dkt_acc[kb, cols, :] += jnp.dot(q_t[cols, :], dz, preferred_element_type=F32)
                new.append((p_carry + px[:, ATT_T - 1:ATT_T], dq))
            return tuple(new)

        init = tuple((jnp.zeros((ATT_T, 1), F32), jnp.zeros((ATT_T, hd), F32)) for _ in range(ATT_HP))
        state = lax.fori_loop(first_kb, i + 1, back, init)
        for h in range(ATT_HP):
            dq_ref[:, h * hd:(h + 1) * hd] = state[h][1].astype(BF16)

        @pl.when(i == nq - 1)
        def _():
            for kb in range(nq):
                dk_ref[kb * ATT_T:(kb + 1) * ATT_T, :] = dkt_acc[kb].T.astype(BF16)
                dv_ref[kb * ATT_T:(kb + 1) * ATT_T, :] = dvt_acc[kb].T.astype(BF16)

    out = jax.ShapeDtypeStruct((t, d), BF16)
    hp_w = ATT_HP * hd
    return pl.pallas_call(
        body, name="attn_bwd",
        grid=(bsz, n_hp, nq),
        in_specs=[row_blk(0), seq_blk(0), seq_blk(0), row_blk(ZB_GROUP), row_blk(0), row_blk(0)],
        out_specs=[row_blk(0), seq_blk(0), seq_blk(0), row_blk(0)],
        out_shape=[out, out, out, out],
        scratch_shapes=[pltpu.VMEM((ATT_HP, nq, ATT_T, ATT_T), F32), pltpu.VMEM((ATT_HP, nq, ATT_T, ATT_T), F32),
                        pltpu.VMEM((nq, hp_w, ATT_T), F32), pltpu.VMEM((nq, hp_w, ATT_T), F32)],
        compiler_params=_params(("arbitrary", "arbitrary", "arbitrary")),
    )(q16, k16, v16, proj, o, dyb)


def _out_proj(ya, yb, g_pre, x2d, tgt, w_og, w_osb, w_out, g_f):
    t, d = x2d.shape

    def body(ya_ref, yb_ref, ga_ref, gb_ref, x_ref, tgt_ref, wog_ref, wosb_ref, wout_ref, gf_ref,
             dya_ref, dyb_ref, dg_ref, dx2_ref, loss_ref, gnf_ref, gwog_ref, gwosb_ref, gwout_ref):
        @pl.when(pl.program_id(0) == 0)
        def _():
            loss_ref[...] = jnp.zeros_like(loss_ref)
            gnf_ref[...] = jnp.zeros_like(gnf_ref)
            gwog_ref[...] = jnp.zeros_like(gwog_ref)
            gwosb_ref[...] = jnp.zeros_like(gwosb_ref)
            gwout_ref[...] = jnp.zeros_like(gwout_ref)

        ya = ya_ref[...]
        yb = yb_ref[...]
        pa = jnp.dot(ya, wog_ref[...], preferred_element_type=F32)
        pb = jnp.dot(yb, wosb_ref[...], preferred_element_type=F32)
        sga = _sigmoid(ga_ref[...])
        sgb = _sigmoid(gb_ref[...])
        merged = (sga * pa + sgb * pb).astype(BF16)
        x2 = x_ref[...] + jnp.dot(merged, wout_ref[...], preferred_element_type=F32)
        r2 = lax.rsqrt(jnp.mean(x2 * x2, axis=-1, keepdims=True) + EPS)
        n2 = x2 * r2
        gf = gf_ref[...]
        err = n2 * gf - tgt_ref[...]
        loss_ref[...] += 0.5 * jnp.sum(jnp.sum(err * err, axis=-1, keepdims=True), axis=0, keepdims=True) / d
        dy = err * (1.0 / d)
        gnf_ref[...] += jnp.sum(dy * n2, axis=0, keepdims=True)
        dn = dy * gf
        dx2 = r2 * (dn - n2 * jnp.mean(dn * n2, axis=-1, keepdims=True))
        dx2_ref[...] = dx2
        dx2_b = dx2.astype(BF16)
        dmerged = lax.dot_general(dx2_b, wout_ref[...], NT, preferred_element_type=F32)
        gwout_ref[...] += lax.dot_general(merged, dx2_b, TN, preferred_element_type=F32)
        dg_ref[:, 0:d] = (dmerged * pa * (sga * (1.0 - sga))).astype(BF16)
        dg_ref[:, d:2 * d] = (dmerged * pb * (sgb * (1.0 - sgb))).astype(BF16)
        dpa = (dmerged * sga).astype(BF16)
        dpb = (dmerged * sgb).astype(BF16)
        dya_ref[...] = lax.dot_general(dpa, wog_ref[...], NT, preferred_element_type=F32)
        dyb_ref[...] = lax.dot_general(dpb, wosb_ref[...], NT, preferred_element_type=F32)
        gwog_ref[...] += lax.dot_general(ya, dpa, TN, preferred_element_type=F32)
        gwosb_ref[...] += lax.dot_general(yb, dpb, TN, preferred_element_type=F32)

    row = lambda i: (i, 0)
    return pl.pallas_call(
        body, name="out_proj",
        grid=(t // TM,),
        in_specs=[pl.BlockSpec((TM, d), row), pl.BlockSpec((TM, d), row),
                  pl.BlockSpec((TM, d), lambda i: (i, GA_GROUP)), pl.BlockSpec((TM, d), lambda i: (i, GB_GROUP)),
                  pl.BlockSpec((TM, d), row), pl.BlockSpec((TM, d), row),
                  _resident((d, d)), _resident((d, d)), _resident((d, d)), _const((1, d))],
        out_specs=[pl.BlockSpec((TM, d), row), pl.BlockSpec((TM, d), row), pl.BlockSpec((TM, 2 * d), row),
                   pl.BlockSpec((TM, d), row), _const((1, 1)), _const((1, d)),
                   _const((d, d)), _const((d, d)), _const((d, d))],
        out_shape=[jax.ShapeDtypeStruct((t, d), F32), jax.ShapeDtypeStruct((t, d), F32),
                   jax.ShapeDtypeStruct((t, 2 * d), BF16), jax.ShapeDtypeStruct((t, d), F32),
                   jax.ShapeDtypeStruct((1, 1), F32), jax.ShapeDtypeStruct((1, d), F32),
                   jax.ShapeDtypeStruct((d, d), F32), jax.ShapeDtypeStruct((d, d), F32),
                   jax.ShapeDtypeStruct((d, d), F32)],
        compiler_params=_params(("arbitrary",)),
    )(ya, yb, g_pre, g_pre, x2d, tgt, w_og, w_osb, w_out, g_f)


def _dproj_pieces(d):
    return [(0, 0, 3), (1, 3, 1), (2, 4, 1), (3, 5, 1), (4, 6, 1), (5, 7, 2)]


def _in_proj_bwd_x(pieces, wg, x2d, g_in, dx2, gw16):
    t, d = x2d.shape
    ncol = wg.shape[2]
    segs = _segments(d, ncol)
    layout = _dproj_pieces(d)
    nsteps = t // TM

    def body(da_ref, dq_ref, dk_ref, dv_ref, dzb_ref, dg_ref, w_ref, x_ref, g_ref, dx2_ref, gw16_ref,
             gx_ref, gn_ref, recv_ref, send_sems, recv_sems):
        x_pos, y_pos, c_pos, chips = _place()

        def share(k, chunk):
            px, py = chips[k]
            return pltpu.make_async_remote_copy(
                src_ref=gw16_ref.at[:, chunk * ncol:(chunk + 1) * ncol], dst_ref=recv_ref.at[k],
                send_sem=send_sems.at[k], recv_sem=recv_sems.at[k], device_id=(px, py, c_pos), device_id_type=MESH)

        @pl.when(pl.program_id(0) == 0)
        def _():
            gn_ref[...] = jnp.zeros_like(gn_ref)
            for k, (px, py) in enumerate(chips):
                for chunk in range(N_CHIP):
                    @pl.when(2 * px + py == chunk)
                    def _(k=k, chunk=chunk):
                        share(k, chunk).start()

        @pl.when(pl.program_id(0) == nsteps - 1)
        def _():
            for k in range(N_CHIP - 1):
                share(k, 0).wait()

        refs = (da_ref, dq_ref, dk_ref, dv_ref, dzb_ref, dg_ref)
        dh = jnp.zeros((TM, d), F32)
        for chip, c0, grp, s0, width in segs:
            piece, first, _ = next(p for p in layout if p[1] <= grp < p[1] + p[2])
            off = (grp - first) * d + s0
            dh = dh + lax.dot_general(refs[piece][:, off:off + width], w_ref[chip, :, c0:c0 + width], NT,
                                      preferred_element_type=F32)
        x = x_ref[...]
        r = lax.rsqrt(jnp.mean(x * x, axis=-1, keepdims=True) + EPS)
        n = x * r
        gn_ref[...] += jnp.sum(dh * n, axis=0, keepdims=True)
        dn = dh * g_ref[...]
        gx_ref[...] = dx2_ref[...] + r * (dn - n * jnp.mean(dn * n, axis=-1, keepdims=True))

    row = lambda i: (i, 0)
    return pl.pallas_call(
        body, name="in_proj_bwd_x",
        grid=(t // TM,),
        in_specs=[pl.BlockSpec((TM, p.shape[1]), row) for p in pieces]
        + [_resident(wg.shape), pl.BlockSpec((TM, d), row), _const((1, d)), pl.BlockSpec((TM, d), row), ANY],
        out_specs=[pl.BlockSpec((TM, d), row), _const((1, d)), ANY],
        out_shape=[jax.ShapeDtypeStruct((t, d), F32), jax.ShapeDtypeStruct((1, d), F32),
                   jax.ShapeDtypeStruct((N_CHIP - 1, d, ncol), BF16)],
        scratch_shapes=[pltpu.SemaphoreType.DMA((N_CHIP - 1,)), pltpu.SemaphoreType.DMA((N_CHIP - 1,))],
        compiler_params=_params(("arbitrary",)),
    )(*pieces, wg, x2d, g_in, dx2, gw16)


def _in_proj_bwd_w(h_t, pieces, mats16, pack):
    d, t = h_t.shape
    nk = t // TKW
    layout = _dproj_pieces(d)
    n_mats = len(mats16)
    n_dev = 2 * N_CHIP
    flips = [(dx, dy, dc) for dx in (0, 1) for dy in (0, 1) for dc in (0, 1)][1:]

    def body(ht_ref, da_ref, dq_ref, dk_ref, dv_ref, dzb_ref, dg_ref, *rest):
        mat_refs, pack_ref = rest[:n_mats], rest[n_mats]
        gw_ref, gw16_ref = rest[n_mats + 1:n_mats + 3]
        recv_refs, slots_ref = rest[n_mats + 3:2 * n_mats + 3], rest[2 * n_mats + 3]
        acc, mat_send, mat_recv, pack_send, pack_recv, own_sem = rest[2 * n_mats + 4:]
        s = pl.program_id(0)
        i = pl.program_id(1)
        x_pos, y_pos, c_pos, chips = _place()
        me = 4 * x_pos + 2 * y_pos + c_pos

        def exchanges():
            cps = []
            for k, (px, py) in enumerate(chips):
                for a in range(n_mats):
                    cps.append(pltpu.make_async_remote_copy(
                        src_ref=mat_refs[a].at[2 * px + py], dst_ref=recv_refs[a].at[k],
                        send_sem=mat_send.at[a, k], recv_sem=mat_recv.at[a, k],
                        device_id=(px, py, c_pos), device_id_type=MESH))
            for k, (dx, dy, dc) in enumerate(flips):
                peer = (1 - x_pos if dx else x_pos, 1 - y_pos if dy else y_pos, 1 - c_pos if dc else c_pos)
                cps.append(pltpu.make_async_remote_copy(
                    src_ref=pack_ref, dst_ref=slots_ref.at[me], send_sem=pack_send.at[k], recv_sem=pack_recv.at[k],
                    device_id=peer, device_id_type=MESH))
            return cps, pltpu.make_async_copy(pack_ref, slots_ref.at[me], own_sem)

        @pl.when((s == 0) & (i == 0))
        def _():
            cps, own = exchanges()
            own.start()
            for cp in cps:
                cp.start()

        @pl.when((s == N_SPLIT - 1) & (i == nk - 1))
        def _():
            cps, own = exchanges()
            own.wait()
            for cp in cps:
                cp.wait()

        @pl.when(i == 0)
        def _():
            acc[...] = jnp.zeros_like(acc)

        refs = (da_ref, dq_ref, dk_ref, dv_ref, dzb_ref, dg_ref)
        for piece, first, count in layout:
            @pl.when((s >= first) & (s < first + count))
            def _(piece=piece):
                acc[...] += jnp.dot(ht_ref[...], refs[piece][...], preferred_element_type=F32)

        @pl.when(i == nk - 1)
        def _():
            gw_ref[...] = acc[...]
            gw16_ref[...] = acc[...].astype(BF16)

    def piece_spec(p, first, count):
        def index(s, i):
            mine = (s >= first) & (s < first + count)
            return jnp.where(mine, i, 0), jnp.where(mine, s - first, 0)
        return pl.BlockSpec((TKW, d), index)

    col_blk = pl.BlockSpec((d, d), lambda s, i: (0, s))
    outs = pl.pallas_call(
        body, name="in_proj_bwd_w",
        grid=(N_SPLIT, nk),
        in_specs=[pl.BlockSpec((d, TKW), lambda s, i: (0, i))] + [piece_spec(*p) for p in layout] + [ANY] * (n_mats + 1),
        out_specs=[col_blk, col_blk] + [ANY] * (n_mats + 1),
        out_shape=[jax.ShapeDtypeStruct((d, N_SPLIT * d), F32), jax.ShapeDtypeStruct((d, N_SPLIT * d), BF16)]
        + [jax.ShapeDtypeStruct((N_CHIP - 1,) + m.shape[1:], BF16) for m in mats16]
        + [jax.ShapeDtypeStruct((n_dev,) + pack.shape, F32)],
        scratch_shapes=[pltpu.VMEM((d, d), F32),
                        pltpu.SemaphoreType.DMA((n_mats, N_CHIP - 1)), pltpu.SemaphoreType.DMA((n_mats, N_CHIP - 1)),
                        pltpu.SemaphoreType.DMA((n_dev - 1,)), pltpu.SemaphoreType.DMA((n_dev - 1,)),
                        pltpu.SemaphoreType.DMA],
        compiler_params=_params(("arbitrary", "arbitrary")),
    )(h_t, *pieces, *mats16, pack)
    return outs[0], outs[1], outs[2:2 + n_mats], outs[2 + n_mats]


def _local_step(proj, x2d, tgt2d, bsz, seq, norm_v, w_s, b_s, w_og, w_osb, w_out, norm_final):
    d = x2d.shape[1]
    chunk = w_s.shape[-1]
    causal = jnp.tril(jnp.ones((chunk, chunk), dtype=bool))
    wm = jnp.where(causal[None], w_s, 0.0).astype(BF16)
    wm_t = jnp.swapaxes(wm, 1, 2)
    b_t = b_s.T

    ya = _branch_a_fwd(proj, norm_v, wm, b_t)
    o, yb, q16, k16, v16 = _attn_fwd(proj, d, bsz, seq)
    dya, dyb, dg, dx2, loss, g_nf, g_wog, g_wosb, g_wout = _out_proj(
        ya, yb, proj, x2d, tgt2d, w_og, w_osb, w_out, norm_final.reshape(1, d))
    dq, dk, dv, dzb = _attn_bwd(q16, k16, v16, proj, o, dyb, bsz, seq)
    d_a, g_ws, g_bt, g_nv = _branch_a_bwd(proj, dya, norm_v, wm, wm_t, b_t)
    g_ws = jnp.where(causal[None], g_ws, 0.0)
    return loss, (d_a, dq, dk, dv, dzb, dg), dx2, g_nv, g_ws, g_bt.T, g_wog, g_wosb, g_wout, g_nf


def _row_tile(rows):
    return next(r for r in (128, 64, 32, 16, 8) if rows % r == 0)


def _cast_bf16(arrs):
    n = len(arrs)

    def body(*refs):
        for a_ref, o_ref in zip(refs[:n], refs[n:]):
            o_ref[...] = a_ref[...].astype(BF16)

    specs = [pl.BlockSpec((a.shape[0] // CAST_STEPS, a.shape[1]), lambda i: (i, 0)) for a in arrs]
    return pl.pallas_call(
        body, name="cast_bf16", grid=(CAST_STEPS,),
        in_specs=specs, out_specs=specs,
        out_shape=[jax.ShapeDtypeStruct(a.shape, BF16) for a in arrs],
        compiler_params=_params(("arbitrary",)),
    )(*arrs)


def _add_received(full, recv, chip, by_cols):
    _, rows, cols = recv.shape
    tr = _row_tile(rows)
    nb = rows // tr

    def body(chip_ref, own_ref, recv_ref, o_ref):
        s = own_ref[...]
        for k in range(N_CHIP - 1):
            s = s + recv_ref[k].astype(F32)
        o_ref[...] = s

    own_map = (lambda i, chip_ref: (i, chip_ref[0])) if by_cols else (lambda i, chip_ref: (chip_ref[0] * nb + i, 0))
    return pl.pallas_call(
        body, name="add_received",
        grid_spec=pltpu.PrefetchScalarGridSpec(
            num_scalar_prefetch=1, grid=(nb,),
            in_specs=[pl.BlockSpec((tr, cols), own_map),
                      pl.BlockSpec((N_CHIP - 1, tr, cols), lambda i, chip_ref: (0, i, 0))],
            out_specs=pl.BlockSpec((tr, cols), lambda i, chip_ref: (i, 0))),
        out_shape=jax.ShapeDtypeStruct((rows, cols), F32),
        compiler_params=_params(("arbitrary",)),
    )(chip.reshape(1).astype(jnp.int32), full, recv)


def _adamw_math(w, m, v, g):
    new_m = ADAM_B1 * m + (1.0 - ADAM_B1) * g
    new_v = ADAM_B2 * v + (1.0 - ADAM_B2) * (g * g)
    m_hat = new_m / (1.0 - ADAM_B1 ** ADAM_STEP)
    v_hat = new_v / (1.0 - ADAM_B2 ** ADAM_STEP)
    return -ADAM_LR * (m_hat / (jnp.sqrt(v_hat) + ADAM_EPS) + ADAM_WD * w), new_m, new_v


def _adamw(w, m, v, g_parts):
    rows, cols = w.shape
    tr = _row_tile(rows)
    n_parts = len(g_parts)

    def body(*refs):
        w_ref, m_ref, v_ref = refs[:3]
        part_refs = refs[3:3 + n_parts]
        g_ref, d_ref, nm_ref, nv_ref = refs[3 + n_parts:]
        g = part_refs[0][...]
        for p in part_refs[1:]:
            g = g + p[...]
        g_ref[...] = g
        d_ref[...], nm_ref[...], nv_ref[...] = _adamw_math(w_ref[...], m_ref[...], v_ref[...], g)

    spec = pl.BlockSpec((tr, cols), lambda i: (i, 0))
    out = jax.ShapeDtypeStruct(w.shape, F32)
    return pl.pallas_call(
        body, name="adamw", grid=(rows // tr,),
        in_specs=[spec] * (3 + n_parts), out_specs=[spec] * 4, out_shape=[out] * 4,
        compiler_params=_params(("arbitrary",)),
    )(w, m, v, *g_parts)


def _adamw_small(w, m, v, slots_head, slots_tail):
    n_dev, p0, _ = slots_head.shape

    def body(w_ref, m_ref, v_ref, head_ref, tail_ref, g_ref, d_ref, nm_ref, nv_ref):
        for ref, rows in ((head_ref, slice(0, p0)), (tail_ref, slice(p0, w.shape[0]))):
            g = ref[0]
            for i in range(1, n_dev):
                g = g + ref[i]
            g_ref[rows, :] = g
            d_ref[rows, :], nm_ref[rows, :], nv_ref[rows, :] = _adamw_math(w_ref[rows, :], m_ref[rows, :], v_ref[rows, :], g)

    vmem = pl.BlockSpec(memory_space=pltpu.VMEM)
    out = jax.ShapeDtypeStruct(w.shape, F32)
    return pl.pallas_call(
        body, name="adamw_small", in_specs=[vmem] * 5, out_specs=[vmem] * 4, out_shape=[out] * 4,
        compiler_params=pltpu.CompilerParams(vmem_limit_bytes=VMEM_LIMIT),
    )(w, m, v, slots_head, slots_tail)


ANY = pl.BlockSpec(memory_space=pl.ANY)


def _place():
    x, y, c = lax.axis_index("x"), lax.axis_index("y"), lax.axis_index("c")
    other_chips = [(1 - x, y), (x, 1 - y), (1 - x, 1 - y)]
    return x, y, c, other_chips


def _swap_and_gather(arrs, pack):
    n = len(arrs)
    n_dev = 2 * N_CHIP
    flips = [(dx, dy, dc) for dx in (0, 1) for dy in (0, 1) for dc in (0, 1)][1:]

    def body(*refs):
        ins, pack_ref = refs[:n], refs[n]
        outs, slots_ref = refs[n + 1:2 * n + 1], refs[2 * n + 1]
        send_sems, recv_sems, pack_send, pack_recv, own_sem = refs[2 * n + 2:]
        x, y, c, _ = _place()
        me = 4 * x + 2 * y + c
        own = pltpu.make_async_copy(pack_ref, slots_ref.at[me], own_sem)
        own.start()
        copies = []
        for k, (dx, dy, dc) in enumerate(flips):
            peer = (1 - x if dx else x, 1 - y if dy else y, 1 - c if dc else c)
            copies.append(pltpu.make_async_remote_copy(
                src_ref=pack_ref, dst_ref=slots_ref.at[me], send_sem=pack_send.at[k], recv_sem=pack_recv.at[k],
                device_id=peer, device_id_type=MESH))
        copies += [pltpu.make_async_remote_copy(
            src_ref=ins[a], dst_ref=outs[a], send_sem=send_sems.at[a], recv_sem=recv_sems.at[a],
            device_id=(x, y, 1 - c), device_id_type=MESH) for a in range(n)]
        for cp in copies:
            cp.start()
        for cp in copies:
            cp.wait()
        own.wait()

    return pl.pallas_call(
        body, name="swap_and_gather",
        in_specs=[ANY] * (n + 1), out_specs=[ANY] * (n + 1),
        out_shape=[jax.ShapeDtypeStruct(a.shape, a.dtype) for a in arrs] + [jax.ShapeDtypeStruct((n_dev,) + pack.shape, F32)],
        scratch_shapes=[pltpu.SemaphoreType.DMA((n,)), pltpu.SemaphoreType.DMA((n,)),
                        pltpu.SemaphoreType.DMA((n_dev - 1,)), pltpu.SemaphoreType.DMA((n_dev - 1,)),
                        pltpu.SemaphoreType.DMA],
    )(*arrs, pack)


SLAB = 8


def _slab(vec, d):
    return jnp.pad(vec.reshape(1, d), ((0, SLAB - 1), (0, 0)))


def _pack_tail(vec_nv, b_s, vec_nf, w_s):
    d = vec_nv.shape[-1]
    return jnp.concatenate([_slab(v, d) for v in (vec_nv, b_s, vec_nf)] + [w_s.reshape(-1, d)], axis=0)


def _pack_small(vec_nin, vec_nv, b_s, vec_nf, w_s):
    return jnp.concatenate([_slab(vec_nin, vec_nin.shape[-1]), _pack_tail(vec_nv, b_s, vec_nf, w_s)], axis=0)


def _unpack_small(pack, w_s_shape, b_s_shape):
    return (pack[0:1], pack[SLAB:SLAB + 1], pack[2 * SLAB].reshape(b_s_shape), pack[3 * SLAB],
            pack[4 * SLAB:].reshape(w_s_shape))


def kernel(x, norm_in, w_in, norm_v, w_s, b_s, w_o_gmlp, w_o_sb, w_out, norm_final, loss_target, m_norm_in, m_w_in, m_norm_v, m_w_s, m_b_s, m_w_o_gmlp, m_w_o_sb, m_w_out, m_norm_final, v_norm_in, v_w_in, v_norm_v, v_w_s, v_b_s, v_w_o_gmlp, v_w_o_sb, v_w_out, v_norm_final):
    d = x.shape[-1]
    ncol = w_in.shape[-1]
    nrow = w_o_gmlp.shape[-2]
    chip = 2 * lax.axis_index("x") + lax.axis_index("y")

    bsz, seq, _ = x.shape
    x2d = x.reshape(bsz * seq, d)
    shards = [w_in[0], w_o_gmlp[0], w_o_sb[0], w_out[0]]
    halves = [s16.reshape(2, s16.shape[0] // 2, s16.shape[1]) for s16 in _cast_bf16(shards)]
    proj, h_t, (wg, w_og, w_osb, w_o) = _in_proj_fwd(x2d, norm_in, halves)
    wg = wg.reshape(N_CHIP, d, ncol)

    loss, pieces, dx2, g_nv, g_ws, g_bs, g_wog, g_wosb, g_wout, g_nf = _local_step(
        proj, x2d, loss_target.reshape(bsz * seq, d), bsz, seq, norm_v, w_s[0], b_s[0],
        w_og.reshape(d, d), w_osb.reshape(d, d), w_o.reshape(d, d), norm_final)

    mats = [g_wog, g_wosb, g_wout]
    mats16 = [g16.reshape(N_CHIP, nrow, d) for g16 in _cast_bf16(mats)]
    g_win, g_win16, recv_mats, slots_tail = _in_proj_bwd_w(h_t, pieces, mats16, _pack_tail(g_nv, g_bs, g_nf, g_ws))
    grad_x, g_nin, recv_win = _in_proj_bwd_x(pieces, wg, x2d, norm_in, dx2, g_win16)
    grad_x = grad_x.reshape(bsz, seq, d)

    sums = [_add_received(g_win, recv_win, chip, True)] + [
        _add_received(g, r, chip, False) for g, r in zip(mats, recv_mats)]
    *sibling_sums, slots_head = _swap_and_gather(sums, _slab(g_nin, d))
    stats = []
    for w, m, v, mine, theirs in zip(shards, [m_w_in[0], m_w_o_gmlp[0], m_w_o_sb[0], m_w_out[0]],
                                     [v_w_in[0], v_w_o_gmlp[0], v_w_o_sb[0], v_w_out[0]], sums, sibling_sums):
        stats.append(_adamw(w, m, v, [mine, theirs]))
    (gw_in, dw_in, nm_in, nv_in), (gw_og, dw_og, nm_og, nv_og), (gw_osb, dw_osb, nm_osb, nv_osb), \
        (gw_out, dw_out, nm_out, nv_out) = stats

    gs, ds, ms, vs = _adamw_small(
        _pack_small(norm_in, norm_v, b_s[0], norm_final, w_s[0]),
        _pack_small(m_norm_in, m_norm_v, m_b_s[0], m_norm_final, m_w_s[0]),
        _pack_small(v_norm_in, v_norm_v, v_b_s[0], v_norm_final, v_w_s[0]), slots_head, slots_tail)

    def small(pack):
        nin, nv, bs, nf, ws = _unpack_small(pack, w_s.shape, b_s.shape)
        return nin, nv, ws, bs, nf

    loss = lax.psum(loss[0, 0], ("x", "y", "c"))
    out = []
    for small_pack, win, wog, wosb, wout in ((gs, gw_in, gw_og, gw_osb, gw_out), (ds, dw_in, dw_og, dw_osb, dw_out),
                                             (ms, nm_in, nm_og, nm_osb, nm_out), (vs, nv_in, nv_og, nv_osb, nv_out)):
        nin, nv, ws, bs, nf = small(small_pack)
        out += [nin, win[None], nv, ws, bs, wog[None], wosb[None], wout[None], nf]
    return (loss, grad_x, *out)
```

```python
import functools
import math

import jax
import jax.numpy as jnp
from jax import lax
from jax.experimental import pallas as pl
from jax.experimental.pallas import tpu as pltpu

F32 = jnp.float32
BF16 = jnp.bfloat16
EPS = 1e-6
HEADS = 8
N_SPLIT = 9
N_CHIP = 4
MESH = pl.DeviceIdType.MESH

ADAM_LR = 0.001
ADAM_B1 = 0.9
ADAM_B2 = 0.999
ADAM_EPS = 1e-08
ADAM_WD = 0.01
ADAM_STEP = 10

VMEM_LIMIT = 56 * 2 ** 20
TM = 256
ATT_T = 256
ATT_HP = 4
TKW = 1024
CHUNKS_PER_STEP = 4
CAST_STEPS = 8

NT = (((1,), (1,)), ((), ()))
TN = (((0,), (0,)), ((), ()))


def _params(sem):
    return pltpu.CompilerParams(dimension_semantics=sem, vmem_limit_bytes=VMEM_LIMIT)


def _resident(shape):
    nd = len(shape)
    return pl.BlockSpec(shape, lambda *_: (0,) * nd, pipeline_mode=pl.Buffered(1))


def _const(shape):
    nd = len(shape)
    return pl.BlockSpec(shape, lambda *_: (0,) * nd)


def _segments(d, ncol):
    segs = []
    edges = sorted({j * ncol for j in range(N_CHIP + 1)} | {s * d for s in range(N_SPLIT + 1)})
    for lo, hi in zip(edges[:-1], edges[1:]):
        segs.append((lo // ncol, lo % ncol, lo // d, lo % d, hi - lo))
    return segs


def _sigmoid(x):
    return 1.0 / (1.0 + jnp.exp(-x))


_GELU_C = math.sqrt(2.0 / math.pi)


def _gelu(x):
    return 0.5 * x * (1.0 + jnp.tanh(_GELU_C * (x + 0.044715 * (x * x * x))))


def _gelu_and_grad(x):
    x2 = x * x
    th = jnp.tanh(_GELU_C * (x + 0.044715 * (x2 * x)))
    val = 0.5 * x * (1.0 + th)
    grad = 0.5 * (1.0 + th) + 0.5 * x * (1.0 - th * th) * (_GELU_C * (1.0 + 3.0 * 0.044715 * x2))
    return val, grad


def _split_bf16(a):
    hi = a.astype(BF16)
    lo = (a - hi.astype(F32)).astype(BF16)
    return hi, lo


def _in_proj_fwd(x2d, g_in, halves):
    t, d = x2d.shape
    n = len(halves)
    ncol = halves[0].shape[2]
    n_row = t // TM
    assert halves[0].shape[1] * 2 == d

    def body(order_ref, x_ref, g_ref, *rest):
        ins = rest[:n]
        proj_ref, ht_ref = rest[n:n + 2]
        outs = rest[n + 2:2 * n + 2]
        wbuf, send_sems, recv_sems, local_sems, load_sem = rest[2 * n + 2:]
        phase = pl.program_id(0)
        i = pl.program_id(1)
        x_pos, y_pos, c_pos, chips = _place()
        sibling = (x_pos, y_pos, 1 - c_pos)
        me = (x_pos, y_pos, c_pos)
        my_chip = 2 * x_pos + y_pos

        def copy(a, k, block, to, src=None):
            return pltpu.make_async_remote_copy(
                src_ref=outs[a].at[block] if src is None else src, dst_ref=outs[a].at[block],
                send_sem=send_sems.at[a, k], recv_sem=recv_sems.at[a, k], device_id=to, device_id_type=MESH)

        def local(a):
            return pltpu.make_async_copy(ins[a], outs[a].at[pl.ds(2 * my_chip, 2)], local_sems.at[a])

        def load(src, first, slot):
            for half in range(2):
                cp = pltpu.make_async_copy(src.at[first + half], wbuf.at[slot, pl.ds(half * (d // 2), d // 2)], load_sem)
                cp.start()
                cp.wait()

        def send_mine(k):
            px, py = chips[k]
            for a in range(n):
                copy(a, k, 2 * my_chip + c_pos, (px, py, c_pos), src=ins[a].at[c_pos]).start()

        @pl.when((phase == 0) & (i == 0))
        def _():
            for a in range(n):
                local(a).start()
            send_mine(0)
            send_mine(1)
            load(ins[0], 0, 0)

        for k, (px, py) in enumerate(chips):
            @pl.when((phase == k + 1) & (i == 0))
            def _(k=k, px=px, py=py):
                theirs = 2 * (2 * px + py)
                for a in range(n):
                    copy(a, k, theirs + c_pos, me).wait_recv()
                    copy(a, 3 + k, theirs + c_pos, sibling).start()
                if k == 0:
                    send_mine(2)
                for a in range(n):
                    copy(a, 3 + k, theirs + 1 - c_pos, me).wait_recv()
                load(outs[0], theirs, (k + 1) % 2)

        x = x_ref[...]
        r = lax.rsqrt(jnp.mean(x * x, axis=-1, keepdims=True) + EPS)
        hf = x * r * g_ref[...]
        h = hf.astype(BF16)

        @pl.when(phase == 0)
        def _():
            ht_ref[...] = hf.T.astype(BF16)

        for slot in range(2):
            @pl.when(phase % 2 == slot)
            def _(slot=slot):
                proj_ref[...] = jnp.dot(h, wbuf[slot], preferred_element_type=F32)

        @pl.when((phase == N_CHIP - 1) & (i == n_row - 1))
        def _():
            for a in range(n):
                for k in range(2 * (N_CHIP - 1)):
                    copy(a, k, 0, me).wait_send()
                local(a).wait()

    x_pos, y_pos = lax.axis_index("x"), lax.axis_index("y")
    order = jnp.stack([2 * x_pos + y_pos, 2 * (1 - x_pos) + y_pos, 2 * x_pos + 1 - y_pos,
                       2 * (1 - x_pos) + 1 - y_pos]).astype(jnp.int32)
    last = n_row - 1
    outs = pl.pallas_call(
        body, name="in_proj_fwd",
        grid_spec=pltpu.PrefetchScalarGridSpec(
            num_scalar_prefetch=1, grid=(N_CHIP, n_row),
            in_specs=[pl.BlockSpec((TM, d), lambda p, i, order: (i, 0)),
                      pl.BlockSpec((1, d), lambda p, i, order: (0, 0))] + [ANY] * n,
            out_specs=[pl.BlockSpec((TM, ncol), lambda p, i, order: (i, order[p])),
                       pl.BlockSpec((d, TM), lambda p, i, order: (0, jnp.where(p == 0, i, last)))] + [ANY] * n,
            scratch_shapes=[pltpu.VMEM((2, d, ncol), BF16),
                            pltpu.SemaphoreType.DMA((n, 2 * (N_CHIP - 1))), pltpu.SemaphoreType.DMA((n, 2 * (N_CHIP - 1))),
                            pltpu.SemaphoreType.DMA((n,)), pltpu.SemaphoreType.DMA]),
        out_shape=[jax.ShapeDtypeStruct((t, N_CHIP * ncol), F32), jax.ShapeDtypeStruct((d, t), BF16)]
        + [jax.ShapeDtypeStruct((2 * N_CHIP,) + hv.shape[1:], BF16) for hv in halves],
        compiler_params=_params(("arbitrary", "arbitrary")),
    )(order, x2d, g_in, *halves)
    return outs[0], outs[1], outs[2:]


def _branch_a_fwd(a_pre, g_v, wm, b_t):
    t = a_pre.shape[0]
    d = g_v.shape[1]
    d3 = 3 * d
    ng, chunk, _ = wm.shape
    cw = d // ng

    per_step = CHUNKS_PER_STEP if t % (CHUNKS_PER_STEP * chunk) == 0 else 1

    def body(a_ref, gv_ref, wm_ref, bt_ref, ya_ref):
        for n in range(per_step):
            rows = slice(n * chunk, (n + 1) * chunk)
            ua = _gelu(a_ref[rows, 0:d])
            vg = _gelu(a_ref[rows, d:2 * d])
            za = a_ref[rows, 2 * d:3 * d]
            rv = lax.rsqrt(jnp.mean(vg * vg, axis=-1, keepdims=True) + EPS)
            va = (vg * rv * gv_ref[...]).astype(BF16)
            gate = ua * (za * _sigmoid(za))
            for g in range(ng):
                sl = slice(g * cw, (g + 1) * cw)
                mixed = jnp.dot(wm_ref[g], va[:, sl], preferred_element_type=F32) + bt_ref[:, g:g + 1]
                ya_ref[rows, sl] = (gate[:, sl] * mixed).astype(BF16)

    tile = per_step * chunk
    return pl.pallas_call(
        body, name="branch_a_fwd",
        grid=(t // tile,),
        in_specs=[pl.BlockSpec((tile, d3), lambda i: (i, 0)), _const((1, d)), _const(wm.shape), _const(b_t.shape)],
        out_specs=pl.BlockSpec((tile, d), lambda i: (i, 0)),
        out_shape=jax.ShapeDtypeStruct((t, d), BF16),
        compiler_params=_params(("arbitrary",)),
    )(a_pre, g_v, wm, b_t)


def _branch_a_bwd(a_pre, dya, g_v, wm, wm_t, b_t):
    t = a_pre.shape[0]
    d = g_v.shape[1]
    d3 = 3 * d
    ng, chunk, _ = wm.shape
    cw = d // ng
    nsteps = t // chunk

    def body(a_ref, dya_ref, gv_ref, wm_ref, wmt_ref, bt_ref, da_ref, gws_ref, gbt_ref, gnv_ref, db_acc):
        i = pl.program_id(0)

        @pl.when(i == 0)
        def _():
            gws_ref[...] = jnp.zeros_like(gws_ref)
            gnv_ref[...] = jnp.zeros_like(gnv_ref)
            db_acc[...] = jnp.zeros_like(db_acc)

        ua, dgelu_u = _gelu_and_grad(a_ref[:, 0:d])
        vg, dgelu_v = _gelu_and_grad(a_ref[:, d:2 * d])
        za = a_ref[:, 2 * d:3 * d]
        sig = _sigmoid(za)
        sz = za * sig
        dsz = sig * (1.0 + za * (1.0 - sig))
        rv = lax.rsqrt(jnp.mean(vg * vg, axis=-1, keepdims=True) + EPS)
        nv = vg * rv
        gv = gv_ref[...]
        va = (nv * gv).astype(BF16)
        dya = dya_ref[...]
        dmix = dya * ua * sz
        db_acc[...] += dmix
        dmix_b = dmix.astype(BF16)
        t_gate = dya * sz
        t_z = dya * ua * dsz
        dva_parts = []
        for g in range(ng):
            sl = slice(g * cw, (g + 1) * cw)
            mixed = jnp.dot(wm_ref[g], va[:, sl], preferred_element_type=F32) + bt_ref[:, g:g + 1]
            da_ref[:, sl] = (t_gate[:, sl] * mixed * dgelu_u[:, sl]).astype(BF16)
            da_ref[:, 2 * d + g * cw:2 * d + (g + 1) * cw] = (t_z[:, sl] * mixed).astype(BF16)
            gws_ref[g] += lax.dot_general(dmix_b[:, sl], va[:, sl], NT, preferred_element_type=F32)
            dva_parts.append(jnp.dot(wmt_ref[g], dmix_b[:, sl], preferred_element_type=F32))
        dva = jnp.concatenate(dva_parts, axis=1)
        gnv_ref[...] += jnp.sum(dva * nv, axis=0, keepdims=True)
        dnv = dva * gv
        dvg = rv * (dnv - nv * jnp.mean(dnv * nv, axis=-1, keepdims=True))
        da_ref[:, d:2 * d] = (dvg * dgelu_v).astype(BF16)

        @pl.when(i == nsteps - 1)
        def _():
            acc = db_acc[...]
            for g in range(ng):
                gbt_ref[:, g:g + 1] = jnp.sum(acc[:, g * cw:(g + 1) * cw], axis=1, keepdims=True)

    return pl.pallas_call(
        body, name="branch_a_bwd",
        grid=(nsteps,),
        in_specs=[pl.BlockSpec((chunk, d3), lambda i: (i, 0)), pl.BlockSpec((chunk, d), lambda i: (i, 0)),
                  _const((1, d)), _const(wm.shape), _const(wm_t.shape), _const(b_t.shape)],
        out_specs=[pl.BlockSpec((chunk, d3), lambda i: (i, 0)), _const(wm.shape), _const(b_t.shape), _const((1, d))],
        out_shape=[jax.ShapeDtypeStruct((t, d3), BF16), jax.ShapeDtypeStruct(wm.shape, F32),
                   jax.ShapeDtypeStruct(b_t.shape, F32), jax.ShapeDtypeStruct((1, d), F32)],
        scratch_shapes=[pltpu.VMEM((chunk, d), F32)],
        compiler_params=_params(("arbitrary",)),
    )(a_pre, dya, g_v, wm, wm_t, b_t)


def _tri(n, rows_gt_cols):
    r = lax.broadcasted_iota(jnp.int32, (n, n), 0)
    c = lax.broadcasted_iota(jnp.int32, (n, n), 1)
    return (r > c) if rows_gt_cols else (r < c)


def _twice(tri):
    t = tri.astype(BF16)
    return jnp.concatenate([t, t], axis=0)


def _cumsum_mm(a, tri2):
    hi, lo = _split_bf16(a)
    return jnp.dot(jnp.concatenate([hi, lo], axis=1), tri2, preferred_element_type=F32)


LOG2E = 1.4426950408889634
_SIGN = 0x80000000


def _sb_block(q, k, scale, upper2, causal):
    z2 = lax.dot_general(q, k, NT, preferred_element_type=F32) * (scale * LOG2E)
    neg_abs = lax.bitcast_convert_type(lax.bitcast_convert_type(z2, jnp.uint32) | jnp.uint32(_SIGN), F32)
    l2 = jnp.log(1.0 + jnp.exp2(neg_abs)) * LOG2E
    log_beta = jnp.minimum(z2, 0.0) - l2
    lom = log_beta - z2
    if causal is not None:
        lom = jnp.where(causal, lom, 0.0)
    sx = _cumsum_mm(lom, upper2)
    return log_beta, sx, sx[:, 0:1] + lom[:, 0:1]


DEAD_LOG2 = -150.0


def _max_carry(carries):
    return jnp.max(functools.reduce(jnp.maximum, carries))


Q_GROUP, K_GROUP, V_GROUP, ZB_GROUP, GA_GROUP, GB_GROUP = 3, 4, 5, 6, 7, 8


def _attn_specs(d, seq, nq):
    hp_w = ATT_HP * (d // HEADS)
    n_hp = d // hp_w
    row_blk = lambda group: pl.BlockSpec((ATT_T, hp_w), lambda b, h, i: (b * nq + i, group * n_hp + h))
    seq_blk = lambda group: pl.BlockSpec((seq, hp_w), lambda b, h, i: (b, group * n_hp + h))
    return row_blk, seq_blk, n_hp


def _attn_fwd(proj, d, bsz, seq):
    t = proj.shape[0]
    hd = d // HEADS
    nq = seq // ATT_T
    scale = hd ** -0.5
    row_blk, seq_blk, n_hp = _attn_specs(d, seq, nq)

    def body(q_ref, k32_ref, v32_ref, zb_ref, o_ref, yb_ref, q16_ref, k_ref, v_ref):
        i = pl.program_id(2)
        causal = _tri(ATT_T, True)
        upper2 = _twice(causal)

        @pl.when(i == 0)
        def _():
            k_ref[...] = k32_ref[...].astype(BF16)
            v_ref[...] = v32_ref[...].astype(BF16)

        q16_ref[...] = q_ref[...].astype(BF16)

        def step(kb, state, mask):
            rows = pl.ds(pl.multiple_of(kb * ATT_T, ATT_T), ATT_T)
            heads = [slice(h * hd, (h + 1) * hd) for h in range(ATT_HP)]
            scores = [_sb_block(q16_ref[:, cols], k_ref[rows, cols], scale, upper2, mask) for cols in heads]
            new = []
            for cols, (carry, acc), (log_beta, sx, total) in zip(heads, state, scores):
                a = jnp.exp2(log_beta + sx + carry)
                if mask is not None:
                    a = jnp.where(mask, a, 0.0)
                acc = acc + jnp.dot(a.astype(BF16), v_ref[rows, cols], preferred_element_type=F32)
                new.append((carry + total, acc))
            return tuple(new)

        init = tuple((jnp.zeros((ATT_T, 1), F32), jnp.zeros((ATT_T, hd), F32)) for _ in range(ATT_HP))
        state = step(i, init, causal)
        def more(c):
            new = step(c[0], c[1], None)
            return c[0] - 1, new, _max_carry([s[0] for s in new])

        _, state, _ = lax.while_loop(lambda c: (c[0] >= 0) & (c[2] > DEAD_LOG2), more,
                                     (i - 1, state, _max_carry([s[0] for s in state])))
        for h in range(ATT_HP):
            cols = slice(h * hd, (h + 1) * hd)
            acc = state[h][1]
            zb = zb_ref[:, cols]
            o_ref[:, cols] = acc
            yb_ref[:, cols] = (acc * (zb * _sigmoid(zb))).astype(BF16)

    return pl.pallas_call(
        body, name="attn_fwd",
        grid=(bsz, n_hp, nq),
        in_specs=[row_blk(Q_GROUP), seq_blk(K_GROUP), seq_blk(V_GROUP), row_blk(ZB_GROUP)],
        out_specs=[row_blk(0), row_blk(0), row_blk(0), seq_blk(0), seq_blk(0)],
        out_shape=[jax.ShapeDtypeStruct((t, d), F32)] + [jax.ShapeDtypeStruct((t, d), BF16)] * 4,
        compiler_params=_params(("arbitrary", "arbitrary", "arbitrary")),
    )(proj, proj, proj, proj)


def _attn_bwd(q16, k16, v16, proj, o, dyb, bsz, seq):
    t, d = q16.shape
    hd = d // HEADS
    nq = seq // ATT_T
    scale = hd ** -0.5
    row_blk, seq_blk, n_hp = _attn_specs(d, seq, nq)

    def body(q_ref, k_ref, v_ref, zb_ref, o_ref, dyb_ref, dq_ref, dk_ref, dv_ref, dzb_ref,
             g_s, beta_s, dkt_acc, dvt_acc):
        i = pl.program_id(2)

        @pl.when(i == 0)
        def _():
            dkt_acc[...] = jnp.zeros_like(dkt_acc)
            dvt_acc[...] = jnp.zeros_like(dvt_acc)

        causal = _tri(ATT_T, True)
        upper2 = _twice(causal)
        lower2 = _twice(~causal)
        zb = zb_ref[...]
        sig = _sigmoid(zb)
        dyb_t = dyb_ref[...]
        do_f = dyb_t * (zb * sig)
        do = do_f.astype(BF16)
        do_t = do_f.T.astype(BF16)
        q_t = q_ref[...].astype(F32).T.astype(BF16)
        dzb_ref[...] = (dyb_t * o_ref[...] * (sig * (1.0 + zb * (1.0 - sig)))).astype(BF16)

        def sweep(kb, carries, mask):
            rows = pl.ds(pl.multiple_of(kb * ATT_T, ATT_T), ATT_T)
            heads = [slice(h * hd, (h + 1) * hd) for h in range(ATT_HP)]
            scores = [_sb_block(q_ref[:, cols], k_ref[rows, cols], scale, upper2, mask) for cols in heads]
            das = [lax.dot_general(do[:, cols], v_ref[rows, cols], NT, preferred_element_type=F32) for cols in heads]
            new = []
            for h, (cols, carry, (log_beta, sx, total), da) in enumerate(zip(heads, carries, scores, das)):
                a = jnp.exp2(log_beta + sx + carry)
                beta = jnp.exp2(log_beta)
                if mask is not None:
                    a = jnp.where(mask, a, 0.0)
                    beta = jnp.where(mask, beta, 0.0)
                g_s[h, kb] = a * da
                beta_s[h, kb] = beta
                dvt_acc[kb, cols, :] += jnp.dot(do_t[cols, :], a.astype(BF16), preferred_element_type=F32)
                new.append(carry + total)
            return tuple(new)

        carries = sweep(i, tuple(jnp.zeros((ATT_T, 1), F32) for _ in range(ATT_HP)), causal)

        def more(c):
            new = sweep(c[0], c[1], None)
            return c[0] - 1, new, _max_carry(new)

        last, _, _ = lax.while_loop(lambda c: (c[0] >= 0) & (c[2] > DEAD_LOG2), more, (i - 1, carries, _max_carry(carries)))
        first_kb = last + 1

        def back(kb, state):
            rows = pl.ds(pl.multiple_of(kb * ATT_T, ATT_T), ATT_T)
            heads = [slice(h * hd, (h + 1) * hd) for h in range(ATT_HP)]
            sums = [_cumsum_mm(g_s[h, kb], lower2) for h in range(ATT_HP)]
            new = []
            for h, (cols, (p_carry, dq), px) in enumerate(zip(heads, state, sums)):
                dz = ((g_s[h, kb] - (p_carry + px) * beta_s[h, kb]) * scale).astype(BF16)
                dq = dq + jnp.dot(dz, k_ref[rows, cols], preferred_element_type=F32)
                dkt_acc[kb, cols, :] += jnp.dot(q_t[cols, :], dz, preferred_element_type=F32)
                new.append((p_carry + px[:, ATT_T - 1:ATT_T], dq))
            return tuple(new)

        init = tuple((jnp.zeros((ATT_T, 1), F32), jnp.zeros((ATT_T, hd), F32)) for _ in range(ATT_HP))
        state = lax.fori_loop(first_kb, i + 1, back, init)
        for h in range(ATT_HP):
            dq_ref[:, h * hd:(h + 1) * hd] = state[h][1].astype(BF16)

        @pl.when(i == nq - 1)
        def _():
            for kb in range(nq):
                dk_ref[kb * ATT_T:(kb + 1) * ATT_T, :] = dkt_acc[kb].T.astype(BF16)
                dv_ref[kb * ATT_T:(kb + 1) * ATT_T, :] = dvt_acc[kb].T.astype(BF16)

    out = jax.ShapeDtypeStruct((t, d), BF16)
    hp_w = ATT_HP * hd
    return pl.pallas_call(
        body, name="attn_bwd",
        grid=(bsz, n_hp, nq),
        in_specs=[row_blk(0), seq_blk(0), seq_blk(0), row_blk(ZB_GROUP), row_blk(0), row_blk(0)],
        out_specs=[row_blk(0), seq_blk(0), seq_blk(0), row_blk(0)],
        out_shape=[out, out, out, out],
        scratch_shapes=[pltpu.VMEM((ATT_HP, nq, ATT_T, ATT_T), F32), pltpu.VMEM((ATT_HP, nq, ATT_T, ATT_T), F32),
                        pltpu.VMEM((nq, hp_w, ATT_T), F32), pltpu.VMEM((nq, hp_w, ATT_T), F32)],
        compiler_params=_params(("arbitrary", "arbitrary", "arbitrary")),
    )(q16, k16, v16, proj, o, dyb)


def _out_proj(ya, yb, g_pre, x2d, tgt, w_og, w_osb, w_out, g_f):
    t, d = x2d.shape

    def body(ya_ref, yb_ref, ga_ref, gb_ref, x_ref, tgt_ref, wog_ref, wosb_ref, wout_ref, gf_ref,
             dya_ref, dyb_ref, dg_ref, dx2_ref, loss_ref, gnf_ref, gwog_ref, gwosb_ref, gwout_ref):
        @pl.when(pl.program_id(0) == 0)
        def _():
            loss_ref[...] = jnp.zeros_like(loss_ref)
            gnf_ref[...] = jnp.zeros_like(gnf_ref)
            gwog_ref[...] = jnp.zeros_like(gwog_ref)
            gwosb_ref[...] = jnp.zeros_like(gwosb_ref)
            gwout_ref[...] = jnp.zeros_like(gwout_ref)

        ya = ya_ref[...]
        yb = yb_ref[...]
        pa = jnp.dot(ya, wog_ref[...], preferred_element_type=F32)
        pb = jnp.dot(yb, wosb_ref[...], preferred_element_type=F32)
        sga = _sigmoid(ga_ref[...])
        sgb = _sigmoid(gb_ref[...])
        merged = (sga * pa + sgb * pb).astype(BF16)
        x2 = x_ref[...] + jnp.dot(merged, wout_ref[...], preferred_element_type=F32)
        r2 = lax.rsqrt(jnp.mean(x2 * x2, axis=-1, keepdims=True) + EPS)
        n2 = x2 * r2
        gf = gf_ref[...]
        err = n2 * gf - tgt_ref[...]
        loss_ref[...] += 0.5 * jnp.sum(jnp.sum(err * err, axis=-1, keepdims=True), axis=0, keepdims=True) / d
        dy = err * (1.0 / d)
        gnf_ref[...] += jnp.sum(dy * n2, axis=0, keepdims=True)
        dn = dy * gf
        dx2 = r2 * (dn - n2 * jnp.mean(dn * n2, axis=-1, keepdims=True))
        dx2_ref[...] = dx2
        dx2_b = dx2.astype(BF16)
        dmerged = lax.dot_general(dx2_b, wout_ref[...], NT, preferred_element_type=F32)
        gwout_ref[...] += lax.dot_general(merged, dx2_b, TN, preferred_element_type=F32)
        dg_ref[:, 0:d] = (dmerged * pa * (sga * (1.0 - sga))).astype(BF16)
        dg_ref[:, d:2 * d] = (dmerged * pb * (sgb * (1.0 - sgb))).astype(BF16)
        dpa = (dmerged * sga).astype(BF16)
        dpb = (dmerged * sgb).astype(BF16)
        dya_ref[...] = lax.dot_general(dpa, wog_ref[...], NT, preferred_element_type=F32)
        dyb_ref[...] = lax.dot_general(dpb, wosb_ref[...], NT, preferred_element_type=F32)
        gwog_ref[...] += lax.dot_general(ya, dpa, TN, preferred_element_type=F32)
        gwosb_ref[...] += lax.dot_general(yb, dpb, TN, preferred_element_type=F32)

    row = lambda i: (i, 0)
    return pl.pallas_call(
        body, name="out_proj",
        grid=(t // TM,),
        in_specs=[pl.BlockSpec((TM, d), row), pl.BlockSpec((TM, d), row),
                  pl.BlockSpec((TM, d), lambda i: (i, GA_GROUP)), pl.BlockSpec((TM, d), lambda i: (i, GB_GROUP)),
                  pl.BlockSpec((TM, d), row), pl.BlockSpec((TM, d), row),
                  _resident((d, d)), _resident((d, d)), _resident((d, d)), _const((1, d))],
        out_specs=[pl.BlockSpec((TM, d), row), pl.BlockSpec((TM, d), row), pl.BlockSpec((TM, 2 * d), row),
                   pl.BlockSpec((TM, d), row), _const((1, 1)), _const((1, d)),
                   _const((d, d)), _const((d, d)), _const((d, d))],
        out_shape=[jax.ShapeDtypeStruct((t, d), F32), jax.ShapeDtypeStruct((t, d), F32),
                   jax.ShapeDtypeStruct((t, 2 * d), BF16), jax.ShapeDtypeStruct((t, d), F32),
                   jax.ShapeDtypeStruct((1, 1), F32), jax.ShapeDtypeStruct((1, d), F32),
                   jax.ShapeDtypeStruct((d, d), F32), jax.ShapeDtypeStruct((d, d), F32),
                   jax.ShapeDtypeStruct((d, d), F32)],
        compiler_params=_params(("arbitrary",)),
    )(ya, yb, g_pre, g_pre, x2d, tgt, w_og, w_osb, w_out, g_f)


def _dproj_pieces(d):
    return [(0, 0, 3), (1, 3, 1), (2, 4, 1), (3, 5, 1), (4, 6, 1), (5, 7, 2)]


def _in_proj_bwd_x(pieces, wg, x2d, g_in, dx2, gw16):
    t, d = x2d.shape
    ncol = wg.shape[2]
    segs = _segments(d, ncol)
    layout = _dproj_pieces(d)
    nsteps = t // TM

    def body(da_ref, dq_ref, dk_ref, dv_ref, dzb_ref, dg_ref, w_ref, x_ref, g_ref, dx2_ref, gw16_ref,
             gx_ref, gn_ref, recv_ref, send_sems, recv_sems):
        x_pos, y_pos, c_pos, chips = _place()

        def share(k, chunk):
            px, py = chips[k]
            return pltpu.make_async_remote_copy(
                src_ref=gw16_ref.at[:, chunk * ncol:(chunk + 1) * ncol], dst_ref=recv_ref.at[k],
                send_sem=send_sems.at[k], recv_sem=recv_sems.at[k], device_id=(px, py, c_pos), device_id_type=MESH)

        @pl.when(pl.program_id(0) == 0)
        def _():
            gn_ref[...] = jnp.zeros_like(gn_ref)
            for k, (px, py) in enumerate(chips):
                for chunk in range(N_CHIP):
                    @pl.when(2 * px + py == chunk)
                    def _(k=k, chunk=chunk):
                        share(k, chunk).start()

        @pl.when(pl.program_id(0) == nsteps - 1)
        def _():
            for k in range(N_CHIP - 1):
                share(k, 0).wait()

        refs = (da_ref, dq_ref, dk_ref, dv_ref, dzb_ref, dg_ref)
        dh = jnp.zeros((TM, d), F32)
        for chip, c0, grp, s0, width in segs:
            piece, first, _ = next(p for p in layout if p[1] <= grp < p[1] + p[2])
            off = (grp - first) * d + s0
            dh = dh + lax.dot_general(refs[piece][:, off:off + width], w_ref[chip, :, c0:c0 + width], NT,
                                      preferred_element_type=F32)
        x = x_ref[...]
        r = lax.rsqrt(jnp.mean(x * x, axis=-1, keepdims=True) + EPS)
        n = x * r
        gn_ref[...] += jnp.sum(dh * n, axis=0, keepdims=True)
        dn = dh * g_ref[...]
        gx_ref[...] = dx2_ref[...] + r * (dn - n * jnp.mean(dn * n, axis=-1, keepdims=True))

    row = lambda i: (i, 0)
    return pl.pallas_call(
        body, name="in_proj_bwd_x",
        grid=(t // TM,),
        in_specs=[pl.BlockSpec((TM, p.shape[1]), row) for p in pieces]
        + [_resident(wg.shape), pl.BlockSpec((TM, d), row), _const((1, d)), pl.BlockSpec((TM, d), row), ANY],
        out_specs=[pl.BlockSpec((TM, d), row), _const((1, d)), ANY],
        out_shape=[jax.ShapeDtypeStruct((t, d), F32), jax.ShapeDtypeStruct((1, d), F32),
                   jax.ShapeDtypeStruct((N_CHIP - 1, d, ncol), BF16)],
        scratch_shapes=[pltpu.SemaphoreType.DMA((N_CHIP - 1,)), pltpu.SemaphoreType.DMA((N_CHIP - 1,))],
        compiler_params=_params(("arbitrary",)),
    )(*pieces, wg, x2d, g_in, dx2, gw16)


def _in_proj_bwd_w(h_t, pieces, mats16, pack):
    d, t = h_t.shape
    nk = t // TKW
    layout = _dproj_pieces(d)
    n_mats = len(mats16)
    n_dev = 2 * N_CHIP
    flips = [(dx, dy, dc) for dx in (0, 1) for dy in (0, 1) for dc in (0, 1)][1:]

    def body(ht_ref, da_ref, dq_ref, dk_ref, dv_ref, dzb_ref, dg_ref, *rest):
        mat_refs, pack_ref = rest[:n_mats], rest[n_mats]
        gw_ref, gw16_ref = rest[n_mats + 1:n_mats + 3]
        recv_refs, slots_ref = rest[n_mats + 3:2 * n_mats + 3], rest[2 * n_mats + 3]
        acc, mat_send, mat_recv, pack_send, pack_recv, own_sem = rest[2 * n_mats + 4:]
        s = pl.program_id(0)
        i = pl.program_id(1)
        x_pos, y_pos, c_pos, chips = _place()
        me = 4 * x_pos + 2 * y_pos + c_pos

        def exchanges():
            cps = []
            for k, (px, py) in enumerate(chips):
                for a in range(n_mats):
                    cps.append(pltpu.make_async_remote_copy(
                        src_ref=mat_refs[a].at[2 * px + py], dst_ref=recv_refs[a].at[k],
                        send_sem=mat_send.at[a, k], recv_sem=mat_recv.at[a, k],
                        device_id=(px, py, c_pos), device_id_type=MESH))
            for k, (dx, dy, dc) in enumerate(flips):
                peer = (1 - x_pos if dx else x_pos, 1 - y_pos if dy else y_pos, 1 - c_pos if dc else c_pos)
                cps.append(pltpu.make_async_remote_copy(
                    src_ref=pack_ref, dst_ref=slots_ref.at[me], send_sem=pack_send.at[k], recv_sem=pack_recv.at[k],
                    device_id=peer, device_id_type=MESH))
            return cps, pltpu.make_async_copy(pack_ref, slots_ref.at[me], own_sem)

        @pl.when((s == 0) & (i == 0))
        def _():
            cps, own = exchanges()
            own.start()
            for cp in cps:
                cp.start()

        @pl.when((s == N_SPLIT - 1) & (i == nk - 1))
        def _():
            cps, own = exchanges()
            own.wait()
            for cp in cps:
                cp.wait()

        @pl.when(i == 0)
        def _():
            acc[...] = jnp.zeros_like(acc)

        refs = (da_ref, dq_ref, dk_ref, dv_ref, dzb_ref, dg_ref)
        for piece, first, count in layout:
            @pl.when((s >= first) & (s < first + count))
            def _(piece=piece):
                acc[...] += jnp.dot(ht_ref[...], refs[piece][...], preferred_element_type=F32)

        @pl.when(i == nk - 1)
        def _():
            gw_ref[...] = acc[...]
            gw16_ref[...] = acc[...].astype(BF16)

    def piece_spec(p, first, count):
        def index(s, i):
            mine = (s >= first) & (s < first + count)
            return jnp.where(mine, i, 0), jnp.where(mine, s - first, 0)
        return pl.BlockSpec((TKW, d), index)

    col_blk = pl.BlockSpec((d, d), lambda s, i: (0, s))
    outs = pl.pallas_call(
        body, name="in_proj_bwd_w",
        grid=(N_SPLIT, nk),
        in_specs=[pl.BlockSpec((d, TKW), lambda s, i: (0, i))] + [piece_spec(*p) for p in layout] + [ANY] * (n_mats + 1),
        out_specs=[col_blk, col_blk] + [ANY] * (n_mats + 1),
        out_shape=[jax.ShapeDtypeStruct((d, N_SPLIT * d), F32), jax.ShapeDtypeStruct((d, N_SPLIT * d), BF16)]
        + [jax.ShapeDtypeStruct((N_CHIP - 1,) + m.shape[1:], BF16) for m in mats16]
        + [jax.ShapeDtypeStruct((n_dev,) + pack.shape, F32)],
        scratch_shapes=[pltpu.VMEM((d, d), F32),
                        pltpu.SemaphoreType.DMA((n_mats, N_CHIP - 1)), pltpu.SemaphoreType.DMA((n_mats, N_CHIP - 1)),
                        pltpu.SemaphoreType.DMA((n_dev - 1,)), pltpu.SemaphoreType.DMA((n_dev - 1,)),
                        pltpu.SemaphoreType.DMA],
        compiler_params=_params(("arbitrary", "arbitrary")),
    )(h_t, *pieces, *mats16, pack)
    return outs[0], outs[1], outs[2:2 + n_mats], outs[2 + n_mats]


def _local_step(proj, x2d, tgt2d, bsz, seq, norm_v, w_s, b_s, w_og, w_osb, w_out, norm_final):
    d = x2d.shape[1]
    chunk = w_s.shape[-1]
    causal = jnp.tril(jnp.ones((chunk, chunk), dtype=bool))
    wm = jnp.where(causal[None], w_s, 0.0).astype(BF16)
    wm_t = jnp.swapaxes(wm, 1, 2)
    b_t = b_s.T

    ya = _branch_a_fwd(proj, norm_v, wm, b_t)
    o, yb, q16, k16, v16 = _attn_fwd(proj, d, bsz, seq)
    dya, dyb, dg, dx2, loss, g_nf, g_wog, g_wosb, g_wout = _out_proj(
        ya, yb, proj, x2d, tgt2d, w_og, w_osb, w_out, norm_final.reshape(1, d))
    dq, dk, dv, dzb = _attn_bwd(q16, k16, v16, proj, o, dyb, bsz, seq)
    d_a, g_ws, g_bt, g_nv = _branch_a_bwd(proj, dya, norm_v, wm, wm_t, b_t)
    g_ws = jnp.where(causal[None], g_ws, 0.0)
    return loss, (d_a, dq, dk, dv, dzb, dg), dx2, g_nv, g_ws, g_bt.T, g_wog, g_wosb, g_wout, g_nf


def _row_tile(rows):
    return next(r for r in (128, 64, 32, 16, 8) if rows % r == 0)


def _cast_bf16(arrs):
    n = len(arrs)

    def body(*refs):
        for a_ref, o_ref in zip(refs[:n], refs[n:]):
            o_ref[...] = a_ref[...].astype(BF16)

    specs = [pl.BlockSpec((a.shape[0] // CAST_STEPS, a.shape[1]), lambda i: (i, 0)) for a in arrs]
    return pl.pallas_call(
        body, name="cast_bf16", grid=(CAST_STEPS,),
        in_specs=specs, out_specs=specs,
        out_shape=[jax.ShapeDtypeStruct(a.shape, BF16) for a in arrs],
        compiler_params=_params(("arbitrary",)),
    )(*arrs)


def _add_received(full, recv, chip, by_cols):
    _, rows, cols = recv.shape
    tr = _row_tile(rows)
    nb = rows // tr

    def body(chip_ref, own_ref, recv_ref, o_ref):
        s = own_ref[...]
        for k in range(N_CHIP - 1):
            s = s + recv_ref[k].astype(F32)
        o_ref[...] = s

    own_map = (lambda i, chip_ref: (i, chip_ref[0])) if by_cols else (lambda i, chip_ref: (chip_ref[0] * nb + i, 0))
    return pl.pallas_call(
        body, name="add_received",
        grid_spec=pltpu.PrefetchScalarGridSpec(
            num_scalar_prefetch=1, grid=(nb,),
            in_specs=[pl.BlockSpec((tr, cols), own_map),
                      pl.BlockSpec((N_CHIP - 1, tr, cols), lambda i, chip_ref: (0, i, 0))],
            out_specs=pl.BlockSpec((tr, cols), lambda i, chip_ref: (i, 0))),
        out_shape=jax.ShapeDtypeStruct((rows, cols), F32),
        compiler_params=_params(("arbitrary",)),
    )(chip.reshape(1).astype(jnp.int32), full, recv)


def _adamw_math(w, m, v, g):
    new_m = ADAM_B1 * m + (1.0 - ADAM_B1) * g
    new_v = ADAM_B2 * v + (1.0 - ADAM_B2) * (g * g)
    m_hat = new_m / (1.0 - ADAM_B1 ** ADAM_STEP)
    v_hat = new_v / (1.0 - ADAM_B2 ** ADAM_STEP)
    return -ADAM_LR * (m_hat / (jnp.sqrt(v_hat) + ADAM_EPS) + ADAM_WD * w), new_m, new_v


def _adamw(w, m, v, g_parts):
    rows, cols = w.shape
    tr = _row_tile(rows)
    n_parts = len(g_parts)

    def body(*refs):
        w_ref, m_ref, v_ref = refs[:3]
        part_refs = refs[3:3 + n_parts]
        g_ref, d_ref, nm_ref, nv_ref = refs[3 + n_parts:]
        g = part_refs[0][...]
        for p in part_refs[1:]:
            g = g + p[...]
        g_ref[...] = g
        d_ref[...], nm_ref[...], nv_ref[...] = _adamw_math(w_ref[...], m_ref[...], v_ref[...], g)

    spec = pl.BlockSpec((tr, cols), lambda i: (i, 0))
    out = jax.ShapeDtypeStruct(w.shape, F32)
    return pl.pallas_call(
        body, name="adamw", grid=(rows // tr,),
        in_specs=[spec] * (3 + n_parts), out_specs=[spec] * 4, out_shape=[out] * 4,
        compiler_params=_params(("arbitrary",)),
    )(w, m, v, *g_parts)


def _adamw_small(w, m, v, slots_head, slots_tail):
    n_dev, p0, _ = slots_head.shape

    def body(w_ref, m_ref, v_ref, head_ref, tail_ref, g_ref, d_ref, nm_ref, nv_ref):
        for ref, rows in ((head_ref, slice(0, p0)), (tail_ref, slice(p0, w.shape[0]))):
            g = ref[0]
            for i in range(1, n_dev):
                g = g + ref[i]
            g_ref[rows, :] = g
            d_ref[rows, :], nm_ref[rows, :], nv_ref[rows, :] = _adamw_math(w_ref[rows, :], m_ref[rows, :], v_ref[rows, :], g)

    vmem = pl.BlockSpec(memory_space=pltpu.VMEM)
    out = jax.ShapeDtypeStruct(w.shape, F32)
    return pl.pallas_call(
        body, name="adamw_small", in_specs=[vmem] * 5, out_specs=[vmem] * 4, out_shape=[out] * 4,
        compiler_params=pltpu.CompilerParams(vmem_limit_bytes=VMEM_LIMIT),
    )(w, m, v, slots_head, slots_tail)


ANY = pl.BlockSpec(memory_space=pl.ANY)


def _place():
    x, y, c = lax.axis_index("x"), lax.axis_index("y"), lax.axis_index("c")
    other_chips = [(1 - x, y), (x, 1 - y), (1 - x, 1 - y)]
    return x, y, c, other_chips


def _swap_and_gather(arrs, pack):
    n = len(arrs)
    n_dev = 2 * N_CHIP
    flips = [(dx, dy, dc) for dx in (0, 1) for dy in (0, 1) for dc in (0, 1)][1:]

    def body(*refs):
        ins, pack_ref = refs[:n], refs[n]
        outs, slots_ref = refs[n + 1:2 * n + 1], refs[2 * n + 1]
        send_sems, recv_sems, pack_send, pack_recv, own_sem = refs[2 * n + 2:]
        x, y, c, _ = _place()
        me = 4 * x + 2 * y + c
        own = pltpu.make_async_copy(pack_ref, slots_ref.at[me], own_sem)
        own.start()
        copies = []
        for k, (dx, dy, dc) in enumerate(flips):
            peer = (1 - x if dx else x, 1 - y if dy else y, 1 - c if dc else c)
            copies.append(pltpu.make_async_remote_copy(
                src_ref=pack_ref, dst_ref=slots_ref.at[me], send_sem=pack_send.at[k], recv_sem=pack_recv.at[k],
                device_id=peer, device_id_type=MESH))
        copies += [pltpu.make_async_remote_copy(
            src_ref=ins[a], dst_ref=outs[a], send_sem=send_sems.at[a], recv_sem=recv_sems.at[a],
            device_id=(x, y, 1 - c), device_id_type=MESH) for a in range(n)]
        for cp in copies:
            cp.start()
        for cp in copies:
            cp.wait()
        own.wait()

    return pl.pallas_call(
        body, name="swap_and_gather",
        in_specs=[ANY] * (n + 1), out_specs=[ANY] * (n + 1),
        out_shape=[jax.ShapeDtypeStruct(a.shape, a.dtype) for a in arrs] + [jax.ShapeDtypeStruct((n_dev,) + pack.shape, F32)],
        scratch_shapes=[pltpu.SemaphoreType.DMA((n,)), pltpu.SemaphoreType.DMA((n,)),
                        pltpu.SemaphoreType.DMA((n_dev - 1,)), pltpu.SemaphoreType.DMA((n_dev - 1,)),
                        pltpu.SemaphoreType.DMA],
    )(*arrs, pack)


SLAB = 8


def _slab(vec, d):
    return jnp.pad(vec.reshape(1, d), ((0, SLAB - 1), (0, 0)))


def _pack_tail(vec_nv, b_s, vec_nf, w_s):
    d = vec_nv.shape[-1]
    return jnp.concatenate([_slab(v, d) for v in (vec_nv, b_s, vec_nf)] + [w_s.reshape(-1, d)], axis=0)


def _pack_small(vec_nin, vec_nv, b_s, vec_nf, w_s):
    return jnp.concatenate([_slab(vec_nin, vec_nin.shape[-1]), _pack_tail(vec_nv, b_s, vec_nf, w_s)], axis=0)


def _unpack_small(pack, w_s_shape, b_s_shape):
    return (pack[0:1], pack[SLAB:SLAB + 1], pack[2 * SLAB].reshape(b_s_shape), pack[3 * SLAB],
            pack[4 * SLAB:].reshape(w_s_shape))


def kernel(x, norm_in, w_in, norm_v, w_s, b_s, w_o_gmlp, w_o_sb, w_out, norm_final, loss_target, m_norm_in, m_w_in, m_norm_v, m_w_s, m_b_s, m_w_o_gmlp, m_w_o_sb, m_w_out, m_norm_final, v_norm_in, v_w_in, v_norm_v, v_w_s, v_b_s, v_w_o_gmlp, v_w_o_sb, v_w_out, v_norm_final):
    d = x.shape[-1]
    ncol = w_in.shape[-1]
    nrow = w_o_gmlp.shape[-2]
    chip = 2 * lax.axis_index("x") + lax.axis_index("y")

    bsz, seq, _ = x.shape
    x2d = x.reshape(bsz * seq, d)
    shards = [w_in[0], w_o_gmlp[0], w_o_sb[0], w_out[0]]
    halves = [s16.reshape(2, s16.shape[0] // 2, s16.shape[1]) for s16 in _cast_bf16(shards)]
    proj, h_t, (wg, w_og, w_osb, w_o) = _in_proj_fwd(x2d, norm_in, halves)
    wg = wg.reshape(N_CHIP, d, ncol)

    loss, pieces, dx2, g_nv, g_ws, g_bs, g_wog, g_wosb, g_wout, g_nf = _local_step(
        proj, x2d, loss_target.reshape(bsz * seq, d), bsz, seq, norm_v, w_s[0], b_s[0],
        w_og.reshape(d, d), w_osb.reshape(d, d), w_o.reshape(d, d), norm_final)

    mats = [g_wog, g_wosb, g_wout]
    mats16 = [g16.reshape(N_CHIP, nrow, d) for g16 in _cast_bf16(mats)]
    g_win, g_win16, recv_mats, slots_tail = _in_proj_bwd_w(h_t, pieces, mats16, _pack_tail(g_nv, g_bs, g_nf, g_ws))
    grad_x, g_nin, recv_win = _in_proj_bwd_x(pieces, wg, x2d, norm_in, dx2, g_win16)
    grad_x = grad_x.reshape(bsz, seq, d)

    sums = [_add_received(g_win, recv_win, chip, True)] + [
        _add_received(g, r, chip, False) for g, r in zip(mats, recv_mats)]
    *sibling_sums, slots_head = _swap_and_gather(sums, _slab(g_nin, d))
    stats = []
    for w, m, v, mine, theirs in zip(shards, [m_w_in[0], m_w_o_gmlp[0], m_w_o_sb[0], m_w_out[0]],
                                     [v_w_in[0], v_w_o_gmlp[0], v_w_o_sb[0], v_w_out[0]], sums, sibling_sums):
        stats.append(_adamw(w, m, v, [mine, theirs]))
    (gw_in, dw_in, nm_in, nv_in), (gw_og, dw_og, nm_og, nv_og), (gw_osb, dw_osb, nm_osb, nv_osb), \
        (gw_out, dw_out, nm_out, nv_out) = stats

    gs, ds, ms, vs = _adamw_small(
        _pack_small(norm_in, norm_v, b_s[0], norm_final, w_s[0]),
        _pack_small(m_norm_in, m_norm_v, m_b_s[0], m_norm_final, m_w_s[0]),
        _pack_small(v_norm_in, v_norm_v, v_b_s[0], v_norm_final, v_w_s[0]), slots_head, slots_tail)

    def small(pack):
        nin, nv, bs, nf, ws = _unpack_small(pack, w_s.shape, b_s.shape)
        return nin, nv, ws, bs, nf

    loss = lax.psum(loss[0, 0], ("x", "y", "c"))
    out = []
    for small_pack, win, wog, wosb, wout in ((gs, gw_in, gw_og, gw_osb, gw_out), (ds, dw_in, dw_og, dw_osb, dw_out),
                                             (ms, nm_in, nm_og, nm_osb, nm_out), (vs, nv_in, nv_og, nv_osb, nv_out)):
        nin, nv, ws, bs, nf = small(small_pack)
        out += [nin, win[None], nv, ws, bs, wog[None], wosb[None], wout[None], nf]
    return (loss, grad_x, *out)
```

```python
import functools
import math

import jax
import jax.numpy as jnp
from jax import lax
from jax.experimental import pallas as pl
from jax.experimental.pallas import tpu as pltpu

F32 = jnp.float32
BF16 = jnp.bfloat16
EPS = 1e-6
HEADS = 8
N_SPLIT = 9
N_CHIP = 4
MESH = pl.DeviceIdType.MESH

ADAM_LR = 0.001
ADAM_B1 = 0.9
ADAM_B2 = 0.999
ADAM_EPS = 1e-08
ADAM_WD = 0.01
ADAM_STEP = 10

VMEM_LIMIT = 56 * 2 ** 20
TM = 256
TMF = 512
ATT_T = 256
ATT_HP = 4
TKW = 1024
CHUNKS_PER_STEP = 4
CAST_STEPS = 8

NT = (((1,), (1,)), ((), ()))
TN = (((0,), (0,)), ((), ()))


def _params(sem):
    return pltpu.CompilerParams(dimension_semantics=sem, vmem_limit_bytes=VMEM_LIMIT)


def _resident(shape):
    nd = len(shape)
    return pl.BlockSpec(shape, lambda *_: (0,) * nd, pipeline_mode=pl.Buffered(1))


def _const(shape):
    nd = len(shape)
    return pl.BlockSpec(shape, lambda *_: (0,) * nd)


def _segments(d, ncol):
    segs = []
    edges = sorted({j * ncol for j in range(N_CHIP + 1)} | {s * d for s in range(N_SPLIT + 1)})
    for lo, hi in zip(edges[:-1], edges[1:]):
        segs.append((lo // ncol, lo % ncol, lo // d, lo % d, hi - lo))
    return segs


def _sigmoid(x):
    return 1.0 / (1.0 + jnp.exp(-x))


_GELU_C = math.sqrt(2.0 / math.pi)


def _gelu(x):
    return 0.5 * x * (1.0 + jnp.tanh(_GELU_C * (x + 0.044715 * (x * x * x))))


def _gelu_and_grad(x):
    x2 = x * x
    th = jnp.tanh(_GELU_C * (x + 0.044715 * (x2 * x)))
    val = 0.5 * x * (1.0 + th)
    grad = 0.5 * (1.0 + th) + 0.5 * x * (1.0 - th * th) * (_GELU_C * (1.0 + 3.0 * 0.044715 * x2))
    return val, grad


def _split_bf16(a):
    hi = a.astype(BF16)
    lo = (a - hi.astype(F32)).astype(BF16)
    return hi, lo


def _in_proj_fwd(x2d, g_in, halves):
    t, d = x2d.shape
    n = len(halves)
    ncol = halves[0].shape[2]
    n_row = t // TMF
    assert halves[0].shape[1] * 2 == d

    def body(order_ref, x_ref, g_ref, *rest):
        ins = rest[:n]
        proj_ref, ht_ref = rest[n:n + 2]
        outs = rest[n + 2:2 * n + 2]
        wbuf, h_all, send_sems, recv_sems, local_sems, load_sem = rest[2 * n + 2:]
        phase = pl.program_id(0)
        i = pl.program_id(1)
        x_pos, y_pos, c_pos, chips = _place()
        sibling = (x_pos, y_pos, 1 - c_pos)
        me = (x_pos, y_pos, c_pos)
        my_chip = 2 * x_pos + y_pos

        def copy(a, k, block, to, src=None):
            return pltpu.make_async_remote_copy(
                src_ref=outs[a].at[block] if src is None else src, dst_ref=outs[a].at[block],
                send_sem=send_sems.at[a, k], recv_sem=recv_sems.at[a, k], device_id=to, device_id_type=MESH)

        def local(a):
            return pltpu.make_async_copy(ins[a], outs[a].at[pl.ds(2 * my_chip, 2)], local_sems.at[a])

        def load(src, first, slot):
            for half in range(2):
                cp = pltpu.make_async_copy(src.at[first + half], wbuf.at[slot, pl.ds(half * (d // 2), d // 2)], load_sem)
                cp.start()
                cp.wait()

        def send_mine(k):
            px, py = chips[k]
            for a in range(n):
                copy(a, k, 2 * my_chip + c_pos, (px, py, c_pos), src=ins[a].at[c_pos]).start()

        @pl.when((phase == 0) & (i == 0))
        def _():
            for a in range(n):
                local(a).start()
            send_mine(0)
            send_mine(1)
            load(ins[0], 0, 0)

        for k, (px, py) in enumerate(chips):
            @pl.when((phase == k + 1) & (i == 0))
            def _(k=k, px=px, py=py):
                theirs = 2 * (2 * px + py)
                for a in range(n):
                    copy(a, k, theirs + c_pos, me).wait_recv()
                    copy(a, 3 + k, theirs + c_pos, sibling).start()
                if k == 0:
                    send_mine(2)
                for a in range(n):
                    copy(a, 3 + k, theirs + 1 - c_pos, me).wait_recv()
                load(outs[0], theirs, (k + 1) % 2)

        @pl.when(phase == 0)
        def _():
            x = x_ref[...]
            r = lax.rsqrt(jnp.mean(x * x, axis=-1, keepdims=True) + EPS)
            hf = x * r * g_ref[...]
            h_all[i] = hf.astype(BF16)
            ht_ref[...] = hf.T.astype(BF16)

        for slot in range(2):
            @pl.when(phase % 2 == slot)
            def _(slot=slot):
                proj_ref[...] = jnp.dot(h_all[i], wbuf[slot], preferred_element_type=F32)

        @pl.when((phase == N_CHIP - 1) & (i == n_row - 1))
        def _():
            for a in range(n):
                for k in range(2 * (N_CHIP - 1)):
                    copy(a, k, 0, me).wait_send()
                local(a).wait()

    x_pos, y_pos = lax.axis_index("x"), lax.axis_index("y")
    order = jnp.stack([2 * x_pos + y_pos, 2 * (1 - x_pos) + y_pos, 2 * x_pos + 1 - y_pos,
                       2 * (1 - x_pos) + 1 - y_pos]).astype(jnp.int32)
    last = n_row - 1
    outs = pl.pallas_call(
        body, name="in_proj_fwd",
        grid_spec=pltpu.PrefetchScalarGridSpec(
            num_scalar_prefetch=1, grid=(N_CHIP, n_row),
            in_specs=[pl.BlockSpec((TMF, d), lambda p, i, order: (jnp.where(p == 0, i, last), 0)),
                      pl.BlockSpec((1, d), lambda p, i, order: (0, 0))] + [ANY] * n,
            out_specs=[pl.BlockSpec((TMF, ncol), lambda p, i, order: (i, order[p])),
                       pl.BlockSpec((d, TMF), lambda p, i, order: (0, jnp.where(p == 0, i, last)))] + [ANY] * n,
            scratch_shapes=[pltpu.VMEM((2, d, ncol), BF16), pltpu.VMEM((n_row, TMF, d), BF16),
                            pltpu.SemaphoreType.DMA((n, 2 * (N_CHIP - 1))), pltpu.SemaphoreType.DMA((n, 2 * (N_CHIP - 1))),
                            pltpu.SemaphoreType.DMA((n,)), pltpu.SemaphoreType.DMA]),
        out_shape=[jax.ShapeDtypeStruct((t, N_CHIP * ncol), F32), jax.ShapeDtypeStruct((d, t), BF16)]
        + [jax.ShapeDtypeStruct((2 * N_CHIP,) + hv.shape[1:], BF16) for hv in halves],
        compiler_params=_params(("arbitrary", "arbitrary")),
    )(order, x2d, g_in, *halves)
    return outs[0], outs[1], outs[2:]


def _branch_a_fwd(a_pre, g_v, wm, b_t):
    t = a_pre.shape[0]
    d = g_v.shape[1]
    d3 = 3 * d
    ng, chunk, _ = wm.shape
    cw = d // ng

    per_step = CHUNKS_PER_STEP if t % (CHUNKS_PER_STEP * chunk) == 0 else 1

    def body(a_ref, gv_ref, wm_ref, bt_ref, ya_ref):
        for n in range(per_step):
            rows = slice(n * chunk, (n + 1) * chunk)
            ua = _gelu(a_ref[rows, 0:d])
            vg = _gelu(a_ref[rows, d:2 * d])
            za = a_ref[rows, 2 * d:3 * d]
            rv = lax.rsqrt(jnp.mean(vg * vg, axis=-1, keepdims=True) + EPS)
            va = (vg * rv * gv_ref[...]).astype(BF16)
            gate = ua * (za * _sigmoid(za))
            for g in range(ng):
                sl = slice(g * cw, (g + 1) * cw)
                mixed = jnp.dot(wm_ref[g], va[:, sl], preferred_element_type=F32) + bt_ref[:, g:g + 1]
                ya_ref[rows, sl] = (gate[:, sl] * mixed).astype(BF16)

    tile = per_step * chunk
    return pl.pallas_call(
        body, name="branch_a_fwd",
        grid=(t // tile,),
        in_specs=[pl.BlockSpec((tile, d3), lambda i: (i, 0)), _const((1, d)), _const(wm.shape), _const(b_t.shape)],
        out_specs=pl.BlockSpec((tile, d), lambda i: (i, 0)),
        out_shape=jax.ShapeDtypeStruct((t, d), BF16),
        compiler_params=_params(("arbitrary",)),
    )(a_pre, g_v, wm, b_t)


def _branch_a_bwd(a_pre, dya, g_v, wm, wm_t, b_t):
    t = a_pre.shape[0]
    d = g_v.shape[1]
    d3 = 3 * d
    ng, chunk, _ = wm.shape
    cw = d // ng
    nsteps = t // chunk

    def body(a_ref, dya_ref, gv_ref, wm_ref, wmt_ref, bt_ref, da_ref, gws_ref, gbt_ref, gnv_ref, db_acc):
        i = pl.program_id(0)

        @pl.when(i == 0)
        def _():
            gws_ref[...] = jnp.zeros_like(gws_ref)
            gnv_ref[...] = jnp.zeros_like(gnv_ref)
            db_acc[...] = jnp.zeros_like(db_acc)

        ua, dgelu_u = _gelu_and_grad(a_ref[:, 0:d])
        vg, dgelu_v = _gelu_and_grad(a_ref[:, d:2 * d])
        za = a_ref[:, 2 * d:3 * d]
        sig = _sigmoid(za)
        sz = za * sig
        dsz = sig * (1.0 + za * (1.0 - sig))
        rv = lax.rsqrt(jnp.mean(vg * vg, axis=-1, keepdims=True) + EPS)
        nv = vg * rv
        gv = gv_ref[...]
        va = (nv * gv).astype(BF16)
        dya = dya_ref[...]
        dmix = dya * ua * sz
        db_acc[...] += dmix
        dmix_b = dmix.astype(BF16)
        t_gate = dya * sz
        t_z = dya * ua * dsz
        dva_parts = []
        for g in range(ng):
            sl = slice(g * cw, (g + 1) * cw)
            mixed = jnp.dot(wm_ref[g], va[:, sl], preferred_element_type=F32) + bt_ref[:, g:g + 1]
            da_ref[:, sl] = (t_gate[:, sl] * mixed * dgelu_u[:, sl]).astype(BF16)
            da_ref[:, 2 * d + g * cw:2 * d + (g + 1) * cw] = (t_z[:, sl] * mixed).astype(BF16)
            gws_ref[g] += lax.dot_general(dmix_b[:, sl], va[:, sl], NT, preferred_element_type=F32)
            dva_parts.append(jnp.dot(wmt_ref[g], dmix_b[:, sl], preferred_element_type=F32))
        dva = jnp.concatenate(dva_parts, axis=1)
        gnv_ref[...] += jnp.sum(dva * nv, axis=0, keepdims=True)
        dnv = dva * gv
        dvg = rv * (dnv - nv * jnp.mean(dnv * nv, axis=-1, keepdims=True))
        da_ref[:, d:2 * d] = (dvg * dgelu_v).astype(BF16)

        @pl.when(i == nsteps - 1)
        def _():
            acc = db_acc[...]
            for g in range(ng):
                gbt_ref[:, g:g + 1] = jnp.sum(acc[:, g * cw:(g + 1) * cw], axis=1, keepdims=True)

    return pl.pallas_call(
        body, name="branch_a_bwd",
        grid=(nsteps,),
        in_specs=[pl.BlockSpec((chunk, d3), lambda i: (i, 0)), pl.BlockSpec((chunk, d), lambda i: (i, 0)),
                  _const((1, d)), _const(wm.shape), _const(wm_t.shape), _const(b_t.shape)],
        out_specs=[pl.BlockSpec((chunk, d3), lambda i: (i, 0)), _const(wm.shape), _const(b_t.shape), _const((1, d))],
        out_shape=[jax.ShapeDtypeStruct((t, d3), BF16), jax.ShapeDtypeStruct(wm.shape, F32),
                   jax.ShapeDtypeStruct(b_t.shape, F32), jax.ShapeDtypeStruct((1, d), F32)],
        scratch_shapes=[pltpu.VMEM((chunk, d), F32)],
        compiler_params=_params(("arbitrary",)),
    )(a_pre, dya, g_v, wm, wm_t, b_t)


def _tri(n, rows_gt_cols):
    r = lax.broadcasted_iota(jnp.int32, (n, n), 0)
    c = lax.broadcasted_iota(jnp.int32, (n, n), 1)
    return (r > c) if rows_gt_cols else (r < c)


def _twice(tri):
    t = tri.astype(BF16)
    return jnp.concatenate([t, t], axis=0)


def _cumsum_mm(a, tri2):
    hi, lo = _split_bf16(a)
    return jnp.dot(jnp.concatenate([hi, lo], axis=1), tri2, preferred_element_type=F32)


LOG2E = 1.4426950408889634
_SIGN = 0x80000000


def _sb_block(q, k, scale, upper2, causal):
    z2 = lax.dot_general(q, k, NT, preferred_element_type=F32) * (scale * LOG2E)
    neg_abs = lax.bitcast_convert_type(lax.bitcast_convert_type(z2, jnp.uint32) | jnp.uint32(_SIGN), F32)
    l2 = jnp.log(1.0 + jnp.exp2(neg_abs)) * LOG2E
    log_beta = jnp.minimum(z2, 0.0) - l2
    lom = log_beta - z2
    if causal is not None:
        lom = jnp.where(causal, lom, 0.0)
    sx = _cumsum_mm(lom, upper2)
    return log_beta, sx, sx[:, 0:1] + lom[:, 0:1]


DEAD_LOG2 = -150.0


def _max_carry(carries):
    return jnp.max(functools.reduce(jnp.maximum, carries))


Q_GROUP, K_GROUP, V_GROUP, ZB_GROUP, GA_GROUP, GB_GROUP = 3, 4, 5, 6, 7, 8


def _attn_specs(d, seq, nq):
    hp_w = ATT_HP * (d // HEADS)
    n_hp = d // hp_w
    row_blk = lambda group: pl.BlockSpec((ATT_T, hp_w), lambda b, h, i: (b * nq + i, group * n_hp + h))
    seq_blk = lambda group: pl.BlockSpec((seq, hp_w), lambda b, h, i: (b, group * n_hp + h))
    return row_blk, seq_blk, n_hp


def _attn_fwd(proj, d, bsz, seq):
    t = proj.shape[0]
    hd = d // HEADS
    nq = seq // ATT_T
    scale = hd ** -0.5
    row_blk, seq_blk, n_hp = _attn_specs(d, seq, nq)

    def body(q_ref, k32_ref, v32_ref, zb_ref, o_ref, yb_ref, q16_ref, k_ref, v_ref):
        i = pl.program_id(2)
        causal = _tri(ATT_T, True)
        upper2 = _twice(causal)

        @pl.when(i == 0)
        def _():
            k_ref[...] = k32_ref[...].astype(BF16)
            v_ref[...] = v32_ref[...].astype(BF16)

        q16_ref[...] = q_ref[...].astype(BF16)

        def step(kb, state, mask):
            rows = pl.ds(pl.multiple_of(kb * ATT_T, ATT_T), ATT_T)
            heads = [slice(h * hd, (h + 1) * hd) for h in range(ATT_HP)]
            scores = [_sb_block(q16_ref[:, cols], k_ref[rows, cols], scale, upper2, mask) for cols in heads]
            new = []
            for cols, (carry, acc), (log_beta, sx, total) in zip(heads, state, scores):
                a = jnp.exp2(log_beta + sx + carry)
                if mask is not None:
                    a = jnp.where(mask, a, 0.0)
                acc = acc + jnp.dot(a.astype(BF16), v_ref[rows, cols], preferred_element_type=F32)
                new.append((carry + total, acc))
            return tuple(new)

        init = tuple((jnp.zeros((ATT_T, 1), F32), jnp.zeros((ATT_T, hd), F32)) for _ in range(ATT_HP))
        state = step(i, init, causal)
        def more(c):
            new = step(c[0], c[1], None)
            return c[0] - 1, new, _max_carry([s[0] for s in new])

        _, state, _ = lax.while_loop(lambda c: (c[0] >= 0) & (c[2] > DEAD_LOG2), more,
                                     (i - 1, state, _max_carry([s[0] for s in state])))
        for h in range(ATT_HP):
            cols = slice(h * hd, (h + 1) * hd)
            acc = state[h][1]
            zb = zb_ref[:, cols]
            o_ref[:, cols] = acc
            yb_ref[:, cols] = (acc * (zb * _sigmoid(zb))).astype(BF16)

    return pl.pallas_call(
        body, name="attn_fwd",
        grid=(bsz, n_hp, nq),
        in_specs=[row_blk(Q_GROUP), seq_blk(K_GROUP), seq_blk(V_GROUP), row_blk(ZB_GROUP)],
        out_specs=[row_blk(0), row_blk(0), row_blk(0), seq_blk(0), seq_blk(0)],
        out_shape=[jax.ShapeDtypeStruct((t, d), F32)] + [jax.ShapeDtypeStruct((t, d), BF16)] * 4,
        compiler_params=_params(("arbitrary", "arbitrary", "arbitrary")),
    )(proj, proj, proj, proj)


def _attn_bwd(q16, k16, v16, proj, o, dyb, bsz, seq):
    t, d = q16.shape
    hd = d // HEADS
    nq = seq // ATT_T
    scale = hd ** -0.5
    row_blk, seq_blk, n_hp = _attn_specs(d, seq, nq)

    def body(q_ref, k_ref, v_ref, zb_ref, o_ref, dyb_ref, dq_ref, dk_ref, dv_ref, dzb_ref,
             g_s, beta_s, dkt_acc, dvt_acc):
        i = pl.program_id(2)

        @pl.when(i == 0)
        def _():
            dkt_acc[...] = jnp.zeros_like(dkt_acc)
            dvt_acc[...] = jnp.zeros_like(dvt_acc)

        causal = _tri(ATT_T, True)
        upper2 = _twice(causal)
        lower2 = _twice(~causal)
        zb = zb_ref[...]
        sig = _sigmoid(zb)
        dyb_t = dyb_ref[...]
        do_f = dyb_t * (zb * sig)
        do = do_f.astype(BF16)
        do_t = do_f.T.astype(BF16)
        q_t = q_ref[...].astype(F32).T.astype(BF16)
        dzb_ref[...] = (dyb_t * o_ref[...] * (sig * (1.0 + zb * (1.0 - sig)))).astype(BF16)

        def sweep(kb, carries, mask):
            rows = pl.ds(pl.multiple_of(kb * ATT_T, ATT_T), ATT_T)
            heads = [slice(h * hd, (h + 1) * hd) for h in range(ATT_HP)]
            scores = [_sb_block(q_ref[:, cols], k_ref[rows, cols], scale, upper2, mask) for cols in heads]
            das = [lax.dot_general(do[:, cols], v_ref[rows, cols], NT, preferred_element_type=F32) for cols in heads]
            new = []
            for h, (cols, carry, (log_beta, sx, total), da) in enumerate(zip(heads, carries, scores, das)):
                a = jnp.exp2(log_beta + sx + carry)
                beta = jnp.exp2(log_beta)
                if mask is not None:
                    a = jnp.where(mask, a, 0.0)
                    beta = jnp.where(mask, beta, 0.0)
                g_s[h, kb] = a * da
                beta_s[h, kb] = beta
                dvt_acc[kb, cols, :] += jnp.dot(do_t[cols, :], a.astype(BF16), preferred_element_type=F32)
                new.append(carry + total)
            return tuple(new)

        carries = sweep(i, tuple(jnp.zeros((ATT_T, 1), F32) for _ in range(ATT_HP)), causal)

        def more(c):
            new = sweep(c[0], c[1], None)
            return c[0] - 1, new, _max_carry(new)

        last, _, _ = lax.while_loop(lambda c: (c[0] >= 0) & (c[2] > DEAD_LOG2), more, (i - 1, carries, _max_carry(carries)))
        first_kb = last + 1

        def back(kb, state):
            rows = pl.ds(pl.multiple_of(kb * ATT_T, ATT_T), ATT_T)
            heads = [slice(h * hd, (h + 1) * hd) for h in range(ATT_HP)]
            sums = [_cumsum_mm(g_s[h, kb], lower2) for h in range(ATT_HP)]
            new = []
            for h, (cols, (p_carry, dq), px) in enumerate(zip(heads, state, sums)):
                dz = ((g_s[h, kb] - (p_carry + px) * beta_s[h, kb]) * scale).astype(BF16)
                dq = dq + jnp.dot(dz, k_ref[rows, cols], preferred_element_type=F32)
                dkt_acc[kb, cols, :] += jnp.dot(q_t[cols, :], dz, preferred_element_type=F32)
                new.append((p_carry + px[:, ATT_T - 1:ATT_T], dq))
            return tuple(new)

        init = tuple((jnp.zeros((ATT_T, 1), F32), jnp.zeros((ATT_T, hd), F32)) for _ in range(ATT_HP))
        state = lax.fori_loop(first_kb, i + 1, back, init)
        for h in range(ATT_HP):
            dq_ref[:, h * hd:(h + 1) * hd] = state[h][1].astype(BF16)

        @pl.when(i == nq - 1)
        def _():
            for kb in range(nq):
                dk_ref[kb * ATT_T:(kb + 1) * ATT_T, :] = dkt_acc[kb].T.astype(BF16)
                dv_ref[kb * ATT_T:(kb + 1) * ATT_T, :] = dvt_acc[kb].T.astype(BF16)

    out = jax.ShapeDtypeStruct((t, d), BF16)
    hp_w = ATT_HP * hd
    return pl.pallas_call(
        body, name="attn_bwd",
        grid=(bsz, n_hp, nq),
        in_specs=[row_blk(0), seq_blk(0), seq_blk(0), row_blk(ZB_GROUP), row_blk(0), row_blk(0)],
        out_specs=[row_blk(0), seq_blk(0), seq_blk(0), row_blk(0)],
        out_shape=[out, out, out, out],
        scratch_shapes=[pltpu.VMEM((ATT_HP, nq, ATT_T, ATT_T), F32), pltpu.VMEM((ATT_HP, nq, ATT_T, ATT_T), F32),
                        pltpu.VMEM((nq, hp_w, ATT_T), F32), pltpu.VMEM((nq, hp_w, ATT_T), F32)],
        compiler_params=_params(("arbitrary", "arbitrary", "arbitrary")),
    )(q16, k16, v16, proj, o, dyb)


def _out_proj(ya, yb, g_pre, x2d, tgt, w_og, w_osb, w_out, g_f):
    t, d = x2d.shape

    def body(ya_ref, yb_ref, ga_ref, gb_ref, x_ref, tgt_ref, wog_ref, wosb_ref, wout_ref, gf_ref,
             dya_ref, dyb_ref, dg_ref, dx2_ref, loss_ref, gnf_ref, gwog_ref, gwosb_ref, gwout_ref):
        @pl.when(pl.program_id(0) == 0)
        def _():
            loss_ref[...] = jnp.zeros_like(loss_ref)
            gnf_ref[...] = jnp.zeros_like(gnf_ref)
            gwog_ref[...] = jnp.zeros_like(gwog_ref)
            gwosb_ref[...] = jnp.zeros_like(gwosb_ref)
            gwout_ref[...] = jnp.zeros_like(gwout_ref)

        ya = ya_ref[...]
        yb = yb_ref[...]
        pa = jnp.dot(ya, wog_ref[...], preferred_element_type=F32)
        pb = jnp.dot(yb, wosb_ref[...], preferred_element_type=F32)
        sga = _sigmoid(ga_ref[...])
        sgb = _sigmoid(gb_ref[...])
        merged = (sga * pa + sgb * pb).astype(BF16)
        x2 = x_ref[...] + jnp.dot(merged, wout_ref[...], preferred_element_type=F32)
        r2 = lax.rsqrt(jnp.mean(x2 * x2, axis=-1, keepdims=True) + EPS)
        n2 = x2 * r2
        gf = gf_ref[...]
        err = n2 * gf - tgt_ref[...]
        loss_ref[...] += 0.5 * jnp.sum(jnp.sum(err * err, axis=-1, keepdims=True), axis=0, keepdims=True) / d
        dy = err * (1.0 / d)
        gnf_ref[...] += jnp.sum(dy * n2, axis=0, keepdims=True)
        dn = dy * gf
        dx2 = r2 * (dn - n2 * jnp.mean(dn * n2, axis=-1, keepdims=True))
        dx2_ref[...] = dx2
        dx2_b = dx2.astype(BF16)
        dmerged = lax.dot_general(dx2_b, wout_ref[...], NT, preferred_element_type=F32)
        gwout_ref[...] += lax.dot_general(merged, dx2_b, TN, preferred_element_type=F32)
        dg_ref[:, 0:d] = (dmerged * pa * (sga * (1.0 - sga))).astype(BF16)
        dg_ref[:, d:2 * d] = (dmerged * pb * (sgb * (1.0 - sgb))).astype(BF16)
        dpa = (dmerged * sga).astype(BF16)
        dpb = (dmerged * sgb).astype(BF16)
        dya_ref[...] = lax.dot_general(dpa, wog_ref[...], NT, preferred_element_type=F32)
        dyb_ref[...] = lax.dot_general(dpb, wosb_ref[...], NT, preferred_element_type=F32)
        gwog_ref[...] += lax.dot_general(ya, dpa, TN, preferred_element_type=F32)
        gwosb_ref[...] += lax.dot_general(yb, dpb, TN, preferred_element_type=F32)

    row = lambda i: (i, 0)
    return pl.pallas_call(
        body, name="out_proj",
        grid=(t // TM,),
        in_specs=[pl.BlockSpec((TM, d), row), pl.BlockSpec((TM, d), row),
                  pl.BlockSpec((TM, d), lambda i: (i, GA_GROUP)), pl.BlockSpec((TM, d), lambda i: (i, GB_GROUP)),
                  pl.BlockSpec((TM, d), row), pl.BlockSpec((TM, d), row),
                  _resident((d, d)), _resident((d, d)), _resident((d, d)), _const((1, d))],
        out_specs=[pl.BlockSpec((TM, d), row), pl.BlockSpec((TM, d), row), pl.BlockSpec((TM, 2 * d), row),
                   pl.BlockSpec((TM, d), row), _const((1, 1)), _const((1, d)),
                   _const((d, d)), _const((d, d)), _const((d, d))],
        out_shape=[jax.ShapeDtypeStruct((t, d), F32), jax.ShapeDtypeStruct((t, d), F32),
                   jax.ShapeDtypeStruct((t, 2 * d), BF16), jax.ShapeDtypeStruct((t, d), F32),
                   jax.ShapeDtypeStruct((1, 1), F32), jax.ShapeDtypeStruct((1, d), F32),
                   jax.ShapeDtypeStruct((d, d), F32), jax.ShapeDtypeStruct((d, d), F32),
                   jax.ShapeDtypeStruct((d, d), F32)],
        compiler_params=_params(("arbitrary",)),
    )(ya, yb, g_pre, g_pre, x2d, tgt, w_og, w_osb, w_out, g_f)


def _dproj_pieces(d):
    return [(0, 0, 3), (1, 3, 1), (2, 4, 1), (3, 5, 1), (4, 6, 1), (5, 7, 2)]


def _in_proj_bwd_x(pieces, wg, x2d, g_in, dx2, gw16):
    t, d = x2d.shape
    ncol = wg.shape[2]
    segs = _segments(d, ncol)
    layout = _dproj_pieces(d)
    nsteps = t // TM

    def body(da_ref, dq_ref, dk_ref, dv_ref, dzb_ref, dg_ref, w_ref, x_ref, g_ref, dx2_ref, gw16_ref,
             gx_ref, gn_ref, recv_ref, send_sems, recv_sems):
        x_pos, y_pos, c_pos, chips = _place()

        def share(k, chunk):
            px, py = chips[k]
            return pltpu.make_async_remote_copy(
                src_ref=gw16_ref.at[:, chunk * ncol:(chunk + 1) * ncol], dst_ref=recv_ref.at[k],
                send_sem=send_sems.at[k], recv_sem=recv_sems.at[k], device_id=(px, py, c_pos), device_id_type=MESH)

        @pl.when(pl.program_id(0) == 0)
        def _():
            gn_ref[...] = jnp.zeros_like(gn_ref)
            for k, (px, py) in enumerate(chips):
                for chunk in range(N_CHIP):
                    @pl.when(2 * px + py == chunk)
                    def _(k=k, chunk=chunk):
                        share(k, chunk).start()

        @pl.when(pl.program_id(0) == nsteps - 1)
        def _():
            for k in range(N_CHIP - 1):
                share(k, 0).wait()

        refs = (da_ref, dq_ref, dk_ref, dv_ref, dzb_ref, dg_ref)
        dh = jnp.zeros((TM, d), F32)
        for chip, c0, grp, s0, width in segs:
            piece, first, _ = next(p for p in layout if p[1] <= grp < p[1] + p[2])
            off = (grp - first) * d + s0
            dh = dh + lax.dot_general(refs[piece][:, off:off + width], w_ref[chip, :, c0:c0 + width], NT,
                                      preferred_element_type=F32)
        x = x_ref[...]
        r = lax.rsqrt(jnp.mean(x * x, axis=-1, keepdims=True) + EPS)
        n = x * r
        gn_ref[...] += jnp.sum(dh * n, axis=0, keepdims=True)
        dn = dh * g_ref[...]
        gx_ref[...] = dx2_ref[...] + r * (dn - n * jnp.mean(dn * n, axis=-1, keepdims=True))

    row = lambda i: (i, 0)
    return pl.pallas_call(
        body, name="in_proj_bwd_x",
        grid=(t // TM,),
        in_specs=[pl.BlockSpec((TM, p.shape[1]), row) for p in pieces]
        + [_resident(wg.shape), pl.BlockSpec((TM, d), row), _const((1, d)), pl.BlockSpec((TM, d), row), ANY],
        out_specs=[pl.BlockSpec((TM, d), row), _const((1, d)), ANY],
        out_shape=[jax.ShapeDtypeStruct((t, d), F32), jax.ShapeDtypeStruct((1, d), F32),
                   jax.ShapeDtypeStruct((N_CHIP - 1, d, ncol), BF16)],
        scratch_shapes=[pltpu.SemaphoreType.DMA((N_CHIP - 1,)), pltpu.SemaphoreType.DMA((N_CHIP - 1,))],
        compiler_params=_params(("arbitrary",)),
    )(*pieces, wg, x2d, g_in, dx2, gw16)


def _in_proj_bwd_w(h_t, pieces, mats16, pack):
    d, t = h_t.shape
    nk = t // TKW
    layout = _dproj_pieces(d)
    n_mats = len(mats16)
    n_dev = 2 * N_CHIP
    flips = [(dx, dy, dc) for dx in (0, 1) for dy in (0, 1) for dc in (0, 1)][1:]

    def body(ht_ref, da_ref, dq_ref, dk_ref, dv_ref, dzb_ref, dg_ref, *rest):
        mat_refs, pack_ref = rest[:n_mats], rest[n_mats]
        gw_ref, gw16_ref = rest[n_mats + 1:n_mats + 3]
        recv_refs, slots_ref = rest[n_mats + 3:2 * n_mats + 3], rest[2 * n_mats + 3]
        acc, mat_send, mat_recv, pack_send, pack_recv, own_sem = rest[2 * n_mats + 4:]
        s = pl.program_id(0)
        i = pl.program_id(1)
        x_pos, y_pos, c_pos, chips = _place()
        me = 4 * x_pos + 2 * y_pos + c_pos

        def exchanges():
            cps = []
            for k, (px, py) in enumerate(chips):
                for a in range(n_mats):
                    cps.append(pltpu.make_async_remote_copy(
                        src_ref=mat_refs[a].at[2 * px + py], dst_ref=recv_refs[a].at[k],
                        send_sem=mat_send.at[a, k], recv_sem=mat_recv.at[a, k],
                        device_id=(px, py, c_pos), device_id_type=MESH))
            for k, (dx, dy, dc) in enumerate(flips):
                peer = (1 - x_pos if dx else x_pos, 1 - y_pos if dy else y_pos, 1 - c_pos if dc else c_pos)
                cps.append(pltpu.make_async_remote_copy(
                    src_ref=pack_ref, dst_ref=slots_ref.at[me], send_sem=pack_send.at[k], recv_sem=pack_recv.at[k],
                    device_id=peer, device_id_type=MESH))
            return cps, pltpu.make_async_copy(pack_ref, slots_ref.at[me], own_sem)

        @pl.when((s == 0) & (i == 0))
        def _():
            cps, own = exchanges()
            own.start()
            for cp in cps:
                cp.start()

        @pl.when((s == N_SPLIT - 1) & (i == nk - 1))
        def _():
            cps, own = exchanges()
            own.wait()
            for cp in cps:
                cp.wait()

        @pl.when(i == 0)
        def _():
            acc[...] = jnp.zeros_like(acc)

        refs = (da_ref, dq_ref, dk_ref, dv_ref, dzb_ref, dg_ref)
        for piece, first, count in layout:
            @pl.when((s >= first) & (s < first + count))
            def _(piece=piece):
                acc[...] += jnp.dot(ht_ref[...], refs[piece][...], preferred_element_type=F32)

        @pl.when(i == nk - 1)
        def _():
            gw_ref[...] = acc[...]
            gw16_ref[...] = acc[...].astype(BF16)

    def piece_spec(p, first, count):
        def index(s, i):
            mine = (s >= first) & (s < first + count)
            return jnp.where(mine, i, 0), jnp.where(mine, s - first, 0)
        return pl.BlockSpec((TKW, d), index)

    col_blk = pl.BlockSpec((d, d), lambda s, i: (0, s))
    outs = pl.pallas_call(
        body, name="in_proj_bwd_w",
        grid=(N_SPLIT, nk),
        in_specs=[pl.BlockSpec((d, TKW), lambda s, i: (0, i))] + [piece_spec(*p) for p in layout] + [ANY] * (n_mats + 1),
        out_specs=[col_blk, col_blk] + [ANY] * (n_mats + 1),
        out_shape=[jax.ShapeDtypeStruct((d, N_SPLIT * d), F32), jax.ShapeDtypeStruct((d, N_SPLIT * d), BF16)]
        + [jax.ShapeDtypeStruct((N_CHIP - 1,) + m.shape[1:], BF16) for m in mats16]
        + [jax.ShapeDtypeStruct((n_dev,) + pack.shape, F32)],
        scratch_shapes=[pltpu.VMEM((d, d), F32),
                        pltpu.SemaphoreType.DMA((n_mats, N_CHIP - 1)), pltpu.SemaphoreType.DMA((n_mats, N_CHIP - 1)),
                        pltpu.SemaphoreType.DMA((n_dev - 1,)), pltpu.SemaphoreType.DMA((n_dev - 1,)),
                        pltpu.SemaphoreType.DMA],
        compiler_params=_params(("arbitrary", "arbitrary")),
    )(h_t, *pieces, *mats16, pack)
    return outs[0], outs[1], outs[2:2 + n_mats], outs[2 + n_mats]


def _local_step(proj, x2d, tgt2d, bsz, seq, norm_v, w_s, b_s, w_og, w_osb, w_out, norm_final):
    d = x2d.shape[1]
    chunk = w_s.shape[-1]
    causal = jnp.tril(jnp.ones((chunk, chunk), dtype=bool))
    wm = jnp.where(causal[None], w_s, 0.0).astype(BF16)
    wm_t = jnp.swapaxes(wm, 1, 2)
    b_t = b_s.T

    ya = _branch_a_fwd(proj, norm_v, wm, b_t)
    o, yb, q16, k16, v16 = _attn_fwd(proj, d, bsz, seq)
    dya, dyb, dg, dx2, loss, g_nf, g_wog, g_wosb, g_wout = _out_proj(
        ya, yb, proj, x2d, tgt2d, w_og, w_osb, w_out, norm_final.reshape(1, d))
    dq, dk, dv, dzb = _attn_bwd(q16, k16, v16, proj, o, dyb, bsz, seq)
    d_a, g_ws, g_bt, g_nv = _branch_a_bwd(proj, dya, norm_v, wm, wm_t, b_t)
    g_ws = jnp.where(causal[None], g_ws, 0.0)
    return loss, (d_a, dq, dk, dv, dzb, dg), dx2, g_nv, g_ws, g_bt.T, g_wog, g_wosb, g_wout, g_nf


def _row_tile(rows):
    return next(r for r in (128, 64, 32, 16, 8) if rows % r == 0)


def _cast_bf16(arrs):
    n = len(arrs)

    def body(*refs):
        for a_ref, o_ref in zip(refs[:n], refs[n:]):
            o_ref[...] = a_ref[...].astype(BF16)

    specs = [pl.BlockSpec((a.shape[0] // CAST_STEPS, a.shape[1]), lambda i: (i, 0)) for a in arrs]
    return pl.pallas_call(
        body, name="cast_bf16", grid=(CAST_STEPS,),
        in_specs=specs, out_specs=specs,
        out_shape=[jax.ShapeDtypeStruct(a.shape, BF16) for a in arrs],
        compiler_params=_params(("arbitrary",)),
    )(*arrs)


def _add_received(full, recv, chip, by_cols):
    _, rows, cols = recv.shape
    tr = _row_tile(rows)
    nb = rows // tr

    def body(chip_ref, own_ref, recv_ref, o_ref):
        s = own_ref[...]
        for k in range(N_CHIP - 1):
            s = s + recv_ref[k].astype(F32)
        o_ref[...] = s

    own_map = (lambda i, chip_ref: (i, chip_ref[0])) if by_cols else (lambda i, chip_ref: (chip_ref[0] * nb + i, 0))
    return pl.pallas_call(
        body, name="add_received",
        grid_spec=pltpu.PrefetchScalarGridSpec(
            num_scalar_prefetch=1, grid=(nb,),
            in_specs=[pl.BlockSpec((tr, cols), own_map),
                      pl.BlockSpec((N_CHIP - 1, tr, cols), lambda i, chip_ref: (0, i, 0))],
            out_specs=pl.BlockSpec((tr, cols), lambda i, chip_ref: (i, 0))),
        out_shape=jax.ShapeDtypeStruct((rows, cols), F32),
        compiler_params=_params(("arbitrary",)),
    )(chip.reshape(1).astype(jnp.int32), full, recv)


def _adamw_math(w, m, v, g):
    new_m = ADAM_B1 * m + (1.0 - ADAM_B1) * g
    new_v = ADAM_B2 * v + (1.0 - ADAM_B2) * (g * g)
    m_hat = new_m / (1.0 - ADAM_B1 ** ADAM_STEP)
    v_hat = new_v / (1.0 - ADAM_B2 ** ADAM_STEP)
    return -ADAM_LR * (m_hat / (jnp.sqrt(v_hat) + ADAM_EPS) + ADAM_WD * w), new_m, new_v


def _adamw(w, m, v, g_parts):
    rows, cols = w.shape
    tr = _row_tile(rows)
    n_parts = len(g_parts)

    def body(*refs):
        w_ref, m_ref, v_ref = refs[:3]
        part_refs = refs[3:3 + n_parts]
        g_ref, d_ref, nm_ref, nv_ref = refs[3 + n_parts:]
        g = part_refs[0][...]
        for p in part_refs[1:]:
            g = g + p[...]
        g_ref[...] = g
        d_ref[...], nm_ref[...], nv_ref[...] = _adamw_math(w_ref[...], m_ref[...], v_ref[...], g)

    spec = pl.BlockSpec((tr, cols), lambda i: (i, 0))
    out = jax.ShapeDtypeStruct(w.shape, F32)
    return pl.pallas_call(
        body, name="adamw", grid=(rows // tr,),
        in_specs=[spec] * (3 + n_parts), out_specs=[spec] * 4, out_shape=[out] * 4,
        compiler_params=_params(("arbitrary",)),
    )(w, m, v, *g_parts)


def _adamw_small(w, m, v, slots_head, slots_tail):
    n_dev, p0, _ = slots_head.shape

    def body(w_ref, m_ref, v_ref, head_ref, tail_ref, g_ref, d_ref, nm_ref, nv_ref):
        for ref, rows in ((head_ref, slice(0, p0)), (tail_ref, slice(p0, w.shape[0]))):
            g = ref[0]
            for i in range(1, n_dev):
                g = g + ref[i]
            g_ref[rows, :] = g
            d_ref[rows, :], nm_ref[rows, :], nv_ref[rows, :] = _adamw_math(w_ref[rows, :], m_ref[rows, :], v_ref[rows, :], g)

    vmem = pl.BlockSpec(memory_space=pltpu.VMEM)
    out = jax.ShapeDtypeStruct(w.shape, F32)
    return pl.pallas_call(
        body, name="adamw_small", in_specs=[vmem] * 5, out_specs=[vmem] * 4, out_shape=[out] * 4,
        compiler_params=pltpu.CompilerParams(vmem_limit_bytes=VMEM_LIMIT),
    )(w, m, v, slots_head, slots_tail)


ANY = pl.BlockSpec(memory_space=pl.ANY)


def _place():
    x, y, c = lax.axis_index("x"), lax.axis_index("y"), lax.axis_index("c")
    other_chips = [(1 - x, y), (x, 1 - y), (1 - x, 1 - y)]
    return x, y, c, other_chips


def _swap_and_gather(arrs, pack):
    n = len(arrs)
    n_dev = 2 * N_CHIP
    flips = [(dx, dy, dc) for dx in (0, 1) for dy in (0, 1) for dc in (0, 1)][1:]

    def body(*refs):
        ins, pack_ref = refs[:n], refs[n]
        outs, slots_ref = refs[n + 1:2 * n + 1], refs[2 * n + 1]
        send_sems, recv_sems, pack_send, pack_recv, own_sem = refs[2 * n + 2:]
        x, y, c, _ = _place()
        me = 4 * x + 2 * y + c
        own = pltpu.make_async_copy(pack_ref, slots_ref.at[me], own_sem)
        own.start()
        copies = []
        for k, (dx, dy, dc) in enumerate(flips):
            peer = (1 - x if dx else x, 1 - y if dy else y, 1 - c if dc else c)
            copies.append(pltpu.make_async_remote_copy(
                src_ref=pack_ref, dst_ref=slots_ref.at[me], send_sem=pack_send.at[k], recv_sem=pack_recv.at[k],
                device_id=peer, device_id_type=MESH))
        copies += [pltpu.make_async_remote_copy(
            src_ref=ins[a], dst_ref=outs[a], send_sem=send_sems.at[a], recv_sem=recv_sems.at[a],
            device_id=(x, y, 1 - c), device_id_type=MESH) for a in range(n)]
        for cp in copies:
            cp.start()
        for cp in copies:
            cp.wait()
        own.wait()

    return pl.pallas_call(
        body, name="swap_and_gather",
        in_specs=[ANY] * (n + 1), out_specs=[ANY] * (n + 1),
        out_shape=[jax.ShapeDtypeStruct(a.shape, a.dtype) for a in arrs] + [jax.ShapeDtypeStruct((n_dev,) + pack.shape, F32)],
        scratch_shapes=[pltpu.SemaphoreType.DMA((n,)), pltpu.SemaphoreType.DMA((n,)),
                        pltpu.SemaphoreType.DMA((n_dev - 1,)), pltpu.SemaphoreType.DMA((n_dev - 1,)),
                        pltpu.SemaphoreType.DMA],
    )(*arrs, pack)


SLAB = 8


def _slab(vec, d):
    return jnp.pad(vec.reshape(1, d), ((0, SLAB - 1), (0, 0)))


def _pack_tail(vec_nv, b_s, vec_nf, w_s):
    d = vec_nv.shape[-1]
    return jnp.concatenate([_slab(v, d) for v in (vec_nv, b_s, vec_nf)] + [w_s.reshape(-1, d)], axis=0)


def _pack_small(vec_nin, vec_nv, b_s, vec_nf, w_s):
    return jnp.concatenate([_slab(vec_nin, vec_nin.shape[-1]), _pack_tail(vec_nv, b_s, vec_nf, w_s)], axis=0)


def _unpack_small(pack, w_s_shape, b_s_shape):
    return (pack[0:1], pack[SLAB:SLAB + 1], pack[2 * SLAB].reshape(b_s_shape), pack[3 * SLAB],
            pack[4 * SLAB:].reshape(w_s_shape))


def kernel(x, norm_in, w_in, norm_v, w_s, b_s, w_o_gmlp, w_o_sb, w_out, norm_final, loss_target, m_norm_in, m_w_in, m_norm_v, m_w_s, m_b_s, m_w_o_gmlp, m_w_o_sb, m_w_out, m_norm_final, v_norm_in, v_w_in, v_norm_v, v_w_s, v_b_s, v_w_o_gmlp, v_w_o_sb, v_w_out, v_norm_final):
    d = x.shape[-1]
    ncol = w_in.shape[-1]
    nrow = w_o_gmlp.shape[-2]
    chip = 2 * lax.axis_index("x") + lax.axis_index("y")

    bsz, seq, _ = x.shape
    x2d = x.reshape(bsz * seq, d)
    shards = [w_in[0], w_o_gmlp[0], w_o_sb[0], w_out[0]]
    halves = [s16.reshape(2, s16.shape[0] // 2, s16.shape[1]) for s16 in _cast_bf16(shards)]
    proj, h_t, (wg, w_og, w_osb, w_o) = _in_proj_fwd(x2d, norm_in, halves)
    wg = wg.reshape(N_CHIP, d, ncol)

    loss, pieces, dx2, g_nv, g_ws, g_bs, g_wog, g_wosb, g_wout, g_nf = _local_step(
        proj, x2d, loss_target.reshape(bsz * seq, d), bsz, seq, norm_v, w_s[0], b_s[0],
        w_og.reshape(d, d), w_osb.reshape(d, d), w_o.reshape(d, d), norm_final)

    mats = [g_wog, g_wosb, g_wout]
    mats16 = [g16.reshape(N_CHIP, nrow, d) for g16 in _cast_bf16(mats)]
    g_win, g_win16, recv_mats, slots_tail = _in_proj_bwd_w(h_t, pieces, mats16, _pack_tail(g_nv, g_bs, g_nf, g_ws))
    grad_x, g_nin, recv_win = _in_proj_bwd_x(pieces, wg, x2d, norm_in, dx2, g_win16)
    grad_x = grad_x.reshape(bsz, seq, d)

    sums = [_add_received(g_win, recv_win, chip, True)] + [
        _add_received(g, r, chip, False) for g, r in zip(mats, recv_mats)]
    *sibling_sums, slots_head = _swap_and_gather(sums, _slab(g_nin, d))
    stats = []
    for w, m, v, mine, theirs in zip(shards, [m_w_in[0], m_w_o_gmlp[0], m_w_o_sb[0], m_w_out[0]],
                                     [v_w_in[0], v_w_o_gmlp[0], v_w_o_sb[0], v_w_out[0]], sums, sibling_sums):
        stats.append(_adamw(w, m, v, [mine, theirs]))
    (gw_in, dw_in, nm_in, nv_in), (gw_og, dw_og, nm_og, nv_og), (gw_osb, dw_osb, nm_osb, nv_osb), \
        (gw_out, dw_out, nm_out, nv_out) = stats

    gs, ds, ms, vs = _adamw_small(
        _pack_small(norm_in, norm_v, b_s[0], norm_final, w_s[0]),
        _pack_small(m_norm_in, m_norm_v, m_b_s[0], m_norm_final, m_w_s[0]),
        _pack_small(v_norm_in, v_norm_v, v_b_s[0], v_norm_final, v_w_s[0]), slots_head, slots_tail)

    def small(pack):
        nin, nv, bs, nf, ws = _unpack_small(pack, w_s.shape, b_s.shape)
        return nin, nv, ws, bs, nf

    loss = lax.psum(loss[0, 0], ("x", "y", "c"))
    out = []
    for small_pack, win, wog, wosb, wout in ((gs, gw_in, gw_og, gw_osb, gw_out), (ds, dw_in, dw_og, dw_osb, dw_out),
                                             (ms, nm_in, nm_og, nm_osb, nm_out), (vs, nv_in, nv_og, nv_osb, nv_out)):
        nin, nv, ws, bs, nf = small(small_pack)
        out += [nin, win[None], nv, ws, bs, wog[None], wosb[None], wout[None], nf]
    return (loss, grad_x, *out)
```

```python
import functools
import math

import jax
import jax.numpy as jnp
from jax import lax
from jax.experimental import pallas as pl
from jax.experimental.pallas import tpu as pltpu

F32 = jnp.float32
BF16 = jnp.bfloat16
EPS = 1e-6
HEADS = 8
N_SPLIT = 9
N_CHIP = 4
MESH = pl.DeviceIdType.MESH

ADAM_LR = 0.001
ADAM_B1 = 0.9
ADAM_B2 = 0.999
ADAM_EPS = 1e-08
ADAM_WD = 0.01
ADAM_STEP = 10

VMEM_LIMIT = 56 * 2 ** 20
TM = 256
TMF = 512
ATT_T = 256
ATT_HP = 4
TKW = 1024
CHUNKS_PER_STEP = 4
CAST_STEPS = 8

NT = (((1,), (1,)), ((), ()))
TN = (((0,), (0,)), ((), ()))


def _params(sem):
    return pltpu.CompilerParams(dimension_semantics=sem, vmem_limit_bytes=VMEM_LIMIT)


def _resident(shape):
    nd = len(shape)
    return pl.BlockSpec(shape, lambda *_: (0,) * nd, pipeline_mode=pl.Buffered(1))


def _const(shape):
    nd = len(shape)
    return pl.BlockSpec(shape, lambda *_: (0,) * nd)


def _segments(d, ncol):
    segs = []
    edges = sorted({j * ncol for j in range(N_CHIP + 1)} | {s * d for s in range(N_SPLIT + 1)})
    for lo, hi in zip(edges[:-1], edges[1:]):
        segs.append((lo // ncol, lo % ncol, lo // d, lo % d, hi - lo))
    return segs


def _sigmoid(x):
    return 1.0 / (1.0 + jnp.exp(-x))


_GELU_C = math.sqrt(2.0 / math.pi)


def _gelu(x):
    return 0.5 * x * (1.0 + jnp.tanh(_GELU_C * (x + 0.044715 * (x * x * x))))


def _gelu_and_grad(x):
    x2 = x * x
    th = jnp.tanh(_GELU_C * (x + 0.044715 * (x2 * x)))
    val = 0.5 * x * (1.0 + th)
    grad = 0.5 * (1.0 + th) + 0.5 * x * (1.0 - th * th) * (_GELU_C * (1.0 + 3.0 * 0.044715 * x2))
    return val, grad


def _split_bf16(a):
    hi = a.astype(BF16)
    lo = (a - hi.astype(F32)).astype(BF16)
    return hi, lo


def _in_proj_fwd(x2d, g_in, halves):
    t, d = x2d.shape
    n = len(halves)
    ncol = halves[0].shape[2]
    n_row = t // TMF
    last = n_row - 1
    assert halves[0].shape[1] * 2 == d
    qkv_parts = {j: (max(j * ncol, 3 * d) - j * ncol, max(j * ncol, 3 * d) - 3 * d)
                 for j in range(N_CHIP) if min((j + 1) * ncol, 6 * d) > max(j * ncol, 3 * d)}
    qkv_w = 3 * d // len(qkv_parts)
    assert all(min((j + 1) * ncol, 6 * d) - max(j * ncol, 3 * d) == qkv_w and q0 % qkv_w == 0
               for j, (_, q0) in qkv_parts.items())

    def body(order_ref, qrow_ref, qcol_ref, x_ref, g_ref, *rest):
        ins = rest[:n]
        proj_ref, qkv_ref, ht_ref = rest[n:n + 3]
        outs = rest[n + 3:2 * n + 3]
        wbuf, h_all, send_sems, recv_sems, local_sems, load_sem = rest[2 * n + 3:]
        phase = pl.program_id(0)
        i = pl.program_id(1)
        x_pos, y_pos, c_pos, chips = _place()
        sibling = (x_pos, y_pos, 1 - c_pos)
        me = (x_pos, y_pos, c_pos)
        my_chip = 2 * x_pos + y_pos

        def copy(a, k, block, to, src=None):
            return pltpu.make_async_remote_copy(
                src_ref=outs[a].at[block] if src is None else src, dst_ref=outs[a].at[block],
                send_sem=send_sems.at[a, k], recv_sem=recv_sems.at[a, k], device_id=to, device_id_type=MESH)

        def local(a):
            return pltpu.make_async_copy(ins[a], outs[a].at[pl.ds(2 * my_chip, 2)], local_sems.at[a])

        def load(src, first, slot):
            for half in range(2):
                cp = pltpu.make_async_copy(src.at[first + half], wbuf.at[slot, pl.ds(half * (d // 2), d // 2)], load_sem)
                cp.start()
                cp.wait()

        def send_mine(k):
            px, py = chips[k]
            for a in range(n):
                copy(a, k, 2 * my_chip + c_pos, (px, py, c_pos), src=ins[a].at[c_pos]).start()

        @pl.when((phase == 0) & (i == 0))
        def _():
            for a in range(n):
                local(a).start()
            send_mine(0)
            send_mine(1)
            load(ins[0], 0, 0)

        for k, (px, py) in enumerate(chips):
            @pl.when((phase == k + 1) & (i == 0))
            def _(k=k, px=px, py=py):
                theirs = 2 * (2 * px + py)
                for a in range(n):
                    copy(a, k, theirs + c_pos, me).wait_recv()
                    copy(a, 3 + k, theirs + c_pos, sibling).start()
                if k == 0:
                    send_mine(2)
                for a in range(n):
                    copy(a, 3 + k, theirs + 1 - c_pos, me).wait_recv()
                load(outs[0], theirs, (k + 1) % 2)

        @pl.when(phase == 0)
        def _():
            x = x_ref[...]
            r = lax.rsqrt(jnp.mean(x * x, axis=-1, keepdims=True) + EPS)
            hf = x * r * g_ref[...]
            h_all[i] = hf.astype(BF16)
            ht_ref[...] = hf.T.astype(BF16)

        for slot in range(2):
            @pl.when(phase % 2 == slot)
            def _(slot=slot):
                proj_ref[...] = jnp.dot(h_all[i], wbuf[slot], preferred_element_type=F32)

        for chunk, (c0, _) in qkv_parts.items():
            @pl.when(order_ref[phase] == chunk)
            def _(c0=c0):
                qkv_ref[...] = proj_ref[:, c0:c0 + qkv_w].astype(BF16)

        @pl.when((phase == N_CHIP - 1) & (i == n_row - 1))
        def _():
            for a in range(n):
                for k in range(2 * (N_CHIP - 1)):
                    copy(a, k, 0, me).wait_send()
                local(a).wait()

    x_pos, y_pos = lax.axis_index("x"), lax.axis_index("y")
    order = jnp.stack([2 * x_pos + y_pos, 2 * (1 - x_pos) + y_pos, 2 * x_pos + 1 - y_pos,
                       2 * (1 - x_pos) + 1 - y_pos]).astype(jnp.int32)
    holds = [functools.reduce(jnp.logical_or, [order[p] == j for j in qkv_parts]) for p in range(N_CHIP)]
    col = [sum(jnp.where(order[p] == j, q0 // qkv_w, 0) for j, (_, q0) in qkv_parts.items()) for p in range(N_CHIP)]
    cur = col[-1]
    for p in reversed(range(N_CHIP - 1)):
        cur = jnp.where(holds[p], col[p], cur)
    seen = jnp.bool_(False)
    qrow, qcol = [], []
    for p in range(N_CHIP):
        cur = jnp.where(holds[p], col[p], cur)
        qrow.append(jnp.where(holds[p], -1, jnp.where(seen, last, 0)))
        qcol.append(cur)
        seen = seen | holds[p]
    qrow = jnp.stack(qrow).astype(jnp.int32)
    qcol = jnp.stack(qcol).astype(jnp.int32)

    outs = pl.pallas_call(
        body, name="in_proj_fwd",
        grid_spec=pltpu.PrefetchScalarGridSpec(
            num_scalar_prefetch=3, grid=(N_CHIP, n_row),
            in_specs=[pl.BlockSpec((TMF, d), lambda p, i, order, qrow, qcol: (jnp.where(p == 0, i, last), 0)),
                      pl.BlockSpec((1, d), lambda p, i, order, qrow, qcol: (0, 0))] + [ANY] * n,
            out_specs=[pl.BlockSpec((TMF, ncol), lambda p, i, order, qrow, qcol: (i, order[p])),
                       pl.BlockSpec((TMF, qkv_w),
                                    lambda p, i, order, qrow, qcol: (jnp.where(qrow[p] < 0, i, qrow[p]), qcol[p])),
                       pl.BlockSpec((d, TMF), lambda p, i, order, qrow, qcol: (0, jnp.where(p == 0, i, last)))]
            + [ANY] * n,
            scratch_shapes=[pltpu.VMEM((2, d, ncol), BF16), pltpu.VMEM((n_row, TMF, d), BF16),
                            pltpu.SemaphoreType.DMA((n, 2 * (N_CHIP - 1))), pltpu.SemaphoreType.DMA((n, 2 * (N_CHIP - 1))),
                            pltpu.SemaphoreType.DMA((n,)), pltpu.SemaphoreType.DMA]),
        out_shape=[jax.ShapeDtypeStruct((t, N_CHIP * ncol), F32), jax.ShapeDtypeStruct((t, 3 * d), BF16),
                   jax.ShapeDtypeStruct((d, t), BF16)]
        + [jax.ShapeDtypeStruct((2 * N_CHIP,) + hv.shape[1:], BF16) for hv in halves],
        compiler_params=_params(("arbitrary", "arbitrary")),
    )(order, qrow, qcol, x2d, g_in, *halves)
    return outs[0], outs[1], outs[2], outs[3:]


def _branch_a_fwd(a_pre, g_v, wm, b_t):
    t = a_pre.shape[0]
    d = g_v.shape[1]
    d3 = 3 * d
    ng, chunk, _ = wm.shape
    cw = d // ng

    per_step = CHUNKS_PER_STEP if t % (CHUNKS_PER_STEP * chunk) == 0 else 1

    def body(a_ref, gv_ref, wm_ref, bt_ref, ya_ref):
        for n in range(per_step):
            rows = slice(n * chunk, (n + 1) * chunk)
            ua = _gelu(a_ref[rows, 0:d])
            vg = _gelu(a_ref[rows, d:2 * d])
            za = a_ref[rows, 2 * d:3 * d]
            rv = lax.rsqrt(jnp.mean(vg * vg, axis=-1, keepdims=True) + EPS)
            va = (vg * rv * gv_ref[...]).astype(BF16)
            gate = ua * (za * _sigmoid(za))
            for g in range(ng):
                sl = slice(g * cw, (g + 1) * cw)
                mixed = jnp.dot(wm_ref[g], va[:, sl], preferred_element_type=F32) + bt_ref[:, g:g + 1]
                ya_ref[rows, sl] = (gate[:, sl] * mixed).astype(BF16)

    tile = per_step * chunk
    return pl.pallas_call(
        body, name="branch_a_fwd",
        grid=(t // tile,),
        in_specs=[pl.BlockSpec((tile, d3), lambda i: (i, 0)), _const((1, d)), _const(wm.shape), _const(b_t.shape)],
        out_specs=pl.BlockSpec((tile, d), lambda i: (i, 0)),
        out_shape=jax.ShapeDtypeStruct((t, d), BF16),
        compiler_params=_params(("arbitrary",)),
    )(a_pre, g_v, wm, b_t)


def _branch_a_bwd(a_pre, dya, g_v, wm, wm_t, b_t):
    t = a_pre.shape[0]
    d = g_v.shape[1]
    d3 = 3 * d
    ng, chunk, _ = wm.shape
    cw = d // ng
    nsteps = t // chunk

    def body(a_ref, dya_ref, gv_ref, wm_ref, wmt_ref, bt_ref, da_ref, gws_ref, gbt_ref, gnv_ref, db_acc):
        i = pl.program_id(0)

        @pl.when(i == 0)
        def _():
            gws_ref[...] = jnp.zeros_like(gws_ref)
            gnv_ref[...] = jnp.zeros_like(gnv_ref)
            db_acc[...] = jnp.zeros_like(db_acc)

        ua, dgelu_u = _gelu_and_grad(a_ref[:, 0:d])
        vg, dgelu_v = _gelu_and_grad(a_ref[:, d:2 * d])
        za = a_ref[:, 2 * d:3 * d]
        sig = _sigmoid(za)
        sz = za * sig
        dsz = sig * (1.0 + za * (1.0 - sig))
        rv = lax.rsqrt(jnp.mean(vg * vg, axis=-1, keepdims=True) + EPS)
        nv = vg * rv
        gv = gv_ref[...]
        va = (nv * gv).astype(BF16)
        dya = dya_ref[...]
        dmix = dya * ua * sz
        db_acc[...] += dmix
        dmix_b = dmix.astype(BF16)
        t_gate = dya * sz
        t_z = dya * ua * dsz
        dva_parts = []
        for g in range(ng):
            sl = slice(g * cw, (g + 1) * cw)
            mixed = jnp.dot(wm_ref[g], va[:, sl], preferred_element_type=F32) + bt_ref[:, g:g + 1]
            da_ref[:, sl] = (t_gate[:, sl] * mixed * dgelu_u[:, sl]).astype(BF16)
            da_ref[:, 2 * d + g * cw:2 * d + (g + 1) * cw] = (t_z[:, sl] * mixed).astype(BF16)
            gws_ref[g] += lax.dot_general(dmix_b[:, sl], va[:, sl], NT, preferred_element_type=F32)
            dva_parts.append(jnp.dot(wmt_ref[g], dmix_b[:, sl], preferred_element_type=F32))
        dva = jnp.concatenate(dva_parts, axis=1)
        gnv_ref[...] += jnp.sum(dva * nv, axis=0, keepdims=True)
        dnv = dva * gv
        dvg = rv * (dnv - nv * jnp.mean(dnv * nv, axis=-1, keepdims=True))
        da_ref[:, d:2 * d] = (dvg * dgelu_v).astype(BF16)

        @pl.when(i == nsteps - 1)
        def _():
            acc = db_acc[...]
            for g in range(ng):
                gbt_ref[:, g:g + 1] = jnp.sum(acc[:, g * cw:(g + 1) * cw], axis=1, keepdims=True)

    return pl.pallas_call(
        body, name="branch_a_bwd",
        grid=(nsteps,),
        in_specs=[pl.BlockSpec((chunk, d3), lambda i: (i, 0)), pl.BlockSpec((chunk, d), lambda i: (i, 0)),
                  _const((1, d)), _const(wm.shape), _const(wm_t.shape), _const(b_t.shape)],
        out_specs=[pl.BlockSpec((chunk, d3), lambda i: (i, 0)), _const(wm.shape), _const(b_t.shape), _const((1, d))],
        out_shape=[jax.ShapeDtypeStruct((t, d3), BF16), jax.ShapeDtypeStruct(wm.shape, F32),
                   jax.ShapeDtypeStruct(b_t.shape, F32), jax.ShapeDtypeStruct((1, d), F32)],
        scratch_shapes=[pltpu.VMEM((chunk, d), F32)],
        compiler_params=_params(("arbitrary",)),
    )(a_pre, dya, g_v, wm, wm_t, b_t)


def _tri(n, rows_gt_cols):
    r = lax.broadcasted_iota(jnp.int32, (n, n), 0)
    c = lax.broadcasted_iota(jnp.int32, (n, n), 1)
    return (r > c) if rows_gt_cols else (r < c)


def _twice(tri):
    t = tri.astype(BF16)
    return jnp.concatenate([t, t], axis=0)


def _cumsum_mm(a, tri2):
    hi, lo = _split_bf16(a)
    return jnp.dot(jnp.concatenate([hi, lo], axis=1), tri2, preferred_element_type=F32)


LOG2E = 1.4426950408889634
_SIGN = 0x80000000


def _sb_block(q, k, scale, upper2, causal):
    z2 = lax.dot_general(q, k, NT, preferred_element_type=F32) * (scale * LOG2E)
    neg_abs = lax.bitcast_convert_type(lax.bitcast_convert_type(z2, jnp.uint32) | jnp.uint32(_SIGN), F32)
    l2 = jnp.log(1.0 + jnp.exp2(neg_abs)) * LOG2E
    log_beta = jnp.minimum(z2, 0.0) - l2
    lom = log_beta - z2
    if causal is not None:
        lom = jnp.where(causal, lom, 0.0)
    sx = _cumsum_mm(lom, upper2)
    return log_beta, sx, sx[:, 0:1] + lom[:, 0:1]


DEAD_LOG2 = -150.0


def _max_carry(carries):
    return jnp.max(functools.reduce(jnp.maximum, carries))


Q_GROUP, K_GROUP, V_GROUP, ZB_GROUP, GA_GROUP, GB_GROUP = 3, 4, 5, 6, 7, 8


def _attn_specs(d, seq, nq):
    hp_w = ATT_HP * (d // HEADS)
    n_hp = d // hp_w
    row_blk = lambda group: pl.BlockSpec((ATT_T, hp_w), lambda b, h, i: (b * nq + i, group * n_hp + h))
    seq_blk = lambda group: pl.BlockSpec((seq, hp_w), lambda b, h, i: (b, group * n_hp + h))
    return row_blk, seq_blk, n_hp


def _attn_fwd(qkv, proj, bsz, seq):
    t, d3 = qkv.shape
    d = d3 // 3
    hd = d // HEADS
    nq = seq // ATT_T
    scale = hd ** -0.5
    row_blk, seq_blk, n_hp = _attn_specs(d, seq, nq)

    def body(q_ref, k_ref, v_ref, zb_ref, o_ref, yb_ref):
        i = pl.program_id(2)
        causal = _tri(ATT_T, True)
        upper2 = _twice(causal)

        def step(kb, state, mask):
            rows = pl.ds(pl.multiple_of(kb * ATT_T, ATT_T), ATT_T)
            heads = [slice(h * hd, (h + 1) * hd) for h in range(ATT_HP)]
            scores = [_sb_block(q_ref[:, cols], k_ref[rows, cols], scale, upper2, mask) for cols in heads]
            new = []
            for cols, (carry, acc), (log_beta, sx, total) in zip(heads, state, scores):
                a = jnp.exp2(log_beta + sx + carry)
                if mask is not None:
                    a = jnp.where(mask, a, 0.0)
                acc = acc + jnp.dot(a.astype(BF16), v_ref[rows, cols], preferred_element_type=F32)
                new.append((carry + total, acc))
            return tuple(new)

        init = tuple((jnp.zeros((ATT_T, 1), F32), jnp.zeros((ATT_T, hd), F32)) for _ in range(ATT_HP))
        state = step(i, init, causal)
        def more(c):
            new = step(c[0], c[1], None)
            return c[0] - 1, new, _max_carry([s[0] for s in new])

        _, state, _ = lax.while_loop(lambda c: (c[0] >= 0) & (c[2] > DEAD_LOG2), more,
                                     (i - 1, state, _max_carry([s[0] for s in state])))
        for h in range(ATT_HP):
            cols = slice(h * hd, (h + 1) * hd)
            acc = state[h][1]
            zb = zb_ref[:, cols]
            o_ref[:, cols] = acc
            yb_ref[:, cols] = (acc * (zb * _sigmoid(zb))).astype(BF16)

    return pl.pallas_call(
        body, name="attn_fwd",
        grid=(bsz, n_hp, nq),
        in_specs=[row_blk(0), seq_blk(1), seq_blk(2), row_blk(ZB_GROUP)],
        out_specs=[row_blk(0), row_blk(0)],
        out_shape=[jax.ShapeDtypeStruct((t, d), F32), jax.ShapeDtypeStruct((t, d), BF16)],
        compiler_params=_params(("arbitrary", "arbitrary", "arbitrary")),
    )(qkv, qkv, qkv, proj)


def _attn_bwd(qkv, proj, o, dyb, bsz, seq):
    t, d3 = qkv.shape
    d = d3 // 3
    hd = d // HEADS
    nq = seq // ATT_T
    scale = hd ** -0.5
    row_blk, seq_blk, n_hp = _attn_specs(d, seq, nq)

    def body(q_ref, k_ref, v_ref, zb_ref, o_ref, dyb_ref, dq_ref, dk_ref, dv_ref, dzb_ref,
             g_s, beta_s, dkt_acc, dvt_acc):
        i = pl.program_id(2)

        @pl.when(i == 0)
        def _():
            dkt_acc[...] = jnp.zeros_like(dkt_acc)
            dvt_acc[...] = jnp.zeros_like(dvt_acc)

        causal = _tri(ATT_T, True)
        upper2 = _twice(causal)
        lower2 = _twice(~causal)
        zb = zb_ref[...]
        sig = _sigmoid(zb)
        dyb_t = dyb_ref[...]
        do_f = dyb_t * (zb * sig)
        do = do_f.astype(BF16)
        do_t = do_f.T.astype(BF16)
        q_t = q_ref[...].astype(F32).T.astype(BF16)
        dzb_ref[...] = (dyb_t * o_ref[...] * (sig * (1.0 + zb * (1.0 - sig)))).astype(BF16)

        def sweep(kb, carries, mask):
            rows = pl.ds(pl.multiple_of(kb * ATT_T, ATT_T), ATT_T)
            heads = [slice(h * hd, (h + 1) * hd) for h in range(ATT_HP)]
            scores = [_sb_block(q_ref[:, cols], k_ref[rows, cols], scale, upper2, mask) for cols in heads]
            das = [lax.dot_general(do[:, cols], v_ref[rows, cols], NT, preferred_element_type=F32) for cols in heads]
            new = []
            for h, (cols, carry, (log_beta, sx, total), da) in enumerate(zip(heads, carries, scores, das)):
                a = jnp.exp2(log_beta + sx + carry)
                beta = jnp.exp2(log_beta)
                if mask is not None:
                    a = jnp.where(mask, a, 0.0)
                    beta = jnp.where(mask, beta, 0.0)
                g_s[h, kb] = a * da
                beta_s[h, kb] = beta
                dvt_acc[kb, cols, :] += jnp.dot(do_t[cols, :], a.astype(BF16), preferred_element_type=F32)
                new.append(carry + total)
            return tuple(new)

        carries = sweep(i, tuple(jnp.zeros((ATT_T, 1), F32) for _ in range(ATT_HP)), causal)

        def more(c):
            new = sweep(c[0], c[1], None)
            return c[0] - 1, new, _max_carry(new)

        last, _, _ = lax.while_loop(lambda c: (c[0] >= 0) & (c[2] > DEAD_LOG2), more, (i - 1, carries, _max_carry(carries)))
        first_kb = last + 1

        def back(kb, state):
            rows = pl.ds(pl.multiple_of(kb * ATT_T, ATT_T), ATT_T)
            heads = [slice(h * hd, (h + 1) * hd) for h in range(ATT_HP)]
            sums = [_cumsum_mm(g_s[h, kb], lower2) for h in range(ATT_HP)]
            new = []
            for h, (cols, (p_carry, dq), px) in enumerate(zip(heads, state, sums)):
                dz = ((g_s[h, kb] - (p_carry + px) * beta_s[h, kb]) * scale).astype(BF16)
                dq = dq + jnp.dot(dz, k_ref[rows, cols], preferred_element_type=F32)
                dkt_acc[kb, cols, :] += jnp.dot(q_t[cols, :], dz, preferred_element_type=F32)
                new.append((p_carry + px[:, ATT_T - 1:ATT_T], dq))
            return tuple(new)

        init = tuple((jnp.zeros((ATT_T, 1), F32), jnp.zeros((ATT_T, hd), F32)) for _ in range(ATT_HP))
        state = lax.fori_loop(first_kb, i + 1, back, init)
        for h in range(ATT_HP):
            dq_ref[:, h * hd:(h + 1) * hd] = state[h][1].astype(BF16)

        @pl.when(i == nq - 1)
        def _():
            for kb in range(nq):
                dk_ref[kb * ATT_T:(kb + 1) * ATT_T, :] = dkt_acc[kb].T.astype(BF16)
                dv_ref[kb * ATT_T:(kb + 1) * ATT_T, :] = dvt_acc[kb].T.astype(BF16)

    out = jax.ShapeDtypeStruct((t, d), BF16)
    hp_w = ATT_HP * hd
    return pl.pallas_call(
        body, name="attn_bwd",
        grid=(bsz, n_hp, nq),
        in_specs=[row_blk(0), seq_blk(1), seq_blk(2), row_blk(ZB_GROUP), row_blk(0), row_blk(0)],
        out_specs=[row_blk(0), seq_blk(0), seq_blk(0), row_blk(0)],
        out_shape=[out, out, out, out],
        scratch_shapes=[pltpu.VMEM((ATT_HP, nq, ATT_T, ATT_T), F32), pltpu.VMEM((ATT_HP, nq, ATT_T, ATT_T), F32),
                        pltpu.VMEM((nq, hp_w, ATT_T), F32), pltpu.VMEM((nq, hp_w, ATT_T), F32)],
        compiler_params=_params(("arbitrary", "arbitrary", "arbitrary")),
    )(qkv, qkv, qkv, proj, o, dyb)


def _out_proj(ya, yb, g_pre, x2d, tgt, w_og, w_osb, w_out, g_f):
    t, d = x2d.shape

    def body(ya_ref, yb_ref, ga_ref, gb_ref, x_ref, tgt_ref, wog_ref, wosb_ref, wout_ref, gf_ref,
             dya_ref, dyb_ref, dg_ref, dx2_ref, loss_ref, gnf_ref, gwog_ref, gwosb_ref, gwout_ref):
        @pl.when(pl.program_id(0) == 0)
        def _():
            loss_ref[...] = jnp.zeros_like(loss_ref)
            gnf_ref[...] = jnp.zeros_like(gnf_ref)
            gwog_ref[...] = jnp.zeros_like(gwog_ref)
            gwosb_ref[...] = jnp.zeros_like(gwosb_ref)
            gwout_ref[...] = jnp.zeros_like(gwout_ref)

        ya = ya_ref[...]
        yb = yb_ref[...]
        pa = jnp.dot(ya, wog_ref[...], preferred_element_type=F32)
        pb = jnp.dot(yb, wosb_ref[...], preferred_element_type=F32)
        sga = _sigmoid(ga_ref[...])
        sgb = _sigmoid(gb_ref[...])
        merged = (sga * pa + sgb * pb).astype(BF16)
        x2 = x_ref[...] + jnp.dot(merged, wout_ref[...], preferred_element_type=F32)
        r2 = lax.rsqrt(jnp.mean(x2 * x2, axis=-1, keepdims=True) + EPS)
        n2 = x2 * r2
        gf = gf_ref[...]
        err = n2 * gf - tgt_ref[...]
        loss_ref[...] += 0.5 * jnp.sum(jnp.sum(err * err, axis=-1, keepdims=True), axis=0, keepdims=True) / d
        dy = err * (1.0 / d)
        gnf_ref[...] += jnp.sum(dy * n2, axis=0, keepdims=True)
        dn = dy * gf
        dx2 = r2 * (dn - n2 * jnp.mean(dn * n2, axis=-1, keepdims=True))
        dx2_ref[...] = dx2
        dx2_b = dx2.astype(BF16)
        dmerged = lax.dot_general(dx2_b, wout_ref[...], NT, preferred_element_type=F32)
        gwout_ref[...] += lax.dot_general(merged, dx2_b, TN, preferred_element_type=F32)
        dg_ref[:, 0:d] = (dmerged * pa * (sga * (1.0 - sga))).astype(BF16)
        dg_ref[:, d:2 * d] = (dmerged * pb * (sgb * (1.0 - sgb))).astype(BF16)
        dpa = (dmerged * sga).astype(BF16)
        dpb = (dmerged * sgb).astype(BF16)
        dya_ref[...] = lax.dot_general(dpa, wog_ref[...], NT, preferred_element_type=F32)
        dyb_ref[...] = lax.dot_general(dpb, wosb_ref[...], NT, preferred_element_type=F32)
        gwog_ref[...] += lax.dot_general(ya, dpa, TN, preferred_element_type=F32)
        gwosb_ref[...] += lax.dot_general(yb, dpb, TN, preferred_element_type=F32)

    row = lambda i: (i, 0)
    return pl.pallas_call(
        body, name="out_proj",
        grid=(t // TM,),
        in_specs=[pl.BlockSpec((TM, d), row), pl.BlockSpec((TM, d), row),
                  pl.BlockSpec((TM, d), lambda i: (i, GA_GROUP)), pl.BlockSpec((TM, d), lambda i: (i, GB_GROUP)),
                  pl.BlockSpec((TM, d), row), pl.BlockSpec((TM, d), row),
                  _resident((d, d)), _resident((d, d)), _resident((d, d)), _const((1, d))],
        out_specs=[pl.BlockSpec((TM, d), row), pl.BlockSpec((TM, d), row), pl.BlockSpec((TM, 2 * d), row),
                   pl.BlockSpec((TM, d), row), _const((1, 1)), _const((1, d)),
                   _const((d, d)), _const((d, d)), _const((d, d))],
        out_shape=[jax.ShapeDtypeStruct((t, d), F32), jax.ShapeDtypeStruct((t, d), F32),
                   jax.ShapeDtypeStruct((t, 2 * d), BF16), jax.ShapeDtypeStruct((t, d), F32),
                   jax.ShapeDtypeStruct((1, 1), F32), jax.ShapeDtypeStruct((1, d), F32),
                   jax.ShapeDtypeStruct((d, d), F32), jax.ShapeDtypeStruct((d, d), F32),
                   jax.ShapeDtypeStruct((d, d), F32)],
        compiler_params=_params(("arbitrary",)),
    )(ya, yb, g_pre, g_pre, x2d, tgt, w_og, w_osb, w_out, g_f)


def _dproj_pieces(d):
    return [(0, 0, 3), (1, 3, 1), (2, 4, 1), (3, 5, 1), (4, 6, 1), (5, 7, 2)]


def _in_proj_bwd_x(pieces, wg, x2d, g_in, dx2, gw16):
    t, d = x2d.shape
    ncol = wg.shape[2]
    segs = _segments(d, ncol)
    layout = _dproj_pieces(d)
    nsteps = t // TM

    def body(da_ref, dq_ref, dk_ref, dv_ref, dzb_ref, dg_ref, w_ref, x_ref, g_ref, dx2_ref, gw16_ref,
             gx_ref, gn_ref, recv_ref, send_sems, recv_sems):
        x_pos, y_pos, c_pos, chips = _place()

        def share(k, chunk):
            px, py = chips[k]
            return pltpu.make_async_remote_copy(
                src_ref=gw16_ref.at[:, chunk * ncol:(chunk + 1) * ncol], dst_ref=recv_ref.at[k],
                send_sem=send_sems.at[k], recv_sem=recv_sems.at[k], device_id=(px, py, c_pos), device_id_type=MESH)

        @pl.when(pl.program_id(0) == 0)
        def _():
            gn_ref[...] = jnp.zeros_like(gn_ref)
            for k, (px, py) in enumerate(chips):
                for chunk in range(N_CHIP):
                    @pl.when(2 * px + py == chunk)
                    def _(k=k, chunk=chunk):
                        share(k, chunk).start()

        @pl.when(pl.program_id(0) == nsteps - 1)
        def _():
            for k in range(N_CHIP - 1):
                share(k, 0).wait()

        refs = (da_ref, dq_ref, dk_ref, dv_ref, dzb_ref, dg_ref)
        dh = jnp.zeros((TM, d), F32)
        for chip, c0, grp, s0, width in segs:
            piece, first, _ = next(p for p in layout if p[1] <= grp < p[1] + p[2])
            off = (grp - first) * d + s0
            dh = dh + lax.dot_general(refs[piece][:, off:off + width], w_ref[chip, :, c0:c0 + width], NT,
                                      preferred_element_type=F32)
        x = x_ref[...]
        r = lax.rsqrt(jnp.mean(x * x, axis=-1, keepdims=True) + EPS)
        n = x * r
        gn_ref[...] += jnp.sum(dh * n, axis=0, keepdims=True)
        dn = dh * g_ref[...]
        gx_ref[...] = dx2_ref[...] + r * (dn - n * jnp.mean(dn * n, axis=-1, keepdims=True))

    row = lambda i: (i, 0)
    return pl.pallas_call(
        body, name="in_proj_bwd_x",
        grid=(t // TM,),
        in_specs=[pl.BlockSpec((TM, p.shape[1]), row) for p in pieces]
        + [_resident(wg.shape), pl.BlockSpec((TM, d), row), _const((1, d)), pl.BlockSpec((TM, d), row), ANY],
        out_specs=[pl.BlockSpec((TM, d), row), _const((1, d)), ANY],
        out_shape=[jax.ShapeDtypeStruct((t, d), F32), jax.ShapeDtypeStruct((1, d), F32),
                   jax.ShapeDtypeStruct((N_CHIP - 1, d, ncol), BF16)],
        scratch_shapes=[pltpu.SemaphoreType.DMA((N_CHIP - 1,)), pltpu.SemaphoreType.DMA((N_CHIP - 1,))],
        compiler_params=_params(("arbitrary",)),
    )(*pieces, wg, x2d, g_in, dx2, gw16)


def _in_proj_bwd_w(h_t, pieces, mats16, pack):
    d, t = h_t.shape
    nk = t // TKW
    layout = _dproj_pieces(d)
    n_mats = len(mats16)
    n_dev = 2 * N_CHIP
    flips = [(dx, dy, dc) for dx in (0, 1) for dy in (0, 1) for dc in (0, 1)][1:]

    def body(ht_ref, da_ref, dq_ref, dk_ref, dv_ref, dzb_ref, dg_ref, *rest):
        mat_refs, pack_ref = rest[:n_mats], rest[n_mats]
        gw_ref, gw16_ref = rest[n_mats + 1:n_mats + 3]
        recv_refs, slots_ref = rest[n_mats + 3:2 * n_mats + 3], rest[2 * n_mats + 3]
        acc, mat_send, mat_recv, pack_send, pack_recv, own_sem = rest[2 * n_mats + 4:]
        s = pl.program_id(0)
        i = pl.program_id(1)
        x_pos, y_pos, c_pos, chips = _place()
        me = 4 * x_pos + 2 * y_pos + c_pos

        def exchanges():
            cps = []
            for k, (px, py) in enumerate(chips):
                for a in range(n_mats):
                    cps.append(pltpu.make_async_remote_copy(
                        src_ref=mat_refs[a].at[2 * px + py], dst_ref=recv_refs[a].at[k],
                        send_sem=mat_send.at[a, k], recv_sem=mat_recv.at[a, k],
                        device_id=(px, py, c_pos), device_id_type=MESH))
            for k, (dx, dy, dc) in enumerate(flips):
                peer = (1 - x_pos if dx else x_pos, 1 - y_pos if dy else y_pos, 1 - c_pos if dc else c_pos)
                cps.append(pltpu.make_async_remote_copy(
                    src_ref=pack_ref, dst_ref=slots_ref.at[me], send_sem=pack_send.at[k], recv_sem=pack_recv.at[k],
                    device_id=peer, device_id_type=MESH))
            return cps, pltpu.make_async_copy(pack_ref, slots_ref.at[me], own_sem)

        @pl.when((s == 0) & (i == 0))
        def _():
            cps, own = exchanges()
            own.start()
            for cp in cps:
                cp.start()

        @pl.when((s == N_SPLIT - 1) & (i == nk - 1))
        def _():
            cps, own = exchanges()
            own.wait()
            for cp in cps:
                cp.wait()

        @pl.when(i == 0)
        def _():
            acc[...] = jnp.zeros_like(acc)

        refs = (da_ref, dq_ref, dk_ref, dv_ref, dzb_ref, dg_ref)
        for piece, first, count in layout:
            @pl.when((s >= first) & (s < first + count))
            def _(piece=piece):
                acc[...] += jnp.dot(ht_ref[...], refs[piece][...], preferred_element_type=F32)

        @pl.when(i == nk - 1)
        def _():
            gw_ref[...] = acc[...]
            gw16_ref[...] = acc[...].astype(BF16)

    def piece_spec(p, first, count):
        def index(s, i):
            mine = (s >= first) & (s < first + count)
            return jnp.where(mine, i, 0), jnp.where(mine, s - first, 0)
        return pl.BlockSpec((TKW, d), index)

    col_blk = pl.BlockSpec((d, d), lambda s, i: (0, s))
    outs = pl.pallas_call(
        body, name="in_proj_bwd_w",
        grid=(N_SPLIT, nk),
        in_specs=[pl.BlockSpec((d, TKW), lambda s, i: (0, i))] + [piece_spec(*p) for p in layout] + [ANY] * (n_mats + 1),
        out_specs=[col_blk, col_blk] + [ANY] * (n_mats + 1),
        out_shape=[jax.ShapeDtypeStruct((d, N_SPLIT * d), F32), jax.ShapeDtypeStruct((d, N_SPLIT * d), BF16)]
        + [jax.ShapeDtypeStruct((N_CHIP - 1,) + m.shape[1:], BF16) for m in mats16]
        + [jax.ShapeDtypeStruct((n_dev,) + pack.shape, F32)],
        scratch_shapes=[pltpu.VMEM((d, d), F32),
                        pltpu.SemaphoreType.DMA((n_mats, N_CHIP - 1)), pltpu.SemaphoreType.DMA((n_mats, N_CHIP - 1)),
                        pltpu.SemaphoreType.DMA((n_dev - 1,)), pltpu.SemaphoreType.DMA((n_dev - 1,)),
                        pltpu.SemaphoreType.DMA],
        compiler_params=_params(("arbitrary", "arbitrary")),
    )(h_t, *pieces, *mats16, pack)
    return outs[0], outs[1], outs[2:2 + n_mats], outs[2 + n_mats]


def _local_step(proj, qkv, x2d, tgt2d, bsz, seq, norm_v, w_s, b_s, w_og, w_osb, w_out, norm_final):
    d = x2d.shape[1]
    chunk = w_s.shape[-1]
    causal = jnp.tril(jnp.ones((chunk, chunk), dtype=bool))
    wm = jnp.where(causal[None], w_s, 0.0).astype(BF16)
    wm_t = jnp.swapaxes(wm, 1, 2)
    b_t = b_s.T

    ya = _branch_a_fwd(proj, norm_v, wm, b_t)
    o, yb = _attn_fwd(qkv, proj, bsz, seq)
    dya, dyb, dg, dx2, loss, g_nf, g_wog, g_wosb, g_wout = _out_proj(
        ya, yb, proj, x2d, tgt2d, w_og, w_osb, w_out, norm_final.reshape(1, d))
    dq, dk, dv, dzb = _attn_bwd(qkv, proj, o, dyb, bsz, seq)
    d_a, g_ws, g_bt, g_nv = _branch_a_bwd(proj, dya, norm_v, wm, wm_t, b_t)
    g_ws = jnp.where(causal[None], g_ws, 0.0)
    return loss, (d_a, dq, dk, dv, dzb, dg), dx2, g_nv, g_ws, g_bt.T, g_wog, g_wosb, g_wout, g_nf


def _row_tile(rows):
    return next(r for r in (128, 64, 32, 16, 8) if rows % r == 0)


def _cast_bf16(arrs):
    n = len(arrs)

    def body(*refs):
        for a_ref, o_ref in zip(refs[:n], refs[n:]):
            o_ref[...] = a_ref[...].astype(BF16)

    specs = [pl.BlockSpec((a.shape[0] // CAST_STEPS, a.shape[1]), lambda i: (i, 0)) for a in arrs]
    return pl.pallas_call(
        body, name="cast_bf16", grid=(CAST_STEPS,),
        in_specs=specs, out_specs=specs,
        out_shape=[jax.ShapeDtypeStruct(a.shape, BF16) for a in arrs],
        compiler_params=_params(("arbitrary",)),
    )(*arrs)


def _add_received(full, recv, chip, by_cols):
    _, rows, cols = recv.shape
    tr = _row_tile(rows)
    nb = rows // tr

    def body(chip_ref, own_ref, recv_ref, o_ref):
        s = own_ref[...]
        for k in range(N_CHIP - 1):
            s = s + recv_ref[k].astype(F32)
        o_ref[...] = s

    own_map = (lambda i, chip_ref: (i, chip_ref[0])) if by_cols else (lambda i, chip_ref: (chip_ref[0] * nb + i, 0))
    return pl.pallas_call(
        body, name="add_received",
        grid_spec=pltpu.PrefetchScalarGridSpec(
            num_scalar_prefetch=1, grid=(nb,),
            in_specs=[pl.BlockSpec((tr, cols), own_map),
                      pl.BlockSpec((N_CHIP - 1, tr, cols), lambda i, chip_ref: (0, i, 0))],
            out_specs=pl.BlockSpec((tr, cols), lambda i, chip_ref: (i, 0))),
        out_shape=jax.ShapeDtypeStruct((rows, cols), F32),
        compiler_params=_params(("arbitrary",)),
    )(chip.reshape(1).astype(jnp.int32), full, recv)


def _adamw_math(w, m, v, g):
    new_m = ADAM_B1 * m + (1.0 - ADAM_B1) * g
    new_v = ADAM_B2 * v + (1.0 - ADAM_B2) * (g * g)
    m_hat = new_m / (1.0 - ADAM_B1 ** ADAM_STEP)
    v_hat = new_v / (1.0 - ADAM_B2 ** ADAM_STEP)
    return -ADAM_LR * (m_hat / (jnp.sqrt(v_hat) + ADAM_EPS) + ADAM_WD * w), new_m, new_v


def _adamw(w, m, v, g_parts):
    rows, cols = w.shape
    tr = _row_tile(rows)
    n_parts = len(g_parts)

    def body(*refs):
        w_ref, m_ref, v_ref = refs[:3]
        part_refs = refs[3:3 + n_parts]
        g_ref, d_ref, nm_ref, nv_ref = refs[3 + n_parts:]
        g = part_refs[0][...]
        for p in part_refs[1:]:
            g = g + p[...]
        g_ref[...] = g
        d_ref[...], nm_ref[...], nv_ref[...] = _adamw_math(w_ref[...], m_ref[...], v_ref[...], g)

    spec = pl.BlockSpec((tr, cols), lambda i: (i, 0))
    out = jax.ShapeDtypeStruct(w.shape, F32)
    return pl.pallas_call(
        body, name="adamw", grid=(rows // tr,),
        in_specs=[spec] * (3 + n_parts), out_specs=[spec] * 4, out_shape=[out] * 4,
        compiler_params=_params(("arbitrary",)),
    )(w, m, v, *g_parts)


def _adamw_small(w, m, v, slots_head, slots_tail):
    n_dev, p0, _ = slots_head.shape

    def body(w_ref, m_ref, v_ref, head_ref, tail_ref, g_ref, d_ref, nm_ref, nv_ref):
        for ref, rows in ((head_ref, slice(0, p0)), (tail_ref, slice(p0, w.shape[0]))):
            g = ref[0]
            for i in range(1, n_dev):
                g = g + ref[i]
            g_ref[rows, :] = g
            d_ref[rows, :], nm_ref[rows, :], nv_ref[rows, :] = _adamw_math(w_ref[rows, :], m_ref[rows, :], v_ref[rows, :], g)

    vmem = pl.BlockSpec(memory_space=pltpu.VMEM)
    out = jax.ShapeDtypeStruct(w.shape, F32)
    return pl.pallas_call(
        body, name="adamw_small", in_specs=[vmem] * 5, out_specs=[vmem] * 4, out_shape=[out] * 4,
        compiler_params=pltpu.CompilerParams(vmem_limit_bytes=VMEM_LIMIT),
    )(w, m, v, slots_head, slots_tail)


ANY = pl.BlockSpec(memory_space=pl.ANY)


def _place():
    x, y, c = lax.axis_index("x"), lax.axis_index("y"), lax.axis_index("c")
    other_chips = [(1 - x, y), (x, 1 - y), (1 - x, 1 - y)]
    return x, y, c, other_chips


def _swap_and_gather(arrs, pack):
    n = len(arrs)
    n_dev = 2 * N_CHIP
    flips = [(dx, dy, dc) for dx in (0, 1) for dy in (0, 1) for dc in (0, 1)][1:]

    def body(*refs):
        ins, pack_ref = refs[:n], refs[n]
        outs, slots_ref = refs[n + 1:2 * n + 1], refs[2 * n + 1]
        send_sems, recv_sems, pack_send, pack_recv, own_sem = refs[2 * n + 2:]
        x, y, c, _ = _place()
        me = 4 * x + 2 * y + c
        own = pltpu.make_async_copy(pack_ref, slots_ref.at[me], own_sem)
        own.start()
        copies = []
        for k, (dx, dy, dc) in enumerate(flips):
            peer = (1 - x if dx else x, 1 - y if dy else y, 1 - c if dc else c)
            copies.append(pltpu.make_async_remote_copy(
                src_ref=pack_ref, dst_ref=slots_ref.at[me], send_sem=pack_send.at[k], recv_sem=pack_recv.at[k],
                device_id=peer, device_id_type=MESH))
        copies += [pltpu.make_async_remote_copy(
            src_ref=ins[a], dst_ref=outs[a], send_sem=send_sems.at[a], recv_sem=recv_sems.at[a],
            device_id=(x, y, 1 - c), device_id_type=MESH) for a in range(n)]
        for cp in copies:
            cp.start()
        for cp in copies:
            cp.wait()
        own.wait()

    return pl.pallas_call(
        body, name="swap_and_gather",
        in_specs=[ANY] * (n + 1), out_specs=[ANY] * (n + 1),
        out_shape=[jax.ShapeDtypeStruct(a.shape, a.dtype) for a in arrs] + [jax.ShapeDtypeStruct((n_dev,) + pack.shape, F32)],
        scratch_shapes=[pltpu.SemaphoreType.DMA((n,)), pltpu.SemaphoreType.DMA((n,)),
                        pltpu.SemaphoreType.DMA((n_dev - 1,)), pltpu.SemaphoreType.DMA((n_dev - 1,)),
                        pltpu.SemaphoreType.DMA],
    )(*arrs, pack)


SLAB = 8


def _slab(vec, d):
    return jnp.pad(vec.reshape(1, d), ((0, SLAB - 1), (0, 0)))


def _pack_tail(vec_nv, b_s, vec_nf, w_s, scalar=None):
    d = vec_nv.shape[-1]
    extra = jnp.zeros((1, d), F32) if scalar is None else jnp.pad(scalar, ((0, 0), (0, d - 1)))
    return jnp.concatenate([_slab(v, d) for v in (vec_nv, b_s, vec_nf, extra)] + [w_s.reshape(-1, d)], axis=0)


def _pack_small(vec_nin, vec_nv, b_s, vec_nf, w_s):
    return jnp.concatenate([_slab(vec_nin, vec_nin.shape[-1]), _pack_tail(vec_nv, b_s, vec_nf, w_s)], axis=0)


def _unpack_small(pack, w_s_shape, b_s_shape):
    return (pack[0:1], pack[SLAB:SLAB + 1], pack[2 * SLAB].reshape(b_s_shape), pack[3 * SLAB],
            pack[5 * SLAB:].reshape(w_s_shape), pack[4 * SLAB, 0])


def kernel(x, norm_in, w_in, norm_v, w_s, b_s, w_o_gmlp, w_o_sb, w_out, norm_final, loss_target, m_norm_in, m_w_in, m_norm_v, m_w_s, m_b_s, m_w_o_gmlp, m_w_o_sb, m_w_out, m_norm_final, v_norm_in, v_w_in, v_norm_v, v_w_s, v_b_s, v_w_o_gmlp, v_w_o_sb, v_w_out, v_norm_final):
    d = x.shape[-1]
    ncol = w_in.shape[-1]
    nrow = w_o_gmlp.shape[-2]
    chip = 2 * lax.axis_index("x") + lax.axis_index("y")

    bsz, seq, _ = x.shape
    x2d = x.reshape(bsz * seq, d)
    shards = [w_in[0], w_o_gmlp[0], w_o_sb[0], w_out[0]]
    halves = [s16.reshape(2, s16.shape[0] // 2, s16.shape[1]) for s16 in _cast_bf16(shards)]
    proj, qkv, h_t, (wg, w_og, w_osb, w_o) = _in_proj_fwd(x2d, norm_in, halves)
    wg = wg.reshape(N_CHIP, d, ncol)

    loss, pieces, dx2, g_nv, g_ws, g_bs, g_wog, g_wosb, g_wout, g_nf = _local_step(
        proj, qkv, x2d, loss_target.reshape(bsz * seq, d), bsz, seq, norm_v, w_s[0], b_s[0],
        w_og.reshape(d, d), w_osb.reshape(d, d), w_o.reshape(d, d), norm_final)

    mats = [g_wog, g_wosb, g_wout]
    mats16 = [g16.reshape(N_CHIP, nrow, d) for g16 in _cast_bf16(mats)]
    g_win, g_win16, recv_mats, slots_tail = _in_proj_bwd_w(h_t, pieces, mats16, _pack_tail(g_nv, g_bs, g_nf, g_ws, loss))
    grad_x, g_nin, recv_win = _in_proj_bwd_x(pieces, wg, x2d, norm_in, dx2, g_win16)
    grad_x = grad_x.reshape(bsz, seq, d)

    sums = [_add_received(g_win, recv_win, chip, True)] + [
        _add_received(g, r, chip, False) for g, r in zip(mats, recv_mats)]
    *sibling_sums, slots_head = _swap_and_gather(sums, _slab(g_nin, d))
    stats = []
    for w, m, v, mine, theirs in zip(shards, [m_w_in[0], m_w_o_gmlp[0], m_w_o_sb[0], m_w_out[0]],
                                     [v_w_in[0], v_w_o_gmlp[0], v_w_o_sb[0], v_w_out[0]], sums, sibling_sums):
        stats.append(_adamw(w, m, v, [mine, theirs]))
    (gw_in, dw_in, nm_in, nv_in), (gw_og, dw_og, nm_og, nv_og), (gw_osb, dw_osb, nm_osb, nv_osb), \
        (gw_out, dw_out, nm_out, nv_out) = stats

    gs, ds, ms, vs = _adamw_small(
        _pack_small(norm_in, norm_v, b_s[0], norm_final, w_s[0]),
        _pack_small(m_norm_in, m_norm_v, m_b_s[0], m_norm_final, m_w_s[0]),
        _pack_small(v_norm_in, v_norm_v, v_b_s[0], v_norm_final, v_w_s[0]), slots_head, slots_tail)

    def small(pack):
        nin, nv, bs, nf, ws, _ = _unpack_small(pack, w_s.shape, b_s.shape)
        return nin, nv, ws, bs, nf

    loss = _unpack_small(gs, w_s.shape, b_s.shape)[-1]
    out = []
    for small_pack, win, wog, wosb, wout in ((gs, gw_in, gw_og, gw_osb, gw_out), (ds, dw_in, dw_og, dw_osb, dw_out),
                                             (ms, nm_in, nm_og, nm_osb, nm_out), (vs, nv_in, nv_og, nv_osb, nv_out)):
        nin, nv, ws, bs, nf = small(small_pack)
        out += [nin, win[None], nv, ws, bs, wog[None], wosb[None], wout[None], nf]
    return (loss, grad_x, *out)
```

```python
import functools
import math

import jax
import jax.numpy as jnp
from jax import lax
from jax.experimental import pallas as pl
from jax.experimental.pallas import tpu as pltpu

F32 = jnp.float32
BF16 = jnp.bfloat16
EPS = 1e-6
HEADS = 8
N_SPLIT = 9
N_CHIP = 4
MESH = pl.DeviceIdType.MESH

ADAM_LR = 0.001
ADAM_B1 = 0.9
ADAM_B2 = 0.999
ADAM_EPS = 1e-08
ADAM_WD = 0.01
ADAM_STEP = 10

VMEM_LIMIT = 56 * 2 ** 20
TM = 256
TMF = 512
ATT_T = 256
ATT_HP = 4
TKW = 1024
CHUNKS_PER_STEP = 4
CAST_STEPS = 8

NT = (((1,), (1,)), ((), ()))
TN = (((0,), (0,)), ((), ()))


def _params(sem):
    return pltpu.CompilerParams(dimension_semantics=sem, vmem_limit_bytes=VMEM_LIMIT)


def _resident(shape):
    nd = len(shape)
    return pl.BlockSpec(shape, lambda *_: (0,) * nd, pipeline_mode=pl.Buffered(1))


def _const(shape):
    nd = len(shape)
    return pl.BlockSpec(shape, lambda *_: (0,) * nd)


def _segments(d, ncol):
    segs = []
    edges = sorted({j * ncol for j in range(N_CHIP + 1)} | {s * d for s in range(N_SPLIT + 1)})
    for lo, hi in zip(edges[:-1], edges[1:]):
        segs.append((lo // ncol, lo % ncol, lo // d, lo % d, hi - lo))
    return segs


def _sigmoid(x):
    return 0.5 * jnp.tanh(0.5 * x) + 0.5


_GELU_C = math.sqrt(2.0 / math.pi)


_GELU_CA = _GELU_C * 0.044715


def _gelu(x):
    return x * (0.5 * jnp.tanh(x * (_GELU_C + _GELU_CA * (x * x))) + 0.5)


def _gelu_and_grad(x):
    x2 = x * x
    u = 0.5 * jnp.tanh(x * (_GELU_C + _GELU_CA * x2)) + 0.5
    slope = (1.0 - u) * (x * (_GELU_C + (3.0 * _GELU_CA) * x2))
    return x * u, u * (2.0 * slope + 1.0)


def _split_bf16(a):
    hi = a.astype(BF16)
    lo = (a - hi.astype(F32)).astype(BF16)
    return hi, lo


def _in_proj_fwd(x2d, g_in, halves):
    t, d = x2d.shape
    n = len(halves)
    ncol = halves[0].shape[2]
    n_row = t // TMF
    last = n_row - 1
    assert halves[0].shape[1] * 2 == d
    qkv_parts = {j: (max(j * ncol, 3 * d) - j * ncol, max(j * ncol, 3 * d) - 3 * d)
                 for j in range(N_CHIP) if min((j + 1) * ncol, 6 * d) > max(j * ncol, 3 * d)}
    qkv_w = 3 * d // len(qkv_parts)
    assert all(min((j + 1) * ncol, 6 * d) - max(j * ncol, 3 * d) == qkv_w and q0 % qkv_w == 0
               for j, (_, q0) in qkv_parts.items())

    def body(order_ref, qrow_ref, qcol_ref, x_ref, g_ref, *rest):
        ins = rest[:n]
        proj_ref, qkv_ref, ht_ref = rest[n:n + 3]
        outs = rest[n + 3:2 * n + 3]
        wbuf, h_all, send_sems, recv_sems, local_sems, load_sem = rest[2 * n + 3:]
        phase = pl.program_id(0)
        i = pl.program_id(1)
        x_pos, y_pos, c_pos, chips = _place()
        sibling = (x_pos, y_pos, 1 - c_pos)
        me = (x_pos, y_pos, c_pos)
        my_chip = 2 * x_pos + y_pos

        def copy(a, k, block, to, src=None):
            return pltpu.make_async_remote_copy(
                src_ref=outs[a].at[block] if src is None else src, dst_ref=outs[a].at[block],
                send_sem=send_sems.at[a, k], recv_sem=recv_sems.at[a, k], device_id=to, device_id_type=MESH)

        def local(a):
            return pltpu.make_async_copy(ins[a], outs[a].at[pl.ds(2 * my_chip, 2)], local_sems.at[a])

        def load(src, first, slot):
            for half in range(2):
                cp = pltpu.make_async_copy(src.at[first + half], wbuf.at[slot, pl.ds(half * (d // 2), d // 2)], load_sem)
                cp.start()
                cp.wait()

        def send_mine(k):
            px, py = chips[k]
            for a in range(n):
                copy(a, k, 2 * my_chip + c_pos, (px, py, c_pos), src=ins[a].at[c_pos]).start()

        @pl.when((phase == 0) & (i == 0))
        def _():
            for a in range(n):
                local(a).start()
            send_mine(0)
            send_mine(1)
            load(ins[0], 0, 0)

        for k, (px, py) in enumerate(chips):
            @pl.when((phase == k + 1) & (i == 0))
            def _(k=k, px=px, py=py):
                theirs = 2 * (2 * px + py)
                for a in range(n):
                    copy(a, k, theirs + c_pos, me).wait_recv()
                    copy(a, 3 + k, theirs + c_pos, sibling).start()
                if k == 0:
                    send_mine(2)
                for a in range(n):
                    copy(a, 3 + k, theirs + 1 - c_pos, me).wait_recv()
                load(outs[0], theirs, (k + 1) % 2)

        @pl.when(phase == 0)
        def _():
            x = x_ref[...]
            r = lax.rsqrt(jnp.mean(x * x, axis=-1, keepdims=True) + EPS)
            hf = x * r * g_ref[...]
            h_all[i] = hf.astype(BF16)
            ht_ref[...] = hf.T.astype(BF16)

        for slot in range(2):
            @pl.when(phase % 2 == slot)
            def _(slot=slot):
                proj_ref[...] = jnp.dot(h_all[i], wbuf[slot], preferred_element_type=F32)

        for chunk, (c0, _) in qkv_parts.items():
            @pl.when(order_ref[phase] == chunk)
            def _(c0=c0):
                qkv_ref[...] = proj_ref[:, c0:c0 + qkv_w].astype(BF16)

        @pl.when((phase == N_CHIP - 1) & (i == n_row - 1))
        def _():
            for a in range(n):
                for k in range(2 * (N_CHIP - 1)):
                    copy(a, k, 0, me).wait_send()
                local(a).wait()

    x_pos, y_pos = lax.axis_index("x"), lax.axis_index("y")
    order = jnp.stack([2 * x_pos + y_pos, 2 * (1 - x_pos) + y_pos, 2 * x_pos + 1 - y_pos,
                       2 * (1 - x_pos) + 1 - y_pos]).astype(jnp.int32)
    holds = [functools.reduce(jnp.logical_or, [order[p] == j for j in qkv_parts]) for p in range(N_CHIP)]
    col = [sum(jnp.where(order[p] == j, q0 // qkv_w, 0) for j, (_, q0) in qkv_parts.items()) for p in range(N_CHIP)]
    cur = col[-1]
    for p in reversed(range(N_CHIP - 1)):
        cur = jnp.where(holds[p], col[p], cur)
    seen = jnp.bool_(False)
    qrow, qcol = [], []
    for p in range(N_CHIP):
        cur = jnp.where(holds[p], col[p], cur)
        qrow.append(jnp.where(holds[p], -1, jnp.where(seen, last, 0)))
        qcol.append(cur)
        seen = seen | holds[p]
    qrow = jnp.stack(qrow).astype(jnp.int32)
    qcol = jnp.stack(qcol).astype(jnp.int32)

    outs = pl.pallas_call(
        body, name="in_proj_fwd",
        grid_spec=pltpu.PrefetchScalarGridSpec(
            num_scalar_prefetch=3, grid=(N_CHIP, n_row),
            in_specs=[pl.BlockSpec((TMF, d), lambda p, i, order, qrow, qcol: (jnp.where(p == 0, i, last), 0)),
                      pl.BlockSpec((1, d), lambda p, i, order, qrow, qcol: (0, 0))] + [ANY] * n,
            out_specs=[pl.BlockSpec((TMF, ncol), lambda p, i, order, qrow, qcol: (i, order[p])),
                       pl.BlockSpec((TMF, qkv_w),
                                    lambda p, i, order, qrow, qcol: (jnp.where(qrow[p] < 0, i, qrow[p]), qcol[p])),
                       pl.BlockSpec((d, TMF), lambda p, i, order, qrow, qcol: (0, jnp.where(p == 0, i, last)))]
            + [ANY] * n,
            scratch_shapes=[pltpu.VMEM((2, d, ncol), BF16), pltpu.VMEM((n_row, TMF, d), BF16),
                            pltpu.SemaphoreType.DMA((n, 2 * (N_CHIP - 1))), pltpu.SemaphoreType.DMA((n, 2 * (N_CHIP - 1))),
                            pltpu.SemaphoreType.DMA((n,)), pltpu.SemaphoreType.DMA]),
        out_shape=[jax.ShapeDtypeStruct((t, N_CHIP * ncol), F32), jax.ShapeDtypeStruct((t, 3 * d), BF16),
                   jax.ShapeDtypeStruct((d, t), BF16)]
        + [jax.ShapeDtypeStruct((2 * N_CHIP,) + hv.shape[1:], BF16) for hv in halves],
        compiler_params=_params(("arbitrary", "arbitrary")),
    )(order, qrow, qcol, x2d, g_in, *halves)
    return outs[0], outs[1], outs[2], outs[3:]


def _branch_a_fwd(a_pre, g_v, wm, b_t):
    t = a_pre.shape[0]
    d = g_v.shape[1]
    d3 = 3 * d
    ng, chunk, _ = wm.shape
    cw = d // ng

    per_step = CHUNKS_PER_STEP if t % (CHUNKS_PER_STEP * chunk) == 0 else 1

    def body(a_ref, gv_ref, wm_ref, bt_ref, ya_ref):
        for n in range(per_step):
            rows = slice(n * chunk, (n + 1) * chunk)
            ua = _gelu(a_ref[rows, 0:d])
            vg = _gelu(a_ref[rows, d:2 * d])
            za = a_ref[rows, 2 * d:3 * d]
            rv = lax.rsqrt(jnp.mean(vg * vg, axis=-1, keepdims=True) + EPS)
            va = (vg * rv * gv_ref[...]).astype(BF16)
            gate = ua * (za * _sigmoid(za))
            for g in range(ng):
                sl = slice(g * cw, (g + 1) * cw)
                mixed = jnp.dot(wm_ref[g], va[:, sl], preferred_element_type=F32) + bt_ref[:, g:g + 1]
                ya_ref[rows, sl] = (gate[:, sl] * mixed).astype(BF16)

    tile = per_step * chunk
    return pl.pallas_call(
        body, name="branch_a_fwd",
        grid=(t // tile,),
        in_specs=[pl.BlockSpec((tile, d3), lambda i: (i, 0)), _const((1, d)), _const(wm.shape), _const(b_t.shape)],
        out_specs=pl.BlockSpec((tile, d), lambda i: (i, 0)),
        out_shape=jax.ShapeDtypeStruct((t, d), BF16),
        compiler_params=_params(("arbitrary",)),
    )(a_pre, g_v, wm, b_t)


def _branch_a_bwd(a_pre, dya, g_v, wm, wm_t, b_t):
    t = a_pre.shape[0]
    d = g_v.shape[1]
    d3 = 3 * d
    ng, chunk, _ = wm.shape
    cw = d // ng
    nsteps = t // chunk

    def body(a_ref, dya_ref, gv_ref, wm_ref, wmt_ref, bt_ref, da_ref, gws_ref, gbt_ref, gnv_ref, db_acc):
        i = pl.program_id(0)

        @pl.when(i == 0)
        def _():
            gws_ref[...] = jnp.zeros_like(gws_ref)
            gnv_ref[...] = jnp.zeros_like(gnv_ref)
            db_acc[...] = jnp.zeros_like(db_acc)

        ua, dgelu_u = _gelu_and_grad(a_ref[:, 0:d])
        vg, dgelu_v = _gelu_and_grad(a_ref[:, d:2 * d])
        za = a_ref[:, 2 * d:3 * d]
        sig = _sigmoid(za)
        sz = za * sig
        dsz = sig * (1.0 + za * (1.0 - sig))
        rv = lax.rsqrt(jnp.mean(vg * vg, axis=-1, keepdims=True) + EPS)
        nv = vg * rv
        gv = gv_ref[...]
        va = (nv * gv).astype(BF16)
        dya = dya_ref[...]
        dmix = dya * ua * sz
        db_acc[...] += dmix
        dmix_b = dmix.astype(BF16)
        t_gate = dya * sz
        t_z = dya * ua * dsz
        dva_parts = []
        for g in range(ng):
            sl = slice(g * cw, (g + 1) * cw)
            mixed = jnp.dot(wm_ref[g], va[:, sl], preferred_element_type=F32) + bt_ref[:, g:g + 1]
            da_ref[:, sl] = (t_gate[:, sl] * mixed * dgelu_u[:, sl]).astype(BF16)
            da_ref[:, 2 * d + g * cw:2 * d + (g + 1) * cw] = (t_z[:, sl] * mixed).astype(BF16)
            gws_ref[g] += lax.dot_general(dmix_b[:, sl], va[:, sl], NT, preferred_element_type=F32)
            dva_parts.append(jnp.dot(wmt_ref[g], dmix_b[:, sl], preferred_element_type=F32))
        dva = jnp.concatenate(dva_parts, axis=1)
        gnv_ref[...] += jnp.sum(dva * nv, axis=0, keepdims=True)
        dnv = dva * gv
        dvg = rv * (dnv - nv * jnp.mean(dnv * nv, axis=-1, keepdims=True))
        da_ref[:, d:2 * d] = (dvg * dgelu_v).astype(BF16)

        @pl.when(i == nsteps - 1)
        def _():
            acc = db_acc[...]
            for g in range(ng):
                gbt_ref[:, g:g + 1] = jnp.sum(acc[:, g * cw:(g + 1) * cw], axis=1, keepdims=True)

    return pl.pallas_call(
        body, name="branch_a_bwd",
        grid=(nsteps,),
        in_specs=[pl.BlockSpec((chunk, d3), lambda i: (i, 0)), pl.BlockSpec((chunk, d), lambda i: (i, 0)),
                  _const((1, d)), _const(wm.shape), _const(wm_t.shape), _const(b_t.shape)],
        out_specs=[pl.BlockSpec((chunk, d3), lambda i: (i, 0)), _const(wm.shape), _const(b_t.shape), _const((1, d))],
        out_shape=[jax.ShapeDtypeStruct((t, d3), BF16), jax.ShapeDtypeStruct(wm.shape, F32),
                   jax.ShapeDtypeStruct(b_t.shape, F32), jax.ShapeDtypeStruct((1, d), F32)],
        scratch_shapes=[pltpu.VMEM((chunk, d), F32)],
        compiler_params=_params(("arbitrary",)),
    )(a_pre, dya, g_v, wm, wm_t, b_t)


def _tri(n, rows_gt_cols):
    r = lax.broadcasted_iota(jnp.int32, (n, n), 0)
    c = lax.broadcasted_iota(jnp.int32, (n, n), 1)
    return (r > c) if rows_gt_cols else (r < c)


def _twice(tri):
    t = tri.astype(BF16)
    return jnp.concatenate([t, t], axis=0)


def _cumsum_mm(a, tri2):
    hi, lo = _split_bf16(a)
    return jnp.dot(jnp.concatenate([hi, lo], axis=1), tri2, preferred_element_type=F32)


LOG2E = 1.4426950408889634
_SIGN = 0x80000000


def _sb_block(q, k, scale, upper2, causal):
    z2 = lax.dot_general(q, k, NT, preferred_element_type=F32) * (scale * LOG2E)
    neg_abs = lax.bitcast_convert_type(lax.bitcast_convert_type(z2, jnp.uint32) | jnp.uint32(_SIGN), F32)
    l2 = jnp.log(1.0 + jnp.exp2(neg_abs)) * LOG2E
    log_beta = jnp.minimum(z2, 0.0) - l2
    lom = log_beta - z2
    if causal is not None:
        lom = jnp.where(causal, lom, 0.0)
    sx = _cumsum_mm(lom, upper2)
    return log_beta, sx, sx[:, 0:1] + lom[:, 0:1]


DEAD_LOG2 = -150.0


def _max_carry(carries):
    return jnp.max(functools.reduce(jnp.maximum, carries))


ZB_GROUP, GA_GROUP, GB_GROUP = 6, 7, 8


def _attn_specs(d, seq, nq):
    hp_w = ATT_HP * (d // HEADS)
    n_hp = d // hp_w
    row_blk = lambda group: pl.BlockSpec((ATT_T, hp_w), lambda b, h, i: (b * nq + i, group * n_hp + h))
    seq_blk = lambda group: pl.BlockSpec((seq, hp_w), lambda b, h, i: (b, group * n_hp + h))
    return row_blk, seq_blk, n_hp


def _attn_fwd(qkv, proj, bsz, seq):
    t, d3 = qkv.shape
    d = d3 // 3
    hd = d // HEADS
    nq = seq // ATT_T
    scale = hd ** -0.5
    row_blk, seq_blk, n_hp = _attn_specs(d, seq, nq)

    def body(q_ref, k_ref, v_ref, zb_ref, o_ref, yb_ref):
        i = pl.program_id(2)
        causal = _tri(ATT_T, True)
        upper2 = _twice(causal)

        def step(kb, state, mask):
            rows = pl.ds(pl.multiple_of(kb * ATT_T, ATT_T), ATT_T)
            heads = [slice(h * hd, (h + 1) * hd) for h in range(ATT_HP)]
            scores = [_sb_block(q_ref[:, cols], k_ref[rows, cols], scale, upper2, mask) for cols in heads]
            new = []
            for cols, (carry, acc), (log_beta, sx, total) in zip(heads, state, scores):
                a = jnp.exp2(log_beta + sx + carry)
                if mask is not None:
                    a = jnp.where(mask, a, 0.0)
                acc = acc + jnp.dot(a.astype(BF16), v_ref[rows, cols], preferred_element_type=F32)
                new.append((carry + total, acc))
            return tuple(new)

        init = tuple((jnp.zeros((ATT_T, 1), F32), jnp.zeros((ATT_T, hd), F32)) for _ in range(ATT_HP))
        state = step(i, init, causal)
        def more(c):
            new = step(c[0], c[1], None)
            return c[0] - 1, new, _max_carry([s[0] for s in new])

        _, state, _ = lax.while_loop(lambda c: (c[0] >= 0) & (c[2] > DEAD_LOG2), more,
                                     (i - 1, state, _max_carry([s[0] for s in state])))
        for h in range(ATT_HP):
            cols = slice(h * hd, (h + 1) * hd)
            acc = state[h][1]
            zb = zb_ref[:, cols]
            o_ref[:, cols] = acc
            yb_ref[:, cols] = (acc * (zb * _sigmoid(zb))).astype(BF16)

    return pl.pallas_call(
        body, name="attn_fwd",
        grid=(bsz, n_hp, nq),
        in_specs=[row_blk(0), seq_blk(1), seq_blk(2), row_blk(ZB_GROUP)],
        out_specs=[row_blk(0), row_blk(0)],
        out_shape=[jax.ShapeDtypeStruct((t, d), F32), jax.ShapeDtypeStruct((t, d), BF16)],
        compiler_params=_params(("arbitrary", "arbitrary", "arbitrary")),
    )(qkv, qkv, qkv, proj)


def _attn_bwd(qkv, proj, o, dyb, bsz, seq):
    t, d3 = qkv.shape
    d = d3 // 3
    hd = d // HEADS
    nq = seq // ATT_T
    scale = hd ** -0.5
    row_blk, seq_blk, n_hp = _attn_specs(d, seq, nq)

    def body(q_ref, k_ref, v_ref, zb_ref, o_ref, dyb_ref, dq_ref, dk_ref, dv_ref, dzb_ref,
             g_s, beta_s, dkt_acc, dvt_acc):
        i = pl.program_id(2)

        @pl.when(i == 0)
        def _():
            dkt_acc[...] = jnp.zeros_like(dkt_acc)
            dvt_acc[...] = jnp.zeros_like(dvt_acc)

        causal = _tri(ATT_T, True)
        upper2 = _twice(causal)
        lower2 = _twice(~causal)
        zb = zb_ref[...]
        sig = _sigmoid(zb)
        dyb_t = dyb_ref[...]
        do_f = dyb_t * (zb * sig)
        do = do_f.astype(BF16)
        do_t = do_f.T.astype(BF16)
        q_t = q_ref[...].astype(F32).T.astype(BF16)
        dzb_ref[...] = (dyb_t * o_ref[...] * (sig * (1.0 + zb * (1.0 - sig)))).astype(BF16)

        def sweep(kb, carries, mask):
            rows = pl.ds(pl.multiple_of(kb * ATT_T, ATT_T), ATT_T)
            heads = [slice(h * hd, (h + 1) * hd) for h in range(ATT_HP)]
            scores = [_sb_block(q_ref[:, cols], k_ref[rows, cols], scale, upper2, mask) for cols in heads]
            das = [lax.dot_general(do[:, cols], v_ref[rows, cols], NT, preferred_element_type=F32) for cols in heads]
            new = []
            for h, (cols, carry, (log_beta, sx, total), da) in enumerate(zip(heads, carries, scores, das)):
                a = jnp.exp2(log_beta + sx + carry)
                beta = jnp.exp2(log_beta)
                if mask is not None:
                    a = jnp.where(mask, a, 0.0)
                    beta = jnp.where(mask, beta, 0.0)
                g_s[h, kb] = a * da
                beta_s[h, kb] = beta
                dvt_acc[kb, cols, :] += jnp.dot(do_t[cols, :], a.astype(BF16), preferred_element_type=F32)
                new.append(carry + total)
            return tuple(new)

        carries = sweep(i, tuple(jnp.zeros((ATT_T, 1), F32) for _ in range(ATT_HP)), causal)

        def more(c):
            new = sweep(c[0], c[1], None)
            return c[0] - 1, new, _max_carry(new)

        last, _, _ = lax.while_loop(lambda c: (c[0] >= 0) & (c[2] > DEAD_LOG2), more, (i - 1, carries, _max_carry(carries)))
        first_kb = last + 1

        def back(kb, state):
            rows = pl.ds(pl.multiple_of(kb * ATT_T, ATT_T), ATT_T)
            heads = [slice(h * hd, (h + 1) * hd) for h in range(ATT_HP)]
            sums = [_cumsum_mm(g_s[h, kb], lower2) for h in range(ATT_HP)]
            new = []
            for h, (cols, (p_carry, dq), px) in enumerate(zip(heads, state, sums)):
                dz = ((g_s[h, kb] - (p_carry + px) * beta_s[h, kb]) * scale).astype(BF16)
                dq = dq + jnp.dot(dz, k_ref[rows, cols], preferred_element_type=F32)
                dkt_acc[kb, cols, :] += jnp.dot(q_t[cols, :], dz, preferred_element_type=F32)
                new.append((p_carry + px[:, ATT_T - 1:ATT_T], dq))
            return tuple(new)

        init = tuple((jnp.zeros((ATT_T, 1), F32), jnp.zeros((ATT_T, hd), F32)) for _ in range(ATT_HP))
        state = lax.fori_loop(first_kb, i + 1, back, init)
        for h in range(ATT_HP):
            dq_ref[:, h * hd:(h + 1) * hd] = state[h][1].astype(BF16)

        @pl.when(i == nq - 1)
        def _():
            for kb in range(nq):
                dk_ref[kb * ATT_T:(kb + 1) * ATT_T, :] = dkt_acc[kb].T.astype(BF16)
                dv_ref[kb * ATT_T:(kb + 1) * ATT_T, :] = dvt_acc[kb].T.astype(BF16)

    out = jax.ShapeDtypeStruct((t, d), BF16)
    hp_w = ATT_HP * hd
    return pl.pallas_call(
        body, name="attn_bwd",
        grid=(bsz, n_hp, nq),
        in_specs=[row_blk(0), seq_blk(1), seq_blk(2), row_blk(ZB_GROUP), row_blk(0), row_blk(0)],
        out_specs=[row_blk(0), seq_blk(0), seq_blk(0), row_blk(0)],
        out_shape=[out, out, out, out],
        scratch_shapes=[pltpu.VMEM((ATT_HP, nq, ATT_T, ATT_T), F32), pltpu.VMEM((ATT_HP, nq, ATT_T, ATT_T), F32),
                        pltpu.VMEM((nq, hp_w, ATT_T), F32), pltpu.VMEM((nq, hp_w, ATT_T), F32)],
        compiler_params=_params(("arbitrary", "arbitrary", "arbitrary")),
    )(qkv, qkv, qkv, proj, o, dyb)


def _out_proj(ya, yb, g_pre, x2d, tgt, w_og, w_osb, w_out, g_f):
    t, d = x2d.shape

    def body(ya_ref, yb_ref, ga_ref, gb_ref, x_ref, tgt_ref, wog_ref, wosb_ref, wout_ref, gf_ref,
             dya_ref, dyb_ref, dg_ref, dx2_ref, loss_ref, gnf_ref, gwog_ref, gwosb_ref, gwout_ref):
        @pl.when(pl.program_id(0) == 0)
        def _():
            loss_ref[...] = jnp.zeros_like(loss_ref)
            gnf_ref[...] = jnp.zeros_like(gnf_ref)
            gwog_ref[...] = jnp.zeros_like(gwog_ref)
            gwosb_ref[...] = jnp.zeros_like(gwosb_ref)
            gwout_ref[...] = jnp.zeros_like(gwout_ref)

        ya = ya_ref[...]
        yb = yb_ref[...]
        pa = jnp.dot(ya, wog_ref[...], preferred_element_type=F32)
        pb = jnp.dot(yb, wosb_ref[...], preferred_element_type=F32)
        sga = _sigmoid(ga_ref[...])
        sgb = _sigmoid(gb_ref[...])
        merged = (sga * pa + sgb * pb).astype(BF16)
        x2 = x_ref[...] + jnp.dot(merged, wout_ref[...], preferred_element_type=F32)
        r2 = lax.rsqrt(jnp.mean(x2 * x2, axis=-1, keepdims=True) + EPS)
        n2 = x2 * r2
        gf = gf_ref[...]
        err = n2 * gf - tgt_ref[...]
        loss_ref[...] += 0.5 * jnp.sum(jnp.sum(err * err, axis=-1, keepdims=True), axis=0, keepdims=True) / d
        dy = err * (1.0 / d)
        gnf_ref[...] += jnp.sum(dy * n2, axis=0, keepdims=True)
        dn = dy * gf
        dx2 = r2 * (dn - n2 * jnp.mean(dn * n2, axis=-1, keepdims=True))
        dx2_ref[...] = dx2
        dx2_b = dx2.astype(BF16)
        dmerged = lax.dot_general(dx2_b, wout_ref[...], NT, preferred_element_type=F32)
        gwout_ref[...] += lax.dot_general(merged, dx2_b, TN, preferred_element_type=F32)
        dg_ref[:, 0:d] = (dmerged * pa * (sga * (1.0 - sga))).astype(BF16)
        dg_ref[:, d:2 * d] = (dmerged * pb * (sgb * (1.0 - sgb))).astype(BF16)
        dpa = (dmerged * sga).astype(BF16)
        dpb = (dmerged * sgb).astype(BF16)
        dya_ref[...] = lax.dot_general(dpa, wog_ref[...], NT, preferred_element_type=F32)
        dyb_ref[...] = lax.dot_general(dpb, wosb_ref[...], NT, preferred_element_type=F32)
        gwog_ref[...] += lax.dot_general(ya, dpa, TN, preferred_element_type=F32)
        gwosb_ref[...] += lax.dot_general(yb, dpb, TN, preferred_element_type=F32)

    row = lambda i: (i, 0)
    return pl.pallas_call(
        body, name="out_proj",
        grid=(t // TM,),
        in_specs=[pl.BlockSpec((TM, d), row), pl.BlockSpec((TM, d), row),
                  pl.BlockSpec((TM, d), lambda i: (i, GA_GROUP)), pl.BlockSpec((TM, d), lambda i: (i, GB_GROUP)),
                  pl.BlockSpec((TM, d), row), pl.BlockSpec((TM, d), row),
                  _resident((d, d)), _resident((d, d)), _resident((d, d)), _const((1, d))],
        out_specs=[pl.BlockSpec((TM, d), row), pl.BlockSpec((TM, d), row), pl.BlockSpec((TM, 2 * d), row),
                   pl.BlockSpec((TM, d), row), _const((1, 1)), _const((1, d)),
                   _const((d, d)), _const((d, d)), _const((d, d))],
        out_shape=[jax.ShapeDtypeStruct((t, d), F32), jax.ShapeDtypeStruct((t, d), F32),
                   jax.ShapeDtypeStruct((t, 2 * d), BF16), jax.ShapeDtypeStruct((t, d), F32),
                   jax.ShapeDtypeStruct((1, 1), F32), jax.ShapeDtypeStruct((1, d), F32),
                   jax.ShapeDtypeStruct((d, d), F32), jax.ShapeDtypeStruct((d, d), F32),
                   jax.ShapeDtypeStruct((d, d), F32)],
        compiler_params=_params(("arbitrary",)),
    )(ya, yb, g_pre, g_pre, x2d, tgt, w_og, w_osb, w_out, g_f)


def _dproj_pieces(d):
    return [(0, 0, 3), (1, 3, 1), (2, 4, 1), (3, 5, 1), (4, 6, 1), (5, 7, 2)]


def _in_proj_bwd_x(pieces, wg, x2d, g_in, dx2, gw16):
    t, d = x2d.shape
    ncol = wg.shape[2]
    segs = _segments(d, ncol)
    layout = _dproj_pieces(d)
    nsteps = t // TM

    def body(da_ref, dq_ref, dk_ref, dv_ref, dzb_ref, dg_ref, w_ref, x_ref, g_ref, dx2_ref, gw16_ref,
             gx_ref, gn_ref, recv_ref, send_sems, recv_sems):
        x_pos, y_pos, c_pos, chips = _place()

        def share(k, chunk):
            px, py = chips[k]
            return pltpu.make_async_remote_copy(
                src_ref=gw16_ref.at[:, chunk * ncol:(chunk + 1) * ncol], dst_ref=recv_ref.at[k],
                send_sem=send_sems.at[k], recv_sem=recv_sems.at[k], device_id=(px, py, c_pos), device_id_type=MESH)

        @pl.when(pl.program_id(0) == 0)
        def _():
            gn_ref[...] = jnp.zeros_like(gn_ref)
            for k, (px, py) in enumerate(chips):
                for chunk in range(N_CHIP):
                    @pl.when(2 * px + py == chunk)
                    def _(k=k, chunk=chunk):
                        share(k, chunk).start()

        @pl.when(pl.program_id(0) == nsteps - 1)
        def _():
            for k in range(N_CHIP - 1):
                share(k, 0).wait()

        refs = (da_ref, dq_ref, dk_ref, dv_ref, dzb_ref, dg_ref)
        dh = jnp.zeros((TM, d), F32)
        for chip, c0, grp, s0, width in segs:
            piece, first, _ = next(p for p in layout if p[1] <= grp < p[1] + p[2])
            off = (grp - first) * d + s0
            dh = dh + lax.dot_general(refs[piece][:, off:off + width], w_ref[chip, :, c0:c0 + width], NT,
                                      preferred_element_type=F32)
        x = x_ref[...]
        r = lax.rsqrt(jnp.mean(x * x, axis=-1, keepdims=True) + EPS)
        n = x * r
        gn_ref[...] += jnp.sum(dh * n, axis=0, keepdims=True)
        dn = dh * g_ref[...]
        gx_ref[...] = dx2_ref[...] + r * (dn - n * jnp.mean(dn * n, axis=-1, keepdims=True))

    row = lambda i: (i, 0)
    return pl.pallas_call(
        body, name="in_proj_bwd_x",
        grid=(t // TM,),
        in_specs=[pl.BlockSpec((TM, p.shape[1]), row) for p in pieces]
        + [_resident(wg.shape), pl.BlockSpec((TM, d), row), _const((1, d)), pl.BlockSpec((TM, d), row), ANY],
        out_specs=[pl.BlockSpec((TM, d), row), _const((1, d)), ANY],
        out_shape=[jax.ShapeDtypeStruct((t, d), F32), jax.ShapeDtypeStruct((1, d), F32),
                   jax.ShapeDtypeStruct((N_CHIP - 1, d, ncol), BF16)],
        scratch_shapes=[pltpu.SemaphoreType.DMA((N_CHIP - 1,)), pltpu.SemaphoreType.DMA((N_CHIP - 1,))],
        compiler_params=_params(("arbitrary",)),
    )(*pieces, wg, x2d, g_in, dx2, gw16)


def _in_proj_bwd_w(h_t, pieces, mats16, pack):
    d, t = h_t.shape
    nk = t // TKW
    layout = _dproj_pieces(d)
    n_mats = len(mats16)
    n_dev = 2 * N_CHIP
    flips = [(dx, dy, dc) for dx in (0, 1) for dy in (0, 1) for dc in (0, 1)][1:]

    def body(ht_ref, da_ref, dq_ref, dk_ref, dv_ref, dzb_ref, dg_ref, *rest):
        mat_refs, pack_ref = rest[:n_mats], rest[n_mats]
        gw_ref, gw16_ref = rest[n_mats + 1:n_mats + 3]
        recv_refs, slots_ref = rest[n_mats + 3:2 * n_mats + 3], rest[2 * n_mats + 3]
        acc, mat_send, mat_recv, pack_send, pack_recv, own_sem = rest[2 * n_mats + 4:]
        s = pl.program_id(0)
        i = pl.program_id(1)
        x_pos, y_pos, c_pos, chips = _place()
        me = 4 * x_pos + 2 * y_pos + c_pos

        def exchanges():
            cps = []
            for k, (px, py) in enumerate(chips):
                for a in range(n_mats):
                    cps.append(pltpu.make_async_remote_copy(
                        src_ref=mat_refs[a].at[2 * px + py], dst_ref=recv_refs[a].at[k],
                        send_sem=mat_send.at[a, k], recv_sem=mat_recv.at[a, k],
                        device_id=(px, py, c_pos), device_id_type=MESH))
            for k, (dx, dy, dc) in enumerate(flips):
                peer = (1 - x_pos if dx else x_pos, 1 - y_pos if dy else y_pos, 1 - c_pos if dc else c_pos)
                cps.append(pltpu.make_async_remote_copy(
                    src_ref=pack_ref, dst_ref=slots_ref.at[me], send_sem=pack_send.at[k], recv_sem=pack_recv.at[k],
                    device_id=peer, device_id_type=MESH))
            return cps, pltpu.make_async_copy(pack_ref, slots_ref.at[me], own_sem)

        @pl.when((s == 0) & (i == 0))
        def _():
            cps, own = exchanges()
            own.start()
            for cp in cps:
                cp.start()

        @pl.when((s == N_SPLIT - 1) & (i == nk - 1))
        def _():
            cps, own = exchanges()
            own.wait()
            for cp in cps:
                cp.wait()

        @pl.when(i == 0)
        def _():
            acc[...] = jnp.zeros_like(acc)

        refs = (da_ref, dq_ref, dk_ref, dv_ref, dzb_ref, dg_ref)
        for piece, first, count in layout:
            @pl.when((s >= first) & (s < first + count))
            def _(piece=piece):
                acc[...] += jnp.dot(ht_ref[...], refs[piece][...], preferred_element_type=F32)

        @pl.when(i == nk - 1)
        def _():
            gw_ref[...] = acc[...]
            gw16_ref[...] = acc[...].astype(BF16)

    def piece_spec(p, first, count):
        def index(s, i):
            mine = (s >= first) & (s < first + count)
            return jnp.where(mine, i, 0), jnp.where(mine, s - first, 0)
        return pl.BlockSpec((TKW, d), index)

    col_blk = pl.BlockSpec((d, d), lambda s, i: (0, s))
    outs = pl.pallas_call(
        body, name="in_proj_bwd_w",
        grid=(N_SPLIT, nk),
        in_specs=[pl.BlockSpec((d, TKW), lambda s, i: (0, i))] + [piece_spec(*p) for p in layout] + [ANY] * (n_mats + 1),
        out_specs=[col_blk, col_blk] + [ANY] * (n_mats + 1),
        out_shape=[jax.ShapeDtypeStruct((d, N_SPLIT * d), F32), jax.ShapeDtypeStruct((d, N_SPLIT * d), BF16)]
        + [jax.ShapeDtypeStruct((N_CHIP - 1,) + m.shape[1:], BF16) for m in mats16]
        + [jax.ShapeDtypeStruct((n_dev,) + pack.shape, F32)],
        scratch_shapes=[pltpu.VMEM((d, d), F32),
                        pltpu.SemaphoreType.DMA((n_mats, N_CHIP - 1)), pltpu.SemaphoreType.DMA((n_mats, N_CHIP - 1)),
                        pltpu.SemaphoreType.DMA((n_dev - 1,)), pltpu.SemaphoreType.DMA((n_dev - 1,)),
                        pltpu.SemaphoreType.DMA],
        compiler_params=_params(("arbitrary", "arbitrary")),
    )(h_t, *pieces, *mats16, pack)
    return outs[0], outs[1], outs[2:2 + n_mats], outs[2 + n_mats]


def _local_step(proj, qkv, x2d, tgt2d, bsz, seq, norm_v, w_s, b_s, w_og, w_osb, w_out, norm_final):
    d = x2d.shape[1]
    chunk = w_s.shape[-1]
    causal = jnp.tril(jnp.ones((chunk, chunk), dtype=bool))
    wm = jnp.where(causal[None], w_s, 0.0).astype(BF16)
    wm_t = jnp.swapaxes(wm, 1, 2)
    b_t = b_s.T

    ya = _branch_a_fwd(proj, norm_v, wm, b_t)
    o, yb = _attn_fwd(qkv, proj, bsz, seq)
    dya, dyb, dg, dx2, loss, g_nf, g_wog, g_wosb, g_wout = _out_proj(
        ya, yb, proj, x2d, tgt2d, w_og, w_osb, w_out, norm_final.reshape(1, d))
    dq, dk, dv, dzb = _attn_bwd(qkv, proj, o, dyb, bsz, seq)
    d_a, g_ws, g_bt, g_nv = _branch_a_bwd(proj, dya, norm_v, wm, wm_t, b_t)
    g_ws = jnp.where(causal[None], g_ws, 0.0)
    return loss, (d_a, dq, dk, dv, dzb, dg), dx2, g_nv, g_ws, g_bt.T, g_wog, g_wosb, g_wout, g_nf


def _row_tile(rows):
    return next(r for r in (128, 64, 32, 16, 8) if rows % r == 0)


def _cast_bf16(arrs):
    n = len(arrs)

    def body(*refs):
        for a_ref, o_ref in zip(refs[:n], refs[n:]):
            o_ref[...] = a_ref[...].astype(BF16)

    specs = [pl.BlockSpec((a.shape[0] // CAST_STEPS, a.shape[1]), lambda i: (i, 0)) for a in arrs]
    return pl.pallas_call(
        body, name="cast_bf16", grid=(CAST_STEPS,),
        in_specs=specs, out_specs=specs,
        out_shape=[jax.ShapeDtypeStruct(a.shape, BF16) for a in arrs],
        compiler_params=_params(("arbitrary",)),
    )(*arrs)


def _add_received(full, recv, chip, by_cols):
    _, rows, cols = recv.shape
    tr = _row_tile(rows)
    nb = rows // tr

    def body(chip_ref, own_ref, recv_ref, o_ref):
        s = own_ref[...]
        for k in range(N_CHIP - 1):
            s = s + recv_ref[k].astype(F32)
        o_ref[...] = s

    own_map = (lambda i, chip_ref: (i, chip_ref[0])) if by_cols else (lambda i, chip_ref: (chip_ref[0] * nb + i, 0))
    return pl.pallas_call(
        body, name="add_received",
        grid_spec=pltpu.PrefetchScalarGridSpec(
            num_scalar_prefetch=1, grid=(nb,),
            in_specs=[pl.BlockSpec((tr, cols), own_map),
                      pl.BlockSpec((N_CHIP - 1, tr, cols), lambda i, chip_ref: (0, i, 0))],
            out_specs=pl.BlockSpec((tr, cols), lambda i, chip_ref: (i, 0))),
        out_shape=jax.ShapeDtypeStruct((rows, cols), F32),
        compiler_params=_params(("arbitrary",)),
    )(chip.reshape(1).astype(jnp.int32), full, recv)


def _adamw_math(w, m, v, g):
    new_m = ADAM_B1 * m + (1.0 - ADAM_B1) * g
    new_v = ADAM_B2 * v + (1.0 - ADAM_B2) * (g * g)
    m_hat = new_m / (1.0 - ADAM_B1 ** ADAM_STEP)
    v_hat = new_v / (1.0 - ADAM_B2 ** ADAM_STEP)
    return -ADAM_LR * (m_hat / (jnp.sqrt(v_hat) + ADAM_EPS) + ADAM_WD * w), new_m, new_v


def _adamw(w, m, v, g_parts):
    rows, cols = w.shape
    tr = _row_tile(rows)
    n_parts = len(g_parts)

    def body(*refs):
        w_ref, m_ref, v_ref = refs[:3]
        part_refs = refs[3:3 + n_parts]
        g_ref, d_ref, nm_ref, nv_ref = refs[3 + n_parts:]
        g = part_refs[0][...]
        for p in part_refs[1:]:
            g = g + p[...]
        g_ref[...] = g
        d_ref[...], nm_ref[...], nv_ref[...] = _adamw_math(w_ref[...], m_ref[...], v_ref[...], g)

    spec = pl.BlockSpec((tr, cols), lambda i: (i, 0))
    out = jax.ShapeDtypeStruct(w.shape, F32)
    return pl.pallas_call(
        body, name="adamw", grid=(rows // tr,),
        in_specs=[spec] * (3 + n_parts), out_specs=[spec] * 4, out_shape=[out] * 4,
        compiler_params=_params(("arbitrary",)),
    )(w, m, v, *g_parts)


def _adamw_small(w, m, v, slots_head, slots_tail):
    n_dev, p0, _ = slots_head.shape

    def body(w_ref, m_ref, v_ref, head_ref, tail_ref, g_ref, d_ref, nm_ref, nv_ref):
        for ref, rows in ((head_ref, slice(0, p0)), (tail_ref, slice(p0, w.shape[0]))):
            g = ref[0]
            for i in range(1, n_dev):
                g = g + ref[i]
            g_ref[rows, :] = g
            d_ref[rows, :], nm_ref[rows, :], nv_ref[rows, :] = _adamw_math(w_ref[rows, :], m_ref[rows, :], v_ref[rows, :], g)

    vmem = pl.BlockSpec(memory_space=pltpu.VMEM)
    out = jax.ShapeDtypeStruct(w.shape, F32)
    return pl.pallas_call(
        body, name="adamw_small", in_specs=[vmem] * 5, out_specs=[vmem] * 4, out_shape=[out] * 4,
        compiler_params=pltpu.CompilerParams(vmem_limit_bytes=VMEM_LIMIT),
    )(w, m, v, slots_head, slots_tail)


ANY = pl.BlockSpec(memory_space=pl.ANY)


def _place():
    x, y, c = lax.axis_index("x"), lax.axis_index("y"), lax.axis_index("c")
    other_chips = [(1 - x, y), (x, 1 - y), (1 - x, 1 - y)]
    return x, y, c, other_chips


def _swap_and_gather(arrs, pack):
    n = len(arrs)
    n_dev = 2 * N_CHIP
    flips = [(dx, dy, dc) for dx in (0, 1) for dy in (0, 1) for dc in (0, 1)][1:]

    def body(*refs):
        ins, pack_ref = refs[:n], refs[n]
        outs, slots_ref = refs[n + 1:2 * n + 1], refs[2 * n + 1]
        send_sems, recv_sems, pack_send, pack_recv, own_sem = refs[2 * n + 2:]
        x, y, c, _ = _place()
        me = 4 * x + 2 * y + c
        own = pltpu.make_async_copy(pack_ref, slots_ref.at[me], own_sem)
        own.start()
        copies = []
        for k, (dx, dy, dc) in enumerate(flips):
            peer = (1 - x if dx else x, 1 - y if dy else y, 1 - c if dc else c)
            copies.append(pltpu.make_async_remote_copy(
                src_ref=pack_ref, dst_ref=slots_ref.at[me], send_sem=pack_send.at[k], recv_sem=pack_recv.at[k],
                device_id=peer, device_id_type=MESH))
        copies += [pltpu.make_async_remote_copy(
            src_ref=ins[a], dst_ref=outs[a], send_sem=send_sems.at[a], recv_sem=recv_sems.at[a],
            device_id=(x, y, 1 - c), device_id_type=MESH) for a in range(n)]
        for cp in copies:
            cp.start()
        for cp in copies:
            cp.wait()
        own.wait()

    return pl.pallas_call(
        body, name="swap_and_gather",
        in_specs=[ANY] * (n + 1), out_specs=[ANY] * (n + 1),
        out_shape=[jax.ShapeDtypeStruct(a.shape, a.dtype) for a in arrs] + [jax.ShapeDtypeStruct((n_dev,) + pack.shape, F32)],
        scratch_shapes=[pltpu.SemaphoreType.DMA((n,)), pltpu.SemaphoreType.DMA((n,)),
                        pltpu.SemaphoreType.DMA((n_dev - 1,)), pltpu.SemaphoreType.DMA((n_dev - 1,)),
                        pltpu.SemaphoreType.DMA],
    )(*arrs, pack)


SLAB = 8


def _slab(vec, d):
    return jnp.pad(vec.reshape(1, d), ((0, SLAB - 1), (0, 0)))


def _pack_tail(vec_nv, b_s, vec_nf, w_s, scalar=None):
    d = vec_nv.shape[-1]
    extra = jnp.zeros((1, d), F32) if scalar is None else jnp.pad(scalar, ((0, 0), (0, d - 1)))
    return jnp.concatenate([_slab(v, d) for v in (vec_nv, b_s, vec_nf, extra)] + [w_s.reshape(-1, d)], axis=0)


def _pack_small(vec_nin, vec_nv, b_s, vec_nf, w_s):
    return jnp.concatenate([_slab(vec_nin, vec_nin.shape[-1]), _pack_tail(vec_nv, b_s, vec_nf, w_s)], axis=0)


def _unpack_small(pack, w_s_shape, b_s_shape):
    return (pack[0:1], pack[SLAB:SLAB + 1], pack[2 * SLAB].reshape(b_s_shape), pack[3 * SLAB],
            pack[5 * SLAB:].reshape(w_s_shape), pack[4 * SLAB, 0])


def kernel(x, norm_in, w_in, norm_v, w_s, b_s, w_o_gmlp, w_o_sb, w_out, norm_final, loss_target, m_norm_in, m_w_in, m_norm_v, m_w_s, m_b_s, m_w_o_gmlp, m_w_o_sb, m_w_out, m_norm_final, v_norm_in, v_w_in, v_norm_v, v_w_s, v_b_s, v_w_o_gmlp, v_w_o_sb, v_w_out, v_norm_final):
    d = x.shape[-1]
    ncol = w_in.shape[-1]
    nrow = w_o_gmlp.shape[-2]
    chip = 2 * lax.axis_index("x") + lax.axis_index("y")

    bsz, seq, _ = x.shape
    x2d = x.reshape(bsz * seq, d)
    shards = [w_in[0], w_o_gmlp[0], w_o_sb[0], w_out[0]]
    halves = [s16.reshape(2, s16.shape[0] // 2, s16.shape[1]) for s16 in _cast_bf16(shards)]
    proj, qkv, h_t, (wg, w_og, w_osb, w_o) = _in_proj_fwd(x2d, norm_in, halves)
    wg = wg.reshape(N_CHIP, d, ncol)

    loss, pieces, dx2, g_nv, g_ws, g_bs, g_wog, g_wosb, g_wout, g_nf = _local_step(
        proj, qkv, x2d, loss_target.reshape(bsz * seq, d), bsz, seq, norm_v, w_s[0], b_s[0],
        w_og.reshape(d, d), w_osb.reshape(d, d), w_o.reshape(d, d), norm_final)

    mats = [g_wog, g_wosb, g_wout]
    mats16 = [g16.reshape(N_CHIP, nrow, d) for g16 in _cast_bf16(mats)]
    g_win, g_win16, recv_mats, slots_tail = _in_proj_bwd_w(h_t, pieces, mats16, _pack_tail(g_nv, g_bs, g_nf, g_ws, loss))
    grad_x, g_nin, recv_win = _in_proj_bwd_x(pieces, wg, x2d, norm_in, dx2, g_win16)
    grad_x = grad_x.reshape(bsz, seq, d)

    sums = [_add_received(g_win, recv_win, chip, True)] + [
        _add_received(g, r, chip, False) for g, r in zip(mats, recv_mats)]
    *sibling_sums, slots_head = _swap_and_gather(sums, _slab(g_nin, d))
    stats = []
    for w, m, v, mine, theirs in zip(shards, [m_w_in[0], m_w_o_gmlp[0], m_w_o_sb[0], m_w_out[0]],
                                     [v_w_in[0], v_w_o_gmlp[0], v_w_o_sb[0], v_w_out[0]], sums, sibling_sums):
        stats.append(_adamw(w, m, v, [mine, theirs]))
    (gw_in, dw_in, nm_in, nv_in), (gw_og, dw_og, nm_og, nv_og), (gw_osb, dw_osb, nm_osb, nv_osb), \
        (gw_out, dw_out, nm_out, nv_out) = stats

    gs, ds, ms, vs = _adamw_small(
        _pack_small(norm_in, norm_v, b_s[0], norm_final, w_s[0]),
        _pack_small(m_norm_in, m_norm_v, m_b_s[0], m_norm_final, m_w_s[0]),
        _pack_small(v_norm_in, v_norm_v, v_b_s[0], v_norm_final, v_w_s[0]), slots_head, slots_tail)

    def small(pack):
        nin, nv, bs, nf, ws, _ = _unpack_small(pack, w_s.shape, b_s.shape)
        return nin, nv, ws, bs, nf

    loss = _unpack_small(gs, w_s.shape, b_s.shape)[-1]
    out = []
    for small_pack, win, wog, wosb, wout in ((gs, gw_in, gw_og, gw_osb, gw_out), (ds, dw_in, dw_og, dw_osb, dw_out),
                                             (ms, nm_in, nm_og, nm_osb, nm_out), (vs, nv_in, nv_og, nv_osb, nv_out)):
        nin, nv, ws, bs, nf = small(small_pack)
        out += [nin, win[None], nv, ws, bs, wog[None], wosb[None], wout[None], nf]
    return (loss, grad_x, *out)
```

```python
import functools
import math

import jax
import jax.numpy as jnp
from jax import lax
from jax.experimental import pallas as pl
from jax.experimental.pallas import tpu as pltpu

F32 = jnp.float32
BF16 = jnp.bfloat16
EPS = 1e-6
HEADS = 8
N_SPLIT = 9
N_CHIP = 4
MESH = pl.DeviceIdType.MESH

ADAM_LR = 0.001
ADAM_B1 = 0.9
ADAM_B2 = 0.999
ADAM_EPS = 1e-08
ADAM_WD = 0.01
ADAM_STEP = 10

VMEM_LIMIT = 56 * 2 ** 20
TM = 256
TMF = 512
ATT_T = 256
ATT_HP = 4
TKW = 1024
CHUNKS_PER_STEP = 4
CAST_STEPS = 8

NT = (((1,), (1,)), ((), ()))
TN = (((0,), (0,)), ((), ()))


def _params(sem):
    return pltpu.CompilerParams(dimension_semantics=sem, vmem_limit_bytes=VMEM_LIMIT)


def _resident(shape):
    nd = len(shape)
    return pl.BlockSpec(shape, lambda *_: (0,) * nd, pipeline_mode=pl.Buffered(1))


def _const(shape):
    nd = len(shape)
    return pl.BlockSpec(shape, lambda *_: (0,) * nd)


def _segments(d, ncol):
    segs = []
    edges = sorted({j * ncol for j in range(N_CHIP + 1)} | {s * d for s in range(N_SPLIT + 1)})
    for lo, hi in zip(edges[:-1], edges[1:]):
        segs.append((lo // ncol, lo % ncol, lo // d, lo % d, hi - lo))
    return segs


def _sigmoid(x):
    return 0.5 * jnp.tanh(0.5 * x) + 0.5


_GELU_C = math.sqrt(2.0 / math.pi)


_GELU_CA = _GELU_C * 0.044715


def _gelu(x):
    return x * (0.5 * jnp.tanh(x * (_GELU_C + _GELU_CA * (x * x))) + 0.5)


def _gelu_and_grad(x):
    x2 = x * x
    u = 0.5 * jnp.tanh(x * (_GELU_C + _GELU_CA * x2)) + 0.5
    slope = (1.0 - u) * (x * (_GELU_C + (3.0 * _GELU_CA) * x2))
    return x * u, u * (2.0 * slope + 1.0)


def _split_bf16(a):
    hi = a.astype(BF16)
    lo = (a - hi.astype(F32)).astype(BF16)
    return hi, lo


def _in_proj_fwd(x2d, g_in, halves):
    t, d = x2d.shape
    n = len(halves)
    ncol = halves[0].shape[2]
    n_row = t // TMF
    last = n_row - 1
    assert halves[0].shape[1] * 2 == d
    qkv_parts = {j: (max(j * ncol, 3 * d) - j * ncol, max(j * ncol, 3 * d) - 3 * d)
                 for j in range(N_CHIP) if min((j + 1) * ncol, 6 * d) > max(j * ncol, 3 * d)}
    qkv_w = 3 * d // len(qkv_parts)
    assert all(min((j + 1) * ncol, 6 * d) - max(j * ncol, 3 * d) == qkv_w and q0 % qkv_w == 0
               for j, (_, q0) in qkv_parts.items())

    def body(order_ref, qrow_ref, qcol_ref, x_ref, g_ref, *rest):
        ins = rest[:n]
        proj_ref, qkv_ref, ht_ref = rest[n:n + 3]
        outs = rest[n + 3:2 * n + 3]
        wbuf, h_all, send_sems, recv_sems, local_sems, load_sem = rest[2 * n + 3:]
        phase = pl.program_id(0)
        i = pl.program_id(1)
        x_pos, y_pos, c_pos, chips = _place()
        sibling = (x_pos, y_pos, 1 - c_pos)
        me = (x_pos, y_pos, c_pos)
        my_chip = 2 * x_pos + y_pos

        def copy(a, k, block, to, src=None):
            return pltpu.make_async_remote_copy(
                src_ref=outs[a].at[block] if src is None else src, dst_ref=outs[a].at[block],
                send_sem=send_sems.at[a, k], recv_sem=recv_sems.at[a, k], device_id=to, device_id_type=MESH)

        def local(a):
            return pltpu.make_async_copy(ins[a], outs[a].at[pl.ds(2 * my_chip, 2)], local_sems.at[a])

        def load(src, first, slot):
            for half in range(2):
                cp = pltpu.make_async_copy(src.at[first + half], wbuf.at[slot, pl.ds(half * (d // 2), d // 2)], load_sem)
                cp.start()
                cp.wait()

        def send_mine(k):
            px, py = chips[k]
            for a in range(n):
                copy(a, k, 2 * my_chip + c_pos, (px, py, c_pos), src=ins[a].at[c_pos]).start()

        @pl.when((phase == 0) & (i == 0))
        def _():
            for a in range(n):
                local(a).start()
            send_mine(0)
            send_mine(1)
            load(ins[0], 0, 0)

        for k, (px, py) in enumerate(chips):
            @pl.when((phase == k + 1) & (i == 0))
            def _(k=k, px=px, py=py):
                theirs = 2 * (2 * px + py)
                for a in range(n):
                    copy(a, k, theirs + c_pos, me).wait_recv()
                    copy(a, 3 + k, theirs + c_pos, sibling).start()
                if k == 0:
                    send_mine(2)
                for a in range(n):
                    copy(a, 3 + k, theirs + 1 - c_pos, me).wait_recv()
                load(outs[0], theirs, (k + 1) % 2)

        @pl.when(phase == 0)
        def _():
            x = x_ref[...]
            r = lax.rsqrt(jnp.mean(x * x, axis=-1, keepdims=True) + EPS)
            hf = x * r * g_ref[...]
            h_all[i] = hf.astype(BF16)
            ht_ref[...] = hf.T.astype(BF16)

        for slot in range(2):
            @pl.when(phase % 2 == slot)
            def _(slot=slot):
                proj_ref[...] = jnp.dot(h_all[i], wbuf[slot], preferred_element_type=F32)

        for chunk, (c0, _) in qkv_parts.items():
            @pl.when(order_ref[phase] == chunk)
            def _(c0=c0):
                qkv_ref[...] = proj_ref[:, c0:c0 + qkv_w].astype(BF16)

        @pl.when((phase == N_CHIP - 1) & (i == n_row - 1))
        def _():
            for a in range(n):
                for k in range(2 * (N_CHIP - 1)):
                    copy(a, k, 0, me).wait_send()
                local(a).wait()

    x_pos, y_pos = lax.axis_index("x"), lax.axis_index("y")
    order = jnp.stack([2 * x_pos + y_pos, 2 * (1 - x_pos) + y_pos, 2 * x_pos + 1 - y_pos,
                       2 * (1 - x_pos) + 1 - y_pos]).astype(jnp.int32)
    holds = [functools.reduce(jnp.logical_or, [order[p] == j for j in qkv_parts]) for p in range(N_CHIP)]
    col = [sum(jnp.where(order[p] == j, q0 // qkv_w, 0) for j, (_, q0) in qkv_parts.items()) for p in range(N_CHIP)]
    cur = col[-1]
    for p in reversed(range(N_CHIP - 1)):
        cur = jnp.where(holds[p], col[p], cur)
    seen = jnp.bool_(False)
    qrow, qcol = [], []
    for p in range(N_CHIP):
        cur = jnp.where(holds[p], col[p], cur)
        qrow.append(jnp.where(holds[p], -1, jnp.where(seen, last, 0)))
        qcol.append(cur)
        seen = seen | holds[p]
    qrow = jnp.stack(qrow).astype(jnp.int32)
    qcol = jnp.stack(qcol).astype(jnp.int32)

    outs = pl.pallas_call(
        body, name="in_proj_fwd",
        grid_spec=pltpu.PrefetchScalarGridSpec(
            num_scalar_prefetch=3, grid=(N_CHIP, n_row),
            in_specs=[pl.BlockSpec((TMF, d), lambda p, i, order, qrow, qcol: (jnp.where(p == 0, i, last), 0)),
                      pl.BlockSpec((1, d), lambda p, i, order, qrow, qcol: (0, 0))] + [ANY] * n,
            out_specs=[pl.BlockSpec((TMF, ncol), lambda p, i, order, qrow, qcol: (i, order[p])),
                       pl.BlockSpec((TMF, qkv_w),
                                    lambda p, i, order, qrow, qcol: (jnp.where(qrow[p] < 0, i, qrow[p]), qcol[p])),
                       pl.BlockSpec((d, TMF), lambda p, i, order, qrow, qcol: (0, jnp.where(p == 0, i, last)))]
            + [ANY] * n,
            scratch_shapes=[pltpu.VMEM((2, d, ncol), BF16), pltpu.VMEM((n_row, TMF, d), BF16),
                            pltpu.SemaphoreType.DMA((n, 2 * (N_CHIP - 1))), pltpu.SemaphoreType.DMA((n, 2 * (N_CHIP - 1))),
                            pltpu.SemaphoreType.DMA((n,)), pltpu.SemaphoreType.DMA]),
        out_shape=[jax.ShapeDtypeStruct((t, N_CHIP * ncol), F32), jax.ShapeDtypeStruct((t, 3 * d), BF16),
                   jax.ShapeDtypeStruct((d, t), BF16)]
        + [jax.ShapeDtypeStruct((2 * N_CHIP,) + hv.shape[1:], BF16) for hv in halves],
        compiler_params=_params(("arbitrary", "arbitrary")),
    )(order, qrow, qcol, x2d, g_in, *halves)
    return outs[0], outs[1], outs[2], outs[3:]


def _branch_a_fwd(a_pre, g_v, wm, b_t):
    t = a_pre.shape[0]
    d = g_v.shape[1]
    d3 = 3 * d
    ng, chunk, _ = wm.shape
    cw = d // ng

    per_step = CHUNKS_PER_STEP if t % (CHUNKS_PER_STEP * chunk) == 0 else 1

    def body(a_ref, gv_ref, wm_ref, bt_ref, ya_ref):
        for n in range(per_step):
            rows = slice(n * chunk, (n + 1) * chunk)
            ua = _gelu(a_ref[rows, 0:d])
            vg = _gelu(a_ref[rows, d:2 * d])
            za = a_ref[rows, 2 * d:3 * d]
            rv = lax.rsqrt(jnp.mean(vg * vg, axis=-1, keepdims=True) + EPS)
            va = (vg * rv * gv_ref[...]).astype(BF16)
            gate = ua * (za * _sigmoid(za))
            for g in range(ng):
                sl = slice(g * cw, (g + 1) * cw)
                mixed = jnp.dot(wm_ref[g], va[:, sl], preferred_element_type=F32) + bt_ref[:, g:g + 1]
                ya_ref[rows, sl] = (gate[:, sl] * mixed).astype(BF16)

    tile = per_step * chunk
    return pl.pallas_call(
        body, name="branch_a_fwd",
        grid=(t // tile,),
        in_specs=[pl.BlockSpec((tile, d3), lambda i: (i, 0)), _const((1, d)), _const(wm.shape), _const(b_t.shape)],
        out_specs=pl.BlockSpec((tile, d), lambda i: (i, 0)),
        out_shape=jax.ShapeDtypeStruct((t, d), BF16),
        compiler_params=_params(("arbitrary",)),
    )(a_pre, g_v, wm, b_t)


def _branch_a_bwd(a_pre, dya, g_v, wm, wm_t, b_t):
    t = a_pre.shape[0]
    d = g_v.shape[1]
    d3 = 3 * d
    ng, chunk, _ = wm.shape
    cw = d // ng
    nsteps = t // chunk

    def body(a_ref, dya_ref, gv_ref, wm_ref, wmt_ref, bt_ref, da_ref, gws_ref, gbt_ref, gnv_ref, db_acc):
        i = pl.program_id(0)

        @pl.when(i == 0)
        def _():
            gws_ref[...] = jnp.zeros_like(gws_ref)
            gnv_ref[...] = jnp.zeros_like(gnv_ref)
            db_acc[...] = jnp.zeros_like(db_acc)

        ua, dgelu_u = _gelu_and_grad(a_ref[:, 0:d])
        vg, dgelu_v = _gelu_and_grad(a_ref[:, d:2 * d])
        za = a_ref[:, 2 * d:3 * d]
        sig = _sigmoid(za)
        sz = za * sig
        dsz = sig * (1.0 + za * (1.0 - sig))
        rv = lax.rsqrt(jnp.mean(vg * vg, axis=-1, keepdims=True) + EPS)
        nv = vg * rv
        gv = gv_ref[...]
        va = (nv * gv).astype(BF16)
        dya = dya_ref[...]
        dmix = dya * ua * sz
        db_acc[...] += dmix
        dmix_b = dmix.astype(BF16)
        t_gate = dya * sz
        t_z = dya * ua * dsz
        dva_parts = []
        for g in range(ng):
            sl = slice(g * cw, (g + 1) * cw)
            mixed = jnp.dot(wm_ref[g], va[:, sl], preferred_element_type=F32) + bt_ref[:, g:g + 1]
            da_ref[:, sl] = (t_gate[:, sl] * mixed * dgelu_u[:, sl]).astype(BF16)
            da_ref[:, 2 * d + g * cw:2 * d + (g + 1) * cw] = (t_z[:, sl] * mixed).astype(BF16)
            gws_ref[g] += lax.dot_general(dmix_b[:, sl], va[:, sl], NT, preferred_element_type=F32)
            dva_parts.append(jnp.dot(wmt_ref[g], dmix_b[:, sl], preferred_element_type=F32))
        dva = jnp.concatenate(dva_parts, axis=1)
        gnv_ref[...] += jnp.sum(dva * nv, axis=0, keepdims=True)
        dnv = dva * gv
        dvg = rv * (dnv - nv * jnp.mean(dnv * nv, axis=-1, keepdims=True))
        da_ref[:, d:2 * d] = (dvg * dgelu_v).astype(BF16)

        @pl.when(i == nsteps - 1)
        def _():
            acc = db_acc[...]
            for g in range(ng):
                gbt_ref[:, g:g + 1] = jnp.sum(acc[:, g * cw:(g + 1) * cw], axis=1, keepdims=True)

    return pl.pallas_call(
        body, name="branch_a_bwd",
        grid=(nsteps,),
        in_specs=[pl.BlockSpec((chunk, d3), lambda i: (i, 0)), pl.BlockSpec((chunk, d), lambda i: (i, 0)),
                  _const((1, d)), _const(wm.shape), _const(wm_t.shape), _const(b_t.shape)],
        out_specs=[pl.BlockSpec((chunk, d3), lambda i: (i, 0)), _const(wm.shape), _const(b_t.shape), _const((1, d))],
        out_shape=[jax.ShapeDtypeStruct((t, d3), BF16), jax.ShapeDtypeStruct(wm.shape, F32),
                   jax.ShapeDtypeStruct(b_t.shape, F32), jax.ShapeDtypeStruct((1, d), F32)],
        scratch_shapes=[pltpu.VMEM((chunk, d), F32)],
        compiler_params=_params(("arbitrary",)),
    )(a_pre, dya, g_v, wm, wm_t, b_t)


def _tri(n, rows_gt_cols):
    r = lax.broadcasted_iota(jnp.int32, (n, n), 0)
    c = lax.broadcasted_iota(jnp.int32, (n, n), 1)
    return (r > c) if rows_gt_cols else (r < c)


def _twice(tri):
    t = tri.astype(BF16)
    return jnp.concatenate([t, t], axis=0)


def _cumsum_mm(a, tri2):
    hi, lo = _split_bf16(a)
    return jnp.dot(jnp.concatenate([hi, lo], axis=1), tri2, preferred_element_type=F32)


LOG2E = 1.4426950408889634
_SIGN = 0x80000000


def _sb_block(q, k, scale, upper2, causal):
    z2 = lax.dot_general(q, k, NT, preferred_element_type=F32) * (scale * LOG2E)
    neg_abs = lax.bitcast_convert_type(lax.bitcast_convert_type(z2, jnp.uint32) | jnp.uint32(_SIGN), F32)
    l2 = jnp.log(1.0 + jnp.exp2(neg_abs)) * LOG2E
    log_beta = jnp.minimum(z2, 0.0) - l2
    lom = log_beta - z2
    if causal is not None:
        lom = jnp.where(causal, lom, 0.0)
    sx = _cumsum_mm(lom, upper2)
    return log_beta, sx, sx[:, 0:1] + lom[:, 0:1]


DEAD_LOG2 = -150.0


def _max_carry(carries):
    return jnp.max(functools.reduce(jnp.maximum, carries))


ZB_GROUP, GA_GROUP, GB_GROUP = 6, 7, 8


def _attn_specs(d, seq, nq):
    hp_w = ATT_HP * (d // HEADS)
    n_hp = d // hp_w
    row_blk = lambda group: pl.BlockSpec((ATT_T, hp_w), lambda b, h, i: (b * nq + i, group * n_hp + h))
    seq_blk = lambda group: pl.BlockSpec((seq, hp_w), lambda b, h, i: (b, group * n_hp + h))
    return row_blk, seq_blk, n_hp


def _attn_fwd(qkv, proj, bsz, seq):
    t, d3 = qkv.shape
    d = d3 // 3
    hd = d // HEADS
    nq = seq // ATT_T
    scale = hd ** -0.5
    row_blk, seq_blk, n_hp = _attn_specs(d, seq, nq)

    def body(q_ref, k_ref, v_ref, zb_ref, o_ref, yb_ref):
        i = pl.program_id(2)
        causal = _tri(ATT_T, True)
        upper2 = _twice(causal)

        def step(kb, state, mask):
            rows = pl.ds(pl.multiple_of(kb * ATT_T, ATT_T), ATT_T)
            heads = [slice(h * hd, (h + 1) * hd) for h in range(ATT_HP)]
            scores = [_sb_block(q_ref[:, cols], k_ref[rows, cols], scale, upper2, mask) for cols in heads]
            new = []
            for cols, (carry, acc), (log_beta, sx, total) in zip(heads, state, scores):
                a = jnp.exp2(log_beta + sx + carry)
                if mask is not None:
                    a = jnp.where(mask, a, 0.0)
                acc = acc + jnp.dot(a.astype(BF16), v_ref[rows, cols], preferred_element_type=F32)
                new.append((carry + total, acc))
            return tuple(new)

        init = tuple((jnp.zeros((ATT_T, 1), F32), jnp.zeros((ATT_T, hd), F32)) for _ in range(ATT_HP))
        state = step(i, init, causal)
        def more(c):
            new = step(c[0], c[1], None)
            return c[0] - 1, new, _max_carry([s[0] for s in new])

        _, state, _ = lax.while_loop(lambda c: (c[0] >= 0) & (c[2] > DEAD_LOG2), more,
                                     (i - 1, state, _max_carry([s[0] for s in state])))
        for h in range(ATT_HP):
            cols = slice(h * hd, (h + 1) * hd)
            acc = state[h][1]
            zb = zb_ref[:, cols]
            o_ref[:, cols] = acc
            yb_ref[:, cols] = (acc * (zb * _sigmoid(zb))).astype(BF16)

    return pl.pallas_call(
        body, name="attn_fwd",
        grid=(bsz, n_hp, nq),
        in_specs=[row_blk(0), seq_blk(1), seq_blk(2), row_blk(ZB_GROUP)],
        out_specs=[row_blk(0), row_blk(0)],
        out_shape=[jax.ShapeDtypeStruct((t, d), F32), jax.ShapeDtypeStruct((t, d), BF16)],
        compiler_params=_params(("arbitrary", "arbitrary", "arbitrary")),
    )(qkv, qkv, qkv, proj)


def _attn_bwd(qkv, proj, o, dyb, bsz, seq):
    t, d3 = qkv.shape
    d = d3 // 3
    hd = d // HEADS
    nq = seq // ATT_T
    scale = hd ** -0.5
    row_blk, seq_blk, n_hp = _attn_specs(d, seq, nq)

    def body(q_ref, k_ref, v_ref, zb_ref, o_ref, dyb_ref, dq_ref, dk_ref, dv_ref, dzb_ref,
             g_s, beta_s, dkt_acc, dvt_acc):
        i = pl.program_id(2)

        @pl.when(i == 0)
        def _():
            dkt_acc[...] = jnp.zeros_like(dkt_acc)
            dvt_acc[...] = jnp.zeros_like(dvt_acc)

        causal = _tri(ATT_T, True)
        upper2 = _twice(causal)
        lower2 = _twice(~causal)
        zb = zb_ref[...]
        sig = _sigmoid(zb)
        dyb_t = dyb_ref[...]
        do_f = dyb_t * (zb * sig)
        do = do_f.astype(BF16)
        do_t = do_f.T.astype(BF16)
        q_t = q_ref[...].astype(F32).T.astype(BF16)
        dzb_ref[...] = (dyb_t * o_ref[...] * (sig * (1.0 + zb * (1.0 - sig)))).astype(BF16)

        def sweep(kb, carries, mask):
            rows = pl.ds(pl.multiple_of(kb * ATT_T, ATT_T), ATT_T)
            heads = [slice(h * hd, (h + 1) * hd) for h in range(ATT_HP)]
            scores = [_sb_block(q_ref[:, cols], k_ref[rows, cols], scale, upper2, mask) for cols in heads]
            das = [lax.dot_general(do[:, cols], v_ref[rows, cols], NT, preferred_element_type=F32) for cols in heads]
            new = []
            for h, (cols, carry, (log_beta, sx, total), da) in enumerate(zip(heads, carries, scores, das)):
                a = jnp.exp2(log_beta + sx + carry)
                beta = jnp.exp2(log_beta)
                if mask is not None:
                    a = jnp.where(mask, a, 0.0)
                    beta = jnp.where(mask, beta, 0.0)
                g_s[h, kb] = a * da
                beta_s[h, kb] = beta
                dvt_acc[kb, cols, :] += jnp.dot(do_t[cols, :], a.astype(BF16), preferred_element_type=F32)
                new.append(carry + total)
            return tuple(new)

        carries = sweep(i, tuple(jnp.zeros((ATT_T, 1), F32) for _ in range(ATT_HP)), causal)

        def more(c):
            new = sweep(c[0], c[1], None)
            return c[0] - 1, new, _max_carry(new)

        last, _, _ = lax.while_loop(lambda c: (c[0] >= 0) & (c[2] > DEAD_LOG2), more, (i - 1, carries, _max_carry(carries)))
        first_kb = last + 1

        def back(kb, state):
            rows = pl.ds(pl.multiple_of(kb * ATT_T, ATT_T), ATT_T)
            heads = [slice(h * hd, (h + 1) * hd) for h in range(ATT_HP)]
            sums = [_cumsum_mm(g_s[h, kb], lower2) for h in range(ATT_HP)]
            new = []
            for h, (cols, (p_carry, dq), px) in enumerate(zip(heads, state, sums)):
                dz = ((g_s[h, kb] - (p_carry + px) * beta_s[h, kb]) * scale).astype(BF16)
                dq = dq + jnp.dot(dz, k_ref[rows, cols], preferred_element_type=F32)
                dkt_acc[kb, cols, :] += jnp.dot(q_t[cols, :], dz, preferred_element_type=F32)
                new.append((p_carry + px[:, ATT_T - 1:ATT_T], dq))
            return tuple(new)

        init = tuple((jnp.zeros((ATT_T, 1), F32), jnp.zeros((ATT_T, hd), F32)) for _ in range(ATT_HP))
        state = lax.fori_loop(first_kb, i + 1, back, init)
        for h in range(ATT_HP):
            dq_ref[:, h * hd:(h + 1) * hd] = state[h][1].astype(BF16)

        @pl.when(i == nq - 1)
        def _():
            for kb in range(nq):
                dk_ref[kb * ATT_T:(kb + 1) * ATT_T, :] = dkt_acc[kb].T.astype(BF16)
                dv_ref[kb * ATT_T:(kb + 1) * ATT_T, :] = dvt_acc[kb].T.astype(BF16)

    out = jax.ShapeDtypeStruct((t, d), BF16)
    hp_w = ATT_HP * hd
    return pl.pallas_call(
        body, name="attn_bwd",
        grid=(bsz, n_hp, nq),
        in_specs=[row_blk(0), seq_blk(1), seq_blk(2), row_blk(ZB_GROUP), row_blk(0), row_blk(0)],
        out_specs=[row_blk(0), seq_blk(0), seq_blk(0), row_blk(0)],
        out_shape=[out, out, out, out],
        scratch_shapes=[pltpu.VMEM((ATT_HP, nq, ATT_T, ATT_T), F32), pltpu.VMEM((ATT_HP, nq, ATT_T, ATT_T), F32),
                        pltpu.VMEM((nq, hp_w, ATT_T), F32), pltpu.VMEM((nq, hp_w, ATT_T), F32)],
        compiler_params=_params(("arbitrary", "arbitrary", "arbitrary")),
    )(qkv, qkv, qkv, proj, o, dyb)


def _out_proj(ya, yb, g_pre, x2d, tgt, w_og, w_osb, w_out, g_f):
    t, d = x2d.shape

    def body(ya_ref, yb_ref, ga_ref, gb_ref, x_ref, tgt_ref, wog_ref, wosb_ref, wout_ref, gf_ref,
             dya_ref, dyb_ref, dg_ref, dx2_ref, loss_ref, gnf_ref, gwog_ref, gwosb_ref, gwout_ref):
        @pl.when(pl.program_id(0) == 0)
        def _():
            loss_ref[...] = jnp.zeros_like(loss_ref)
            gnf_ref[...] = jnp.zeros_like(gnf_ref)
            gwog_ref[...] = jnp.zeros_like(gwog_ref)
            gwosb_ref[...] = jnp.zeros_like(gwosb_ref)
            gwout_ref[...] = jnp.zeros_like(gwout_ref)

        ya = ya_ref[...]
        yb = yb_ref[...]
        pa = jnp.dot(ya, wog_ref[...], preferred_element_type=F32)
        pb = jnp.dot(yb, wosb_ref[...], preferred_element_type=F32)
        sga = _sigmoid(ga_ref[...])
        sgb = _sigmoid(gb_ref[...])
        merged = (sga * pa + sgb * pb).astype(BF16)
        x2 = x_ref[...] + jnp.dot(merged, wout_ref[...], preferred_element_type=F32)
        r2 = lax.rsqrt(jnp.mean(x2 * x2, axis=-1, keepdims=True) + EPS)
        n2 = x2 * r2
        gf = gf_ref[...]
        err = n2 * gf - tgt_ref[...]
        loss_ref[...] += 0.5 * jnp.sum(jnp.sum(err * err, axis=-1, keepdims=True), axis=0, keepdims=True) / d
        dy = err * (1.0 / d)
        gnf_ref[...] += jnp.sum(dy * n2, axis=0, keepdims=True)
        dn = dy * gf
        dx2 = r2 * (dn - n2 * jnp.mean(dn * n2, axis=-1, keepdims=True))
        dx2_ref[...] = dx2
        dx2_b = dx2.astype(BF16)
        dmerged = lax.dot_general(dx2_b, wout_ref[...], NT, preferred_element_type=F32)
        gwout_ref[...] += lax.dot_general(merged, dx2_b, TN, preferred_element_type=F32)
        dg_ref[:, 0:d] = (dmerged * pa * (sga * (1.0 - sga))).astype(BF16)
        dg_ref[:, d:2 * d] = (dmerged * pb * (sgb * (1.0 - sgb))).astype(BF16)
        dpa = (dmerged * sga).astype(BF16)
        dpb = (dmerged * sgb).astype(BF16)
        dya_ref[...] = lax.dot_general(dpa, wog_ref[...], NT, preferred_element_type=F32)
        dyb_ref[...] = lax.dot_general(dpb, wosb_ref[...], NT, preferred_element_type=F32)
        gwog_ref[...] += lax.dot_general(ya, dpa, TN, preferred_element_type=F32)
        gwosb_ref[...] += lax.dot_general(yb, dpb, TN, preferred_element_type=F32)

    row = lambda i: (i, 0)
    return pl.pallas_call(
        body, name="out_proj",
        grid=(t // TM,),
        in_specs=[pl.BlockSpec((TM, d), row), pl.BlockSpec((TM, d), row),
                  pl.BlockSpec((TM, d), lambda i: (i, GA_GROUP)), pl.BlockSpec((TM, d), lambda i: (i, GB_GROUP)),
                  pl.BlockSpec((TM, d), row), pl.BlockSpec((TM, d), row),
                  _resident((d, d)), _resident((d, d)), _resident((d, d)), _const((1, d))],
        out_specs=[pl.BlockSpec((TM, d), row), pl.BlockSpec((TM, d), row), pl.BlockSpec((TM, 2 * d), row),
                   pl.BlockSpec((TM, d), row), _const((1, 1)), _const((1, d)),
                   _const((d, d)), _const((d, d)), _const((d, d))],
        out_shape=[jax.ShapeDtypeStruct((t, d), F32), jax.ShapeDtypeStruct((t, d), F32),
                   jax.ShapeDtypeStruct((t, 2 * d), BF16), jax.ShapeDtypeStruct((t, d), F32),
                   jax.ShapeDtypeStruct((1, 1), F32), jax.ShapeDtypeStruct((1, d), F32),
                   jax.ShapeDtypeStruct((d, d), F32), jax.ShapeDtypeStruct((d, d), F32),
                   jax.ShapeDtypeStruct((d, d), F32)],
        compiler_params=_params(("arbitrary",)),
    )(ya, yb, g_pre, g_pre, x2d, tgt, w_og, w_osb, w_out, g_f)


def _dproj_pieces(d):
    return [(0, 0, 3), (1, 3, 1), (2, 4, 1), (3, 5, 1), (4, 6, 1), (5, 7, 2)]


def _in_proj_bwd_x(pieces, wg, x2d, g_in, dx2):
    t, d = x2d.shape
    ncol = wg.shape[2]
    segs = _segments(d, ncol)
    layout = _dproj_pieces(d)

    def body(da_ref, dq_ref, dk_ref, dv_ref, dzb_ref, dg_ref, w_ref, x_ref, g_ref, dx2_ref, gx_ref, gn_ref):
        @pl.when(pl.program_id(0) == 0)
        def _():
            gn_ref[...] = jnp.zeros_like(gn_ref)

        refs = (da_ref, dq_ref, dk_ref, dv_ref, dzb_ref, dg_ref)
        dh = jnp.zeros((TM, d), F32)
        for chip, c0, grp, s0, width in segs:
            piece, first, _ = next(p for p in layout if p[1] <= grp < p[1] + p[2])
            off = (grp - first) * d + s0
            dh = dh + lax.dot_general(refs[piece][:, off:off + width], w_ref[chip, :, c0:c0 + width], NT,
                                      preferred_element_type=F32)
        x = x_ref[...]
        r = lax.rsqrt(jnp.mean(x * x, axis=-1, keepdims=True) + EPS)
        n = x * r
        gn_ref[...] += jnp.sum(dh * n, axis=0, keepdims=True)
        dn = dh * g_ref[...]
        gx_ref[...] = dx2_ref[...] + r * (dn - n * jnp.mean(dn * n, axis=-1, keepdims=True))

    row = lambda i: (i, 0)
    return pl.pallas_call(
        body, name="in_proj_bwd_x",
        grid=(t // TM,),
        in_specs=[pl.BlockSpec((TM, p.shape[1]), row) for p in pieces]
        + [_resident(wg.shape), pl.BlockSpec((TM, d), row), _const((1, d)), pl.BlockSpec((TM, d), row)],
        out_specs=[pl.BlockSpec((TM, d), row), _const((1, d))],
        out_shape=[jax.ShapeDtypeStruct((t, d), F32), jax.ShapeDtypeStruct((1, d), F32)],
        compiler_params=_params(("arbitrary",)),
    )(*pieces, wg, x2d, g_in, dx2)


def _group_order(d, ncol):
    def rank(chip, g):
        inside = min((chip + 1) * ncol, (g + 1) * d) - max(chip * ncol, g * d)
        return 0 if inside <= 0 else 2 if inside == d else 1
    return [sorted(range(N_SPLIT), key=functools.partial(rank, chip)) for chip in range(N_CHIP)]


def _in_proj_bwd_w(h_t, pieces, mats16, pack, ncol):
    d, t = h_t.shape
    nk = t // TKW
    layout = _dproj_pieces(d)
    segs = _segments(d, ncol)
    n_mats = len(mats16)
    n_dev = 2 * N_CHIP
    flips = [(dx, dy, dc) for dx in (0, 1) for dy in (0, 1) for dc in (0, 1)][1:]

    def body(order_ref, ht_ref, da_ref, dq_ref, dk_ref, dv_ref, dzb_ref, dg_ref, *rest):
        mat_refs, pack_ref = rest[:n_mats], rest[n_mats]
        gw_ref = rest[n_mats + 1]
        recv_refs, slots_ref, recv_w = rest[n_mats + 2:2 * n_mats + 2], rest[2 * n_mats + 2], rest[2 * n_mats + 3]
        acc, stage, mat_send, mat_recv, pack_send, pack_recv, own_sem, seg_send, w_recv = rest[2 * n_mats + 4:]
        s = pl.program_id(0)
        i = pl.program_id(1)
        group = order_ref[s]
        x_pos, y_pos, c_pos, chips = _place()
        me = 4 * x_pos + 2 * y_pos + c_pos

        def for_shares(which, parity, action):
            for g in range(N_SPLIT):
                for n_seg, (chunk, c0, _, s0, width) in enumerate([sg for sg in segs if sg[2] == g]):
                    for k, (px, py) in enumerate(chips):
                        @pl.when((which == g) & (2 * px + py == chunk))
                        def _(n_seg=n_seg, c0=c0, s0=s0, width=width, k=k, px=px, py=py):
                            action(pltpu.make_async_remote_copy(
                                src_ref=stage.at[parity, :, s0:s0 + width], dst_ref=recv_w.at[k, :, c0:c0 + width],
                                send_sem=seg_send.at[parity, n_seg], recv_sem=w_recv.at[k],
                                device_id=(px, py, c_pos), device_id_type=MESH))

        def exchanges():
            cps = []
            for k, (px, py) in enumerate(chips):
                for a in range(n_mats):
                    cps.append(pltpu.make_async_remote_copy(
                        src_ref=mat_refs[a].at[2 * px + py], dst_ref=recv_refs[a].at[k],
                        send_sem=mat_send.at[a, k], recv_sem=mat_recv.at[a, k],
                        device_id=(px, py, c_pos), device_id_type=MESH))
            for k, (dx, dy, dc) in enumerate(flips):
                peer = (1 - x_pos if dx else x_pos, 1 - y_pos if dy else y_pos, 1 - c_pos if dc else c_pos)
                cps.append(pltpu.make_async_remote_copy(
                    src_ref=pack_ref, dst_ref=slots_ref.at[me], send_sem=pack_send.at[k], recv_sem=pack_recv.at[k],
                    device_id=peer, device_id_type=MESH))
            return cps, pltpu.make_async_copy(pack_ref, slots_ref.at[me], own_sem)

        @pl.when((s == 0) & (i == 0))
        def _():
            cps, own = exchanges()
            own.start()
            for cp in cps:
                cp.start()

        @pl.when(i == 0)
        def _():
            acc[...] = jnp.zeros_like(acc)

        refs = (da_ref, dq_ref, dk_ref, dv_ref, dzb_ref, dg_ref)
        for piece, first, count in layout:
            @pl.when((group >= first) & (group < first + count))
            def _(piece=piece):
                acc[...] += jnp.dot(ht_ref[...], refs[piece][...], preferred_element_type=F32)

        @pl.when(i == nk - 1)
        def _():
            gw_ref[...] = acc[...]
            for parity in range(2):
                @pl.when(s % 2 == parity)
                def _(parity=parity):
                    @pl.when(s >= 2)
                    def _():
                        for_shares(order_ref[s - 2], parity, lambda cp: cp.wait_send())
                    stage[parity] = acc[...].astype(BF16)
                    for_shares(group, parity, lambda cp: cp.start())

        @pl.when((s == N_SPLIT - 1) & (i == nk - 1))
        def _():
            for_shares(order_ref[N_SPLIT - 2], (N_SPLIT - 2) % 2, lambda cp: cp.wait_send())
            for_shares(group, (N_SPLIT - 1) % 2, lambda cp: cp.wait_send())
            for k in range(N_CHIP - 1):
                pltpu.make_async_remote_copy(
                    src_ref=recv_w.at[k], dst_ref=recv_w.at[k], send_sem=seg_send.at[0, 0], recv_sem=w_recv.at[k],
                    device_id=(x_pos, y_pos, c_pos), device_id_type=MESH).wait_recv()
            cps, own = exchanges()
            own.wait()
            for cp in cps:
                cp.wait()

    def piece_spec(p, first, count):
        def index(s, i, order):
            mine = (order[s] >= first) & (order[s] < first + count)
            return jnp.where(mine, i, 0), jnp.where(mine, order[s] - first, 0)
        return pl.BlockSpec((TKW, d), index)

    chip = 2 * lax.axis_index("x") + lax.axis_index("y")
    order = jnp.asarray(_group_order(d, ncol), jnp.int32)[chip]
    outs = pl.pallas_call(
        body, name="in_proj_bwd_w",
        grid_spec=pltpu.PrefetchScalarGridSpec(
            num_scalar_prefetch=1, grid=(N_SPLIT, nk),
            in_specs=[pl.BlockSpec((d, TKW), lambda s, i, order: (0, i))] + [piece_spec(*p) for p in layout]
            + [ANY] * (n_mats + 1),
            out_specs=[pl.BlockSpec((d, d), lambda s, i, order: (0, order[s]))] + [ANY] * (n_mats + 2),
            scratch_shapes=[pltpu.VMEM((d, d), F32), pltpu.VMEM((2, d, d), BF16),
                            pltpu.SemaphoreType.DMA((n_mats, N_CHIP - 1)), pltpu.SemaphoreType.DMA((n_mats, N_CHIP - 1)),
                            pltpu.SemaphoreType.DMA((n_dev - 1,)), pltpu.SemaphoreType.DMA((n_dev - 1,)),
                            pltpu.SemaphoreType.DMA,
                            pltpu.SemaphoreType.DMA((2, 2)), pltpu.SemaphoreType.DMA((N_CHIP - 1,))]),
        out_shape=[jax.ShapeDtypeStruct((d, N_SPLIT * d), F32)]
        + [jax.ShapeDtypeStruct((N_CHIP - 1,) + m.shape[1:], BF16) for m in mats16]
        + [jax.ShapeDtypeStruct((n_dev,) + pack.shape, F32), jax.ShapeDtypeStruct((N_CHIP - 1, d, ncol), BF16)],
        compiler_params=_params(("arbitrary", "arbitrary")),
    )(order, h_t, *pieces, *mats16, pack)
    return outs[0], outs[1:1 + n_mats], outs[1 + n_mats], outs[2 + n_mats]


def _local_step(proj, qkv, x2d, tgt2d, bsz, seq, norm_v, w_s, b_s, w_og, w_osb, w_out, norm_final):
    d = x2d.shape[1]
    chunk = w_s.shape[-1]
    causal = jnp.tril(jnp.ones((chunk, chunk), dtype=bool))
    wm = jnp.where(causal[None], w_s, 0.0).astype(BF16)
    wm_t = jnp.swapaxes(wm, 1, 2)
    b_t = b_s.T

    ya = _branch_a_fwd(proj, norm_v, wm, b_t)
    o, yb = _attn_fwd(qkv, proj, bsz, seq)
    dya, dyb, dg, dx2, loss, g_nf, g_wog, g_wosb, g_wout = _out_proj(
        ya, yb, proj, x2d, tgt2d, w_og, w_osb, w_out, norm_final.reshape(1, d))
    dq, dk, dv, dzb = _attn_bwd(qkv, proj, o, dyb, bsz, seq)
    d_a, g_ws, g_bt, g_nv = _branch_a_bwd(proj, dya, norm_v, wm, wm_t, b_t)
    g_ws = jnp.where(causal[None], g_ws, 0.0)
    return loss, (d_a, dq, dk, dv, dzb, dg), dx2, g_nv, g_ws, g_bt.T, g_wog, g_wosb, g_wout, g_nf


def _row_tile(rows):
    return next(r for r in (128, 64, 32, 16, 8) if rows % r == 0)


def _cast_bf16(arrs):
    n = len(arrs)

    def body(*refs):
        for a_ref, o_ref in zip(refs[:n], refs[n:]):
            o_ref[...] = a_ref[...].astype(BF16)

    specs = [pl.BlockSpec((a.shape[0] // CAST_STEPS, a.shape[1]), lambda i: (i, 0)) for a in arrs]
    return pl.pallas_call(
        body, name="cast_bf16", grid=(CAST_STEPS,),
        in_specs=specs, out_specs=specs,
        out_shape=[jax.ShapeDtypeStruct(a.shape, BF16) for a in arrs],
        compiler_params=_params(("arbitrary",)),
    )(*arrs)


def _add_received(full, recv, chip, by_cols):
    _, rows, cols = recv.shape
    tr = _row_tile(rows)
    nb = rows // tr

    def body(chip_ref, own_ref, recv_ref, o_ref):
        s = own_ref[...]
        for k in range(N_CHIP - 1):
            s = s + recv_ref[k].astype(F32)
        o_ref[...] = s

    own_map = (lambda i, chip_ref: (i, chip_ref[0])) if by_cols else (lambda i, chip_ref: (chip_ref[0] * nb + i, 0))
    return pl.pallas_call(
        body, name="add_received",
        grid_spec=pltpu.PrefetchScalarGridSpec(
            num_scalar_prefetch=1, grid=(nb,),
            in_specs=[pl.BlockSpec((tr, cols), own_map),
                      pl.BlockSpec((N_CHIP - 1, tr, cols), lambda i, chip_ref: (0, i, 0))],
            out_specs=pl.BlockSpec((tr, cols), lambda i, chip_ref: (i, 0))),
        out_shape=jax.ShapeDtypeStruct((rows, cols), F32),
        compiler_params=_params(("arbitrary",)),
    )(chip.reshape(1).astype(jnp.int32), full, recv)


def _adamw_math(w, m, v, g):
    new_m = ADAM_B1 * m + (1.0 - ADAM_B1) * g
    new_v = ADAM_B2 * v + (1.0 - ADAM_B2) * (g * g)
    m_hat = new_m / (1.0 - ADAM_B1 ** ADAM_STEP)
    v_hat = new_v / (1.0 - ADAM_B2 ** ADAM_STEP)
    return -ADAM_LR * (m_hat / (jnp.sqrt(v_hat) + ADAM_EPS) + ADAM_WD * w), new_m, new_v


def _adamw(w, m, v, g_parts):
    rows, cols = w.shape
    tr = _row_tile(rows)
    n_parts = len(g_parts)

    def body(*refs):
        w_ref, m_ref, v_ref = refs[:3]
        part_refs = refs[3:3 + n_parts]
        g_ref, d_ref, nm_ref, nv_ref = refs[3 + n_parts:]
        g = part_refs[0][...]
        for p in part_refs[1:]:
            g = g + p[...]
        g_ref[...] = g
        d_ref[...], nm_ref[...], nv_ref[...] = _adamw_math(w_ref[...], m_ref[...], v_ref[...], g)

    spec = pl.BlockSpec((tr, cols), lambda i: (i, 0))
    out = jax.ShapeDtypeStruct(w.shape, F32)
    return pl.pallas_call(
        body, name="adamw", grid=(rows // tr,),
        in_specs=[spec] * (3 + n_parts), out_specs=[spec] * 4, out_shape=[out] * 4,
        compiler_params=_params(("arbitrary",)),
    )(w, m, v, *g_parts)


def _adamw_small(w, m, v, slots):
    n_dev = slots.shape[0]

    def body(w_ref, m_ref, v_ref, slots_ref, g_ref, d_ref, nm_ref, nv_ref):
        g = slots_ref[0]
        for i in range(1, n_dev):
            g = g + slots_ref[i]
        g_ref[...] = g
        d_ref[...], nm_ref[...], nv_ref[...] = _adamw_math(w_ref[...], m_ref[...], v_ref[...], g)

    vmem = pl.BlockSpec(memory_space=pltpu.VMEM)
    out = jax.ShapeDtypeStruct(w.shape, F32)
    return pl.pallas_call(
        body, name="adamw_small", in_specs=[vmem] * 4, out_specs=[vmem] * 4, out_shape=[out] * 4,
        compiler_params=pltpu.CompilerParams(vmem_limit_bytes=VMEM_LIMIT),
    )(w, m, v, slots)


ANY = pl.BlockSpec(memory_space=pl.ANY)


def _place():
    x, y, c = lax.axis_index("x"), lax.axis_index("y"), lax.axis_index("c")
    other_chips = [(1 - x, y), (x, 1 - y), (1 - x, 1 - y)]
    return x, y, c, other_chips


def _swap_with_sibling(arrs):
    n = len(arrs)

    def body(*refs):
        ins, outs = refs[:n], refs[n:2 * n]
        send_sems, recv_sems = refs[2 * n:]
        x, y, c, _ = _place()
        copies = [pltpu.make_async_remote_copy(
            src_ref=ins[a], dst_ref=outs[a], send_sem=send_sems.at[a], recv_sem=recv_sems.at[a],
            device_id=(x, y, 1 - c), device_id_type=MESH) for a in range(n)]
        for cp in copies:
            cp.start()
        for cp in copies:
            cp.wait()

    return pl.pallas_call(
        body, name="swap_with_sibling",
        in_specs=[ANY] * n, out_specs=[ANY] * n,
        out_shape=[jax.ShapeDtypeStruct(a.shape, a.dtype) for a in arrs],
        scratch_shapes=[pltpu.SemaphoreType.DMA((n,)), pltpu.SemaphoreType.DMA((n,))],
    )(*arrs)


SLAB = 8


def _slab(vec, d):
    return jnp.pad(vec.reshape(1, d), ((0, SLAB - 1), (0, 0)))


def _pack_tail(vec_nv, b_s, vec_nf, w_s, scalar=None):
    d = vec_nv.shape[-1]
    extra = jnp.zeros((1, d), F32) if scalar is None else jnp.pad(scalar, ((0, 0), (0, d - 1)))
    return jnp.concatenate([_slab(v, d) for v in (vec_nv, b_s, vec_nf, extra)] + [w_s.reshape(-1, d)], axis=0)


def _pack_small(vec_nin, vec_nv, b_s, vec_nf, w_s, scalar=None):
    return jnp.concatenate([_slab(vec_nin, vec_nin.shape[-1]), _pack_tail(vec_nv, b_s, vec_nf, w_s, scalar)], axis=0)


def _unpack_small(pack, w_s_shape, b_s_shape):
    return (pack[0:1], pack[SLAB:SLAB + 1], pack[2 * SLAB].reshape(b_s_shape), pack[3 * SLAB],
            pack[5 * SLAB:].reshape(w_s_shape), pack[4 * SLAB, 0])


def kernel(x, norm_in, w_in, norm_v, w_s, b_s, w_o_gmlp, w_o_sb, w_out, norm_final, loss_target, m_norm_in, m_w_in, m_norm_v, m_w_s, m_b_s, m_w_o_gmlp, m_w_o_sb, m_w_out, m_norm_final, v_norm_in, v_w_in, v_norm_v, v_w_s, v_b_s, v_w_o_gmlp, v_w_o_sb, v_w_out, v_norm_final):
    d = x.shape[-1]
    ncol = w_in.shape[-1]
    nrow = w_o_gmlp.shape[-2]
    chip = 2 * lax.axis_index("x") + lax.axis_index("y")

    bsz, seq, _ = x.shape
    x2d = x.reshape(bsz * seq, d)
    shards = [w_in[0], w_o_gmlp[0], w_o_sb[0], w_out[0]]
    halves = [s16.reshape(2, s16.shape[0] // 2, s16.shape[1]) for s16 in _cast_bf16(shards)]
    proj, qkv, h_t, (wg, w_og, w_osb, w_o) = _in_proj_fwd(x2d, norm_in, halves)
    wg = wg.reshape(N_CHIP, d, ncol)

    loss, pieces, dx2, g_nv, g_ws, g_bs, g_wog, g_wosb, g_wout, g_nf = _local_step(
        proj, qkv, x2d, loss_target.reshape(bsz * seq, d), bsz, seq, norm_v, w_s[0], b_s[0],
        w_og.reshape(d, d), w_osb.reshape(d, d), w_o.reshape(d, d), norm_final)

    grad_x, g_nin = _in_proj_bwd_x(pieces, wg, x2d, norm_in, dx2)
    grad_x = grad_x.reshape(bsz, seq, d)
    mats = [g_wog, g_wosb, g_wout]
    mats16 = [g16.reshape(N_CHIP, nrow, d) for g16 in _cast_bf16(mats)]
    g_win, recv_mats, slots, recv_win = _in_proj_bwd_w(
        h_t, pieces, mats16, _pack_small(g_nin, g_nv, g_bs, g_nf, g_ws, loss), ncol)

    sums = [_add_received(g_win, recv_win, chip, True)] + [
        _add_received(g, r, chip, False) for g, r in zip(mats, recv_mats)]
    sibling_sums = _swap_with_sibling(sums)
    stats = []
    for w, m, v, mine, theirs in zip(shards, [m_w_in[0], m_w_o_gmlp[0], m_w_o_sb[0], m_w_out[0]],
                                     [v_w_in[0], v_w_o_gmlp[0], v_w_o_sb[0], v_w_out[0]], sums, sibling_sums):
        stats.append(_adamw(w, m, v, [mine, theirs]))
    (gw_in, dw_in, nm_in, nv_in), (gw_og, dw_og, nm_og, nv_og), (gw_osb, dw_osb, nm_osb, nv_osb), \
        (gw_out, dw_out, nm_out, nv_out) = stats

    gs, ds, ms, vs = _adamw_small(
        _pack_small(norm_in, norm_v, b_s[0], norm_final, w_s[0]),
        _pack_small(m_norm_in, m_norm_v, m_b_s[0], m_norm_final, m_w_s[0]),
        _pack_small(v_norm_in, v_norm_v, v_b_s[0], v_norm_final, v_w_s[0]), slots)

    def small(pack):
        nin, nv, bs, nf, ws, _ = _unpack_small(pack, w_s.shape, b_s.shape)
        return nin, nv, ws, bs, nf

    loss = _unpack_small(gs, w_s.shape, b_s.shape)[-1]
    out = []
    for small_pack, win, wog, wosb, wout in ((gs, gw_in, gw_og, gw_osb, gw_out), (ds, dw_in, dw_og, dw_osb, dw_out),
                                             (ms, nm_in, nm_og, nm_osb, nm_out), (vs, nv_in, nv_og, nv_osb, nv_out)):
        nin, nv, ws, bs, nf = small(small_pack)
        out += [nin, win[None], nv, ws, bs, wog[None], wosb[None], wout[None], nf]
    return (loss, grad_x, *out)
```

```python
import functools
import math

import jax
import jax.numpy as jnp
from jax import lax
from jax.experimental import pallas as pl
from jax.experimental.pallas import tpu as pltpu

F32 = jnp.float32
BF16 = jnp.bfloat16
EPS = 1e-6
HEADS = 8
N_SPLIT = 9
N_CHIP = 4
MESH = pl.DeviceIdType.MESH

ADAM_LR = 0.001
ADAM_B1 = 0.9
ADAM_B2 = 0.999
ADAM_EPS = 1e-08
ADAM_WD = 0.01
ADAM_STEP = 10

VMEM_LIMIT = 56 * 2 ** 20
TM = 256
TMF = 512
ATT_T = 256
ATT_HP = 4
TKW = 1024
CHUNKS_PER_STEP = 4
CAST_STEPS = 8
STAGE_SLOTS = 4

NT = (((1,), (1,)), ((), ()))
TN = (((0,), (0,)), ((), ()))


def _params(sem):
    return pltpu.CompilerParams(dimension_semantics=sem, vmem_limit_bytes=VMEM_LIMIT)


def _resident(shape):
    nd = len(shape)
    return pl.BlockSpec(shape, lambda *_: (0,) * nd, pipeline_mode=pl.Buffered(1))


def _const(shape):
    nd = len(shape)
    return pl.BlockSpec(shape, lambda *_: (0,) * nd)


def _segments(d, ncol):
    segs = []
    edges = sorted({j * ncol for j in range(N_CHIP + 1)} | {s * d for s in range(N_SPLIT + 1)})
    for lo, hi in zip(edges[:-1], edges[1:]):
        segs.append((lo // ncol, lo % ncol, lo // d, lo % d, hi - lo))
    return segs


def _sigmoid(x):
    return 0.5 * jnp.tanh(0.5 * x) + 0.5


_GELU_C = math.sqrt(2.0 / math.pi)


_GELU_CA = _GELU_C * 0.044715


def _gelu(x):
    return x * (0.5 * jnp.tanh(x * (_GELU_C + _GELU_CA * (x * x))) + 0.5)


def _gelu_and_grad(x):
    x2 = x * x
    u = 0.5 * jnp.tanh(x * (_GELU_C + _GELU_CA * x2)) + 0.5
    slope = (1.0 - u) * (x * (_GELU_C + (3.0 * _GELU_CA) * x2))
    return x * u, u * (2.0 * slope + 1.0)


def _split_bf16(a):
    hi = a.astype(BF16)
    lo = (a - hi.astype(F32)).astype(BF16)
    return hi, lo


def _in_proj_fwd(x2d, g_in, halves):
    t, d = x2d.shape
    n = len(halves)
    ncol = halves[0].shape[2]
    n_row = t // TMF
    last = n_row - 1
    assert halves[0].shape[1] * 2 == d
    qkv_parts = {j: (max(j * ncol, 3 * d) - j * ncol, max(j * ncol, 3 * d) - 3 * d)
                 for j in range(N_CHIP) if min((j + 1) * ncol, 6 * d) > max(j * ncol, 3 * d)}
    qkv_w = 3 * d // len(qkv_parts)
    assert all(min((j + 1) * ncol, 6 * d) - max(j * ncol, 3 * d) == qkv_w and q0 % qkv_w == 0
               for j, (_, q0) in qkv_parts.items())

    def body(order_ref, qrow_ref, qcol_ref, x_ref, g_ref, *rest):
        ins = rest[:n]
        proj_ref, qkv_ref, ht_ref = rest[n:n + 3]
        outs = rest[n + 3:2 * n + 3]
        wbuf, h_all, send_sems, recv_sems, local_sems, load_sem = rest[2 * n + 3:]
        phase = pl.program_id(0)
        i = pl.program_id(1)
        x_pos, y_pos, c_pos, chips = _place()
        sibling = (x_pos, y_pos, 1 - c_pos)
        me = (x_pos, y_pos, c_pos)
        my_chip = 2 * x_pos + y_pos

        def copy(a, k, block, to, src=None):
            return pltpu.make_async_remote_copy(
                src_ref=outs[a].at[block] if src is None else src, dst_ref=outs[a].at[block],
                send_sem=send_sems.at[a, k], recv_sem=recv_sems.at[a, k], device_id=to, device_id_type=MESH)

        def local(a):
            return pltpu.make_async_copy(ins[a], outs[a].at[pl.ds(2 * my_chip, 2)], local_sems.at[a])

        def load(src, first, slot):
            for half in range(2):
                cp = pltpu.make_async_copy(src.at[first + half], wbuf.at[slot, pl.ds(half * (d // 2), d // 2)], load_sem)
                cp.start()
                cp.wait()

        def send_mine(k):
            px, py = chips[k]
            for a in range(n):
                copy(a, k, 2 * my_chip + c_pos, (px, py, c_pos), src=ins[a].at[c_pos]).start()

        @pl.when((phase == 0) & (i == 0))
        def _():
            for a in range(n):
                local(a).start()
            send_mine(0)
            send_mine(1)
            load(ins[0], 0, 0)

        for k, (px, py) in enumerate(chips):
            @pl.when((phase == k + 1) & (i == 0))
            def _(k=k, px=px, py=py):
                theirs = 2 * (2 * px + py)
                for a in range(n):
                    copy(a, k, theirs + c_pos, me).wait_recv()
                    copy(a, 3 + k, theirs + c_pos, sibling).start()
                if k == 0:
                    send_mine(2)
                for a in range(n):
                    copy(a, 3 + k, theirs + 1 - c_pos, me).wait_recv()
                load(outs[0], theirs, (k + 1) % 2)

        @pl.when(phase == 0)
        def _():
            x = x_ref[...]
            r = lax.rsqrt(jnp.mean(x * x, axis=-1, keepdims=True) + EPS)
            hf = x * r * g_ref[...]
            h_all[i] = hf.astype(BF16)
            ht_ref[...] = hf.T.astype(BF16)

        for slot in range(2):
            @pl.when(phase % 2 == slot)
            def _(slot=slot):
                proj_ref[...] = jnp.dot(h_all[i], wbuf[slot], preferred_element_type=F32)

        for chunk, (c0, _) in qkv_parts.items():
            @pl.when(order_ref[phase] == chunk)
            def _(c0=c0):
                qkv_ref[...] = proj_ref[:, c0:c0 + qkv_w].astype(BF16)

        @pl.when((phase == N_CHIP - 1) & (i == n_row - 1))
        def _():
            for a in range(n):
                for k in range(2 * (N_CHIP - 1)):
                    copy(a, k, 0, me).wait_send()
                local(a).wait()

    x_pos, y_pos = lax.axis_index("x"), lax.axis_index("y")
    order = jnp.stack([2 * x_pos + y_pos, 2 * (1 - x_pos) + y_pos, 2 * x_pos + 1 - y_pos,
                       2 * (1 - x_pos) + 1 - y_pos]).astype(jnp.int32)
    holds = [functools.reduce(jnp.logical_or, [order[p] == j for j in qkv_parts]) for p in range(N_CHIP)]
    col = [sum(jnp.where(order[p] == j, q0 // qkv_w, 0) for j, (_, q0) in qkv_parts.items()) for p in range(N_CHIP)]
    cur = col[-1]
    for p in reversed(range(N_CHIP - 1)):
        cur = jnp.where(holds[p], col[p], cur)
    seen = jnp.bool_(False)
    qrow, qcol = [], []
    for p in range(N_CHIP):
        cur = jnp.where(holds[p], col[p], cur)
        qrow.append(jnp.where(holds[p], -1, jnp.where(seen, last, 0)))
        qcol.append(cur)
        seen = seen | holds[p]
    qrow = jnp.stack(qrow).astype(jnp.int32)
    qcol = jnp.stack(qcol).astype(jnp.int32)

    outs = pl.pallas_call(
        body, name="in_proj_fwd",
        grid_spec=pltpu.PrefetchScalarGridSpec(
            num_scalar_prefetch=3, grid=(N_CHIP, n_row),
            in_specs=[pl.BlockSpec((TMF, d), lambda p, i, order, qrow, qcol: (jnp.where(p == 0, i, last), 0)),
                      pl.BlockSpec((1, d), lambda p, i, order, qrow, qcol: (0, 0))] + [ANY] * n,
            out_specs=[pl.BlockSpec((TMF, ncol), lambda p, i, order, qrow, qcol: (i, order[p])),
                       pl.BlockSpec((TMF, qkv_w),
                                    lambda p, i, order, qrow, qcol: (jnp.where(qrow[p] < 0, i, qrow[p]), qcol[p])),
                       pl.BlockSpec((d, TMF), lambda p, i, order, qrow, qcol: (0, jnp.where(p == 0, i, last)))]
            + [ANY] * n,
            scratch_shapes=[pltpu.VMEM((2, d, ncol), BF16), pltpu.VMEM((n_row, TMF, d), BF16),
                            pltpu.SemaphoreType.DMA((n, 2 * (N_CHIP - 1))), pltpu.SemaphoreType.DMA((n, 2 * (N_CHIP - 1))),
                            pltpu.SemaphoreType.DMA((n,)), pltpu.SemaphoreType.DMA]),
        out_shape=[jax.ShapeDtypeStruct((t, N_CHIP * ncol), F32), jax.ShapeDtypeStruct((t, 3 * d), BF16),
                   jax.ShapeDtypeStruct((d, t), BF16)]
        + [jax.ShapeDtypeStruct((2 * N_CHIP,) + hv.shape[1:], BF16) for hv in halves],
        compiler_params=_params(("arbitrary", "arbitrary")),
    )(order, qrow, qcol, x2d, g_in, *halves)
    return outs[0], outs[1], outs[2], outs[3:]


def _branch_a_fwd(a_pre, g_v, wm, b_t):
    t = a_pre.shape[0]
    d = g_v.shape[1]
    d3 = 3 * d
    ng, chunk, _ = wm.shape
    cw = d // ng

    per_step = CHUNKS_PER_STEP if t % (CHUNKS_PER_STEP * chunk) == 0 else 1

    def body(a_ref, gv_ref, wm_ref, bt_ref, ya_ref):
        for n in range(per_step):
            rows = slice(n * chunk, (n + 1) * chunk)
            ua = _gelu(a_ref[rows, 0:d])
            vg = _gelu(a_ref[rows, d:2 * d])
            za = a_ref[rows, 2 * d:3 * d]
            rv = lax.rsqrt(jnp.mean(vg * vg, axis=-1, keepdims=True) + EPS)
            va = (vg * rv * gv_ref[...]).astype(BF16)
            gate = ua * (za * _sigmoid(za))
            for g in range(ng):
                sl = slice(g * cw, (g + 1) * cw)
                mixed = jnp.dot(wm_ref[g], va[:, sl], preferred_element_type=F32) + bt_ref[:, g:g + 1]
                ya_ref[rows, sl] = (gate[:, sl] * mixed).astype(BF16)

    tile = per_step * chunk
    return pl.pallas_call(
        body, name="branch_a_fwd",
        grid=(t // tile,),
        in_specs=[pl.BlockSpec((tile, d3), lambda i: (i, 0)), _const((1, d)), _const(wm.shape), _const(b_t.shape)],
        out_specs=pl.BlockSpec((tile, d), lambda i: (i, 0)),
        out_shape=jax.ShapeDtypeStruct((t, d), BF16),
        compiler_params=_params(("arbitrary",)),
    )(a_pre, g_v, wm, b_t)


def _branch_a_bwd(a_pre, dya, g_v, wm, wm_t, b_t):
    t = a_pre.shape[0]
    d = g_v.shape[1]
    d3 = 3 * d
    ng, chunk, _ = wm.shape
    cw = d // ng
    nsteps = t // chunk

    def body(a_ref, dya_ref, gv_ref, wm_ref, wmt_ref, bt_ref, da_ref, gws_ref, gbt_ref, gnv_ref, db_acc):
        i = pl.program_id(0)

        @pl.when(i == 0)
        def _():
            gws_ref[...] = jnp.zeros_like(gws_ref)
            gnv_ref[...] = jnp.zeros_like(gnv_ref)
            db_acc[...] = jnp.zeros_like(db_acc)

        ua, dgelu_u = _gelu_and_grad(a_ref[:, 0:d])
        vg, dgelu_v = _gelu_and_grad(a_ref[:, d:2 * d])
        za = a_ref[:, 2 * d:3 * d]
        sig = _sigmoid(za)
        sz = za * sig
        dsz = sig * (1.0 + za * (1.0 - sig))
        rv = lax.rsqrt(jnp.mean(vg * vg, axis=-1, keepdims=True) + EPS)
        nv = vg * rv
        gv = gv_ref[...]
        va = (nv * gv).astype(BF16)
        dya = dya_ref[...]
        dmix = dya * ua * sz
        db_acc[...] += dmix
        dmix_b = dmix.astype(BF16)
        t_gate = dya * sz
        t_z = dya * ua * dsz
        dva_parts = []
        for g in range(ng):
            sl = slice(g * cw, (g + 1) * cw)
            mixed = jnp.dot(wm_ref[g], va[:, sl], preferred_element_type=F32) + bt_ref[:, g:g + 1]
            da_ref[:, sl] = (t_gate[:, sl] * mixed * dgelu_u[:, sl]).astype(BF16)
            da_ref[:, 2 * d + g * cw:2 * d + (g + 1) * cw] = (t_z[:, sl] * mixed).astype(BF16)
            gws_ref[g] += lax.dot_general(dmix_b[:, sl], va[:, sl], NT, preferred_element_type=F32)
            dva_parts.append(jnp.dot(wmt_ref[g], dmix_b[:, sl], preferred_element_type=F32))
        dva = jnp.concatenate(dva_parts, axis=1)
        gnv_ref[...] += jnp.sum(dva * nv, axis=0, keepdims=True)
        dnv = dva * gv
        dvg = rv * (dnv - nv * jnp.mean(dnv * nv, axis=-1, keepdims=True))
        da_ref[:, d:2 * d] = (dvg * dgelu_v).astype(BF16)

        @pl.when(i == nsteps - 1)
        def _():
            acc = db_acc[...]
            for g in range(ng):
                gbt_ref[:, g:g + 1] = jnp.sum(acc[:, g * cw:(g + 1) * cw], axis=1, keepdims=True)

    return pl.pallas_call(
        body, name="branch_a_bwd",
        grid=(nsteps,),
        in_specs=[pl.BlockSpec((chunk, d3), lambda i: (i, 0)), pl.BlockSpec((chunk, d), lambda i: (i, 0)),
                  _const((1, d)), _const(wm.shape), _const(wm_t.shape), _const(b_t.shape)],
        out_specs=[pl.BlockSpec((chunk, d3), lambda i: (i, 0)), _const(wm.shape), _const(b_t.shape), _const((1, d))],
        out_shape=[jax.ShapeDtypeStruct((t, d3), BF16), jax.ShapeDtypeStruct(wm.shape, F32),
                   jax.ShapeDtypeStruct(b_t.shape, F32), jax.ShapeDtypeStruct((1, d), F32)],
        scratch_shapes=[pltpu.VMEM((chunk, d), F32)],
        compiler_params=_params(("arbitrary",)),
    )(a_pre, dya, g_v, wm, wm_t, b_t)


def _tri(n, rows_gt_cols):
    r = lax.broadcasted_iota(jnp.int32, (n, n), 0)
    c = lax.broadcasted_iota(jnp.int32, (n, n), 1)
    return (r > c) if rows_gt_cols else (r < c)


def _twice(tri):
    t = tri.astype(BF16)
    return jnp.concatenate([t, t], axis=0)


def _cumsum_mm(a, tri2):
    hi, lo = _split_bf16(a)
    return jnp.dot(jnp.concatenate([hi, lo], axis=1), tri2, preferred_element_type=F32)


LOG2E = 1.4426950408889634
_SIGN = 0x80000000


def _sb_block(q, k, scale, upper2, causal):
    z2 = lax.dot_general(q, k, NT, preferred_element_type=F32) * (scale * LOG2E)
    neg_abs = lax.bitcast_convert_type(lax.bitcast_convert_type(z2, jnp.uint32) | jnp.uint32(_SIGN), F32)
    l2 = jnp.log(1.0 + jnp.exp2(neg_abs)) * LOG2E
    log_beta = jnp.minimum(z2, 0.0) - l2
    lom = log_beta - z2
    if causal is not None:
        lom = jnp.where(causal, lom, 0.0)
    sx = _cumsum_mm(lom, upper2)
    return log_beta, sx, sx[:, 0:1] + lom[:, 0:1]


DEAD_LOG2 = -150.0


def _max_carry(carries):
    return jnp.max(functools.reduce(jnp.maximum, carries))


ZB_GROUP, GA_GROUP, GB_GROUP = 6, 7, 8


def _attn_specs(d, seq, nq):
    hp_w = ATT_HP * (d // HEADS)
    n_hp = d // hp_w
    row_blk = lambda group: pl.BlockSpec((ATT_T, hp_w), lambda b, h, i: (b * nq + i, group * n_hp + h))
    seq_blk = lambda group: pl.BlockSpec((seq, hp_w), lambda b, h, i: (b, group * n_hp + h))
    return row_blk, seq_blk, n_hp


def _attn_fwd(qkv, proj, bsz, seq):
    t, d3 = qkv.shape
    d = d3 // 3
    hd = d // HEADS
    nq = seq // ATT_T
    scale = hd ** -0.5
    row_blk, seq_blk, n_hp = _attn_specs(d, seq, nq)

    def body(q_ref, k_ref, v_ref, zb_ref, o_ref, yb_ref):
        i = pl.program_id(2)
        causal = _tri(ATT_T, True)
        upper2 = _twice(causal)

        def step(kb, state, mask):
            rows = pl.ds(pl.multiple_of(kb * ATT_T, ATT_T), ATT_T)
            heads = [slice(h * hd, (h + 1) * hd) for h in range(ATT_HP)]
            scores = [_sb_block(q_ref[:, cols], k_ref[rows, cols], scale, upper2, mask) for cols in heads]
            new = []
            for cols, (carry, acc), (log_beta, sx, total) in zip(heads, state, scores):
                a = jnp.exp2(log_beta + sx + carry)
                if mask is not None:
                    a = jnp.where(mask, a, 0.0)
                acc = acc + jnp.dot(a.astype(BF16), v_ref[rows, cols], preferred_element_type=F32)
                new.append((carry + total, acc))
            return tuple(new)

        init = tuple((jnp.zeros((ATT_T, 1), F32), jnp.zeros((ATT_T, hd), F32)) for _ in range(ATT_HP))
        state = step(i, init, causal)
        def more(c):
            new = step(c[0], c[1], None)
            return c[0] - 1, new, _max_carry([s[0] for s in new])

        _, state, _ = lax.while_loop(lambda c: (c[0] >= 0) & (c[2] > DEAD_LOG2), more,
                                     (i - 1, state, _max_carry([s[0] for s in state])))
        for h in range(ATT_HP):
            cols = slice(h * hd, (h + 1) * hd)
            acc = state[h][1]
            zb = zb_ref[:, cols]
            o_ref[:, cols] = acc
            yb_ref[:, cols] = (acc * (zb * _sigmoid(zb))).astype(BF16)

    return pl.pallas_call(
        body, name="attn_fwd",
        grid=(bsz, n_hp, nq),
        in_specs=[row_blk(0), seq_blk(1), seq_blk(2), row_blk(ZB_GROUP)],
        out_specs=[row_blk(0), row_blk(0)],
        out_shape=[jax.ShapeDtypeStruct((t, d), F32), jax.ShapeDtypeStruct((t, d), BF16)],
        compiler_params=_params(("arbitrary", "arbitrary", "arbitrary")),
    )(qkv, qkv, qkv, proj)


def _attn_bwd(qkv, proj, o, dyb, bsz, seq):
    t, d3 = qkv.shape
    d = d3 // 3
    hd = d // HEADS
    nq = seq // ATT_T
    scale = hd ** -0.5
    row_blk, seq_blk, n_hp = _attn_specs(d, seq, nq)

    def body(q_ref, k_ref, v_ref, zb_ref, o_ref, dyb_ref, dq_ref, dk_ref, dv_ref, dzb_ref,
             g_s, beta_s, dkt_acc, dvt_acc):
        i = pl.program_id(2)

        @pl.when(i == 0)
        def _():
            dkt_acc[...] = jnp.zeros_like(dkt_acc)
            dvt_acc[...] = jnp.zeros_like(dvt_acc)

        causal = _tri(ATT_T, True)
        upper2 = _twice(causal)
        lower2 = _twice(~causal)
        zb = zb_ref[...]
        sig = _sigmoid(zb)
        dyb_t = dyb_ref[...]
        do_f = dyb_t * (zb * sig)
        do = do_f.astype(BF16)
        do_t = do_f.T.astype(BF16)
        q_t = q_ref[...].astype(F32).T.astype(BF16)
        dzb_ref[...] = (dyb_t * o_ref[...] * (sig * (1.0 + zb * (1.0 - sig)))).astype(BF16)

        def sweep(kb, carries, mask):
            rows = pl.ds(pl.multiple_of(kb * ATT_T, ATT_T), ATT_T)
            heads = [slice(h * hd, (h + 1) * hd) for h in range(ATT_HP)]
            scores = [_sb_block(q_ref[:, cols], k_ref[rows, cols], scale, upper2, mask) for cols in heads]
            das = [lax.dot_general(do[:, cols], v_ref[rows, cols], NT, preferred_element_type=F32) for cols in heads]
            new = []
            for h, (cols, carry, (log_beta, sx, total), da) in enumerate(zip(heads, carries, scores, das)):
                a = jnp.exp2(log_beta + sx + carry)
                beta = jnp.exp2(log_beta)
                if mask is not None:
                    a = jnp.where(mask, a, 0.0)
                    beta = jnp.where(mask, beta, 0.0)
                g_s[h, kb] = a * da
                beta_s[h, kb] = beta
                dvt_acc[kb, cols, :] += jnp.dot(do_t[cols, :], a.astype(BF16), preferred_element_type=F32)
                new.append(carry + total)
            return tuple(new)

        carries = sweep(i, tuple(jnp.zeros((ATT_T, 1), F32) for _ in range(ATT_HP)), causal)

        def more(c):
            new = sweep(c[0], c[1], None)
            return c[0] - 1, new, _max_carry(new)

        last, _, _ = lax.while_loop(lambda c: (c[0] >= 0) & (c[2] > DEAD_LOG2), more, (i - 1, carries, _max_carry(carries)))
        first_kb = last + 1

        def back(kb, state):
            rows = pl.ds(pl.multiple_of(kb * ATT_T, ATT_T), ATT_T)
            heads = [slice(h * hd, (h + 1) * hd) for h in range(ATT_HP)]
            sums = [_cumsum_mm(g_s[h, kb], lower2) for h in range(ATT_HP)]
            new = []
            for h, (cols, (p_carry, dq), px) in enumerate(zip(heads, state, sums)):
                dz = ((g_s[h, kb] - (p_carry + px) * beta_s[h, kb]) * scale).astype(BF16)
                dq = dq + jnp.dot(dz, k_ref[rows, cols], preferred_element_type=F32)
                dkt_acc[kb, cols, :] += jnp.dot(q_t[cols, :], dz, preferred_element_type=F32)
                new.append((p_carry + px[:, ATT_T - 1:ATT_T], dq))
            return tuple(new)

        init = tuple((jnp.zeros((ATT_T, 1), F32), jnp.zeros((ATT_T, hd), F32)) for _ in range(ATT_HP))
        state = lax.fori_loop(first_kb, i + 1, back, init)
        for h in range(ATT_HP):
            dq_ref[:, h * hd:(h + 1) * hd] = state[h][1].astype(BF16)

        @pl.when(i == nq - 1)
        def _():
            for kb in range(nq):
                dk_ref[kb * ATT_T:(kb + 1) * ATT_T, :] = dkt_acc[kb].T.astype(BF16)
                dv_ref[kb * ATT_T:(kb + 1) * ATT_T, :] = dvt_acc[kb].T.astype(BF16)

    out = jax.ShapeDtypeStruct((t, d), BF16)
    hp_w = ATT_HP * hd
    return pl.pallas_call(
        body, name="attn_bwd",
        grid=(bsz, n_hp, nq),
        in_specs=[row_blk(0), seq_blk(1), seq_blk(2), row_blk(ZB_GROUP), row_blk(0), row_blk(0)],
        out_specs=[row_blk(0), seq_blk(0), seq_blk(0), row_blk(0)],
        out_shape=[out, out, out, out],
        scratch_shapes=[pltpu.VMEM((ATT_HP, nq, ATT_T, ATT_T), F32), pltpu.VMEM((ATT_HP, nq, ATT_T, ATT_T), F32),
                        pltpu.VMEM((nq, hp_w, ATT_T), F32), pltpu.VMEM((nq, hp_w, ATT_T), F32)],
        compiler_params=_params(("arbitrary", "arbitrary", "arbitrary")),
    )(qkv, qkv, qkv, proj, o, dyb)


def _out_proj(ya, yb, g_pre, x2d, tgt, w_og, w_osb, w_out, g_f):
    t, d = x2d.shape

    def body(ya_ref, yb_ref, ga_ref, gb_ref, x_ref, tgt_ref, wog_ref, wosb_ref, wout_ref, gf_ref,
             dya_ref, dyb_ref, dg_ref, dx2_ref, loss_ref, gnf_ref, gwog_ref, gwosb_ref, gwout_ref):
        @pl.when(pl.program_id(0) == 0)
        def _():
            loss_ref[...] = jnp.zeros_like(loss_ref)
            gnf_ref[...] = jnp.zeros_like(gnf_ref)
            gwog_ref[...] = jnp.zeros_like(gwog_ref)
            gwosb_ref[...] = jnp.zeros_like(gwosb_ref)
            gwout_ref[...] = jnp.zeros_like(gwout_ref)

        ya = ya_ref[...]
        yb = yb_ref[...]
        pa = jnp.dot(ya, wog_ref[...], preferred_element_type=F32)
        pb = jnp.dot(yb, wosb_ref[...], preferred_element_type=F32)
        sga = _sigmoid(ga_ref[...])
        sgb = _sigmoid(gb_ref[...])
        merged = (sga * pa + sgb * pb).astype(BF16)
        x2 = x_ref[...] + jnp.dot(merged, wout_ref[...], preferred_element_type=F32)
        r2 = lax.rsqrt(jnp.mean(x2 * x2, axis=-1, keepdims=True) + EPS)
        n2 = x2 * r2
        gf = gf_ref[...]
        err = n2 * gf - tgt_ref[...]
        loss_ref[...] += 0.5 * jnp.sum(jnp.sum(err * err, axis=-1, keepdims=True), axis=0, keepdims=True) / d
        dy = err * (1.0 / d)
        gnf_ref[...] += jnp.sum(dy * n2, axis=0, keepdims=True)
        dn = dy * gf
        dx2 = r2 * (dn - n2 * jnp.mean(dn * n2, axis=-1, keepdims=True))
        dx2_ref[...] = dx2
        dx2_b = dx2.astype(BF16)
        dmerged = lax.dot_general(dx2_b, wout_ref[...], NT, preferred_element_type=F32)
        gwout_ref[...] += lax.dot_general(merged, dx2_b, TN, preferred_element_type=F32)
        dg_ref[:, 0:d] = (dmerged * pa * (sga * (1.0 - sga))).astype(BF16)
        dg_ref[:, d:2 * d] = (dmerged * pb * (sgb * (1.0 - sgb))).astype(BF16)
        dpa = (dmerged * sga).astype(BF16)
        dpb = (dmerged * sgb).astype(BF16)
        dya_ref[...] = lax.dot_general(dpa, wog_ref[...], NT, preferred_element_type=F32)
        dyb_ref[...] = lax.dot_general(dpb, wosb_ref[...], NT, preferred_element_type=F32)
        gwog_ref[...] += lax.dot_general(ya, dpa, TN, preferred_element_type=F32)
        gwosb_ref[...] += lax.dot_general(yb, dpb, TN, preferred_element_type=F32)

    row = lambda i: (i, 0)
    return pl.pallas_call(
        body, name="out_proj",
        grid=(t // TM,),
        in_specs=[pl.BlockSpec((TM, d), row), pl.BlockSpec((TM, d), row),
                  pl.BlockSpec((TM, d), lambda i: (i, GA_GROUP)), pl.BlockSpec((TM, d), lambda i: (i, GB_GROUP)),
                  pl.BlockSpec((TM, d), row), pl.BlockSpec((TM, d), row),
                  _resident((d, d)), _resident((d, d)), _resident((d, d)), _const((1, d))],
        out_specs=[pl.BlockSpec((TM, d), row), pl.BlockSpec((TM, d), row), pl.BlockSpec((TM, 2 * d), row),
                   pl.BlockSpec((TM, d), row), _const((1, 1)), _const((1, d)),
                   _const((d, d)), _const((d, d)), _const((d, d))],
        out_shape=[jax.ShapeDtypeStruct((t, d), F32), jax.ShapeDtypeStruct((t, d), F32),
                   jax.ShapeDtypeStruct((t, 2 * d), BF16), jax.ShapeDtypeStruct((t, d), F32),
                   jax.ShapeDtypeStruct((1, 1), F32), jax.ShapeDtypeStruct((1, d), F32),
                   jax.ShapeDtypeStruct((d, d), F32), jax.ShapeDtypeStruct((d, d), F32),
                   jax.ShapeDtypeStruct((d, d), F32)],
        compiler_params=_params(("arbitrary",)),
    )(ya, yb, g_pre, g_pre, x2d, tgt, w_og, w_osb, w_out, g_f)


def _dproj_pieces(d):
    return [(0, 0, 3), (1, 3, 1), (2, 4, 1), (3, 5, 1), (4, 6, 1), (5, 7, 2)]


def _in_proj_bwd_x(pieces, wg, x2d, g_in, dx2):
    t, d = x2d.shape
    ncol = wg.shape[2]
    segs = _segments(d, ncol)
    layout = _dproj_pieces(d)

    def body(da_ref, dq_ref, dk_ref, dv_ref, dzb_ref, dg_ref, w_ref, x_ref, g_ref, dx2_ref, gx_ref, gn_ref):
        @pl.when(pl.program_id(0) == 0)
        def _():
            gn_ref[...] = jnp.zeros_like(gn_ref)

        refs = (da_ref, dq_ref, dk_ref, dv_ref, dzb_ref, dg_ref)
        dh = jnp.zeros((TM, d), F32)
        for chip, c0, grp, s0, width in segs:
            piece, first, _ = next(p for p in layout if p[1] <= grp < p[1] + p[2])
            off = (grp - first) * d + s0
            dh = dh + lax.dot_general(refs[piece][:, off:off + width], w_ref[chip, :, c0:c0 + width], NT,
                                      preferred_element_type=F32)
        x = x_ref[...]
        r = lax.rsqrt(jnp.mean(x * x, axis=-1, keepdims=True) + EPS)
        n = x * r
        gn_ref[...] += jnp.sum(dh * n, axis=0, keepdims=True)
        dn = dh * g_ref[...]
        gx_ref[...] = dx2_ref[...] + r * (dn - n * jnp.mean(dn * n, axis=-1, keepdims=True))

    row = lambda i: (i, 0)
    return pl.pallas_call(
        body, name="in_proj_bwd_x",
        grid=(t // TM,),
        in_specs=[pl.BlockSpec((TM, p.shape[1]), row) for p in pieces]
        + [_resident(wg.shape), pl.BlockSpec((TM, d), row), _const((1, d)), pl.BlockSpec((TM, d), row)],
        out_specs=[pl.BlockSpec((TM, d), row), _const((1, d))],
        out_shape=[jax.ShapeDtypeStruct((t, d), F32), jax.ShapeDtypeStruct((1, d), F32)],
        compiler_params=_params(("arbitrary",)),
    )(*pieces, wg, x2d, g_in, dx2)


def _group_order(d, ncol):
    orders = []
    for chip in range(N_CHIP):
        inside = [min((chip + 1) * ncol, (g + 1) * d) - max(chip * ncol, g * d) for g in range(N_SPLIT)]
        by_owner = [[g for g in range(N_SPLIT) if inside[g] <= 0 and g * d // ncol == owner] for owner in range(N_CHIP)]
        by_owner = [groups for groups in by_owner if groups]
        tail = [g for g in range(N_SPLIT) if 0 < inside[g] < d] + [g for g in range(N_SPLIT) if inside[g] == d]
        per_core = []
        for core in range(2):
            lists = [list(groups) for groups in by_owner[core:] + by_owner[:core]]
            head = []
            while any(lists):
                head += [groups.pop(0) for groups in lists if groups]
            per_core.append(head + tail)
        orders.append(per_core)
    return orders


def _in_proj_bwd_w(h_t, pieces, mats16, pack, ncol):
    d, t = h_t.shape
    nk = t // TKW
    layout = _dproj_pieces(d)
    segs = _segments(d, ncol)
    n_mats = len(mats16)
    n_dev = 2 * N_CHIP
    flips = [(dx, dy, dc) for dx in (0, 1) for dy in (0, 1) for dc in (0, 1)][1:]

    def body(order_ref, ht_ref, da_ref, dq_ref, dk_ref, dv_ref, dzb_ref, dg_ref, *rest):
        mat_refs, pack_ref = rest[:n_mats], rest[n_mats]
        gw_ref = rest[n_mats + 1]
        recv_refs, slots_ref, recv_w = rest[n_mats + 2:2 * n_mats + 2], rest[2 * n_mats + 2], rest[2 * n_mats + 3]
        acc, stage, mat_send, mat_recv, pack_send, pack_recv, own_sem, seg_send, w_recv = rest[2 * n_mats + 4:]
        s = pl.program_id(0)
        i = pl.program_id(1)
        group = order_ref[s]
        x_pos, y_pos, c_pos, chips = _place()
        me = 4 * x_pos + 2 * y_pos + c_pos

        def for_shares(which, parity, action):
            for g in range(N_SPLIT):
                for n_seg, (chunk, c0, _, s0, width) in enumerate([sg for sg in segs if sg[2] == g]):
                    for k, (px, py) in enumerate(chips):
                        @pl.when((which == g) & (2 * px + py == chunk))
                        def _(n_seg=n_seg, c0=c0, s0=s0, width=width, k=k, px=px, py=py):
                            action(pltpu.make_async_remote_copy(
                                src_ref=stage.at[parity, :, s0:s0 + width], dst_ref=recv_w.at[k, :, c0:c0 + width],
                                send_sem=seg_send.at[parity, n_seg], recv_sem=w_recv.at[k],
                                device_id=(px, py, c_pos), device_id_type=MESH))

        def exchanges():
            cps = []
            for k, (px, py) in enumerate(chips):
                for a in range(n_mats):
                    cps.append(pltpu.make_async_remote_copy(
                        src_ref=mat_refs[a].at[2 * px + py], dst_ref=recv_refs[a].at[k],
                        send_sem=mat_send.at[a, k], recv_sem=mat_recv.at[a, k],
                        device_id=(px, py, c_pos), device_id_type=MESH))
            for k, (dx, dy, dc) in enumerate(flips):
                peer = (1 - x_pos if dx else x_pos, 1 - y_pos if dy else y_pos, 1 - c_pos if dc else c_pos)
                cps.append(pltpu.make_async_remote_copy(
                    src_ref=pack_ref, dst_ref=slots_ref.at[me], send_sem=pack_send.at[k], recv_sem=pack_recv.at[k],
                    device_id=peer, device_id_type=MESH))
            return cps, pltpu.make_async_copy(pack_ref, slots_ref.at[me], own_sem)

        @pl.when((s == 0) & (i == 0))
        def _():
            cps, own = exchanges()
            own.start()
            for cp in cps:
                cp.start()

        @pl.when(i == 0)
        def _():
            acc[...] = jnp.zeros_like(acc)

        refs = (da_ref, dq_ref, dk_ref, dv_ref, dzb_ref, dg_ref)
        for piece, first, count in layout:
            @pl.when((group >= first) & (group < first + count))
            def _(piece=piece):
                acc[...] += jnp.dot(ht_ref[...], refs[piece][...], preferred_element_type=F32)

        @pl.when(i == nk - 1)
        def _():
            gw_ref[...] = acc[...]
            for slot in range(STAGE_SLOTS):
                @pl.when(s % STAGE_SLOTS == slot)
                def _(slot=slot):
                    @pl.when(s >= STAGE_SLOTS)
                    def _():
                        for_shares(order_ref[s - STAGE_SLOTS], slot, lambda cp: cp.wait_send())
                    stage[slot] = acc[...].astype(BF16)
                    for_shares(group, slot, lambda cp: cp.start())

        @pl.when((s == N_SPLIT - 1) & (i == nk - 1))
        def _():
            for late in range(N_SPLIT - STAGE_SLOTS, N_SPLIT):
                for_shares(order_ref[late], late % STAGE_SLOTS, lambda cp: cp.wait_send())
            for k in range(N_CHIP - 1):
                pltpu.make_async_remote_copy(
                    src_ref=recv_w.at[k], dst_ref=recv_w.at[k], send_sem=seg_send.at[0, 0], recv_sem=w_recv.at[k],
                    device_id=(x_pos, y_pos, c_pos), device_id_type=MESH).wait_recv()
            cps, own = exchanges()
            own.wait()
            for cp in cps:
                cp.wait()

    def piece_spec(p, first, count):
        def index(s, i, order):
            mine = (order[s] >= first) & (order[s] < first + count)
            return jnp.where(mine, i, 0), jnp.where(mine, order[s] - first, 0)
        return pl.BlockSpec((TKW, d), index)

    chip = 2 * lax.axis_index("x") + lax.axis_index("y")
    order = jnp.asarray(_group_order(d, ncol), jnp.int32)[chip, lax.axis_index("c")]
    outs = pl.pallas_call(
        body, name="in_proj_bwd_w",
        grid_spec=pltpu.PrefetchScalarGridSpec(
            num_scalar_prefetch=1, grid=(N_SPLIT, nk),
            in_specs=[pl.BlockSpec((d, TKW), lambda s, i, order: (0, i))] + [piece_spec(*p) for p in layout]
            + [ANY] * (n_mats + 1),
            out_specs=[pl.BlockSpec((d, d), lambda s, i, order: (0, order[s]))] + [ANY] * (n_mats + 2),
            scratch_shapes=[pltpu.VMEM((d, d), F32), pltpu.VMEM((STAGE_SLOTS, d, d), BF16),
                            pltpu.SemaphoreType.DMA((n_mats, N_CHIP - 1)), pltpu.SemaphoreType.DMA((n_mats, N_CHIP - 1)),
                            pltpu.SemaphoreType.DMA((n_dev - 1,)), pltpu.SemaphoreType.DMA((n_dev - 1,)),
                            pltpu.SemaphoreType.DMA,
                            pltpu.SemaphoreType.DMA((STAGE_SLOTS, 2)), pltpu.SemaphoreType.DMA((N_CHIP - 1,))]),
        out_shape=[jax.ShapeDtypeStruct((d, N_SPLIT * d), F32)]
        + [jax.ShapeDtypeStruct((N_CHIP - 1,) + m.shape[1:], BF16) for m in mats16]
        + [jax.ShapeDtypeStruct((n_dev,) + pack.shape, F32), jax.ShapeDtypeStruct((N_CHIP - 1, d, ncol), BF16)],
        compiler_params=_params(("arbitrary", "arbitrary")),
    )(order, h_t, *pieces, *mats16, pack)
    return outs[0], outs[1:1 + n_mats], outs[1 + n_mats], outs[2 + n_mats]


def _local_step(proj, qkv, x2d, tgt2d, bsz, seq, norm_v, w_s, b_s, w_og, w_osb, w_out, norm_final):
    d = x2d.shape[1]
    chunk = w_s.shape[-1]
    causal = jnp.tril(jnp.ones((chunk, chunk), dtype=bool))
    wm = jnp.where(causal[None], w_s, 0.0).astype(BF16)
    wm_t = jnp.swapaxes(wm, 1, 2)
    b_t = b_s.T

    ya = _branch_a_fwd(proj, norm_v, wm, b_t)
    o, yb = _attn_fwd(qkv, proj, bsz, seq)
    dya, dyb, dg, dx2, loss, g_nf, g_wog, g_wosb, g_wout = _out_proj(
        ya, yb, proj, x2d, tgt2d, w_og, w_osb, w_out, norm_final.reshape(1, d))
    dq, dk, dv, dzb = _attn_bwd(qkv, proj, o, dyb, bsz, seq)
    d_a, g_ws, g_bt, g_nv = _branch_a_bwd(proj, dya, norm_v, wm, wm_t, b_t)
    g_ws = jnp.where(causal[None], g_ws, 0.0)
    return loss, (d_a, dq, dk, dv, dzb, dg), dx2, g_nv, g_ws, g_bt.T, g_wog, g_wosb, g_wout, g_nf


def _row_tile(rows):
    return next(r for r in (128, 64, 32, 16, 8) if rows % r == 0)


def _cast_bf16(arrs):
    n = len(arrs)

    def body(*refs):
        for a_ref, o_ref in zip(refs[:n], refs[n:]):
            o_ref[...] = a_ref[...].astype(BF16)

    specs = [pl.BlockSpec((a.shape[0] // CAST_STEPS, a.shape[1]), lambda i: (i, 0)) for a in arrs]
    return pl.pallas_call(
        body, name="cast_bf16", grid=(CAST_STEPS,),
        in_specs=specs, out_specs=specs,
        out_shape=[jax.ShapeDtypeStruct(a.shape, BF16) for a in arrs],
        compiler_params=_params(("arbitrary",)),
    )(*arrs)


def _add_received(full, recv, chip, by_cols):
    _, rows, cols = recv.shape
    tr = _row_tile(rows)
    nb = rows // tr

    def body(chip_ref, own_ref, recv_ref, o_ref):
        s = own_ref[...]
        for k in range(N_CHIP - 1):
            s = s + recv_ref[k].astype(F32)
        o_ref[...] = s

    own_map = (lambda i, chip_ref: (i, chip_ref[0])) if by_cols else (lambda i, chip_ref: (chip_ref[0] * nb + i, 0))
    return pl.pallas_call(
        body, name="add_received",
        grid_spec=pltpu.PrefetchScalarGridSpec(
            num_scalar_prefetch=1, grid=(nb,),
            in_specs=[pl.BlockSpec((tr, cols), own_map),
                      pl.BlockSpec((N_CHIP - 1, tr, cols), lambda i, chip_ref: (0, i, 0))],
            out_specs=pl.BlockSpec((tr, cols), lambda i, chip_ref: (i, 0))),
        out_shape=jax.ShapeDtypeStruct((rows, cols), F32),
        compiler_params=_params(("arbitrary",)),
    )(chip.reshape(1).astype(jnp.int32), full, recv)


def _adamw_math(w, m, v, g):
    new_m = ADAM_B1 * m + (1.0 - ADAM_B1) * g
    new_v = ADAM_B2 * v + (1.0 - ADAM_B2) * (g * g)
    m_hat = new_m / (1.0 - ADAM_B1 ** ADAM_STEP)
    v_hat = new_v / (1.0 - ADAM_B2 ** ADAM_STEP)
    return -ADAM_LR * (m_hat / (jnp.sqrt(v_hat) + ADAM_EPS) + ADAM_WD * w), new_m, new_v


def _adamw(w, m, v, g_parts):
    rows, cols = w.shape
    tr = _row_tile(rows)
    n_parts = len(g_parts)

    def body(*refs):
        w_ref, m_ref, v_ref = refs[:3]
        part_refs = refs[3:3 + n_parts]
        g_ref, d_ref, nm_ref, nv_ref = refs[3 + n_parts:]
        g = part_refs[0][...]
        for p in part_refs[1:]:
            g = g + p[...]
        g_ref[...] = g
        d_ref[...], nm_ref[...], nv_ref[...] = _adamw_math(w_ref[...], m_ref[...], v_ref[...], g)

    spec = pl.BlockSpec((tr, cols), lambda i: (i, 0))
    out = jax.ShapeDtypeStruct(w.shape, F32)
    return pl.pallas_call(
        body, name="adamw", grid=(rows // tr,),
        in_specs=[spec] * (3 + n_parts), out_specs=[spec] * 4, out_shape=[out] * 4,
        compiler_params=_params(("arbitrary",)),
    )(w, m, v, *g_parts)


def _adamw_small(w, m, v, slots):
    n_dev = slots.shape[0]

    def body(w_ref, m_ref, v_ref, slots_ref, g_ref, d_ref, nm_ref, nv_ref):
        g = slots_ref[0]
        for i in range(1, n_dev):
            g = g + slots_ref[i]
        g_ref[...] = g
        d_ref[...], nm_ref[...], nv_ref[...] = _adamw_math(w_ref[...], m_ref[...], v_ref[...], g)

    vmem = pl.BlockSpec(memory_space=pltpu.VMEM)
    out = jax.ShapeDtypeStruct(w.shape, F32)
    return pl.pallas_call(
        body, name="adamw_small", in_specs=[vmem] * 4, out_specs=[vmem] * 4, out_shape=[out] * 4,
        compiler_params=pltpu.CompilerParams(vmem_limit_bytes=VMEM_LIMIT),
    )(w, m, v, slots)


ANY = pl.BlockSpec(memory_space=pl.ANY)


def _place():
    x, y, c = lax.axis_index("x"), lax.axis_index("y"), lax.axis_index("c")
    other_chips = [(1 - x, y), (x, 1 - y), (1 - x, 1 - y)]
    return x, y, c, other_chips


def _swap_with_sibling(arrs):
    n = len(arrs)

    def body(*refs):
        ins, outs = refs[:n], refs[n:2 * n]
        send_sems, recv_sems = refs[2 * n:]
        x, y, c, _ = _place()
        copies = [pltpu.make_async_remote_copy(
            src_ref=ins[a], dst_ref=outs[a], send_sem=send_sems.at[a], recv_sem=recv_sems.at[a],
            device_id=(x, y, 1 - c), device_id_type=MESH) for a in range(n)]
        for cp in copies:
            cp.start()
        for cp in copies:
            cp.wait()

    return pl.pallas_call(
        body, name="swap_with_sibling",
        in_specs=[ANY] * n, out_specs=[ANY] * n,
        out_shape=[jax.ShapeDtypeStruct(a.shape, a.dtype) for a in arrs],
        scratch_shapes=[pltpu.SemaphoreType.DMA((n,)), pltpu.SemaphoreType.DMA((n,))],
    )(*arrs)


SLAB = 8


def _slab(vec, d):
    return jnp.pad(vec.reshape(1, d), ((0, SLAB - 1), (0, 0)))


def _pack_tail(vec_nv, b_s, vec_nf, w_s, scalar=None):
    d = vec_nv.shape[-1]
    extra = jnp.zeros((1, d), F32) if scalar is None else jnp.pad(scalar, ((0, 0), (0, d - 1)))
    return jnp.concatenate([_slab(v, d) for v in (vec_nv, b_s, vec_nf, extra)] + [w_s.reshape(-1, d)], axis=0)


def _pack_small(vec_nin, vec_nv, b_s, vec_nf, w_s, scalar=None):
    return jnp.concatenate([_slab(vec_nin, vec_nin.shape[-1]), _pack_tail(vec_nv, b_s, vec_nf, w_s, scalar)], axis=0)


def _unpack_small(pack, w_s_shape, b_s_shape):
    return (pack[0:1], pack[SLAB:SLAB + 1], pack[2 * SLAB].reshape(b_s_shape), pack[3 * SLAB],
            pack[5 * SLAB:].reshape(w_s_shape), pack[4 * SLAB, 0])


def kernel(x, norm_in, w_in, norm_v, w_s, b_s, w_o_gmlp, w_o_sb, w_out, norm_final, loss_target, m_norm_in, m_w_in, m_norm_v, m_w_s, m_b_s, m_w_o_gmlp, m_w_o_sb, m_w_out, m_norm_final, v_norm_in, v_w_in, v_norm_v, v_w_s, v_b_s, v_w_o_gmlp, v_w_o_sb, v_w_out, v_norm_final):
    d = x.shape[-1]
    ncol = w_in.shape[-1]
    nrow = w_o_gmlp.shape[-2]
    chip = 2 * lax.axis_index("x") + lax.axis_index("y")

    bsz, seq, _ = x.shape
    x2d = x.reshape(bsz * seq, d)
    shards = [w_in[0], w_o_gmlp[0], w_o_sb[0], w_out[0]]
    halves = [s16.reshape(2, s16.shape[0] // 2, s16.shape[1]) for s16 in _cast_bf16(shards)]
    proj, qkv, h_t, (wg, w_og, w_osb, w_o) = _in_proj_fwd(x2d, norm_in, halves)
    wg = wg.reshape(N_CHIP, d, ncol)

    loss, pieces, dx2, g_nv, g_ws, g_bs, g_wog, g_wosb, g_wout, g_nf = _local_step(
        proj, qkv, x2d, loss_target.reshape(bsz * seq, d), bsz, seq, norm_v, w_s[0], b_s[0],
        w_og.reshape(d, d), w_osb.reshape(d, d), w_o.reshape(d, d), norm_final)

    grad_x, g_nin = _in_proj_bwd_x(pieces, wg, x2d, norm_in, dx2)
    grad_x = grad_x.reshape(bsz, seq, d)
    mats = [g_wog, g_wosb, g_wout]
    mats16 = [g16.reshape(N_CHIP, nrow, d) for g16 in _cast_bf16(mats)]
    g_win, recv_mats, slots, recv_win = _in_proj_bwd_w(
        h_t, pieces, mats16, _pack_small(g_nin, g_nv, g_bs, g_nf, g_ws, loss), ncol)

    sums = [_add_received(g_win, recv_win, chip, True)] + [
        _add_received(g, r, chip, False) for g, r in zip(mats, recv_mats)]
    sibling_sums = _swap_with_sibling(sums)
    stats = []
    for w, m, v, mine, theirs in zip(shards, [m_w_in[0], m_w_o_gmlp[0], m_w_o_sb[0], m_w_out[0]],
                                     [v_w_in[0], v_w_o_gmlp[0], v_w_o_sb[0], v_w_out[0]], sums, sibling_sums):
        stats.append(_adamw(w, m, v, [mine, theirs]))
    (gw_in, dw_in, nm_in, nv_in), (gw_og, dw_og, nm_og, nv_og), (gw_osb, dw_osb, nm_osb, nv_osb), \
        (gw_out, dw_out, nm_out, nv_out) = stats

    gs, ds, ms, vs = _adamw_small(
        _pack_small(norm_in, norm_v, b_s[0], norm_final, w_s[0]),
        _pack_small(m_norm_in, m_norm_v, m_b_s[0], m_norm_final, m_w_s[0]),
        _pack_small(v_norm_in, v_norm_v, v_b_s[0], v_norm_final, v_w_s[0]), slots)

    def small(pack):
        nin, nv, bs, nf, ws, _ = _unpack_small(pack, w_s.shape, b_s.shape)
        return nin, nv, ws, bs, nf

    loss = _unpack_small(gs, w_s.shape, b_s.shape)[-1]
    out = []
    for small_pack, win, wog, wosb, wout in ((gs, gw_in, gw_og, gw_osb, gw_out), (ds, dw_in, dw_og, dw_osb, dw_out),
                                             (ms, nm_in, nm_og, nm_osb, nm_out), (vs, nv_in, nv_og, nv_osb, nv_out)):
        nin, nv, ws, bs, nf = small(small_pack)
        out += [nin, win[None], nv, ws, bs, wog[None], wosb[None], wout[None], nf]
    return (loss, grad_x, *out)
```

```python
import functools
import math

import jax
import jax.numpy as jnp
from jax import lax
from jax.experimental import pallas as pl
from jax.experimental.pallas import tpu as pltpu

F32 = jnp.float32
BF16 = jnp.bfloat16
EPS = 1e-6
HEADS = 8
N_SPLIT = 9
N_CHIP = 4
MESH = pl.DeviceIdType.MESH

ADAM_LR = 0.001
ADAM_B1 = 0.9
ADAM_B2 = 0.999
ADAM_EPS = 1e-08
ADAM_WD = 0.01
ADAM_STEP = 10

VMEM_LIMIT = 56 * 2 ** 20
TM = 256
TMF = 512
ATT_T = 256
ATT_HP = 4
TKW = 1024
CHUNKS_PER_STEP = 4
CAST_STEPS = 8
STAGE_SLOTS = 3

NT = (((1,), (1,)), ((), ()))
TN = (((0,), (0,)), ((), ()))


def _params(sem):
    return pltpu.CompilerParams(dimension_semantics=sem, vmem_limit_bytes=VMEM_LIMIT)


def _resident(shape):
    nd = len(shape)
    return pl.BlockSpec(shape, lambda *_: (0,) * nd, pipeline_mode=pl.Buffered(1))


def _const(shape):
    nd = len(shape)
    return pl.BlockSpec(shape, lambda *_: (0,) * nd)


def _segments(d, ncol):
    segs = []
    edges = sorted({j * ncol for j in range(N_CHIP + 1)} | {s * d for s in range(N_SPLIT + 1)})
    for lo, hi in zip(edges[:-1], edges[1:]):
        segs.append((lo // ncol, lo % ncol, lo // d, lo % d, hi - lo))
    return segs


def _sigmoid(x):
    return 0.5 * jnp.tanh(0.5 * x) + 0.5


_GELU_C = math.sqrt(2.0 / math.pi)


_GELU_CA = _GELU_C * 0.044715


def _gelu(x):
    return x * (0.5 * jnp.tanh(x * (_GELU_C + _GELU_CA * (x * x))) + 0.5)


def _gelu_and_grad(x):
    x2 = x * x
    u = 0.5 * jnp.tanh(x * (_GELU_C + _GELU_CA * x2)) + 0.5
    slope = (1.0 - u) * (x * (_GELU_C + (3.0 * _GELU_CA) * x2))
    return x * u, u * (2.0 * slope + 1.0)


def _split_bf16(a):
    hi = a.astype(BF16)
    lo = (a - hi.astype(F32)).astype(BF16)
    return hi, lo


def _in_proj_fwd(x2d, g_in, halves):
    t, d = x2d.shape
    n = len(halves)
    ncol = halves[0].shape[2]
    n_row = t // TMF
    last = n_row - 1
    assert halves[0].shape[1] * 2 == d
    qkv_parts = {j: (max(j * ncol, 3 * d) - j * ncol, max(j * ncol, 3 * d) - 3 * d)
                 for j in range(N_CHIP) if min((j + 1) * ncol, 6 * d) > max(j * ncol, 3 * d)}
    qkv_w = 3 * d // len(qkv_parts)
    assert all(min((j + 1) * ncol, 6 * d) - max(j * ncol, 3 * d) == qkv_w and q0 % qkv_w == 0
               for j, (_, q0) in qkv_parts.items())

    def body(order_ref, qrow_ref, qcol_ref, x_ref, g_ref, *rest):
        ins = rest[:n]
        proj_ref, qkv_ref, ht_ref = rest[n:n + 3]
        outs = rest[n + 3:2 * n + 3]
        wbuf, h_all, send_sems, recv_sems, local_sems, load_sem = rest[2 * n + 3:]
        phase = pl.program_id(0)
        i = pl.program_id(1)
        x_pos, y_pos, c_pos, chips = _place()
        sibling = (x_pos, y_pos, 1 - c_pos)
        me = (x_pos, y_pos, c_pos)
        my_chip = 2 * x_pos + y_pos

        def copy(a, k, block, to, src=None):
            return pltpu.make_async_remote_copy(
                src_ref=outs[a].at[block] if src is None else src, dst_ref=outs[a].at[block],
                send_sem=send_sems.at[a, k], recv_sem=recv_sems.at[a, k], device_id=to, device_id_type=MESH)

        def local(a):
            return pltpu.make_async_copy(ins[a], outs[a].at[pl.ds(2 * my_chip, 2)], local_sems.at[a])

        def load(src, first, slot):
            for half in range(2):
                cp = pltpu.make_async_copy(src.at[first + half], wbuf.at[slot, pl.ds(half * (d // 2), d // 2)], load_sem)
                cp.start()
                cp.wait()

        def send_mine(k):
            px, py = chips[k]
            for a in range(n):
                copy(a, k, 2 * my_chip + c_pos, (px, py, c_pos), src=ins[a].at[c_pos]).start()

        @pl.when((phase == 0) & (i == 0))
        def _():
            for a in range(n):
                local(a).start()
            send_mine(0)
            send_mine(1)
            load(ins[0], 0, 0)

        for k, (px, py) in enumerate(chips):
            @pl.when((phase == k + 1) & (i == 0))
            def _(k=k, px=px, py=py):
                theirs = 2 * (2 * px + py)
                for a in range(n):
                    copy(a, k, theirs + c_pos, me).wait_recv()
                    copy(a, 3 + k, theirs + c_pos, sibling).start()
                if k == 0:
                    send_mine(2)
                for a in range(n):
                    copy(a, 3 + k, theirs + 1 - c_pos, me).wait_recv()
                load(outs[0], theirs, (k + 1) % 2)

        @pl.when(phase == 0)
        def _():
            x = x_ref[...]
            r = lax.rsqrt(jnp.mean(x * x, axis=-1, keepdims=True) + EPS)
            hf = x * r * g_ref[...]
            h_all[i] = hf.astype(BF16)
            ht_ref[...] = hf.T.astype(BF16)

        for slot in range(2):
            @pl.when(phase % 2 == slot)
            def _(slot=slot):
                proj_ref[...] = jnp.dot(h_all[i], wbuf[slot], preferred_element_type=F32)

        for chunk, (c0, _) in qkv_parts.items():
            @pl.when(order_ref[phase] == chunk)
            def _(c0=c0):
                qkv_ref[...] = proj_ref[:, c0:c0 + qkv_w].astype(BF16)

        @pl.when((phase == N_CHIP - 1) & (i == n_row - 1))
        def _():
            for a in range(n):
                for k in range(2 * (N_CHIP - 1)):
                    copy(a, k, 0, me).wait_send()
                local(a).wait()

    x_pos, y_pos = lax.axis_index("x"), lax.axis_index("y")
    order = jnp.stack([2 * x_pos + y_pos, 2 * (1 - x_pos) + y_pos, 2 * x_pos + 1 - y_pos,
                       2 * (1 - x_pos) + 1 - y_pos]).astype(jnp.int32)
    holds = [functools.reduce(jnp.logical_or, [order[p] == j for j in qkv_parts]) for p in range(N_CHIP)]
    col = [sum(jnp.where(order[p] == j, q0 // qkv_w, 0) for j, (_, q0) in qkv_parts.items()) for p in range(N_CHIP)]
    cur = col[-1]
    for p in reversed(range(N_CHIP - 1)):
        cur = jnp.where(holds[p], col[p], cur)
    seen = jnp.bool_(False)
    qrow, qcol = [], []
    for p in range(N_CHIP):
        cur = jnp.where(holds[p], col[p], cur)
        qrow.append(jnp.where(holds[p], -1, jnp.where(seen, last, 0)))
        qcol.append(cur)
        seen = seen | holds[p]
    qrow = jnp.stack(qrow).astype(jnp.int32)
    qcol = jnp.stack(qcol).astype(jnp.int32)

    outs = pl.pallas_call(
        body, name="in_proj_fwd",
        grid_spec=pltpu.PrefetchScalarGridSpec(
            num_scalar_prefetch=3, grid=(N_CHIP, n_row),
            in_specs=[pl.BlockSpec((TMF, d), lambda p, i, order, qrow, qcol: (jnp.where(p == 0, i, last), 0)),
                      pl.BlockSpec((1, d), lambda p, i, order, qrow, qcol: (0, 0))] + [ANY] * n,
            out_specs=[pl.BlockSpec((TMF, ncol), lambda p, i, order, qrow, qcol: (i, order[p])),
                       pl.BlockSpec((TMF, qkv_w),
                                    lambda p, i, order, qrow, qcol: (jnp.where(qrow[p] < 0, i, qrow[p]), qcol[p])),
                       pl.BlockSpec((d, TMF), lambda p, i, order, qrow, qcol: (0, jnp.where(p == 0, i, last)))]
            + [ANY] * n,
            scratch_shapes=[pltpu.VMEM((2, d, ncol), BF16), pltpu.VMEM((n_row, TMF, d), BF16),
                            pltpu.SemaphoreType.DMA((n, 2 * (N_CHIP - 1))), pltpu.SemaphoreType.DMA((n, 2 * (N_CHIP - 1))),
                            pltpu.SemaphoreType.DMA((n,)), pltpu.SemaphoreType.DMA]),
        out_shape=[jax.ShapeDtypeStruct((t, N_CHIP * ncol), F32), jax.ShapeDtypeStruct((t, 3 * d), BF16),
                   jax.ShapeDtypeStruct((d, t), BF16)]
        + [jax.ShapeDtypeStruct((2 * N_CHIP,) + hv.shape[1:], BF16) for hv in halves],
        compiler_params=_params(("arbitrary", "arbitrary")),
    )(order, qrow, qcol, x2d, g_in, *halves)
    return outs[0], outs[1], outs[2], outs[3:]


def _branch_a_fwd(a_pre, g_v, wm, b_t):
    t = a_pre.shape[0]
    d = g_v.shape[1]
    d3 = 3 * d
    ng, chunk, _ = wm.shape
    cw = d // ng

    per_step = CHUNKS_PER_STEP if t % (CHUNKS_PER_STEP * chunk) == 0 else 1

    def body(a_ref, gv_ref, wm_ref, bt_ref, ya_ref):
        for n in range(per_step):
            rows = slice(n * chunk, (n + 1) * chunk)
            ua = _gelu(a_ref[rows, 0:d])
            vg = _gelu(a_ref[rows, d:2 * d])
            za = a_ref[rows, 2 * d:3 * d]
            rv = lax.rsqrt(jnp.mean(vg * vg, axis=-1, keepdims=True) + EPS)
            va = (vg * rv * gv_ref[...]).astype(BF16)
            gate = ua * (za * _sigmoid(za))
            for g in range(ng):
                sl = slice(g * cw, (g + 1) * cw)
                mixed = jnp.dot(wm_ref[g], va[:, sl], preferred_element_type=F32) + bt_ref[:, g:g + 1]
                ya_ref[rows, sl] = (gate[:, sl] * mixed).astype(BF16)

    tile = per_step * chunk
    return pl.pallas_call(
        body, name="branch_a_fwd",
        grid=(t // tile,),
        in_specs=[pl.BlockSpec((tile, d3), lambda i: (i, 0)), _const((1, d)), _const(wm.shape), _const(b_t.shape)],
        out_specs=pl.BlockSpec((tile, d), lambda i: (i, 0)),
        out_shape=jax.ShapeDtypeStruct((t, d), BF16),
        compiler_params=_params(("arbitrary",)),
    )(a_pre, g_v, wm, b_t)


def _branch_a_bwd(a_pre, dya, g_v, wm, wm_t, b_t):
    t = a_pre.shape[0]
    d = g_v.shape[1]
    d3 = 3 * d
    ng, chunk, _ = wm.shape
    cw = d // ng
    nsteps = t // chunk

    def body(a_ref, dya_ref, gv_ref, wm_ref, wmt_ref, bt_ref, da_ref, gws_ref, gbt_ref, gnv_ref, db_acc):
        i = pl.program_id(0)

        @pl.when(i == 0)
        def _():
            gws_ref[...] = jnp.zeros_like(gws_ref)
            gnv_ref[...] = jnp.zeros_like(gnv_ref)
            db_acc[...] = jnp.zeros_like(db_acc)

        ua, dgelu_u = _gelu_and_grad(a_ref[:, 0:d])
        vg, dgelu_v = _gelu_and_grad(a_ref[:, d:2 * d])
        za = a_ref[:, 2 * d:3 * d]
        sig = _sigmoid(za)
        sz = za * sig
        dsz = sig * (1.0 + za * (1.0 - sig))
        rv = lax.rsqrt(jnp.mean(vg * vg, axis=-1, keepdims=True) + EPS)
        nv = vg * rv
        gv = gv_ref[...]
        va = (nv * gv).astype(BF16)
        dya = dya_ref[...]
        dmix = dya * ua * sz
        db_acc[...] += dmix
        dmix_b = dmix.astype(BF16)
        t_gate = dya * sz
        t_z = dya * ua * dsz
        dva_parts = []
        for g in range(ng):
            sl = slice(g * cw, (g + 1) * cw)
            mixed = jnp.dot(wm_ref[g], va[:, sl], preferred_element_type=F32) + bt_ref[:, g:g + 1]
            da_ref[:, sl] = (t_gate[:, sl] * mixed * dgelu_u[:, sl]).astype(BF16)
            da_ref[:, 2 * d + g * cw:2 * d + (g + 1) * cw] = (t_z[:, sl] * mixed).astype(BF16)
            gws_ref[g] += lax.dot_general(dmix_b[:, sl], va[:, sl], NT, preferred_element_type=F32)
            dva_parts.append(jnp.dot(wmt_ref[g], dmix_b[:, sl], preferred_element_type=F32))
        dva = jnp.concatenate(dva_parts, axis=1)
        gnv_ref[...] += jnp.sum(dva * nv, axis=0, keepdims=True)
        dnv = dva * gv
        dvg = rv * (dnv - nv * jnp.mean(dnv * nv, axis=-1, keepdims=True))
        da_ref[:, d:2 * d] = (dvg * dgelu_v).astype(BF16)

        @pl.when(i == nsteps - 1)
        def _():
            acc = db_acc[...]
            for g in range(ng):
                gbt_ref[:, g:g + 1] = jnp.sum(acc[:, g * cw:(g + 1) * cw], axis=1, keepdims=True)

    return pl.pallas_call(
        body, name="branch_a_bwd",
        grid=(nsteps,),
        in_specs=[pl.BlockSpec((chunk, d3), lambda i: (i, 0)), pl.BlockSpec((chunk, d), lambda i: (i, 0)),
                  _const((1, d)), _const(wm.shape), _const(wm_t.shape), _const(b_t.shape)],
        out_specs=[pl.BlockSpec((chunk, d3), lambda i: (i, 0)), _const(wm.shape), _const(b_t.shape), _const((1, d))],
        out_shape=[jax.ShapeDtypeStruct((t, d3), BF16), jax.ShapeDtypeStruct(wm.shape, F32),
                   jax.ShapeDtypeStruct(b_t.shape, F32), jax.ShapeDtypeStruct((1, d), F32)],
        scratch_shapes=[pltpu.VMEM((chunk, d), F32)],
        compiler_params=_params(("arbitrary",)),
    )(a_pre, dya, g_v, wm, wm_t, b_t)


def _tri(n, rows_gt_cols):
    r = lax.broadcasted_iota(jnp.int32, (n, n), 0)
    c = lax.broadcasted_iota(jnp.int32, (n, n), 1)
    return (r > c) if rows_gt_cols else (r < c)


def _twice(tri):
    t = tri.astype(BF16)
    return jnp.concatenate([t, t], axis=0)


def _cumsum_mm(a, tri2):
    hi, lo = _split_bf16(a)
    return jnp.dot(jnp.concatenate([hi, lo], axis=1), tri2, preferred_element_type=F32)


LOG2E = 1.4426950408889634
_SIGN = 0x80000000


def _sb_block(q, k, scale, upper2, causal):
    z2 = lax.dot_general(q, k, NT, preferred_element_type=F32) * (scale * LOG2E)
    neg_abs = lax.bitcast_convert_type(lax.bitcast_convert_type(z2, jnp.uint32) | jnp.uint32(_SIGN), F32)
    l2 = jnp.log(1.0 + jnp.exp2(neg_abs)) * LOG2E
    log_beta = jnp.minimum(z2, 0.0) - l2
    lom = log_beta - z2
    if causal is not None:
        lom = jnp.where(causal, lom, 0.0)
    sx = _cumsum_mm(lom, upper2)
    return log_beta, sx, sx[:, 0:1] + lom[:, 0:1]


DEAD_LOG2 = -150.0


def _max_carry(carries):
    return jnp.max(functools.reduce(jnp.maximum, carries))


ZB_GROUP, GA_GROUP, GB_GROUP = 6, 7, 8


def _attn_specs(d, seq, nq):
    hp_w = ATT_HP * (d // HEADS)
    n_hp = d // hp_w
    row_blk = lambda group: pl.BlockSpec((ATT_T, hp_w), lambda b, h, i: (b * nq + i, group * n_hp + h))
    seq_blk = lambda group: pl.BlockSpec((seq, hp_w), lambda b, h, i: (b, group * n_hp + h))
    return row_blk, seq_blk, n_hp


def _attn_fwd(qkv, proj, bsz, seq):
    t, d3 = qkv.shape
    d = d3 // 3
    hd = d // HEADS
    nq = seq // ATT_T
    scale = hd ** -0.5
    row_blk, seq_blk, n_hp = _attn_specs(d, seq, nq)

    def body(q_ref, k_ref, v_ref, zb_ref, o_ref, yb_ref):
        i = pl.program_id(2)
        causal = _tri(ATT_T, True)
        upper2 = _twice(causal)

        def step(kb, state, mask):
            rows = pl.ds(pl.multiple_of(kb * ATT_T, ATT_T), ATT_T)
            heads = [slice(h * hd, (h + 1) * hd) for h in range(ATT_HP)]
            scores = [_sb_block(q_ref[:, cols], k_ref[rows, cols], scale, upper2, mask) for cols in heads]
            new = []
            for cols, (carry, acc), (log_beta, sx, total) in zip(heads, state, scores):
                a = jnp.exp2(log_beta + sx + carry)
                if mask is not None:
                    a = jnp.where(mask, a, 0.0)
                acc = acc + jnp.dot(a.astype(BF16), v_ref[rows, cols], preferred_element_type=F32)
                new.append((carry + total, acc))
            return tuple(new)

        init = tuple((jnp.zeros((ATT_T, 1), F32), jnp.zeros((ATT_T, hd), F32)) for _ in range(ATT_HP))
        state = step(i, init, causal)
        def more(c):
            new = step(c[0], c[1], None)
            return c[0] - 1, new, _max_carry([s[0] for s in new])

        _, state, _ = lax.while_loop(lambda c: (c[0] >= 0) & (c[2] > DEAD_LOG2), more,
                                     (i - 1, state, _max_carry([s[0] for s in state])))
        for h in range(ATT_HP):
            cols = slice(h * hd, (h + 1) * hd)
            acc = state[h][1]
            zb = zb_ref[:, cols]
            o_ref[:, cols] = acc
            yb_ref[:, cols] = (acc * (zb * _sigmoid(zb))).astype(BF16)

    return pl.pallas_call(
        body, name="attn_fwd",
        grid=(bsz, n_hp, nq),
        in_specs=[row_blk(0), seq_blk(1), seq_blk(2), row_blk(ZB_GROUP)],
        out_specs=[row_blk(0), row_blk(0)],
        out_shape=[jax.ShapeDtypeStruct((t, d), F32), jax.ShapeDtypeStruct((t, d), BF16)],
        compiler_params=_params(("arbitrary", "arbitrary", "arbitrary")),
    )(qkv, qkv, qkv, proj)


def _attn_bwd(qkv, proj, o, dyb, bsz, seq):
    t, d3 = qkv.shape
    d = d3 // 3
    hd = d // HEADS
    nq = seq // ATT_T
    scale = hd ** -0.5
    row_blk, seq_blk, n_hp = _attn_specs(d, seq, nq)

    def body(q_ref, k_ref, v_ref, zb_ref, o_ref, dyb_ref, dq_ref, dk_ref, dv_ref, dzb_ref,
             g_s, beta_s, dkt_acc, dvt_acc):
        i = pl.program_id(2)

        @pl.when(i == 0)
        def _():
            dkt_acc[...] = jnp.zeros_like(dkt_acc)
            dvt_acc[...] = jnp.zeros_like(dvt_acc)

        causal = _tri(ATT_T, True)
        upper2 = _twice(causal)
        lower2 = _twice(~causal)
        zb = zb_ref[...]
        sig = _sigmoid(zb)
        dyb_t = dyb_ref[...]
        do_f = dyb_t * (zb * sig)
        do = do_f.astype(BF16)
        do_t = do_f.T.astype(BF16)
        q_t = q_ref[...].astype(F32).T.astype(BF16)
        dzb_ref[...] = (dyb_t * o_ref[...] * (sig * (1.0 + zb * (1.0 - sig)))).astype(BF16)

        def sweep(kb, carries, mask):
            rows = pl.ds(pl.multiple_of(kb * ATT_T, ATT_T), ATT_T)
            heads = [slice(h * hd, (h + 1) * hd) for h in range(ATT_HP)]
            scores = [_sb_block(q_ref[:, cols], k_ref[rows, cols], scale, upper2, mask) for cols in heads]
            das = [lax.dot_general(do[:, cols], v_ref[rows, cols], NT, preferred_element_type=F32) for cols in heads]
            new = []
            for h, (cols, carry, (log_beta, sx, total), da) in enumerate(zip(heads, carries, scores, das)):
                a = jnp.exp2(log_beta + sx + carry)
                beta = jnp.exp2(log_beta)
                if mask is not None:
                    a = jnp.where(mask, a, 0.0)
                    beta = jnp.where(mask, beta, 0.0)
                g_s[h, kb] = a * da
                beta_s[h, kb] = beta
                dvt_acc[kb, cols, :] += jnp.dot(do_t[cols, :], a.astype(BF16), preferred_element_type=F32)
                new.append(carry + total)
            return tuple(new)

        carries = sweep(i, tuple(jnp.zeros((ATT_T, 1), F32) for _ in range(ATT_HP)), causal)

        def more(c):
            new = sweep(c[0], c[1], None)
            return c[0] - 1, new, _max_carry(new)

        last, _, _ = lax.while_loop(lambda c: (c[0] >= 0) & (c[2] > DEAD_LOG2), more, (i - 1, carries, _max_carry(carries)))
        first_kb = last + 1

        def back(kb, state):
            rows = pl.ds(pl.multiple_of(kb * ATT_T, ATT_T), ATT_T)
            heads = [slice(h * hd, (h + 1) * hd) for h in range(ATT_HP)]
            sums = [_cumsum_mm(g_s[h, kb], lower2) for h in range(ATT_HP)]
            new = []
            for h, (cols, (p_carry, dq), px) in enumerate(zip(heads, state, sums)):
                dz = ((g_s[h, kb] - (p_carry + px) * beta_s[h, kb]) * scale).astype(BF16)
                dq = dq + jnp.dot(dz, k_ref[rows, cols], preferred_element_type=F32)
                dkt_acc[kb, cols, :] += jnp.dot(q_t[cols, :], dz, preferred_element_type=F32)
                new.append((p_carry + px[:, ATT_T - 1:ATT_T], dq))
            return tuple(new)

        init = tuple((jnp.zeros((ATT_T, 1), F32), jnp.zeros((ATT_T, hd), F32)) for _ in range(ATT_HP))
        state = lax.fori_loop(first_kb, i + 1, back, init)
        for h in range(ATT_HP):
            dq_ref[:, h * hd:(h + 1) * hd] = state[h][1].astype(BF16)

        @pl.when(i == nq - 1)
        def _():
            for kb in range(nq):
                dk_ref[kb * ATT_T:(kb + 1) * ATT_T, :] = dkt_acc[kb].T.astype(BF16)
                dv_ref[kb * ATT_T:(kb + 1) * ATT_T, :] = dvt_acc[kb].T.astype(BF16)

    out = jax.ShapeDtypeStruct((t, d), BF16)
    hp_w = ATT_HP * hd
    return pl.pallas_call(
        body, name="attn_bwd",
        grid=(bsz, n_hp, nq),
        in_specs=[row_blk(0), seq_blk(1), seq_blk(2), row_blk(ZB_GROUP), row_blk(0), row_blk(0)],
        out_specs=[row_blk(0), seq_blk(0), seq_blk(0), row_blk(0)],
        out_shape=[out, out, out, out],
        scratch_shapes=[pltpu.VMEM((ATT_HP, nq, ATT_T, ATT_T), F32), pltpu.VMEM((ATT_HP, nq, ATT_T, ATT_T), F32),
                        pltpu.VMEM((nq, hp_w, ATT_T), F32), pltpu.VMEM((nq, hp_w, ATT_T), F32)],
        compiler_params=_params(("arbitrary", "arbitrary", "arbitrary")),
    )(qkv, qkv, qkv, proj, o, dyb)


def _out_proj(ya, yb, g_pre, x2d, tgt, w_og, w_osb, w_out, g_f):
    t, d = x2d.shape

    def body(ya_ref, yb_ref, ga_ref, gb_ref, x_ref, tgt_ref, wog_ref, wosb_ref, wout_ref, gf_ref,
             dya_ref, dyb_ref, dg_ref, dx2_ref, loss_ref, gnf_ref, gwog_ref, gwosb_ref, gwout_ref):
        @pl.when(pl.program_id(0) == 0)
        def _():
            loss_ref[...] = jnp.zeros_like(loss_ref)
            gnf_ref[...] = jnp.zeros_like(gnf_ref)
            gwog_ref[...] = jnp.zeros_like(gwog_ref)
            gwosb_ref[...] = jnp.zeros_like(gwosb_ref)
            gwout_ref[...] = jnp.zeros_like(gwout_ref)

        ya = ya_ref[...]
        yb = yb_ref[...]
        pa = jnp.dot(ya, wog_ref[...], preferred_element_type=F32)
        pb = jnp.dot(yb, wosb_ref[...], preferred_element_type=F32)
        sga = _sigmoid(ga_ref[...])
        sgb = _sigmoid(gb_ref[...])
        merged = (sga * pa + sgb * pb).astype(BF16)
        x2 = x_ref[...] + jnp.dot(merged, wout_ref[...], preferred_element_type=F32)
        r2 = lax.rsqrt(jnp.mean(x2 * x2, axis=-1, keepdims=True) + EPS)
        n2 = x2 * r2
        gf = gf_ref[...]
        err = n2 * gf - tgt_ref[...]
        loss_ref[...] += 0.5 * jnp.sum(jnp.sum(err * err, axis=-1, keepdims=True), axis=0, keepdims=True) / d
        dy = err * (1.0 / d)
        gnf_ref[...] += jnp.sum(dy * n2, axis=0, keepdims=True)
        dn = dy * gf
        dx2 = r2 * (dn - n2 * jnp.mean(dn * n2, axis=-1, keepdims=True))
        dx2_ref[...] = dx2
        dx2_b = dx2.astype(BF16)
        dmerged = lax.dot_general(dx2_b, wout_ref[...], NT, preferred_element_type=F32)
        gwout_ref[...] += lax.dot_general(merged, dx2_b, TN, preferred_element_type=F32)
        dg_ref[:, 0:d] = (dmerged * pa * (sga * (1.0 - sga))).astype(BF16)
        dg_ref[:, d:2 * d] = (dmerged * pb * (sgb * (1.0 - sgb))).astype(BF16)
        dpa = (dmerged * sga).astype(BF16)
        dpb = (dmerged * sgb).astype(BF16)
        dya_ref[...] = lax.dot_general(dpa, wog_ref[...], NT, preferred_element_type=F32)
        dyb_ref[...] = lax.dot_general(dpb, wosb_ref[...], NT, preferred_element_type=F32)
        gwog_ref[...] += lax.dot_general(ya, dpa, TN, preferred_element_type=F32)
        gwosb_ref[...] += lax.dot_general(yb, dpb, TN, preferred_element_type=F32)

    row = lambda i: (i, 0)
    return pl.pallas_call(
        body, name="out_proj",
        grid=(t // TM,),
        in_specs=[pl.BlockSpec((TM, d), row), pl.BlockSpec((TM, d), row),
                  pl.BlockSpec((TM, d), lambda i: (i, GA_GROUP)), pl.BlockSpec((TM, d), lambda i: (i, GB_GROUP)),
                  pl.BlockSpec((TM, d), row), pl.BlockSpec((TM, d), row),
                  _resident((d, d)), _resident((d, d)), _resident((d, d)), _const((1, d))],
        out_specs=[pl.BlockSpec((TM, d), row), pl.BlockSpec((TM, d), row), pl.BlockSpec((TM, 2 * d), row),
                   pl.BlockSpec((TM, d), row), _const((1, 1)), _const((1, d)),
                   _const((d, d)), _const((d, d)), _const((d, d))],
        out_shape=[jax.ShapeDtypeStruct((t, d), F32), jax.ShapeDtypeStruct((t, d), F32),
                   jax.ShapeDtypeStruct((t, 2 * d), BF16), jax.ShapeDtypeStruct((t, d), F32),
                   jax.ShapeDtypeStruct((1, 1), F32), jax.ShapeDtypeStruct((1, d), F32),
                   jax.ShapeDtypeStruct((d, d), F32), jax.ShapeDtypeStruct((d, d), F32),
                   jax.ShapeDtypeStruct((d, d), F32)],
        compiler_params=_params(("arbitrary",)),
    )(ya, yb, g_pre, g_pre, x2d, tgt, w_og, w_osb, w_out, g_f)


def _dproj_pieces(d):
    return [(0, 0, 3), (1, 3, 1), (2, 4, 1), (3, 5, 1), (4, 6, 1), (5, 7, 2)]


def _in_proj_bwd_x(pieces, wg, x2d, g_in, dx2, gw16, received):
    t, d = x2d.shape
    ncol = wg.shape[2]
    segs = _segments(d, ncol)
    layout = _dproj_pieces(d)
    nsteps = t // TM

    def body(da_ref, dq_ref, dk_ref, dv_ref, dzb_ref, dg_ref, w_ref, x_ref, g_ref, dx2_ref, gw16_ref, recv_in,
             gx_ref, gn_ref, recv_ref, send_sems, recv_sems):
        x_pos, y_pos, c_pos, chips = _place()
        my_chip = 2 * x_pos + y_pos

        def share(k, chunk):
            px, py = chips[k]
            return pltpu.make_async_remote_copy(
                src_ref=gw16_ref.at[:, chunk * ncol:(chunk + 1) * ncol], dst_ref=recv_ref.at[k],
                send_sem=send_sems.at[k], recv_sem=recv_sems.at[k], device_id=(px, py, c_pos), device_id_type=MESH)

        def for_late(action):
            for k, (px, py) in enumerate(chips):
                for chunk in LATE_CHUNKS:
                    @pl.when(2 * px + py == chunk)
                    def _(k=k, chunk=chunk):
                        action(share(k, chunk))

        @pl.when(pl.program_id(0) == 0)
        def _():
            gn_ref[...] = jnp.zeros_like(gn_ref)
            for_late(lambda cp: cp.start())

        @pl.when(pl.program_id(0) == nsteps - 1)
        def _():
            for_late(lambda cp: cp.wait_send())

            @pl.when(functools.reduce(jnp.logical_or, [my_chip == chunk for chunk in LATE_CHUNKS]))
            def _():
                for k in range(N_CHIP - 1):
                    share(k, 0).wait_recv()

        refs = (da_ref, dq_ref, dk_ref, dv_ref, dzb_ref, dg_ref)
        dh = jnp.zeros((TM, d), F32)
        for chip, c0, grp, s0, width in segs:
            piece, first, _ = next(p for p in layout if p[1] <= grp < p[1] + p[2])
            off = (grp - first) * d + s0
            dh = dh + lax.dot_general(refs[piece][:, off:off + width], w_ref[chip, :, c0:c0 + width], NT,
                                      preferred_element_type=F32)
        x = x_ref[...]
        r = lax.rsqrt(jnp.mean(x * x, axis=-1, keepdims=True) + EPS)
        n = x * r
        gn_ref[...] += jnp.sum(dh * n, axis=0, keepdims=True)
        dn = dh * g_ref[...]
        gx_ref[...] = dx2_ref[...] + r * (dn - n * jnp.mean(dn * n, axis=-1, keepdims=True))

    row = lambda i: (i, 0)
    return pl.pallas_call(
        body, name="in_proj_bwd_x",
        grid=(t // TM,),
        in_specs=[pl.BlockSpec((TM, p.shape[1]), row) for p in pieces]
        + [_resident(wg.shape), pl.BlockSpec((TM, d), row), _const((1, d)), pl.BlockSpec((TM, d), row), ANY, ANY],
        out_specs=[pl.BlockSpec((TM, d), row), _const((1, d)), ANY],
        out_shape=[jax.ShapeDtypeStruct((t, d), F32), jax.ShapeDtypeStruct((1, d), F32),
                   jax.ShapeDtypeStruct(received.shape, received.dtype)],
        input_output_aliases={len(pieces) + 5: 2},
        scratch_shapes=[pltpu.SemaphoreType.DMA((N_CHIP - 1,)), pltpu.SemaphoreType.DMA((N_CHIP - 1,))],
        compiler_params=_params(("arbitrary",)),
    )(*pieces, wg, x2d, g_in, dx2, gw16, received)


EARLY_CHUNKS = (0, 1)
LATE_CHUNKS = (2, 3)


def _group_order(d, ncol):
    def chunks_of(g):
        return {c for c in range(N_CHIP) if min((c + 1) * ncol, (g + 1) * d) > max(c * ncol, g * d)}
    early = [g for g in range(N_SPLIT) if chunks_of(g) & set(EARLY_CHUNKS)]
    rest = [g for g in range(N_SPLIT) if g not in early]
    second_first = sorted(early, key=lambda g: EARLY_CHUNKS[-1] not in chunks_of(g))
    return [early + rest, second_first + rest]


def _in_proj_bwd_w(h_t, pieces, mats16, pack, ncol):
    d, t = h_t.shape
    nk = t // TKW
    layout = _dproj_pieces(d)
    segs = _segments(d, ncol)
    n_mats = len(mats16)
    n_dev = 2 * N_CHIP
    flips = [(dx, dy, dc) for dx in (0, 1) for dy in (0, 1) for dc in (0, 1)][1:]

    def body(order_ref, ht_ref, da_ref, dq_ref, dk_ref, dv_ref, dzb_ref, dg_ref, *rest):
        mat_refs, pack_ref = rest[:n_mats], rest[n_mats]
        gw_ref, gw16_ref = rest[n_mats + 1:n_mats + 3]
        recv_refs, slots_ref, recv_w = rest[n_mats + 3:2 * n_mats + 3], rest[2 * n_mats + 3], rest[2 * n_mats + 4]
        acc, stage, mat_send, mat_recv, pack_send, pack_recv, own_sem, seg_send, w_recv = rest[2 * n_mats + 5:]
        s = pl.program_id(0)
        i = pl.program_id(1)
        group = order_ref[s]
        x_pos, y_pos, c_pos, chips = _place()
        me = 4 * x_pos + 2 * y_pos + c_pos
        my_chip = 2 * x_pos + y_pos

        def for_shares(which, parity, action):
            for g in range(N_SPLIT):
                for n_seg, (chunk, c0, _, s0, width) in enumerate([sg for sg in segs if sg[2] == g]):
                    if chunk not in EARLY_CHUNKS:
                        continue
                    for k, (px, py) in enumerate(chips):
                        @pl.when((which == g) & (2 * px + py == chunk))
                        def _(n_seg=n_seg, c0=c0, s0=s0, width=width, k=k, px=px, py=py):
                            action(pltpu.make_async_remote_copy(
                                src_ref=stage.at[parity, :, s0:s0 + width], dst_ref=recv_w.at[k, :, c0:c0 + width],
                                send_sem=seg_send.at[parity, n_seg], recv_sem=w_recv.at[k],
                                device_id=(px, py, c_pos), device_id_type=MESH))

        def exchanges():
            cps = []
            for k, (px, py) in enumerate(chips):
                for a in range(n_mats):
                    cps.append(pltpu.make_async_remote_copy(
                        src_ref=mat_refs[a].at[2 * px + py], dst_ref=recv_refs[a].at[k],
                        send_sem=mat_send.at[a, k], recv_sem=mat_recv.at[a, k],
                        device_id=(px, py, c_pos), device_id_type=MESH))
            for k, (dx, dy, dc) in enumerate(flips):
                peer = (1 - x_pos if dx else x_pos, 1 - y_pos if dy else y_pos, 1 - c_pos if dc else c_pos)
                cps.append(pltpu.make_async_remote_copy(
                    src_ref=pack_ref, dst_ref=slots_ref.at[me], send_sem=pack_send.at[k], recv_sem=pack_recv.at[k],
                    device_id=peer, device_id_type=MESH))
            return cps, pltpu.make_async_copy(pack_ref, slots_ref.at[me], own_sem)

        @pl.when((s == 0) & (i == 0))
        def _():
            cps, own = exchanges()
            own.start()
            for cp in cps:
                cp.start()

        @pl.when(i == 0)
        def _():
            acc[...] = jnp.zeros_like(acc)

        refs = (da_ref, dq_ref, dk_ref, dv_ref, dzb_ref, dg_ref)
        for piece, first, count in layout:
            @pl.when((group >= first) & (group < first + count))
            def _(piece=piece):
                acc[...] += jnp.dot(ht_ref[...], refs[piece][...], preferred_element_type=F32)

        @pl.when(i == nk - 1)
        def _():
            gw_ref[...] = acc[...]
            gw16_ref[...] = acc[...].astype(BF16)
            for slot in range(STAGE_SLOTS):
                @pl.when(s % STAGE_SLOTS == slot)
                def _(slot=slot):
                    @pl.when(s >= STAGE_SLOTS)
                    def _():
                        for_shares(order_ref[s - STAGE_SLOTS], slot, lambda cp: cp.wait_send())
                    stage[slot] = acc[...].astype(BF16)
                    for_shares(group, slot, lambda cp: cp.start())

        @pl.when((s == N_SPLIT - 1) & (i == nk - 1))
        def _():
            for late in range(N_SPLIT - STAGE_SLOTS, N_SPLIT):
                for_shares(order_ref[late], late % STAGE_SLOTS, lambda cp: cp.wait_send())
            @pl.when(functools.reduce(jnp.logical_or, [my_chip == chunk for chunk in EARLY_CHUNKS]))
            def _():
                for k in range(N_CHIP - 1):
                    pltpu.make_async_remote_copy(
                        src_ref=recv_w.at[k], dst_ref=recv_w.at[k], send_sem=seg_send.at[0, 0], recv_sem=w_recv.at[k],
                        device_id=(x_pos, y_pos, c_pos), device_id_type=MESH).wait_recv()
            cps, own = exchanges()
            own.wait()
            for cp in cps:
                cp.wait()

    def piece_spec(p, first, count):
        def index(s, i, order):
            mine = (order[s] >= first) & (order[s] < first + count)
            return jnp.where(mine, i, 0), jnp.where(mine, order[s] - first, 0)
        return pl.BlockSpec((TKW, d), index)

    order = jnp.asarray(_group_order(d, ncol), jnp.int32)[lax.axis_index("c")]
    col_blk = pl.BlockSpec((d, d), lambda s, i, order: (0, order[s]))
    outs = pl.pallas_call(
        body, name="in_proj_bwd_w",
        grid_spec=pltpu.PrefetchScalarGridSpec(
            num_scalar_prefetch=1, grid=(N_SPLIT, nk),
            in_specs=[pl.BlockSpec((d, TKW), lambda s, i, order: (0, i))] + [piece_spec(*p) for p in layout]
            + [ANY] * (n_mats + 1),
            out_specs=[col_blk, col_blk] + [ANY] * (n_mats + 2),
            scratch_shapes=[pltpu.VMEM((d, d), F32), pltpu.VMEM((STAGE_SLOTS, d, d), BF16),
                            pltpu.SemaphoreType.DMA((n_mats, N_CHIP - 1)), pltpu.SemaphoreType.DMA((n_mats, N_CHIP - 1)),
                            pltpu.SemaphoreType.DMA((n_dev - 1,)), pltpu.SemaphoreType.DMA((n_dev - 1,)),
                            pltpu.SemaphoreType.DMA,
                            pltpu.SemaphoreType.DMA((STAGE_SLOTS, 2)), pltpu.SemaphoreType.DMA((N_CHIP - 1,))]),
        out_shape=[jax.ShapeDtypeStruct((d, N_SPLIT * d), F32), jax.ShapeDtypeStruct((d, N_SPLIT * d), BF16)]
        + [jax.ShapeDtypeStruct((N_CHIP - 1,) + m.shape[1:], BF16) for m in mats16]
        + [jax.ShapeDtypeStruct((n_dev,) + pack.shape, F32), jax.ShapeDtypeStruct((N_CHIP - 1, d, ncol), BF16)],
        compiler_params=_params(("arbitrary", "arbitrary")),
    )(order, h_t, *pieces, *mats16, pack)
    return outs[0], outs[1], outs[2:2 + n_mats], outs[2 + n_mats], outs[3 + n_mats]


def _local_step(proj, qkv, x2d, tgt2d, bsz, seq, norm_v, w_s, b_s, w_og, w_osb, w_out, norm_final):
    d = x2d.shape[1]
    chunk = w_s.shape[-1]
    causal = jnp.tril(jnp.ones((chunk, chunk), dtype=bool))
    wm = jnp.where(causal[None], w_s, 0.0).astype(BF16)
    wm_t = jnp.swapaxes(wm, 1, 2)
    b_t = b_s.T

    ya = _branch_a_fwd(proj, norm_v, wm, b_t)
    o, yb = _attn_fwd(qkv, proj, bsz, seq)
    dya, dyb, dg, dx2, loss, g_nf, g_wog, g_wosb, g_wout = _out_proj(
        ya, yb, proj, x2d, tgt2d, w_og, w_osb, w_out, norm_final.reshape(1, d))
    dq, dk, dv, dzb = _attn_bwd(qkv, proj, o, dyb, bsz, seq)
    d_a, g_ws, g_bt, g_nv = _branch_a_bwd(proj, dya, norm_v, wm, wm_t, b_t)
    g_ws = jnp.where(causal[None], g_ws, 0.0)
    return loss, (d_a, dq, dk, dv, dzb, dg), dx2, g_nv, g_ws, g_bt.T, g_wog, g_wosb, g_wout, g_nf


def _row_tile(rows):
    return next(r for r in (128, 64, 32, 16, 8) if rows % r == 0)


def _cast_bf16(arrs):
    n = len(arrs)

    def body(*refs):
        for a_ref, o_ref in zip(refs[:n], refs[n:]):
            o_ref[...] = a_ref[...].astype(BF16)

    specs = [pl.BlockSpec((a.shape[0] // CAST_STEPS, a.shape[1]), lambda i: (i, 0)) for a in arrs]
    return pl.pallas_call(
        body, name="cast_bf16", grid=(CAST_STEPS,),
        in_specs=specs, out_specs=specs,
        out_shape=[jax.ShapeDtypeStruct(a.shape, BF16) for a in arrs],
        compiler_params=_params(("arbitrary",)),
    )(*arrs)


def _add_received(full, recv, chip, by_cols):
    _, rows, cols = recv.shape
    tr = _row_tile(rows)
    nb = rows // tr

    def body(chip_ref, own_ref, recv_ref, o_ref):
        s = own_ref[...]
        for k in range(N_CHIP - 1):
            s = s + recv_ref[k].astype(F32)
        o_ref[...] = s

    own_map = (lambda i, chip_ref: (i, chip_ref[0])) if by_cols else (lambda i, chip_ref: (chip_ref[0] * nb + i, 0))
    return pl.pallas_call(
        body, name="add_received",
        grid_spec=pltpu.PrefetchScalarGridSpec(
            num_scalar_prefetch=1, grid=(nb,),
            in_specs=[pl.BlockSpec((tr, cols), own_map),
                      pl.BlockSpec((N_CHIP - 1, tr, cols), lambda i, chip_ref: (0, i, 0))],
            out_specs=pl.BlockSpec((tr, cols), lambda i, chip_ref: (i, 0))),
        out_shape=jax.ShapeDtypeStruct((rows, cols), F32),
        compiler_params=_params(("arbitrary",)),
    )(chip.reshape(1).astype(jnp.int32), full, recv)


def _adamw_math(w, m, v, g):
    new_m = ADAM_B1 * m + (1.0 - ADAM_B1) * g
    new_v = ADAM_B2 * v + (1.0 - ADAM_B2) * (g * g)
    m_hat = new_m / (1.0 - ADAM_B1 ** ADAM_STEP)
    v_hat = new_v / (1.0 - ADAM_B2 ** ADAM_STEP)
    return -ADAM_LR * (m_hat / (jnp.sqrt(v_hat) + ADAM_EPS) + ADAM_WD * w), new_m, new_v


def _adamw(w, m, v, g_parts):
    rows, cols = w.shape
    tr = _row_tile(rows)
    n_parts = len(g_parts)

    def body(*refs):
        w_ref, m_ref, v_ref = refs[:3]
        part_refs = refs[3:3 + n_parts]
        g_ref, d_ref, nm_ref, nv_ref = refs[3 + n_parts:]
        g = part_refs[0][...]
        for p in part_refs[1:]:
            g = g + p[...]
        g_ref[...] = g
        d_ref[...], nm_ref[...], nv_ref[...] = _adamw_math(w_ref[...], m_ref[...], v_ref[...], g)

    spec = pl.BlockSpec((tr, cols), lambda i: (i, 0))
    out = jax.ShapeDtypeStruct(w.shape, F32)
    return pl.pallas_call(
        body, name="adamw", grid=(rows // tr,),
        in_specs=[spec] * (3 + n_parts), out_specs=[spec] * 4, out_shape=[out] * 4,
        compiler_params=_params(("arbitrary",)),
    )(w, m, v, *g_parts)


def _adamw_small(w, m, v, slots_head, slots_tail):
    n_dev, p0, _ = slots_head.shape

    def body(w_ref, m_ref, v_ref, head_ref, tail_ref, g_ref, d_ref, nm_ref, nv_ref):
        for ref, rows in ((head_ref, slice(0, p0)), (tail_ref, slice(p0, w.shape[0]))):
            g = ref[0]
            for i in range(1, n_dev):
                g = g + ref[i]
            g_ref[rows, :] = g
            d_ref[rows, :], nm_ref[rows, :], nv_ref[rows, :] = _adamw_math(w_ref[rows, :], m_ref[rows, :], v_ref[rows, :], g)

    vmem = pl.BlockSpec(memory_space=pltpu.VMEM)
    out = jax.ShapeDtypeStruct(w.shape, F32)
    return pl.pallas_call(
        body, name="adamw_small", in_specs=[vmem] * 5, out_specs=[vmem] * 4, out_shape=[out] * 4,
        compiler_params=pltpu.CompilerParams(vmem_limit_bytes=VMEM_LIMIT),
    )(w, m, v, slots_head, slots_tail)


ANY = pl.BlockSpec(memory_space=pl.ANY)


def _place():
    x, y, c = lax.axis_index("x"), lax.axis_index("y"), lax.axis_index("c")
    other_chips = [(1 - x, y), (x, 1 - y), (1 - x, 1 - y)]
    return x, y, c, other_chips


def _swap_and_gather(arrs, pack):
    n = len(arrs)
    n_dev = 2 * N_CHIP
    flips = [(dx, dy, dc) for dx in (0, 1) for dy in (0, 1) for dc in (0, 1)][1:]

    def body(*refs):
        ins, pack_ref = refs[:n], refs[n]
        outs, slots_ref = refs[n + 1:2 * n + 1], refs[2 * n + 1]
        send_sems, recv_sems, pack_send, pack_recv, own_sem = refs[2 * n + 2:]
        x, y, c, _ = _place()
        me = 4 * x + 2 * y + c
        own = pltpu.make_async_copy(pack_ref, slots_ref.at[me], own_sem)
        own.start()
        copies = []
        for k, (dx, dy, dc) in enumerate(flips):
            peer = (1 - x if dx else x, 1 - y if dy else y, 1 - c if dc else c)
            copies.append(pltpu.make_async_remote_copy(
                src_ref=pack_ref, dst_ref=slots_ref.at[me], send_sem=pack_send.at[k], recv_sem=pack_recv.at[k],
                device_id=peer, device_id_type=MESH))
        copies += [pltpu.make_async_remote_copy(
            src_ref=ins[a], dst_ref=outs[a], send_sem=send_sems.at[a], recv_sem=recv_sems.at[a],
            device_id=(x, y, 1 - c), device_id_type=MESH) for a in range(n)]
        for cp in copies:
            cp.start()
        for cp in copies:
            cp.wait()
        own.wait()

    return pl.pallas_call(
        body, name="swap_and_gather",
        in_specs=[ANY] * (n + 1), out_specs=[ANY] * (n + 1),
        out_shape=[jax.ShapeDtypeStruct(a.shape, a.dtype) for a in arrs] + [jax.ShapeDtypeStruct((n_dev,) + pack.shape, F32)],
        scratch_shapes=[pltpu.SemaphoreType.DMA((n,)), pltpu.SemaphoreType.DMA((n,)),
                        pltpu.SemaphoreType.DMA((n_dev - 1,)), pltpu.SemaphoreType.DMA((n_dev - 1,)),
                        pltpu.SemaphoreType.DMA],
    )(*arrs, pack)


SLAB = 8


def _slab(vec, d):
    return jnp.pad(vec.reshape(1, d), ((0, SLAB - 1), (0, 0)))


def _pack_tail(vec_nv, b_s, vec_nf, w_s, scalar=None):
    d = vec_nv.shape[-1]
    extra = jnp.zeros((1, d), F32) if scalar is None else jnp.pad(scalar, ((0, 0), (0, d - 1)))
    return jnp.concatenate([_slab(v, d) for v in (vec_nv, b_s, vec_nf, extra)] + [w_s.reshape(-1, d)], axis=0)


def _pack_small(vec_nin, vec_nv, b_s, vec_nf, w_s, scalar=None):
    return jnp.concatenate([_slab(vec_nin, vec_nin.shape[-1]), _pack_tail(vec_nv, b_s, vec_nf, w_s, scalar)], axis=0)


def _unpack_small(pack, w_s_shape, b_s_shape):
    return (pack[0:1], pack[SLAB:SLAB + 1], pack[2 * SLAB].reshape(b_s_shape), pack[3 * SLAB],
            pack[5 * SLAB:].reshape(w_s_shape), pack[4 * SLAB, 0])


def kernel(x, norm_in, w_in, norm_v, w_s, b_s, w_o_gmlp, w_o_sb, w_out, norm_final, loss_target, m_norm_in, m_w_in, m_norm_v, m_w_s, m_b_s, m_w_o_gmlp, m_w_o_sb, m_w_out, m_norm_final, v_norm_in, v_w_in, v_norm_v, v_w_s, v_b_s, v_w_o_gmlp, v_w_o_sb, v_w_out, v_norm_final):
    d = x.shape[-1]
    ncol = w_in.shape[-1]
    nrow = w_o_gmlp.shape[-2]
    chip = 2 * lax.axis_index("x") + lax.axis_index("y")

    bsz, seq, _ = x.shape
    x2d = x.reshape(bsz * seq, d)
    shards = [w_in[0], w_o_gmlp[0], w_o_sb[0], w_out[0]]
    halves = [s16.reshape(2, s16.shape[0] // 2, s16.shape[1]) for s16 in _cast_bf16(shards)]
    proj, qkv, h_t, (wg, w_og, w_osb, w_o) = _in_proj_fwd(x2d, norm_in, halves)
    wg = wg.reshape(N_CHIP, d, ncol)

    loss, pieces, dx2, g_nv, g_ws, g_bs, g_wog, g_wosb, g_wout, g_nf = _local_step(
        proj, qkv, x2d, loss_target.reshape(bsz * seq, d), bsz, seq, norm_v, w_s[0], b_s[0],
        w_og.reshape(d, d), w_osb.reshape(d, d), w_o.reshape(d, d), norm_final)

    mats = [g_wog, g_wosb, g_wout]
    mats16 = [g16.reshape(N_CHIP, nrow, d) for g16 in _cast_bf16(mats)]
    g_win, g_win16, recv_mats, slots_tail, recv_win = _in_proj_bwd_w(
        h_t, pieces, mats16, _pack_tail(g_nv, g_bs, g_nf, g_ws, loss), ncol)
    grad_x, g_nin, recv_win = _in_proj_bwd_x(pieces, wg, x2d, norm_in, dx2, g_win16, recv_win)
    grad_x = grad_x.reshape(bsz, seq, d)

    sums = [_add_received(g_win, recv_win, chip, True)] + [
        _add_received(g, r, chip, False) for g, r in zip(mats, recv_mats)]
    *sibling_sums, slots_head = _swap_and_gather(sums, _slab(g_nin, d))
    stats = []
    for w, m, v, mine, theirs in zip(shards, [m_w_in[0], m_w_o_gmlp[0], m_w_o_sb[0], m_w_out[0]],
                                     [v_w_in[0], v_w_o_gmlp[0], v_w_o_sb[0], v_w_out[0]], sums, sibling_sums):
        stats.append(_adamw(w, m, v, [mine, theirs]))
    (gw_in, dw_in, nm_in, nv_in), (gw_og, dw_og, nm_og, nv_og), (gw_osb, dw_osb, nm_osb, nv_osb), \
        (gw_out, dw_out, nm_out, nv_out) = stats

    gs, ds, ms, vs = _adamw_small(
        _pack_small(norm_in, norm_v, b_s[0], norm_final, w_s[0]),
        _pack_small(m_norm_in, m_norm_v, m_b_s[0], m_norm_final, m_w_s[0]),
        _pack_small(v_norm_in, v_norm_v, v_b_s[0], v_norm_final, v_w_s[0]), slots_head, slots_tail)

    def small(pack):
        nin, nv, bs, nf, ws, _ = _unpack_small(pack, w_s.shape, b_s.shape)
        return nin, nv, ws, bs, nf

    loss = _unpack_small(gs, w_s.shape, b_s.shape)[-1]
    out = []
    for small_pack, win, wog, wosb, wout in ((gs, gw_in, gw_og, gw_osb, gw_out), (ds, dw_in, dw_og, dw_osb, dw_out),
                                             (ms, nm_in, nm_og, nm_osb, nm_out), (vs, nv_in, nv_og, nv_osb, nv_out)):
        nin, nv, ws, bs, nf = small(small_pack)
        out += [nin, win[None], nv, ws, bs, wog[None], wosb[None], wout[None], nf]
    return (loss, grad_x, *out)
```

```python
import functools
import math

import jax
import jax.numpy as jnp
from jax import lax
from jax.experimental import pallas as pl
from jax.experimental.pallas import tpu as pltpu

F32 = jnp.float32
BF16 = jnp.bfloat16
EPS = 1e-6
HEADS = 8
N_SPLIT = 9
N_CHIP = 4
MESH = pl.DeviceIdType.MESH

ADAM_LR = 0.001
ADAM_B1 = 0.9
ADAM_B2 = 0.999
ADAM_EPS = 1e-08
ADAM_WD = 0.01
ADAM_STEP = 10

VMEM_LIMIT = 56 * 2 ** 20
TM = 256
TMF = 512
ATT_T = 256
ATT_HP = 4
TKW = 1024
CHUNKS_PER_STEP = 4
CAST_STEPS = 8

NT = (((1,), (1,)), ((), ()))
TN = (((0,), (0,)), ((), ()))


def _params(sem):
    return pltpu.CompilerParams(dimension_semantics=sem, vmem_limit_bytes=VMEM_LIMIT)


def _resident(shape):
    nd = len(shape)
    return pl.BlockSpec(shape, lambda *_: (0,) * nd, pipeline_mode=pl.Buffered(1))


def _const(shape):
    nd = len(shape)
    return pl.BlockSpec(shape, lambda *_: (0,) * nd)


def _segments(d, ncol):
    segs = []
    edges = sorted({j * ncol for j in range(N_CHIP + 1)} | {s * d for s in range(N_SPLIT + 1)})
    for lo, hi in zip(edges[:-1], edges[1:]):
        segs.append((lo // ncol, lo % ncol, lo // d, lo % d, hi - lo))
    return segs


def _sigmoid(x):
    return 0.5 * jnp.tanh(0.5 * x) + 0.5


_GELU_C = math.sqrt(2.0 / math.pi)


_GELU_CA = _GELU_C * 0.044715


def _gelu(x):
    return x * (0.5 * jnp.tanh(x * (_GELU_C + _GELU_CA * (x * x))) + 0.5)


def _gelu_and_grad(x):
    x2 = x * x
    u = 0.5 * jnp.tanh(x * (_GELU_C + _GELU_CA * x2)) + 0.5
    slope = (1.0 - u) * (x * (_GELU_C + (3.0 * _GELU_CA) * x2))
    return x * u, u * (2.0 * slope + 1.0)


def _split_bf16(a):
    hi = a.astype(BF16)
    lo = (a - hi.astype(F32)).astype(BF16)
    return hi, lo


def _in_proj_fwd(x2d, g_in, halves):
    t, d = x2d.shape
    n = len(halves)
    ncol = halves[0].shape[2]
    n_row = t // TMF
    last = n_row - 1
    assert halves[0].shape[1] * 2 == d
    qkv_parts = {j: (max(j * ncol, 3 * d) - j * ncol, max(j * ncol, 3 * d) - 3 * d)
                 for j in range(N_CHIP) if min((j + 1) * ncol, 6 * d) > max(j * ncol, 3 * d)}
    qkv_w = 3 * d // len(qkv_parts)
    assert all(min((j + 1) * ncol, 6 * d) - max(j * ncol, 3 * d) == qkv_w and q0 % qkv_w == 0
               for j, (_, q0) in qkv_parts.items())

    def body(order_ref, qrow_ref, qcol_ref, x_ref, g_ref, *rest):
        ins = rest[:n]
        proj_ref, qkv_ref, ht_ref = rest[n:n + 3]
        outs = rest[n + 3:2 * n + 3]
        wbuf, h_all, send_sems, recv_sems, local_sems, load_sem = rest[2 * n + 3:]
        phase = pl.program_id(0)
        i = pl.program_id(1)
        x_pos, y_pos, c_pos, chips = _place()
        sibling = (x_pos, y_pos, 1 - c_pos)
        me = (x_pos, y_pos, c_pos)
        my_chip = 2 * x_pos + y_pos

        def copy(a, k, block, to, src=None):
            return pltpu.make_async_remote_copy(
                src_ref=outs[a].at[block] if src is None else src, dst_ref=outs[a].at[block],
                send_sem=send_sems.at[a, k], recv_sem=recv_sems.at[a, k], device_id=to, device_id_type=MESH)

        def local(a):
            return pltpu.make_async_copy(ins[a], outs[a].at[pl.ds(2 * my_chip, 2)], local_sems.at[a])

        def load(src, first, slot):
            for half in range(2):
                cp = pltpu.make_async_copy(src.at[first + half], wbuf.at[slot, pl.ds(half * (d // 2), d // 2)], load_sem)
                cp.start()
                cp.wait()

        def send_mine(k):
            px, py = chips[k]
            for a in range(n):
                copy(a, k, 2 * my_chip + c_pos, (px, py, c_pos), src=ins[a].at[c_pos]).start()

        @pl.when((phase == 0) & (i == 0))
        def _():
            for a in range(n):
                local(a).start()
            send_mine(0)
            send_mine(1)
            load(ins[0], 0, 0)

        for k, (px, py) in enumerate(chips):
            @pl.when((phase == k + 1) & (i == 0))
            def _(k=k, px=px, py=py):
                theirs = 2 * (2 * px + py)
                for a in range(n):
                    copy(a, k, theirs + c_pos, me).wait_recv()
                    copy(a, 3 + k, theirs + c_pos, sibling).start()
                if k == 0:
                    send_mine(2)
                for a in range(n):
                    copy(a, 3 + k, theirs + 1 - c_pos, me).wait_recv()
                load(outs[0], theirs, (k + 1) % 2)

        @pl.when(phase == 0)
        def _():
            x = x_ref[...]
            r = lax.rsqrt(jnp.mean(x * x, axis=-1, keepdims=True) + EPS)
            hf = x * r * g_ref[...]
            h_all[i] = hf.astype(BF16)
            ht_ref[...] = hf.T.astype(BF16)

        for slot in range(2):
            @pl.when(phase % 2 == slot)
            def _(slot=slot):
                proj_ref[...] = jnp.dot(h_all[i], wbuf[slot], preferred_element_type=F32)

        for chunk, (c0, _) in qkv_parts.items():
            @pl.when(order_ref[phase] == chunk)
            def _(c0=c0):
                qkv_ref[...] = proj_ref[:, c0:c0 + qkv_w].astype(BF16)

        @pl.when((phase == N_CHIP - 1) & (i == n_row - 1))
        def _():
            for a in range(n):
                for k in range(2 * (N_CHIP - 1)):
                    copy(a, k, 0, me).wait_send()
                local(a).wait()

    x_pos, y_pos = lax.axis_index("x"), lax.axis_index("y")
    order = jnp.stack([2 * x_pos + y_pos, 2 * (1 - x_pos) + y_pos, 2 * x_pos + 1 - y_pos,
                       2 * (1 - x_pos) + 1 - y_pos]).astype(jnp.int32)
    holds = [functools.reduce(jnp.logical_or, [order[p] == j for j in qkv_parts]) for p in range(N_CHIP)]
    col = [sum(jnp.where(order[p] == j, q0 // qkv_w, 0) for j, (_, q0) in qkv_parts.items()) for p in range(N_CHIP)]
    cur = col[-1]
    for p in reversed(range(N_CHIP - 1)):
        cur = jnp.where(holds[p], col[p], cur)
    seen = jnp.bool_(False)
    qrow, qcol = [], []
    for p in range(N_CHIP):
        cur = jnp.where(holds[p], col[p], cur)
        qrow.append(jnp.where(holds[p], -1, jnp.where(seen, last, 0)))
        qcol.append(cur)
        seen = seen | holds[p]
    qrow = jnp.stack(qrow).astype(jnp.int32)
    qcol = jnp.stack(qcol).astype(jnp.int32)

    outs = pl.pallas_call(
        body, name="in_proj_fwd",
        grid_spec=pltpu.PrefetchScalarGridSpec(
            num_scalar_prefetch=3, grid=(N_CHIP, n_row),
            in_specs=[pl.BlockSpec((TMF, d), lambda p, i, order, qrow, qcol: (jnp.where(p == 0, i, last), 0)),
                      pl.BlockSpec((1, d), lambda p, i, order, qrow, qcol: (0, 0))] + [ANY] * n,
            out_specs=[pl.BlockSpec((TMF, ncol), lambda p, i, order, qrow, qcol: (i, order[p])),
                       pl.BlockSpec((TMF, qkv_w),
                                    lambda p, i, order, qrow, qcol: (jnp.where(qrow[p] < 0, i, qrow[p]), qcol[p])),
                       pl.BlockSpec((d, TMF), lambda p, i, order, qrow, qcol: (0, jnp.where(p == 0, i, last)))]
            + [ANY] * n,
            scratch_shapes=[pltpu.VMEM((2, d, ncol), BF16), pltpu.VMEM((n_row, TMF, d), BF16),
                            pltpu.SemaphoreType.DMA((n, 2 * (N_CHIP - 1))), pltpu.SemaphoreType.DMA((n, 2 * (N_CHIP - 1))),
                            pltpu.SemaphoreType.DMA((n,)), pltpu.SemaphoreType.DMA]),
        out_shape=[jax.ShapeDtypeStruct((t, N_CHIP * ncol), F32), jax.ShapeDtypeStruct((t, 3 * d), BF16),
                   jax.ShapeDtypeStruct((d, t), BF16)]
        + [jax.ShapeDtypeStruct((2 * N_CHIP,) + hv.shape[1:], BF16) for hv in halves],
        compiler_params=_params(("arbitrary", "arbitrary")),
    )(order, qrow, qcol, x2d, g_in, *halves)
    return outs[0], outs[1], outs[2], outs[3:]


def _branch_a_fwd(a_pre, g_v, wm, b_t):
    t = a_pre.shape[0]
    d = g_v.shape[1]
    d3 = 3 * d
    ng, chunk, _ = wm.shape
    cw = d // ng

    per_step = CHUNKS_PER_STEP if t % (CHUNKS_PER_STEP * chunk) == 0 else 1

    def body(a_ref, gv_ref, wm_ref, bt_ref, ya_ref):
        for n in range(per_step):
            rows = slice(n * chunk, (n + 1) * chunk)
            ua = _gelu(a_ref[rows, 0:d])
            vg = _gelu(a_ref[rows, d:2 * d])
            za = a_ref[rows, 2 * d:3 * d]
            rv = lax.rsqrt(jnp.mean(vg * vg, axis=-1, keepdims=True) + EPS)
            va = (vg * rv * gv_ref[...]).astype(BF16)
            gate = ua * (za * _sigmoid(za))
            for g in range(ng):
                sl = slice(g * cw, (g + 1) * cw)
                mixed = jnp.dot(wm_ref[g], va[:, sl], preferred_element_type=F32) + bt_ref[:, g:g + 1]
                ya_ref[rows, sl] = (gate[:, sl] * mixed).astype(BF16)

    tile = per_step * chunk
    return pl.pallas_call(
        body, name="branch_a_fwd",
        grid=(t // tile,),
        in_specs=[pl.BlockSpec((tile, d3), lambda i: (i, 0)), _const((1, d)), _const(wm.shape), _const(b_t.shape)],
        out_specs=pl.BlockSpec((tile, d), lambda i: (i, 0)),
        out_shape=jax.ShapeDtypeStruct((t, d), BF16),
        compiler_params=_params(("arbitrary",)),
    )(a_pre, g_v, wm, b_t)


def _branch_a_bwd(a_pre, dya, g_v, wm, wm_t, b_t):
    t = a_pre.shape[0]
    d = g_v.shape[1]
    d3 = 3 * d
    ng, chunk, _ = wm.shape
    cw = d // ng
    nsteps = t // chunk

    def body(a_ref, dya_ref, gv_ref, wm_ref, wmt_ref, bt_ref, da_ref, gws_ref, gbt_ref, gnv_ref, db_acc):
        i = pl.program_id(0)

        @pl.when(i == 0)
        def _():
            gws_ref[...] = jnp.zeros_like(gws_ref)
            gnv_ref[...] = jnp.zeros_like(gnv_ref)
            db_acc[...] = jnp.zeros_like(db_acc)

        ua, dgelu_u = _gelu_and_grad(a_ref[:, 0:d])
        vg, dgelu_v = _gelu_and_grad(a_ref[:, d:2 * d])
        za = a_ref[:, 2 * d:3 * d]
        sig = _sigmoid(za)
        sz = za * sig
        dsz = sig * (1.0 + za * (1.0 - sig))
        rv = lax.rsqrt(jnp.mean(vg * vg, axis=-1, keepdims=True) + EPS)
        nv = vg * rv
        gv = gv_ref[...]
        va = (nv * gv).astype(BF16)
        dya = dya_ref[...]
        dmix = dya * ua * sz
        db_acc[...] += dmix
        dmix_b = dmix.astype(BF16)
        t_gate = dya * sz
        t_z = dya * ua * dsz
        dva_parts = []
        for g in range(ng):
            sl = slice(g * cw, (g + 1) * cw)
            mixed = jnp.dot(wm_ref[g], va[:, sl], preferred_element_type=F32) + bt_ref[:, g:g + 1]
            da_ref[:, sl] = (t_gate[:, sl] * mixed * dgelu_u[:, sl]).astype(BF16)
            da_ref[:, 2 * d + g * cw:2 * d + (g + 1) * cw] = (t_z[:, sl] * mixed).astype(BF16)
            gws_ref[g] += lax.dot_general(dmix_b[:, sl], va[:, sl], NT, preferred_element_type=F32)
            dva_parts.append(jnp.dot(wmt_ref[g], dmix_b[:, sl], preferred_element_type=F32))
        dva = jnp.concatenate(dva_parts, axis=1)
        gnv_ref[...] += jnp.sum(dva * nv, axis=0, keepdims=True)
        dnv = dva * gv
        dvg = rv * (dnv - nv * jnp.mean(dnv * nv, axis=-1, keepdims=True))
        da_ref[:, d:2 * d] = (dvg * dgelu_v).astype(BF16)

        @pl.when(i == nsteps - 1)
        def _():
            acc = db_acc[...]
            for g in range(ng):
                gbt_ref[:, g:g + 1] = jnp.sum(acc[:, g * cw:(g + 1) * cw], axis=1, keepdims=True)

    return pl.pallas_call(
        body, name="branch_a_bwd",
        grid=(nsteps,),
        in_specs=[pl.BlockSpec((chunk, d3), lambda i: (i, 0)), pl.BlockSpec((chunk, d), lambda i: (i, 0)),
                  _const((1, d)), _const(wm.shape), _const(wm_t.shape), _const(b_t.shape)],
        out_specs=[pl.BlockSpec((chunk, d3), lambda i: (i, 0)), _const(wm.shape), _const(b_t.shape), _const((1, d))],
        out_shape=[jax.ShapeDtypeStruct((t, d3), BF16), jax.ShapeDtypeStruct(wm.shape, F32),
                   jax.ShapeDtypeStruct(b_t.shape, F32), jax.ShapeDtypeStruct((1, d), F32)],
        scratch_shapes=[pltpu.VMEM((chunk, d), F32)],
        compiler_params=_params(("arbitrary",)),
    )(a_pre, dya, g_v, wm, wm_t, b_t)


def _tri(n, rows_gt_cols):
    r = lax.broadcasted_iota(jnp.int32, (n, n), 0)
    c = lax.broadcasted_iota(jnp.int32, (n, n), 1)
    return (r > c) if rows_gt_cols else (r < c)


def _twice(tri):
    t = tri.astype(BF16)
    return jnp.concatenate([t, t], axis=0)


def _cumsum_mm(a, tri2):
    hi, lo = _split_bf16(a)
    return jnp.dot(jnp.concatenate([hi, lo], axis=1), tri2, preferred_element_type=F32)


LOG2E = 1.4426950408889634
_SIGN = 0x80000000


def _sb_block(q, k, scale, upper2, causal):
    z2 = lax.dot_general(q, k, NT, preferred_element_type=F32) * (scale * LOG2E)
    neg_abs = lax.bitcast_convert_type(lax.bitcast_convert_type(z2, jnp.uint32) | jnp.uint32(_SIGN), F32)
    l2 = jnp.log(1.0 + jnp.exp2(neg_abs)) * LOG2E
    log_beta = jnp.minimum(z2, 0.0) - l2
    lom = log_beta - z2
    if causal is not None:
        lom = jnp.where(causal, lom, 0.0)
    sx = _cumsum_mm(lom, upper2)
    return log_beta, sx, sx[:, 0:1] + lom[:, 0:1]


DEAD_LOG2 = -150.0


def _max_carry(carries):
    return jnp.max(functools.reduce(jnp.maximum, carries))


ZB_GROUP, GA_GROUP, GB_GROUP = 6, 7, 8


def _attn_specs(d, seq, nq, heads_per_step):
    hp_w = heads_per_step * (d // HEADS)
    n_hp = d // hp_w
    row_blk = lambda group: pl.BlockSpec((ATT_T, hp_w), lambda b, h, i: (b * nq + i, group * n_hp + h))
    seq_blk = lambda group: pl.BlockSpec((seq, hp_w), lambda b, h, i: (b, group * n_hp + h))
    return row_blk, seq_blk, n_hp


def _attn_fwd(qkv, proj, bsz, seq):
    t, d3 = qkv.shape
    d = d3 // 3
    hd = d // HEADS
    nq = seq // ATT_T
    scale = hd ** -0.5
    n_heads = ATT_HP
    row_blk, seq_blk, n_hp = _attn_specs(d, seq, nq, n_heads)

    def body(q_ref, k_ref, v_ref, zb_ref, o_ref, yb_ref):
        i = pl.program_id(2)
        causal = _tri(ATT_T, True)
        upper2 = _twice(causal)

        def step(kb, state, mask):
            rows = pl.ds(pl.multiple_of(kb * ATT_T, ATT_T), ATT_T)
            heads = [slice(h * hd, (h + 1) * hd) for h in range(n_heads)]
            scores = [_sb_block(q_ref[:, cols], k_ref[rows, cols], scale, upper2, mask) for cols in heads]
            new = []
            for cols, (carry, acc), (log_beta, sx, total) in zip(heads, state, scores):
                a = jnp.exp2(log_beta + sx + carry)
                if mask is not None:
                    a = jnp.where(mask, a, 0.0)
                acc = acc + jnp.dot(a.astype(BF16), v_ref[rows, cols], preferred_element_type=F32)
                new.append((carry + total, acc))
            return tuple(new)

        init = tuple((jnp.zeros((ATT_T, 1), F32), jnp.zeros((ATT_T, hd), F32)) for _ in range(n_heads))
        state = step(i, init, causal)
        def more(c):
            new = step(c[0], c[1], None)
            return c[0] - 1, new, _max_carry([s[0] for s in new])

        _, state, _ = lax.while_loop(lambda c: (c[0] >= 0) & (c[2] > DEAD_LOG2), more,
                                     (i - 1, state, _max_carry([s[0] for s in state])))
        for h in range(n_heads):
            cols = slice(h * hd, (h + 1) * hd)
            acc = state[h][1]
            zb = zb_ref[:, cols]
            o_ref[:, cols] = acc
            yb_ref[:, cols] = (acc * (zb * _sigmoid(zb))).astype(BF16)

    return pl.pallas_call(
        body, name="attn_fwd",
        grid=(bsz, n_hp, nq),
        in_specs=[row_blk(0), seq_blk(1), seq_blk(2), row_blk(ZB_GROUP)],
        out_specs=[row_blk(0), row_blk(0)],
        out_shape=[jax.ShapeDtypeStruct((t, d), F32), jax.ShapeDtypeStruct((t, d), BF16)],
        compiler_params=_params(("arbitrary", "arbitrary", "arbitrary")),
    )(qkv, qkv, qkv, proj)


def _attn_bwd(qkv, proj, o, dyb, bsz, seq):
    t, d3 = qkv.shape
    d = d3 // 3
    hd = d // HEADS
    nq = seq // ATT_T
    scale = hd ** -0.5
    row_blk, seq_blk, n_hp = _attn_specs(d, seq, nq, ATT_HP)

    def body(q_ref, k_ref, v_ref, zb_ref, o_ref, dyb_ref, dq_ref, dk_ref, dv_ref, dzb_ref,
             g_s, beta_s, dkt_acc, dvt_acc):
        i = pl.program_id(2)

        @pl.when(i == 0)
        def _():
            dkt_acc[...] = jnp.zeros_like(dkt_acc)
            dvt_acc[...] = jnp.zeros_like(dvt_acc)

        causal = _tri(ATT_T, True)
        upper2 = _twice(causal)
        lower2 = _twice(~causal)
        zb = zb_ref[...]
        sig = _sigmoid(zb)
        dyb_t = dyb_ref[...]
        do_f = dyb_t * (zb * sig)
        do = do_f.astype(BF16)
        do_t = do_f.T.astype(BF16)
        q_t = q_ref[...].astype(F32).T.astype(BF16)
        dzb_ref[...] = (dyb_t * o_ref[...] * (sig * (1.0 + zb * (1.0 - sig)))).astype(BF16)

        def sweep(kb, carries, mask):
            rows = pl.ds(pl.multiple_of(kb * ATT_T, ATT_T), ATT_T)
            heads = [slice(h * hd, (h + 1) * hd) for h in range(ATT_HP)]
            scores = [_sb_block(q_ref[:, cols], k_ref[rows, cols], scale, upper2, mask) for cols in heads]
            das = [lax.dot_general(do[:, cols], v_ref[rows, cols], NT, preferred_element_type=F32) for cols in heads]
            new = []
            for h, (cols, carry, (log_beta, sx, total), da) in enumerate(zip(heads, carries, scores, das)):
                a = jnp.exp2(log_beta + sx + carry)
                beta = jnp.exp2(log_beta)
                if mask is not None:
                    a = jnp.where(mask, a, 0.0)
                    beta = jnp.where(mask, beta, 0.0)
                g_s[h, kb] = a * da
                beta_s[h, kb] = beta
                dvt_acc[kb, cols, :] += jnp.dot(do_t[cols, :], a.astype(BF16), preferred_element_type=F32)
                new.append(carry + total)
            return tuple(new)

        carries = sweep(i, tuple(jnp.zeros((ATT_T, 1), F32) for _ in range(ATT_HP)), causal)

        def more(c):
            new = sweep(c[0], c[1], None)
            return c[0] - 1, new, _max_carry(new)

        last, _, _ = lax.while_loop(lambda c: (c[0] >= 0) & (c[2] > DEAD_LOG2), more, (i - 1, carries, _max_carry(carries)))
        first_kb = last + 1

        def back(kb, state):
            rows = pl.ds(pl.multiple_of(kb * ATT_T, ATT_T), ATT_T)
            heads = [slice(h * hd, (h + 1) * hd) for h in range(ATT_HP)]
            sums = [_cumsum_mm(g_s[h, kb], lower2) for h in range(ATT_HP)]
            new = []
            for h, (cols, (p_carry, dq), px) in enumerate(zip(heads, state, sums)):
                dz = ((g_s[h, kb] - (p_carry + px) * beta_s[h, kb]) * scale).astype(BF16)
                dq = dq + jnp.dot(dz, k_ref[rows, cols], preferred_element_type=F32)
                dkt_acc[kb, cols, :] += jnp.dot(q_t[cols, :], dz, preferred_element_type=F32)
                new.append((p_carry + px[:, ATT_T - 1:ATT_T], dq))
            return tuple(new)

        init = tuple((jnp.zeros((ATT_T, 1), F32), jnp.zeros((ATT_T, hd), F32)) for _ in range(ATT_HP))
        state = lax.fori_loop(first_kb, i + 1, back, init)
        for h in range(ATT_HP):
            dq_ref[:, h * hd:(h + 1) * hd] = state[h][1].astype(BF16)

        @pl.when(i == nq - 1)
        def _():
            for kb in range(nq):
                dk_ref[kb * ATT_T:(kb + 1) * ATT_T, :] = dkt_acc[kb].T.astype(BF16)
                dv_ref[kb * ATT_T:(kb + 1) * ATT_T, :] = dvt_acc[kb].T.astype(BF16)

    out = jax.ShapeDtypeStruct((t, d), BF16)
    hp_w = ATT_HP * hd
    return pl.pallas_call(
        body, name="attn_bwd",
        grid=(bsz, n_hp, nq),
        in_specs=[row_blk(0), seq_blk(1), seq_blk(2), row_blk(ZB_GROUP), row_blk(0), row_blk(0)],
        out_specs=[row_blk(0), seq_blk(0), seq_blk(0), row_blk(0)],
        out_shape=[out, out, out, out],
        scratch_shapes=[pltpu.VMEM((ATT_HP, nq, ATT_T, ATT_T), F32), pltpu.VMEM((ATT_HP, nq, ATT_T, ATT_T), F32),
                        pltpu.VMEM((nq, hp_w, ATT_T), F32), pltpu.VMEM((nq, hp_w, ATT_T), F32)],
        compiler_params=_params(("arbitrary", "arbitrary", "arbitrary")),
    )(qkv, qkv, qkv, proj, o, dyb)


def _out_proj(ya, yb, g_pre, x2d, tgt, w_og, w_osb, w_out, g_f):
    t, d = x2d.shape

    def body(ya_ref, yb_ref, ga_ref, gb_ref, x_ref, tgt_ref, wog_ref, wosb_ref, wout_ref, gf_ref,
             dya_ref, dyb_ref, dg_ref, dx2_ref, loss_ref, gnf_ref, gwog_ref, gwosb_ref, gwout_ref):
        @pl.when(pl.program_id(0) == 0)
        def _():
            loss_ref[...] = jnp.zeros_like(loss_ref)
            gnf_ref[...] = jnp.zeros_like(gnf_ref)
            gwog_ref[...] = jnp.zeros_like(gwog_ref)
            gwosb_ref[...] = jnp.zeros_like(gwosb_ref)
            gwout_ref[...] = jnp.zeros_like(gwout_ref)

        ya = ya_ref[...]
        yb = yb_ref[...]
        pa = jnp.dot(ya, wog_ref[...], preferred_element_type=F32)
        pb = jnp.dot(yb, wosb_ref[...], preferred_element_type=F32)
        sga = _sigmoid(ga_ref[...])
        sgb = _sigmoid(gb_ref[...])
        merged = (sga * pa + sgb * pb).astype(BF16)
        x2 = x_ref[...] + jnp.dot(merged, wout_ref[...], preferred_element_type=F32)
        r2 = lax.rsqrt(jnp.mean(x2 * x2, axis=-1, keepdims=True) + EPS)
        n2 = x2 * r2
        gf = gf_ref[...]
        err = n2 * gf - tgt_ref[...]
        loss_ref[...] += 0.5 * jnp.sum(jnp.sum(err * err, axis=-1, keepdims=True), axis=0, keepdims=True) / d
        dy = err * (1.0 / d)
        gnf_ref[...] += jnp.sum(dy * n2, axis=0, keepdims=True)
        dn = dy * gf
        dx2 = r2 * (dn - n2 * jnp.mean(dn * n2, axis=-1, keepdims=True))
        dx2_ref[...] = dx2
        dx2_b = dx2.astype(BF16)
        dmerged = lax.dot_general(dx2_b, wout_ref[...], NT, preferred_element_type=F32)
        gwout_ref[...] += lax.dot_general(merged, dx2_b, TN, preferred_element_type=F32)
        dg_ref[:, 0:d] = (dmerged * pa * (sga * (1.0 - sga))).astype(BF16)
        dg_ref[:, d:2 * d] = (dmerged * pb * (sgb * (1.0 - sgb))).astype(BF16)
        dpa = (dmerged * sga).astype(BF16)
        dpb = (dmerged * sgb).astype(BF16)
        dya_ref[...] = lax.dot_general(dpa, wog_ref[...], NT, preferred_element_type=F32)
        dyb_ref[...] = lax.dot_general(dpb, wosb_ref[...], NT, preferred_element_type=F32)
        gwog_ref[...] += lax.dot_general(ya, dpa, TN, preferred_element_type=F32)
        gwosb_ref[...] += lax.dot_general(yb, dpb, TN, preferred_element_type=F32)

    row = lambda i: (i, 0)
    return pl.pallas_call(
        body, name="out_proj",
        grid=(t // TM,),
        in_specs=[pl.BlockSpec((TM, d), row), pl.BlockSpec((TM, d), row),
                  pl.BlockSpec((TM, d), lambda i: (i, GA_GROUP)), pl.BlockSpec((TM, d), lambda i: (i, GB_GROUP)),
                  pl.BlockSpec((TM, d), row), pl.BlockSpec((TM, d), row),
                  _resident((d, d)), _resident((d, d)), _resident((d, d)), _const((1, d))],
        out_specs=[pl.BlockSpec((TM, d), row), pl.BlockSpec((TM, d), row), pl.BlockSpec((TM, 2 * d), row),
                   pl.BlockSpec((TM, d), row), _const((1, 1)), _const((1, d)),
                   _const((d, d)), _const((d, d)), _const((d, d))],
        out_shape=[jax.ShapeDtypeStruct((t, d), F32), jax.ShapeDtypeStruct((t, d), F32),
                   jax.ShapeDtypeStruct((t, 2 * d), BF16), jax.ShapeDtypeStruct((t, d), F32),
                   jax.ShapeDtypeStruct((1, 1), F32), jax.ShapeDtypeStruct((1, d), F32),
                   jax.ShapeDtypeStruct((d, d), F32), jax.ShapeDtypeStruct((d, d), F32),
                   jax.ShapeDtypeStruct((d, d), F32)],
        compiler_params=_params(("arbitrary",)),
    )(ya, yb, g_pre, g_pre, x2d, tgt, w_og, w_osb, w_out, g_f)


def _dproj_pieces(d):
    return [(0, 0, 3), (1, 3, 1), (2, 4, 1), (3, 5, 1), (4, 6, 1), (5, 7, 2)]


def _in_proj_bwd_x(pieces, wg, x2d, g_in, dx2, gw16):
    t, d = x2d.shape
    ncol = wg.shape[2]
    segs = _segments(d, ncol)
    layout = _dproj_pieces(d)
    nsteps = t // TM

    def body(da_ref, dq_ref, dk_ref, dv_ref, dzb_ref, dg_ref, w_ref, x_ref, g_ref, dx2_ref, gw16_ref,
             gx_ref, gn_ref, recv_ref, send_sems, recv_sems):
        x_pos, y_pos, c_pos, chips = _place()

        def share(k, chunk):
            px, py = chips[k]
            return pltpu.make_async_remote_copy(
                src_ref=gw16_ref.at[:, chunk * ncol:(chunk + 1) * ncol], dst_ref=recv_ref.at[k],
                send_sem=send_sems.at[k], recv_sem=recv_sems.at[k], device_id=(px, py, c_pos), device_id_type=MESH)

        @pl.when(pl.program_id(0) == 0)
        def _():
            gn_ref[...] = jnp.zeros_like(gn_ref)
            for k, (px, py) in enumerate(chips):
                for chunk in range(N_CHIP):
                    @pl.when(2 * px + py == chunk)
                    def _(k=k, chunk=chunk):
                        share(k, chunk).start()

        @pl.when(pl.program_id(0) == nsteps - 1)
        def _():
            for k in range(N_CHIP - 1):
                share(k, 0).wait()

        refs = (da_ref, dq_ref, dk_ref, dv_ref, dzb_ref, dg_ref)
        dh = jnp.zeros((TM, d), F32)
        for chip, c0, grp, s0, width in segs:
            piece, first, _ = next(p for p in layout if p[1] <= grp < p[1] + p[2])
            off = (grp - first) * d + s0
            dh = dh + lax.dot_general(refs[piece][:, off:off + width], w_ref[chip, :, c0:c0 + width], NT,
                                      preferred_element_type=F32)
        x = x_ref[...]
        r = lax.rsqrt(jnp.mean(x * x, axis=-1, keepdims=True) + EPS)
        n = x * r
        gn_ref[...] += jnp.sum(dh * n, axis=0, keepdims=True)
        dn = dh * g_ref[...]
        gx_ref[...] = dx2_ref[...] + r * (dn - n * jnp.mean(dn * n, axis=-1, keepdims=True))

    row = lambda i: (i, 0)
    return pl.pallas_call(
        body, name="in_proj_bwd_x",
        grid=(t // TM,),
        in_specs=[pl.BlockSpec((TM, p.shape[1]), row) for p in pieces]
        + [_resident(wg.shape), pl.BlockSpec((TM, d), row), _const((1, d)), pl.BlockSpec((TM, d), row), ANY],
        out_specs=[pl.BlockSpec((TM, d), row), _const((1, d)), ANY],
        out_shape=[jax.ShapeDtypeStruct((t, d), F32), jax.ShapeDtypeStruct((1, d), F32),
                   jax.ShapeDtypeStruct((N_CHIP - 1, gw16.shape[0], ncol), BF16)],
        scratch_shapes=[pltpu.SemaphoreType.DMA((N_CHIP - 1,)), pltpu.SemaphoreType.DMA((N_CHIP - 1,))],
        compiler_params=_params(("arbitrary",)),
    )(*pieces, wg, x2d, g_in, dx2, gw16)


def _in_proj_bwd_w(h_t, pieces, mats16, pack):
    d, t = h_t.shape
    nk = t // TKW
    half = d // 2
    layout = _dproj_pieces(d)
    n_mats = len(mats16)
    n_dev = 2 * N_CHIP
    flips = [(dx, dy, dc) for dx in (0, 1) for dy in (0, 1) for dc in (0, 1)][1:]

    def body(ht_ref, da_ref, dq_ref, dk_ref, dv_ref, dzb_ref, dg_ref, *rest):
        mat_refs, pack_ref = rest[:n_mats], rest[n_mats]
        gw_ref, sib_ref = rest[n_mats + 1:n_mats + 3]
        recv_refs, slots_ref = rest[n_mats + 3:2 * n_mats + 3], rest[2 * n_mats + 3]
        acc, stage, mat_send, mat_recv, pack_send, pack_recv, own_sem, stage_send, sib_recv = rest[2 * n_mats + 4:]
        s = pl.program_id(0)
        i = pl.program_id(1)
        x_pos, y_pos, c_pos, chips = _place()
        me = 4 * x_pos + 2 * y_pos + c_pos

        def to_sibling(slot, group):
            return pltpu.make_async_remote_copy(
                src_ref=stage.at[slot], dst_ref=sib_ref.at[:, group * d:(group + 1) * d],
                send_sem=stage_send.at[slot], recv_sem=sib_recv, device_id=(x_pos, y_pos, 1 - c_pos), device_id_type=MESH)

        def exchanges():
            cps = []
            for k, (px, py) in enumerate(chips):
                for a in range(n_mats):
                    cps.append(pltpu.make_async_remote_copy(
                        src_ref=mat_refs[a].at[2 * px + py], dst_ref=recv_refs[a].at[k],
                        send_sem=mat_send.at[a, k], recv_sem=mat_recv.at[a, k],
                        device_id=(px, py, c_pos), device_id_type=MESH))
            for k, (dx, dy, dc) in enumerate(flips):
                peer = (1 - x_pos if dx else x_pos, 1 - y_pos if dy else y_pos, 1 - c_pos if dc else c_pos)
                cps.append(pltpu.make_async_remote_copy(
                    src_ref=pack_ref, dst_ref=slots_ref.at[me], send_sem=pack_send.at[k], recv_sem=pack_recv.at[k],
                    device_id=peer, device_id_type=MESH))
            return cps, pltpu.make_async_copy(pack_ref, slots_ref.at[me], own_sem)

        @pl.when((s == 0) & (i == 0))
        def _():
            cps, own = exchanges()
            own.start()
            for cp in cps:
                cp.start()

        @pl.when(i == 0)
        def _():
            acc[...] = jnp.zeros_like(acc)

        refs = (da_ref, dq_ref, dk_ref, dv_ref, dzb_ref, dg_ref)
        for piece, first, count in layout:
            @pl.when((s >= first) & (s < first + count))
            def _(piece=piece):
                acc[...] += jnp.dot(ht_ref[...], refs[piece][...], preferred_element_type=F32)

        @pl.when(i == nk - 1)
        def _():
            gw_ref[...] = acc[...]
            for slot in range(2):
                @pl.when(s % 2 == slot)
                def _(slot=slot):
                    @pl.when(s >= 2)
                    def _():
                        to_sibling(slot, 0).wait_send()
                    for other in range(2):
                        @pl.when(c_pos == 1 - other)
                        def _(other=other):
                            stage[slot] = acc[other * half:(other + 1) * half, :].astype(BF16)
                    for group in range(N_SPLIT):
                        @pl.when(s == group)
                        def _(group=group):
                            to_sibling(slot, group).start()

        @pl.when((s == N_SPLIT - 1) & (i == nk - 1))
        def _():
            for slot in range(2):
                to_sibling(slot, 0).wait_send()
            pltpu.make_async_remote_copy(
                src_ref=sib_ref, dst_ref=sib_ref, send_sem=stage_send.at[0], recv_sem=sib_recv,
                device_id=(x_pos, y_pos, c_pos), device_id_type=MESH).wait_recv()
            cps, own = exchanges()
            own.wait()
            for cp in cps:
                cp.wait()

    def piece_spec(p, first, count):
        def index(s, i):
            mine = (s >= first) & (s < first + count)
            return jnp.where(mine, i, 0), jnp.where(mine, s - first, 0)
        return pl.BlockSpec((TKW, d), index)

    col_blk = pl.BlockSpec((d, d), lambda s, i: (0, s))
    outs = pl.pallas_call(
        body, name="in_proj_bwd_w",
        grid=(N_SPLIT, nk),
        in_specs=[pl.BlockSpec((d, TKW), lambda s, i: (0, i))] + [piece_spec(*p) for p in layout] + [ANY] * (n_mats + 1),
        out_specs=[col_blk, ANY] + [ANY] * (n_mats + 1),
        out_shape=[jax.ShapeDtypeStruct((d, N_SPLIT * d), F32), jax.ShapeDtypeStruct((half, N_SPLIT * d), BF16)]
        + [jax.ShapeDtypeStruct((N_CHIP - 1,) + m.shape[1:], BF16) for m in mats16]
        + [jax.ShapeDtypeStruct((n_dev,) + pack.shape, F32)],
        scratch_shapes=[pltpu.VMEM((d, d), F32), pltpu.VMEM((2, half, d), BF16),
                        pltpu.SemaphoreType.DMA((n_mats, N_CHIP - 1)), pltpu.SemaphoreType.DMA((n_mats, N_CHIP - 1)),
                        pltpu.SemaphoreType.DMA((n_dev - 1,)), pltpu.SemaphoreType.DMA((n_dev - 1,)),
                        pltpu.SemaphoreType.DMA, pltpu.SemaphoreType.DMA((2,)), pltpu.SemaphoreType.DMA],
        compiler_params=_params(("arbitrary", "arbitrary")),
    )(h_t, *pieces, *mats16, pack)
    return outs[0], outs[1], outs[2:2 + n_mats], outs[2 + n_mats]


def _local_step(proj, qkv, x2d, tgt2d, bsz, seq, norm_v, w_s, b_s, w_og, w_osb, w_out, norm_final):
    d = x2d.shape[1]
    chunk = w_s.shape[-1]
    causal = jnp.tril(jnp.ones((chunk, chunk), dtype=bool))
    wm = jnp.where(causal[None], w_s, 0.0).astype(BF16)
    wm_t = jnp.swapaxes(wm, 1, 2)
    b_t = b_s.T

    ya = _branch_a_fwd(proj, norm_v, wm, b_t)
    o, yb = _attn_fwd(qkv, proj, bsz, seq)
    dya, dyb, dg, dx2, loss, g_nf, g_wog, g_wosb, g_wout = _out_proj(
        ya, yb, proj, x2d, tgt2d, w_og, w_osb, w_out, norm_final.reshape(1, d))
    dq, dk, dv, dzb = _attn_bwd(qkv, proj, o, dyb, bsz, seq)
    d_a, g_ws, g_bt, g_nv = _branch_a_bwd(proj, dya, norm_v, wm, wm_t, b_t)
    g_ws = jnp.where(causal[None], g_ws, 0.0)
    return loss, (d_a, dq, dk, dv, dzb, dg), dx2, g_nv, g_ws, g_bt.T, g_wog, g_wosb, g_wout, g_nf


def _row_tile(rows):
    return next(r for r in (128, 64, 32, 16, 8) if rows % r == 0)


def _cast_bf16(arrs):
    n = len(arrs)

    def body(*refs):
        for a_ref, o_ref in zip(refs[:n], refs[n:]):
            o_ref[...] = a_ref[...].astype(BF16)

    specs = [pl.BlockSpec((a.shape[0] // CAST_STEPS, a.shape[1]), lambda i: (i, 0)) for a in arrs]
    return pl.pallas_call(
        body, name="cast_bf16", grid=(CAST_STEPS,),
        in_specs=specs, out_specs=specs,
        out_shape=[jax.ShapeDtypeStruct(a.shape, BF16) for a in arrs],
        compiler_params=_params(("arbitrary",)),
    )(*arrs)


def _chip_half(full, from_sibling, core, tile):
    half, n = from_sibling.shape

    def body(core_ref, own_ref, sib_ref, o32_ref, o16_ref):
        total = own_ref[...] + sib_ref[...].astype(F32)
        o32_ref[...] = total
        o16_ref[...] = total.astype(BF16)

    blk = pl.BlockSpec((half, tile), lambda j, core_ref: (0, j))
    return pl.pallas_call(
        body, name="chip_half",
        grid_spec=pltpu.PrefetchScalarGridSpec(
            num_scalar_prefetch=1, grid=(n // tile,),
            in_specs=[pl.BlockSpec((half, tile), lambda j, core_ref: (core_ref[0], j)), blk],
            out_specs=[blk, blk]),
        out_shape=[jax.ShapeDtypeStruct((half, n), F32), jax.ShapeDtypeStruct((half, n), BF16)],
        compiler_params=_params(("arbitrary",)),
    )(core.reshape(1).astype(jnp.int32), full, from_sibling)


def _add_received(full, recv, chip, by_cols):
    _, rows, cols = recv.shape
    tr = _row_tile(rows)
    nb = rows // tr

    def body(chip_ref, own_ref, recv_ref, o_ref):
        s = own_ref[...]
        for k in range(N_CHIP - 1):
            s = s + recv_ref[k].astype(F32)
        o_ref[...] = s

    own_map = (lambda i, chip_ref: (i, chip_ref[0])) if by_cols else (lambda i, chip_ref: (chip_ref[0] * nb + i, 0))
    return pl.pallas_call(
        body, name="add_received",
        grid_spec=pltpu.PrefetchScalarGridSpec(
            num_scalar_prefetch=1, grid=(nb,),
            in_specs=[pl.BlockSpec((tr, cols), own_map),
                      pl.BlockSpec((N_CHIP - 1, tr, cols), lambda i, chip_ref: (0, i, 0))],
            out_specs=pl.BlockSpec((tr, cols), lambda i, chip_ref: (i, 0))),
        out_shape=jax.ShapeDtypeStruct((rows, cols), F32),
        compiler_params=_params(("arbitrary",)),
    )(chip.reshape(1).astype(jnp.int32), full, recv)


def _adamw_math(w, m, v, g):
    new_m = ADAM_B1 * m + (1.0 - ADAM_B1) * g
    new_v = ADAM_B2 * v + (1.0 - ADAM_B2) * (g * g)
    m_hat = new_m / (1.0 - ADAM_B1 ** ADAM_STEP)
    v_hat = new_v / (1.0 - ADAM_B2 ** ADAM_STEP)
    return -ADAM_LR * (m_hat / (jnp.sqrt(v_hat) + ADAM_EPS) + ADAM_WD * w), new_m, new_v


def _adamw(w, m, v, g_parts):
    rows, cols = w.shape
    tr = _row_tile(rows)
    n_parts = len(g_parts)

    def body(*refs):
        w_ref, m_ref, v_ref = refs[:3]
        part_refs = refs[3:3 + n_parts]
        g_ref, d_ref, nm_ref, nv_ref = refs[3 + n_parts:]
        g = part_refs[0][...]
        for p in part_refs[1:]:
            g = g + p[...]
        g_ref[...] = g
        d_ref[...], nm_ref[...], nv_ref[...] = _adamw_math(w_ref[...], m_ref[...], v_ref[...], g)

    spec = pl.BlockSpec((tr, cols), lambda i: (i, 0))
    out = jax.ShapeDtypeStruct(w.shape, F32)
    return pl.pallas_call(
        body, name="adamw", grid=(rows // tr,),
        in_specs=[spec] * (3 + n_parts), out_specs=[spec] * 4, out_shape=[out] * 4,
        compiler_params=_params(("arbitrary",)),
    )(w, m, v, *g_parts)


def _adamw_small(w, m, v, slots_head, slots_tail):
    n_dev, p0, _ = slots_head.shape

    def body(w_ref, m_ref, v_ref, head_ref, tail_ref, g_ref, d_ref, nm_ref, nv_ref):
        for ref, rows in ((head_ref, slice(0, p0)), (tail_ref, slice(p0, w.shape[0]))):
            g = ref[0]
            for i in range(1, n_dev):
                g = g + ref[i]
            g_ref[rows, :] = g
            d_ref[rows, :], nm_ref[rows, :], nv_ref[rows, :] = _adamw_math(w_ref[rows, :], m_ref[rows, :], v_ref[rows, :], g)

    vmem = pl.BlockSpec(memory_space=pltpu.VMEM)
    out = jax.ShapeDtypeStruct(w.shape, F32)
    return pl.pallas_call(
        body, name="adamw_small", in_specs=[vmem] * 5, out_specs=[vmem] * 4, out_shape=[out] * 4,
        compiler_params=pltpu.CompilerParams(vmem_limit_bytes=VMEM_LIMIT),
    )(w, m, v, slots_head, slots_tail)


ANY = pl.BlockSpec(memory_space=pl.ANY)


def _place():
    x, y, c = lax.axis_index("x"), lax.axis_index("y"), lax.axis_index("c")
    other_chips = [(1 - x, y), (x, 1 - y), (1 - x, 1 - y)]
    return x, y, c, other_chips


def _swap_and_gather(arrs, row_half, pack):
    n = len(arrs)
    n_dev = 2 * N_CHIP
    rows = row_half.shape[0]
    flips = [(dx, dy, dc) for dx in (0, 1) for dy in (0, 1) for dc in (0, 1)][1:]

    def body(*refs):
        ins, half_ref, pack_ref = refs[:n], refs[n], refs[n + 1]
        outs, whole_ref, slots_ref = refs[n + 2:2 * n + 2], refs[2 * n + 2], refs[2 * n + 3]
        send_sems, recv_sems, pack_send, pack_recv, own_sem, half_send, half_recv, half_own = refs[2 * n + 4:]
        x, y, c, _ = _place()
        me = 4 * x + 2 * y + c
        own = pltpu.make_async_copy(pack_ref, slots_ref.at[me], own_sem)
        own.start()
        mine = whole_ref.at[pl.ds(pl.multiple_of(c * rows, rows), rows)]
        keep = pltpu.make_async_copy(half_ref, mine, half_own)
        keep.start()
        copies = [pltpu.make_async_remote_copy(
            src_ref=half_ref, dst_ref=mine, send_sem=half_send, recv_sem=half_recv,
            device_id=(x, y, 1 - c), device_id_type=MESH)]
        for k, (dx, dy, dc) in enumerate(flips):
            peer = (1 - x if dx else x, 1 - y if dy else y, 1 - c if dc else c)
            copies.append(pltpu.make_async_remote_copy(
                src_ref=pack_ref, dst_ref=slots_ref.at[me], send_sem=pack_send.at[k], recv_sem=pack_recv.at[k],
                device_id=peer, device_id_type=MESH))
        copies += [pltpu.make_async_remote_copy(
            src_ref=ins[a], dst_ref=outs[a], send_sem=send_sems.at[a], recv_sem=recv_sems.at[a],
            device_id=(x, y, 1 - c), device_id_type=MESH) for a in range(n)]
        for cp in copies:
            cp.start()
        for cp in copies:
            cp.wait()
        own.wait()
        keep.wait()

    outs = pl.pallas_call(
        body, name="swap_and_gather",
        in_specs=[ANY] * (n + 2), out_specs=[ANY] * (n + 2),
        out_shape=[jax.ShapeDtypeStruct(a.shape, a.dtype) for a in arrs]
        + [jax.ShapeDtypeStruct((2 * rows, row_half.shape[1]), row_half.dtype),
           jax.ShapeDtypeStruct((n_dev,) + pack.shape, F32)],
        scratch_shapes=[pltpu.SemaphoreType.DMA((n,)), pltpu.SemaphoreType.DMA((n,)),
                        pltpu.SemaphoreType.DMA((n_dev - 1,)), pltpu.SemaphoreType.DMA((n_dev - 1,)),
                        pltpu.SemaphoreType.DMA, pltpu.SemaphoreType.DMA, pltpu.SemaphoreType.DMA,
                        pltpu.SemaphoreType.DMA],
    )(*arrs, row_half, pack)
    return outs[:n], outs[n], outs[n + 1]


SLAB = 8


def _slab(vec, d):
    return jnp.pad(vec.reshape(1, d), ((0, SLAB - 1), (0, 0)))


def _pack_tail(vec_nv, b_s, vec_nf, w_s, scalar=None):
    d = vec_nv.shape[-1]
    extra = jnp.zeros((1, d), F32) if scalar is None else jnp.pad(scalar, ((0, 0), (0, d - 1)))
    return jnp.concatenate([_slab(v, d) for v in (vec_nv, b_s, vec_nf, extra)] + [w_s.reshape(-1, d)], axis=0)


def _pack_small(vec_nin, vec_nv, b_s, vec_nf, w_s):
    return jnp.concatenate([_slab(vec_nin, vec_nin.shape[-1]), _pack_tail(vec_nv, b_s, vec_nf, w_s)], axis=0)


def _unpack_small(pack, w_s_shape, b_s_shape):
    return (pack[0:1], pack[SLAB:SLAB + 1], pack[2 * SLAB].reshape(b_s_shape), pack[3 * SLAB],
            pack[5 * SLAB:].reshape(w_s_shape), pack[4 * SLAB, 0])


def kernel(x, norm_in, w_in, norm_v, w_s, b_s, w_o_gmlp, w_o_sb, w_out, norm_final, loss_target, m_norm_in, m_w_in, m_norm_v, m_w_s, m_b_s, m_w_o_gmlp, m_w_o_sb, m_w_out, m_norm_final, v_norm_in, v_w_in, v_norm_v, v_w_s, v_b_s, v_w_o_gmlp, v_w_o_sb, v_w_out, v_norm_final):
    d = x.shape[-1]
    ncol = w_in.shape[-1]
    nrow = w_o_gmlp.shape[-2]
    chip = 2 * lax.axis_index("x") + lax.axis_index("y")

    bsz, seq, _ = x.shape
    x2d = x.reshape(bsz * seq, d)
    shards = [w_in[0], w_o_gmlp[0], w_o_sb[0], w_out[0]]
    halves = [s16.reshape(2, s16.shape[0] // 2, s16.shape[1]) for s16 in _cast_bf16(shards)]
    proj, qkv, h_t, (wg, w_og, w_osb, w_o) = _in_proj_fwd(x2d, norm_in, halves)
    wg = wg.reshape(N_CHIP, d, ncol)

    loss, pieces, dx2, g_nv, g_ws, g_bs, g_wog, g_wosb, g_wout, g_nf = _local_step(
        proj, qkv, x2d, loss_target.reshape(bsz * seq, d), bsz, seq, norm_v, w_s[0], b_s[0],
        w_og.reshape(d, d), w_osb.reshape(d, d), w_o.reshape(d, d), norm_final)

    mats = [g_wog, g_wosb, g_wout]
    mats16 = [g16.reshape(N_CHIP, nrow, d) for g16 in _cast_bf16(mats)]
    g_win, from_sibling, recv_mats, slots_tail = _in_proj_bwd_w(
        h_t, pieces, mats16, _pack_tail(g_nv, g_bs, g_nf, g_ws, loss))
    core = lax.axis_index("c")
    half_win, half_win16 = _chip_half(g_win, from_sibling, core, ncol)
    grad_x, g_nin, recv_win = _in_proj_bwd_x(pieces, wg, x2d, norm_in, dx2, half_win16)
    grad_x = grad_x.reshape(bsz, seq, d)

    sum_win = _add_received(half_win, recv_win, chip, True)
    sums = [_add_received(g, r, chip, False) for g, r in zip(mats, recv_mats)]
    sibling_sums, g_win_whole, slots_head = _swap_and_gather(sums, sum_win, _slab(g_nin, d))
    stats = [_adamw(w_in[0], m_w_in[0], v_w_in[0], [g_win_whole])]
    for w, m, v, mine, theirs in zip(shards[1:], [m_w_o_gmlp[0], m_w_o_sb[0], m_w_out[0]],
                                     [v_w_o_gmlp[0], v_w_o_sb[0], v_w_out[0]], sums, sibling_sums):
        stats.append(_adamw(w, m, v, [mine, theirs]))
    (gw_in, dw_in, nm_in, nv_in), (gw_og, dw_og, nm_og, nv_og), (gw_osb, dw_osb, nm_osb, nv_osb), \
        (gw_out, dw_out, nm_out, nv_out) = stats

    gs, ds, ms, vs = _adamw_small(
        _pack_small(norm_in, norm_v, b_s[0], norm_final, w_s[0]),
        _pack_small(m_norm_in, m_norm_v, m_b_s[0], m_norm_final, m_w_s[0]),
        _pack_small(v_norm_in, v_norm_v, v_b_s[0], v_norm_final, v_w_s[0]), slots_head, slots_tail)

    def small(pack):
        nin, nv, bs, nf, ws, _ = _unpack_small(pack, w_s.shape, b_s.shape)
        return nin, nv, ws, bs, nf

    loss = _unpack_small(gs, w_s.shape, b_s.shape)[-1]
    out = []
    for small_pack, win, wog, wosb, wout in ((gs, gw_in, gw_og, gw_osb, gw_out), (ds, dw_in, dw_og, dw_osb, dw_out),
                                             (ms, nm_in, nm_og, nm_osb, nm_out), (vs, nv_in, nv_og, nv_osb, nv_out)):
        nin, nv, ws, bs, nf = small(small_pack)
        out += [nin, win[None], nv, ws, bs, wog[None], wosb[None], wout[None], nf]
    return (loss, grad_x, *out)
```

```python
import functools
import math

import jax
import jax.numpy as jnp
from jax import lax
from jax.experimental import pallas as pl
from jax.experimental.pallas import tpu as pltpu

F32 = jnp.float32
BF16 = jnp.bfloat16
EPS = 1e-6
HEADS = 8
N_SPLIT = 9
N_CHIP = 4
MESH = pl.DeviceIdType.MESH

ADAM_LR = 0.001
ADAM_B1 = 0.9
ADAM_B2 = 0.999
ADAM_EPS = 1e-08
ADAM_WD = 0.01
ADAM_STEP = 10

VMEM_LIMIT = 56 * 2 ** 20
TM = 256
TMF = 512
ATT_T = 256
ATT_HP = 4
TKW = 1024
CHUNKS_PER_STEP = 4
CAST_STEPS = 8

NT = (((1,), (1,)), ((), ()))
TN = (((0,), (0,)), ((), ()))


def _params(sem):
    return pltpu.CompilerParams(dimension_semantics=sem, vmem_limit_bytes=VMEM_LIMIT)


def _resident(shape):
    nd = len(shape)
    return pl.BlockSpec(shape, lambda *_: (0,) * nd, pipeline_mode=pl.Buffered(1))


def _const(shape):
    nd = len(shape)
    return pl.BlockSpec(shape, lambda *_: (0,) * nd)


def _segments(d, ncol):
    segs = []
    edges = sorted({j * ncol for j in range(N_CHIP + 1)} | {s * d for s in range(N_SPLIT + 1)})
    for lo, hi in zip(edges[:-1], edges[1:]):
        segs.append((lo // ncol, lo % ncol, lo // d, lo % d, hi - lo))
    return segs


def _sigmoid(x):
    return 0.5 * jnp.tanh(0.5 * x) + 0.5


_GELU_C = math.sqrt(2.0 / math.pi)


_GELU_CA = _GELU_C * 0.044715


def _gelu(x):
    return x * (0.5 * jnp.tanh(x * (_GELU_C + _GELU_CA * (x * x))) + 0.5)


def _gelu_and_grad(x):
    x2 = x * x
    u = 0.5 * jnp.tanh(x * (_GELU_C + _GELU_CA * x2)) + 0.5
    slope = (1.0 - u) * (x * (_GELU_C + (3.0 * _GELU_CA) * x2))
    return x * u, u * (2.0 * slope + 1.0)


def _split_bf16(a):
    hi = a.astype(BF16)
    lo = (a - hi.astype(F32)).astype(BF16)
    return hi, lo


def _in_proj_fwd(x2d, g_in, halves):
    t, d = x2d.shape
    n = len(halves)
    ncol = halves[0].shape[2]
    n_row = t // TMF
    last = n_row - 1
    assert halves[0].shape[1] * 2 == d
    qkv_parts = {j: (max(j * ncol, 3 * d) - j * ncol, max(j * ncol, 3 * d) - 3 * d)
                 for j in range(N_CHIP) if min((j + 1) * ncol, 6 * d) > max(j * ncol, 3 * d)}
    qkv_w = 3 * d // len(qkv_parts)
    assert all(min((j + 1) * ncol, 6 * d) - max(j * ncol, 3 * d) == qkv_w and q0 % qkv_w == 0
               for j, (_, q0) in qkv_parts.items())

    def body(order_ref, qrow_ref, qcol_ref, x_ref, g_ref, *rest):
        ins = rest[:n]
        proj_ref, qkv_ref, ht_ref = rest[n:n + 3]
        outs = rest[n + 3:2 * n + 3]
        wbuf, h_all, send_sems, recv_sems, local_sems, load_sem = rest[2 * n + 3:]
        phase = pl.program_id(0)
        i = pl.program_id(1)
        x_pos, y_pos, c_pos, chips = _place()
        sibling = (x_pos, y_pos, 1 - c_pos)
        me = (x_pos, y_pos, c_pos)
        my_chip = 2 * x_pos + y_pos

        def copy(a, k, block, to, src=None):
            return pltpu.make_async_remote_copy(
                src_ref=outs[a].at[block] if src is None else src, dst_ref=outs[a].at[block],
                send_sem=send_sems.at[a, k], recv_sem=recv_sems.at[a, k], device_id=to, device_id_type=MESH)

        def local(a):
            return pltpu.make_async_copy(ins[a], outs[a].at[pl.ds(2 * my_chip, 2)], local_sems.at[a])

        def load(src, first, slot):
            for half in range(2):
                cp = pltpu.make_async_copy(src.at[first + half], wbuf.at[slot, pl.ds(half * (d // 2), d // 2)], load_sem)
                cp.start()
                cp.wait()

        def send_mine(k):
            px, py = chips[k]
            for a in range(n):
                copy(a, k, 2 * my_chip + c_pos, (px, py, c_pos), src=ins[a].at[c_pos]).start()

        @pl.when((phase == 0) & (i == 0))
        def _():
            for a in range(n):
                local(a).start()
            send_mine(0)
            send_mine(1)
            load(ins[0], 0, 0)

        for k, (px, py) in enumerate(chips):
            @pl.when((phase == k + 1) & (i == 0))
            def _(k=k, px=px, py=py):
                theirs = 2 * (2 * px + py)
                for a in range(n):
                    copy(a, k, theirs + c_pos, me).wait_recv()
                    copy(a, 3 + k, theirs + c_pos, sibling).start()
                if k == 0:
                    send_mine(2)
                for a in range(n):
                    copy(a, 3 + k, theirs + 1 - c_pos, me).wait_recv()
                load(outs[0], theirs, (k + 1) % 2)

        @pl.when(phase == 0)
        def _():
            x = x_ref[...]
            r = lax.rsqrt(jnp.mean(x * x, axis=-1, keepdims=True) + EPS)
            hf = x * r * g_ref[...]
            h_all[i] = hf.astype(BF16)
            ht_ref[...] = hf.T.astype(BF16)

        for slot in range(2):
            @pl.when(phase % 2 == slot)
            def _(slot=slot):
                proj_ref[...] = jnp.dot(h_all[i], wbuf[slot], preferred_element_type=F32)

        for chunk, (c0, _) in qkv_parts.items():
            @pl.when(order_ref[phase] == chunk)
            def _(c0=c0):
                qkv_ref[...] = proj_ref[:, c0:c0 + qkv_w].astype(BF16)

        @pl.when((phase == N_CHIP - 1) & (i == n_row - 1))
        def _():
            for a in range(n):
                for k in range(2 * (N_CHIP - 1)):
                    copy(a, k, 0, me).wait_send()
                local(a).wait()

    x_pos, y_pos = lax.axis_index("x"), lax.axis_index("y")
    order = jnp.stack([2 * x_pos + y_pos, 2 * (1 - x_pos) + y_pos, 2 * x_pos + 1 - y_pos,
                       2 * (1 - x_pos) + 1 - y_pos]).astype(jnp.int32)
    holds = [functools.reduce(jnp.logical_or, [order[p] == j for j in qkv_parts]) for p in range(N_CHIP)]
    col = [sum(jnp.where(order[p] == j, q0 // qkv_w, 0) for j, (_, q0) in qkv_parts.items()) for p in range(N_CHIP)]
    cur = col[-1]
    for p in reversed(range(N_CHIP - 1)):
        cur = jnp.where(holds[p], col[p], cur)
    seen = jnp.bool_(False)
    qrow, qcol = [], []
    for p in range(N_CHIP):
        cur = jnp.where(holds[p], col[p], cur)
        qrow.append(jnp.where(holds[p], -1, jnp.where(seen, last, 0)))
        qcol.append(cur)
        seen = seen | holds[p]
    qrow = jnp.stack(qrow).astype(jnp.int32)
    qcol = jnp.stack(qcol).astype(jnp.int32)

    outs = pl.pallas_call(
        body, name="in_proj_fwd",
        grid_spec=pltpu.PrefetchScalarGridSpec(
            num_scalar_prefetch=3, grid=(N_CHIP, n_row),
            in_specs=[pl.BlockSpec((TMF, d), lambda p, i, order, qrow, qcol: (jnp.where(p == 0, i, last), 0)),
                      pl.BlockSpec((1, d), lambda p, i, order, qrow, qcol: (0, 0))] + [ANY] * n,
            out_specs=[pl.BlockSpec((TMF, ncol), lambda p, i, order, qrow, qcol: (i, order[p])),
                       pl.BlockSpec((TMF, qkv_w),
                                    lambda p, i, order, qrow, qcol: (jnp.where(qrow[p] < 0, i, qrow[p]), qcol[p])),
                       pl.BlockSpec((d, TMF), lambda p, i, order, qrow, qcol: (0, jnp.where(p == 0, i, last)))]
            + [ANY] * n,
            scratch_shapes=[pltpu.VMEM((2, d, ncol), BF16), pltpu.VMEM((n_row, TMF, d), BF16),
                            pltpu.SemaphoreType.DMA((n, 2 * (N_CHIP - 1))), pltpu.SemaphoreType.DMA((n, 2 * (N_CHIP - 1))),
                            pltpu.SemaphoreType.DMA((n,)), pltpu.SemaphoreType.DMA]),
        out_shape=[jax.ShapeDtypeStruct((t, N_CHIP * ncol), F32), jax.ShapeDtypeStruct((t, 3 * d), BF16),
                   jax.ShapeDtypeStruct((d, t), BF16)]
        + [jax.ShapeDtypeStruct((2 * N_CHIP,) + hv.shape[1:], BF16) for hv in halves],
        compiler_params=_params(("arbitrary", "arbitrary")),
    )(order, qrow, qcol, x2d, g_in, *halves)
    return outs[0], outs[1], outs[2], outs[3:]


def _branch_a_fwd(a_pre, g_v, wm, b_t):
    t = a_pre.shape[0]
    d = g_v.shape[1]
    d3 = 3 * d
    ng, chunk, _ = wm.shape
    cw = d // ng

    per_step = CHUNKS_PER_STEP if t % (CHUNKS_PER_STEP * chunk) == 0 else 1

    def body(a_ref, gv_ref, wm_ref, bt_ref, ya_ref):
        for n in range(per_step):
            rows = slice(n * chunk, (n + 1) * chunk)
            ua = _gelu(a_ref[rows, 0:d])
            vg = _gelu(a_ref[rows, d:2 * d])
            za = a_ref[rows, 2 * d:3 * d]
            rv = lax.rsqrt(jnp.mean(vg * vg, axis=-1, keepdims=True) + EPS)
            va = (vg * rv * gv_ref[...]).astype(BF16)
            gate = ua * (za * _sigmoid(za))
            for g in range(ng):
                sl = slice(g * cw, (g + 1) * cw)
                mixed = jnp.dot(wm_ref[g], va[:, sl], preferred_element_type=F32) + bt_ref[:, g:g + 1]
                ya_ref[rows, sl] = (gate[:, sl] * mixed).astype(BF16)

    tile = per_step * chunk
    return pl.pallas_call(
        body, name="branch_a_fwd",
        grid=(t // tile,),
        in_specs=[pl.BlockSpec((tile, d3), lambda i: (i, 0)), _const((1, d)), _const(wm.shape), _const(b_t.shape)],
        out_specs=pl.BlockSpec((tile, d), lambda i: (i, 0)),
        out_shape=jax.ShapeDtypeStruct((t, d), BF16),
        compiler_params=_params(("arbitrary",)),
    )(a_pre, g_v, wm, b_t)


def _branch_a_bwd(a_pre, dya, g_v, wm, wm_t, b_t):
    t = a_pre.shape[0]
    d = g_v.shape[1]
    d3 = 3 * d
    ng, chunk, _ = wm.shape
    cw = d // ng
    nsteps = t // chunk

    def body(a_ref, dya_ref, gv_ref, wm_ref, wmt_ref, bt_ref, da_ref, gws_ref, gbt_ref, gnv_ref, db_acc):
        i = pl.program_id(0)

        @pl.when(i == 0)
        def _():
            gws_ref[...] = jnp.zeros_like(gws_ref)
            gnv_ref[...] = jnp.zeros_like(gnv_ref)
            db_acc[...] = jnp.zeros_like(db_acc)

        ua, dgelu_u = _gelu_and_grad(a_ref[:, 0:d])
        vg, dgelu_v = _gelu_and_grad(a_ref[:, d:2 * d])
        za = a_ref[:, 2 * d:3 * d]
        sig = _sigmoid(za)
        sz = za * sig
        dsz = sig * (1.0 + za * (1.0 - sig))
        rv = lax.rsqrt(jnp.mean(vg * vg, axis=-1, keepdims=True) + EPS)
        nv = vg * rv
        gv = gv_ref[...]
        va = (nv * gv).astype(BF16)
        dya = dya_ref[...]
        dmix = dya * ua * sz
        db_acc[...] += dmix
        dmix_b = dmix.astype(BF16)
        t_gate = dya * sz
        t_z = dya * ua * dsz
        dva_parts = []
        for g in range(ng):
            sl = slice(g * cw, (g + 1) * cw)
            mixed = jnp.dot(wm_ref[g], va[:, sl], preferred_element_type=F32) + bt_ref[:, g:g + 1]
            da_ref[:, sl] = (t_gate[:, sl] * mixed * dgelu_u[:, sl]).astype(BF16)
            da_ref[:, 2 * d + g * cw:2 * d + (g + 1) * cw] = (t_z[:, sl] * mixed).astype(BF16)
            gws_ref[g] += lax.dot_general(dmix_b[:, sl], va[:, sl], NT, preferred_element_type=F32)
            dva_parts.append(jnp.dot(wmt_ref[g], dmix_b[:, sl], preferred_element_type=F32))
        dva = jnp.concatenate(dva_parts, axis=1)
        gnv_ref[...] += jnp.sum(dva * nv, axis=0, keepdims=True)
        dnv = dva * gv
        dvg = rv * (dnv - nv * jnp.mean(dnv * nv, axis=-1, keepdims=True))
        da_ref[:, d:2 * d] = (dvg * dgelu_v).astype(BF16)

        @pl.when(i == nsteps - 1)
        def _():
            acc = db_acc[...]
            for g in range(ng):
                gbt_ref[:, g:g + 1] = jnp.sum(acc[:, g * cw:(g + 1) * cw], axis=1, keepdims=True)

    return pl.pallas_call(
        body, name="branch_a_bwd",
        grid=(nsteps,),
        in_specs=[pl.BlockSpec((chunk, d3), lambda i: (i, 0)), pl.BlockSpec((chunk, d), lambda i: (i, 0)),
                  _const((1, d)), _const(wm.shape), _const(wm_t.shape), _const(b_t.shape)],
        out_specs=[pl.BlockSpec((chunk, d3), lambda i: (i, 0)), _const(wm.shape), _const(b_t.shape), _const((1, d))],
        out_shape=[jax.ShapeDtypeStruct((t, d3), BF16), jax.ShapeDtypeStruct(wm.shape, F32),
                   jax.ShapeDtypeStruct(b_t.shape, F32), jax.ShapeDtypeStruct((1, d), F32)],
        scratch_shapes=[pltpu.VMEM((chunk, d), F32)],
        compiler_params=_params(("arbitrary",)),
    )(a_pre, dya, g_v, wm, wm_t, b_t)


def _tri(n, rows_gt_cols):
    r = lax.broadcasted_iota(jnp.int32, (n, n), 0)
    c = lax.broadcasted_iota(jnp.int32, (n, n), 1)
    return (r > c) if rows_gt_cols else (r < c)


def _twice(tri):
    t = tri.astype(BF16)
    return jnp.concatenate([t, t], axis=0)


def _cumsum_mm(a, tri2):
    hi, lo = _split_bf16(a)
    return jnp.dot(jnp.concatenate([hi, lo], axis=1), tri2, preferred_element_type=F32)


LOG2E = 1.4426950408889634
_SIGN = 0x80000000


def _sb_block(q, k, scale, upper2, causal):
    z2 = lax.dot_general(q, k, NT, preferred_element_type=F32) * (scale * LOG2E)
    neg_abs = lax.bitcast_convert_type(lax.bitcast_convert_type(z2, jnp.uint32) | jnp.uint32(_SIGN), F32)
    l2 = jnp.log(1.0 + jnp.exp2(neg_abs)) * LOG2E
    log_beta = jnp.minimum(z2, 0.0) - l2
    lom = log_beta - z2
    if causal is not None:
        lom = jnp.where(causal, lom, 0.0)
    sx = _cumsum_mm(lom, upper2)
    return log_beta, sx, sx[:, 0:1] + lom[:, 0:1]


DEAD_LOG2 = -150.0


def _max_carry(carries):
    return jnp.max(functools.reduce(jnp.maximum, carries))


ZB_GROUP, GA_GROUP, GB_GROUP = 6, 7, 8


def _attn_specs(d, seq, nq, heads_per_step):
    hp_w = heads_per_step * (d // HEADS)
    n_hp = d // hp_w
    row_blk = lambda group: pl.BlockSpec((ATT_T, hp_w), lambda b, h, i: (b * nq + i, group * n_hp + h))
    seq_blk = lambda group: pl.BlockSpec((seq, hp_w), lambda b, h, i: (b, group * n_hp + h))
    return row_blk, seq_blk, n_hp


def _attn_fwd(qkv, proj, bsz, seq):
    t, d3 = qkv.shape
    d = d3 // 3
    hd = d // HEADS
    nq = seq // ATT_T
    scale = hd ** -0.5
    n_heads = ATT_HP
    row_blk, seq_blk, n_hp = _attn_specs(d, seq, nq, n_heads)

    def body(q_ref, k_ref, v_ref, zb_ref, o_ref, yb_ref):
        i = pl.program_id(2)
        causal = _tri(ATT_T, True)
        upper2 = _twice(causal)

        def step(kb, state, mask):
            rows = pl.ds(pl.multiple_of(kb * ATT_T, ATT_T), ATT_T)
            heads = [slice(h * hd, (h + 1) * hd) for h in range(n_heads)]
            scores = [_sb_block(q_ref[:, cols], k_ref[rows, cols], scale, upper2, mask) for cols in heads]
            new = []
            for cols, (carry, acc), (log_beta, sx, total) in zip(heads, state, scores):
                a = jnp.exp2(log_beta + sx + carry)
                if mask is not None:
                    a = jnp.where(mask, a, 0.0)
                acc = acc + jnp.dot(a.astype(BF16), v_ref[rows, cols], preferred_element_type=F32)
                new.append((carry + total, acc))
            return tuple(new)

        init = tuple((jnp.zeros((ATT_T, 1), F32), jnp.zeros((ATT_T, hd), F32)) for _ in range(n_heads))
        state = step(i, init, causal)
        def more(c):
            new = step(c[0], c[1], None)
            return c[0] - 1, new, _max_carry([s[0] for s in new])

        _, state, _ = lax.while_loop(lambda c: (c[0] >= 0) & (c[2] > DEAD_LOG2), more,
                                     (i - 1, state, _max_carry([s[0] for s in state])))
        for h in range(n_heads):
            cols = slice(h * hd, (h + 1) * hd)
            acc = state[h][1]
            zb = zb_ref[:, cols]
            o_ref[:, cols] = acc
            yb_ref[:, cols] = (acc * (zb * _sigmoid(zb))).astype(BF16)

    return pl.pallas_call(
        body, name="attn_fwd",
        grid=(bsz, n_hp, nq),
        in_specs=[row_blk(0), seq_blk(1), seq_blk(2), row_blk(ZB_GROUP)],
        out_specs=[row_blk(0), row_blk(0)],
        out_shape=[jax.ShapeDtypeStruct((t, d), F32), jax.ShapeDtypeStruct((t, d), BF16)],
        compiler_params=_params(("arbitrary", "arbitrary", "arbitrary")),
    )(qkv, qkv, qkv, proj)


def _attn_bwd(qkv, proj, o, dyb, bsz, seq):
    t, d3 = qkv.shape
    d = d3 // 3
    hd = d // HEADS
    nq = seq // ATT_T
    scale = hd ** -0.5
    row_blk, seq_blk, n_hp = _attn_specs(d, seq, nq, ATT_HP)

    def body(q_ref, k_ref, v_ref, zb_ref, o_ref, dyb_ref, dq_ref, dk_ref, dv_ref, dzb_ref,
             g_s, beta_s, dkt_acc, dvt_acc):
        i = pl.program_id(2)

        @pl.when(i == 0)
        def _():
            dkt_acc[...] = jnp.zeros_like(dkt_acc)
            dvt_acc[...] = jnp.zeros_like(dvt_acc)

        causal = _tri(ATT_T, True)
        upper2 = _twice(causal)
        lower2 = _twice(~causal)
        zb = zb_ref[...]
        sig = _sigmoid(zb)
        dyb_t = dyb_ref[...]
        do_f = dyb_t * (zb * sig)
        do = do_f.astype(BF16)
        do_t = do_f.T.astype(BF16)
        q_t = q_ref[...].astype(F32).T.astype(BF16)
        dzb_ref[...] = (dyb_t * o_ref[...] * (sig * (1.0 + zb * (1.0 - sig)))).astype(BF16)

        def sweep(kb, carries, mask):
            rows = pl.ds(pl.multiple_of(kb * ATT_T, ATT_T), ATT_T)
            heads = [slice(h * hd, (h + 1) * hd) for h in range(ATT_HP)]
            scores = [_sb_block(q_ref[:, cols], k_ref[rows, cols], scale, upper2, mask) for cols in heads]
            das = [lax.dot_general(do[:, cols], v_ref[rows, cols], NT, preferred_element_type=F32) for cols in heads]
            new = []
            for h, (cols, carry, (log_beta, sx, total), da) in enumerate(zip(heads, carries, scores, das)):
                a = jnp.exp2(log_beta + sx + carry)
                beta = jnp.exp2(log_beta)
                if mask is not None:
                    a = jnp.where(mask, a, 0.0)
                    beta = jnp.where(mask, beta, 0.0)
                g_s[h, kb] = a * da
                beta_s[h, kb] = beta
                dvt_acc[kb, cols, :] += jnp.dot(do_t[cols, :], a.astype(BF16), preferred_element_type=F32)
                new.append(carry + total)
            return tuple(new)

        carries = sweep(i, tuple(jnp.zeros((ATT_T, 1), F32) for _ in range(ATT_HP)), causal)

        def more(c):
            new = sweep(c[0], c[1], None)
            return c[0] - 1, new, _max_carry(new)

        last, _, _ = lax.while_loop(lambda c: (c[0] >= 0) & (c[2] > DEAD_LOG2), more, (i - 1, carries, _max_carry(carries)))
        first_kb = last + 1

        def back(kb, state):
            rows = pl.ds(pl.multiple_of(kb * ATT_T, ATT_T), ATT_T)
            heads = [slice(h * hd, (h + 1) * hd) for h in range(ATT_HP)]
            sums = [_cumsum_mm(g_s[h, kb], lower2) for h in range(ATT_HP)]
            new = []
            for h, (cols, (p_carry, dq), px) in enumerate(zip(heads, state, sums)):
                dz = ((g_s[h, kb] - (p_carry + px) * beta_s[h, kb]) * scale).astype(BF16)
                dq = dq + jnp.dot(dz, k_ref[rows, cols], preferred_element_type=F32)
                dkt_acc[kb, cols, :] += jnp.dot(q_t[cols, :], dz, preferred_element_type=F32)
                new.append((p_carry + px[:, ATT_T - 1:ATT_T], dq))
            return tuple(new)

        init = tuple((jnp.zeros((ATT_T, 1), F32), jnp.zeros((ATT_T, hd), F32)) for _ in range(ATT_HP))
        state = lax.fori_loop(first_kb, i + 1, back, init)
        for h in range(ATT_HP):
            dq_ref[:, h * hd:(h + 1) * hd] = state[h][1].astype(BF16)

        @pl.when(i == nq - 1)
        def _():
            for kb in range(nq):
                dk_ref[kb * ATT_T:(kb + 1) * ATT_T, :] = dkt_acc[kb].T.astype(BF16)
                dv_ref[kb * ATT_T:(kb + 1) * ATT_T, :] = dvt_acc[kb].T.astype(BF16)

    out = jax.ShapeDtypeStruct((t, d), BF16)
    hp_w = ATT_HP * hd
    return pl.pallas_call(
        body, name="attn_bwd",
        grid=(bsz, n_hp, nq),
        in_specs=[row_blk(0), seq_blk(1), seq_blk(2), row_blk(ZB_GROUP), row_blk(0), row_blk(0)],
        out_specs=[row_blk(0), seq_blk(0), seq_blk(0), row_blk(0)],
        out_shape=[out, out, out, out],
        scratch_shapes=[pltpu.VMEM((ATT_HP, nq, ATT_T, ATT_T), F32), pltpu.VMEM((ATT_HP, nq, ATT_T, ATT_T), F32),
                        pltpu.VMEM((nq, hp_w, ATT_T), F32), pltpu.VMEM((nq, hp_w, ATT_T), F32)],
        compiler_params=_params(("arbitrary", "arbitrary", "arbitrary")),
    )(qkv, qkv, qkv, proj, o, dyb)


def _out_proj(ya, yb, g_pre, x2d, tgt, w_og, w_osb, w_out, g_f):
    t, d = x2d.shape

    def body(ya_ref, yb_ref, ga_ref, gb_ref, x_ref, tgt_ref, wog_ref, wosb_ref, wout_ref, gf_ref,
             dya_ref, dyb_ref, dg_ref, dx2_ref, loss_ref, gnf_ref, gwog_ref, gwosb_ref, gwout_ref):
        @pl.when(pl.program_id(0) == 0)
        def _():
            loss_ref[...] = jnp.zeros_like(loss_ref)
            gnf_ref[...] = jnp.zeros_like(gnf_ref)
            gwog_ref[...] = jnp.zeros_like(gwog_ref)
            gwosb_ref[...] = jnp.zeros_like(gwosb_ref)
            gwout_ref[...] = jnp.zeros_like(gwout_ref)

        ya = ya_ref[...]
        yb = yb_ref[...]
        pa = jnp.dot(ya, wog_ref[...], preferred_element_type=F32)
        pb = jnp.dot(yb, wosb_ref[...], preferred_element_type=F32)
        sga = _sigmoid(ga_ref[...])
        sgb = _sigmoid(gb_ref[...])
        merged = (sga * pa + sgb * pb).astype(BF16)
        x2 = x_ref[...] + jnp.dot(merged, wout_ref[...], preferred_element_type=F32)
        r2 = lax.rsqrt(jnp.mean(x2 * x2, axis=-1, keepdims=True) + EPS)
        n2 = x2 * r2
        gf = gf_ref[...]
        err = n2 * gf - tgt_ref[...]
        loss_ref[...] += 0.5 * jnp.sum(jnp.sum(err * err, axis=-1, keepdims=True), axis=0, keepdims=True) / d
        dy = err * (1.0 / d)
        gnf_ref[...] += jnp.sum(dy * n2, axis=0, keepdims=True)
        dn = dy * gf
        dx2 = r2 * (dn - n2 * jnp.mean(dn * n2, axis=-1, keepdims=True))
        dx2_ref[...] = dx2
        dx2_b = dx2.astype(BF16)
        dmerged = lax.dot_general(dx2_b, wout_ref[...], NT, preferred_element_type=F32)
        gwout_ref[...] += lax.dot_general(merged, dx2_b, TN, preferred_element_type=F32)
        dg_ref[:, 0:d] = (dmerged * pa * (sga * (1.0 - sga))).astype(BF16)
        dg_ref[:, d:2 * d] = (dmerged * pb * (sgb * (1.0 - sgb))).astype(BF16)
        dpa = (dmerged * sga).astype(BF16)
        dpb = (dmerged * sgb).astype(BF16)
        dya_ref[...] = lax.dot_general(dpa, wog_ref[...], NT, preferred_element_type=F32)
        dyb_ref[...] = lax.dot_general(dpb, wosb_ref[...], NT, preferred_element_type=F32)
        gwog_ref[...] += lax.dot_general(ya, dpa, TN, preferred_element_type=F32)
        gwosb_ref[...] += lax.dot_general(yb, dpb, TN, preferred_element_type=F32)

    row = lambda i: (i, 0)
    return pl.pallas_call(
        body, name="out_proj",
        grid=(t // TM,),
        in_specs=[pl.BlockSpec((TM, d), row), pl.BlockSpec((TM, d), row),
                  pl.BlockSpec((TM, d), lambda i: (i, GA_GROUP)), pl.BlockSpec((TM, d), lambda i: (i, GB_GROUP)),
                  pl.BlockSpec((TM, d), row), pl.BlockSpec((TM, d), row),
                  _resident((d, d)), _resident((d, d)), _resident((d, d)), _const((1, d))],
        out_specs=[pl.BlockSpec((TM, d), row), pl.BlockSpec((TM, d), row), pl.BlockSpec((TM, 2 * d), row),
                   pl.BlockSpec((TM, d), row), _const((1, 1)), _const((1, d)),
                   _const((d, d)), _const((d, d)), _const((d, d))],
        out_shape=[jax.ShapeDtypeStruct((t, d), F32), jax.ShapeDtypeStruct((t, d), F32),
                   jax.ShapeDtypeStruct((t, 2 * d), BF16), jax.ShapeDtypeStruct((t, d), F32),
                   jax.ShapeDtypeStruct((1, 1), F32), jax.ShapeDtypeStruct((1, d), F32),
                   jax.ShapeDtypeStruct((d, d), F32), jax.ShapeDtypeStruct((d, d), F32),
                   jax.ShapeDtypeStruct((d, d), F32)],
        compiler_params=_params(("arbitrary",)),
    )(ya, yb, g_pre, g_pre, x2d, tgt, w_og, w_osb, w_out, g_f)


def _dproj_pieces(d):
    return [(0, 0, 3), (1, 3, 1), (2, 4, 1), (3, 5, 1), (4, 6, 1), (5, 7, 2)]


def _in_proj_bwd_x(pieces, wg, x2d, g_in, dx2, gw16):
    t, d = x2d.shape
    ncol = wg.shape[2]
    segs = _segments(d, ncol)
    layout = _dproj_pieces(d)
    nsteps = t // TM

    def body(da_ref, dq_ref, dk_ref, dv_ref, dzb_ref, dg_ref, w_ref, x_ref, g_ref, dx2_ref, gw16_ref,
             gx_ref, gn_ref, recv_ref, send_sems, recv_sems):
        x_pos, y_pos, c_pos, chips = _place()

        def share(k, chunk):
            px, py = chips[k]
            return pltpu.make_async_remote_copy(
                src_ref=gw16_ref.at[:, chunk * ncol:(chunk + 1) * ncol], dst_ref=recv_ref.at[k],
                send_sem=send_sems.at[k], recv_sem=recv_sems.at[k], device_id=(px, py, c_pos), device_id_type=MESH)

        @pl.when(pl.program_id(0) == 0)
        def _():
            gn_ref[...] = jnp.zeros_like(gn_ref)
            for k, (px, py) in enumerate(chips):
                for chunk in range(N_CHIP):
                    @pl.when(2 * px + py == chunk)
                    def _(k=k, chunk=chunk):
                        share(k, chunk).start()

        @pl.when(pl.program_id(0) == nsteps - 1)
        def _():
            for k in range(N_CHIP - 1):
                share(k, 0).wait()

        refs = (da_ref, dq_ref, dk_ref, dv_ref, dzb_ref, dg_ref)
        dh = jnp.zeros((TM, d), F32)
        for chip, c0, grp, s0, width in segs:
            piece, first, _ = next(p for p in layout if p[1] <= grp < p[1] + p[2])
            off = (grp - first) * d + s0
            dh = dh + lax.dot_general(refs[piece][:, off:off + width], w_ref[chip, :, c0:c0 + width], NT,
                                      preferred_element_type=F32)
        x = x_ref[...]
        r = lax.rsqrt(jnp.mean(x * x, axis=-1, keepdims=True) + EPS)
        n = x * r
        gn_ref[...] += jnp.sum(dh * n, axis=0, keepdims=True)
        dn = dh * g_ref[...]
        gx_ref[...] = dx2_ref[...] + r * (dn - n * jnp.mean(dn * n, axis=-1, keepdims=True))

    row = lambda i: (i, 0)
    return pl.pallas_call(
        body, name="in_proj_bwd_x",
        grid=(t // TM,),
        in_specs=[pl.BlockSpec((TM, p.shape[1]), row) for p in pieces]
        + [_resident(wg.shape), pl.BlockSpec((TM, d), row), _const((1, d)), pl.BlockSpec((TM, d), row), ANY],
        out_specs=[pl.BlockSpec((TM, d), row), _const((1, d)), ANY],
        out_shape=[jax.ShapeDtypeStruct((t, d), F32), jax.ShapeDtypeStruct((1, d), F32),
                   jax.ShapeDtypeStruct((N_CHIP - 1, gw16.shape[0], ncol), BF16)],
        scratch_shapes=[pltpu.SemaphoreType.DMA((N_CHIP - 1,)), pltpu.SemaphoreType.DMA((N_CHIP - 1,))],
        compiler_params=_params(("arbitrary",)),
    )(*pieces, wg, x2d, g_in, dx2, gw16)


def _in_proj_bwd_w(h_t, pieces, mats16, pack):
    d, t = h_t.shape
    nk = t // TKW
    half = d // 2
    layout = _dproj_pieces(d)
    n_mats = len(mats16)
    n_dev = 2 * N_CHIP
    flips = [(dx, dy, dc) for dx in (0, 1) for dy in (0, 1) for dc in (0, 1)][1:]

    def body(ht_ref, da_ref, dq_ref, dk_ref, dv_ref, dzb_ref, dg_ref, *rest):
        mat_refs, pack_ref = rest[:n_mats], rest[n_mats]
        gw_ref, sib_ref = rest[n_mats + 1:n_mats + 3]
        recv_refs, slots_ref = rest[n_mats + 3:2 * n_mats + 3], rest[2 * n_mats + 3]
        acc, stage, mat_send, mat_recv, pack_send, pack_recv, own_sem, stage_send, sib_recv = rest[2 * n_mats + 4:]
        s = pl.program_id(0)
        i = pl.program_id(1)
        x_pos, y_pos, c_pos, chips = _place()
        me = 4 * x_pos + 2 * y_pos + c_pos

        def to_sibling(slot, group):
            return pltpu.make_async_remote_copy(
                src_ref=stage.at[slot], dst_ref=sib_ref.at[:, group * d:(group + 1) * d],
                send_sem=stage_send.at[slot], recv_sem=sib_recv, device_id=(x_pos, y_pos, 1 - c_pos), device_id_type=MESH)

        def exchanges():
            cps = []
            for k, (px, py) in enumerate(chips):
                for a in range(n_mats):
                    cps.append(pltpu.make_async_remote_copy(
                        src_ref=mat_refs[a].at[2 * px + py], dst_ref=recv_refs[a].at[k],
                        send_sem=mat_send.at[a, k], recv_sem=mat_recv.at[a, k],
                        device_id=(px, py, c_pos), device_id_type=MESH))
            for k, (dx, dy, dc) in enumerate(flips):
                peer = (1 - x_pos if dx else x_pos, 1 - y_pos if dy else y_pos, 1 - c_pos if dc else c_pos)
                cps.append(pltpu.make_async_remote_copy(
                    src_ref=pack_ref, dst_ref=slots_ref.at[me], send_sem=pack_send.at[k], recv_sem=pack_recv.at[k],
                    device_id=peer, device_id_type=MESH))
            return cps, pltpu.make_async_copy(pack_ref, slots_ref.at[me], own_sem)

        @pl.when((s == 0) & (i == 0))
        def _():
            cps, own = exchanges()
            own.start()
            for cp in cps:
                cp.start()

        @pl.when(i == 0)
        def _():
            acc[...] = jnp.zeros_like(acc)

        refs = (da_ref, dq_ref, dk_ref, dv_ref, dzb_ref, dg_ref)
        for piece, first, count in layout:
            @pl.when((s >= first) & (s < first + count))
            def _(piece=piece):
                acc[...] += jnp.dot(ht_ref[...], refs[piece][...], preferred_element_type=F32)

        @pl.when(i == nk - 1)
        def _():
            gw_ref[...] = acc[...]
            for slot in range(2):
                @pl.when(s % 2 == slot)
                def _(slot=slot):
                    @pl.when(s >= 2)
                    def _():
                        to_sibling(slot, 0).wait_send()
                    for other in range(2):
                        @pl.when(c_pos == 1 - other)
                        def _(other=other):
                            stage[slot] = acc[other * half:(other + 1) * half, :].astype(BF16)
                    for group in range(N_SPLIT):
                        @pl.when(s == group)
                        def _(group=group):
                            to_sibling(slot, group).start()

        @pl.when((s == N_SPLIT - 1) & (i == nk - 1))
        def _():
            for slot in range(2):
                to_sibling(slot, 0).wait_send()
            pltpu.make_async_remote_copy(
                src_ref=sib_ref, dst_ref=sib_ref, send_sem=stage_send.at[0], recv_sem=sib_recv,
                device_id=(x_pos, y_pos, c_pos), device_id_type=MESH).wait_recv()
            cps, own = exchanges()
            own.wait()
            for cp in cps:
                cp.wait()

    def piece_spec(p, first, count):
        def index(s, i):
            mine = (s >= first) & (s < first + count)
            return jnp.where(mine, i, 0), jnp.where(mine, s - first, 0)
        return pl.BlockSpec((TKW, d), index)

    col_blk = pl.BlockSpec((d, d), lambda s, i: (0, s))
    outs = pl.pallas_call(
        body, name="in_proj_bwd_w",
        grid=(N_SPLIT, nk),
        in_specs=[pl.BlockSpec((d, TKW), lambda s, i: (0, i))] + [piece_spec(*p) for p in layout] + [ANY] * (n_mats + 1),
        out_specs=[col_blk, ANY] + [ANY] * (n_mats + 1),
        out_shape=[jax.ShapeDtypeStruct((d, N_SPLIT * d), F32), jax.ShapeDtypeStruct((half, N_SPLIT * d), BF16)]
        + [jax.ShapeDtypeStruct((N_CHIP - 1,) + m.shape[1:], BF16) for m in mats16]
        + [jax.ShapeDtypeStruct((n_dev,) + pack.shape, F32)],
        scratch_shapes=[pltpu.VMEM((d, d), F32), pltpu.VMEM((2, half, d), BF16),
                        pltpu.SemaphoreType.DMA((n_mats, N_CHIP - 1)), pltpu.SemaphoreType.DMA((n_mats, N_CHIP - 1)),
                        pltpu.SemaphoreType.DMA((n_dev - 1,)), pltpu.SemaphoreType.DMA((n_dev - 1,)),
                        pltpu.SemaphoreType.DMA, pltpu.SemaphoreType.DMA((2,)), pltpu.SemaphoreType.DMA],
        compiler_params=_params(("arbitrary", "arbitrary")),
    )(h_t, *pieces, *mats16, pack)
    return outs[0], outs[1], outs[2:2 + n_mats], outs[2 + n_mats]


def _local_step(proj, qkv, x2d, tgt2d, bsz, seq, norm_v, w_s, b_s, w_og, w_osb, w_out, norm_final):
    d = x2d.shape[1]
    chunk = w_s.shape[-1]
    causal = jnp.tril(jnp.ones((chunk, chunk), dtype=bool))
    wm = jnp.where(causal[None], w_s, 0.0).astype(BF16)
    wm_t = jnp.swapaxes(wm, 1, 2)
    b_t = b_s.T

    ya = _branch_a_fwd(proj, norm_v, wm, b_t)
    o, yb = _attn_fwd(qkv, proj, bsz, seq)
    dya, dyb, dg, dx2, loss, g_nf, g_wog, g_wosb, g_wout = _out_proj(
        ya, yb, proj, x2d, tgt2d, w_og, w_osb, w_out, norm_final.reshape(1, d))
    dq, dk, dv, dzb = _attn_bwd(qkv, proj, o, dyb, bsz, seq)
    d_a, g_ws, g_bt, g_nv = _branch_a_bwd(proj, dya, norm_v, wm, wm_t, b_t)
    g_ws = jnp.where(causal[None], g_ws, 0.0)
    return loss, (d_a, dq, dk, dv, dzb, dg), dx2, g_nv, g_ws, g_bt.T, g_wog, g_wosb, g_wout, g_nf


def _row_tile(rows):
    return next(r for r in (128, 64, 32, 16, 8) if rows % r == 0)


def _cast_bf16(arrs):
    n = len(arrs)

    def body(*refs):
        for a_ref, o_ref in zip(refs[:n], refs[n:]):
            o_ref[...] = a_ref[...].astype(BF16)

    specs = [pl.BlockSpec((a.shape[0] // CAST_STEPS, a.shape[1]), lambda i: (i, 0)) for a in arrs]
    return pl.pallas_call(
        body, name="cast_bf16", grid=(CAST_STEPS,),
        in_specs=specs, out_specs=specs,
        out_shape=[jax.ShapeDtypeStruct(a.shape, BF16) for a in arrs],
        compiler_params=_params(("arbitrary",)),
    )(*arrs)


def _chip_half(full, from_sibling, core, tile):
    half, n = from_sibling.shape

    def body(core_ref, own_ref, sib_ref, o32_ref, o16_ref):
        total = own_ref[...] + sib_ref[...].astype(F32)
        o32_ref[...] = total
        o16_ref[...] = total.astype(BF16)

    blk = pl.BlockSpec((half, tile), lambda j, core_ref: (0, j))
    return pl.pallas_call(
        body, name="chip_half",
        grid_spec=pltpu.PrefetchScalarGridSpec(
            num_scalar_prefetch=1, grid=(n // tile,),
            in_specs=[pl.BlockSpec((half, tile), lambda j, core_ref: (core_ref[0], j)), blk],
            out_specs=[blk, blk]),
        out_shape=[jax.ShapeDtypeStruct((half, n), F32), jax.ShapeDtypeStruct((half, n), BF16)],
        compiler_params=_params(("arbitrary",)),
    )(core.reshape(1).astype(jnp.int32), full, from_sibling)


def _add_received(full, recv, chip, by_cols):
    _, rows, cols = recv.shape
    tr = _row_tile(rows)
    nb = rows // tr

    def body(chip_ref, own_ref, recv_ref, o_ref):
        s = own_ref[...]
        for k in range(N_CHIP - 1):
            s = s + recv_ref[k].astype(F32)
        o_ref[...] = s

    own_map = (lambda i, chip_ref: (i, chip_ref[0])) if by_cols else (lambda i, chip_ref: (chip_ref[0] * nb + i, 0))
    return pl.pallas_call(
        body, name="add_received",
        grid_spec=pltpu.PrefetchScalarGridSpec(
            num_scalar_prefetch=1, grid=(nb,),
            in_specs=[pl.BlockSpec((tr, cols), own_map),
                      pl.BlockSpec((N_CHIP - 1, tr, cols), lambda i, chip_ref: (0, i, 0))],
            out_specs=pl.BlockSpec((tr, cols), lambda i, chip_ref: (i, 0))),
        out_shape=jax.ShapeDtypeStruct((rows, cols), F32),
        compiler_params=_params(("arbitrary",)),
    )(chip.reshape(1).astype(jnp.int32), full, recv)


def _adamw_math(w, m, v, g):
    new_m = ADAM_B1 * m + (1.0 - ADAM_B1) * g
    new_v = ADAM_B2 * v + (1.0 - ADAM_B2) * (g * g)
    m_hat = new_m / (1.0 - ADAM_B1 ** ADAM_STEP)
    v_hat = new_v / (1.0 - ADAM_B2 ** ADAM_STEP)
    return -ADAM_LR * (m_hat / (jnp.sqrt(v_hat) + ADAM_EPS) + ADAM_WD * w), new_m, new_v


def _adamw(w, m, v, g_parts):
    rows, cols = w.shape
    tr = _row_tile(rows)
    n_parts = len(g_parts)

    def body(*refs):
        w_ref, m_ref, v_ref = refs[:3]
        part_refs = refs[3:3 + n_parts]
        g_ref, d_ref, nm_ref, nv_ref = refs[3 + n_parts:]
        g = part_refs[0][...]
        for p in part_refs[1:]:
            g = g + p[...]
        g_ref[...] = g
        d_ref[...], nm_ref[...], nv_ref[...] = _adamw_math(w_ref[...], m_ref[...], v_ref[...], g)

    spec = pl.BlockSpec((tr, cols), lambda i: (i, 0))
    out = jax.ShapeDtypeStruct(w.shape, F32)
    return pl.pallas_call(
        body, name="adamw", grid=(rows // tr,),
        in_specs=[spec] * (3 + n_parts), out_specs=[spec] * 4, out_shape=[out] * 4,
        compiler_params=_params(("arbitrary",)),
    )(w, m, v, *g_parts)


def _adamw_halves(w, m, v, mine, theirs, core):
    rows, cols = w.shape
    tr = _row_tile(rows // 2)
    per_half = rows // 2 // tr

    def body(core_ref, w_ref, m_ref, v_ref, mine_ref, theirs_ref, g_ref, d_ref, nm_ref, nv_ref):
        is_mine = pl.program_id(0) // per_half == core_ref[0]
        for part, cond in ((mine_ref, is_mine), (theirs_ref, jnp.logical_not(is_mine))):
            @pl.when(cond)
            def _(part=part):
                g = part[...]
                g_ref[...] = g
                d_ref[...], nm_ref[...], nv_ref[...] = _adamw_math(w_ref[...], m_ref[...], v_ref[...], g)

    spec = pl.BlockSpec((tr, cols), lambda i, core_ref: (i, 0))

    def half_spec(own):
        def index(i, core_ref):
            in_core_half = i // per_half == core_ref[0]
            here = in_core_half if own else jnp.logical_not(in_core_half)
            return jnp.where(here, i % per_half, 0), 0
        return pl.BlockSpec((tr, cols), index)

    out = jax.ShapeDtypeStruct(w.shape, F32)
    return pl.pallas_call(
        body, name="adamw_halves",
        grid_spec=pltpu.PrefetchScalarGridSpec(
            num_scalar_prefetch=1, grid=(rows // tr,),
            in_specs=[spec] * 3 + [half_spec(True), half_spec(False)], out_specs=[spec] * 4),
        out_shape=[out] * 4,
        compiler_params=_params(("arbitrary",)),
    )(core.reshape(1).astype(jnp.int32), w, m, v, mine, theirs)


def _adamw_small(w, m, v, slots_head, slots_tail):
    n_dev, p0, _ = slots_head.shape

    def body(w_ref, m_ref, v_ref, head_ref, tail_ref, g_ref, d_ref, nm_ref, nv_ref):
        for ref, rows in ((head_ref, slice(0, p0)), (tail_ref, slice(p0, w.shape[0]))):
            g = ref[0]
            for i in range(1, n_dev):
                g = g + ref[i]
            g_ref[rows, :] = g
            d_ref[rows, :], nm_ref[rows, :], nv_ref[rows, :] = _adamw_math(w_ref[rows, :], m_ref[rows, :], v_ref[rows, :], g)

    vmem = pl.BlockSpec(memory_space=pltpu.VMEM)
    out = jax.ShapeDtypeStruct(w.shape, F32)
    return pl.pallas_call(
        body, name="adamw_small", in_specs=[vmem] * 5, out_specs=[vmem] * 4, out_shape=[out] * 4,
        compiler_params=pltpu.CompilerParams(vmem_limit_bytes=VMEM_LIMIT),
    )(w, m, v, slots_head, slots_tail)


ANY = pl.BlockSpec(memory_space=pl.ANY)


def _place():
    x, y, c = lax.axis_index("x"), lax.axis_index("y"), lax.axis_index("c")
    other_chips = [(1 - x, y), (x, 1 - y), (1 - x, 1 - y)]
    return x, y, c, other_chips


def _swap_and_gather(arrs, pack):
    n = len(arrs)
    n_dev = 2 * N_CHIP
    flips = [(dx, dy, dc) for dx in (0, 1) for dy in (0, 1) for dc in (0, 1)][1:]

    def body(*refs):
        ins, pack_ref = refs[:n], refs[n]
        outs, slots_ref = refs[n + 1:2 * n + 1], refs[2 * n + 1]
        send_sems, recv_sems, pack_send, pack_recv, own_sem = refs[2 * n + 2:]
        x, y, c, _ = _place()
        me = 4 * x + 2 * y + c
        own = pltpu.make_async_copy(pack_ref, slots_ref.at[me], own_sem)
        own.start()
        copies = []
        for k, (dx, dy, dc) in enumerate(flips):
            peer = (1 - x if dx else x, 1 - y if dy else y, 1 - c if dc else c)
            copies.append(pltpu.make_async_remote_copy(
                src_ref=pack_ref, dst_ref=slots_ref.at[me], send_sem=pack_send.at[k], recv_sem=pack_recv.at[k],
                device_id=peer, device_id_type=MESH))
        copies += [pltpu.make_async_remote_copy(
            src_ref=ins[a], dst_ref=outs[a], send_sem=send_sems.at[a], recv_sem=recv_sems.at[a],
            device_id=(x, y, 1 - c), device_id_type=MESH) for a in range(n)]
        for cp in copies:
            cp.start()
        for cp in copies:
            cp.wait()
        own.wait()

    return pl.pallas_call(
        body, name="swap_and_gather",
        in_specs=[ANY] * (n + 1), out_specs=[ANY] * (n + 1),
        out_shape=[jax.ShapeDtypeStruct(a.shape, a.dtype) for a in arrs] + [jax.ShapeDtypeStruct((n_dev,) + pack.shape, F32)],
        scratch_shapes=[pltpu.SemaphoreType.DMA((n,)), pltpu.SemaphoreType.DMA((n,)),
                        pltpu.SemaphoreType.DMA((n_dev - 1,)), pltpu.SemaphoreType.DMA((n_dev - 1,)),
                        pltpu.SemaphoreType.DMA],
    )(*arrs, pack)


SLAB = 8


def _slab(vec, d):
    return jnp.pad(vec.reshape(1, d), ((0, SLAB - 1), (0, 0)))


def _pack_tail(vec_nv, b_s, vec_nf, w_s, scalar=None):
    d = vec_nv.shape[-1]
    extra = jnp.zeros((1, d), F32) if scalar is None else jnp.pad(scalar, ((0, 0), (0, d - 1)))
    return jnp.concatenate([_slab(v, d) for v in (vec_nv, b_s, vec_nf, extra)] + [w_s.reshape(-1, d)], axis=0)


def _pack_small(vec_nin, vec_nv, b_s, vec_nf, w_s):
    return jnp.concatenate([_slab(vec_nin, vec_nin.shape[-1]), _pack_tail(vec_nv, b_s, vec_nf, w_s)], axis=0)


def _unpack_small(pack, w_s_shape, b_s_shape):
    return (pack[0:1], pack[SLAB:SLAB + 1], pack[2 * SLAB].reshape(b_s_shape), pack[3 * SLAB],
            pack[5 * SLAB:].reshape(w_s_shape), pack[4 * SLAB, 0])


def kernel(x, norm_in, w_in, norm_v, w_s, b_s, w_o_gmlp, w_o_sb, w_out, norm_final, loss_target, m_norm_in, m_w_in, m_norm_v, m_w_s, m_b_s, m_w_o_gmlp, m_w_o_sb, m_w_out, m_norm_final, v_norm_in, v_w_in, v_norm_v, v_w_s, v_b_s, v_w_o_gmlp, v_w_o_sb, v_w_out, v_norm_final):
    d = x.shape[-1]
    ncol = w_in.shape[-1]
    nrow = w_o_gmlp.shape[-2]
    chip = 2 * lax.axis_index("x") + lax.axis_index("y")

    bsz, seq, _ = x.shape
    x2d = x.reshape(bsz * seq, d)
    shards = [w_in[0], w_o_gmlp[0], w_o_sb[0], w_out[0]]
    halves = [s16.reshape(2, s16.shape[0] // 2, s16.shape[1]) for s16 in _cast_bf16(shards)]
    proj, qkv, h_t, (wg, w_og, w_osb, w_o) = _in_proj_fwd(x2d, norm_in, halves)
    wg = wg.reshape(N_CHIP, d, ncol)

    loss, pieces, dx2, g_nv, g_ws, g_bs, g_wog, g_wosb, g_wout, g_nf = _local_step(
        proj, qkv, x2d, loss_target.reshape(bsz * seq, d), bsz, seq, norm_v, w_s[0], b_s[0],
        w_og.reshape(d, d), w_osb.reshape(d, d), w_o.reshape(d, d), norm_final)

    mats = [g_wog, g_wosb, g_wout]
    mats16 = [g16.reshape(N_CHIP, nrow, d) for g16 in _cast_bf16(mats)]
    g_win, from_sibling, recv_mats, slots_tail = _in_proj_bwd_w(
        h_t, pieces, mats16, _pack_tail(g_nv, g_bs, g_nf, g_ws, loss))
    core = lax.axis_index("c")
    half_win, half_win16 = _chip_half(g_win, from_sibling, core, ncol)
    grad_x, g_nin, recv_win = _in_proj_bwd_x(pieces, wg, x2d, norm_in, dx2, half_win16)
    grad_x = grad_x.reshape(bsz, seq, d)

    sums = [_add_received(half_win, recv_win, chip, True)] + [
        _add_received(g, r, chip, False) for g, r in zip(mats, recv_mats)]
    *sibling_sums, slots_head = _swap_and_gather(sums, _slab(g_nin, d))
    stats = [_adamw_halves(w_in[0], m_w_in[0], v_w_in[0], sums[0], sibling_sums[0], core)]
    for w, m, v, mine, theirs in zip(shards[1:], [m_w_o_gmlp[0], m_w_o_sb[0], m_w_out[0]],
                                     [v_w_o_gmlp[0], v_w_o_sb[0], v_w_out[0]], sums[1:], sibling_sums[1:]):
        stats.append(_adamw(w, m, v, [mine, theirs]))
    (gw_in, dw_in, nm_in, nv_in), (gw_og, dw_og, nm_og, nv_og), (gw_osb, dw_osb, nm_osb, nv_osb), \
        (gw_out, dw_out, nm_out, nv_out) = stats

    gs, ds, ms, vs = _adamw_small(
        _pack_small(norm_in, norm_v, b_s[0], norm_final, w_s[0]),
        _pack_small(m_norm_in, m_norm_v, m_b_s[0], m_norm_final, m_w_s[0]),
        _pack_small(v_norm_in, v_norm_v, v_b_s[0], v_norm_final, v_w_s[0]), slots_head, slots_tail)

    def small(pack):
        nin, nv, bs, nf, ws, _ = _unpack_small(pack, w_s.shape, b_s.shape)
        return nin, nv, ws, bs, nf

    loss = _unpack_small(gs, w_s.shape, b_s.shape)[-1]
    out = []
    for small_pack, win, wog, wosb, wout in ((gs, gw_in, gw_og, gw_osb, gw_out), (ds, dw_in, dw_og, dw_osb, dw_out),
                                             (ms, nm_in, nm_og, nm_osb, nm_out), (vs, nv_in, nv_og, nv_osb, nv_out)):
        nin, nv, ws, bs, nf = small(small_pack)
        out += [nin, win[None], nv, ws, bs, wog[None], wosb[None], wout[None], nf]
    return (loss, grad_x, *out)
```

```python
import functools
import math

import jax
import jax.numpy as jnp
from jax import lax
from jax.experimental import pallas as pl
from jax.experimental.pallas import tpu as pltpu

F32 = jnp.float32
BF16 = jnp.bfloat16
EPS = 1e-6
HEADS = 8
N_SPLIT = 9
N_CHIP = 4
MESH = pl.DeviceIdType.MESH

ADAM_LR = 0.001
ADAM_B1 = 0.9
ADAM_B2 = 0.999
ADAM_EPS = 1e-08
ADAM_WD = 0.01
ADAM_STEP = 10

VMEM_LIMIT = 56 * 2 ** 20
TM = 256
TMF = 512
ATT_T = 256
ATT_HP = 4
TKW = 1024
CHUNKS_PER_STEP = 4
CAST_STEPS = 8

NT = (((1,), (1,)), ((), ()))
TN = (((0,), (0,)), ((), ()))


def _params(sem):
    return pltpu.CompilerParams(dimension_semantics=sem, vmem_limit_bytes=VMEM_LIMIT)


def _resident(shape):
    nd = len(shape)
    return pl.BlockSpec(shape, lambda *_: (0,) * nd, pipeline_mode=pl.Buffered(1))


def _const(shape):
    nd = len(shape)
    return pl.BlockSpec(shape, lambda *_: (0,) * nd)


def _segments(d, ncol):
    segs = []
    edges = sorted({j * ncol for j in range(N_CHIP + 1)} | {s * d for s in range(N_SPLIT + 1)})
    for lo, hi in zip(edges[:-1], edges[1:]):
        segs.append((lo // ncol, lo % ncol, lo // d, lo % d, hi - lo))
    return segs


def _sigmoid(x):
    return 0.5 * jnp.tanh(0.5 * x) + 0.5


_GELU_C = math.sqrt(2.0 / math.pi)


_GELU_CA = _GELU_C * 0.044715


def _gelu(x):
    return x * (0.5 * jnp.tanh(x * (_GELU_C + _GELU_CA * (x * x))) + 0.5)


def _gelu_and_grad(x):
    x2 = x * x
    u = 0.5 * jnp.tanh(x * (_GELU_C + _GELU_CA * x2)) + 0.5
    slope = (1.0 - u) * (x * (_GELU_C + (3.0 * _GELU_CA) * x2))
    return x * u, u * (2.0 * slope + 1.0)


def _split_bf16(a):
    hi = a.astype(BF16)
    lo = (a - hi.astype(F32)).astype(BF16)
    return hi, lo


def _in_proj_fwd(x2d, g_in, halves):
    t, d = x2d.shape
    n = len(halves)
    ncol = halves[0].shape[2]
    n_row = t // TMF
    last = n_row - 1
    assert halves[0].shape[1] * 2 == d
    qkv_parts = {j: (max(j * ncol, 3 * d) - j * ncol, max(j * ncol, 3 * d) - 3 * d)
                 for j in range(N_CHIP) if min((j + 1) * ncol, 6 * d) > max(j * ncol, 3 * d)}
    qkv_w = 3 * d // len(qkv_parts)
    assert all(min((j + 1) * ncol, 6 * d) - max(j * ncol, 3 * d) == qkv_w and q0 % qkv_w == 0
               for j, (_, q0) in qkv_parts.items())

    def body(order_ref, qrow_ref, qcol_ref, x_ref, g_ref, *rest):
        ins = rest[:n]
        proj_ref, qkv_ref, ht_ref = rest[n:n + 3]
        outs = rest[n + 3:2 * n + 3]
        wbuf, h_all, send_sems, recv_sems, local_sems, load_sem = rest[2 * n + 3:]
        phase = pl.program_id(0)
        i = pl.program_id(1)
        x_pos, y_pos, c_pos, chips = _place()
        sibling = (x_pos, y_pos, 1 - c_pos)
        me = (x_pos, y_pos, c_pos)
        my_chip = 2 * x_pos + y_pos

        def copy(a, k, block, to, src=None):
            return pltpu.make_async_remote_copy(
                src_ref=outs[a].at[block] if src is None else src, dst_ref=outs[a].at[block],
                send_sem=send_sems.at[a, k], recv_sem=recv_sems.at[a, k], device_id=to, device_id_type=MESH)

        def local(a):
            return pltpu.make_async_copy(ins[a], outs[a].at[pl.ds(2 * my_chip, 2)], local_sems.at[a])

        def load(src, first, slot):
            for half in range(2):
                cp = pltpu.make_async_copy(src.at[first + half], wbuf.at[slot, pl.ds(half * (d // 2), d // 2)], load_sem)
                cp.start()
                cp.wait()

        def send_mine(k):
            px, py = chips[k]
            for a in range(n):
                copy(a, k, 2 * my_chip + c_pos, (px, py, c_pos), src=ins[a].at[c_pos]).start()

        @pl.when((phase == 0) & (i == 0))
        def _():
            for a in range(n):
                local(a).start()
            send_mine(0)
            send_mine(1)
            send_mine(2)
            load(ins[0], 0, 0)

        for k, (px, py) in enumerate(chips):
            @pl.when((phase == k + 1) & (i == 0))
            def _(k=k, px=px, py=py):
                theirs = 2 * (2 * px + py)
                for a in range(n):
                    copy(a, k, theirs + c_pos, me).wait_recv()
                    copy(a, 3 + k, theirs + c_pos, sibling).start()
                for a in range(n):
                    copy(a, 3 + k, theirs + 1 - c_pos, me).wait_recv()
                load(outs[0], theirs, (k + 1) % 2)

        @pl.when(phase == 0)
        def _():
            x = x_ref[...]
            r = lax.rsqrt(jnp.mean(x * x, axis=-1, keepdims=True) + EPS)
            hf = x * r * g_ref[...]
            h_all[i] = hf.astype(BF16)
            ht_ref[...] = hf.T.astype(BF16)

        for slot in range(2):
            @pl.when(phase % 2 == slot)
            def _(slot=slot):
                proj_ref[...] = jnp.dot(h_all[i], wbuf[slot], preferred_element_type=F32)

        for chunk, (c0, _) in qkv_parts.items():
            @pl.when(order_ref[phase] == chunk)
            def _(c0=c0):
                qkv_ref[...] = proj_ref[:, c0:c0 + qkv_w].astype(BF16)

        @pl.when((phase == N_CHIP - 1) & (i == n_row - 1))
        def _():
            for a in range(n):
                for k in range(2 * (N_CHIP - 1)):
                    copy(a, k, 0, me).wait_send()
                local(a).wait()

    x_pos, y_pos = lax.axis_index("x"), lax.axis_index("y")
    order = jnp.stack([2 * x_pos + y_pos, 2 * (1 - x_pos) + y_pos, 2 * x_pos + 1 - y_pos,
                       2 * (1 - x_pos) + 1 - y_pos]).astype(jnp.int32)
    holds = [functools.reduce(jnp.logical_or, [order[p] == j for j in qkv_parts]) for p in range(N_CHIP)]
    col = [sum(jnp.where(order[p] == j, q0 // qkv_w, 0) for j, (_, q0) in qkv_parts.items()) for p in range(N_CHIP)]
    cur = col[-1]
    for p in reversed(range(N_CHIP - 1)):
        cur = jnp.where(holds[p], col[p], cur)
    seen = jnp.bool_(False)
    qrow, qcol = [], []
    for p in range(N_CHIP):
        cur = jnp.where(holds[p], col[p], cur)
        qrow.append(jnp.where(holds[p], -1, jnp.where(seen, last, 0)))
        qcol.append(cur)
        seen = seen | holds[p]
    qrow = jnp.stack(qrow).astype(jnp.int32)
    qcol = jnp.stack(qcol).astype(jnp.int32)

    outs = pl.pallas_call(
        body, name="in_proj_fwd",
        grid_spec=pltpu.PrefetchScalarGridSpec(
            num_scalar_prefetch=3, grid=(N_CHIP, n_row),
            in_specs=[pl.BlockSpec((TMF, d), lambda p, i, order, qrow, qcol: (jnp.where(p == 0, i, last), 0)),
                      pl.BlockSpec((1, d), lambda p, i, order, qrow, qcol: (0, 0))] + [ANY] * n,
            out_specs=[pl.BlockSpec((TMF, ncol), lambda p, i, order, qrow, qcol: (i, order[p])),
                       pl.BlockSpec((TMF, qkv_w),
                                    lambda p, i, order, qrow, qcol: (jnp.where(qrow[p] < 0, i, qrow[p]), qcol[p])),
                       pl.BlockSpec((d, TMF), lambda p, i, order, qrow, qcol: (0, jnp.where(p == 0, i, last)))]
            + [ANY] * n,
            scratch_shapes=[pltpu.VMEM((2, d, ncol), BF16), pltpu.VMEM((n_row, TMF, d), BF16),
                            pltpu.SemaphoreType.DMA((n, 2 * (N_CHIP - 1))), pltpu.SemaphoreType.DMA((n, 2 * (N_CHIP - 1))),
                            pltpu.SemaphoreType.DMA((n,)), pltpu.SemaphoreType.DMA]),
        out_shape=[jax.ShapeDtypeStruct((t, N_CHIP * ncol), F32), jax.ShapeDtypeStruct((t, 3 * d), BF16),
                   jax.ShapeDtypeStruct((d, t), BF16)]
        + [jax.ShapeDtypeStruct((2 * N_CHIP,) + hv.shape[1:], BF16) for hv in halves],
        compiler_params=_params(("arbitrary", "arbitrary")),
    )(order, qrow, qcol, x2d, g_in, *halves)
    return outs[0], outs[1], outs[2], outs[3:]


def _branch_a_fwd(a_pre, g_v, wm, b_t):
    t = a_pre.shape[0]
    d = g_v.shape[1]
    d3 = 3 * d
    ng, chunk, _ = wm.shape
    cw = d // ng

    per_step = CHUNKS_PER_STEP if t % (CHUNKS_PER_STEP * chunk) == 0 else 1

    def body(a_ref, gv_ref, wm_ref, bt_ref, ya_ref):
        for n in range(per_step):
            rows = slice(n * chunk, (n + 1) * chunk)
            ua = _gelu(a_ref[rows, 0:d])
            vg = _gelu(a_ref[rows, d:2 * d])
            za = a_ref[rows, 2 * d:3 * d]
            rv = lax.rsqrt(jnp.mean(vg * vg, axis=-1, keepdims=True) + EPS)
            va = (vg * rv * gv_ref[...]).astype(BF16)
            gate = ua * (za * _sigmoid(za))
            for g in range(ng):
                sl = slice(g * cw, (g + 1) * cw)
                mixed = jnp.dot(wm_ref[g], va[:, sl], preferred_element_type=F32) + bt_ref[:, g:g + 1]
                ya_ref[rows, sl] = (gate[:, sl] * mixed).astype(BF16)

    tile = per_step * chunk
    return pl.pallas_call(
        body, name="branch_a_fwd",
        grid=(t // tile,),
        in_specs=[pl.BlockSpec((tile, d3), lambda i: (i, 0)), _const((1, d)), _const(wm.shape), _const(b_t.shape)],
        out_specs=pl.BlockSpec((tile, d), lambda i: (i, 0)),
        out_shape=jax.ShapeDtypeStruct((t, d), BF16),
        compiler_params=_params(("arbitrary",)),
    )(a_pre, g_v, wm, b_t)


def _branch_a_bwd(a_pre, dya, g_v, wm, wm_t, b_t):
    t = a_pre.shape[0]
    d = g_v.shape[1]
    d3 = 3 * d
    ng, chunk, _ = wm.shape
    cw = d // ng
    nsteps = t // chunk

    def body(a_ref, dya_ref, gv_ref, wm_ref, wmt_ref, bt_ref, da_ref, gws_ref, gbt_ref, gnv_ref, db_acc):
        i = pl.program_id(0)

        @pl.when(i == 0)
        def _():
            gws_ref[...] = jnp.zeros_like(gws_ref)
            gnv_ref[...] = jnp.zeros_like(gnv_ref)
            db_acc[...] = jnp.zeros_like(db_acc)

        ua, dgelu_u = _gelu_and_grad(a_ref[:, 0:d])
        vg, dgelu_v = _gelu_and_grad(a_ref[:, d:2 * d])
        za = a_ref[:, 2 * d:3 * d]
        sig = _sigmoid(za)
        sz = za * sig
        dsz = sig * (1.0 + za * (1.0 - sig))
        rv = lax.rsqrt(jnp.mean(vg * vg, axis=-1, keepdims=True) + EPS)
        nv = vg * rv
        gv = gv_ref[...]
        va = (nv * gv).astype(BF16)
        dya = dya_ref[...]
        dmix = dya * ua * sz
        db_acc[...] += dmix
        dmix_b = dmix.astype(BF16)
        t_gate = dya * sz
        t_z = dya * ua * dsz
        dva_parts = []
        for g in range(ng):
            sl = slice(g * cw, (g + 1) * cw)
            mixed = jnp.dot(wm_ref[g], va[:, sl], preferred_element_type=F32) + bt_ref[:, g:g + 1]
            da_ref[:, sl] = (t_gate[:, sl] * mixed * dgelu_u[:, sl]).astype(BF16)
            da_ref[:, 2 * d + g * cw:2 * d + (g + 1) * cw] = (t_z[:, sl] * mixed).astype(BF16)
            gws_ref[g] += lax.dot_general(dmix_b[:, sl], va[:, sl], NT, preferred_element_type=F32)
            dva_parts.append(jnp.dot(wmt_ref[g], dmix_b[:, sl], preferred_element_type=F32))
        dva = jnp.concatenate(dva_parts, axis=1)
        gnv_ref[...] += jnp.sum(dva * nv, axis=0, keepdims=True)
        dnv = dva * gv
        dvg = rv * (dnv - nv * jnp.mean(dnv * nv, axis=-1, keepdims=True))
        da_ref[:, d:2 * d] = (dvg * dgelu_v).astype(BF16)

        @pl.when(i == nsteps - 1)
        def _():
            acc = db_acc[...]
            for g in range(ng):
                gbt_ref[:, g:g + 1] = jnp.sum(acc[:, g * cw:(g + 1) * cw], axis=1, keepdims=True)

    return pl.pallas_call(
        body, name="branch_a_bwd",
        grid=(nsteps,),
        in_specs=[pl.BlockSpec((chunk, d3), lambda i: (i, 0)), pl.BlockSpec((chunk, d), lambda i: (i, 0)),
                  _const((1, d)), _const(wm.shape), _const(wm_t.shape), _const(b_t.shape)],
        out_specs=[pl.BlockSpec((chunk, d3), lambda i: (i, 0)), _const(wm.shape), _const(b_t.shape), _const((1, d))],
        out_shape=[jax.ShapeDtypeStruct((t, d3), BF16), jax.ShapeDtypeStruct(wm.shape, F32),
                   jax.ShapeDtypeStruct(b_t.shape, F32), jax.ShapeDtypeStruct((1, d), F32)],
        scratch_shapes=[pltpu.VMEM((chunk, d), F32)],
        compiler_params=_params(("arbitrary",)),
    )(a_pre, dya, g_v, wm, wm_t, b_t)


def _tri(n, rows_gt_cols):
    r = lax.broadcasted_iota(jnp.int32, (n, n), 0)
    c = lax.broadcasted_iota(jnp.int32, (n, n), 1)
    return (r > c) if rows_gt_cols else (r < c)


def _twice(tri):
    t = tri.astype(BF16)
    return jnp.concatenate([t, t], axis=0)


def _cumsum_mm(a, tri2):
    hi, lo = _split_bf16(a)
    return jnp.dot(jnp.concatenate([hi, lo], axis=1), tri2, preferred_element_type=F32)


LOG2E = 1.4426950408889634
_SIGN = 0x80000000


def _sb_block(q, k, scale, upper2, causal):
    z2 = lax.dot_general(q, k, NT, preferred_element_type=F32) * (scale * LOG2E)
    neg_abs = lax.bitcast_convert_type(lax.bitcast_convert_type(z2, jnp.uint32) | jnp.uint32(_SIGN), F32)
    l2 = jnp.log(1.0 + jnp.exp2(neg_abs)) * LOG2E
    log_beta = jnp.minimum(z2, 0.0) - l2
    lom = log_beta - z2
    if causal is not None:
        lom = jnp.where(causal, lom, 0.0)
    sx = _cumsum_mm(lom, upper2)
    return log_beta, sx, sx[:, 0:1] + lom[:, 0:1]


DEAD_LOG2 = -150.0


def _max_carry(carries):
    return jnp.max(functools.reduce(jnp.maximum, carries))


ZB_GROUP, GA_GROUP, GB_GROUP = 6, 7, 8


def _attn_specs(d, seq, nq, heads_per_step):
    hp_w = heads_per_step * (d // HEADS)
    n_hp = d // hp_w
    row_blk = lambda group: pl.BlockSpec((ATT_T, hp_w), lambda b, h, i: (b * nq + i, group * n_hp + h))
    seq_blk = lambda group: pl.BlockSpec((seq, hp_w), lambda b, h, i: (b, group * n_hp + h))
    return row_blk, seq_blk, n_hp


def _attn_fwd(qkv, proj, bsz, seq):
    t, d3 = qkv.shape
    d = d3 // 3
    hd = d // HEADS
    nq = seq // ATT_T
    scale = hd ** -0.5
    n_heads = ATT_HP
    row_blk, seq_blk, n_hp = _attn_specs(d, seq, nq, n_heads)

    def body(q_ref, k_ref, v_ref, zb_ref, o_ref, yb_ref):
        i = pl.program_id(2)
        causal = _tri(ATT_T, True)
        upper2 = _twice(causal)

        def step(kb, state, mask):
            rows = pl.ds(pl.multiple_of(kb * ATT_T, ATT_T), ATT_T)
            heads = [slice(h * hd, (h + 1) * hd) for h in range(n_heads)]
            scores = [_sb_block(q_ref[:, cols], k_ref[rows, cols], scale, upper2, mask) for cols in heads]
            new = []
            for cols, (carry, acc), (log_beta, sx, total) in zip(heads, state, scores):
                a = jnp.exp2(log_beta + sx + carry)
                if mask is not None:
                    a = jnp.where(mask, a, 0.0)
                acc = acc + jnp.dot(a.astype(BF16), v_ref[rows, cols], preferred_element_type=F32)
                new.append((carry + total, acc))
            return tuple(new)

        init = tuple((jnp.zeros((ATT_T, 1), F32), jnp.zeros((ATT_T, hd), F32)) for _ in range(n_heads))
        state = step(i, init, causal)
        def more(c):
            new = step(c[0], c[1], None)
            return c[0] - 1, new, _max_carry([s[0] for s in new])

        _, state, _ = lax.while_loop(lambda c: (c[0] >= 0) & (c[2] > DEAD_LOG2), more,
                                     (i - 1, state, _max_carry([s[0] for s in state])))
        for h in range(n_heads):
            cols = slice(h * hd, (h + 1) * hd)
            acc = state[h][1]
            zb = zb_ref[:, cols]
            o_ref[:, cols] = acc
            yb_ref[:, cols] = (acc * (zb * _sigmoid(zb))).astype(BF16)

    return pl.pallas_call(
        body, name="attn_fwd",
        grid=(bsz, n_hp, nq),
        in_specs=[row_blk(0), seq_blk(1), seq_blk(2), row_blk(ZB_GROUP)],
        out_specs=[row_blk(0), row_blk(0)],
        out_shape=[jax.ShapeDtypeStruct((t, d), F32), jax.ShapeDtypeStruct((t, d), BF16)],
        compiler_params=_params(("arbitrary", "arbitrary", "arbitrary")),
    )(qkv, qkv, qkv, proj)


def _attn_bwd(qkv, proj, o, dyb, bsz, seq):
    t, d3 = qkv.shape
    d = d3 // 3
    hd = d // HEADS
    nq = seq // ATT_T
    scale = hd ** -0.5
    row_blk, seq_blk, n_hp = _attn_specs(d, seq, nq, ATT_HP)

    def body(q_ref, k_ref, v_ref, zb_ref, o_ref, dyb_ref, dq_ref, dk_ref, dv_ref, dzb_ref,
             g_s, beta_s, dkt_acc, dvt_acc):
        i = pl.program_id(2)

        @pl.when(i == 0)
        def _():
            dkt_acc[...] = jnp.zeros_like(dkt_acc)
            dvt_acc[...] = jnp.zeros_like(dvt_acc)

        causal = _tri(ATT_T, True)
        upper2 = _twice(causal)
        lower2 = _twice(~causal)
        zb = zb_ref[...]
        sig = _sigmoid(zb)
        dyb_t = dyb_ref[...]
        do_f = dyb_t * (zb * sig)
        do = do_f.astype(BF16)
        do_t = do_f.T.astype(BF16)
        q_t = q_ref[...].astype(F32).T.astype(BF16)
        dzb_ref[...] = (dyb_t * o_ref[...] * (sig * (1.0 + zb * (1.0 - sig)))).astype(BF16)

        def sweep(kb, carries, mask):
            rows = pl.ds(pl.multiple_of(kb * ATT_T, ATT_T), ATT_T)
            heads = [slice(h * hd, (h + 1) * hd) for h in range(ATT_HP)]
            scores = [_sb_block(q_ref[:, cols], k_ref[rows, cols], scale, upper2, mask) for cols in heads]
            das = [lax.dot_general(do[:, cols], v_ref[rows, cols], NT, preferred_element_type=F32) for cols in heads]
            new = []
            for h, (cols, carry, (log_beta, sx, total), da) in enumerate(zip(heads, carries, scores, das)):
                a = jnp.exp2(log_beta + sx + carry)
                beta = jnp.exp2(log_beta)
                if mask is not None:
                    a = jnp.where(mask, a, 0.0)
                    beta = jnp.where(mask, beta, 0.0)
                g_s[h, kb] = a * da
                beta_s[h, kb] = beta
                dvt_acc[kb, cols, :] += jnp.dot(do_t[cols, :], a.astype(BF16), preferred_element_type=F32)
                new.append(carry + total)
            return tuple(new)

        carries = sweep(i, tuple(jnp.zeros((ATT_T, 1), F32) for _ in range(ATT_HP)), causal)

        def more(c):
            new = sweep(c[0], c[1], None)
            return c[0] - 1, new, _max_carry(new)

        last, _, _ = lax.while_loop(lambda c: (c[0] >= 0) & (c[2] > DEAD_LOG2), more, (i - 1, carries, _max_carry(carries)))
        first_kb = last + 1

        def back(kb, state):
            rows = pl.ds(pl.multiple_of(kb * ATT_T, ATT_T), ATT_T)
            heads = [slice(h * hd, (h + 1) * hd) for h in range(ATT_HP)]
            sums = [_cumsum_mm(g_s[h, kb], lower2) for h in range(ATT_HP)]
            new = []
            for h, (cols, (p_carry, dq), px) in enumerate(zip(heads, state, sums)):
                dz = ((g_s[h, kb] - (p_carry + px) * beta_s[h, kb]) * scale).astype(BF16)
                dq = dq + jnp.dot(dz, k_ref[rows, cols], preferred_element_type=F32)
                dkt_acc[kb, cols, :] += jnp.dot(q_t[cols, :], dz, preferred_element_type=F32)
                new.append((p_carry + px[:, ATT_T - 1:ATT_T], dq))
            return tuple(new)

        init = tuple((jnp.zeros((ATT_T, 1), F32), jnp.zeros((ATT_T, hd), F32)) for _ in range(ATT_HP))
        state = lax.fori_loop(first_kb, i + 1, back, init)
        for h in range(ATT_HP):
            dq_ref[:, h * hd:(h + 1) * hd] = state[h][1].astype(BF16)

        @pl.when(i == nq - 1)
        def _():
            for kb in range(nq):
                dk_ref[kb * ATT_T:(kb + 1) * ATT_T, :] = dkt_acc[kb].T.astype(BF16)
                dv_ref[kb * ATT_T:(kb + 1) * ATT_T, :] = dvt_acc[kb].T.astype(BF16)

    out = jax.ShapeDtypeStruct((t, d), BF16)
    hp_w = ATT_HP * hd
    return pl.pallas_call(
        body, name="attn_bwd",
        grid=(bsz, n_hp, nq),
        in_specs=[row_blk(0), seq_blk(1), seq_blk(2), row_blk(ZB_GROUP), row_blk(0), row_blk(0)],
        out_specs=[row_blk(0), seq_blk(0), seq_blk(0), row_blk(0)],
        out_shape=[out, out, out, out],
        scratch_shapes=[pltpu.VMEM((ATT_HP, nq, ATT_T, ATT_T), F32), pltpu.VMEM((ATT_HP, nq, ATT_T, ATT_T), F32),
                        pltpu.VMEM((nq, hp_w, ATT_T), F32), pltpu.VMEM((nq, hp_w, ATT_T), F32)],
        compiler_params=_params(("arbitrary", "arbitrary", "arbitrary")),
    )(qkv, qkv, qkv, proj, o, dyb)


def _out_proj(ya, yb, g_pre, x2d, tgt, w_og, w_osb, w_out, g_f):
    t, d = x2d.shape

    def body(ya_ref, yb_ref, ga_ref, gb_ref, x_ref, tgt_ref, wog_ref, wosb_ref, wout_ref, gf_ref,
             dya_ref, dyb_ref, dg_ref, dx2_ref, loss_ref, gnf_ref, gwog_ref, gwosb_ref, gwout_ref):
        @pl.when(pl.program_id(0) == 0)
        def _():
            loss_ref[...] = jnp.zeros_like(loss_ref)
            gnf_ref[...] = jnp.zeros_like(gnf_ref)
            gwog_ref[...] = jnp.zeros_like(gwog_ref)
            gwosb_ref[...] = jnp.zeros_like(gwosb_ref)
            gwout_ref[...] = jnp.zeros_like(gwout_ref)

        ya = ya_ref[...]
        yb = yb_ref[...]
        pa = jnp.dot(ya, wog_ref[...], preferred_element_type=F32)
        pb = jnp.dot(yb, wosb_ref[...], preferred_element_type=F32)
        sga = _sigmoid(ga_ref[...])
        sgb = _sigmoid(gb_ref[...])
        merged = (sga * pa + sgb * pb).astype(BF16)
        x2 = x_ref[...] + jnp.dot(merged, wout_ref[...], preferred_element_type=F32)
        r2 = lax.rsqrt(jnp.mean(x2 * x2, axis=-1, keepdims=True) + EPS)
        n2 = x2 * r2
        gf = gf_ref[...]
        err = n2 * gf - tgt_ref[...]
        loss_ref[...] += 0.5 * jnp.sum(jnp.sum(err * err, axis=-1, keepdims=True), axis=0, keepdims=True) / d
        dy = err * (1.0 / d)
        gnf_ref[...] += jnp.sum(dy * n2, axis=0, keepdims=True)
        dn = dy * gf
        dx2 = r2 * (dn - n2 * jnp.mean(dn * n2, axis=-1, keepdims=True))
        dx2_ref[...] = dx2
        dx2_b = dx2.astype(BF16)
        dmerged = lax.dot_general(dx2_b, wout_ref[...], NT, preferred_element_type=F32)
        gwout_ref[...] += lax.dot_general(merged, dx2_b, TN, preferred_element_type=F32)
        dg_ref[:, 0:d] = (dmerged * pa * (sga * (1.0 - sga))).astype(BF16)
        dg_ref[:, d:2 * d] = (dmerged * pb * (sgb * (1.0 - sgb))).astype(BF16)
        dpa = (dmerged * sga).astype(BF16)
        dpb = (dmerged * sgb).astype(BF16)
        dya_ref[...] = lax.dot_general(dpa, wog_ref[...], NT, preferred_element_type=F32)
        dyb_ref[...] = lax.dot_general(dpb, wosb_ref[...], NT, preferred_element_type=F32)
        gwog_ref[...] += lax.dot_general(ya, dpa, TN, preferred_element_type=F32)
        gwosb_ref[...] += lax.dot_general(yb, dpb, TN, preferred_element_type=F32)

    row = lambda i: (i, 0)
    return pl.pallas_call(
        body, name="out_proj",
        grid=(t // TM,),
        in_specs=[pl.BlockSpec((TM, d), row), pl.BlockSpec((TM, d), row),
                  pl.BlockSpec((TM, d), lambda i: (i, GA_GROUP)), pl.BlockSpec((TM, d), lambda i: (i, GB_GROUP)),
                  pl.BlockSpec((TM, d), row), pl.BlockSpec((TM, d), row),
                  _resident((d, d)), _resident((d, d)), _resident((d, d)), _const((1, d))],
        out_specs=[pl.BlockSpec((TM, d), row), pl.BlockSpec((TM, d), row), pl.BlockSpec((TM, 2 * d), row),
                   pl.BlockSpec((TM, d), row), _const((1, 1)), _const((1, d)),
                   _const((d, d)), _const((d, d)), _const((d, d))],
        out_shape=[jax.ShapeDtypeStruct((t, d), F32), jax.ShapeDtypeStruct((t, d), F32),
                   jax.ShapeDtypeStruct((t, 2 * d), BF16), jax.ShapeDtypeStruct((t, d), F32),
                   jax.ShapeDtypeStruct((1, 1), F32), jax.ShapeDtypeStruct((1, d), F32),
                   jax.ShapeDtypeStruct((d, d), F32), jax.ShapeDtypeStruct((d, d), F32),
                   jax.ShapeDtypeStruct((d, d), F32)],
        compiler_params=_params(("arbitrary",)),
    )(ya, yb, g_pre, g_pre, x2d, tgt, w_og, w_osb, w_out, g_f)


def _dproj_pieces(d):
    return [(0, 0, 3), (1, 3, 1), (2, 4, 1), (3, 5, 1), (4, 6, 1), (5, 7, 2)]


def _in_proj_bwd_x(pieces, wg, x2d, g_in, dx2, gw16):
    t, d = x2d.shape
    ncol = wg.shape[2]
    segs = _segments(d, ncol)
    layout = _dproj_pieces(d)
    nsteps = t // TM

    def body(da_ref, dq_ref, dk_ref, dv_ref, dzb_ref, dg_ref, w_ref, x_ref, g_ref, dx2_ref, gw16_ref,
             gx_ref, gn_ref, recv_ref, send_sems, recv_sems):
        x_pos, y_pos, c_pos, chips = _place()

        def share(k, chunk):
            px, py = chips[k]
            return pltpu.make_async_remote_copy(
                src_ref=gw16_ref.at[:, chunk * ncol:(chunk + 1) * ncol], dst_ref=recv_ref.at[k],
                send_sem=send_sems.at[k], recv_sem=recv_sems.at[k], device_id=(px, py, c_pos), device_id_type=MESH)

        @pl.when(pl.program_id(0) == 0)
        def _():
            gn_ref[...] = jnp.zeros_like(gn_ref)
            for k, (px, py) in enumerate(chips):
                for chunk in range(N_CHIP):
                    @pl.when(2 * px + py == chunk)
                    def _(k=k, chunk=chunk):
                        share(k, chunk).start()

        @pl.when(pl.program_id(0) == nsteps - 1)
        def _():
            for k in range(N_CHIP - 1):
                share(k, 0).wait()

        refs = (da_ref, dq_ref, dk_ref, dv_ref, dzb_ref, dg_ref)
        dh = jnp.zeros((TM, d), F32)
        for chip, c0, grp, s0, width in segs:
            piece, first, _ = next(p for p in layout if p[1] <= grp < p[1] + p[2])
            off = (grp - first) * d + s0
            dh = dh + lax.dot_general(refs[piece][:, off:off + width], w_ref[chip, :, c0:c0 + width], NT,
                                      preferred_element_type=F32)
        x = x_ref[...]
        r = lax.rsqrt(jnp.mean(x * x, axis=-1, keepdims=True) + EPS)
        n = x * r
        gn_ref[...] += jnp.sum(dh * n, axis=0, keepdims=True)
        dn = dh * g_ref[...]
        gx_ref[...] = dx2_ref[...] + r * (dn - n * jnp.mean(dn * n, axis=-1, keepdims=True))

    row = lambda i: (i, 0)
    return pl.pallas_call(
        body, name="in_proj_bwd_x",
        grid=(t // TM,),
        in_specs=[pl.BlockSpec((TM, p.shape[1]), row) for p in pieces]
        + [_resident(wg.shape), pl.BlockSpec((TM, d), row), _const((1, d)), pl.BlockSpec((TM, d), row), ANY],
        out_specs=[pl.BlockSpec((TM, d), row), _const((1, d)), ANY],
        out_shape=[jax.ShapeDtypeStruct((t, d), F32), jax.ShapeDtypeStruct((1, d), F32),
                   jax.ShapeDtypeStruct((N_CHIP - 1, gw16.shape[0], ncol), BF16)],
        scratch_shapes=[pltpu.SemaphoreType.DMA((N_CHIP - 1,)), pltpu.SemaphoreType.DMA((N_CHIP - 1,))],
        compiler_params=_params(("arbitrary",)),
    )(*pieces, wg, x2d, g_in, dx2, gw16)


def _in_proj_bwd_w(h_t, pieces, mats16, pack):
    d, t = h_t.shape
    nk = t // TKW
    half = d // 2
    layout = _dproj_pieces(d)
    n_mats = len(mats16)
    n_dev = 2 * N_CHIP
    flips = [(dx, dy, dc) for dx in (0, 1) for dy in (0, 1) for dc in (0, 1)][1:]

    def body(ht_ref, da_ref, dq_ref, dk_ref, dv_ref, dzb_ref, dg_ref, *rest):
        mat_refs, pack_ref = rest[:n_mats], rest[n_mats]
        gw_ref, sib_ref = rest[n_mats + 1:n_mats + 3]
        recv_refs, slots_ref = rest[n_mats + 3:2 * n_mats + 3], rest[2 * n_mats + 3]
        acc, stage, mat_send, mat_recv, pack_send, pack_recv, own_sem, stage_send, sib_recv = rest[2 * n_mats + 4:]
        s = pl.program_id(0)
        i = pl.program_id(1)
        x_pos, y_pos, c_pos, chips = _place()
        me = 4 * x_pos + 2 * y_pos + c_pos

        def to_sibling(slot, group):
            return pltpu.make_async_remote_copy(
                src_ref=stage.at[slot], dst_ref=sib_ref.at[:, group * d:(group + 1) * d],
                send_sem=stage_send.at[slot], recv_sem=sib_recv, device_id=(x_pos, y_pos, 1 - c_pos), device_id_type=MESH)

        def exchanges():
            cps = []
            for k, (px, py) in enumerate(chips):
                for a in range(n_mats):
                    cps.append(pltpu.make_async_remote_copy(
                        src_ref=mat_refs[a].at[2 * px + py], dst_ref=recv_refs[a].at[k],
                        send_sem=mat_send.at[a, k], recv_sem=mat_recv.at[a, k],
                        device_id=(px, py, c_pos), device_id_type=MESH))
            for k, (dx, dy, dc) in enumerate(flips):
                peer = (1 - x_pos if dx else x_pos, 1 - y_pos if dy else y_pos, 1 - c_pos if dc else c_pos)
                cps.append(pltpu.make_async_remote_copy(
                    src_ref=pack_ref, dst_ref=slots_ref.at[me], send_sem=pack_send.at[k], recv_sem=pack_recv.at[k],
                    device_id=peer, device_id_type=MESH))
            return cps, pltpu.make_async_copy(pack_ref, slots_ref.at[me], own_sem)

        @pl.when((s == 0) & (i == 0))
        def _():
            cps, own = exchanges()
            own.start()
            for cp in cps:
                cp.start()

        @pl.when(i == 0)
        def _():
            acc[...] = jnp.zeros_like(acc)

        refs = (da_ref, dq_ref, dk_ref, dv_ref, dzb_ref, dg_ref)
        for piece, first, count in layout:
            @pl.when((s >= first) & (s < first + count))
            def _(piece=piece):
                acc[...] += jnp.dot(ht_ref[...], refs[piece][...], preferred_element_type=F32)

        @pl.when(i == nk - 1)
        def _():
            gw_ref[...] = acc[...]
            for slot in range(2):
                @pl.when(s % 2 == slot)
                def _(slot=slot):
                    @pl.when(s >= 2)
                    def _():
                        to_sibling(slot, 0).wait_send()
                    for other in range(2):
                        @pl.when(c_pos == 1 - other)
                        def _(other=other):
                            stage[slot] = acc[other * half:(other + 1) * half, :].astype(BF16)
                    for group in range(N_SPLIT):
                        @pl.when(s == group)
                        def _(group=group):
                            to_sibling(slot, group).start()

        @pl.when((s == N_SPLIT - 1) & (i == nk - 1))
        def _():
            for slot in range(2):
                to_sibling(slot, 0).wait_send()
            pltpu.make_async_remote_copy(
                src_ref=sib_ref, dst_ref=sib_ref, send_sem=stage_send.at[0], recv_sem=sib_recv,
                device_id=(x_pos, y_pos, c_pos), device_id_type=MESH).wait_recv()
            cps, own = exchanges()
            own.wait()
            for cp in cps:
                cp.wait()

    def piece_spec(p, first, count):
        def index(s, i):
            mine = (s >= first) & (s < first + count)
            return jnp.where(mine, i, 0), jnp.where(mine, s - first, 0)
        return pl.BlockSpec((TKW, d), index)

    col_blk = pl.BlockSpec((d, d), lambda s, i: (0, s))
    outs = pl.pallas_call(
        body, name="in_proj_bwd_w",
        grid=(N_SPLIT, nk),
        in_specs=[pl.BlockSpec((d, TKW), lambda s, i: (0, i))] + [piece_spec(*p) for p in layout] + [ANY] * (n_mats + 1),
        out_specs=[col_blk, ANY] + [ANY] * (n_mats + 1),
        out_shape=[jax.ShapeDtypeStruct((d, N_SPLIT * d), F32), jax.ShapeDtypeStruct((half, N_SPLIT * d), BF16)]
        + [jax.ShapeDtypeStruct((N_CHIP - 1,) + m.shape[1:], BF16) for m in mats16]
        + [jax.ShapeDtypeStruct((n_dev,) + pack.shape, F32)],
        scratch_shapes=[pltpu.VMEM((d, d), F32), pltpu.VMEM((2, half, d), BF16),
                        pltpu.SemaphoreType.DMA((n_mats, N_CHIP - 1)), pltpu.SemaphoreType.DMA((n_mats, N_CHIP - 1)),
                        pltpu.SemaphoreType.DMA((n_dev - 1,)), pltpu.SemaphoreType.DMA((n_dev - 1,)),
                        pltpu.SemaphoreType.DMA, pltpu.SemaphoreType.DMA((2,)), pltpu.SemaphoreType.DMA],
        compiler_params=_params(("arbitrary", "arbitrary")),
    )(h_t, *pieces, *mats16, pack)
    return outs[0], outs[1], outs[2:2 + n_mats], outs[2 + n_mats]


def _local_step(proj, qkv, x2d, tgt2d, bsz, seq, norm_v, w_s, b_s, w_og, w_osb, w_out, norm_final):
    d = x2d.shape[1]
    chunk = w_s.shape[-1]
    causal = jnp.tril(jnp.ones((chunk, chunk), dtype=bool))
    wm = jnp.where(causal[None], w_s, 0.0).astype(BF16)
    wm_t = jnp.swapaxes(wm, 1, 2)
    b_t = b_s.T

    ya = _branch_a_fwd(proj, norm_v, wm, b_t)
    o, yb = _attn_fwd(qkv, proj, bsz, seq)
    dya, dyb, dg, dx2, loss, g_nf, g_wog, g_wosb, g_wout = _out_proj(
        ya, yb, proj, x2d, tgt2d, w_og, w_osb, w_out, norm_final.reshape(1, d))
    dq, dk, dv, dzb = _attn_bwd(qkv, proj, o, dyb, bsz, seq)
    d_a, g_ws, g_bt, g_nv = _branch_a_bwd(proj, dya, norm_v, wm, wm_t, b_t)
    g_ws = jnp.where(causal[None], g_ws, 0.0)
    return loss, (d_a, dq, dk, dv, dzb, dg), dx2, g_nv, g_ws, g_bt.T, g_wog, g_wosb, g_wout, g_nf


def _row_tile(rows):
    return next(r for r in (128, 64, 32, 16, 8) if rows % r == 0)


def _cast_bf16(arrs):
    n = len(arrs)

    def body(*refs):
        for a_ref, o_ref in zip(refs[:n], refs[n:]):
            o_ref[...] = a_ref[...].astype(BF16)

    specs = [pl.BlockSpec((a.shape[0] // CAST_STEPS, a.shape[1]), lambda i: (i, 0)) for a in arrs]
    return pl.pallas_call(
        body, name="cast_bf16", grid=(CAST_STEPS,),
        in_specs=specs, out_specs=specs,
        out_shape=[jax.ShapeDtypeStruct(a.shape, BF16) for a in arrs],
        compiler_params=_params(("arbitrary",)),
    )(*arrs)


def _chip_half(full, from_sibling, core, chip, tile):
    half, n = from_sibling.shape

    def body(where_ref, own_ref, sib_ref, o32_ref, o16_ref):
        total = own_ref[...] + sib_ref[...].astype(F32)
        o16_ref[...] = total.astype(BF16)

        @pl.when(pl.program_id(0) == where_ref[1])
        def _():
            o32_ref[...] = total

    blk = pl.BlockSpec((half, tile), lambda j, where_ref: (0, j))
    return pl.pallas_call(
        body, name="chip_half",
        grid_spec=pltpu.PrefetchScalarGridSpec(
            num_scalar_prefetch=1, grid=(n // tile,),
            in_specs=[pl.BlockSpec((half, tile), lambda j, where_ref: (where_ref[0], j)), blk],
            out_specs=[pl.BlockSpec((half, tile), lambda j, where_ref: (0, 0)), blk]),
        out_shape=[jax.ShapeDtypeStruct((half, tile), F32), jax.ShapeDtypeStruct((half, n), BF16)],
        compiler_params=_params(("arbitrary",)),
    )(jnp.stack([core, chip]).astype(jnp.int32), full, from_sibling)


def _add_received(full, recv, chip, by_cols):
    _, rows, cols = recv.shape
    tr = _row_tile(rows)
    nb = rows // tr

    def body(chip_ref, own_ref, recv_ref, o_ref):
        s = own_ref[...]
        for k in range(N_CHIP - 1):
            s = s + recv_ref[k].astype(F32)
        o_ref[...] = s

    own_map = (lambda i, chip_ref: (i, chip_ref[0])) if by_cols else (lambda i, chip_ref: (chip_ref[0] * nb + i, 0))
    return pl.pallas_call(
        body, name="add_received",
        grid_spec=pltpu.PrefetchScalarGridSpec(
            num_scalar_prefetch=1, grid=(nb,),
            in_specs=[pl.BlockSpec((tr, cols), own_map),
                      pl.BlockSpec((N_CHIP - 1, tr, cols), lambda i, chip_ref: (0, i, 0))],
            out_specs=pl.BlockSpec((tr, cols), lambda i, chip_ref: (i, 0))),
        out_shape=jax.ShapeDtypeStruct((rows, cols), F32),
        compiler_params=_params(("arbitrary",)),
    )(chip.reshape(1).astype(jnp.int32), full, recv)


def _adamw_math(w, m, v, g):
    new_m = ADAM_B1 * m + (1.0 - ADAM_B1) * g
    new_v = ADAM_B2 * v + (1.0 - ADAM_B2) * (g * g)
    m_hat = new_m / (1.0 - ADAM_B1 ** ADAM_STEP)
    v_hat = new_v / (1.0 - ADAM_B2 ** ADAM_STEP)
    return -ADAM_LR * (m_hat / (jnp.sqrt(v_hat) + ADAM_EPS) + ADAM_WD * w), new_m, new_v


def _adamw(w, m, v, g_parts):
    rows, cols = w.shape
    tr = _row_tile(rows)
    n_parts = len(g_parts)

    def body(*refs):
        w_ref, m_ref, v_ref = refs[:3]
        part_refs = refs[3:3 + n_parts]
        g_ref, d_ref, nm_ref, nv_ref = refs[3 + n_parts:]
        g = part_refs[0][...]
        for p in part_refs[1:]:
            g = g + p[...]
        g_ref[...] = g
        d_ref[...], nm_ref[...], nv_ref[...] = _adamw_math(w_ref[...], m_ref[...], v_ref[...], g)

    spec = pl.BlockSpec((tr, cols), lambda i: (i, 0))
    out = jax.ShapeDtypeStruct(w.shape, F32)
    return pl.pallas_call(
        body, name="adamw", grid=(rows // tr,),
        in_specs=[spec] * (3 + n_parts), out_specs=[spec] * 4, out_shape=[out] * 4,
        compiler_params=_params(("arbitrary",)),
    )(w, m, v, *g_parts)


def _adamw_halves(w, m, v, mine, theirs, core):
    rows, cols = w.shape
    tr = _row_tile(rows // 2)
    per_half = rows // 2 // tr

    def body(core_ref, w_ref, m_ref, v_ref, mine_ref, theirs_ref, g_ref, d_ref, nm_ref, nv_ref):
        is_mine = pl.program_id(0) // per_half == core_ref[0]
        for part, cond in ((mine_ref, is_mine), (theirs_ref, jnp.logical_not(is_mine))):
            @pl.when(cond)
            def _(part=part):
                g = part[...]
                g_ref[...] = g
                d_ref[...], nm_ref[...], nv_ref[...] = _adamw_math(w_ref[...], m_ref[...], v_ref[...], g)

    spec = pl.BlockSpec((tr, cols), lambda i, core_ref: (i, 0))

    def half_spec(own):
        def index(i, core_ref):
            in_core_half = i // per_half == core_ref[0]
            here = in_core_half if own else jnp.logical_not(in_core_half)
            return jnp.where(here, i % per_half, 0), 0
        return pl.BlockSpec((tr, cols), index)

    out = jax.ShapeDtypeStruct(w.shape, F32)
    return pl.pallas_call(
        body, name="adamw_halves",
        grid_spec=pltpu.PrefetchScalarGridSpec(
            num_scalar_prefetch=1, grid=(rows // tr,),
            in_specs=[spec] * 3 + [half_spec(True), half_spec(False)], out_specs=[spec] * 4),
        out_shape=[out] * 4,
        compiler_params=_params(("arbitrary",)),
    )(core.reshape(1).astype(jnp.int32), w, m, v, mine, theirs)


def _adamw_small(w, m, v, slots_head, slots_tail):
    n_dev, p0, _ = slots_head.shape

    def body(w_ref, m_ref, v_ref, head_ref, tail_ref, g_ref, d_ref, nm_ref, nv_ref):
        for ref, rows in ((head_ref, slice(0, p0)), (tail_ref, slice(p0, w.shape[0]))):
            g = ref[0]
            for i in range(1, n_dev):
                g = g + ref[i]
            g_ref[rows, :] = g
            d_ref[rows, :], nm_ref[rows, :], nv_ref[rows, :] = _adamw_math(w_ref[rows, :], m_ref[rows, :], v_ref[rows, :], g)

    vmem = pl.BlockSpec(memory_space=pltpu.VMEM)
    out = jax.ShapeDtypeStruct(w.shape, F32)
    return pl.pallas_call(
        body, name="adamw_small", in_specs=[vmem] * 5, out_specs=[vmem] * 4, out_shape=[out] * 4,
        compiler_params=pltpu.CompilerParams(vmem_limit_bytes=VMEM_LIMIT),
    )(w, m, v, slots_head, slots_tail)


ANY = pl.BlockSpec(memory_space=pl.ANY)


def _place():
    x, y, c = lax.axis_index("x"), lax.axis_index("y"), lax.axis_index("c")
    other_chips = [(1 - x, y), (x, 1 - y), (1 - x, 1 - y)]
    return x, y, c, other_chips


def _swap_and_gather(arrs, pack):
    n = len(arrs)
    n_dev = 2 * N_CHIP
    flips = [(dx, dy, dc) for dx in (0, 1) for dy in (0, 1) for dc in (0, 1)][1:]

    def body(*refs):
        ins, pack_ref = refs[:n], refs[n]
        outs, slots_ref = refs[n + 1:2 * n + 1], refs[2 * n + 1]
        send_sems, recv_sems, pack_send, pack_recv, own_sem = refs[2 * n + 2:]
        x, y, c, _ = _place()
        me = 4 * x + 2 * y + c
        own = pltpu.make_async_copy(pack_ref, slots_ref.at[me], own_sem)
        own.start()
        copies = []
        for k, (dx, dy, dc) in enumerate(flips):
            peer = (1 - x if dx else x, 1 - y if dy else y, 1 - c if dc else c)
            copies.append(pltpu.make_async_remote_copy(
                src_ref=pack_ref, dst_ref=slots_ref.at[me], send_sem=pack_send.at[k], recv_sem=pack_recv.at[k],
                device_id=peer, device_id_type=MESH))
        copies += [pltpu.make_async_remote_copy(
            src_ref=ins[a], dst_ref=outs[a], send_sem=send_sems.at[a], recv_sem=recv_sems.at[a],
            device_id=(x, y, 1 - c), device_id_type=MESH) for a in range(n)]
        for cp in copies:
            cp.start()
        for cp in copies:
            cp.wait()
        own.wait()

    return pl.pallas_call(
        body, name="swap_and_gather",
        in_specs=[ANY] * (n + 1), out_specs=[ANY] * (n + 1),
        out_shape=[jax.ShapeDtypeStruct(a.shape, a.dtype) for a in arrs] + [jax.ShapeDtypeStruct((n_dev,) + pack.shape, F32)],
        scratch_shapes=[pltpu.SemaphoreType.DMA((n,)), pltpu.SemaphoreType.DMA((n,)),
                        pltpu.SemaphoreType.DMA((n_dev - 1,)), pltpu.SemaphoreType.DMA((n_dev - 1,)),
                        pltpu.SemaphoreType.DMA],
    )(*arrs, pack)


SLAB = 8


def _slab(vec, d):
    return jnp.pad(vec.reshape(1, d), ((0, SLAB - 1), (0, 0)))


def _pack_tail(vec_nv, b_s, vec_nf, w_s, scalar=None):
    d = vec_nv.shape[-1]
    extra = jnp.zeros((1, d), F32) if scalar is None else jnp.pad(scalar, ((0, 0), (0, d - 1)))
    return jnp.concatenate([_slab(v, d) for v in (vec_nv, b_s, vec_nf, extra)] + [w_s.reshape(-1, d)], axis=0)


def _pack_small(vec_nin, vec_nv, b_s, vec_nf, w_s):
    return jnp.concatenate([_slab(vec_nin, vec_nin.shape[-1]), _pack_tail(vec_nv, b_s, vec_nf, w_s)], axis=0)


def _unpack_small(pack, w_s_shape, b_s_shape):
    return (pack[0:1], pack[SLAB:SLAB + 1], pack[2 * SLAB].reshape(b_s_shape), pack[3 * SLAB],
            pack[5 * SLAB:].reshape(w_s_shape), pack[4 * SLAB, 0])


def kernel(x, norm_in, w_in, norm_v, w_s, b_s, w_o_gmlp, w_o_sb, w_out, norm_final, loss_target, m_norm_in, m_w_in, m_norm_v, m_w_s, m_b_s, m_w_o_gmlp, m_w_o_sb, m_w_out, m_norm_final, v_norm_in, v_w_in, v_norm_v, v_w_s, v_b_s, v_w_o_gmlp, v_w_o_sb, v_w_out, v_norm_final):
    d = x.shape[-1]
    ncol = w_in.shape[-1]
    nrow = w_o_gmlp.shape[-2]
    chip = 2 * lax.axis_index("x") + lax.axis_index("y")

    bsz, seq, _ = x.shape
    x2d = x.reshape(bsz * seq, d)
    shards = [w_in[0], w_o_gmlp[0], w_o_sb[0], w_out[0]]
    halves = [s16.reshape(2, s16.shape[0] // 2, s16.shape[1]) for s16 in _cast_bf16(shards)]
    proj, qkv, h_t, (wg, w_og, w_osb, w_o) = _in_proj_fwd(x2d, norm_in, halves)
    wg = wg.reshape(N_CHIP, d, ncol)

    loss, pieces, dx2, g_nv, g_ws, g_bs, g_wog, g_wosb, g_wout, g_nf = _local_step(
        proj, qkv, x2d, loss_target.reshape(bsz * seq, d), bsz, seq, norm_v, w_s[0], b_s[0],
        w_og.reshape(d, d), w_osb.reshape(d, d), w_o.reshape(d, d), norm_final)

    mats = [g_wog, g_wosb, g_wout]
    mats16 = [g16.reshape(N_CHIP, nrow, d) for g16 in _cast_bf16(mats)]
    g_win, from_sibling, recv_mats, slots_tail = _in_proj_bwd_w(
        h_t, pieces, mats16, _pack_tail(g_nv, g_bs, g_nf, g_ws, loss))
    core = lax.axis_index("c")
    half_own, half_win16 = _chip_half(g_win, from_sibling, core, chip, ncol)
    grad_x, g_nin, recv_win = _in_proj_bwd_x(pieces, wg, x2d, norm_in, dx2, half_win16)
    grad_x = grad_x.reshape(bsz, seq, d)

    sums = [_add_received(half_own, recv_win, jnp.zeros((), jnp.int32), True)] + [
        _add_received(g, r, chip, False) for g, r in zip(mats, recv_mats)]
    *sibling_sums, slots_head = _swap_and_gather(sums, _slab(g_nin, d))
    stats = [_adamw_halves(w_in[0], m_w_in[0], v_w_in[0], sums[0], sibling_sums[0], core)]
    for w, m, v, mine, theirs in zip(shards[1:], [m_w_o_gmlp[0], m_w_o_sb[0], m_w_out[0]],
                                     [v_w_o_gmlp[0], v_w_o_sb[0], v_w_out[0]], sums[1:], sibling_sums[1:]):
        stats.append(_adamw(w, m, v, [mine, theirs]))
    (gw_in, dw_in, nm_in, nv_in), (gw_og, dw_og, nm_og, nv_og), (gw_osb, dw_osb, nm_osb, nv_osb), \
        (gw_out, dw_out, nm_out, nv_out) = stats

    gs, ds, ms, vs = _adamw_small(
        _pack_small(norm_in, norm_v, b_s[0], norm_final, w_s[0]),
        _pack_small(m_norm_in, m_norm_v, m_b_s[0], m_norm_final, m_w_s[0]),
        _pack_small(v_norm_in, v_norm_v, v_b_s[0], v_norm_final, v_w_s[0]), slots_head, slots_tail)

    def small(pack):
        nin, nv, bs, nf, ws, _ = _unpack_small(pack, w_s.shape, b_s.shape)
        return nin, nv, ws, bs, nf

    loss = _unpack_small(gs, w_s.shape, b_s.shape)[-1]
    out = []
    for small_pack, win, wog, wosb, wout in ((gs, gw_in, gw_og, gw_osb, gw_out), (ds, dw_in, dw_og, dw_osb, dw_out),
                                             (ms, nm_in, nm_og, nm_osb, nm_out), (vs, nv_in, nv_og, nv_osb, nv_out)):
        nin, nv, ws, bs, nf = small(small_pack)
        out += [nin, win[None], nv, ws, bs, wog[None], wosb[None], wout[None], nf]
    return (loss, grad_x, *out)
```

```python
import functools
import math

import jax
import jax.numpy as jnp
from jax import lax
from jax.experimental import pallas as pl
from jax.experimental.pallas import tpu as pltpu

F32 = jnp.float32
BF16 = jnp.bfloat16
EPS = 1e-6
HEADS = 8
N_SPLIT = 9
N_CHIP = 4
MESH = pl.DeviceIdType.MESH

ADAM_LR = 0.001
ADAM_B1 = 0.9
ADAM_B2 = 0.999
ADAM_EPS = 1e-08
ADAM_WD = 0.01
ADAM_STEP = 10

VMEM_LIMIT = 56 * 2 ** 20
TM = 256
TMF = 512
ATT_T = 256
ATT_HP = 4
TKW = 1024
CHUNKS_PER_STEP = 4
CAST_STEPS = 8

NT = (((1,), (1,)), ((), ()))
TN = (((0,), (0,)), ((), ()))


def _params(sem):
    return pltpu.CompilerParams(dimension_semantics=sem, vmem_limit_bytes=VMEM_LIMIT)


def _resident(shape):
    nd = len(shape)
    return pl.BlockSpec(shape, lambda *_: (0,) * nd, pipeline_mode=pl.Buffered(1))


def _const(shape):
    nd = len(shape)
    return pl.BlockSpec(shape, lambda *_: (0,) * nd)


def _segments(d, ncol):
    segs = []
    edges = sorted({j * ncol for j in range(N_CHIP + 1)} | {s * d for s in range(N_SPLIT + 1)})
    for lo, hi in zip(edges[:-1], edges[1:]):
        segs.append((lo // ncol, lo % ncol, lo // d, lo % d, hi - lo))
    return segs


def _sigmoid(x):
    return 0.5 * jnp.tanh(0.5 * x) + 0.5


_GELU_C = math.sqrt(2.0 / math.pi)


_GELU_CA = _GELU_C * 0.044715


def _gelu(x):
    return x * (0.5 * jnp.tanh(x * (_GELU_C + _GELU_CA * (x * x))) + 0.5)


def _gelu_and_grad(x):
    x2 = x * x
    u = 0.5 * jnp.tanh(x * (_GELU_C + _GELU_CA * x2)) + 0.5
    slope = (1.0 - u) * (x * (_GELU_C + (3.0 * _GELU_CA) * x2))
    return x * u, u * (2.0 * slope + 1.0)


def _split_bf16(a):
    hi = a.astype(BF16)
    lo = (a - hi.astype(F32)).astype(BF16)
    return hi, lo


def _in_proj_fwd(x2d, g_in, halves):
    t, d = x2d.shape
    n = len(halves)
    ncol = halves[0].shape[2]
    n_row = t // TMF
    last = n_row - 1
    assert halves[0].shape[1] * 2 == d
    qkv_parts = {j: (max(j * ncol, 3 * d) - j * ncol, max(j * ncol, 3 * d) - 3 * d)
                 for j in range(N_CHIP) if min((j + 1) * ncol, 6 * d) > max(j * ncol, 3 * d)}
    qkv_w = 3 * d // len(qkv_parts)
    assert all(min((j + 1) * ncol, 6 * d) - max(j * ncol, 3 * d) == qkv_w and q0 % qkv_w == 0
               for j, (_, q0) in qkv_parts.items())

    def body(order_ref, qrow_ref, qcol_ref, x_ref, g_ref, *rest):
        ins = rest[:n]
        proj_ref, qkv_ref, ht_ref = rest[n:n + 3]
        outs = rest[n + 3:2 * n + 3]
        wbuf, h_all, send_sems, recv_sems, local_sems, load_sem = rest[2 * n + 3:]
        phase = pl.program_id(0)
        i = pl.program_id(1)
        x_pos, y_pos, c_pos, chips = _place()
        sibling = (x_pos, y_pos, 1 - c_pos)
        me = (x_pos, y_pos, c_pos)
        my_chip = 2 * x_pos + y_pos

        def copy(a, k, block, to, src=None):
            return pltpu.make_async_remote_copy(
                src_ref=outs[a].at[block] if src is None else src, dst_ref=outs[a].at[block],
                send_sem=send_sems.at[a, k], recv_sem=recv_sems.at[a, k], device_id=to, device_id_type=MESH)

        def local(a):
            return pltpu.make_async_copy(ins[a], outs[a].at[pl.ds(2 * my_chip, 2)], local_sems.at[a])

        def load(src, first, slot):
            for half in range(2):
                cp = pltpu.make_async_copy(src.at[first + half], wbuf.at[slot, pl.ds(half * (d // 2), d // 2)], load_sem)
                cp.start()
                cp.wait()

        def send_mine(k):
            px, py = chips[k]
            for a in range(n):
                copy(a, k, 2 * my_chip + c_pos, (px, py, c_pos), src=ins[a].at[c_pos]).start()

        @pl.when((phase == 0) & (i == 0))
        def _():
            for a in range(n):
                local(a).start()
            send_mine(0)
            send_mine(1)
            load(ins[0], 0, 0)

        for k, (px, py) in enumerate(chips):
            @pl.when((phase == k + 1) & (i == 0))
            def _(k=k, px=px, py=py):
                theirs = 2 * (2 * px + py)
                for a in range(n):
                    copy(a, k, theirs + c_pos, me).wait_recv()
                    copy(a, 3 + k, theirs + c_pos, sibling).start()
                if k == 0:
                    send_mine(2)
                for a in range(n):
                    copy(a, 3 + k, theirs + 1 - c_pos, me).wait_recv()
                load(outs[0], theirs, (k + 1) % 2)

        @pl.when(phase == 0)
        def _():
            x = x_ref[...]
            r = lax.rsqrt(jnp.mean(x * x, axis=-1, keepdims=True) + EPS)
            hf = x * r * g_ref[...]
            h_all[i] = hf.astype(BF16)
            ht_ref[...] = hf.T.astype(BF16)

        for slot in range(2):
            @pl.when(phase % 2 == slot)
            def _(slot=slot):
                proj_ref[...] = jnp.dot(h_all[i], wbuf[slot], preferred_element_type=F32)

        for chunk, (c0, _) in qkv_parts.items():
            @pl.when(order_ref[phase] == chunk)
            def _(c0=c0):
                qkv_ref[...] = proj_ref[:, c0:c0 + qkv_w].astype(BF16)

        @pl.when((phase == N_CHIP - 1) & (i == n_row - 1))
        def _():
            for a in range(n):
                for k in range(2 * (N_CHIP - 1)):
                    copy(a, k, 0, me).wait_send()
                local(a).wait()

    x_pos, y_pos = lax.axis_index("x"), lax.axis_index("y")
    order = jnp.stack([2 * x_pos + y_pos, 2 * (1 - x_pos) + y_pos, 2 * x_pos + 1 - y_pos,
                       2 * (1 - x_pos) + 1 - y_pos]).astype(jnp.int32)
    holds = [functools.reduce(jnp.logical_or, [order[p] == j for j in qkv_parts]) for p in range(N_CHIP)]
    col = [sum(jnp.where(order[p] == j, q0 // qkv_w, 0) for j, (_, q0) in qkv_parts.items()) for p in range(N_CHIP)]
    cur = col[-1]
    for p in reversed(range(N_CHIP - 1)):
        cur = jnp.where(holds[p], col[p], cur)
    seen = jnp.bool_(False)
    qrow, qcol = [], []
    for p in range(N_CHIP):
        cur = jnp.where(holds[p], col[p], cur)
        qrow.append(jnp.where(holds[p], -1, jnp.where(seen, last, 0)))
        qcol.append(cur)
        seen = seen | holds[p]
    qrow = jnp.stack(qrow).astype(jnp.int32)
    qcol = jnp.stack(qcol).astype(jnp.int32)

    outs = pl.pallas_call(
        body, name="in_proj_fwd",
        grid_spec=pltpu.PrefetchScalarGridSpec(
            num_scalar_prefetch=3, grid=(N_CHIP, n_row),
            in_specs=[pl.BlockSpec((TMF, d), lambda p, i, order, qrow, qcol: (jnp.where(p == 0, i, last), 0)),
                      pl.BlockSpec((1, d), lambda p, i, order, qrow, qcol: (0, 0))] + [ANY] * n,
            out_specs=[pl.BlockSpec((TMF, ncol), lambda p, i, order, qrow, qcol: (i, order[p])),
                       pl.BlockSpec((TMF, qkv_w),
                                    lambda p, i, order, qrow, qcol: (jnp.where(qrow[p] < 0, i, qrow[p]), qcol[p])),
                       pl.BlockSpec((d, TMF), lambda p, i, order, qrow, qcol: (0, jnp.where(p == 0, i, last)))]
            + [ANY] * n,
            scratch_shapes=[pltpu.VMEM((2, d, ncol), BF16), pltpu.VMEM((n_row, TMF, d), BF16),
                            pltpu.SemaphoreType.DMA((n, 2 * (N_CHIP - 1))), pltpu.SemaphoreType.DMA((n, 2 * (N_CHIP - 1))),
                            pltpu.SemaphoreType.DMA((n,)), pltpu.SemaphoreType.DMA]),
        out_shape=[jax.ShapeDtypeStruct((t, N_CHIP * ncol), F32), jax.ShapeDtypeStruct((t, 3 * d), BF16),
                   jax.ShapeDtypeStruct((d, t), BF16)]
        + [jax.ShapeDtypeStruct((2 * N_CHIP,) + hv.shape[1:], BF16) for hv in halves],
        compiler_params=_params(("arbitrary", "arbitrary")),
    )(order, qrow, qcol, x2d, g_in, *halves)
    return outs[0], outs[1], outs[2], outs[3:]


def _branch_a_fwd(a_pre, g_v, wm, b_t):
    t = a_pre.shape[0]
    d = g_v.shape[1]
    d3 = 3 * d
    ng, chunk, _ = wm.shape
    cw = d // ng

    per_step = CHUNKS_PER_STEP if t % (CHUNKS_PER_STEP * chunk) == 0 else 1

    def body(a_ref, gv_ref, wm_ref, bt_ref, ya_ref):
        for n in range(per_step):
            rows = slice(n * chunk, (n + 1) * chunk)
            ua = _gelu(a_ref[rows, 0:d])
            vg = _gelu(a_ref[rows, d:2 * d])
            za = a_ref[rows, 2 * d:3 * d]
            rv = lax.rsqrt(jnp.mean(vg * vg, axis=-1, keepdims=True) + EPS)
            va = (vg * rv * gv_ref[...]).astype(BF16)
            gate = ua * (za * _sigmoid(za))
            for g in range(ng):
                sl = slice(g * cw, (g + 1) * cw)
                mixed = jnp.dot(wm_ref[g], va[:, sl], preferred_element_type=F32) + bt_ref[:, g:g + 1]
                ya_ref[rows, sl] = (gate[:, sl] * mixed).astype(BF16)

    tile = per_step * chunk
    return pl.pallas_call(
        body, name="branch_a_fwd",
        grid=(t // tile,),
        in_specs=[pl.BlockSpec((tile, d3), lambda i: (i, 0)), _const((1, d)), _const(wm.shape), _const(b_t.shape)],
        out_specs=pl.BlockSpec((tile, d), lambda i: (i, 0)),
        out_shape=jax.ShapeDtypeStruct((t, d), BF16),
        compiler_params=_params(("arbitrary",)),
    )(a_pre, g_v, wm, b_t)


def _branch_a_bwd(a_pre, dya, g_v, wm, wm_t, b_t):
    t = a_pre.shape[0]
    d = g_v.shape[1]
    d3 = 3 * d
    ng, chunk, _ = wm.shape
    cw = d // ng
    nsteps = t // chunk

    def body(a_ref, dya_ref, gv_ref, wm_ref, wmt_ref, bt_ref, da_ref, gws_ref, gbt_ref, gnv_ref, db_acc):
        i = pl.program_id(0)

        @pl.when(i == 0)
        def _():
            gws_ref[...] = jnp.zeros_like(gws_ref)
            gnv_ref[...] = jnp.zeros_like(gnv_ref)
            db_acc[...] = jnp.zeros_like(db_acc)

        ua, dgelu_u = _gelu_and_grad(a_ref[:, 0:d])
        vg, dgelu_v = _gelu_and_grad(a_ref[:, d:2 * d])
        za = a_ref[:, 2 * d:3 * d]
        sig = _sigmoid(za)
        sz = za * sig
        dsz = sig * (1.0 + za * (1.0 - sig))
        rv = lax.rsqrt(jnp.mean(vg * vg, axis=-1, keepdims=True) + EPS)
        nv = vg * rv
        gv = gv_ref[...]
        va = (nv * gv).astype(BF16)
        dya = dya_ref[...]
        dmix = dya * ua * sz
        db_acc[...] += dmix
        dmix_b = dmix.astype(BF16)
        t_gate = dya * sz
        t_z = dya * ua * dsz
        dva_parts = []
        for g in range(ng):
            sl = slice(g * cw, (g + 1) * cw)
            mixed = jnp.dot(wm_ref[g], va[:, sl], preferred_element_type=F32) + bt_ref[:, g:g + 1]
            da_ref[:, sl] = (t_gate[:, sl] * mixed * dgelu_u[:, sl]).astype(BF16)
            da_ref[:, 2 * d + g * cw:2 * d + (g + 1) * cw] = (t_z[:, sl] * mixed).astype(BF16)
            gws_ref[g] += lax.dot_general(dmix_b[:, sl], va[:, sl], NT, preferred_element_type=F32)
            dva_parts.append(jnp.dot(wmt_ref[g], dmix_b[:, sl], preferred_element_type=F32))
        dva = jnp.concatenate(dva_parts, axis=1)
        gnv_ref[...] += jnp.sum(dva * nv, axis=0, keepdims=True)
        dnv = dva * gv
        dvg = rv * (dnv - nv * jnp.mean(dnv * nv, axis=-1, keepdims=True))
        da_ref[:, d:2 * d] = (dvg * dgelu_v).astype(BF16)

        @pl.when(i == nsteps - 1)
        def _():
            acc = db_acc[...]
            for g in range(ng):
                gbt_ref[:, g:g + 1] = jnp.sum(acc[:, g * cw:(g + 1) * cw], axis=1, keepdims=True)

    return pl.pallas_call(
        body, name="branch_a_bwd",
        grid=(nsteps,),
        in_specs=[pl.BlockSpec((chunk, d3), lambda i: (i, 0)), pl.BlockSpec((chunk, d), lambda i: (i, 0)),
                  _const((1, d)), _const(wm.shape), _const(wm_t.shape), _const(b_t.shape)],
        out_specs=[pl.BlockSpec((chunk, d3), lambda i: (i, 0)), _const(wm.shape), _const(b_t.shape), _const((1, d))],
        out_shape=[jax.ShapeDtypeStruct((t, d3), BF16), jax.ShapeDtypeStruct(wm.shape, F32),
                   jax.ShapeDtypeStruct(b_t.shape, F32), jax.ShapeDtypeStruct((1, d), F32)],
        scratch_shapes=[pltpu.VMEM((chunk, d), F32)],
        compiler_params=_params(("arbitrary",)),
    )(a_pre, dya, g_v, wm, wm_t, b_t)


def _tri(n, rows_gt_cols):
    r = lax.broadcasted_iota(jnp.int32, (n, n), 0)
    c = lax.broadcasted_iota(jnp.int32, (n, n), 1)
    return (r > c) if rows_gt_cols else (r < c)


def _twice(tri):
    t = tri.astype(BF16)
    return jnp.concatenate([t, t], axis=0)


def _cumsum_mm(a, tri2):
    hi, lo = _split_bf16(a)
    return jnp.dot(jnp.concatenate([hi, lo], axis=1), tri2, preferred_element_type=F32)


LOG2E = 1.4426950408889634
_SIGN = 0x80000000


def _sb_block(q, k, scale, upper2, causal):
    z2 = lax.dot_general(q, k, NT, preferred_element_type=F32) * (scale * LOG2E)
    neg_abs = lax.bitcast_convert_type(lax.bitcast_convert_type(z2, jnp.uint32) | jnp.uint32(_SIGN), F32)
    l2 = jnp.log(1.0 + jnp.exp2(neg_abs)) * LOG2E
    log_beta = jnp.minimum(z2, 0.0) - l2
    lom = log_beta - z2
    if causal is not None:
        lom = jnp.where(causal, lom, 0.0)
    sx = _cumsum_mm(lom, upper2)
    return log_beta, sx, sx[:, 0:1] + lom[:, 0:1]


DEAD_LOG2 = -150.0


def _max_carry(carries):
    return jnp.max(functools.reduce(jnp.maximum, carries))


ZB_GROUP, GA_GROUP, GB_GROUP = 6, 7, 8


def _attn_specs(d, seq, nq, heads_per_step):
    hp_w = heads_per_step * (d // HEADS)
    n_hp = d // hp_w
    row_blk = lambda group: pl.BlockSpec((ATT_T, hp_w), lambda b, h, i: (b * nq + i, group * n_hp + h))
    seq_blk = lambda group: pl.BlockSpec((seq, hp_w), lambda b, h, i: (b, group * n_hp + h))
    return row_blk, seq_blk, n_hp


def _attn_fwd(qkv, proj, bsz, seq):
    t, d3 = qkv.shape
    d = d3 // 3
    hd = d // HEADS
    nq = seq // ATT_T
    scale = hd ** -0.5
    n_heads = ATT_HP
    row_blk, seq_blk, n_hp = _attn_specs(d, seq, nq, n_heads)

    def body(q_ref, k_ref, v_ref, zb_ref, o_ref, yb_ref):
        i = pl.program_id(2)
        causal = _tri(ATT_T, True)
        upper2 = _twice(causal)

        def step(kb, state, mask):
            rows = pl.ds(pl.multiple_of(kb * ATT_T, ATT_T), ATT_T)
            heads = [slice(h * hd, (h + 1) * hd) for h in range(n_heads)]
            scores = [_sb_block(q_ref[:, cols], k_ref[rows, cols], scale, upper2, mask) for cols in heads]
            new = []
            for cols, (carry, acc), (log_beta, sx, total) in zip(heads, state, scores):
                a = jnp.exp2(log_beta + sx + carry)
                if mask is not None:
                    a = jnp.where(mask, a, 0.0)
                acc = acc + jnp.dot(a.astype(BF16), v_ref[rows, cols], preferred_element_type=F32)
                new.append((carry + total, acc))
            return tuple(new)

        init = tuple((jnp.zeros((ATT_T, 1), F32), jnp.zeros((ATT_T, hd), F32)) for _ in range(n_heads))
        state = step(i, init, causal)
        def more(c):
            new = step(c[0], c[1], None)
            return c[0] - 1, new, _max_carry([s[0] for s in new])

        _, state, _ = lax.while_loop(lambda c: (c[0] >= 0) & (c[2] > DEAD_LOG2), more,
                                     (i - 1, state, _max_carry([s[0] for s in state])))
        for h in range(n_heads):
            cols = slice(h * hd, (h + 1) * hd)
            acc = state[h][1]
            zb = zb_ref[:, cols]
            o_ref[:, cols] = acc
            yb_ref[:, cols] = (acc * (zb * _sigmoid(zb))).astype(BF16)

    return pl.pallas_call(
        body, name="attn_fwd",
        grid=(bsz, n_hp, nq),
        in_specs=[row_blk(0), seq_blk(1), seq_blk(2), row_blk(ZB_GROUP)],
        out_specs=[row_blk(0), row_blk(0)],
        out_shape=[jax.ShapeDtypeStruct((t, d), F32), jax.ShapeDtypeStruct((t, d), BF16)],
        compiler_params=_params(("arbitrary", "arbitrary", "arbitrary")),
    )(qkv, qkv, qkv, proj)


def _attn_bwd(qkv, proj, o, dyb, bsz, seq):
    t, d3 = qkv.shape
    d = d3 // 3
    hd = d // HEADS
    nq = seq // ATT_T
    scale = hd ** -0.5
    row_blk, seq_blk, n_hp = _attn_specs(d, seq, nq, ATT_HP)

    def body(q_ref, k_ref, v_ref, zb_ref, o_ref, dyb_ref, dq_ref, dk_ref, dv_ref, dzb_ref,
             g_s, beta_s, dkt_acc, dvt_acc):
        i = pl.program_id(2)

        @pl.when(i == 0)
        def _():
            dkt_acc[...] = jnp.zeros_like(dkt_acc)
            dvt_acc[...] = jnp.zeros_like(dvt_acc)

        causal = _tri(ATT_T, True)
        upper2 = _twice(causal)
        lower2 = _twice(~causal)
        zb = zb_ref[...]
        sig = _sigmoid(zb)
        dyb_t = dyb_ref[...]
        do_f = dyb_t * (zb * sig)
        do = do_f.astype(BF16)
        do_t = do_f.T.astype(BF16)
        q_t = q_ref[...].astype(F32).T.astype(BF16)
        dzb_ref[...] = (dyb_t * o_ref[...] * (sig * (1.0 + zb * (1.0 - sig)))).astype(BF16)

        def sweep(kb, carries, mask):
            rows = pl.ds(pl.multiple_of(kb * ATT_T, ATT_T), ATT_T)
            heads = [slice(h * hd, (h + 1) * hd) for h in range(ATT_HP)]
            scores = [_sb_block(q_ref[:, cols], k_ref[rows, cols], scale, upper2, mask) for cols in heads]
            das = [lax.dot_general(do[:, cols], v_ref[rows, cols], NT, preferred_element_type=F32) for cols in heads]
            new = []
            for h, (cols, carry, (log_beta, sx, total), da) in enumerate(zip(heads, carries, scores, das)):
                a = jnp.exp2(log_beta + sx + carry)
                beta = jnp.exp2(log_beta)
                if mask is not None:
                    a = jnp.where(mask, a, 0.0)
                    beta = jnp.where(mask, beta, 0.0)
                g_s[h, kb] = a * da
                beta_s[h, kb] = beta
                dvt_acc[kb, cols, :] += jnp.dot(do_t[cols, :], a.astype(BF16), preferred_element_type=F32)
                new.append(carry + total)
            return tuple(new)

        carries = sweep(i, tuple(jnp.zeros((ATT_T, 1), F32) for _ in range(ATT_HP)), causal)

        def more(c):
            new = sweep(c[0], c[1], None)
            return c[0] - 1, new, _max_carry(new)

        last, _, _ = lax.while_loop(lambda c: (c[0] >= 0) & (c[2] > DEAD_LOG2), more, (i - 1, carries, _max_carry(carries)))
        first_kb = last + 1

        def back(kb, state):
            rows = pl.ds(pl.multiple_of(kb * ATT_T, ATT_T), ATT_T)
            heads = [slice(h * hd, (h + 1) * hd) for h in range(ATT_HP)]
            sums = [_cumsum_mm(g_s[h, kb], lower2) for h in range(ATT_HP)]
            new = []
            for h, (cols, (p_carry, dq), px) in enumerate(zip(heads, state, sums)):
                dz = ((g_s[h, kb] - (p_carry + px) * beta_s[h, kb]) * scale).astype(BF16)
                dq = dq + jnp.dot(dz, k_ref[rows, cols], preferred_element_type=F32)
                dkt_acc[kb, cols, :] += jnp.dot(q_t[cols, :], dz, preferred_element_type=F32)
                new.append((p_carry + px[:, ATT_T - 1:ATT_T], dq))
            return tuple(new)

        init = tuple((jnp.zeros((ATT_T, 1), F32), jnp.zeros((ATT_T, hd), F32)) for _ in range(ATT_HP))
        state = lax.fori_loop(first_kb, i + 1, back, init)
        for h in range(ATT_HP):
            dq_ref[:, h * hd:(h + 1) * hd] = state[h][1].astype(BF16)

        @pl.when(i == nq - 1)
        def _():
            for kb in range(nq):
                dk_ref[kb * ATT_T:(kb + 1) * ATT_T, :] = dkt_acc[kb].T.astype(BF16)
                dv_ref[kb * ATT_T:(kb + 1) * ATT_T, :] = dvt_acc[kb].T.astype(BF16)

    out = jax.ShapeDtypeStruct((t, d), BF16)
    hp_w = ATT_HP * hd
    return pl.pallas_call(
        body, name="attn_bwd",
        grid=(bsz, n_hp, nq),
        in_specs=[row_blk(0), seq_blk(1), seq_blk(2), row_blk(ZB_GROUP), row_blk(0), row_blk(0)],
        out_specs=[row_blk(0), seq_blk(0), seq_blk(0), row_blk(0)],
        out_shape=[out, out, out, out],
        scratch_shapes=[pltpu.VMEM((ATT_HP, nq, ATT_T, ATT_T), F32), pltpu.VMEM((ATT_HP, nq, ATT_T, ATT_T), F32),
                        pltpu.VMEM((nq, hp_w, ATT_T), F32), pltpu.VMEM((nq, hp_w, ATT_T), F32)],
        compiler_params=_params(("arbitrary", "arbitrary", "arbitrary")),
    )(qkv, qkv, qkv, proj, o, dyb)


def _out_proj(ya, yb, g_pre, x2d, tgt, w_og, w_osb, w_out, g_f):
    t, d = x2d.shape

    def body(ya_ref, yb_ref, ga_ref, gb_ref, x_ref, tgt_ref, wog_ref, wosb_ref, wout_ref, gf_ref,
             dya_ref, dyb_ref, dg_ref, dx2_ref, loss_ref, gnf_ref, gwog_ref, gwosb_ref, gwout_ref):
        @pl.when(pl.program_id(0) == 0)
        def _():
            loss_ref[...] = jnp.zeros_like(loss_ref)
            gnf_ref[...] = jnp.zeros_like(gnf_ref)
            gwog_ref[...] = jnp.zeros_like(gwog_ref)
            gwosb_ref[...] = jnp.zeros_like(gwosb_ref)
            gwout_ref[...] = jnp.zeros_like(gwout_ref)

        ya = ya_ref[...]
        yb = yb_ref[...]
        pa = jnp.dot(ya, wog_ref[...], preferred_element_type=F32)
        pb = jnp.dot(yb, wosb_ref[...], preferred_element_type=F32)
        sga = _sigmoid(ga_ref[...])
        sgb = _sigmoid(gb_ref[...])
        merged = (sga * pa + sgb * pb).astype(BF16)
        x2 = x_ref[...] + jnp.dot(merged, wout_ref[...], preferred_element_type=F32)
        r2 = lax.rsqrt(jnp.mean(x2 * x2, axis=-1, keepdims=True) + EPS)
        n2 = x2 * r2
        gf = gf_ref[...]
        err = n2 * gf - tgt_ref[...]
        loss_ref[...] += 0.5 * jnp.sum(jnp.sum(err * err, axis=-1, keepdims=True), axis=0, keepdims=True) / d
        dy = err * (1.0 / d)
        gnf_ref[...] += jnp.sum(dy * n2, axis=0, keepdims=True)
        dn = dy * gf
        dx2 = r2 * (dn - n2 * jnp.mean(dn * n2, axis=-1, keepdims=True))
        dx2_ref[...] = dx2
        dx2_b = dx2.astype(BF16)
        dmerged = lax.dot_general(dx2_b, wout_ref[...], NT, preferred_element_type=F32)
        gwout_ref[...] += lax.dot_general(merged, dx2_b, TN, preferred_element_type=F32)
        dg_ref[:, 0:d] = (dmerged * pa * (sga * (1.0 - sga))).astype(BF16)
        dg_ref[:, d:2 * d] = (dmerged * pb * (sgb * (1.0 - sgb))).astype(BF16)
        dpa = (dmerged * sga).astype(BF16)
        dpb = (dmerged * sgb).astype(BF16)
        dya_ref[...] = lax.dot_general(dpa, wog_ref[...], NT, preferred_element_type=F32)
        dyb_ref[...] = lax.dot_general(dpb, wosb_ref[...], NT, preferred_element_type=F32)
        gwog_ref[...] += lax.dot_general(ya, dpa, TN, preferred_element_type=F32)
        gwosb_ref[...] += lax.dot_general(yb, dpb, TN, preferred_element_type=F32)

    row = lambda i: (i, 0)
    return pl.pallas_call(
        body, name="out_proj",
        grid=(t // TM,),
        in_specs=[pl.BlockSpec((TM, d), row), pl.BlockSpec((TM, d), row),
                  pl.BlockSpec((TM, d), lambda i: (i, GA_GROUP)), pl.BlockSpec((TM, d), lambda i: (i, GB_GROUP)),
                  pl.BlockSpec((TM, d), row), pl.BlockSpec((TM, d), row),
                  _resident((d, d)), _resident((d, d)), _resident((d, d)), _const((1, d))],
        out_specs=[pl.BlockSpec((TM, d), row), pl.BlockSpec((TM, d), row), pl.BlockSpec((TM, 2 * d), row),
                   pl.BlockSpec((TM, d), row), _const((1, 1)), _const((1, d)),
                   _const((d, d)), _const((d, d)), _const((d, d))],
        out_shape=[jax.ShapeDtypeStruct((t, d), F32), jax.ShapeDtypeStruct((t, d), F32),
                   jax.ShapeDtypeStruct((t, 2 * d), BF16), jax.ShapeDtypeStruct((t, d), F32),
                   jax.ShapeDtypeStruct((1, 1), F32), jax.ShapeDtypeStruct((1, d), F32),
                   jax.ShapeDtypeStruct((d, d), F32), jax.ShapeDtypeStruct((d, d), F32),
                   jax.ShapeDtypeStruct((d, d), F32)],
        compiler_params=_params(("arbitrary",)),
    )(ya, yb, g_pre, g_pre, x2d, tgt, w_og, w_osb, w_out, g_f)


def _dproj_pieces(d):
    return [(0, 0, 3), (1, 3, 1), (2, 4, 1), (3, 5, 1), (4, 6, 1), (5, 7, 2)]


def _in_proj_bwd_x(pieces, wg, x2d, g_in, dx2, gw16):
    t, d = x2d.shape
    ncol = wg.shape[2]
    segs = _segments(d, ncol)
    layout = _dproj_pieces(d)
    nsteps = t // TM

    def body(da_ref, dq_ref, dk_ref, dv_ref, dzb_ref, dg_ref, w_ref, x_ref, g_ref, dx2_ref, gw16_ref,
             gx_ref, gn_ref, recv_ref, send_sems, recv_sems):
        x_pos, y_pos, c_pos, chips = _place()

        def share(k, chunk):
            px, py = chips[k]
            return pltpu.make_async_remote_copy(
                src_ref=gw16_ref.at[:, chunk * ncol:(chunk + 1) * ncol], dst_ref=recv_ref.at[k],
                send_sem=send_sems.at[k], recv_sem=recv_sems.at[k], device_id=(px, py, c_pos), device_id_type=MESH)

        @pl.when(pl.program_id(0) == 0)
        def _():
            gn_ref[...] = jnp.zeros_like(gn_ref)
            for k, (px, py) in enumerate(chips):
                for chunk in range(N_CHIP):
                    @pl.when(2 * px + py == chunk)
                    def _(k=k, chunk=chunk):
                        share(k, chunk).start()

        @pl.when(pl.program_id(0) == nsteps - 1)
        def _():
            for k in range(N_CHIP - 1):
                share(k, 0).wait()

        refs = (da_ref, dq_ref, dk_ref, dv_ref, dzb_ref, dg_ref)
        dh = jnp.zeros((TM, d), F32)
        for chip, c0, grp, s0, width in segs:
            piece, first, _ = next(p for p in layout if p[1] <= grp < p[1] + p[2])
            off = (grp - first) * d + s0
            dh = dh + lax.dot_general(refs[piece][:, off:off + width], w_ref[chip, :, c0:c0 + width], NT,
                                      preferred_element_type=F32)
        x = x_ref[...]
        r = lax.rsqrt(jnp.mean(x * x, axis=-1, keepdims=True) + EPS)
        n = x * r
        gn_ref[...] += jnp.sum(dh * n, axis=0, keepdims=True)
        dn = dh * g_ref[...]
        gx_ref[...] = dx2_ref[...] + r * (dn - n * jnp.mean(dn * n, axis=-1, keepdims=True))

    row = lambda i: (i, 0)
    return pl.pallas_call(
        body, name="in_proj_bwd_x",
        grid=(t // TM,),
        in_specs=[pl.BlockSpec((TM, p.shape[1]), row) for p in pieces]
        + [_resident(wg.shape), pl.BlockSpec((TM, d), row), _const((1, d)), pl.BlockSpec((TM, d), row), ANY],
        out_specs=[pl.BlockSpec((TM, d), row), _const((1, d)), ANY],
        out_shape=[jax.ShapeDtypeStruct((t, d), F32), jax.ShapeDtypeStruct((1, d), F32),
                   jax.ShapeDtypeStruct((N_CHIP - 1, gw16.shape[0], ncol), BF16)],
        scratch_shapes=[pltpu.SemaphoreType.DMA((N_CHIP - 1,)), pltpu.SemaphoreType.DMA((N_CHIP - 1,))],
        compiler_params=_params(("arbitrary",)),
    )(*pieces, wg, x2d, g_in, dx2, gw16)


def _in_proj_bwd_w(h_t, pieces, mats16, pack):
    d, t = h_t.shape
    nk = t // TKW
    half = d // 2
    layout = _dproj_pieces(d)
    n_mats = len(mats16)
    n_dev = 2 * N_CHIP
    flips = [(dx, dy, dc) for dx in (0, 1) for dy in (0, 1) for dc in (0, 1)][1:]

    def body(ht_ref, da_ref, dq_ref, dk_ref, dv_ref, dzb_ref, dg_ref, *rest):
        mat_refs, pack_ref = rest[:n_mats], rest[n_mats]
        gw_ref, sib_ref = rest[n_mats + 1:n_mats + 3]
        recv_refs, slots_ref = rest[n_mats + 3:2 * n_mats + 3], rest[2 * n_mats + 3]
        acc, stage, mat_send, mat_recv, pack_send, pack_recv, own_sem, stage_send, sib_recv = rest[2 * n_mats + 4:]
        s = pl.program_id(0)
        i = pl.program_id(1)
        x_pos, y_pos, c_pos, chips = _place()
        me = 4 * x_pos + 2 * y_pos + c_pos

        def to_sibling(slot, group):
            return pltpu.make_async_remote_copy(
                src_ref=stage.at[slot], dst_ref=sib_ref.at[:, group * d:(group + 1) * d],
                send_sem=stage_send.at[slot], recv_sem=sib_recv, device_id=(x_pos, y_pos, 1 - c_pos), device_id_type=MESH)

        def exchanges():
            cps = []
            for k, (px, py) in enumerate(chips):
                for a in range(n_mats):
                    cps.append(pltpu.make_async_remote_copy(
                        src_ref=mat_refs[a].at[2 * px + py], dst_ref=recv_refs[a].at[k],
                        send_sem=mat_send.at[a, k], recv_sem=mat_recv.at[a, k],
                        device_id=(px, py, c_pos), device_id_type=MESH))
            for k, (dx, dy, dc) in enumerate(flips):
                peer = (1 - x_pos if dx else x_pos, 1 - y_pos if dy else y_pos, 1 - c_pos if dc else c_pos)
                cps.append(pltpu.make_async_remote_copy(
                    src_ref=pack_ref, dst_ref=slots_ref.at[me], send_sem=pack_send.at[k], recv_sem=pack_recv.at[k],
                    device_id=peer, device_id_type=MESH))
            return cps, pltpu.make_async_copy(pack_ref, slots_ref.at[me], own_sem)

        @pl.when((s == 0) & (i == 0))
        def _():
            cps, own = exchanges()
            own.start()
            for cp in cps:
                cp.start()

        @pl.when(i == 0)
        def _():
            acc[...] = jnp.zeros_like(acc)

        refs = (da_ref, dq_ref, dk_ref, dv_ref, dzb_ref, dg_ref)
        for piece, first, count in layout:
            @pl.when((s >= first) & (s < first + count))
            def _(piece=piece):
                acc[...] += jnp.dot(ht_ref[...], refs[piece][...], preferred_element_type=F32)

        @pl.when(i == nk - 1)
        def _():
            gw_ref[...] = acc[...]
            for slot in range(2):
                @pl.when(s % 2 == slot)
                def _(slot=slot):
                    @pl.when(s >= 2)
                    def _():
                        to_sibling(slot, 0).wait_send()
                    for other in range(2):
                        @pl.when(c_pos == 1 - other)
                        def _(other=other):
                            stage[slot] = acc[other * half:(other + 1) * half, :].astype(BF16)
                    for group in range(N_SPLIT):
                        @pl.when(s == group)
                        def _(group=group):
                            to_sibling(slot, group).start()

        @pl.when((s == N_SPLIT - 1) & (i == nk - 1))
        def _():
            for slot in range(2):
                to_sibling(slot, 0).wait_send()
            pltpu.make_async_remote_copy(
                src_ref=sib_ref, dst_ref=sib_ref, send_sem=stage_send.at[0], recv_sem=sib_recv,
                device_id=(x_pos, y_pos, c_pos), device_id_type=MESH).wait_recv()
            cps, own = exchanges()
            own.wait()
            for cp in cps:
                cp.wait()

    def piece_spec(p, first, count):
        def index(s, i):
            mine = (s >= first) & (s < first + count)
            return jnp.where(mine, i, 0), jnp.where(mine, s - first, 0)
        return pl.BlockSpec((TKW, d), index)

    col_blk = pl.BlockSpec((d, d), lambda s, i: (0, s))
    outs = pl.pallas_call(
        body, name="in_proj_bwd_w",
        grid=(N_SPLIT, nk),
        in_specs=[pl.BlockSpec((d, TKW), lambda s, i: (0, i))] + [piece_spec(*p) for p in layout] + [ANY] * (n_mats + 1),
        out_specs=[col_blk, ANY] + [ANY] * (n_mats + 1),
        out_shape=[jax.ShapeDtypeStruct((d, N_SPLIT * d), F32), jax.ShapeDtypeStruct((half, N_SPLIT * d), BF16)]
        + [jax.ShapeDtypeStruct((N_CHIP - 1,) + m.shape[1:], BF16) for m in mats16]
        + [jax.ShapeDtypeStruct((n_dev,) + pack.shape, F32)],
        scratch_shapes=[pltpu.VMEM((d, d), F32), pltpu.VMEM((2, half, d), BF16),
                        pltpu.SemaphoreType.DMA((n_mats, N_CHIP - 1)), pltpu.SemaphoreType.DMA((n_mats, N_CHIP - 1)),
                        pltpu.SemaphoreType.DMA((n_dev - 1,)), pltpu.SemaphoreType.DMA((n_dev - 1,)),
                        pltpu.SemaphoreType.DMA, pltpu.SemaphoreType.DMA((2,)), pltpu.SemaphoreType.DMA],
        compiler_params=_params(("arbitrary", "arbitrary")),
    )(h_t, *pieces, *mats16, pack)
    return outs[0], outs[1], outs[2:2 + n_mats], outs[2 + n_mats]


def _local_step(proj, qkv, x2d, tgt2d, bsz, seq, norm_v, w_s, b_s, w_og, w_osb, w_out, norm_final):
    d = x2d.shape[1]
    chunk = w_s.shape[-1]
    causal = jnp.tril(jnp.ones((chunk, chunk), dtype=bool))
    wm = jnp.where(causal[None], w_s, 0.0).astype(BF16)
    wm_t = jnp.swapaxes(wm, 1, 2)
    b_t = b_s.T

    ya = _branch_a_fwd(proj, norm_v, wm, b_t)
    o, yb = _attn_fwd(qkv, proj, bsz, seq)
    dya, dyb, dg, dx2, loss, g_nf, g_wog, g_wosb, g_wout = _out_proj(
        ya, yb, proj, x2d, tgt2d, w_og, w_osb, w_out, norm_final.reshape(1, d))
    dq, dk, dv, dzb = _attn_bwd(qkv, proj, o, dyb, bsz, seq)
    d_a, g_ws, g_bt, g_nv = _branch_a_bwd(proj, dya, norm_v, wm, wm_t, b_t)
    g_ws = jnp.where(causal[None], g_ws, 0.0)
    return loss, (d_a, dq, dk, dv, dzb, dg), dx2, g_nv, g_ws, g_bt.T, g_wog, g_wosb, g_wout, g_nf


def _row_tile(rows):
    return next(r for r in (128, 64, 32, 16, 8) if rows % r == 0)


def _cast_bf16(arrs):
    n = len(arrs)

    def body(*refs):
        for a_ref, o_ref in zip(refs[:n], refs[n:]):
            o_ref[...] = a_ref[...].astype(BF16)

    specs = [pl.BlockSpec((a.shape[0] // CAST_STEPS, a.shape[1]), lambda i: (i, 0)) for a in arrs]
    return pl.pallas_call(
        body, name="cast_bf16", grid=(CAST_STEPS,),
        in_specs=specs, out_specs=specs,
        out_shape=[jax.ShapeDtypeStruct(a.shape, BF16) for a in arrs],
        compiler_params=_params(("arbitrary",)),
    )(*arrs)


def _chip_half(full, from_sibling, core, chip, tile):
    half, n = from_sibling.shape

    def body(where_ref, own_ref, sib_ref, o32_ref, o16_ref):
        total = own_ref[...] + sib_ref[...].astype(F32)
        o16_ref[...] = total.astype(BF16)

        @pl.when(pl.program_id(0) == where_ref[1])
        def _():
            o32_ref[...] = total

    blk = pl.BlockSpec((half, tile), lambda j, where_ref: (0, j))
    return pl.pallas_call(
        body, name="chip_half",
        grid_spec=pltpu.PrefetchScalarGridSpec(
            num_scalar_prefetch=1, grid=(n // tile,),
            in_specs=[pl.BlockSpec((half, tile), lambda j, where_ref: (where_ref[0], j)), blk],
            out_specs=[pl.BlockSpec((half, tile), lambda j, where_ref: (0, 0)), blk]),
        out_shape=[jax.ShapeDtypeStruct((half, tile), F32), jax.ShapeDtypeStruct((half, n), BF16)],
        compiler_params=_params(("arbitrary",)),
    )(jnp.stack([core, chip]).astype(jnp.int32), full, from_sibling)


def _add_received(fulls, recvs, chip, by_cols):
    n = len(fulls)
    _, rows, cols = recvs[0].shape
    tr = _row_tile(rows)
    nb = rows // tr

    def body(chip_ref, *refs):
        for own_ref, recv_ref, o_ref in zip(refs[:n], refs[n:2 * n], refs[2 * n:]):
            s = own_ref[...]
            for k in range(N_CHIP - 1):
                s = s + recv_ref[k].astype(F32)
            o_ref[...] = s

    own_map = (lambda i, chip_ref: (i, chip_ref[0])) if by_cols else (lambda i, chip_ref: (chip_ref[0] * nb + i, 0))
    return pl.pallas_call(
        body, name="add_received",
        grid_spec=pltpu.PrefetchScalarGridSpec(
            num_scalar_prefetch=1, grid=(nb,),
            in_specs=[pl.BlockSpec((tr, cols), own_map)] * n
            + [pl.BlockSpec((N_CHIP - 1, tr, cols), lambda i, chip_ref: (0, i, 0))] * n,
            out_specs=[pl.BlockSpec((tr, cols), lambda i, chip_ref: (i, 0))] * n),
        out_shape=[jax.ShapeDtypeStruct((rows, cols), F32)] * n,
        compiler_params=_params(("arbitrary",)),
    )(chip.reshape(1).astype(jnp.int32), *fulls, *recvs)


def _adamw_math(w, m, v, g):
    new_m = ADAM_B1 * m + (1.0 - ADAM_B1) * g
    new_v = ADAM_B2 * v + (1.0 - ADAM_B2) * (g * g)
    m_hat = new_m / (1.0 - ADAM_B1 ** ADAM_STEP)
    v_hat = new_v / (1.0 - ADAM_B2 ** ADAM_STEP)
    return -ADAM_LR * (m_hat / (jnp.sqrt(v_hat) + ADAM_EPS) + ADAM_WD * w), new_m, new_v


def _adamw_pairs(ws, ms, vs, mines, theirs):
    n = len(ws)
    rows, cols = ws[0].shape
    tr = _row_tile(rows)

    def body(*refs):
        ins, outs = refs[:5 * n], refs[5 * n:]
        for a in range(n):
            w_ref, m_ref, v_ref, mine_ref, theirs_ref = ins[a::n]
            g_ref, d_ref, nm_ref, nv_ref = outs[4 * a:4 * a + 4]
            g = mine_ref[...] + theirs_ref[...]
            g_ref[...] = g
            d_ref[...], nm_ref[...], nv_ref[...] = _adamw_math(w_ref[...], m_ref[...], v_ref[...], g)

    spec = pl.BlockSpec((tr, cols), lambda i: (i, 0))
    out = jax.ShapeDtypeStruct((rows, cols), F32)
    outs = pl.pallas_call(
        body, name="adamw_pairs", grid=(rows // tr,),
        in_specs=[spec] * (5 * n), out_specs=[spec] * (4 * n), out_shape=[out] * (4 * n),
        compiler_params=_params(("arbitrary",)),
    )(*ws, *ms, *vs, *mines, *theirs)
    return [outs[4 * a:4 * a + 4] for a in range(n)]


def _adamw_halves(w, m, v, mine, theirs, core):
    rows, cols = w.shape
    tr = _row_tile(rows // 2)
    per_half = rows // 2 // tr

    def body(core_ref, w_ref, m_ref, v_ref, mine_ref, theirs_ref, g_ref, d_ref, nm_ref, nv_ref):
        is_mine = pl.program_id(0) // per_half == core_ref[0]
        for part, cond in ((mine_ref, is_mine), (theirs_ref, jnp.logical_not(is_mine))):
            @pl.when(cond)
            def _(part=part):
                g = part[...]
                g_ref[...] = g
                d_ref[...], nm_ref[...], nv_ref[...] = _adamw_math(w_ref[...], m_ref[...], v_ref[...], g)

    spec = pl.BlockSpec((tr, cols), lambda i, core_ref: (i, 0))

    def half_spec(own):
        def index(i, core_ref):
            in_core_half = i // per_half == core_ref[0]
            here = in_core_half if own else jnp.logical_not(in_core_half)
            return jnp.where(here, i % per_half, 0), 0
        return pl.BlockSpec((tr, cols), index)

    out = jax.ShapeDtypeStruct(w.shape, F32)
    return pl.pallas_call(
        body, name="adamw_halves",
        grid_spec=pltpu.PrefetchScalarGridSpec(
            num_scalar_prefetch=1, grid=(rows // tr,),
            in_specs=[spec] * 3 + [half_spec(True), half_spec(False)], out_specs=[spec] * 4),
        out_shape=[out] * 4,
        compiler_params=_params(("arbitrary",)),
    )(core.reshape(1).astype(jnp.int32), w, m, v, mine, theirs)


def _adamw_small(w, m, v, slots_head, slots_tail):
    n_dev, p0, _ = slots_head.shape

    def body(w_ref, m_ref, v_ref, head_ref, tail_ref, g_ref, d_ref, nm_ref, nv_ref):
        for ref, rows in ((head_ref, slice(0, p0)), (tail_ref, slice(p0, w.shape[0]))):
            g = ref[0]
            for i in range(1, n_dev):
                g = g + ref[i]
            g_ref[rows, :] = g
            d_ref[rows, :], nm_ref[rows, :], nv_ref[rows, :] = _adamw_math(w_ref[rows, :], m_ref[rows, :], v_ref[rows, :], g)

    vmem = pl.BlockSpec(memory_space=pltpu.VMEM)
    out = jax.ShapeDtypeStruct(w.shape, F32)
    return pl.pallas_call(
        body, name="adamw_small", in_specs=[vmem] * 5, out_specs=[vmem] * 4, out_shape=[out] * 4,
        compiler_params=pltpu.CompilerParams(vmem_limit_bytes=VMEM_LIMIT),
    )(w, m, v, slots_head, slots_tail)


ANY = pl.BlockSpec(memory_space=pl.ANY)


def _place():
    x, y, c = lax.axis_index("x"), lax.axis_index("y"), lax.axis_index("c")
    other_chips = [(1 - x, y), (x, 1 - y), (1 - x, 1 - y)]
    return x, y, c, other_chips


def _swap_and_gather(arrs, pack):
    n = len(arrs)
    n_dev = 2 * N_CHIP
    flips = [(dx, dy, dc) for dx in (0, 1) for dy in (0, 1) for dc in (0, 1)][1:]

    def body(*refs):
        ins, pack_ref = refs[:n], refs[n]
        outs, slots_ref = refs[n + 1:2 * n + 1], refs[2 * n + 1]
        send_sems, recv_sems, pack_send, pack_recv, own_sem = refs[2 * n + 2:]
        x, y, c, _ = _place()
        me = 4 * x + 2 * y + c
        own = pltpu.make_async_copy(pack_ref, slots_ref.at[me], own_sem)
        own.start()
        copies = []
        for k, (dx, dy, dc) in enumerate(flips):
            peer = (1 - x if dx else x, 1 - y if dy else y, 1 - c if dc else c)
            copies.append(pltpu.make_async_remote_copy(
                src_ref=pack_ref, dst_ref=slots_ref.at[me], send_sem=pack_send.at[k], recv_sem=pack_recv.at[k],
                device_id=peer, device_id_type=MESH))
        copies += [pltpu.make_async_remote_copy(
            src_ref=ins[a], dst_ref=outs[a], send_sem=send_sems.at[a], recv_sem=recv_sems.at[a],
            device_id=(x, y, 1 - c), device_id_type=MESH) for a in range(n)]
        for cp in copies:
            cp.start()
        for cp in copies:
            cp.wait()
        own.wait()

    return pl.pallas_call(
        body, name="swap_and_gather",
        in_specs=[ANY] * (n + 1), out_specs=[ANY] * (n + 1),
        out_shape=[jax.ShapeDtypeStruct(a.shape, a.dtype) for a in arrs] + [jax.ShapeDtypeStruct((n_dev,) + pack.shape, F32)],
        scratch_shapes=[pltpu.SemaphoreType.DMA((n,)), pltpu.SemaphoreType.DMA((n,)),
                        pltpu.SemaphoreType.DMA((n_dev - 1,)), pltpu.SemaphoreType.DMA((n_dev - 1,)),
                        pltpu.SemaphoreType.DMA],
    )(*arrs, pack)


SLAB = 8


def _slab(vec, d):
    return jnp.pad(vec.reshape(1, d), ((0, SLAB - 1), (0, 0)))


def _pack_tail(vec_nv, b_s, vec_nf, w_s, scalar=None):
    d = vec_nv.shape[-1]
    extra = jnp.zeros((1, d), F32) if scalar is None else jnp.pad(scalar, ((0, 0), (0, d - 1)))
    return jnp.concatenate([_slab(v, d) for v in (vec_nv, b_s, vec_nf, extra)] + [w_s.reshape(-1, d)], axis=0)


def _pack_small(vec_nin, vec_nv, b_s, vec_nf, w_s):
    return jnp.concatenate([_slab(vec_nin, vec_nin.shape[-1]), _pack_tail(vec_nv, b_s, vec_nf, w_s)], axis=0)


def _unpack_small(pack, w_s_shape, b_s_shape):
    return (pack[0:1], pack[SLAB:SLAB + 1], pack[2 * SLAB].reshape(b_s_shape), pack[3 * SLAB],
            pack[5 * SLAB:].reshape(w_s_shape), pack[4 * SLAB, 0])


def kernel(x, norm_in, w_in, norm_v, w_s, b_s, w_o_gmlp, w_o_sb, w_out, norm_final, loss_target, m_norm_in, m_w_in, m_norm_v, m_w_s, m_b_s, m_w_o_gmlp, m_w_o_sb, m_w_out, m_norm_final, v_norm_in, v_w_in, v_norm_v, v_w_s, v_b_s, v_w_o_gmlp, v_w_o_sb, v_w_out, v_norm_final):
    d = x.shape[-1]
    ncol = w_in.shape[-1]
    nrow = w_o_gmlp.shape[-2]
    chip = 2 * lax.axis_index("x") + lax.axis_index("y")

    bsz, seq, _ = x.shape
    x2d = x.reshape(bsz * seq, d)
    shards = [w_in[0], w_o_gmlp[0], w_o_sb[0], w_out[0]]
    halves = [s16.reshape(2, s16.shape[0] // 2, s16.shape[1]) for s16 in _cast_bf16(shards)]
    proj, qkv, h_t, (wg, w_og, w_osb, w_o) = _in_proj_fwd(x2d, norm_in, halves)
    wg = wg.reshape(N_CHIP, d, ncol)

    loss, pieces, dx2, g_nv, g_ws, g_bs, g_wog, g_wosb, g_wout, g_nf = _local_step(
        proj, qkv, x2d, loss_target.reshape(bsz * seq, d), bsz, seq, norm_v, w_s[0], b_s[0],
        w_og.reshape(d, d), w_osb.reshape(d, d), w_o.reshape(d, d), norm_final)

    mats = [g_wog, g_wosb, g_wout]
    mats16 = [g16.reshape(N_CHIP, nrow, d) for g16 in _cast_bf16(mats)]
    g_win, from_sibling, recv_mats, slots_tail = _in_proj_bwd_w(
        h_t, pieces, mats16, _pack_tail(g_nv, g_bs, g_nf, g_ws, loss))
    core = lax.axis_index("c")
    half_own, half_win16 = _chip_half(g_win, from_sibling, core, chip, ncol)
    grad_x, g_nin, recv_win = _in_proj_bwd_x(pieces, wg, x2d, norm_in, dx2, half_win16)
    grad_x = grad_x.reshape(bsz, seq, d)

    sums = _add_received([half_own], [recv_win], jnp.zeros((), jnp.int32), True) + _add_received(
        mats, recv_mats, chip, False)
    *sibling_sums, slots_head = _swap_and_gather(sums, _slab(g_nin, d))
    stats = [_adamw_halves(w_in[0], m_w_in[0], v_w_in[0], sums[0], sibling_sums[0], core)] + _adamw_pairs(
        shards[1:], [m_w_o_gmlp[0], m_w_o_sb[0], m_w_out[0]], [v_w_o_gmlp[0], v_w_o_sb[0], v_w_out[0]],
        sums[1:], sibling_sums[1:])
    (gw_in, dw_in, nm_in, nv_in), (gw_og, dw_og, nm_og, nv_og), (gw_osb, dw_osb, nm_osb, nv_osb), \
        (gw_out, dw_out, nm_out, nv_out) = stats

    gs, ds, ms, vs = _adamw_small(
        _pack_small(norm_in, norm_v, b_s[0], norm_final, w_s[0]),
        _pack_small(m_norm_in, m_norm_v, m_b_s[0], m_norm_final, m_w_s[0]),
        _pack_small(v_norm_in, v_norm_v, v_b_s[0], v_norm_final, v_w_s[0]), slots_head, slots_tail)

    def small(pack):
        nin, nv, bs, nf, ws, _ = _unpack_small(pack, w_s.shape, b_s.shape)
        return nin, nv, ws, bs, nf

    loss = _unpack_small(gs, w_s.shape, b_s.shape)[-1]
    out = []
    for small_pack, win, wog, wosb, wout in ((gs, gw_in, gw_og, gw_osb, gw_out), (ds, dw_in, dw_og, dw_osb, dw_out),
                                             (ms, nm_in, nm_og, nm_osb, nm_out), (vs, nv_in, nv_og, nv_osb, nv_out)):
        nin, nv, ws, bs, nf = small(small_pack)
        out += [nin, win[None], nv, ws, bs, wog[None], wosb[None], wout[None], nf]
    return (loss, grad_x, *out)
```

```python
import functools
import math

import jax
import jax.numpy as jnp
from jax import lax
from jax.experimental import pallas as pl
from jax.experimental.pallas import tpu as pltpu

F32 = jnp.float32
BF16 = jnp.bfloat16
EPS = 1e-6
HEADS = 8
N_SPLIT = 9
N_CHIP = 4
MESH = pl.DeviceIdType.MESH

ADAM_LR = 0.001
ADAM_B1 = 0.9
ADAM_B2 = 0.999
ADAM_EPS = 1e-08
ADAM_WD = 0.01
ADAM_STEP = 10

VMEM_LIMIT = 56 * 2 ** 20
TM = 256
TMF = 512
TMX = 512
ATT_T = 256
ATT_HP = 4
TKW = 1024
CHUNKS_PER_STEP = 4
CAST_STEPS = 8

NT = (((1,), (1,)), ((), ()))
TN = (((0,), (0,)), ((), ()))


def _params(sem):
    return pltpu.CompilerParams(dimension_semantics=sem, vmem_limit_bytes=VMEM_LIMIT)


def _resident(shape):
    nd = len(shape)
    return pl.BlockSpec(shape, lambda *_: (0,) * nd, pipeline_mode=pl.Buffered(1))


def _const(shape):
    nd = len(shape)
    return pl.BlockSpec(shape, lambda *_: (0,) * nd)


def _segments(d, ncol):
    segs = []
    edges = sorted({j * ncol for j in range(N_CHIP + 1)} | {s * d for s in range(N_SPLIT + 1)})
    for lo, hi in zip(edges[:-1], edges[1:]):
        segs.append((lo // ncol, lo % ncol, lo // d, lo % d, hi - lo))
    return segs


def _sigmoid(x):
    return 0.5 * jnp.tanh(0.5 * x) + 0.5


_GELU_C = math.sqrt(2.0 / math.pi)


_GELU_CA = _GELU_C * 0.044715


def _gelu(x):
    return x * (0.5 * jnp.tanh(x * (_GELU_C + _GELU_CA * (x * x))) + 0.5)


def _gelu_and_grad(x):
    x2 = x * x
    u = 0.5 * jnp.tanh(x * (_GELU_C + _GELU_CA * x2)) + 0.5
    slope = (1.0 - u) * (x * (_GELU_C + (3.0 * _GELU_CA) * x2))
    return x * u, u * (2.0 * slope + 1.0)


def _split_bf16(a):
    hi = a.astype(BF16)
    lo = (a - hi.astype(F32)).astype(BF16)
    return hi, lo


def _in_proj_fwd(x2d, g_in, halves):
    t, d = x2d.shape
    n = len(halves)
    ncol = halves[0].shape[2]
    n_row = t // TMF
    last = n_row - 1
    assert halves[0].shape[1] * 2 == d
    qkv_parts = {j: (max(j * ncol, 3 * d) - j * ncol, max(j * ncol, 3 * d) - 3 * d)
                 for j in range(N_CHIP) if min((j + 1) * ncol, 6 * d) > max(j * ncol, 3 * d)}
    qkv_w = 3 * d // len(qkv_parts)
    assert all(min((j + 1) * ncol, 6 * d) - max(j * ncol, 3 * d) == qkv_w and q0 % qkv_w == 0
               for j, (_, q0) in qkv_parts.items())

    def body(order_ref, qrow_ref, qcol_ref, x_ref, g_ref, *rest):
        ins = rest[:n]
        proj_ref, qkv_ref, ht_ref = rest[n:n + 3]
        outs = rest[n + 3:2 * n + 3]
        wbuf, h_all, send_sems, recv_sems, local_sems, load_sem = rest[2 * n + 3:]
        phase = pl.program_id(0)
        i = pl.program_id(1)
        x_pos, y_pos, c_pos, chips = _place()
        sibling = (x_pos, y_pos, 1 - c_pos)
        me = (x_pos, y_pos, c_pos)
        my_chip = 2 * x_pos + y_pos

        def copy(a, k, block, to, src=None):
            return pltpu.make_async_remote_copy(
                src_ref=outs[a].at[block] if src is None else src, dst_ref=outs[a].at[block],
                send_sem=send_sems.at[a, k], recv_sem=recv_sems.at[a, k], device_id=to, device_id_type=MESH)

        def local(a):
            return pltpu.make_async_copy(ins[a], outs[a].at[pl.ds(2 * my_chip, 2)], local_sems.at[a])

        def load(src, first, slot):
            for half in range(2):
                cp = pltpu.make_async_copy(src.at[first + half], wbuf.at[slot, pl.ds(half * (d // 2), d // 2)], load_sem)
                cp.start()
                cp.wait()

        def send_mine(k):
            px, py = chips[k]
            for a in range(n):
                copy(a, k, 2 * my_chip + c_pos, (px, py, c_pos), src=ins[a].at[c_pos]).start()

        @pl.when((phase == 0) & (i == 0))
        def _():
            for a in range(n):
                local(a).start()
            send_mine(0)
            send_mine(1)
            load(ins[0], 0, 0)

        for k, (px, py) in enumerate(chips):
            @pl.when((phase == k + 1) & (i == 0))
            def _(k=k, px=px, py=py):
                theirs = 2 * (2 * px + py)
                for a in range(n):
                    copy(a, k, theirs + c_pos, me).wait_recv()
                    copy(a, 3 + k, theirs + c_pos, sibling).start()
                if k == 0:
                    send_mine(2)
                for a in range(n):
                    copy(a, 3 + k, theirs + 1 - c_pos, me).wait_recv()
                load(outs[0], theirs, (k + 1) % 2)

        @pl.when(phase == 0)
        def _():
            x = x_ref[...]
            r = lax.rsqrt(jnp.mean(x * x, axis=-1, keepdims=True) + EPS)
            hf = x * r * g_ref[...]
            h_all[i] = hf.astype(BF16)
            ht_ref[...] = hf.T.astype(BF16)

        for slot in range(2):
            @pl.when(phase % 2 == slot)
            def _(slot=slot):
                proj_ref[...] = jnp.dot(h_all[i], wbuf[slot], preferred_element_type=F32)

        for chunk, (c0, _) in qkv_parts.items():
            @pl.when(order_ref[phase] == chunk)
            def _(c0=c0):
                qkv_ref[...] = proj_ref[:, c0:c0 + qkv_w].astype(BF16)

        @pl.when((phase == N_CHIP - 1) & (i == n_row - 1))
        def _():
            for a in range(n):
                for k in range(2 * (N_CHIP - 1)):
                    copy(a, k, 0, me).wait_send()
                local(a).wait()

    x_pos, y_pos = lax.axis_index("x"), lax.axis_index("y")
    order = jnp.stack([2 * x_pos + y_pos, 2 * (1 - x_pos) + y_pos, 2 * x_pos + 1 - y_pos,
                       2 * (1 - x_pos) + 1 - y_pos]).astype(jnp.int32)
    holds = [functools.reduce(jnp.logical_or, [order[p] == j for j in qkv_parts]) for p in range(N_CHIP)]
    col = [sum(jnp.where(order[p] == j, q0 // qkv_w, 0) for j, (_, q0) in qkv_parts.items()) for p in range(N_CHIP)]
    cur = col[-1]
    for p in reversed(range(N_CHIP - 1)):
        cur = jnp.where(holds[p], col[p], cur)
    seen = jnp.bool_(False)
    qrow, qcol = [], []
    for p in range(N_CHIP):
        cur = jnp.where(holds[p], col[p], cur)
        qrow.append(jnp.where(holds[p], -1, jnp.where(seen, last, 0)))
        qcol.append(cur)
        seen = seen | holds[p]
    qrow = jnp.stack(qrow).astype(jnp.int32)
    qcol = jnp.stack(qcol).astype(jnp.int32)

    outs = pl.pallas_call(
        body, name="in_proj_fwd",
        grid_spec=pltpu.PrefetchScalarGridSpec(
            num_scalar_prefetch=3, grid=(N_CHIP, n_row),
            in_specs=[pl.BlockSpec((TMF, d), lambda p, i, order, qrow, qcol: (jnp.where(p == 0, i, last), 0)),
                      pl.BlockSpec((1, d), lambda p, i, order, qrow, qcol: (0, 0))] + [ANY] * n,
            out_specs=[pl.BlockSpec((TMF, ncol), lambda p, i, order, qrow, qcol: (i, order[p])),
                       pl.BlockSpec((TMF, qkv_w),
                                    lambda p, i, order, qrow, qcol: (jnp.where(qrow[p] < 0, i, qrow[p]), qcol[p])),
                       pl.BlockSpec((d, TMF), lambda p, i, order, qrow, qcol: (0, jnp.where(p == 0, i, last)))]
            + [ANY] * n,
            scratch_shapes=[pltpu.VMEM((2, d, ncol), BF16), pltpu.VMEM((n_row, TMF, d), BF16),
                            pltpu.SemaphoreType.DMA((n, 2 * (N_CHIP - 1))), pltpu.SemaphoreType.DMA((n, 2 * (N_CHIP - 1))),
                            pltpu.SemaphoreType.DMA((n,)), pltpu.SemaphoreType.DMA]),
        out_shape=[jax.ShapeDtypeStruct((t, N_CHIP * ncol), F32), jax.ShapeDtypeStruct((t, 3 * d), BF16),
                   jax.ShapeDtypeStruct((d, t), BF16)]
        + [jax.ShapeDtypeStruct((2 * N_CHIP,) + hv.shape[1:], BF16) for hv in halves],
        compiler_params=_params(("arbitrary", "arbitrary")),
    )(order, qrow, qcol, x2d, g_in, *halves)
    return outs[0], outs[1], outs[2], outs[3:]


def _branch_a_fwd(a_pre, g_v, wm, b_t):
    t = a_pre.shape[0]
    d = g_v.shape[1]
    d3 = 3 * d
    ng, chunk, _ = wm.shape
    cw = d // ng

    per_step = CHUNKS_PER_STEP if t % (CHUNKS_PER_STEP * chunk) == 0 else 1

    def body(a_ref, gv_ref, wm_ref, bt_ref, ya_ref):
        for n in range(per_step):
            rows = slice(n * chunk, (n + 1) * chunk)
            ua = _gelu(a_ref[rows, 0:d])
            vg = _gelu(a_ref[rows, d:2 * d])
            za = a_ref[rows, 2 * d:3 * d]
            rv = lax.rsqrt(jnp.mean(vg * vg, axis=-1, keepdims=True) + EPS)
            va = (vg * rv * gv_ref[...]).astype(BF16)
            gate = ua * (za * _sigmoid(za))
            for g in range(ng):
                sl = slice(g * cw, (g + 1) * cw)
                mixed = jnp.dot(wm_ref[g], va[:, sl], preferred_element_type=F32) + bt_ref[:, g:g + 1]
                ya_ref[rows, sl] = (gate[:, sl] * mixed).astype(BF16)

    tile = per_step * chunk
    return pl.pallas_call(
        body, name="branch_a_fwd",
        grid=(t // tile,),
        in_specs=[pl.BlockSpec((tile, d3), lambda i: (i, 0)), _const((1, d)), _const(wm.shape), _const(b_t.shape)],
        out_specs=pl.BlockSpec((tile, d), lambda i: (i, 0)),
        out_shape=jax.ShapeDtypeStruct((t, d), BF16),
        compiler_params=_params(("arbitrary",)),
    )(a_pre, g_v, wm, b_t)


def _branch_a_bwd(a_pre, dya, g_v, wm, wm_t, b_t):
    t = a_pre.shape[0]
    d = g_v.shape[1]
    d3 = 3 * d
    ng, chunk, _ = wm.shape
    cw = d // ng
    nsteps = t // chunk

    def body(a_ref, dya_ref, gv_ref, wm_ref, wmt_ref, bt_ref, da_ref, gws_ref, gbt_ref, gnv_ref, db_acc):
        i = pl.program_id(0)

        @pl.when(i == 0)
        def _():
            gws_ref[...] = jnp.zeros_like(gws_ref)
            gnv_ref[...] = jnp.zeros_like(gnv_ref)
            db_acc[...] = jnp.zeros_like(db_acc)

        ua, dgelu_u = _gelu_and_grad(a_ref[:, 0:d])
        vg, dgelu_v = _gelu_and_grad(a_ref[:, d:2 * d])
        za = a_ref[:, 2 * d:3 * d]
        sig = _sigmoid(za)
        sz = za * sig
        dsz = sig * (1.0 + za * (1.0 - sig))
        rv = lax.rsqrt(jnp.mean(vg * vg, axis=-1, keepdims=True) + EPS)
        nv = vg * rv
        gv = gv_ref[...]
        va = (nv * gv).astype(BF16)
        dya = dya_ref[...]
        dmix = dya * ua * sz
        db_acc[...] += dmix
        dmix_b = dmix.astype(BF16)
        t_gate = dya * sz
        t_z = dya * ua * dsz
        dva_parts = []
        for g in range(ng):
            sl = slice(g * cw, (g + 1) * cw)
            mixed = jnp.dot(wm_ref[g], va[:, sl], preferred_element_type=F32) + bt_ref[:, g:g + 1]
            da_ref[:, sl] = (t_gate[:, sl] * mixed * dgelu_u[:, sl]).astype(BF16)
            da_ref[:, 2 * d + g * cw:2 * d + (g + 1) * cw] = (t_z[:, sl] * mixed).astype(BF16)
            gws_ref[g] += lax.dot_general(dmix_b[:, sl], va[:, sl], NT, preferred_element_type=F32)
            dva_parts.append(jnp.dot(wmt_ref[g], dmix_b[:, sl], preferred_element_type=F32))
        dva = jnp.concatenate(dva_parts, axis=1)
        gnv_ref[...] += jnp.sum(dva * nv, axis=0, keepdims=True)
        dnv = dva * gv
        dvg = rv * (dnv - nv * jnp.mean(dnv * nv, axis=-1, keepdims=True))
        da_ref[:, d:2 * d] = (dvg * dgelu_v).astype(BF16)

        @pl.when(i == nsteps - 1)
        def _():
            acc = db_acc[...]
            for g in range(ng):
                gbt_ref[:, g:g + 1] = jnp.sum(acc[:, g * cw:(g + 1) * cw], axis=1, keepdims=True)

    return pl.pallas_call(
        body, name="branch_a_bwd",
        grid=(nsteps,),
        in_specs=[pl.BlockSpec((chunk, d3), lambda i: (i, 0)), pl.BlockSpec((chunk, d), lambda i: (i, 0)),
                  _const((1, d)), _const(wm.shape), _const(wm_t.shape), _const(b_t.shape)],
        out_specs=[pl.BlockSpec((chunk, d3), lambda i: (i, 0)), _const(wm.shape), _const(b_t.shape), _const((1, d))],
        out_shape=[jax.ShapeDtypeStruct((t, d3), BF16), jax.ShapeDtypeStruct(wm.shape, F32),
                   jax.ShapeDtypeStruct(b_t.shape, F32), jax.ShapeDtypeStruct((1, d), F32)],
        scratch_shapes=[pltpu.VMEM((chunk, d), F32)],
        compiler_params=_params(("arbitrary",)),
    )(a_pre, dya, g_v, wm, wm_t, b_t)


def _tri(n, rows_gt_cols):
    r = lax.broadcasted_iota(jnp.int32, (n, n), 0)
    c = lax.broadcasted_iota(jnp.int32, (n, n), 1)
    return (r > c) if rows_gt_cols else (r < c)


def _twice(tri):
    t = tri.astype(BF16)
    return jnp.concatenate([t, t], axis=0)


def _cumsum_mm(a, tri2):
    hi, lo = _split_bf16(a)
    return jnp.dot(jnp.concatenate([hi, lo], axis=1), tri2, preferred_element_type=F32)


LOG2E = 1.4426950408889634
_SIGN = 0x80000000


def _sb_block(q, k, scale, upper2, causal):
    z2 = lax.dot_general(q, k, NT, preferred_element_type=F32) * (scale * LOG2E)
    neg_abs = lax.bitcast_convert_type(lax.bitcast_convert_type(z2, jnp.uint32) | jnp.uint32(_SIGN), F32)
    l2 = jnp.log(1.0 + jnp.exp2(neg_abs)) * LOG2E
    log_beta = jnp.minimum(z2, 0.0) - l2
    lom = log_beta - z2
    if causal is not None:
        lom = jnp.where(causal, lom, 0.0)
    sx = _cumsum_mm(lom, upper2)
    return log_beta, sx, sx[:, 0:1] + lom[:, 0:1]


DEAD_LOG2 = -150.0


def _max_carry(carries):
    return jnp.max(functools.reduce(jnp.maximum, carries))


ZB_GROUP, GA_GROUP, GB_GROUP = 6, 7, 8


def _attn_specs(d, seq, nq, heads_per_step):
    hp_w = heads_per_step * (d // HEADS)
    n_hp = d // hp_w
    row_blk = lambda group: pl.BlockSpec((ATT_T, hp_w), lambda b, h, i: (b * nq + i, group * n_hp + h))
    seq_blk = lambda group: pl.BlockSpec((seq, hp_w), lambda b, h, i: (b, group * n_hp + h))
    return row_blk, seq_blk, n_hp


def _attn_fwd(qkv, proj, bsz, seq):
    t, d3 = qkv.shape
    d = d3 // 3
    hd = d // HEADS
    nq = seq // ATT_T
    scale = hd ** -0.5
    n_heads = ATT_HP
    row_blk, seq_blk, n_hp = _attn_specs(d, seq, nq, n_heads)

    def body(q_ref, k_ref, v_ref, zb_ref, o_ref, yb_ref):
        i = pl.program_id(2)
        causal = _tri(ATT_T, True)
        upper2 = _twice(causal)

        def step(kb, state, mask):
            rows = pl.ds(pl.multiple_of(kb * ATT_T, ATT_T), ATT_T)
            heads = [slice(h * hd, (h + 1) * hd) for h in range(n_heads)]
            scores = [_sb_block(q_ref[:, cols], k_ref[rows, cols], scale, upper2, mask) for cols in heads]
            new = []
            for cols, (carry, acc), (log_beta, sx, total) in zip(heads, state, scores):
                a = jnp.exp2(log_beta + sx + carry)
                if mask is not None:
                    a = jnp.where(mask, a, 0.0)
                acc = acc + jnp.dot(a.astype(BF16), v_ref[rows, cols], preferred_element_type=F32)
                new.append((carry + total, acc))
            return tuple(new)

        init = tuple((jnp.zeros((ATT_T, 1), F32), jnp.zeros((ATT_T, hd), F32)) for _ in range(n_heads))
        state = step(i, init, causal)
        def more(c):
            new = step(c[0], c[1], None)
            return c[0] - 1, new, _max_carry([s[0] for s in new])

        _, state, _ = lax.while_loop(lambda c: (c[0] >= 0) & (c[2] > DEAD_LOG2), more,
                                     (i - 1, state, _max_carry([s[0] for s in state])))
        for h in range(n_heads):
            cols = slice(h * hd, (h + 1) * hd)
            acc = state[h][1]
            zb = zb_ref[:, cols]
            o_ref[:, cols] = acc
            yb_ref[:, cols] = (acc * (zb * _sigmoid(zb))).astype(BF16)

    return pl.pallas_call(
        body, name="attn_fwd",
        grid=(bsz, n_hp, nq),
        in_specs=[row_blk(0), seq_blk(1), seq_blk(2), row_blk(ZB_GROUP)],
        out_specs=[row_blk(0), row_blk(0)],
        out_shape=[jax.ShapeDtypeStruct((t, d), F32), jax.ShapeDtypeStruct((t, d), BF16)],
        compiler_params=_params(("arbitrary", "arbitrary", "arbitrary")),
    )(qkv, qkv, qkv, proj)


def _attn_bwd(qkv, proj, o, dyb, bsz, seq):
    t, d3 = qkv.shape
    d = d3 // 3
    hd = d // HEADS
    nq = seq // ATT_T
    scale = hd ** -0.5
    row_blk, seq_blk, n_hp = _attn_specs(d, seq, nq, ATT_HP)

    def body(q_ref, k_ref, v_ref, zb_ref, o_ref, dyb_ref, dq_ref, dk_ref, dv_ref, dzb_ref,
             g_s, beta_s, dkt_acc, dvt_acc):
        i = pl.program_id(2)

        @pl.when(i == 0)
        def _():
            dkt_acc[...] = jnp.zeros_like(dkt_acc)
            dvt_acc[...] = jnp.zeros_like(dvt_acc)

        causal = _tri(ATT_T, True)
        upper2 = _twice(causal)
        lower2 = _twice(~causal)
        zb = zb_ref[...]
        sig = _sigmoid(zb)
        dyb_t = dyb_ref[...]
        do_f = dyb_t * (zb * sig)
        do = do_f.astype(BF16)
        do_t = do_f.T.astype(BF16)
        q_t = q_ref[...].astype(F32).T.astype(BF16)
        dzb_ref[...] = (dyb_t * o_ref[...] * (sig * (1.0 + zb * (1.0 - sig)))).astype(BF16)

        def sweep(kb, carries, mask):
            rows = pl.ds(pl.multiple_of(kb * ATT_T, ATT_T), ATT_T)
            heads = [slice(h * hd, (h + 1) * hd) for h in range(ATT_HP)]
            scores = [_sb_block(q_ref[:, cols], k_ref[rows, cols], scale, upper2, mask) for cols in heads]
            das = [lax.dot_general(do[:, cols], v_ref[rows, cols], NT, preferred_element_type=F32) for cols in heads]
            new = []
            for h, (cols, carry, (log_beta, sx, total), da) in enumerate(zip(heads, carries, scores, das)):
                a = jnp.exp2(log_beta + sx + carry)
                beta = jnp.exp2(log_beta)
                if mask is not None:
                    a = jnp.where(mask, a, 0.0)
                    beta = jnp.where(mask, beta, 0.0)
                g_s[h, kb] = a * da
                beta_s[h, kb] = beta
                dvt_acc[kb, cols, :] += jnp.dot(do_t[cols, :], a.astype(BF16), preferred_element_type=F32)
                new.append(carry + total)
            return tuple(new)

        carries = sweep(i, tuple(jnp.zeros((ATT_T, 1), F32) for _ in range(ATT_HP)), causal)

        def more(c):
            new = sweep(c[0], c[1], None)
            return c[0] - 1, new, _max_carry(new)

        last, _, _ = lax.while_loop(lambda c: (c[0] >= 0) & (c[2] > DEAD_LOG2), more, (i - 1, carries, _max_carry(carries)))
        first_kb = last + 1

        def back(kb, state):
            rows = pl.ds(pl.multiple_of(kb * ATT_T, ATT_T), ATT_T)
            heads = [slice(h * hd, (h + 1) * hd) for h in range(ATT_HP)]
            sums = [_cumsum_mm(g_s[h, kb], lower2) for h in range(ATT_HP)]
            new = []
            for h, (cols, (p_carry, dq), px) in enumerate(zip(heads, state, sums)):
                dz = ((g_s[h, kb] - (p_carry + px) * beta_s[h, kb]) * scale).astype(BF16)
                dq = dq + jnp.dot(dz, k_ref[rows, cols], preferred_element_type=F32)
                dkt_acc[kb, cols, :] += jnp.dot(q_t[cols, :], dz, preferred_element_type=F32)
                new.append((p_carry + px[:, ATT_T - 1:ATT_T], dq))
            return tuple(new)

        init = tuple((jnp.zeros((ATT_T, 1), F32), jnp.zeros((ATT_T, hd), F32)) for _ in range(ATT_HP))
        state = lax.fori_loop(first_kb, i + 1, back, init)
        for h in range(ATT_HP):
            dq_ref[:, h * hd:(h + 1) * hd] = state[h][1].astype(BF16)

        @pl.when(i == nq - 1)
        def _():
            for kb in range(nq):
                dk_ref[kb * ATT_T:(kb + 1) * ATT_T, :] = dkt_acc[kb].T.astype(BF16)
                dv_ref[kb * ATT_T:(kb + 1) * ATT_T, :] = dvt_acc[kb].T.astype(BF16)

    out = jax.ShapeDtypeStruct((t, d), BF16)
    hp_w = ATT_HP * hd
    return pl.pallas_call(
        body, name="attn_bwd",
        grid=(bsz, n_hp, nq),
        in_specs=[row_blk(0), seq_blk(1), seq_blk(2), row_blk(ZB_GROUP), row_blk(0), row_blk(0)],
        out_specs=[row_blk(0), seq_blk(0), seq_blk(0), row_blk(0)],
        out_shape=[out, out, out, out],
        scratch_shapes=[pltpu.VMEM((ATT_HP, nq, ATT_T, ATT_T), F32), pltpu.VMEM((ATT_HP, nq, ATT_T, ATT_T), F32),
                        pltpu.VMEM((nq, hp_w, ATT_T), F32), pltpu.VMEM((nq, hp_w, ATT_T), F32)],
        compiler_params=_params(("arbitrary", "arbitrary", "arbitrary")),
    )(qkv, qkv, qkv, proj, o, dyb)


def _out_proj(ya, yb, g_pre, x2d, tgt, w_og, w_osb, w_out, g_f):
    t, d = x2d.shape

    def body(ya_ref, yb_ref, ga_ref, gb_ref, x_ref, tgt_ref, wog_ref, wosb_ref, wout_ref, gf_ref,
             dya_ref, dyb_ref, dg_ref, dx2_ref, loss_ref, gnf_ref, gwog_ref, gwosb_ref, gwout_ref):
        @pl.when(pl.program_id(0) == 0)
        def _():
            loss_ref[...] = jnp.zeros_like(loss_ref)
            gnf_ref[...] = jnp.zeros_like(gnf_ref)
            gwog_ref[...] = jnp.zeros_like(gwog_ref)
            gwosb_ref[...] = jnp.zeros_like(gwosb_ref)
            gwout_ref[...] = jnp.zeros_like(gwout_ref)

        ya = ya_ref[...]
        yb = yb_ref[...]
        pa = jnp.dot(ya, wog_ref[...], preferred_element_type=F32)
        pb = jnp.dot(yb, wosb_ref[...], preferred_element_type=F32)
        sga = _sigmoid(ga_ref[...])
        sgb = _sigmoid(gb_ref[...])
        merged = (sga * pa + sgb * pb).astype(BF16)
        x2 = x_ref[...] + jnp.dot(merged, wout_ref[...], preferred_element_type=F32)
        r2 = lax.rsqrt(jnp.mean(x2 * x2, axis=-1, keepdims=True) + EPS)
        n2 = x2 * r2
        gf = gf_ref[...]
        err = n2 * gf - tgt_ref[...]
        loss_ref[...] += 0.5 * jnp.sum(jnp.sum(err * err, axis=-1, keepdims=True), axis=0, keepdims=True) / d
        dy = err * (1.0 / d)
        gnf_ref[...] += jnp.sum(dy * n2, axis=0, keepdims=True)
        dn = dy * gf
        dx2 = r2 * (dn - n2 * jnp.mean(dn * n2, axis=-1, keepdims=True))
        dx2_ref[...] = dx2
        dx2_b = dx2.astype(BF16)
        dmerged = lax.dot_general(dx2_b, wout_ref[...], NT, preferred_element_type=F32)
        gwout_ref[...] += lax.dot_general(merged, dx2_b, TN, preferred_element_type=F32)
        dg_ref[:, 0:d] = (dmerged * pa * (sga * (1.0 - sga))).astype(BF16)
        dg_ref[:, d:2 * d] = (dmerged * pb * (sgb * (1.0 - sgb))).astype(BF16)
        dpa = (dmerged * sga).astype(BF16)
        dpb = (dmerged * sgb).astype(BF16)
        dya_ref[...] = lax.dot_general(dpa, wog_ref[...], NT, preferred_element_type=F32)
        dyb_ref[...] = lax.dot_general(dpb, wosb_ref[...], NT, preferred_element_type=F32)
        gwog_ref[...] += lax.dot_general(ya, dpa, TN, preferred_element_type=F32)
        gwosb_ref[...] += lax.dot_general(yb, dpb, TN, preferred_element_type=F32)

    row = lambda i: (i, 0)
    return pl.pallas_call(
        body, name="out_proj",
        grid=(t // TM,),
        in_specs=[pl.BlockSpec((TM, d), row), pl.BlockSpec((TM, d), row),
                  pl.BlockSpec((TM, d), lambda i: (i, GA_GROUP)), pl.BlockSpec((TM, d), lambda i: (i, GB_GROUP)),
                  pl.BlockSpec((TM, d), row), pl.BlockSpec((TM, d), row),
                  _resident((d, d)), _resident((d, d)), _resident((d, d)), _const((1, d))],
        out_specs=[pl.BlockSpec((TM, d), row), pl.BlockSpec((TM, d), row), pl.BlockSpec((TM, 2 * d), row),
                   pl.BlockSpec((TM, d), row), _const((1, 1)), _const((1, d)),
                   _const((d, d)), _const((d, d)), _const((d, d))],
        out_shape=[jax.ShapeDtypeStruct((t, d), F32), jax.ShapeDtypeStruct((t, d), F32),
                   jax.ShapeDtypeStruct((t, 2 * d), BF16), jax.ShapeDtypeStruct((t, d), F32),
                   jax.ShapeDtypeStruct((1, 1), F32), jax.ShapeDtypeStruct((1, d), F32),
                   jax.ShapeDtypeStruct((d, d), F32), jax.ShapeDtypeStruct((d, d), F32),
                   jax.ShapeDtypeStruct((d, d), F32)],
        compiler_params=_params(("arbitrary",)),
    )(ya, yb, g_pre, g_pre, x2d, tgt, w_og, w_osb, w_out, g_f)


def _dproj_pieces(d):
    return [(0, 0, 3), (1, 3, 1), (2, 4, 1), (3, 5, 1), (4, 6, 1), (5, 7, 2)]


def _in_proj_bwd_x(pieces, wg, x2d, g_in, dx2, gw16):
    t, d = x2d.shape
    ncol = wg.shape[2]
    segs = _segments(d, ncol)
    layout = _dproj_pieces(d)
    nsteps = t // TMX

    def body(da_ref, dq_ref, dk_ref, dv_ref, dzb_ref, dg_ref, w_ref, x_ref, g_ref, dx2_ref, gw16_ref,
             gx_ref, gn_ref, recv_ref, send_sems, recv_sems):
        x_pos, y_pos, c_pos, chips = _place()

        def share(k, chunk):
            px, py = chips[k]
            return pltpu.make_async_remote_copy(
                src_ref=gw16_ref.at[:, chunk * ncol:(chunk + 1) * ncol], dst_ref=recv_ref.at[k],
                send_sem=send_sems.at[k], recv_sem=recv_sems.at[k], device_id=(px, py, c_pos), device_id_type=MESH)

        @pl.when(pl.program_id(0) == 0)
        def _():
            gn_ref[...] = jnp.zeros_like(gn_ref)
            for k, (px, py) in enumerate(chips):
                for chunk in range(N_CHIP):
                    @pl.when(2 * px + py == chunk)
                    def _(k=k, chunk=chunk):
                        share(k, chunk).start()

        @pl.when(pl.program_id(0) == nsteps - 1)
        def _():
            for k in range(N_CHIP - 1):
                share(k, 0).wait()

        refs = (da_ref, dq_ref, dk_ref, dv_ref, dzb_ref, dg_ref)
        dh = jnp.zeros((TMX, d), F32)
        for chip, c0, grp, s0, width in segs:
            piece, first, _ = next(p for p in layout if p[1] <= grp < p[1] + p[2])
            off = (grp - first) * d + s0
            dh = dh + lax.dot_general(refs[piece][:, off:off + width], w_ref[chip, :, c0:c0 + width], NT,
                                      preferred_element_type=F32)
        x = x_ref[...]
        r = lax.rsqrt(jnp.mean(x * x, axis=-1, keepdims=True) + EPS)
        n = x * r
        gn_ref[...] += jnp.sum(dh * n, axis=0, keepdims=True)
        dn = dh * g_ref[...]
        gx_ref[...] = dx2_ref[...] + r * (dn - n * jnp.mean(dn * n, axis=-1, keepdims=True))

    row = lambda i: (i, 0)
    return pl.pallas_call(
        body, name="in_proj_bwd_x",
        grid=(nsteps,),
        in_specs=[pl.BlockSpec((TMX, p.shape[1]), row) for p in pieces]
        + [_resident(wg.shape), pl.BlockSpec((TMX, d), row), _const((1, d)), pl.BlockSpec((TMX, d), row), ANY],
        out_specs=[pl.BlockSpec((TMX, d), row), _const((1, d)), ANY],
        out_shape=[jax.ShapeDtypeStruct((t, d), F32), jax.ShapeDtypeStruct((1, d), F32),
                   jax.ShapeDtypeStruct((N_CHIP - 1, gw16.shape[0], ncol), BF16)],
        scratch_shapes=[pltpu.SemaphoreType.DMA((N_CHIP - 1,)), pltpu.SemaphoreType.DMA((N_CHIP - 1,))],
        compiler_params=_params(("arbitrary",)),
    )(*pieces, wg, x2d, g_in, dx2, gw16)


def _in_proj_bwd_w(h_t, pieces, mats16, pack):
    d, t = h_t.shape
    nk = t // TKW
    half = d // 2
    layout = _dproj_pieces(d)
    n_mats = len(mats16)
    n_dev = 2 * N_CHIP
    flips = [(dx, dy, dc) for dx in (0, 1) for dy in (0, 1) for dc in (0, 1)][1:]

    def body(ht_ref, da_ref, dq_ref, dk_ref, dv_ref, dzb_ref, dg_ref, *rest):
        mat_refs, pack_ref = rest[:n_mats], rest[n_mats]
        gw_ref, sib_ref = rest[n_mats + 1:n_mats + 3]
        recv_refs, slots_ref = rest[n_mats + 3:2 * n_mats + 3], rest[2 * n_mats + 3]
        acc, stage, mat_send, mat_recv, pack_send, pack_recv, own_sem, stage_send, sib_recv = rest[2 * n_mats + 4:]
        s = pl.program_id(0)
        i = pl.program_id(1)
        x_pos, y_pos, c_pos, chips = _place()
        me = 4 * x_pos + 2 * y_pos + c_pos

        def to_sibling(slot, group):
            return pltpu.make_async_remote_copy(
                src_ref=stage.at[slot], dst_ref=sib_ref.at[:, group * d:(group + 1) * d],
                send_sem=stage_send.at[slot], recv_sem=sib_recv, device_id=(x_pos, y_pos, 1 - c_pos), device_id_type=MESH)

        def exchanges():
            cps = []
            for k, (px, py) in enumerate(chips):
                for a in range(n_mats):
                    cps.append(pltpu.make_async_remote_copy(
                        src_ref=mat_refs[a].at[2 * px + py], dst_ref=recv_refs[a].at[k],
                        send_sem=mat_send.at[a, k], recv_sem=mat_recv.at[a, k],
                        device_id=(px, py, c_pos), device_id_type=MESH))
            for k, (dx, dy, dc) in enumerate(flips):
                peer = (1 - x_pos if dx else x_pos, 1 - y_pos if dy else y_pos, 1 - c_pos if dc else c_pos)
                cps.append(pltpu.make_async_remote_copy(
                    src_ref=pack_ref, dst_ref=slots_ref.at[me], send_sem=pack_send.at[k], recv_sem=pack_recv.at[k],
                    device_id=peer, device_id_type=MESH))
            return cps, pltpu.make_async_copy(pack_ref, slots_ref.at[me], own_sem)

        @pl.when((s == 0) & (i == 0))
        def _():
            cps, own = exchanges()
            own.start()
            for cp in cps:
                cp.start()

        @pl.when(i == 0)
        def _():
            acc[...] = jnp.zeros_like(acc)

        refs = (da_ref, dq_ref, dk_ref, dv_ref, dzb_ref, dg_ref)
        for piece, first, count in layout:
            @pl.when((s >= first) & (s < first + count))
            def _(piece=piece):
                acc[...] += jnp.dot(ht_ref[...], refs[piece][...], preferred_element_type=F32)

        @pl.when(i == nk - 1)
        def _():
            gw_ref[...] = acc[...]
            for slot in range(2):
                @pl.when(s % 2 == slot)
                def _(slot=slot):
                    @pl.when(s >= 2)
                    def _():
                        to_sibling(slot, 0).wait_send()
                    for other in range(2):
                        @pl.when(c_pos == 1 - other)
                        def _(other=other):
                            stage[slot] = acc[other * half:(other + 1) * half, :].astype(BF16)
                    for group in range(N_SPLIT):
                        @pl.when(s == group)
                        def _(group=group):
                            to_sibling(slot, group).start()

        @pl.when((s == N_SPLIT - 1) & (i == nk - 1))
        def _():
            for slot in range(2):
                to_sibling(slot, 0).wait_send()
            pltpu.make_async_remote_copy(
                src_ref=sib_ref, dst_ref=sib_ref, send_sem=stage_send.at[0], recv_sem=sib_recv,
                device_id=(x_pos, y_pos, c_pos), device_id_type=MESH).wait_recv()
            cps, own = exchanges()
            own.wait()
            for cp in cps:
                cp.wait()

    def piece_spec(p, first, count):
        def index(s, i):
            mine = (s >= first) & (s < first + count)
            return jnp.where(mine, i, 0), jnp.where(mine, s - first, 0)
        return pl.BlockSpec((TKW, d), index)

    col_blk = pl.BlockSpec((d, d), lambda s, i: (0, s))
    outs = pl.pallas_call(
        body, name="in_proj_bwd_w",
        grid=(N_SPLIT, nk),
        in_specs=[pl.BlockSpec((d, TKW), lambda s, i: (0, i))] + [piece_spec(*p) for p in layout] + [ANY] * (n_mats + 1),
        out_specs=[col_blk, ANY] + [ANY] * (n_mats + 1),
        out_shape=[jax.ShapeDtypeStruct((d, N_SPLIT * d), F32), jax.ShapeDtypeStruct((half, N_SPLIT * d), BF16)]
        + [jax.ShapeDtypeStruct((N_CHIP - 1,) + m.shape[1:], BF16) for m in mats16]
        + [jax.ShapeDtypeStruct((n_dev,) + pack.shape, F32)],
        scratch_shapes=[pltpu.VMEM((d, d), F32), pltpu.VMEM((2, half, d), BF16),
                        pltpu.SemaphoreType.DMA((n_mats, N_CHIP - 1)), pltpu.SemaphoreType.DMA((n_mats, N_CHIP - 1)),
                        pltpu.SemaphoreType.DMA((n_dev - 1,)), pltpu.SemaphoreType.DMA((n_dev - 1,)),
                        pltpu.SemaphoreType.DMA, pltpu.SemaphoreType.DMA((2,)), pltpu.SemaphoreType.DMA],
        compiler_params=_params(("arbitrary", "arbitrary")),
    )(h_t, *pieces, *mats16, pack)
    return outs[0], outs[1], outs[2:2 + n_mats], outs[2 + n_mats]


def _local_step(proj, qkv, x2d, tgt2d, bsz, seq, norm_v, w_s, b_s, w_og, w_osb, w_out, norm_final):
    d = x2d.shape[1]
    chunk = w_s.shape[-1]
    causal = jnp.tril(jnp.ones((chunk, chunk), dtype=bool))
    wm = jnp.where(causal[None], w_s, 0.0).astype(BF16)
    wm_t = jnp.swapaxes(wm, 1, 2)
    b_t = b_s.T

    ya = _branch_a_fwd(proj, norm_v, wm, b_t)
    o, yb = _attn_fwd(qkv, proj, bsz, seq)
    dya, dyb, dg, dx2, loss, g_nf, g_wog, g_wosb, g_wout = _out_proj(
        ya, yb, proj, x2d, tgt2d, w_og, w_osb, w_out, norm_final.reshape(1, d))
    dq, dk, dv, dzb = _attn_bwd(qkv, proj, o, dyb, bsz, seq)
    d_a, g_ws, g_bt, g_nv = _branch_a_bwd(proj, dya, norm_v, wm, wm_t, b_t)
    g_ws = jnp.where(causal[None], g_ws, 0.0)
    return loss, (d_a, dq, dk, dv, dzb, dg), dx2, g_nv, g_ws, g_bt.T, g_wog, g_wosb, g_wout, g_nf


def _row_tile(rows):
    return next(r for r in (128, 64, 32, 16, 8) if rows % r == 0)


def _cast_bf16(arrs):
    n = len(arrs)

    def body(*refs):
        for a_ref, o_ref in zip(refs[:n], refs[n:]):
            o_ref[...] = a_ref[...].astype(BF16)

    specs = [pl.BlockSpec((a.shape[0] // CAST_STEPS, a.shape[1]), lambda i: (i, 0)) for a in arrs]
    return pl.pallas_call(
        body, name="cast_bf16", grid=(CAST_STEPS,),
        in_specs=specs, out_specs=specs,
        out_shape=[jax.ShapeDtypeStruct(a.shape, BF16) for a in arrs],
        compiler_params=_params(("arbitrary",)),
    )(*arrs)


def _chip_half(full, from_sibling, core, chip, tile):
    half, n = from_sibling.shape

    def body(where_ref, own_ref, sib_ref, o32_ref, o16_ref):
        total = own_ref[...] + sib_ref[...].astype(F32)
        o16_ref[...] = total.astype(BF16)

        @pl.when(pl.program_id(0) == where_ref[1])
        def _():
            o32_ref[...] = total

    blk = pl.BlockSpec((half, tile), lambda j, where_ref: (0, j))
    return pl.pallas_call(
        body, name="chip_half",
        grid_spec=pltpu.PrefetchScalarGridSpec(
            num_scalar_prefetch=1, grid=(n // tile,),
            in_specs=[pl.BlockSpec((half, tile), lambda j, where_ref: (where_ref[0], j)), blk],
            out_specs=[pl.BlockSpec((half, tile), lambda j, where_ref: (0, 0)), blk]),
        out_shape=[jax.ShapeDtypeStruct((half, tile), F32), jax.ShapeDtypeStruct((half, n), BF16)],
        compiler_params=_params(("arbitrary",)),
    )(jnp.stack([core, chip]).astype(jnp.int32), full, from_sibling)


def _add_received(fulls, recvs, chip, by_cols):
    n = len(fulls)
    _, rows, cols = recvs[0].shape
    tr = _row_tile(rows)
    nb = rows // tr

    def body(chip_ref, *refs):
        for own_ref, recv_ref, o_ref in zip(refs[:n], refs[n:2 * n], refs[2 * n:]):
            s = own_ref[...]
            for k in range(N_CHIP - 1):
                s = s + recv_ref[k].astype(F32)
            o_ref[...] = s

    own_map = (lambda i, chip_ref: (i, chip_ref[0])) if by_cols else (lambda i, chip_ref: (chip_ref[0] * nb + i, 0))
    return pl.pallas_call(
        body, name="add_received",
        grid_spec=pltpu.PrefetchScalarGridSpec(
            num_scalar_prefetch=1, grid=(nb,),
            in_specs=[pl.BlockSpec((tr, cols), own_map)] * n
            + [pl.BlockSpec((N_CHIP - 1, tr, cols), lambda i, chip_ref: (0, i, 0))] * n,
            out_specs=[pl.BlockSpec((tr, cols), lambda i, chip_ref: (i, 0))] * n),
        out_shape=[jax.ShapeDtypeStruct((rows, cols), F32)] * n,
        compiler_params=_params(("arbitrary",)),
    )(chip.reshape(1).astype(jnp.int32), *fulls, *recvs)


def _adamw_math(w, m, v, g):
    new_m = ADAM_B1 * m + (1.0 - ADAM_B1) * g
    new_v = ADAM_B2 * v + (1.0 - ADAM_B2) * (g * g)
    m_hat = new_m / (1.0 - ADAM_B1 ** ADAM_STEP)
    v_hat = new_v / (1.0 - ADAM_B2 ** ADAM_STEP)
    return -ADAM_LR * (m_hat / (jnp.sqrt(v_hat) + ADAM_EPS) + ADAM_WD * w), new_m, new_v


def _adamw_pairs(ws, ms, vs, mines, theirs):
    n = len(ws)
    rows, cols = ws[0].shape
    tr = _row_tile(rows)

    def body(*refs):
        ins, outs = refs[:5 * n], refs[5 * n:]
        for a in range(n):
            w_ref, m_ref, v_ref, mine_ref, theirs_ref = ins[a::n]
            g_ref, d_ref, nm_ref, nv_ref = outs[4 * a:4 * a + 4]
            g = mine_ref[...] + theirs_ref[...]
            g_ref[...] = g
            d_ref[...], nm_ref[...], nv_ref[...] = _adamw_math(w_ref[...], m_ref[...], v_ref[...], g)

    spec = pl.BlockSpec((tr, cols), lambda i: (i, 0))
    out = jax.ShapeDtypeStruct((rows, cols), F32)
    outs = pl.pallas_call(
        body, name="adamw_pairs", grid=(rows // tr,),
        in_specs=[spec] * (5 * n), out_specs=[spec] * (4 * n), out_shape=[out] * (4 * n),
        compiler_params=_params(("arbitrary",)),
    )(*ws, *ms, *vs, *mines, *theirs)
    return [outs[4 * a:4 * a + 4] for a in range(n)]


def _adamw_halves(w, m, v, mine, theirs, core):
    rows, cols = w.shape
    tr = _row_tile(rows // 2)
    per_half = rows // 2 // tr

    def body(core_ref, w_ref, m_ref, v_ref, mine_ref, theirs_ref, g_ref, d_ref, nm_ref, nv_ref):
        is_mine = pl.program_id(0) // per_half == core_ref[0]
        for part, cond in ((mine_ref, is_mine), (theirs_ref, jnp.logical_not(is_mine))):
            @pl.when(cond)
            def _(part=part):
                g = part[...]
                g_ref[...] = g
                d_ref[...], nm_ref[...], nv_ref[...] = _adamw_math(w_ref[...], m_ref[...], v_ref[...], g)

    spec = pl.BlockSpec((tr, cols), lambda i, core_ref: (i, 0))

    def half_spec(own):
        def index(i, core_ref):
            in_core_half = i // per_half == core_ref[0]
            here = in_core_half if own else jnp.logical_not(in_core_half)
            return jnp.where(here, i % per_half, 0), 0
        return pl.BlockSpec((tr, cols), index)

    out = jax.ShapeDtypeStruct(w.shape, F32)
    return pl.pallas_call(
        body, name="adamw_halves",
        grid_spec=pltpu.PrefetchScalarGridSpec(
            num_scalar_prefetch=1, grid=(rows // tr,),
            in_specs=[spec] * 3 + [half_spec(True), half_spec(False)], out_specs=[spec] * 4),
        out_shape=[out] * 4,
        compiler_params=_params(("arbitrary",)),
    )(core.reshape(1).astype(jnp.int32), w, m, v, mine, theirs)


def _adamw_small(w, m, v, slots_head, slots_tail):
    n_dev, p0, _ = slots_head.shape

    def body(w_ref, m_ref, v_ref, head_ref, tail_ref, g_ref, d_ref, nm_ref, nv_ref):
        for ref, rows in ((head_ref, slice(0, p0)), (tail_ref, slice(p0, w.shape[0]))):
            g = ref[0]
            for i in range(1, n_dev):
                g = g + ref[i]
            g_ref[rows, :] = g
            d_ref[rows, :], nm_ref[rows, :], nv_ref[rows, :] = _adamw_math(w_ref[rows, :], m_ref[rows, :], v_ref[rows, :], g)

    vmem = pl.BlockSpec(memory_space=pltpu.VMEM)
    out = jax.ShapeDtypeStruct(w.shape, F32)
    return pl.pallas_call(
        body, name="adamw_small", in_specs=[vmem] * 5, out_specs=[vmem] * 4, out_shape=[out] * 4,
        compiler_params=pltpu.CompilerParams(vmem_limit_bytes=VMEM_LIMIT),
    )(w, m, v, slots_head, slots_tail)


ANY = pl.BlockSpec(memory_space=pl.ANY)


def _place():
    x, y, c = lax.axis_index("x"), lax.axis_index("y"), lax.axis_index("c")
    other_chips = [(1 - x, y), (x, 1 - y), (1 - x, 1 - y)]
    return x, y, c, other_chips


def _swap_and_gather(arrs, pack):
    n = len(arrs)
    n_dev = 2 * N_CHIP
    flips = [(dx, dy, dc) for dx in (0, 1) for dy in (0, 1) for dc in (0, 1)][1:]

    def body(*refs):
        ins, pack_ref = refs[:n], refs[n]
        outs, slots_ref = refs[n + 1:2 * n + 1], refs[2 * n + 1]
        send_sems, recv_sems, pack_send, pack_recv, own_sem = refs[2 * n + 2:]
        x, y, c, _ = _place()
        me = 4 * x + 2 * y + c
        own = pltpu.make_async_copy(pack_ref, slots_ref.at[me], own_sem)
        own.start()
        copies = []
        for k, (dx, dy, dc) in enumerate(flips):
            peer = (1 - x if dx else x, 1 - y if dy else y, 1 - c if dc else c)
            copies.append(pltpu.make_async_remote_copy(
                src_ref=pack_ref, dst_ref=slots_ref.at[me], send_sem=pack_send.at[k], recv_sem=pack_recv.at[k],
                device_id=peer, device_id_type=MESH))
        copies += [pltpu.make_async_remote_copy(
            src_ref=ins[a], dst_ref=outs[a], send_sem=send_sems.at[a], recv_sem=recv_sems.at[a],
            device_id=(x, y, 1 - c), device_id_type=MESH) for a in range(n)]
        for cp in copies:
            cp.start()
        for cp in copies:
            cp.wait()
        own.wait()

    return pl.pallas_call(
        body, name="swap_and_gather",
        in_specs=[ANY] * (n + 1), out_specs=[ANY] * (n + 1),
        out_shape=[jax.ShapeDtypeStruct(a.shape, a.dtype) for a in arrs] + [jax.ShapeDtypeStruct((n_dev,) + pack.shape, F32)],
        scratch_shapes=[pltpu.SemaphoreType.DMA((n,)), pltpu.SemaphoreType.DMA((n,)),
                        pltpu.SemaphoreType.DMA((n_dev - 1,)), pltpu.SemaphoreType.DMA((n_dev - 1,)),
                        pltpu.SemaphoreType.DMA],
    )(*arrs, pack)


SLAB = 8


def _slab(vec, d):
    return jnp.pad(vec.reshape(1, d), ((0, SLAB - 1), (0, 0)))


def _pack_tail(vec_nv, b_s, vec_nf, w_s, scalar=None):
    d = vec_nv.shape[-1]
    extra = jnp.zeros((1, d), F32) if scalar is None else jnp.pad(scalar, ((0, 0), (0, d - 1)))
    return jnp.concatenate([_slab(v, d) for v in (vec_nv, b_s, vec_nf, extra)] + [w_s.reshape(-1, d)], axis=0)


def _pack_small(vec_nin, vec_nv, b_s, vec_nf, w_s):
    return jnp.concatenate([_slab(vec_nin, vec_nin.shape[-1]), _pack_tail(vec_nv, b_s, vec_nf, w_s)], axis=0)


def _unpack_small(pack, w_s_shape, b_s_shape):
    return (pack[0:1], pack[SLAB:SLAB + 1], pack[2 * SLAB].reshape(b_s_shape), pack[3 * SLAB],
            pack[5 * SLAB:].reshape(w_s_shape), pack[4 * SLAB, 0])


def kernel(x, norm_in, w_in, norm_v, w_s, b_s, w_o_gmlp, w_o_sb, w_out, norm_final, loss_target, m_norm_in, m_w_in, m_norm_v, m_w_s, m_b_s, m_w_o_gmlp, m_w_o_sb, m_w_out, m_norm_final, v_norm_in, v_w_in, v_norm_v, v_w_s, v_b_s, v_w_o_gmlp, v_w_o_sb, v_w_out, v_norm_final):
    d = x.shape[-1]
    ncol = w_in.shape[-1]
    nrow = w_o_gmlp.shape[-2]
    chip = 2 * lax.axis_index("x") + lax.axis_index("y")

    bsz, seq, _ = x.shape
    x2d = x.reshape(bsz * seq, d)
    shards = [w_in[0], w_o_gmlp[0], w_o_sb[0], w_out[0]]
    halves = [s16.reshape(2, s16.shape[0] // 2, s16.shape[1]) for s16 in _cast_bf16(shards)]
    proj, qkv, h_t, (wg, w_og, w_osb, w_o) = _in_proj_fwd(x2d, norm_in, halves)
    wg = wg.reshape(N_CHIP, d, ncol)

    loss, pieces, dx2, g_nv, g_ws, g_bs, g_wog, g_wosb, g_wout, g_nf = _local_step(
        proj, qkv, x2d, loss_target.reshape(bsz * seq, d), bsz, seq, norm_v, w_s[0], b_s[0],
        w_og.reshape(d, d), w_osb.reshape(d, d), w_o.reshape(d, d), norm_final)

    mats = [g_wog, g_wosb, g_wout]
    mats16 = [g16.reshape(N_CHIP, nrow, d) for g16 in _cast_bf16(mats)]
    g_win, from_sibling, recv_mats, slots_tail = _in_proj_bwd_w(
        h_t, pieces, mats16, _pack_tail(g_nv, g_bs, g_nf, g_ws, loss))
    core = lax.axis_index("c")
    half_own, half_win16 = _chip_half(g_win, from_sibling, core, chip, ncol)
    grad_x, g_nin, recv_win = _in_proj_bwd_x(pieces, wg, x2d, norm_in, dx2, half_win16)
    grad_x = grad_x.reshape(bsz, seq, d)

    sums = _add_received([half_own], [recv_win], jnp.zeros((), jnp.int32), True) + _add_received(
        mats, recv_mats, chip, False)
    *sibling_sums, slots_head = _swap_and_gather(sums, _slab(g_nin, d))
    stats = [_adamw_halves(w_in[0], m_w_in[0], v_w_in[0], sums[0], sibling_sums[0], core)] + _adamw_pairs(
        shards[1:], [m_w_o_gmlp[0], m_w_o_sb[0], m_w_out[0]], [v_w_o_gmlp[0], v_w_o_sb[0], v_w_out[0]],
        sums[1:], sibling_sums[1:])
    (gw_in, dw_in, nm_in, nv_in), (gw_og, dw_og, nm_og, nv_og), (gw_osb, dw_osb, nm_osb, nv_osb), \
        (gw_out, dw_out, nm_out, nv_out) = stats

    gs, ds, ms, vs = _adamw_small(
        _pack_small(norm_in, norm_v, b_s[0], norm_final, w_s[0]),
        _pack_small(m_norm_in, m_norm_v, m_b_s[0], m_norm_final, m_w_s[0]),
        _pack_small(v_norm_in, v_norm_v, v_b_s[0], v_norm_final, v_w_s[0]), slots_head, slots_tail)

    def small(pack):
        nin, nv, bs, nf, ws, _ = _unpack_small(pack, w_s.shape, b_s.shape)
        return nin, nv, ws, bs, nf

    loss = _unpack_small(gs, w_s.shape, b_s.shape)[-1]
    out = []
    for small_pack, win, wog, wosb, wout in ((gs, gw_in, gw_og, gw_osb, gw_out), (ds, dw_in, dw_og, dw_osb, dw_out),
                                             (ms, nm_in, nm_og, nm_osb, nm_out), (vs, nv_in, nv_og, nv_osb, nv_out)):
        nin, nv, ws, bs, nf = small(small_pack)
        out += [nin, win[None], nv, ws, bs, wog[None], wosb[None], wout[None], nf]
    return (loss, grad_x, *out)
```

```python
import functools
import math

import jax
import jax.numpy as jnp
from jax import lax
from jax.experimental import pallas as pl
from jax.experimental.pallas import tpu as pltpu

F32 = jnp.float32
BF16 = jnp.bfloat16
EPS = 1e-6
HEADS = 8
N_SPLIT = 9
N_CHIP = 4
MESH = pl.DeviceIdType.MESH

ADAM_LR = 0.001
ADAM_B1 = 0.9
ADAM_B2 = 0.999
ADAM_EPS = 1e-08
ADAM_WD = 0.01
ADAM_STEP = 10

VMEM_LIMIT = 56 * 2 ** 20
TM = 256
TMF = 512
TMX = 512
ATT_T = 256
ATT_HP = 4
TKW = 1024
CHUNKS_PER_STEP = 4
CAST_STEPS = 8

NT = (((1,), (1,)), ((), ()))
TN = (((0,), (0,)), ((), ()))


def _params(sem):
    return pltpu.CompilerParams(dimension_semantics=sem, vmem_limit_bytes=VMEM_LIMIT)


def _resident(shape):
    nd = len(shape)
    return pl.BlockSpec(shape, lambda *_: (0,) * nd, pipeline_mode=pl.Buffered(1))


def _const(shape):
    nd = len(shape)
    return pl.BlockSpec(shape, lambda *_: (0,) * nd)


def _segments(d, ncol):
    segs = []
    edges = sorted({j * ncol for j in range(N_CHIP + 1)} | {s * d for s in range(N_SPLIT + 1)})
    for lo, hi in zip(edges[:-1], edges[1:]):
        segs.append((lo // ncol, lo % ncol, lo // d, lo % d, hi - lo))
    return segs


def _sigmoid(x):
    return 0.5 * jnp.tanh(0.5 * x) + 0.5


_GELU_C = math.sqrt(2.0 / math.pi)


_GELU_CA = _GELU_C * 0.044715


def _gelu(x):
    return x * (0.5 * jnp.tanh(x * (_GELU_C + _GELU_CA * (x * x))) + 0.5)


def _gelu_and_grad(x):
    x2 = x * x
    u = 0.5 * jnp.tanh(x * (_GELU_C + _GELU_CA * x2)) + 0.5
    slope = (1.0 - u) * (x * (_GELU_C + (3.0 * _GELU_CA) * x2))
    return x * u, u * (2.0 * slope + 1.0)


def _split_bf16(a):
    hi = a.astype(BF16)
    lo = (a - hi.astype(F32)).astype(BF16)
    return hi, lo


def _in_proj_fwd(x2d, g_in, halves):
    t, d = x2d.shape
    n = len(halves)
    ncol = halves[0].shape[2]
    n_row = t // TMF
    last = n_row - 1
    assert halves[0].shape[1] * 2 == d
    qkv_parts = {j: (max(j * ncol, 3 * d) - j * ncol, max(j * ncol, 3 * d) - 3 * d)
                 for j in range(N_CHIP) if min((j + 1) * ncol, 6 * d) > max(j * ncol, 3 * d)}
    qkv_w = 3 * d // len(qkv_parts)
    assert all(min((j + 1) * ncol, 6 * d) - max(j * ncol, 3 * d) == qkv_w and q0 % qkv_w == 0
               for j, (_, q0) in qkv_parts.items())

    def body(order_ref, qrow_ref, qcol_ref, x_ref, g_ref, *rest):
        ins = rest[:n]
        proj_ref, qkv_ref, ht_ref = rest[n:n + 3]
        outs = rest[n + 3:2 * n + 3]
        wbuf, h_all, send_sems, recv_sems, local_sems, load_sem = rest[2 * n + 3:]
        phase = pl.program_id(0)
        i = pl.program_id(1)
        x_pos, y_pos, c_pos, chips = _place()
        sibling = (x_pos, y_pos, 1 - c_pos)
        me = (x_pos, y_pos, c_pos)
        my_chip = 2 * x_pos + y_pos

        def copy(a, k, block, to, src=None):
            return pltpu.make_async_remote_copy(
                src_ref=outs[a].at[block] if src is None else src, dst_ref=outs[a].at[block],
                send_sem=send_sems.at[a, k], recv_sem=recv_sems.at[a, k], device_id=to, device_id_type=MESH)

        def local(a):
            return pltpu.make_async_copy(ins[a], outs[a].at[pl.ds(2 * my_chip, 2)], local_sems.at[a])

        def load(src, first, slot):
            for half in range(2):
                cp = pltpu.make_async_copy(src.at[first + half], wbuf.at[slot, pl.ds(half * (d // 2), d // 2)], load_sem)
                cp.start()
                cp.wait()

        def relay(a, k, block, piece, to):
            rows = halves[a].shape[1] // 2
            ref = outs[a].at[block, pl.ds(piece * rows, rows)]
            return pltpu.make_async_remote_copy(
                src_ref=ref, dst_ref=ref, send_sem=send_sems.at[a, k], recv_sem=recv_sems.at[a, k],
                device_id=to, device_id_type=MESH)

        x_nbr, y_nbr, diagonal = chips
        first_block = lambda chip_xy: 2 * (2 * chip_xy[0] + chip_xy[1])

        @pl.when((phase == 0) & (i == 0))
        def _():
            for a in range(n):
                local(a).start()
            for k, (px, py) in enumerate((x_nbr, y_nbr)):
                for a in range(n):
                    copy(a, k, 2 * my_chip + c_pos, (px, py, c_pos), src=ins[a].at[c_pos]).start()
            load(ins[0], 0, 0)

        @pl.when((phase == 1) & (i == 0))
        def _():
            from_x, from_y = first_block(x_nbr) + c_pos, first_block(y_nbr) + c_pos
            for a in range(n):
                copy(a, 0, from_x, me).wait_recv()
                copy(a, 1, from_y, me).wait_recv()
                relay(a, 2, from_x, 0, (*y_nbr, c_pos)).start()
                relay(a, 3, from_y, 1, (*x_nbr, c_pos)).start()
                copy(a, 4, from_x, sibling).start()
                copy(a, 5, from_y, sibling).start()
            for a in range(n):
                copy(a, 4, first_block(x_nbr) + 1 - c_pos, me).wait_recv()
            load(outs[0], first_block(x_nbr), 1)

        @pl.when((phase == 2) & (i == 0))
        def _():
            for a in range(n):
                copy(a, 5, first_block(y_nbr) + 1 - c_pos, me).wait_recv()
            load(outs[0], first_block(y_nbr), 0)

        @pl.when((phase == 3) & (i == 0))
        def _():
            from_diagonal = first_block(diagonal) + c_pos
            for a in range(n):
                relay(a, 2, from_diagonal, 0, me).wait_recv()
                relay(a, 3, from_diagonal, 1, me).wait_recv()
                copy(a, 6, from_diagonal, sibling).start()
            for a in range(n):
                copy(a, 6, first_block(diagonal) + 1 - c_pos, me).wait_recv()
            load(outs[0], first_block(diagonal), 1)

        @pl.when(phase == 0)
        def _():
            x = x_ref[...]
            r = lax.rsqrt(jnp.mean(x * x, axis=-1, keepdims=True) + EPS)
            hf = x * r * g_ref[...]
            h_all[i] = hf.astype(BF16)
            ht_ref[...] = hf.T.astype(BF16)

        for slot in range(2):
            @pl.when(phase % 2 == slot)
            def _(slot=slot):
                proj_ref[...] = jnp.dot(h_all[i], wbuf[slot], preferred_element_type=F32)

        for chunk, (c0, _) in qkv_parts.items():
            @pl.when(order_ref[phase] == chunk)
            def _(c0=c0):
                qkv_ref[...] = proj_ref[:, c0:c0 + qkv_w].astype(BF16)

        @pl.when((phase == N_CHIP - 1) & (i == n_row - 1))
        def _():
            for a in range(n):
                for k in (0, 1, 4, 5, 6):
                    copy(a, k, 0, me).wait_send()
                for k in (2, 3):
                    relay(a, k, 0, 0, me).wait_send()
                local(a).wait()

    x_pos, y_pos = lax.axis_index("x"), lax.axis_index("y")
    order = jnp.stack([2 * x_pos + y_pos, 2 * (1 - x_pos) + y_pos, 2 * x_pos + 1 - y_pos,
                       2 * (1 - x_pos) + 1 - y_pos]).astype(jnp.int32)
    holds = [functools.reduce(jnp.logical_or, [order[p] == j for j in qkv_parts]) for p in range(N_CHIP)]
    col = [sum(jnp.where(order[p] == j, q0 // qkv_w, 0) for j, (_, q0) in qkv_parts.items()) for p in range(N_CHIP)]
    cur = col[-1]
    for p in reversed(range(N_CHIP - 1)):
        cur = jnp.where(holds[p], col[p], cur)
    seen = jnp.bool_(False)
    qrow, qcol = [], []
    for p in range(N_CHIP):
        cur = jnp.where(holds[p], col[p], cur)
        qrow.append(jnp.where(holds[p], -1, jnp.where(seen, last, 0)))
        qcol.append(cur)
        seen = seen | holds[p]
    qrow = jnp.stack(qrow).astype(jnp.int32)
    qcol = jnp.stack(qcol).astype(jnp.int32)

    outs = pl.pallas_call(
        body, name="in_proj_fwd",
        grid_spec=pltpu.PrefetchScalarGridSpec(
            num_scalar_prefetch=3, grid=(N_CHIP, n_row),
            in_specs=[pl.BlockSpec((TMF, d), lambda p, i, order, qrow, qcol: (jnp.where(p == 0, i, last), 0)),
                      pl.BlockSpec((1, d), lambda p, i, order, qrow, qcol: (0, 0))] + [ANY] * n,
            out_specs=[pl.BlockSpec((TMF, ncol), lambda p, i, order, qrow, qcol: (i, order[p])),
                       pl.BlockSpec((TMF, qkv_w),
                                    lambda p, i, order, qrow, qcol: (jnp.where(qrow[p] < 0, i, qrow[p]), qcol[p])),
                       pl.BlockSpec((d, TMF), lambda p, i, order, qrow, qcol: (0, jnp.where(p == 0, i, last)))]
            + [ANY] * n,
            scratch_shapes=[pltpu.VMEM((2, d, ncol), BF16), pltpu.VMEM((n_row, TMF, d), BF16),
                            pltpu.SemaphoreType.DMA((n, 7)), pltpu.SemaphoreType.DMA((n, 7)),
                            pltpu.SemaphoreType.DMA((n,)), pltpu.SemaphoreType.DMA]),
        out_shape=[jax.ShapeDtypeStruct((t, N_CHIP * ncol), F32), jax.ShapeDtypeStruct((t, 3 * d), BF16),
                   jax.ShapeDtypeStruct((d, t), BF16)]
        + [jax.ShapeDtypeStruct((2 * N_CHIP,) + hv.shape[1:], BF16) for hv in halves],
        compiler_params=_params(("arbitrary", "arbitrary")),
    )(order, qrow, qcol, x2d, g_in, *halves)
    return outs[0], outs[1], outs[2], outs[3:]


def _branch_a_fwd(a_pre, g_v, wm, b_t):
    t = a_pre.shape[0]
    d = g_v.shape[1]
    d3 = 3 * d
    ng, chunk, _ = wm.shape
    cw = d // ng

    per_step = CHUNKS_PER_STEP if t % (CHUNKS_PER_STEP * chunk) == 0 else 1

    def body(a_ref, gv_ref, wm_ref, bt_ref, ya_ref):
        for n in range(per_step):
            rows = slice(n * chunk, (n + 1) * chunk)
            ua = _gelu(a_ref[rows, 0:d])
            vg = _gelu(a_ref[rows, d:2 * d])
            za = a_ref[rows, 2 * d:3 * d]
            rv = lax.rsqrt(jnp.mean(vg * vg, axis=-1, keepdims=True) + EPS)
            va = (vg * rv * gv_ref[...]).astype(BF16)
            gate = ua * (za * _sigmoid(za))
            for g in range(ng):
                sl = slice(g * cw, (g + 1) * cw)
                mixed = jnp.dot(wm_ref[g], va[:, sl], preferred_element_type=F32) + bt_ref[:, g:g + 1]
                ya_ref[rows, sl] = (gate[:, sl] * mixed).astype(BF16)

    tile = per_step * chunk
    return pl.pallas_call(
        body, name="branch_a_fwd",
        grid=(t // tile,),
        in_specs=[pl.BlockSpec((tile, d3), lambda i: (i, 0)), _const((1, d)), _const(wm.shape), _const(b_t.shape)],
        out_specs=pl.BlockSpec((tile, d), lambda i: (i, 0)),
        out_shape=jax.ShapeDtypeStruct((t, d), BF16),
        compiler_params=_params(("arbitrary",)),
    )(a_pre, g_v, wm, b_t)


def _branch_a_bwd(a_pre, dya, g_v, wm, wm_t, b_t):
    t = a_pre.shape[0]
    d = g_v.shape[1]
    d3 = 3 * d
    ng, chunk, _ = wm.shape
    cw = d // ng
    nsteps = t // chunk

    def body(a_ref, dya_ref, gv_ref, wm_ref, wmt_ref, bt_ref, da_ref, gws_ref, gbt_ref, gnv_ref, db_acc):
        i = pl.program_id(0)

        @pl.when(i == 0)
        def _():
            gws_ref[...] = jnp.zeros_like(gws_ref)
            gnv_ref[...] = jnp.zeros_like(gnv_ref)
            db_acc[...] = jnp.zeros_like(db_acc)

        ua, dgelu_u = _gelu_and_grad(a_ref[:, 0:d])
        vg, dgelu_v = _gelu_and_grad(a_ref[:, d:2 * d])
        za = a_ref[:, 2 * d:3 * d]
        sig = _sigmoid(za)
        sz = za * sig
        dsz = sig * (1.0 + za * (1.0 - sig))
        rv = lax.rsqrt(jnp.mean(vg * vg, axis=-1, keepdims=True) + EPS)
        nv = vg * rv
        gv = gv_ref[...]
        va = (nv * gv).astype(BF16)
        dya = dya_ref[...]
        dmix = dya * ua * sz
        db_acc[...] += dmix
        dmix_b = dmix.astype(BF16)
        t_gate = dya * sz
        t_z = dya * ua * dsz
        dva_parts = []
        for g in range(ng):
            sl = slice(g * cw, (g + 1) * cw)
            mixed = jnp.dot(wm_ref[g], va[:, sl], preferred_element_type=F32) + bt_ref[:, g:g + 1]
            da_ref[:, sl] = (t_gate[:, sl] * mixed * dgelu_u[:, sl]).astype(BF16)
            da_ref[:, 2 * d + g * cw:2 * d + (g + 1) * cw] = (t_z[:, sl] * mixed).astype(BF16)
            gws_ref[g] += lax.dot_general(dmix_b[:, sl], va[:, sl], NT, preferred_element_type=F32)
            dva_parts.append(jnp.dot(wmt_ref[g], dmix_b[:, sl], preferred_element_type=F32))
        dva = jnp.concatenate(dva_parts, axis=1)
        gnv_ref[...] += jnp.sum(dva * nv, axis=0, keepdims=True)
        dnv = dva * gv
        dvg = rv * (dnv - nv * jnp.mean(dnv * nv, axis=-1, keepdims=True))
        da_ref[:, d:2 * d] = (dvg * dgelu_v).astype(BF16)

        @pl.when(i == nsteps - 1)
        def _():
            acc = db_acc[...]
            for g in range(ng):
                gbt_ref[:, g:g + 1] = jnp.sum(acc[:, g * cw:(g + 1) * cw], axis=1, keepdims=True)

    return pl.pallas_call(
        body, name="branch_a_bwd",
        grid=(nsteps,),
        in_specs=[pl.BlockSpec((chunk, d3), lambda i: (i, 0)), pl.BlockSpec((chunk, d), lambda i: (i, 0)),
                  _const((1, d)), _const(wm.shape), _const(wm_t.shape), _const(b_t.shape)],
        out_specs=[pl.BlockSpec((chunk, d3), lambda i: (i, 0)), _const(wm.shape), _const(b_t.shape), _const((1, d))],
        out_shape=[jax.ShapeDtypeStruct((t, d3), BF16), jax.ShapeDtypeStruct(wm.shape, F32),
                   jax.ShapeDtypeStruct(b_t.shape, F32), jax.ShapeDtypeStruct((1, d), F32)],
        scratch_shapes=[pltpu.VMEM((chunk, d), F32)],
        compiler_params=_params(("arbitrary",)),
    )(a_pre, dya, g_v, wm, wm_t, b_t)


def _below_diagonal(n):
    return lax.broadcasted_iota(jnp.int32, (n, n), 0) > lax.broadcasted_iota(jnp.int32, (n, n), 1)


def _twice(tri):
    t = tri.astype(BF16)
    return jnp.concatenate([t, t], axis=0)


def _cumsum_mm(a, tri2):
    hi, lo = _split_bf16(a)
    return jnp.dot(jnp.concatenate([hi, lo], axis=1), tri2, preferred_element_type=F32)


LOG2E = 1.4426950408889634
_SIGN = 0x80000000


def _sb_block(q, k, scale, upper2, causal):
    z2 = lax.dot_general(q, k, NT, preferred_element_type=F32) * (scale * LOG2E)
    neg_abs = lax.bitcast_convert_type(lax.bitcast_convert_type(z2, jnp.uint32) | jnp.uint32(_SIGN), F32)
    l2 = jnp.log(1.0 + jnp.exp2(neg_abs)) * LOG2E
    log_beta = jnp.minimum(z2, 0.0) - l2
    lom = log_beta - z2
    if causal is not None:
        lom = jnp.where(causal, lom, 0.0)
    sx = _cumsum_mm(lom, upper2)
    return log_beta, sx, sx[:, 0:1] + lom[:, 0:1]


DEAD_LOG2 = -150.0


def _max_carry(carries):
    return jnp.max(functools.reduce(jnp.maximum, carries))


ZB_GROUP, GA_GROUP, GB_GROUP = 6, 7, 8


def _attn_specs(d, seq, nq, heads_per_step):
    hp_w = heads_per_step * (d // HEADS)
    n_hp = d // hp_w
    row_blk = lambda group: pl.BlockSpec((ATT_T, hp_w), lambda b, h, i: (b * nq + i, group * n_hp + h))
    seq_blk = lambda group: pl.BlockSpec((seq, hp_w), lambda b, h, i: (b, group * n_hp + h))
    return row_blk, seq_blk, n_hp


def _attn_fwd(qkv, proj, bsz, seq):
    t, d3 = qkv.shape
    d = d3 // 3
    hd = d // HEADS
    nq = seq // ATT_T
    scale = hd ** -0.5
    n_heads = ATT_HP
    row_blk, seq_blk, n_hp = _attn_specs(d, seq, nq, n_heads)

    def body(q_ref, k_ref, v_ref, zb_ref, o_ref, yb_ref):
        i = pl.program_id(2)
        causal = _below_diagonal(ATT_T)
        upper2 = _twice(causal)

        def step(kb, state, mask):
            rows = pl.ds(pl.multiple_of(kb * ATT_T, ATT_T), ATT_T)
            heads = [slice(h * hd, (h + 1) * hd) for h in range(n_heads)]
            scores = [_sb_block(q_ref[:, cols], k_ref[rows, cols], scale, upper2, mask) for cols in heads]
            new = []
            for cols, (carry, acc), (log_beta, sx, total) in zip(heads, state, scores):
                a = jnp.exp2(log_beta + sx + carry)
                if mask is not None:
                    a = jnp.where(mask, a, 0.0)
                acc = acc + jnp.dot(a.astype(BF16), v_ref[rows, cols], preferred_element_type=F32)
                new.append((carry + total, acc))
            return tuple(new)

        init = tuple((jnp.zeros((ATT_T, 1), F32), jnp.zeros((ATT_T, hd), F32)) for _ in range(n_heads))
        state = step(i, init, causal)
        def more(c):
            new = step(c[0], c[1], None)
            return c[0] - 1, new, _max_carry([s[0] for s in new])

        _, state, _ = lax.while_loop(lambda c: (c[0] >= 0) & (c[2] > DEAD_LOG2), more,
                                     (i - 1, state, _max_carry([s[0] for s in state])))
        for h in range(n_heads):
            cols = slice(h * hd, (h + 1) * hd)
            acc = state[h][1]
            zb = zb_ref[:, cols]
            o_ref[:, cols] = acc
            yb_ref[:, cols] = (acc * (zb * _sigmoid(zb))).astype(BF16)

    return pl.pallas_call(
        body, name="attn_fwd",
        grid=(bsz, n_hp, nq),
        in_specs=[row_blk(0), seq_blk(1), seq_blk(2), row_blk(ZB_GROUP)],
        out_specs=[row_blk(0), row_blk(0)],
        out_shape=[jax.ShapeDtypeStruct((t, d), F32), jax.ShapeDtypeStruct((t, d), BF16)],
        compiler_params=_params(("arbitrary", "arbitrary", "arbitrary")),
    )(qkv, qkv, qkv, proj)


def _attn_bwd(qkv, proj, o, dyb, bsz, seq):
    t, d3 = qkv.shape
    d = d3 // 3
    hd = d // HEADS
    nq = seq // ATT_T
    scale = hd ** -0.5
    row_blk, seq_blk, n_hp = _attn_specs(d, seq, nq, ATT_HP)

    def body(q_ref, k_ref, v_ref, zb_ref, o_ref, dyb_ref, dq_ref, dk_ref, dv_ref, dzb_ref,
             g_s, beta_s, dkt_acc, dvt_acc):
        i = pl.program_id(2)

        @pl.when(i == 0)
        def _():
            dkt_acc[...] = jnp.zeros_like(dkt_acc)
            dvt_acc[...] = jnp.zeros_like(dvt_acc)

        causal = _below_diagonal(ATT_T)
        upper2 = _twice(causal)
        lower2 = _twice(~causal)
        zb = zb_ref[...]
        sig = _sigmoid(zb)
        dyb_t = dyb_ref[...]
        do_f = dyb_t * (zb * sig)
        do = do_f.astype(BF16)
        do_t = do_f.T.astype(BF16)
        q_t = q_ref[...].astype(F32).T.astype(BF16)
        dzb_ref[...] = (dyb_t * o_ref[...] * (sig * (1.0 + zb * (1.0 - sig)))).astype(BF16)

        def sweep(kb, carries, mask):
            rows = pl.ds(pl.multiple_of(kb * ATT_T, ATT_T), ATT_T)
            heads = [slice(h * hd, (h + 1) * hd) for h in range(ATT_HP)]
            scores = [_sb_block(q_ref[:, cols], k_ref[rows, cols], scale, upper2, mask) for cols in heads]
            das = [lax.dot_general(do[:, cols], v_ref[rows, cols], NT, preferred_element_type=F32) for cols in heads]
            new = []
            for h, (cols, carry, (log_beta, sx, total), da) in enumerate(zip(heads, carries, scores, das)):
                a = jnp.exp2(log_beta + sx + carry)
                beta = jnp.exp2(log_beta)
                if mask is not None:
                    a = jnp.where(mask, a, 0.0)
                    beta = jnp.where(mask, beta, 0.0)
                g_s[h, kb] = a * da
                beta_s[h, kb] = beta
                dvt_acc[kb, cols, :] += jnp.dot(do_t[cols, :], a.astype(BF16), preferred_element_type=F32)
                new.append(carry + total)
            return tuple(new)

        carries = sweep(i, tuple(jnp.zeros((ATT_T, 1), F32) for _ in range(ATT_HP)), causal)

        def more(c):
            new = sweep(c[0], c[1], None)
            return c[0] - 1, new, _max_carry(new)

        last, _, _ = lax.while_loop(lambda c: (c[0] >= 0) & (c[2] > DEAD_LOG2), more, (i - 1, carries, _max_carry(carries)))
        first_kb = last + 1

        def back(kb, state):
            rows = pl.ds(pl.multiple_of(kb * ATT_T, ATT_T), ATT_T)
            heads = [slice(h * hd, (h + 1) * hd) for h in range(ATT_HP)]
            sums = [_cumsum_mm(g_s[h, kb], lower2) for h in range(ATT_HP)]
            new = []
            for h, (cols, (p_carry, dq), px) in enumerate(zip(heads, state, sums)):
                dz = ((g_s[h, kb] - (p_carry + px) * beta_s[h, kb]) * scale).astype(BF16)
                dq = dq + jnp.dot(dz, k_ref[rows, cols], preferred_element_type=F32)
                dkt_acc[kb, cols, :] += jnp.dot(q_t[cols, :], dz, preferred_element_type=F32)
                new.append((p_carry + px[:, ATT_T - 1:ATT_T], dq))
            return tuple(new)

        init = tuple((jnp.zeros((ATT_T, 1), F32), jnp.zeros((ATT_T, hd), F32)) for _ in range(ATT_HP))
        state = lax.fori_loop(first_kb, i + 1, back, init)
        for h in range(ATT_HP):
            dq_ref[:, h * hd:(h + 1) * hd] = state[h][1].astype(BF16)

        @pl.when(i == nq - 1)
        def _():
            for kb in range(nq):
                dk_ref[kb * ATT_T:(kb + 1) * ATT_T, :] = dkt_acc[kb].T.astype(BF16)
                dv_ref[kb * ATT_T:(kb + 1) * ATT_T, :] = dvt_acc[kb].T.astype(BF16)

    out = jax.ShapeDtypeStruct((t, d), BF16)
    hp_w = ATT_HP * hd
    return pl.pallas_call(
        body, name="attn_bwd",
        grid=(bsz, n_hp, nq),
        in_specs=[row_blk(0), seq_blk(1), seq_blk(2), row_blk(ZB_GROUP), row_blk(0), row_blk(0)],
        out_specs=[row_blk(0), seq_blk(0), seq_blk(0), row_blk(0)],
        out_shape=[out, out, out, out],
        scratch_shapes=[pltpu.VMEM((ATT_HP, nq, ATT_T, ATT_T), F32), pltpu.VMEM((ATT_HP, nq, ATT_T, ATT_T), F32),
                        pltpu.VMEM((nq, hp_w, ATT_T), F32), pltpu.VMEM((nq, hp_w, ATT_T), F32)],
        compiler_params=_params(("arbitrary", "arbitrary", "arbitrary")),
    )(qkv, qkv, qkv, proj, o, dyb)


def _out_proj(ya, yb, g_pre, x2d, tgt, w_og, w_osb, w_out, g_f):
    t, d = x2d.shape

    def body(ya_ref, yb_ref, ga_ref, gb_ref, x_ref, tgt_ref, wog_ref, wosb_ref, wout_ref, gf_ref,
             dya_ref, dyb_ref, dg_ref, dx2_ref, loss_ref, gnf_ref, gwog_ref, gwosb_ref, gwout_ref):
        @pl.when(pl.program_id(0) == 0)
        def _():
            loss_ref[...] = jnp.zeros_like(loss_ref)
            gnf_ref[...] = jnp.zeros_like(gnf_ref)
            gwog_ref[...] = jnp.zeros_like(gwog_ref)
            gwosb_ref[...] = jnp.zeros_like(gwosb_ref)
            gwout_ref[...] = jnp.zeros_like(gwout_ref)

        ya = ya_ref[...]
        yb = yb_ref[...]
        pa = jnp.dot(ya, wog_ref[...], preferred_element_type=F32)
        pb = jnp.dot(yb, wosb_ref[...], preferred_element_type=F32)
        sga = _sigmoid(ga_ref[...])
        sgb = _sigmoid(gb_ref[...])
        merged = (sga * pa + sgb * pb).astype(BF16)
        x2 = x_ref[...] + jnp.dot(merged, wout_ref[...], preferred_element_type=F32)
        r2 = lax.rsqrt(jnp.mean(x2 * x2, axis=-1, keepdims=True) + EPS)
        n2 = x2 * r2
        gf = gf_ref[...]
        err = n2 * gf - tgt_ref[...]
        loss_ref[...] += 0.5 * jnp.sum(jnp.sum(err * err, axis=-1, keepdims=True), axis=0, keepdims=True) / d
        dy = err * (1.0 / d)
        gnf_ref[...] += jnp.sum(dy * n2, axis=0, keepdims=True)
        dn = dy * gf
        dx2 = r2 * (dn - n2 * jnp.mean(dn * n2, axis=-1, keepdims=True))
        dx2_ref[...] = dx2
        dx2_b = dx2.astype(BF16)
        dmerged = lax.dot_general(dx2_b, wout_ref[...], NT, preferred_element_type=F32)
        gwout_ref[...] += lax.dot_general(merged, dx2_b, TN, preferred_element_type=F32)
        dg_ref[:, 0:d] = (dmerged * pa * (sga * (1.0 - sga))).astype(BF16)
        dg_ref[:, d:2 * d] = (dmerged * pb * (sgb * (1.0 - sgb))).astype(BF16)
        dpa = (dmerged * sga).astype(BF16)
        dpb = (dmerged * sgb).astype(BF16)
        dya_ref[...] = lax.dot_general(dpa, wog_ref[...], NT, preferred_element_type=F32)
        dyb_ref[...] = lax.dot_general(dpb, wosb_ref[...], NT, preferred_element_type=F32)
        gwog_ref[...] += lax.dot_general(ya, dpa, TN, preferred_element_type=F32)
        gwosb_ref[...] += lax.dot_general(yb, dpb, TN, preferred_element_type=F32)

    row = lambda i: (i, 0)
    return pl.pallas_call(
        body, name="out_proj",
        grid=(t // TM,),
        in_specs=[pl.BlockSpec((TM, d), row), pl.BlockSpec((TM, d), row),
                  pl.BlockSpec((TM, d), lambda i: (i, GA_GROUP)), pl.BlockSpec((TM, d), lambda i: (i, GB_GROUP)),
                  pl.BlockSpec((TM, d), row), pl.BlockSpec((TM, d), row),
                  _resident((d, d)), _resident((d, d)), _resident((d, d)), _const((1, d))],
        out_specs=[pl.BlockSpec((TM, d), row), pl.BlockSpec((TM, d), row), pl.BlockSpec((TM, 2 * d), row),
                   pl.BlockSpec((TM, d), row), _const((1, 1)), _const((1, d)),
                   _const((d, d)), _const((d, d)), _const((d, d))],
        out_shape=[jax.ShapeDtypeStruct((t, d), F32), jax.ShapeDtypeStruct((t, d), F32),
                   jax.ShapeDtypeStruct((t, 2 * d), BF16), jax.ShapeDtypeStruct((t, d), F32),
                   jax.ShapeDtypeStruct((1, 1), F32), jax.ShapeDtypeStruct((1, d), F32),
                   jax.ShapeDtypeStruct((d, d), F32), jax.ShapeDtypeStruct((d, d), F32),
                   jax.ShapeDtypeStruct((d, d), F32)],
        compiler_params=_params(("arbitrary",)),
    )(ya, yb, g_pre, g_pre, x2d, tgt, w_og, w_osb, w_out, g_f)


def _dproj_pieces(d):
    return [(0, 0, 3), (1, 3, 1), (2, 4, 1), (3, 5, 1), (4, 6, 1), (5, 7, 2)]


def _in_proj_bwd_x(pieces, wg, x2d, g_in, dx2, gw16):
    t, d = x2d.shape
    ncol = wg.shape[2]
    segs = _segments(d, ncol)
    layout = _dproj_pieces(d)
    nsteps = t // TMX

    def body(da_ref, dq_ref, dk_ref, dv_ref, dzb_ref, dg_ref, w_ref, x_ref, g_ref, dx2_ref, gw16_ref,
             gx_ref, gn_ref, recv_ref, send_sems, recv_sems):
        x_pos, y_pos, c_pos, chips = _place()

        def share(k, chunk):
            px, py = chips[k]
            return pltpu.make_async_remote_copy(
                src_ref=gw16_ref.at[:, chunk * ncol:(chunk + 1) * ncol], dst_ref=recv_ref.at[k],
                send_sem=send_sems.at[k], recv_sem=recv_sems.at[k], device_id=(px, py, c_pos), device_id_type=MESH)

        @pl.when(pl.program_id(0) == 0)
        def _():
            gn_ref[...] = jnp.zeros_like(gn_ref)
            for k, (px, py) in enumerate(chips):
                for chunk in range(N_CHIP):
                    @pl.when(2 * px + py == chunk)
                    def _(k=k, chunk=chunk):
                        share(k, chunk).start()

        @pl.when(pl.program_id(0) == nsteps - 1)
        def _():
            for k in range(N_CHIP - 1):
                share(k, 0).wait()

        refs = (da_ref, dq_ref, dk_ref, dv_ref, dzb_ref, dg_ref)
        dh = jnp.zeros((TMX, d), F32)
        for chip, c0, grp, s0, width in segs:
            piece, first, _ = next(p for p in layout if p[1] <= grp < p[1] + p[2])
            off = (grp - first) * d + s0
            dh = dh + lax.dot_general(refs[piece][:, off:off + width], w_ref[chip, :, c0:c0 + width], NT,
                                      preferred_element_type=F32)
        x = x_ref[...]
        r = lax.rsqrt(jnp.mean(x * x, axis=-1, keepdims=True) + EPS)
        n = x * r
        gn_ref[...] += jnp.sum(dh * n, axis=0, keepdims=True)
        dn = dh * g_ref[...]
        gx_ref[...] = dx2_ref[...] + r * (dn - n * jnp.mean(dn * n, axis=-1, keepdims=True))

    row = lambda i: (i, 0)
    return pl.pallas_call(
        body, name="in_proj_bwd_x",
        grid=(nsteps,),
        in_specs=[pl.BlockSpec((TMX, p.shape[1]), row) for p in pieces]
        + [_resident(wg.shape), pl.BlockSpec((TMX, d), row), _const((1, d)), pl.BlockSpec((TMX, d), row), ANY],
        out_specs=[pl.BlockSpec((TMX, d), row), _const((1, d)), ANY],
        out_shape=[jax.ShapeDtypeStruct((t, d), F32), jax.ShapeDtypeStruct((1, d), F32),
                   jax.ShapeDtypeStruct((N_CHIP - 1, gw16.shape[0], ncol), BF16)],
        scratch_shapes=[pltpu.SemaphoreType.DMA((N_CHIP - 1,)), pltpu.SemaphoreType.DMA((N_CHIP - 1,))],
        compiler_params=_params(("arbitrary",)),
    )(*pieces, wg, x2d, g_in, dx2, gw16)


def _in_proj_bwd_w(h_t, pieces, mats16, pack):
    d, t = h_t.shape
    nk = t // TKW
    half = d // 2
    layout = _dproj_pieces(d)
    n_mats = len(mats16)
    n_dev = 2 * N_CHIP
    flips = [(dx, dy, dc) for dx in (0, 1) for dy in (0, 1) for dc in (0, 1)][1:]

    def body(ht_ref, da_ref, dq_ref, dk_ref, dv_ref, dzb_ref, dg_ref, *rest):
        mat_refs, pack_ref = rest[:n_mats], rest[n_mats]
        gw_ref, sib_ref = rest[n_mats + 1:n_mats + 3]
        recv_refs, slots_ref = rest[n_mats + 3:2 * n_mats + 3], rest[2 * n_mats + 3]
        acc, stage, mat_send, mat_recv, pack_send, pack_recv, own_sem, stage_send, sib_recv = rest[2 * n_mats + 4:]
        s = pl.program_id(0)
        i = pl.program_id(1)
        x_pos, y_pos, c_pos, chips = _place()
        me = 4 * x_pos + 2 * y_pos + c_pos

        def to_sibling(slot, group):
            return pltpu.make_async_remote_copy(
                src_ref=stage.at[slot], dst_ref=sib_ref.at[:, group * d:(group + 1) * d],
                send_sem=stage_send.at[slot], recv_sem=sib_recv, device_id=(x_pos, y_pos, 1 - c_pos), device_id_type=MESH)

        def exchanges():
            cps = []
            for k, (px, py) in enumerate(chips):
                for a in range(n_mats):
                    cps.append(pltpu.make_async_remote_copy(
                        src_ref=mat_refs[a].at[2 * px + py], dst_ref=recv_refs[a].at[k],
                        send_sem=mat_send.at[a, k], recv_sem=mat_recv.at[a, k],
                        device_id=(px, py, c_pos), device_id_type=MESH))
            for k, (dx, dy, dc) in enumerate(flips):
                peer = (1 - x_pos if dx else x_pos, 1 - y_pos if dy else y_pos, 1 - c_pos if dc else c_pos)
                cps.append(pltpu.make_async_remote_copy(
                    src_ref=pack_ref, dst_ref=slots_ref.at[me], send_sem=pack_send.at[k], recv_sem=pack_recv.at[k],
                    device_id=peer, device_id_type=MESH))
            return cps, pltpu.make_async_copy(pack_ref, slots_ref.at[me], own_sem)

        @pl.when((s == 0) & (i == 0))
        def _():
            cps, own = exchanges()
            own.start()
            for cp in cps:
                cp.start()

        @pl.when(i == 0)
        def _():
            acc[...] = jnp.zeros_like(acc)

        refs = (da_ref, dq_ref, dk_ref, dv_ref, dzb_ref, dg_ref)
        for piece, first, count in layout:
            @pl.when((s >= first) & (s < first + count))
            def _(piece=piece):
                acc[...] += jnp.dot(ht_ref[...], refs[piece][...], preferred_element_type=F32)

        @pl.when(i == nk - 1)
        def _():
            gw_ref[...] = acc[...]
            for slot in range(2):
                @pl.when(s % 2 == slot)
                def _(slot=slot):
                    @pl.when(s >= 2)
                    def _():
                        to_sibling(slot, 0).wait_send()
                    for other in range(2):
                        @pl.when(c_pos == 1 - other)
                        def _(other=other):
                            stage[slot] = acc[other * half:(other + 1) * half, :].astype(BF16)
                    for group in range(N_SPLIT):
                        @pl.when(s == group)
                        def _(group=group):
                            to_sibling(slot, group).start()

        @pl.when((s == N_SPLIT - 1) & (i == nk - 1))
        def _():
            for slot in range(2):
                to_sibling(slot, 0).wait_send()
            pltpu.make_async_remote_copy(
                src_ref=sib_ref, dst_ref=sib_ref, send_sem=stage_send.at[0], recv_sem=sib_recv,
                device_id=(x_pos, y_pos, c_pos), device_id_type=MESH).wait_recv()
            cps, own = exchanges()
            own.wait()
            for cp in cps:
                cp.wait()

    def piece_spec(p, first, count):
        def index(s, i):
            mine = (s >= first) & (s < first + count)
            return jnp.where(mine, i, 0), jnp.where(mine, s - first, 0)
        return pl.BlockSpec((TKW, d), index)

    col_blk = pl.BlockSpec((d, d), lambda s, i: (0, s))
    outs = pl.pallas_call(
        body, name="in_proj_bwd_w",
        grid=(N_SPLIT, nk),
        in_specs=[pl.BlockSpec((d, TKW), lambda s, i: (0, i))] + [piece_spec(*p) for p in layout] + [ANY] * (n_mats + 1),
        out_specs=[col_blk, ANY] + [ANY] * (n_mats + 1),
        out_shape=[jax.ShapeDtypeStruct((d, N_SPLIT * d), F32), jax.ShapeDtypeStruct((half, N_SPLIT * d), BF16)]
        + [jax.ShapeDtypeStruct((N_CHIP - 1,) + m.shape[1:], BF16) for m in mats16]
        + [jax.ShapeDtypeStruct((n_dev,) + pack.shape, F32)],
        scratch_shapes=[pltpu.VMEM((d, d), F32), pltpu.VMEM((2, half, d), BF16),
                        pltpu.SemaphoreType.DMA((n_mats, N_CHIP - 1)), pltpu.SemaphoreType.DMA((n_mats, N_CHIP - 1)),
                        pltpu.SemaphoreType.DMA((n_dev - 1,)), pltpu.SemaphoreType.DMA((n_dev - 1,)),
                        pltpu.SemaphoreType.DMA, pltpu.SemaphoreType.DMA((2,)), pltpu.SemaphoreType.DMA],
        compiler_params=_params(("arbitrary", "arbitrary")),
    )(h_t, *pieces, *mats16, pack)
    return outs[0], outs[1], outs[2:2 + n_mats], outs[2 + n_mats]


def _local_step(proj, qkv, x2d, tgt2d, bsz, seq, norm_v, w_s, b_s, w_og, w_osb, w_out, norm_final):
    d = x2d.shape[1]
    chunk = w_s.shape[-1]
    causal = jnp.tril(jnp.ones((chunk, chunk), dtype=bool))
    wm = jnp.where(causal[None], w_s, 0.0).astype(BF16)
    wm_t = jnp.swapaxes(wm, 1, 2)
    b_t = b_s.T

    ya = _branch_a_fwd(proj, norm_v, wm, b_t)
    o, yb = _attn_fwd(qkv, proj, bsz, seq)
    dya, dyb, dg, dx2, loss, g_nf, g_wog, g_wosb, g_wout = _out_proj(
        ya, yb, proj, x2d, tgt2d, w_og, w_osb, w_out, norm_final.reshape(1, d))
    dq, dk, dv, dzb = _attn_bwd(qkv, proj, o, dyb, bsz, seq)
    d_a, g_ws, g_bt, g_nv = _branch_a_bwd(proj, dya, norm_v, wm, wm_t, b_t)
    g_ws = jnp.where(causal[None], g_ws, 0.0)
    return loss, (d_a, dq, dk, dv, dzb, dg), dx2, g_nv, g_ws, g_bt.T, g_wog, g_wosb, g_wout, g_nf


def _row_tile(rows):
    return next(r for r in (128, 64, 32, 16, 8) if rows % r == 0)


def _cast_bf16(arrs):
    n = len(arrs)

    def body(*refs):
        for a_ref, o_ref in zip(refs[:n], refs[n:]):
            o_ref[...] = a_ref[...].astype(BF16)

    specs = [pl.BlockSpec((a.shape[0] // CAST_STEPS, a.shape[1]), lambda i: (i, 0)) for a in arrs]
    return pl.pallas_call(
        body, name="cast_bf16", grid=(CAST_STEPS,),
        in_specs=specs, out_specs=specs,
        out_shape=[jax.ShapeDtypeStruct(a.shape, BF16) for a in arrs],
        compiler_params=_params(("arbitrary",)),
    )(*arrs)


def _chip_half(full, from_sibling, core, chip, tile):
    half, n = from_sibling.shape

    def body(where_ref, own_ref, sib_ref, o32_ref, o16_ref):
        total = own_ref[...] + sib_ref[...].astype(F32)
        o16_ref[...] = total.astype(BF16)

        @pl.when(pl.program_id(0) == where_ref[1])
        def _():
            o32_ref[...] = total

    blk = pl.BlockSpec((half, tile), lambda j, where_ref: (0, j))
    return pl.pallas_call(
        body, name="chip_half",
        grid_spec=pltpu.PrefetchScalarGridSpec(
            num_scalar_prefetch=1, grid=(n // tile,),
            in_specs=[pl.BlockSpec((half, tile), lambda j, where_ref: (where_ref[0], j)), blk],
            out_specs=[pl.BlockSpec((half, tile), lambda j, where_ref: (0, 0)), blk]),
        out_shape=[jax.ShapeDtypeStruct((half, tile), F32), jax.ShapeDtypeStruct((half, n), BF16)],
        compiler_params=_params(("arbitrary",)),
    )(jnp.stack([core, chip]).astype(jnp.int32), full, from_sibling)


def _add_received(fulls, recvs, chip, by_cols):
    n = len(fulls)
    _, rows, cols = recvs[0].shape
    tr = _row_tile(rows)
    nb = rows // tr

    def body(chip_ref, *refs):
        for own_ref, recv_ref, o_ref in zip(refs[:n], refs[n:2 * n], refs[2 * n:]):
            s = own_ref[...]
            for k in range(N_CHIP - 1):
                s = s + recv_ref[k].astype(F32)
            o_ref[...] = s

    own_map = (lambda i, chip_ref: (i, chip_ref[0])) if by_cols else (lambda i, chip_ref: (chip_ref[0] * nb + i, 0))
    return pl.pallas_call(
        body, name="add_received",
        grid_spec=pltpu.PrefetchScalarGridSpec(
            num_scalar_prefetch=1, grid=(nb,),
            in_specs=[pl.BlockSpec((tr, cols), own_map)] * n
            + [pl.BlockSpec((N_CHIP - 1, tr, cols), lambda i, chip_ref: (0, i, 0))] * n,
            out_specs=[pl.BlockSpec((tr, cols), lambda i, chip_ref: (i, 0))] * n),
        out_shape=[jax.ShapeDtypeStruct((rows, cols), F32)] * n,
        compiler_params=_params(("arbitrary",)),
    )(chip.reshape(1).astype(jnp.int32), *fulls, *recvs)


def _adamw_math(w, m, v, g):
    new_m = ADAM_B1 * m + (1.0 - ADAM_B1) * g
    new_v = ADAM_B2 * v + (1.0 - ADAM_B2) * (g * g)
    m_hat = new_m / (1.0 - ADAM_B1 ** ADAM_STEP)
    v_hat = new_v / (1.0 - ADAM_B2 ** ADAM_STEP)
    return -ADAM_LR * (m_hat / (jnp.sqrt(v_hat) + ADAM_EPS) + ADAM_WD * w), new_m, new_v


def _adamw_pairs(ws, ms, vs, mines, theirs):
    n = len(ws)
    rows, cols = ws[0].shape
    tr = _row_tile(rows)

    def body(*refs):
        ins, outs = refs[:5 * n], refs[5 * n:]
        for a in range(n):
            w_ref, m_ref, v_ref, mine_ref, theirs_ref = ins[a::n]
            g_ref, d_ref, nm_ref, nv_ref = outs[4 * a:4 * a + 4]
            g = mine_ref[...] + theirs_ref[...]
            g_ref[...] = g
            d_ref[...], nm_ref[...], nv_ref[...] = _adamw_math(w_ref[...], m_ref[...], v_ref[...], g)

    spec = pl.BlockSpec((tr, cols), lambda i: (i, 0))
    out = jax.ShapeDtypeStruct((rows, cols), F32)
    outs = pl.pallas_call(
        body, name="adamw_pairs", grid=(rows // tr,),
        in_specs=[spec] * (5 * n), out_specs=[spec] * (4 * n), out_shape=[out] * (4 * n),
        compiler_params=_params(("arbitrary",)),
    )(*ws, *ms, *vs, *mines, *theirs)
    return [outs[4 * a:4 * a + 4] for a in range(n)]


def _adamw_halves(w, m, v, mine, theirs, core):
    rows, cols = w.shape
    tr = _row_tile(rows // 2)
    per_half = rows // 2 // tr

    def body(core_ref, w_ref, m_ref, v_ref, mine_ref, theirs_ref, g_ref, d_ref, nm_ref, nv_ref):
        is_mine = pl.program_id(0) // per_half == core_ref[0]
        for part, cond in ((mine_ref, is_mine), (theirs_ref, jnp.logical_not(is_mine))):
            @pl.when(cond)
            def _(part=part):
                g = part[...]
                g_ref[...] = g
                d_ref[...], nm_ref[...], nv_ref[...] = _adamw_math(w_ref[...], m_ref[...], v_ref[...], g)

    spec = pl.BlockSpec((tr, cols), lambda i, core_ref: (i, 0))

    def half_spec(own):
        def index(i, core_ref):
            in_core_half = i // per_half == core_ref[0]
            here = in_core_half if own else jnp.logical_not(in_core_half)
            return jnp.where(here, i % per_half, 0), 0
        return pl.BlockSpec((tr, cols), index)

    out = jax.ShapeDtypeStruct(w.shape, F32)
    return pl.pallas_call(
        body, name="adamw_halves",
        grid_spec=pltpu.PrefetchScalarGridSpec(
            num_scalar_prefetch=1, grid=(rows // tr,),
            in_specs=[spec] * 3 + [half_spec(True), half_spec(False)], out_specs=[spec] * 4),
        out_shape=[out] * 4,
        compiler_params=_params(("arbitrary",)),
    )(core.reshape(1).astype(jnp.int32), w, m, v, mine, theirs)


def _adamw_small(w, m, v, slots_head, slots_tail):
    n_dev, p0, _ = slots_head.shape

    def body(w_ref, m_ref, v_ref, head_ref, tail_ref, g_ref, d_ref, nm_ref, nv_ref):
        for ref, rows in ((head_ref, slice(0, p0)), (tail_ref, slice(p0, w.shape[0]))):
            g = ref[0]
            for i in range(1, n_dev):
                g = g + ref[i]
            g_ref[rows, :] = g
            d_ref[rows, :], nm_ref[rows, :], nv_ref[rows, :] = _adamw_math(w_ref[rows, :], m_ref[rows, :], v_ref[rows, :], g)

    vmem = pl.BlockSpec(memory_space=pltpu.VMEM)
    out = jax.ShapeDtypeStruct(w.shape, F32)
    return pl.pallas_call(
        body, name="adamw_small", in_specs=[vmem] * 5, out_specs=[vmem] * 4, out_shape=[out] * 4,
        compiler_params=pltpu.CompilerParams(vmem_limit_bytes=VMEM_LIMIT),
    )(w, m, v, slots_head, slots_tail)


ANY = pl.BlockSpec(memory_space=pl.ANY)


def _place():
    x, y, c = lax.axis_index("x"), lax.axis_index("y"), lax.axis_index("c")
    other_chips = [(1 - x, y), (x, 1 - y), (1 - x, 1 - y)]
    return x, y, c, other_chips


def _swap_and_gather(arrs, pack):
    n = len(arrs)
    n_dev = 2 * N_CHIP
    flips = [(dx, dy, dc) for dx in (0, 1) for dy in (0, 1) for dc in (0, 1)][1:]

    def body(*refs):
        ins, pack_ref = refs[:n], refs[n]
        outs, slots_ref = refs[n + 1:2 * n + 1], refs[2 * n + 1]
        send_sems, recv_sems, pack_send, pack_recv, own_sem = refs[2 * n + 2:]
        x, y, c, _ = _place()
        me = 4 * x + 2 * y + c
        own = pltpu.make_async_copy(pack_ref, slots_ref.at[me], own_sem)
        own.start()
        copies = []
        for k, (dx, dy, dc) in enumerate(flips):
            peer = (1 - x if dx else x, 1 - y if dy else y, 1 - c if dc else c)
            copies.append(pltpu.make_async_remote_copy(
                src_ref=pack_ref, dst_ref=slots_ref.at[me], send_sem=pack_send.at[k], recv_sem=pack_recv.at[k],
                device_id=peer, device_id_type=MESH))
        copies += [pltpu.make_async_remote_copy(
            src_ref=ins[a], dst_ref=outs[a], send_sem=send_sems.at[a], recv_sem=recv_sems.at[a],
            device_id=(x, y, 1 - c), device_id_type=MESH) for a in range(n)]
        for cp in copies:
            cp.start()
        for cp in copies:
            cp.wait()
        own.wait()

    return pl.pallas_call(
        body, name="swap_and_gather",
        in_specs=[ANY] * (n + 1), out_specs=[ANY] * (n + 1),
        out_shape=[jax.ShapeDtypeStruct(a.shape, a.dtype) for a in arrs] + [jax.ShapeDtypeStruct((n_dev,) + pack.shape, F32)],
        scratch_shapes=[pltpu.SemaphoreType.DMA((n,)), pltpu.SemaphoreType.DMA((n,)),
                        pltpu.SemaphoreType.DMA((n_dev - 1,)), pltpu.SemaphoreType.DMA((n_dev - 1,)),
                        pltpu.SemaphoreType.DMA],
    )(*arrs, pack)


SLAB = 8


def _slab(vec, d):
    return jnp.pad(vec.reshape(1, d), ((0, SLAB - 1), (0, 0)))


def _pack_tail(vec_nv, b_s, vec_nf, w_s, scalar=None):
    d = vec_nv.shape[-1]
    extra = jnp.zeros((1, d), F32) if scalar is None else jnp.pad(scalar, ((0, 0), (0, d - 1)))
    return jnp.concatenate([_slab(v, d) for v in (vec_nv, b_s, vec_nf, extra)] + [w_s.reshape(-1, d)], axis=0)


def _pack_small(vec_nin, vec_nv, b_s, vec_nf, w_s):
    return jnp.concatenate([_slab(vec_nin, vec_nin.shape[-1]), _pack_tail(vec_nv, b_s, vec_nf, w_s)], axis=0)


def _unpack_small(pack, w_s_shape, b_s_shape):
    return (pack[0:1], pack[SLAB:SLAB + 1], pack[2 * SLAB].reshape(b_s_shape), pack[3 * SLAB],
            pack[5 * SLAB:].reshape(w_s_shape), pack[4 * SLAB, 0])


def kernel(x, norm_in, w_in, norm_v, w_s, b_s, w_o_gmlp, w_o_sb, w_out, norm_final, loss_target, m_norm_in, m_w_in, m_norm_v, m_w_s, m_b_s, m_w_o_gmlp, m_w_o_sb, m_w_out, m_norm_final, v_norm_in, v_w_in, v_norm_v, v_w_s, v_b_s, v_w_o_gmlp, v_w_o_sb, v_w_out, v_norm_final):
    d = x.shape[-1]
    ncol = w_in.shape[-1]
    nrow = w_o_gmlp.shape[-2]
    chip = 2 * lax.axis_index("x") + lax.axis_index("y")

    bsz, seq, _ = x.shape
    x2d = x.reshape(bsz * seq, d)
    shards = [w_in[0], w_o_gmlp[0], w_o_sb[0], w_out[0]]
    halves = [s16.reshape(2, s16.shape[0] // 2, s16.shape[1]) for s16 in _cast_bf16(shards)]
    proj, qkv, h_t, (wg, w_og, w_osb, w_o) = _in_proj_fwd(x2d, norm_in, halves)
    wg = wg.reshape(N_CHIP, d, ncol)

    loss, pieces, dx2, g_nv, g_ws, g_bs, g_wog, g_wosb, g_wout, g_nf = _local_step(
        proj, qkv, x2d, loss_target.reshape(bsz * seq, d), bsz, seq, norm_v, w_s[0], b_s[0],
        w_og.reshape(d, d), w_osb.reshape(d, d), w_o.reshape(d, d), norm_final)

    mats = [g_wog, g_wosb, g_wout]
    mats16 = [g16.reshape(N_CHIP, nrow, d) for g16 in _cast_bf16(mats)]
    g_win, from_sibling, recv_mats, slots_tail = _in_proj_bwd_w(
        h_t, pieces, mats16, _pack_tail(g_nv, g_bs, g_nf, g_ws, loss))
    core = lax.axis_index("c")
    half_own, half_win16 = _chip_half(g_win, from_sibling, core, chip, ncol)
    grad_x, g_nin, recv_win = _in_proj_bwd_x(pieces, wg, x2d, norm_in, dx2, half_win16)
    grad_x = grad_x.reshape(bsz, seq, d)

    sums = _add_received([half_own], [recv_win], jnp.zeros((), jnp.int32), True) + _add_received(
        mats, recv_mats, chip, False)
    *sibling_sums, slots_head = _swap_and_gather(sums, _slab(g_nin, d))
    stats = [_adamw_halves(w_in[0], m_w_in[0], v_w_in[0], sums[0], sibling_sums[0], core)] + _adamw_pairs(
        shards[1:], [m_w_o_gmlp[0], m_w_o_sb[0], m_w_out[0]], [v_w_o_gmlp[0], v_w_o_sb[0], v_w_out[0]],
        sums[1:], sibling_sums[1:])
    (gw_in, dw_in, nm_in, nv_in), (gw_og, dw_og, nm_og, nv_og), (gw_osb, dw_osb, nm_osb, nv_osb), \
        (gw_out, dw_out, nm_out, nv_out) = stats

    gs, ds, ms, vs = _adamw_small(
        _pack_small(norm_in, norm_v, b_s[0], norm_final, w_s[0]),
        _pack_small(m_norm_in, m_norm_v, m_b_s[0], m_norm_final, m_w_s[0]),
        _pack_small(v_norm_in, v_norm_v, v_b_s[0], v_norm_final, v_w_s[0]), slots_head, slots_tail)

    def small(pack):
        nin, nv, bs, nf, ws, _ = _unpack_small(pack, w_s.shape, b_s.shape)
        return nin, nv, ws, bs, nf

    loss = _unpack_small(gs, w_s.shape, b_s.shape)[-1]
    out = []
    for small_pack, win, wog, wosb, wout in ((gs, gw_in, gw_og, gw_osb, gw_out), (ds, dw_in, dw_og, dw_osb, dw_out),
                                             (ms, nm_in, nm_og, nm_osb, nm_out), (vs, nv_in, nv_og, nv_osb, nv_out)):
        nin, nv, ws, bs, nf = small(small_pack)
        out += [nin, win[None], nv, ws, bs, wog[None], wosb[None], wout[None], nf]
    return (loss, grad_x, *out)
```

```python
import functools
import math

import jax
import jax.numpy as jnp
from jax import lax
from jax.experimental import pallas as pl
from jax.experimental.pallas import tpu as pltpu

F32 = jnp.float32
BF16 = jnp.bfloat16
EPS = 1e-6
HEADS = 8
N_SPLIT = 9
N_CHIP = 4
MESH = pl.DeviceIdType.MESH

ADAM_LR = 0.001
ADAM_B1 = 0.9
ADAM_B2 = 0.999
ADAM_EPS = 1e-08
ADAM_WD = 0.01
ADAM_STEP = 10

VMEM_LIMIT = 56 * 2 ** 20
TM = 256
TMF = 512
TMX = 512
ATT_T = 256
ATT_HP = 4
TKW = 1024
CHUNKS_PER_STEP = 4
CAST_STEPS = 8

NT = (((1,), (1,)), ((), ()))
TN = (((0,), (0,)), ((), ()))


def _params(sem):
    return pltpu.CompilerParams(dimension_semantics=sem, vmem_limit_bytes=VMEM_LIMIT)


def _resident(shape):
    nd = len(shape)
    return pl.BlockSpec(shape, lambda *_: (0,) * nd, pipeline_mode=pl.Buffered(1))


def _const(shape):
    nd = len(shape)
    return pl.BlockSpec(shape, lambda *_: (0,) * nd)


def _segments(d, ncol):
    segs = []
    edges = sorted({j * ncol for j in range(N_CHIP + 1)} | {s * d for s in range(N_SPLIT + 1)})
    for lo, hi in zip(edges[:-1], edges[1:]):
        segs.append((lo // ncol, lo % ncol, lo // d, lo % d, hi - lo))
    return segs


def _sigmoid(x):
    return 0.5 * jnp.tanh(0.5 * x) + 0.5


_GELU_C = math.sqrt(2.0 / math.pi)


_GELU_CA = _GELU_C * 0.044715


def _gelu(x):
    return x * (0.5 * jnp.tanh(x * (_GELU_C + _GELU_CA * (x * x))) + 0.5)


def _gelu_and_grad(x):
    x2 = x * x
    u = 0.5 * jnp.tanh(x * (_GELU_C + _GELU_CA * x2)) + 0.5
    slope = (1.0 - u) * (x * (_GELU_C + (3.0 * _GELU_CA) * x2))
    return x * u, u * (2.0 * slope + 1.0)


def _split_bf16(a):
    hi = a.astype(BF16)
    lo = (a - hi.astype(F32)).astype(BF16)
    return hi, lo


def _in_proj_fwd(x2d, g_in, halves):
    t, d = x2d.shape
    n = len(halves)
    ncol = halves[0].shape[2]
    n_row = t // TMF
    last = n_row - 1
    assert halves[0].shape[1] * 2 == d
    qkv_parts = {j: (max(j * ncol, 3 * d) - j * ncol, max(j * ncol, 3 * d) - 3 * d)
                 for j in range(N_CHIP) if min((j + 1) * ncol, 6 * d) > max(j * ncol, 3 * d)}
    qkv_w = 3 * d // len(qkv_parts)
    assert all(min((j + 1) * ncol, 6 * d) - max(j * ncol, 3 * d) == qkv_w and q0 % qkv_w == 0
               for j, (_, q0) in qkv_parts.items())

    def body(order_ref, qrow_ref, qcol_ref, x_ref, g_ref, *rest):
        ins = rest[:n]
        proj_ref, qkv_ref, ht_ref = rest[n:n + 3]
        outs = rest[n + 3:2 * n + 3]
        wbuf, h_all, send_sems, recv_sems, local_sems, load_sem = rest[2 * n + 3:]
        phase = pl.program_id(0)
        i = pl.program_id(1)
        x_pos, y_pos, c_pos, chips = _place()
        sibling = (x_pos, y_pos, 1 - c_pos)
        me = (x_pos, y_pos, c_pos)
        my_chip = 2 * x_pos + y_pos

        def copy(a, k, block, to, src=None):
            return pltpu.make_async_remote_copy(
                src_ref=outs[a].at[block] if src is None else src, dst_ref=outs[a].at[block],
                send_sem=send_sems.at[a, k], recv_sem=recv_sems.at[a, k], device_id=to, device_id_type=MESH)

        def local(a):
            return pltpu.make_async_copy(ins[a], outs[a].at[pl.ds(2 * my_chip, 2)], local_sems.at[a])

        def load(src, first, slot):
            for half in range(2):
                cp = pltpu.make_async_copy(src.at[first + half], wbuf.at[slot, pl.ds(half * (d // 2), d // 2)], load_sem)
                cp.start()
                cp.wait()

        def send_mine(k):
            px, py = chips[k]
            for a in range(n):
                copy(a, k, 2 * my_chip + c_pos, (px, py, c_pos), src=ins[a].at[c_pos]).start()

        @pl.when((phase == 0) & (i == 0))
        def _():
            for a in range(n):
                local(a).start()
            send_mine(0)
            send_mine(1)
            load(ins[0], 0, 0)

        def pass_on(a, k):
            theirs = 2 * (2 * chips[k][0] + chips[k][1]) + c_pos
            copy(a, k, theirs, me).wait_recv()
            copy(a, 3 + k, theirs, sibling).start()

        def from_sibling(a, k):
            copy(a, 3 + k, 2 * (2 * chips[k][0] + chips[k][1]) + 1 - c_pos, me).wait_recv()

        for k, (px, py) in enumerate(chips):
            @pl.when((phase == k + 1) & (i == 0))
            def _(k=k, px=px, py=py):
                pass_on(0, k)
                if k == 0:
                    send_mine(2)
                from_sibling(0, k)
                load(outs[0], 2 * (2 * px + py), (k + 1) % 2)

        @pl.when(phase == 0)
        def _():
            x = x_ref[...]
            r = lax.rsqrt(jnp.mean(x * x, axis=-1, keepdims=True) + EPS)
            hf = x * r * g_ref[...]
            h_all[i] = hf.astype(BF16)
            ht_ref[...] = hf.T.astype(BF16)

        for slot in range(2):
            @pl.when(phase % 2 == slot)
            def _(slot=slot):
                proj_ref[...] = jnp.dot(h_all[i], wbuf[slot], preferred_element_type=F32)

        for chunk, (c0, _) in qkv_parts.items():
            @pl.when(order_ref[phase] == chunk)
            def _(c0=c0):
                qkv_ref[...] = proj_ref[:, c0:c0 + qkv_w].astype(BF16)

        @pl.when((phase == N_CHIP - 1) & (i == n_row - 1))
        def _():
            for k in range(N_CHIP - 1):
                for a in range(1, n):
                    pass_on(a, k)
            for k in range(N_CHIP - 1):
                for a in range(1, n):
                    from_sibling(a, k)
            for a in range(n):
                for k in range(2 * (N_CHIP - 1)):
                    copy(a, k, 0, me).wait_send()
                local(a).wait()

    x_pos, y_pos = lax.axis_index("x"), lax.axis_index("y")
    order = jnp.stack([2 * x_pos + y_pos, 2 * (1 - x_pos) + y_pos, 2 * x_pos + 1 - y_pos,
                       2 * (1 - x_pos) + 1 - y_pos]).astype(jnp.int32)
    holds = [functools.reduce(jnp.logical_or, [order[p] == j for j in qkv_parts]) for p in range(N_CHIP)]
    col = [sum(jnp.where(order[p] == j, q0 // qkv_w, 0) for j, (_, q0) in qkv_parts.items()) for p in range(N_CHIP)]
    cur = col[-1]
    for p in reversed(range(N_CHIP - 1)):
        cur = jnp.where(holds[p], col[p], cur)
    seen = jnp.bool_(False)
    qrow, qcol = [], []
    for p in range(N_CHIP):
        cur = jnp.where(holds[p], col[p], cur)
        qrow.append(jnp.where(holds[p], -1, jnp.where(seen, last, 0)))
        qcol.append(cur)
        seen = seen | holds[p]
    qrow = jnp.stack(qrow).astype(jnp.int32)
    qcol = jnp.stack(qcol).astype(jnp.int32)

    outs = pl.pallas_call(
        body, name="in_proj_fwd",
        grid_spec=pltpu.PrefetchScalarGridSpec(
            num_scalar_prefetch=3, grid=(N_CHIP, n_row),
            in_specs=[pl.BlockSpec((TMF, d), lambda p, i, order, qrow, qcol: (jnp.where(p == 0, i, last), 0)),
                      pl.BlockSpec((1, d), lambda p, i, order, qrow, qcol: (0, 0))] + [ANY] * n,
            out_specs=[pl.BlockSpec((TMF, ncol), lambda p, i, order, qrow, qcol: (i, order[p])),
                       pl.BlockSpec((TMF, qkv_w),
                                    lambda p, i, order, qrow, qcol: (jnp.where(qrow[p] < 0, i, qrow[p]), qcol[p])),
                       pl.BlockSpec((d, TMF), lambda p, i, order, qrow, qcol: (0, jnp.where(p == 0, i, last)))]
            + [ANY] * n,
            scratch_shapes=[pltpu.VMEM((2, d, ncol), BF16), pltpu.VMEM((n_row, TMF, d), BF16),
                            pltpu.SemaphoreType.DMA((n, 2 * (N_CHIP - 1))), pltpu.SemaphoreType.DMA((n, 2 * (N_CHIP - 1))),
                            pltpu.SemaphoreType.DMA((n,)), pltpu.SemaphoreType.DMA]),
        out_shape=[jax.ShapeDtypeStruct((t, N_CHIP * ncol), F32), jax.ShapeDtypeStruct((t, 3 * d), BF16),
                   jax.ShapeDtypeStruct((d, t), BF16)]
        + [jax.ShapeDtypeStruct((2 * N_CHIP,) + hv.shape[1:], BF16) for hv in halves],
        compiler_params=_params(("arbitrary", "arbitrary")),
    )(order, qrow, qcol, x2d, g_in, *halves)
    return outs[0], outs[1], outs[2], outs[3:]


def _branch_a_fwd(a_pre, g_v, wm, b_t):
    t = a_pre.shape[0]
    d = g_v.shape[1]
    d3 = 3 * d
    ng, chunk, _ = wm.shape
    cw = d // ng

    per_step = CHUNKS_PER_STEP if t % (CHUNKS_PER_STEP * chunk) == 0 else 1

    def body(a_ref, gv_ref, wm_ref, bt_ref, ya_ref):
        for n in range(per_step):
            rows = slice(n * chunk, (n + 1) * chunk)
            ua = _gelu(a_ref[rows, 0:d])
            vg = _gelu(a_ref[rows, d:2 * d])
            za = a_ref[rows, 2 * d:3 * d]
            rv = lax.rsqrt(jnp.mean(vg * vg, axis=-1, keepdims=True) + EPS)
            va = (vg * rv * gv_ref[...]).astype(BF16)
            gate = ua * (za * _sigmoid(za))
            for g in range(ng):
                sl = slice(g * cw, (g + 1) * cw)
                mixed = jnp.dot(wm_ref[g], va[:, sl], preferred_element_type=F32) + bt_ref[:, g:g + 1]
                ya_ref[rows, sl] = (gate[:, sl] * mixed).astype(BF16)

    tile = per_step * chunk
    return pl.pallas_call(
        body, name="branch_a_fwd",
        grid=(t // tile,),
        in_specs=[pl.BlockSpec((tile, d3), lambda i: (i, 0)), _const((1, d)), _const(wm.shape), _const(b_t.shape)],
        out_specs=pl.BlockSpec((tile, d), lambda i: (i, 0)),
        out_shape=jax.ShapeDtypeStruct((t, d), BF16),
        compiler_params=_params(("arbitrary",)),
    )(a_pre, g_v, wm, b_t)


def _branch_a_bwd(a_pre, dya, g_v, wm, wm_t, b_t):
    t = a_pre.shape[0]
    d = g_v.shape[1]
    d3 = 3 * d
    ng, chunk, _ = wm.shape
    cw = d // ng
    nsteps = t // chunk

    def body(a_ref, dya_ref, gv_ref, wm_ref, wmt_ref, bt_ref, da_ref, gws_ref, gbt_ref, gnv_ref, db_acc):
        i = pl.program_id(0)

        @pl.when(i == 0)
        def _():
            gws_ref[...] = jnp.zeros_like(gws_ref)
            gnv_ref[...] = jnp.zeros_like(gnv_ref)
            db_acc[...] = jnp.zeros_like(db_acc)

        ua, dgelu_u = _gelu_and_grad(a_ref[:, 0:d])
        vg, dgelu_v = _gelu_and_grad(a_ref[:, d:2 * d])
        za = a_ref[:, 2 * d:3 * d]
        sig = _sigmoid(za)
        sz = za * sig
        dsz = sig * (1.0 + za * (1.0 - sig))
        rv = lax.rsqrt(jnp.mean(vg * vg, axis=-1, keepdims=True) + EPS)
        nv = vg * rv
        gv = gv_ref[...]
        va = (nv * gv).astype(BF16)
        dya = dya_ref[...]
        dmix = dya * ua * sz
        db_acc[...] += dmix
        dmix_b = dmix.astype(BF16)
        t_gate = dya * sz
        t_z = dya * ua * dsz
        dva_parts = []
        for g in range(ng):
            sl = slice(g * cw, (g + 1) * cw)
            mixed = jnp.dot(wm_ref[g], va[:, sl], preferred_element_type=F32) + bt_ref[:, g:g + 1]
            da_ref[:, sl] = (t_gate[:, sl] * mixed * dgelu_u[:, sl]).astype(BF16)
            da_ref[:, 2 * d + g * cw:2 * d + (g + 1) * cw] = (t_z[:, sl] * mixed).astype(BF16)
            gws_ref[g] += lax.dot_general(dmix_b[:, sl], va[:, sl], NT, preferred_element_type=F32)
            dva_parts.append(jnp.dot(wmt_ref[g], dmix_b[:, sl], preferred_element_type=F32))
        dva = jnp.concatenate(dva_parts, axis=1)
        gnv_ref[...] += jnp.sum(dva * nv, axis=0, keepdims=True)
        dnv = dva * gv
        dvg = rv * (dnv - nv * jnp.mean(dnv * nv, axis=-1, keepdims=True))
        da_ref[:, d:2 * d] = (dvg * dgelu_v).astype(BF16)

        @pl.when(i == nsteps - 1)
        def _():
            acc = db_acc[...]
            for g in range(ng):
                gbt_ref[:, g:g + 1] = jnp.sum(acc[:, g * cw:(g + 1) * cw], axis=1, keepdims=True)

    return pl.pallas_call(
        body, name="branch_a_bwd",
        grid=(nsteps,),
        in_specs=[pl.BlockSpec((chunk, d3), lambda i: (i, 0)), pl.BlockSpec((chunk, d), lambda i: (i, 0)),
                  _const((1, d)), _const(wm.shape), _const(wm_t.shape), _const(b_t.shape)],
        out_specs=[pl.BlockSpec((chunk, d3), lambda i: (i, 0)), _const(wm.shape), _const(b_t.shape), _const((1, d))],
        out_shape=[jax.ShapeDtypeStruct((t, d3), BF16), jax.ShapeDtypeStruct(wm.shape, F32),
                   jax.ShapeDtypeStruct(b_t.shape, F32), jax.ShapeDtypeStruct((1, d), F32)],
        scratch_shapes=[pltpu.VMEM((chunk, d), F32)],
        compiler_params=_params(("arbitrary",)),
    )(a_pre, dya, g_v, wm, wm_t, b_t)


def _below_diagonal(n):
    return lax.broadcasted_iota(jnp.int32, (n, n), 0) > lax.broadcasted_iota(jnp.int32, (n, n), 1)


def _twice(tri):
    t = tri.astype(BF16)
    return jnp.concatenate([t, t], axis=0)


def _cumsum_mm(a, tri2):
    hi, lo = _split_bf16(a)
    return jnp.dot(jnp.concatenate([hi, lo], axis=1), tri2, preferred_element_type=F32)


LOG2E = 1.4426950408889634
_SIGN = 0x80000000


def _sb_block(q, k, scale, upper2, causal):
    z2 = lax.dot_general(q, k, NT, preferred_element_type=F32) * (scale * LOG2E)
    neg_abs = lax.bitcast_convert_type(lax.bitcast_convert_type(z2, jnp.uint32) | jnp.uint32(_SIGN), F32)
    l2 = jnp.log(1.0 + jnp.exp2(neg_abs)) * LOG2E
    log_beta = jnp.minimum(z2, 0.0) - l2
    lom = log_beta - z2
    if causal is not None:
        lom = jnp.where(causal, lom, 0.0)
    sx = _cumsum_mm(lom, upper2)
    return log_beta, sx, sx[:, 0:1] + lom[:, 0:1]


DEAD_LOG2 = -150.0


def _max_carry(carries):
    return jnp.max(functools.reduce(jnp.maximum, carries))


ZB_GROUP, GA_GROUP, GB_GROUP = 6, 7, 8


def _attn_specs(d, seq, nq, heads_per_step):
    hp_w = heads_per_step * (d // HEADS)
    n_hp = d // hp_w
    row_blk = lambda group: pl.BlockSpec((ATT_T, hp_w), lambda b, h, i: (b * nq + i, group * n_hp + h))
    seq_blk = lambda group: pl.BlockSpec((seq, hp_w), lambda b, h, i: (b, group * n_hp + h))
    return row_blk, seq_blk, n_hp


def _attn_fwd(qkv, proj, bsz, seq):
    t, d3 = qkv.shape
    d = d3 // 3
    hd = d // HEADS
    nq = seq // ATT_T
    scale = hd ** -0.5
    n_heads = ATT_HP
    row_blk, seq_blk, n_hp = _attn_specs(d, seq, nq, n_heads)

    def body(q_ref, k_ref, v_ref, zb_ref, o_ref, yb_ref):
        i = pl.program_id(2)
        causal = _below_diagonal(ATT_T)
        upper2 = _twice(causal)

        def step(kb, state, mask):
            rows = pl.ds(pl.multiple_of(kb * ATT_T, ATT_T), ATT_T)
            heads = [slice(h * hd, (h + 1) * hd) for h in range(n_heads)]
            scores = [_sb_block(q_ref[:, cols], k_ref[rows, cols], scale, upper2, mask) for cols in heads]
            new = []
            for cols, (carry, acc), (log_beta, sx, total) in zip(heads, state, scores):
                a = jnp.exp2(log_beta + sx + carry)
                if mask is not None:
                    a = jnp.where(mask, a, 0.0)
                acc = acc + jnp.dot(a.astype(BF16), v_ref[rows, cols], preferred_element_type=F32)
                new.append((carry + total, acc))
            return tuple(new)

        init = tuple((jnp.zeros((ATT_T, 1), F32), jnp.zeros((ATT_T, hd), F32)) for _ in range(n_heads))
        state = step(i, init, causal)
        def more(c):
            new = step(c[0], c[1], None)
            return c[0] - 1, new, _max_carry([s[0] for s in new])

        _, state, _ = lax.while_loop(lambda c: (c[0] >= 0) & (c[2] > DEAD_LOG2), more,
                                     (i - 1, state, _max_carry([s[0] for s in state])))
        for h in range(n_heads):
            cols = slice(h * hd, (h + 1) * hd)
            acc = state[h][1]
            zb = zb_ref[:, cols]
            o_ref[:, cols] = acc
            yb_ref[:, cols] = (acc * (zb * _sigmoid(zb))).astype(BF16)

    return pl.pallas_call(
        body, name="attn_fwd",
        grid=(bsz, n_hp, nq),
        in_specs=[row_blk(0), seq_blk(1), seq_blk(2), row_blk(ZB_GROUP)],
        out_specs=[row_blk(0), row_blk(0)],
        out_shape=[jax.ShapeDtypeStruct((t, d), F32), jax.ShapeDtypeStruct((t, d), BF16)],
        compiler_params=_params(("arbitrary", "arbitrary", "arbitrary")),
    )(qkv, qkv, qkv, proj)


def _attn_bwd(qkv, proj, o, dyb, bsz, seq):
    t, d3 = qkv.shape
    d = d3 // 3
    hd = d // HEADS
    nq = seq // ATT_T
    scale = hd ** -0.5
    row_blk, seq_blk, n_hp = _attn_specs(d, seq, nq, ATT_HP)

    def body(q_ref, k_ref, v_ref, zb_ref, o_ref, dyb_ref, dq_ref, dk_ref, dv_ref, dzb_ref,
             g_s, beta_s, dkt_acc, dvt_acc):
        i = pl.program_id(2)

        @pl.when(i == 0)
        def _():
            dkt_acc[...] = jnp.zeros_like(dkt_acc)
            dvt_acc[...] = jnp.zeros_like(dvt_acc)

        causal = _below_diagonal(ATT_T)
        upper2 = _twice(causal)
        lower2 = _twice(~causal)
        zb = zb_ref[...]
        sig = _sigmoid(zb)
        dyb_t = dyb_ref[...]
        do_f = dyb_t * (zb * sig)
        do = do_f.astype(BF16)
        do_t = do_f.T.astype(BF16)
        q_t = q_ref[...].astype(F32).T.astype(BF16)
        dzb_ref[...] = (dyb_t * o_ref[...] * (sig * (1.0 + zb * (1.0 - sig)))).astype(BF16)

        def sweep(kb, carries, mask):
            rows = pl.ds(pl.multiple_of(kb * ATT_T, ATT_T), ATT_T)
            heads = [slice(h * hd, (h + 1) * hd) for h in range(ATT_HP)]
            scores = [_sb_block(q_ref[:, cols], k_ref[rows, cols], scale, upper2, mask) for cols in heads]
            das = [lax.dot_general(do[:, cols], v_ref[rows, cols], NT, preferred_element_type=F32) for cols in heads]
            new = []
            for h, (cols, carry, (log_beta, sx, total), da) in enumerate(zip(heads, carries, scores, das)):
                a = jnp.exp2(log_beta + sx + carry)
                beta = jnp.exp2(log_beta)
                if mask is not None:
                    a = jnp.where(mask, a, 0.0)
                    beta = jnp.where(mask, beta, 0.0)
                g_s[h, kb] = a * da
                beta_s[h, kb] = beta
                dvt_acc[kb, cols, :] += jnp.dot(do_t[cols, :], a.astype(BF16), preferred_element_type=F32)
                new.append(carry + total)
            return tuple(new)

        carries = sweep(i, tuple(jnp.zeros((ATT_T, 1), F32) for _ in range(ATT_HP)), causal)

        def more(c):
            new = sweep(c[0], c[1], None)
            return c[0] - 1, new, _max_carry(new)

        last, _, _ = lax.while_loop(lambda c: (c[0] >= 0) & (c[2] > DEAD_LOG2), more, (i - 1, carries, _max_carry(carries)))
        first_kb = last + 1

        def back(kb, state):
            rows = pl.ds(pl.multiple_of(kb * ATT_T, ATT_T), ATT_T)
            heads = [slice(h * hd, (h + 1) * hd) for h in range(ATT_HP)]
            sums = [_cumsum_mm(g_s[h, kb], lower2) for h in range(ATT_HP)]
            new = []
            for h, (cols, (p_carry, dq), px) in enumerate(zip(heads, state, sums)):
                dz = ((g_s[h, kb] - (p_carry + px) * beta_s[h, kb]) * scale).astype(BF16)
                dq = dq + jnp.dot(dz, k_ref[rows, cols], preferred_element_type=F32)
                dkt_acc[kb, cols, :] += jnp.dot(q_t[cols, :], dz, preferred_element_type=F32)
                new.append((p_carry + px[:, ATT_T - 1:ATT_T], dq))
            return tuple(new)

        init = tuple((jnp.zeros((ATT_T, 1), F32), jnp.zeros((ATT_T, hd), F32)) for _ in range(ATT_HP))
        state = lax.fori_loop(first_kb, i + 1, back, init)
        for h in range(ATT_HP):
            dq_ref[:, h * hd:(h + 1) * hd] = state[h][1].astype(BF16)

        @pl.when(i == nq - 1)
        def _():
            for kb in range(nq):
                dk_ref[kb * ATT_T:(kb + 1) * ATT_T, :] = dkt_acc[kb].T.astype(BF16)
                dv_ref[kb * ATT_T:(kb + 1) * ATT_T, :] = dvt_acc[kb].T.astype(BF16)

    out = jax.ShapeDtypeStruct((t, d), BF16)
    hp_w = ATT_HP * hd
    return pl.pallas_call(
        body, name="attn_bwd",
        grid=(bsz, n_hp, nq),
        in_specs=[row_blk(0), seq_blk(1), seq_blk(2), row_blk(ZB_GROUP), row_blk(0), row_blk(0)],
        out_specs=[row_blk(0), seq_blk(0), seq_blk(0), row_blk(0)],
        out_shape=[out, out, out, out],
        scratch_shapes=[pltpu.VMEM((ATT_HP, nq, ATT_T, ATT_T), F32), pltpu.VMEM((ATT_HP, nq, ATT_T, ATT_T), F32),
                        pltpu.VMEM((nq, hp_w, ATT_T), F32), pltpu.VMEM((nq, hp_w, ATT_T), F32)],
        compiler_params=_params(("arbitrary", "arbitrary", "arbitrary")),
    )(qkv, qkv, qkv, proj, o, dyb)


def _out_proj(ya, yb, g_pre, x2d, tgt, w_og, w_osb, w_out, g_f):
    t, d = x2d.shape

    def body(ya_ref, yb_ref, ga_ref, gb_ref, x_ref, tgt_ref, wog_ref, wosb_ref, wout_ref, gf_ref,
             dya_ref, dyb_ref, dg_ref, dx2_ref, loss_ref, gnf_ref, gwog_ref, gwosb_ref, gwout_ref):
        @pl.when(pl.program_id(0) == 0)
        def _():
            loss_ref[...] = jnp.zeros_like(loss_ref)
            gnf_ref[...] = jnp.zeros_like(gnf_ref)
            gwog_ref[...] = jnp.zeros_like(gwog_ref)
            gwosb_ref[...] = jnp.zeros_like(gwosb_ref)
            gwout_ref[...] = jnp.zeros_like(gwout_ref)

        ya = ya_ref[...]
        yb = yb_ref[...]
        pa = jnp.dot(ya, wog_ref[...], preferred_element_type=F32)
        pb = jnp.dot(yb, wosb_ref[...], preferred_element_type=F32)
        sga = _sigmoid(ga_ref[...])
        sgb = _sigmoid(gb_ref[...])
        merged = (sga * pa + sgb * pb).astype(BF16)
        x2 = x_ref[...] + jnp.dot(merged, wout_ref[...], preferred_element_type=F32)
        r2 = lax.rsqrt(jnp.mean(x2 * x2, axis=-1, keepdims=True) + EPS)
        n2 = x2 * r2
        gf = gf_ref[...]
        err = n2 * gf - tgt_ref[...]
        loss_ref[...] += 0.5 * jnp.sum(jnp.sum(err * err, axis=-1, keepdims=True), axis=0, keepdims=True) / d
        dy = err * (1.0 / d)
        gnf_ref[...] += jnp.sum(dy * n2, axis=0, keepdims=True)
        dn = dy * gf
        dx2 = r2 * (dn - n2 * jnp.mean(dn * n2, axis=-1, keepdims=True))
        dx2_ref[...] = dx2
        dx2_b = dx2.astype(BF16)
        dmerged = lax.dot_general(dx2_b, wout_ref[...], NT, preferred_element_type=F32)
        gwout_ref[...] += lax.dot_general(merged, dx2_b, TN, preferred_element_type=F32)
        dg_ref[:, 0:d] = (dmerged * pa * (sga * (1.0 - sga))).astype(BF16)
        dg_ref[:, d:2 * d] = (dmerged * pb * (sgb * (1.0 - sgb))).astype(BF16)
        dpa = (dmerged * sga).astype(BF16)
        dpb = (dmerged * sgb).astype(BF16)
        dya_ref[...] = lax.dot_general(dpa, wog_ref[...], NT, preferred_element_type=F32)
        dyb_ref[...] = lax.dot_general(dpb, wosb_ref[...], NT, preferred_element_type=F32)
        gwog_ref[...] += lax.dot_general(ya, dpa, TN, preferred_element_type=F32)
        gwosb_ref[...] += lax.dot_general(yb, dpb, TN, preferred_element_type=F32)

    row = lambda i: (i, 0)
    return pl.pallas_call(
        body, name="out_proj",
        grid=(t // TM,),
        in_specs=[pl.BlockSpec((TM, d), row), pl.BlockSpec((TM, d), row),
                  pl.BlockSpec((TM, d), lambda i: (i, GA_GROUP)), pl.BlockSpec((TM, d), lambda i: (i, GB_GROUP)),
                  pl.BlockSpec((TM, d), row), pl.BlockSpec((TM, d), row),
                  _resident((d, d)), _resident((d, d)), _resident((d, d)), _const((1, d))],
        out_specs=[pl.BlockSpec((TM, d), row), pl.BlockSpec((TM, d), row), pl.BlockSpec((TM, 2 * d), row),
                   pl.BlockSpec((TM, d), row), _const((1, 1)), _const((1, d)),
                   _const((d, d)), _const((d, d)), _const((d, d))],
        out_shape=[jax.ShapeDtypeStruct((t, d), F32), jax.ShapeDtypeStruct((t, d), F32),
                   jax.ShapeDtypeStruct((t, 2 * d), BF16), jax.ShapeDtypeStruct((t, d), F32),
                   jax.ShapeDtypeStruct((1, 1), F32), jax.ShapeDtypeStruct((1, d), F32),
                   jax.ShapeDtypeStruct((d, d), F32), jax.ShapeDtypeStruct((d, d), F32),
                   jax.ShapeDtypeStruct((d, d), F32)],
        compiler_params=_params(("arbitrary",)),
    )(ya, yb, g_pre, g_pre, x2d, tgt, w_og, w_osb, w_out, g_f)


def _dproj_pieces(d):
    return [(0, 0, 3), (1, 3, 1), (2, 4, 1), (3, 5, 1), (4, 6, 1), (5, 7, 2)]


def _in_proj_bwd_x(pieces, wg, x2d, g_in, dx2, gw16):
    t, d = x2d.shape
    ncol = wg.shape[2]
    segs = _segments(d, ncol)
    layout = _dproj_pieces(d)
    nsteps = t // TMX

    def body(da_ref, dq_ref, dk_ref, dv_ref, dzb_ref, dg_ref, w_ref, x_ref, g_ref, dx2_ref, gw16_ref,
             gx_ref, gn_ref, recv_ref, send_sems, recv_sems):
        x_pos, y_pos, c_pos, chips = _place()

        def share(k, chunk):
            px, py = chips[k]
            return pltpu.make_async_remote_copy(
                src_ref=gw16_ref.at[:, chunk * ncol:(chunk + 1) * ncol], dst_ref=recv_ref.at[k],
                send_sem=send_sems.at[k], recv_sem=recv_sems.at[k], device_id=(px, py, c_pos), device_id_type=MESH)

        @pl.when(pl.program_id(0) == 0)
        def _():
            gn_ref[...] = jnp.zeros_like(gn_ref)
            for k, (px, py) in enumerate(chips):
                for chunk in range(N_CHIP):
                    @pl.when(2 * px + py == chunk)
                    def _(k=k, chunk=chunk):
                        share(k, chunk).start()

        @pl.when(pl.program_id(0) == nsteps - 1)
        def _():
            for k in range(N_CHIP - 1):
                share(k, 0).wait()

        refs = (da_ref, dq_ref, dk_ref, dv_ref, dzb_ref, dg_ref)
        dh = jnp.zeros((TMX, d), F32)
        for chip, c0, grp, s0, width in segs:
            piece, first, _ = next(p for p in layout if p[1] <= grp < p[1] + p[2])
            off = (grp - first) * d + s0
            dh = dh + lax.dot_general(refs[piece][:, off:off + width], w_ref[chip, :, c0:c0 + width], NT,
                                      preferred_element_type=F32)
        x = x_ref[...]
        r = lax.rsqrt(jnp.mean(x * x, axis=-1, keepdims=True) + EPS)
        n = x * r
        gn_ref[...] += jnp.sum(dh * n, axis=0, keepdims=True)
        dn = dh * g_ref[...]
        gx_ref[...] = dx2_ref[...] + r * (dn - n * jnp.mean(dn * n, axis=-1, keepdims=True))

    row = lambda i: (i, 0)
    return pl.pallas_call(
        body, name="in_proj_bwd_x",
        grid=(nsteps,),
        in_specs=[pl.BlockSpec((TMX, p.shape[1]), row) for p in pieces]
        + [_resident(wg.shape), pl.BlockSpec((TMX, d), row), _const((1, d)), pl.BlockSpec((TMX, d), row), ANY],
        out_specs=[pl.BlockSpec((TMX, d), row), _const((1, d)), ANY],
        out_shape=[jax.ShapeDtypeStruct((t, d), F32), jax.ShapeDtypeStruct((1, d), F32),
                   jax.ShapeDtypeStruct((N_CHIP - 1, gw16.shape[0], ncol), BF16)],
        scratch_shapes=[pltpu.SemaphoreType.DMA((N_CHIP - 1,)), pltpu.SemaphoreType.DMA((N_CHIP - 1,))],
        compiler_params=_params(("arbitrary",)),
    )(*pieces, wg, x2d, g_in, dx2, gw16)


def _in_proj_bwd_w(h_t, pieces, mats16, pack):
    d, t = h_t.shape
    nk = t // TKW
    half = d // 2
    layout = _dproj_pieces(d)
    n_mats = len(mats16)
    n_dev = 2 * N_CHIP
    flips = [(dx, dy, dc) for dx in (0, 1) for dy in (0, 1) for dc in (0, 1)][1:]

    def body(ht_ref, da_ref, dq_ref, dk_ref, dv_ref, dzb_ref, dg_ref, *rest):
        mat_refs, pack_ref = rest[:n_mats], rest[n_mats]
        gw_ref, sib_ref = rest[n_mats + 1:n_mats + 3]
        recv_refs, slots_ref = rest[n_mats + 3:2 * n_mats + 3], rest[2 * n_mats + 3]
        acc, stage, mat_send, mat_recv, pack_send, pack_recv, own_sem, stage_send, sib_recv = rest[2 * n_mats + 4:]
        s = pl.program_id(0)
        i = pl.program_id(1)
        x_pos, y_pos, c_pos, chips = _place()
        me = 4 * x_pos + 2 * y_pos + c_pos

        def to_sibling(slot, group):
            return pltpu.make_async_remote_copy(
                src_ref=stage.at[slot], dst_ref=sib_ref.at[:, group * d:(group + 1) * d],
                send_sem=stage_send.at[slot], recv_sem=sib_recv, device_id=(x_pos, y_pos, 1 - c_pos), device_id_type=MESH)

        def exchanges():
            cps = []
            for k, (px, py) in enumerate(chips):
                for a in range(n_mats):
                    cps.append(pltpu.make_async_remote_copy(
                        src_ref=mat_refs[a].at[2 * px + py], dst_ref=recv_refs[a].at[k],
                        send_sem=mat_send.at[a, k], recv_sem=mat_recv.at[a, k],
                        device_id=(px, py, c_pos), device_id_type=MESH))
            for k, (dx, dy, dc) in enumerate(flips):
                peer = (1 - x_pos if dx else x_pos, 1 - y_pos if dy else y_pos, 1 - c_pos if dc else c_pos)
                cps.append(pltpu.make_async_remote_copy(
                    src_ref=pack_ref, dst_ref=slots_ref.at[me], send_sem=pack_send.at[k], recv_sem=pack_recv.at[k],
                    device_id=peer, device_id_type=MESH))
            return cps, pltpu.make_async_copy(pack_ref, slots_ref.at[me], own_sem)

        @pl.when((s == 0) & (i == 0))
        def _():
            cps, own = exchanges()
            own.start()
            for cp in cps:
                cp.start()

        @pl.when(i == 0)
        def _():
            acc[...] = jnp.zeros_like(acc)

        refs = (da_ref, dq_ref, dk_ref, dv_ref, dzb_ref, dg_ref)
        for piece, first, count in layout:
            @pl.when((s >= first) & (s < first + count))
            def _(piece=piece):
                acc[...] += jnp.dot(ht_ref[...], refs[piece][...], preferred_element_type=F32)

        @pl.when(i == nk - 1)
        def _():
            gw_ref[...] = acc[...]
            for slot in range(2):
                @pl.when(s % 2 == slot)
                def _(slot=slot):
                    @pl.when(s >= 2)
                    def _():
                        to_sibling(slot, 0).wait_send()
                    for other in range(2):
                        @pl.when(c_pos == 1 - other)
                        def _(other=other):
                            stage[slot] = acc[other * half:(other + 1) * half, :].astype(BF16)
                    for group in range(N_SPLIT):
                        @pl.when(s == group)
                        def _(group=group):
                            to_sibling(slot, group).start()

        @pl.when((s == N_SPLIT - 1) & (i == nk - 1))
        def _():
            for slot in range(2):
                to_sibling(slot, 0).wait_send()
            pltpu.make_async_remote_copy(
                src_ref=sib_ref, dst_ref=sib_ref, send_sem=stage_send.at[0], recv_sem=sib_recv,
                device_id=(x_pos, y_pos, c_pos), device_id_type=MESH).wait_recv()
            cps, own = exchanges()
            own.wait()
            for cp in cps:
                cp.wait()

    def piece_spec(p, first, count):
        def index(s, i):
            mine = (s >= first) & (s < first + count)
            return jnp.where(mine, i, 0), jnp.where(mine, s - first, 0)
        return pl.BlockSpec((TKW, d), index)

    col_blk = pl.BlockSpec((d, d), lambda s, i: (0, s))
    outs = pl.pallas_call(
        body, name="in_proj_bwd_w",
        grid=(N_SPLIT, nk),
        in_specs=[pl.BlockSpec((d, TKW), lambda s, i: (0, i))] + [piece_spec(*p) for p in layout] + [ANY] * (n_mats + 1),
        out_specs=[col_blk, ANY] + [ANY] * (n_mats + 1),
        out_shape=[jax.ShapeDtypeStruct((d, N_SPLIT * d), F32), jax.ShapeDtypeStruct((half, N_SPLIT * d), BF16)]
        + [jax.ShapeDtypeStruct((N_CHIP - 1,) + m.shape[1:], BF16) for m in mats16]
        + [jax.ShapeDtypeStruct((n_dev,) + pack.shape, F32)],
        scratch_shapes=[pltpu.VMEM((d, d), F32), pltpu.VMEM((2, half, d), BF16),
                        pltpu.SemaphoreType.DMA((n_mats, N_CHIP - 1)), pltpu.SemaphoreType.DMA((n_mats, N_CHIP - 1)),
                        pltpu.SemaphoreType.DMA((n_dev - 1,)), pltpu.SemaphoreType.DMA((n_dev - 1,)),
                        pltpu.SemaphoreType.DMA, pltpu.SemaphoreType.DMA((2,)), pltpu.SemaphoreType.DMA],
        compiler_params=_params(("arbitrary", "arbitrary")),
    )(h_t, *pieces, *mats16, pack)
    return outs[0], outs[1], outs[2:2 + n_mats], outs[2 + n_mats]


def _local_step(proj, qkv, x2d, tgt2d, bsz, seq, norm_v, w_s, b_s, w_og, w_osb, w_out, norm_final):
    d = x2d.shape[1]
    chunk = w_s.shape[-1]
    causal = jnp.tril(jnp.ones((chunk, chunk), dtype=bool))
    wm = jnp.where(causal[None], w_s, 0.0).astype(BF16)
    wm_t = jnp.swapaxes(wm, 1, 2)
    b_t = b_s.T

    ya = _branch_a_fwd(proj, norm_v, wm, b_t)
    o, yb = _attn_fwd(qkv, proj, bsz, seq)
    dya, dyb, dg, dx2, loss, g_nf, g_wog, g_wosb, g_wout = _out_proj(
        ya, yb, proj, x2d, tgt2d, w_og, w_osb, w_out, norm_final.reshape(1, d))
    dq, dk, dv, dzb = _attn_bwd(qkv, proj, o, dyb, bsz, seq)
    d_a, g_ws, g_bt, g_nv = _branch_a_bwd(proj, dya, norm_v, wm, wm_t, b_t)
    g_ws = jnp.where(causal[None], g_ws, 0.0)
    return loss, (d_a, dq, dk, dv, dzb, dg), dx2, g_nv, g_ws, g_bt.T, g_wog, g_wosb, g_wout, g_nf


def _row_tile(rows):
    return next(r for r in (128, 64, 32, 16, 8) if rows % r == 0)


def _cast_bf16(arrs):
    n = len(arrs)

    def body(*refs):
        for a_ref, o_ref in zip(refs[:n], refs[n:]):
            o_ref[...] = a_ref[...].astype(BF16)

    specs = [pl.BlockSpec((a.shape[0] // CAST_STEPS, a.shape[1]), lambda i: (i, 0)) for a in arrs]
    return pl.pallas_call(
        body, name="cast_bf16", grid=(CAST_STEPS,),
        in_specs=specs, out_specs=specs,
        out_shape=[jax.ShapeDtypeStruct(a.shape, BF16) for a in arrs],
        compiler_params=_params(("arbitrary",)),
    )(*arrs)


def _chip_half(full, from_sibling, core, chip, tile):
    half, n = from_sibling.shape

    def body(where_ref, own_ref, sib_ref, o32_ref, o16_ref):
        total = own_ref[...] + sib_ref[...].astype(F32)
        o16_ref[...] = total.astype(BF16)

        @pl.when(pl.program_id(0) == where_ref[1])
        def _():
            o32_ref[...] = total

    blk = pl.BlockSpec((half, tile), lambda j, where_ref: (0, j))
    return pl.pallas_call(
        body, name="chip_half",
        grid_spec=pltpu.PrefetchScalarGridSpec(
            num_scalar_prefetch=1, grid=(n // tile,),
            in_specs=[pl.BlockSpec((half, tile), lambda j, where_ref: (where_ref[0], j)), blk],
            out_specs=[pl.BlockSpec((half, tile), lambda j, where_ref: (0, 0)), blk]),
        out_shape=[jax.ShapeDtypeStruct((half, tile), F32), jax.ShapeDtypeStruct((half, n), BF16)],
        compiler_params=_params(("arbitrary",)),
    )(jnp.stack([core, chip]).astype(jnp.int32), full, from_sibling)


def _add_received(fulls, recvs, chip, by_cols):
    n = len(fulls)
    _, rows, cols = recvs[0].shape
    tr = _row_tile(rows)
    nb = rows // tr

    def body(chip_ref, *refs):
        for own_ref, recv_ref, o_ref in zip(refs[:n], refs[n:2 * n], refs[2 * n:]):
            s = own_ref[...]
            for k in range(N_CHIP - 1):
                s = s + recv_ref[k].astype(F32)
            o_ref[...] = s

    own_map = (lambda i, chip_ref: (i, chip_ref[0])) if by_cols else (lambda i, chip_ref: (chip_ref[0] * nb + i, 0))
    return pl.pallas_call(
        body, name="add_received",
        grid_spec=pltpu.PrefetchScalarGridSpec(
            num_scalar_prefetch=1, grid=(nb,),
            in_specs=[pl.BlockSpec((tr, cols), own_map)] * n
            + [pl.BlockSpec((N_CHIP - 1, tr, cols), lambda i, chip_ref: (0, i, 0))] * n,
            out_specs=[pl.BlockSpec((tr, cols), lambda i, chip_ref: (i, 0))] * n),
        out_shape=[jax.ShapeDtypeStruct((rows, cols), F32)] * n,
        compiler_params=_params(("arbitrary",)),
    )(chip.reshape(1).astype(jnp.int32), *fulls, *recvs)


def _adamw_math(w, m, v, g):
    new_m = ADAM_B1 * m + (1.0 - ADAM_B1) * g
    new_v = ADAM_B2 * v + (1.0 - ADAM_B2) * (g * g)
    m_hat = new_m / (1.0 - ADAM_B1 ** ADAM_STEP)
    v_hat = new_v / (1.0 - ADAM_B2 ** ADAM_STEP)
    return -ADAM_LR * (m_hat / (jnp.sqrt(v_hat) + ADAM_EPS) + ADAM_WD * w), new_m, new_v


def _adamw_pairs(ws, ms, vs, mines, theirs):
    n = len(ws)
    rows, cols = ws[0].shape
    tr = _row_tile(rows)

    def body(*refs):
        ins, outs = refs[:5 * n], refs[5 * n:]
        for a in range(n):
            w_ref, m_ref, v_ref, mine_ref, theirs_ref = ins[a::n]
            g_ref, d_ref, nm_ref, nv_ref = outs[4 * a:4 * a + 4]
            g = mine_ref[...] + theirs_ref[...]
            g_ref[...] = g
            d_ref[...], nm_ref[...], nv_ref[...] = _adamw_math(w_ref[...], m_ref[...], v_ref[...], g)

    spec = pl.BlockSpec((tr, cols), lambda i: (i, 0))
    out = jax.ShapeDtypeStruct((rows, cols), F32)
    outs = pl.pallas_call(
        body, name="adamw_pairs", grid=(rows // tr,),
        in_specs=[spec] * (5 * n), out_specs=[spec] * (4 * n), out_shape=[out] * (4 * n),
        compiler_params=_params(("arbitrary",)),
    )(*ws, *ms, *vs, *mines, *theirs)
    return [outs[4 * a:4 * a + 4] for a in range(n)]


def _adamw_halves(w, m, v, mine, theirs, core):
    rows, cols = w.shape
    tr = _row_tile(rows // 2)
    per_half = rows // 2 // tr

    def body(core_ref, w_ref, m_ref, v_ref, mine_ref, theirs_ref, g_ref, d_ref, nm_ref, nv_ref):
        is_mine = pl.program_id(0) // per_half == core_ref[0]
        for part, cond in ((mine_ref, is_mine), (theirs_ref, jnp.logical_not(is_mine))):
            @pl.when(cond)
            def _(part=part):
                g = part[...]
                g_ref[...] = g
                d_ref[...], nm_ref[...], nv_ref[...] = _adamw_math(w_ref[...], m_ref[...], v_ref[...], g)

    spec = pl.BlockSpec((tr, cols), lambda i, core_ref: (i, 0))

    def half_spec(own):
        def index(i, core_ref):
            in_core_half = i // per_half == core_ref[0]
            here = in_core_half if own else jnp.logical_not(in_core_half)
            return jnp.where(here, i % per_half, 0), 0
        return pl.BlockSpec((tr, cols), index)

    out = jax.ShapeDtypeStruct(w.shape, F32)
    return pl.pallas_call(
        body, name="adamw_halves",
        grid_spec=pltpu.PrefetchScalarGridSpec(
            num_scalar_prefetch=1, grid=(rows // tr,),
            in_specs=[spec] * 3 + [half_spec(True), half_spec(False)], out_specs=[spec] * 4),
        out_shape=[out] * 4,
        compiler_params=_params(("arbitrary",)),
    )(core.reshape(1).astype(jnp.int32), w, m, v, mine, theirs)


def _adamw_small(w, m, v, slots_head, slots_tail):
    n_dev, p0, _ = slots_head.shape

    def body(w_ref, m_ref, v_ref, head_ref, tail_ref, g_ref, d_ref, nm_ref, nv_ref):
        for ref, rows in ((head_ref, slice(0, p0)), (tail_ref, slice(p0, w.shape[0]))):
            g = ref[0]
            for i in range(1, n_dev):
                g = g + ref[i]
            g_ref[rows, :] = g
            d_ref[rows, :], nm_ref[rows, :], nv_ref[rows, :] = _adamw_math(w_ref[rows, :], m_ref[rows, :], v_ref[rows, :], g)

    vmem = pl.BlockSpec(memory_space=pltpu.VMEM)
    out = jax.ShapeDtypeStruct(w.shape, F32)
    return pl.pallas_call(
        body, name="adamw_small", in_specs=[vmem] * 5, out_specs=[vmem] * 4, out_shape=[out] * 4,
        compiler_params=pltpu.CompilerParams(vmem_limit_bytes=VMEM_LIMIT),
    )(w, m, v, slots_head, slots_tail)


ANY = pl.BlockSpec(memory_space=pl.ANY)


def _place():
    x, y, c = lax.axis_index("x"), lax.axis_index("y"), lax.axis_index("c")
    other_chips = [(1 - x, y), (x, 1 - y), (1 - x, 1 - y)]
    return x, y, c, other_chips


def _swap_and_gather(arrs, pack):
    n = len(arrs)
    n_dev = 2 * N_CHIP
    flips = [(dx, dy, dc) for dx in (0, 1) for dy in (0, 1) for dc in (0, 1)][1:]

    def body(*refs):
        ins, pack_ref = refs[:n], refs[n]
        outs, slots_ref = refs[n + 1:2 * n + 1], refs[2 * n + 1]
        send_sems, recv_sems, pack_send, pack_recv, own_sem = refs[2 * n + 2:]
        x, y, c, _ = _place()
        me = 4 * x + 2 * y + c
        own = pltpu.make_async_copy(pack_ref, slots_ref.at[me], own_sem)
        own.start()
        copies = []
        for k, (dx, dy, dc) in enumerate(flips):
            peer = (1 - x if dx else x, 1 - y if dy else y, 1 - c if dc else c)
            copies.append(pltpu.make_async_remote_copy(
                src_ref=pack_ref, dst_ref=slots_ref.at[me], send_sem=pack_send.at[k], recv_sem=pack_recv.at[k],
                device_id=peer, device_id_type=MESH))
        copies += [pltpu.make_async_remote_copy(
            src_ref=ins[a], dst_ref=outs[a], send_sem=send_sems.at[a], recv_sem=recv_sems.at[a],
            device_id=(x, y, 1 - c), device_id_type=MESH) for a in range(n)]
        for cp in copies:
            cp.start()
        for cp in copies:
            cp.wait()
        own.wait()

    return pl.pallas_call(
        body, name="swap_and_gather",
        in_specs=[ANY] * (n + 1), out_specs=[ANY] * (n + 1),
        out_shape=[jax.ShapeDtypeStruct(a.shape, a.dtype) for a in arrs] + [jax.ShapeDtypeStruct((n_dev,) + pack.shape, F32)],
        scratch_shapes=[pltpu.SemaphoreType.DMA((n,)), pltpu.SemaphoreType.DMA((n,)),
                        pltpu.SemaphoreType.DMA((n_dev - 1,)), pltpu.SemaphoreType.DMA((n_dev - 1,)),
                        pltpu.SemaphoreType.DMA],
    )(*arrs, pack)


SLAB = 8


def _slab(vec, d):
    return jnp.pad(vec.reshape(1, d), ((0, SLAB - 1), (0, 0)))


def _pack_tail(vec_nv, b_s, vec_nf, w_s, scalar=None):
    d = vec_nv.shape[-1]
    extra = jnp.zeros((1, d), F32) if scalar is None else jnp.pad(scalar, ((0, 0), (0, d - 1)))
    return jnp.concatenate([_slab(v, d) for v in (vec_nv, b_s, vec_nf, extra)] + [w_s.reshape(-1, d)], axis=0)


def _pack_small(vec_nin, vec_nv, b_s, vec_nf, w_s):
    return jnp.concatenate([_slab(vec_nin, vec_nin.shape[-1]), _pack_tail(vec_nv, b_s, vec_nf, w_s)], axis=0)


def _unpack_small(pack, w_s_shape, b_s_shape):
    return (pack[0:1], pack[SLAB:SLAB + 1], pack[2 * SLAB].reshape(b_s_shape), pack[3 * SLAB],
            pack[5 * SLAB:].reshape(w_s_shape), pack[4 * SLAB, 0])


def kernel(x, norm_in, w_in, norm_v, w_s, b_s, w_o_gmlp, w_o_sb, w_out, norm_final, loss_target, m_norm_in, m_w_in, m_norm_v, m_w_s, m_b_s, m_w_o_gmlp, m_w_o_sb, m_w_out, m_norm_final, v_norm_in, v_w_in, v_norm_v, v_w_s, v_b_s, v_w_o_gmlp, v_w_o_sb, v_w_out, v_norm_final):
    d = x.shape[-1]
    ncol = w_in.shape[-1]
    nrow = w_o_gmlp.shape[-2]
    chip = 2 * lax.axis_index("x") + lax.axis_index("y")

    bsz, seq, _ = x.shape
    x2d = x.reshape(bsz * seq, d)
    shards = [w_in[0], w_o_gmlp[0], w_o_sb[0], w_out[0]]
    halves = [s16.reshape(2, s16.shape[0] // 2, s16.shape[1]) for s16 in _cast_bf16(shards)]
    proj, qkv, h_t, (wg, w_og, w_osb, w_o) = _in_proj_fwd(x2d, norm_in, halves)
    wg = wg.reshape(N_CHIP, d, ncol)

    loss, pieces, dx2, g_nv, g_ws, g_bs, g_wog, g_wosb, g_wout, g_nf = _local_step(
        proj, qkv, x2d, loss_target.reshape(bsz * seq, d), bsz, seq, norm_v, w_s[0], b_s[0],
        w_og.reshape(d, d), w_osb.reshape(d, d), w_o.reshape(d, d), norm_final)

    mats = [g_wog, g_wosb, g_wout]
    mats16 = [g16.reshape(N_CHIP, nrow, d) for g16 in _cast_bf16(mats)]
    g_win, from_sibling, recv_mats, slots_tail = _in_proj_bwd_w(
        h_t, pieces, mats16, _pack_tail(g_nv, g_bs, g_nf, g_ws, loss))
    core = lax.axis_index("c")
    half_own, half_win16 = _chip_half(g_win, from_sibling, core, chip, ncol)
    grad_x, g_nin, recv_win = _in_proj_bwd_x(pieces, wg, x2d, norm_in, dx2, half_win16)
    grad_x = grad_x.reshape(bsz, seq, d)

    sums = _add_received([half_own], [recv_win], jnp.zeros((), jnp.int32), True) + _add_received(
        mats, recv_mats, chip, False)
    *sibling_sums, slots_head = _swap_and_gather(sums, _slab(g_nin, d))
    stats = [_adamw_halves(w_in[0], m_w_in[0], v_w_in[0], sums[0], sibling_sums[0], core)] + _adamw_pairs(
        shards[1:], [m_w_o_gmlp[0], m_w_o_sb[0], m_w_out[0]], [v_w_o_gmlp[0], v_w_o_sb[0], v_w_out[0]],
        sums[1:], sibling_sums[1:])
    (gw_in, dw_in, nm_in, nv_in), (gw_og, dw_og, nm_og, nv_og), (gw_osb, dw_osb, nm_osb, nv_osb), \
        (gw_out, dw_out, nm_out, nv_out) = stats

    gs, ds, ms, vs = _adamw_small(
        _pack_small(norm_in, norm_v, b_s[0], norm_final, w_s[0]),
        _pack_small(m_norm_in, m_norm_v, m_b_s[0], m_norm_final, m_w_s[0]),
        _pack_small(v_norm_in, v_norm_v, v_b_s[0], v_norm_final, v_w_s[0]), slots_head, slots_tail)

    def small(pack):
        nin, nv, bs, nf, ws, _ = _unpack_small(pack, w_s.shape, b_s.shape)
        return nin, nv, ws, bs, nf

    loss = _unpack_small(gs, w_s.shape, b_s.shape)[-1]
    out = []
    for small_pack, win, wog, wosb, wout in ((gs, gw_in, gw_og, gw_osb, gw_out), (ds, dw_in, dw_og, dw_osb, dw_out),
                                             (ms, nm_in, nm_og, nm_osb, nm_out), (vs, nv_in, nv_og, nv_osb, nv_out)):
        nin, nv, ws, bs, nf = small(small_pack)
        out += [nin, win[None], nv, ws, bs, wog[None], wosb[None], wout[None], nf]
    return (loss, grad_x, *out)
```

```python
import functools
import math

import jax
import jax.numpy as jnp
from jax import lax
from jax.experimental import pallas as pl
from jax.experimental.pallas import tpu as pltpu

F32 = jnp.float32
BF16 = jnp.bfloat16
EPS = 1e-6
HEADS = 8
N_SPLIT = 9
N_CHIP = 4
MESH = pl.DeviceIdType.MESH

ADAM_LR = 0.001
ADAM_B1 = 0.9
ADAM_B2 = 0.999
ADAM_EPS = 1e-08
ADAM_WD = 0.01
ADAM_STEP = 10

VMEM_LIMIT = 56 * 2 ** 20
TM = 256
TMF = 512
TMX = 512
ATT_T = 256
ATT_HP = 4
TKW = 1024
CHUNKS_PER_STEP = 4
CAST_STEPS = 8

NT = (((1,), (1,)), ((), ()))
TN = (((0,), (0,)), ((), ()))


def _params(sem):
    return pltpu.CompilerParams(dimension_semantics=sem, vmem_limit_bytes=VMEM_LIMIT)


def _resident(shape):
    nd = len(shape)
    return pl.BlockSpec(shape, lambda *_: (0,) * nd, pipeline_mode=pl.Buffered(1))


def _const(shape):
    nd = len(shape)
    return pl.BlockSpec(shape, lambda *_: (0,) * nd)


def _segments(d, ncol):
    segs = []
    edges = sorted({j * ncol for j in range(N_CHIP + 1)} | {s * d for s in range(N_SPLIT + 1)})
    for lo, hi in zip(edges[:-1], edges[1:]):
        segs.append((lo // ncol, lo % ncol, lo // d, lo % d, hi - lo))
    return segs


def _sigmoid(x):
    return 0.5 * jnp.tanh(0.5 * x) + 0.5


_GELU_C = math.sqrt(2.0 / math.pi)


_GELU_CA = _GELU_C * 0.044715


def _gelu(x):
    return x * (0.5 * jnp.tanh(x * (_GELU_C + _GELU_CA * (x * x))) + 0.5)


def _gelu_and_grad(x):
    x2 = x * x
    u = 0.5 * jnp.tanh(x * (_GELU_C + _GELU_CA * x2)) + 0.5
    slope = (1.0 - u) * (x * (_GELU_C + (3.0 * _GELU_CA) * x2))
    return x * u, u * (2.0 * slope + 1.0)


def _split_bf16(a):
    hi = a.astype(BF16)
    lo = (a - hi.astype(F32)).astype(BF16)
    return hi, lo


def _in_proj_fwd(x2d, g_in, halves):
    t, d = x2d.shape
    n = len(halves)
    ncol = halves[0].shape[2]
    n_row = t // TMF
    last = n_row - 1
    assert halves[0].shape[1] * 2 == d
    qkv_parts = {j: (max(j * ncol, 3 * d) - j * ncol, max(j * ncol, 3 * d) - 3 * d)
                 for j in range(N_CHIP) if min((j + 1) * ncol, 6 * d) > max(j * ncol, 3 * d)}
    qkv_w = 3 * d // len(qkv_parts)
    assert all(min((j + 1) * ncol, 6 * d) - max(j * ncol, 3 * d) == qkv_w and q0 % qkv_w == 0
               for j, (_, q0) in qkv_parts.items())

    def body(order_ref, qrow_ref, qcol_ref, x_ref, g_ref, *rest):
        ins = rest[:n]
        proj_ref, qkv_ref, ht_ref = rest[n:n + 3]
        outs = rest[n + 3:2 * n + 3]
        wbuf, h_all, send_sems, recv_sems, local_sems, load_sem = rest[2 * n + 3:]
        phase = pl.program_id(0)
        i = pl.program_id(1)
        x_pos, y_pos, c_pos, chips = _place()
        sibling = (x_pos, y_pos, 1 - c_pos)
        me = (x_pos, y_pos, c_pos)
        my_chip = 2 * x_pos + y_pos

        def copy(a, k, block, to, src=None):
            return pltpu.make_async_remote_copy(
                src_ref=outs[a].at[block] if src is None else src, dst_ref=outs[a].at[block],
                send_sem=send_sems.at[a, k], recv_sem=recv_sems.at[a, k], device_id=to, device_id_type=MESH)

        def local(a):
            return pltpu.make_async_copy(ins[a], outs[a].at[pl.ds(2 * my_chip, 2)], local_sems.at[a])

        def load(src, first, slot):
            for half in range(2):
                cp = pltpu.make_async_copy(src.at[first + half], wbuf.at[slot, pl.ds(half * (d // 2), d // 2)], load_sem)
                cp.start()
                cp.wait()

        def relay(a, k, block, piece, to):
            rows = halves[a].shape[1] // 2
            ref = outs[a].at[block, pl.ds(piece * rows, rows)]
            return pltpu.make_async_remote_copy(
                src_ref=ref, dst_ref=ref, send_sem=send_sems.at[a, k], recv_sem=recv_sems.at[a, k],
                device_id=to, device_id_type=MESH)

        x_nbr, y_nbr, diagonal = chips
        first_block = lambda chip_xy: 2 * (2 * chip_xy[0] + chip_xy[1])

        def neighbours_arrived(arrays):
            from_x, from_y = first_block(x_nbr) + c_pos, first_block(y_nbr) + c_pos
            for a in arrays:
                copy(a, 0, from_x, me).wait_recv()
                copy(a, 1, from_y, me).wait_recv()
                relay(a, 2, from_x, 0, (*y_nbr, c_pos)).start()
                relay(a, 3, from_y, 1, (*x_nbr, c_pos)).start()
                copy(a, 4, from_x, sibling).start()
                copy(a, 5, from_y, sibling).start()

        def diagonal_arrived(arrays):
            from_diagonal = first_block(diagonal) + c_pos
            for a in arrays:
                relay(a, 2, from_diagonal, 0, me).wait_recv()
                relay(a, 3, from_diagonal, 1, me).wait_recv()
                copy(a, 6, from_diagonal, sibling).start()

        def from_sibling(a, k, chip_xy):
            copy(a, k, first_block(chip_xy) + 1 - c_pos, me).wait_recv()

        @pl.when((phase == 0) & (i == 0))
        def _():
            for a in range(n):
                local(a).start()
            for k, (px, py) in enumerate((x_nbr, y_nbr)):
                for a in range(n):
                    copy(a, k, 2 * my_chip + c_pos, (px, py, c_pos), src=ins[a].at[c_pos]).start()
            load(ins[0], 0, 0)

        @pl.when((phase == 1) & (i == 0))
        def _():
            neighbours_arrived([0])
            from_sibling(0, 4, x_nbr)
            load(outs[0], first_block(x_nbr), 1)

        @pl.when((phase == 2) & (i == 0))
        def _():
            from_sibling(0, 5, y_nbr)
            load(outs[0], first_block(y_nbr), 0)
            neighbours_arrived(range(1, n))

        @pl.when((phase == 3) & (i == 0))
        def _():
            diagonal_arrived(range(n))
            from_sibling(0, 6, diagonal)
            load(outs[0], first_block(diagonal), 1)

        @pl.when(phase == 0)
        def _():
            x = x_ref[...]
            r = lax.rsqrt(jnp.mean(x * x, axis=-1, keepdims=True) + EPS)
            hf = x * r * g_ref[...]
            h_all[i] = hf.astype(BF16)
            ht_ref[...] = hf.T.astype(BF16)

        for slot in range(2):
            @pl.when(phase % 2 == slot)
            def _(slot=slot):
                proj_ref[...] = jnp.dot(h_all[i], wbuf[slot], preferred_element_type=F32)

        for chunk, (c0, _) in qkv_parts.items():
            @pl.when(order_ref[phase] == chunk)
            def _(c0=c0):
                qkv_ref[...] = proj_ref[:, c0:c0 + qkv_w].astype(BF16)

        @pl.when((phase == N_CHIP - 1) & (i == n_row - 1))
        def _():
            for a in range(1, n):
                for k, chip_xy in ((4, x_nbr), (5, y_nbr), (6, diagonal)):
                    from_sibling(a, k, chip_xy)
            for a in range(n):
                for k in (0, 1, 4, 5, 6):
                    copy(a, k, 0, me).wait_send()
                for k in (2, 3):
                    relay(a, k, 0, 0, me).wait_send()
                local(a).wait()

    x_pos, y_pos = lax.axis_index("x"), lax.axis_index("y")
    order = jnp.stack([2 * x_pos + y_pos, 2 * (1 - x_pos) + y_pos, 2 * x_pos + 1 - y_pos,
                       2 * (1 - x_pos) + 1 - y_pos]).astype(jnp.int32)
    holds = [functools.reduce(jnp.logical_or, [order[p] == j for j in qkv_parts]) for p in range(N_CHIP)]
    col = [sum(jnp.where(order[p] == j, q0 // qkv_w, 0) for j, (_, q0) in qkv_parts.items()) for p in range(N_CHIP)]
    cur = col[-1]
    for p in reversed(range(N_CHIP - 1)):
        cur = jnp.where(holds[p], col[p], cur)
    seen = jnp.bool_(False)
    qrow, qcol = [], []
    for p in range(N_CHIP):
        cur = jnp.where(holds[p], col[p], cur)
        qrow.append(jnp.where(holds[p], -1, jnp.where(seen, last, 0)))
        qcol.append(cur)
        seen = seen | holds[p]
    qrow = jnp.stack(qrow).astype(jnp.int32)
    qcol = jnp.stack(qcol).astype(jnp.int32)

    outs = pl.pallas_call(
        body, name="in_proj_fwd",
        grid_spec=pltpu.PrefetchScalarGridSpec(
            num_scalar_prefetch=3, grid=(N_CHIP, n_row),
            in_specs=[pl.BlockSpec((TMF, d), lambda p, i, order, qrow, qcol: (jnp.where(p == 0, i, last), 0)),
                      pl.BlockSpec((1, d), lambda p, i, order, qrow, qcol: (0, 0))] + [ANY] * n,
            out_specs=[pl.BlockSpec((TMF, ncol), lambda p, i, order, qrow, qcol: (i, order[p])),
                       pl.BlockSpec((TMF, qkv_w),
                                    lambda p, i, order, qrow, qcol: (jnp.where(qrow[p] < 0, i, qrow[p]), qcol[p])),
                       pl.BlockSpec((d, TMF), lambda p, i, order, qrow, qcol: (0, jnp.where(p == 0, i, last)))]
            + [ANY] * n,
            scratch_shapes=[pltpu.VMEM((2, d, ncol), BF16), pltpu.VMEM((n_row, TMF, d), BF16),
                            pltpu.SemaphoreType.DMA((n, 7)), pltpu.SemaphoreType.DMA((n, 7)),
                            pltpu.SemaphoreType.DMA((n,)), pltpu.SemaphoreType.DMA]),
        out_shape=[jax.ShapeDtypeStruct((t, N_CHIP * ncol), F32), jax.ShapeDtypeStruct((t, 3 * d), BF16),
                   jax.ShapeDtypeStruct((d, t), BF16)]
        + [jax.ShapeDtypeStruct((2 * N_CHIP,) + hv.shape[1:], BF16) for hv in halves],
        compiler_params=_params(("arbitrary", "arbitrary")),
    )(order, qrow, qcol, x2d, g_in, *halves)
    return outs[0], outs[1], outs[2], outs[3:]


def _branch_a_fwd(a_pre, g_v, wm, b_t):
    t = a_pre.shape[0]
    d = g_v.shape[1]
    d3 = 3 * d
    ng, chunk, _ = wm.shape
    cw = d // ng

    per_step = CHUNKS_PER_STEP if t % (CHUNKS_PER_STEP * chunk) == 0 else 1

    def body(a_ref, gv_ref, wm_ref, bt_ref, ya_ref):
        for n in range(per_step):
            rows = slice(n * chunk, (n + 1) * chunk)
            ua = _gelu(a_ref[rows, 0:d])
            vg = _gelu(a_ref[rows, d:2 * d])
            za = a_ref[rows, 2 * d:3 * d]
            rv = lax.rsqrt(jnp.mean(vg * vg, axis=-1, keepdims=True) + EPS)
            va = (vg * rv * gv_ref[...]).astype(BF16)
            gate = ua * (za * _sigmoid(za))
            for g in range(ng):
                sl = slice(g * cw, (g + 1) * cw)
                mixed = jnp.dot(wm_ref[g], va[:, sl], preferred_element_type=F32) + bt_ref[:, g:g + 1]
                ya_ref[rows, sl] = (gate[:, sl] * mixed).astype(BF16)

    tile = per_step * chunk
    return pl.pallas_call(
        body, name="branch_a_fwd",
        grid=(t // tile,),
        in_specs=[pl.BlockSpec((tile, d3), lambda i: (i, 0)), _const((1, d)), _const(wm.shape), _const(b_t.shape)],
        out_specs=pl.BlockSpec((tile, d), lambda i: (i, 0)),
        out_shape=jax.ShapeDtypeStruct((t, d), BF16),
        compiler_params=_params(("arbitrary",)),
    )(a_pre, g_v, wm, b_t)


def _branch_a_bwd(a_pre, dya, g_v, wm, wm_t, b_t):
    t = a_pre.shape[0]
    d = g_v.shape[1]
    d3 = 3 * d
    ng, chunk, _ = wm.shape
    cw = d // ng
    nsteps = t // chunk

    def body(a_ref, dya_ref, gv_ref, wm_ref, wmt_ref, bt_ref, da_ref, gws_ref, gbt_ref, gnv_ref, db_acc):
        i = pl.program_id(0)

        @pl.when(i == 0)
        def _():
            gws_ref[...] = jnp.zeros_like(gws_ref)
            gnv_ref[...] = jnp.zeros_like(gnv_ref)
            db_acc[...] = jnp.zeros_like(db_acc)

        ua, dgelu_u = _gelu_and_grad(a_ref[:, 0:d])
        vg, dgelu_v = _gelu_and_grad(a_ref[:, d:2 * d])
        za = a_ref[:, 2 * d:3 * d]
        sig = _sigmoid(za)
        sz = za * sig
        dsz = sig * (1.0 + za * (1.0 - sig))
        rv = lax.rsqrt(jnp.mean(vg * vg, axis=-1, keepdims=True) + EPS)
        nv = vg * rv
        gv = gv_ref[...]
        va = (nv * gv).astype(BF16)
        dya = dya_ref[...]
        dmix = dya * ua * sz
        db_acc[...] += dmix
        dmix_b = dmix.astype(BF16)
        t_gate = dya * sz
        t_z = dya * ua * dsz
        dva_parts = []
        for g in range(ng):
            sl = slice(g * cw, (g + 1) * cw)
            mixed = jnp.dot(wm_ref[g], va[:, sl], preferred_element_type=F32) + bt_ref[:, g:g + 1]
            da_ref[:, sl] = (t_gate[:, sl] * mixed * dgelu_u[:, sl]).astype(BF16)
            da_ref[:, 2 * d + g * cw:2 * d + (g + 1) * cw] = (t_z[:, sl] * mixed).astype(BF16)
            gws_ref[g] += lax.dot_general(dmix_b[:, sl], va[:, sl], NT, preferred_element_type=F32)
            dva_parts.append(jnp.dot(wmt_ref[g], dmix_b[:, sl], preferred_element_type=F32))
        dva = jnp.concatenate(dva_parts, axis=1)
        gnv_ref[...] += jnp.sum(dva * nv, axis=0, keepdims=True)
        dnv = dva * gv
        dvg = rv * (dnv - nv * jnp.mean(dnv * nv, axis=-1, keepdims=True))
        da_ref[:, d:2 * d] = (dvg * dgelu_v).astype(BF16)

        @pl.when(i == nsteps - 1)
        def _():
            acc = db_acc[...]
            for g in range(ng):
                gbt_ref[:, g:g + 1] = jnp.sum(acc[:, g * cw:(g + 1) * cw], axis=1, keepdims=True)

    return pl.pallas_call(
        body, name="branch_a_bwd",
        grid=(nsteps,),
        in_specs=[pl.BlockSpec((chunk, d3), lambda i: (i, 0)), pl.BlockSpec((chunk, d), lambda i: (i, 0)),
                  _const((1, d)), _const(wm.shape), _const(wm_t.shape), _const(b_t.shape)],
        out_specs=[pl.BlockSpec((chunk, d3), lambda i: (i, 0)), _const(wm.shape), _const(b_t.shape), _const((1, d))],
        out_shape=[jax.ShapeDtypeStruct((t, d3), BF16), jax.ShapeDtypeStruct(wm.shape, F32),
                   jax.ShapeDtypeStruct(b_t.shape, F32), jax.ShapeDtypeStruct((1, d), F32)],
        scratch_shapes=[pltpu.VMEM((chunk, d), F32)],
        compiler_params=_params(("arbitrary",)),
    )(a_pre, dya, g_v, wm, wm_t, b_t)


def _below_diagonal(n):
    return lax.broadcasted_iota(jnp.int32, (n, n), 0) > lax.broadcasted_iota(jnp.int32, (n, n), 1)


def _twice(tri):
    t = tri.astype(BF16)
    return jnp.concatenate([t, t], axis=0)


def _cumsum_mm(a, tri2):
    hi, lo = _split_bf16(a)
    return jnp.dot(jnp.concatenate([hi, lo], axis=1), tri2, preferred_element_type=F32)


LOG2E = 1.4426950408889634
_SIGN = 0x80000000


def _sb_block(q, k, scale, upper2, causal):
    z2 = lax.dot_general(q, k, NT, preferred_element_type=F32) * (scale * LOG2E)
    neg_abs = lax.bitcast_convert_type(lax.bitcast_convert_type(z2, jnp.uint32) | jnp.uint32(_SIGN), F32)
    l2 = jnp.log(1.0 + jnp.exp2(neg_abs)) * LOG2E
    log_beta = jnp.minimum(z2, 0.0) - l2
    lom = log_beta - z2
    if causal is not None:
        lom = jnp.where(causal, lom, 0.0)
    sx = _cumsum_mm(lom, upper2)
    return log_beta, sx, sx[:, 0:1] + lom[:, 0:1]


DEAD_LOG2 = -150.0


def _max_carry(carries):
    return jnp.max(functools.reduce(jnp.maximum, carries))


ZB_GROUP, GA_GROUP, GB_GROUP = 6, 7, 8


def _attn_specs(d, seq, nq, heads_per_step):
    hp_w = heads_per_step * (d // HEADS)
    n_hp = d // hp_w
    row_blk = lambda group: pl.BlockSpec((ATT_T, hp_w), lambda b, h, i: (b * nq + i, group * n_hp + h))
    seq_blk = lambda group: pl.BlockSpec((seq, hp_w), lambda b, h, i: (b, group * n_hp + h))
    return row_blk, seq_blk, n_hp


def _attn_fwd(qkv, proj, bsz, seq):
    t, d3 = qkv.shape
    d = d3 // 3
    hd = d // HEADS
    nq = seq // ATT_T
    scale = hd ** -0.5
    n_heads = ATT_HP
    row_blk, seq_blk, n_hp = _attn_specs(d, seq, nq, n_heads)

    def body(q_ref, k_ref, v_ref, zb_ref, o_ref, yb_ref):
        i = pl.program_id(2)
        causal = _below_diagonal(ATT_T)
        upper2 = _twice(causal)

        def step(kb, state, mask):
            rows = pl.ds(pl.multiple_of(kb * ATT_T, ATT_T), ATT_T)
            heads = [slice(h * hd, (h + 1) * hd) for h in range(n_heads)]
            scores = [_sb_block(q_ref[:, cols], k_ref[rows, cols], scale, upper2, mask) for cols in heads]
            new = []
            for cols, (carry, acc), (log_beta, sx, total) in zip(heads, state, scores):
                a = jnp.exp2(log_beta + sx + carry)
                if mask is not None:
                    a = jnp.where(mask, a, 0.0)
                acc = acc + jnp.dot(a.astype(BF16), v_ref[rows, cols], preferred_element_type=F32)
                new.append((carry + total, acc))
            return tuple(new)

        init = tuple((jnp.zeros((ATT_T, 1), F32), jnp.zeros((ATT_T, hd), F32)) for _ in range(n_heads))
        state = step(i, init, causal)
        def more(c):
            new = step(c[0], c[1], None)
            return c[0] - 1, new, _max_carry([s[0] for s in new])

        _, state, _ = lax.while_loop(lambda c: (c[0] >= 0) & (c[2] > DEAD_LOG2), more,
                                     (i - 1, state, _max_carry([s[0] for s in state])))
        for h in range(n_heads):
            cols = slice(h * hd, (h + 1) * hd)
            acc = state[h][1]
            zb = zb_ref[:, cols]
            o_ref[:, cols] = acc
            yb_ref[:, cols] = (acc * (zb * _sigmoid(zb))).astype(BF16)

    return pl.pallas_call(
        body, name="attn_fwd",
        grid=(bsz, n_hp, nq),
        in_specs=[row_blk(0), seq_blk(1), seq_blk(2), row_blk(ZB_GROUP)],
        out_specs=[row_blk(0), row_blk(0)],
        out_shape=[jax.ShapeDtypeStruct((t, d), F32), jax.ShapeDtypeStruct((t, d), BF16)],
        compiler_params=_params(("arbitrary", "arbitrary", "arbitrary")),
    )(qkv, qkv, qkv, proj)


def _attn_bwd(qkv, proj, o, dyb, bsz, seq):
    t, d3 = qkv.shape
    d = d3 // 3
    hd = d // HEADS
    nq = seq // ATT_T
    scale = hd ** -0.5
    row_blk, seq_blk, n_hp = _attn_specs(d, seq, nq, ATT_HP)

    def body(q_ref, k_ref, v_ref, zb_ref, o_ref, dyb_ref, dq_ref, dk_ref, dv_ref, dzb_ref,
             g_s, beta_s, dkt_acc, dvt_acc):
        i = pl.program_id(2)

        @pl.when(i == 0)
        def _():
            dkt_acc[...] = jnp.zeros_like(dkt_acc)
            dvt_acc[...] = jnp.zeros_like(dvt_acc)

        causal = _below_diagonal(ATT_T)
        upper2 = _twice(causal)
        lower2 = _twice(~causal)
        zb = zb_ref[...]
        sig = _sigmoid(zb)
        dyb_t = dyb_ref[...]
        do_f = dyb_t * (zb * sig)
        do = do_f.astype(BF16)
        do_t = do_f.T.astype(BF16)
        q_t = q_ref[...].astype(F32).T.astype(BF16)
        dzb_ref[...] = (dyb_t * o_ref[...] * (sig * (1.0 + zb * (1.0 - sig)))).astype(BF16)

        def sweep(kb, carries, mask):
            rows = pl.ds(pl.multiple_of(kb * ATT_T, ATT_T), ATT_T)
            heads = [slice(h * hd, (h + 1) * hd) for h in range(ATT_HP)]
            scores = [_sb_block(q_ref[:, cols], k_ref[rows, cols], scale, upper2, mask) for cols in heads]
            das = [lax.dot_general(do[:, cols], v_ref[rows, cols], NT, preferred_element_type=F32) for cols in heads]
            new = []
            for h, (cols, carry, (log_beta, sx, total), da) in enumerate(zip(heads, carries, scores, das)):
                a = jnp.exp2(log_beta + sx + carry)
                beta = jnp.exp2(log_beta)
                if mask is not None:
                    a = jnp.where(mask, a, 0.0)
                    beta = jnp.where(mask, beta, 0.0)
                g_s[h, kb] = a * da
                beta_s[h, kb] = beta
                dvt_acc[kb, cols, :] += jnp.dot(do_t[cols, :], a.astype(BF16), preferred_element_type=F32)
                new.append(carry + total)
            return tuple(new)

        carries = sweep(i, tuple(jnp.zeros((ATT_T, 1), F32) for _ in range(ATT_HP)), causal)

        def more(c):
            new = sweep(c[0], c[1], None)
            return c[0] - 1, new, _max_carry(new)

        last, _, _ = lax.while_loop(lambda c: (c[0] >= 0) & (c[2] > DEAD_LOG2), more, (i - 1, carries, _max_carry(carries)))
        first_kb = last + 1

        def back(kb, state):
            rows = pl.ds(pl.multiple_of(kb * ATT_T, ATT_T), ATT_T)
            heads = [slice(h * hd, (h + 1) * hd) for h in range(ATT_HP)]
            sums = [_cumsum_mm(g_s[h, kb], lower2) for h in range(ATT_HP)]
            new = []
            for h, (cols, (p_carry, dq), px) in enumerate(zip(heads, state, sums)):
                dz = ((g_s[h, kb] - (p_carry + px) * beta_s[h, kb]) * scale).astype(BF16)
                dq = dq + jnp.dot(dz, k_ref[rows, cols], preferred_element_type=F32)
                dkt_acc[kb, cols, :] += jnp.dot(q_t[cols, :], dz, preferred_element_type=F32)
                new.append((p_carry + px[:, ATT_T - 1:ATT_T], dq))
            return tuple(new)

        init = tuple((jnp.zeros((ATT_T, 1), F32), jnp.zeros((ATT_T, hd), F32)) for _ in range(ATT_HP))
        state = lax.fori_loop(first_kb, i + 1, back, init)
        for h in range(ATT_HP):
            dq_ref[:, h * hd:(h + 1) * hd] = state[h][1].astype(BF16)

        @pl.when(i == nq - 1)
        def _():
            for kb in range(nq):
                dk_ref[kb * ATT_T:(kb + 1) * ATT_T, :] = dkt_acc[kb].T.astype(BF16)
                dv_ref[kb * ATT_T:(kb + 1) * ATT_T, :] = dvt_acc[kb].T.astype(BF16)

    out = jax.ShapeDtypeStruct((t, d), BF16)
    hp_w = ATT_HP * hd
    return pl.pallas_call(
        body, name="attn_bwd",
        grid=(bsz, n_hp, nq),
        in_specs=[row_blk(0), seq_blk(1), seq_blk(2), row_blk(ZB_GROUP), row_blk(0), row_blk(0)],
        out_specs=[row_blk(0), seq_blk(0), seq_blk(0), row_blk(0)],
        out_shape=[out, out, out, out],
        scratch_shapes=[pltpu.VMEM((ATT_HP, nq, ATT_T, ATT_T), F32), pltpu.VMEM((ATT_HP, nq, ATT_T, ATT_T), F32),
                        pltpu.VMEM((nq, hp_w, ATT_T), F32), pltpu.VMEM((nq, hp_w, ATT_T), F32)],
        compiler_params=_params(("arbitrary", "arbitrary", "arbitrary")),
    )(qkv, qkv, qkv, proj, o, dyb)


def _out_proj(ya, yb, g_pre, x2d, tgt, w_og, w_osb, w_out, g_f):
    t, d = x2d.shape

    def body(ya_ref, yb_ref, ga_ref, gb_ref, x_ref, tgt_ref, wog_ref, wosb_ref, wout_ref, gf_ref,
             dya_ref, dyb_ref, dg_ref, dx2_ref, loss_ref, gnf_ref, gwog_ref, gwosb_ref, gwout_ref):
        @pl.when(pl.program_id(0) == 0)
        def _():
            loss_ref[...] = jnp.zeros_like(loss_ref)
            gnf_ref[...] = jnp.zeros_like(gnf_ref)
            gwog_ref[...] = jnp.zeros_like(gwog_ref)
            gwosb_ref[...] = jnp.zeros_like(gwosb_ref)
            gwout_ref[...] = jnp.zeros_like(gwout_ref)

        ya = ya_ref[...]
        yb = yb_ref[...]
        pa = jnp.dot(ya, wog_ref[...], preferred_element_type=F32)
        pb = jnp.dot(yb, wosb_ref[...], preferred_element_type=F32)
        sga = _sigmoid(ga_ref[...])
        sgb = _sigmoid(gb_ref[...])
        merged = (sga * pa + sgb * pb).astype(BF16)
        x2 = x_ref[...] + jnp.dot(merged, wout_ref[...], preferred_element_type=F32)
        r2 = lax.rsqrt(jnp.mean(x2 * x2, axis=-1, keepdims=True) + EPS)
        n2 = x2 * r2
        gf = gf_ref[...]
        err = n2 * gf - tgt_ref[...]
        loss_ref[...] += 0.5 * jnp.sum(jnp.sum(err * err, axis=-1, keepdims=True), axis=0, keepdims=True) / d
        dy = err * (1.0 / d)
        gnf_ref[...] += jnp.sum(dy * n2, axis=0, keepdims=True)
        dn = dy * gf
        dx2 = r2 * (dn - n2 * jnp.mean(dn * n2, axis=-1, keepdims=True))
        dx2_ref[...] = dx2
        dx2_b = dx2.astype(BF16)
        dmerged = lax.dot_general(dx2_b, wout_ref[...], NT, preferred_element_type=F32)
        gwout_ref[...] += lax.dot_general(merged, dx2_b, TN, preferred_element_type=F32)
        dg_ref[:, 0:d] = (dmerged * pa * (sga * (1.0 - sga))).astype(BF16)
        dg_ref[:, d:2 * d] = (dmerged * pb * (sgb * (1.0 - sgb))).astype(BF16)
        dpa = (dmerged * sga).astype(BF16)
        dpb = (dmerged * sgb).astype(BF16)
        dya_ref[...] = lax.dot_general(dpa, wog_ref[...], NT, preferred_element_type=F32)
        dyb_ref[...] = lax.dot_general(dpb, wosb_ref[...], NT, preferred_element_type=F32)
        gwog_ref[...] += lax.dot_general(ya, dpa, TN, preferred_element_type=F32)
        gwosb_ref[...] += lax.dot_general(yb, dpb, TN, preferred_element_type=F32)

    row = lambda i: (i, 0)
    return pl.pallas_call(
        body, name="out_proj",
        grid=(t // TM,),
        in_specs=[pl.BlockSpec((TM, d), row), pl.BlockSpec((TM, d), row),
                  pl.BlockSpec((TM, d), lambda i: (i, GA_GROUP)), pl.BlockSpec((TM, d), lambda i: (i, GB_GROUP)),
                  pl.BlockSpec((TM, d), row), pl.BlockSpec((TM, d), row),
                  _resident((d, d)), _resident((d, d)), _resident((d, d)), _const((1, d))],
        out_specs=[pl.BlockSpec((TM, d), row), pl.BlockSpec((TM, d), row), pl.BlockSpec((TM, 2 * d), row),
                   pl.BlockSpec((TM, d), row), _const((1, 1)), _const((1, d)),
                   _const((d, d)), _const((d, d)), _const((d, d))],
        out_shape=[jax.ShapeDtypeStruct((t, d), F32), jax.ShapeDtypeStruct((t, d), F32),
                   jax.ShapeDtypeStruct((t, 2 * d), BF16), jax.ShapeDtypeStruct((t, d), F32),
                   jax.ShapeDtypeStruct((1, 1), F32), jax.ShapeDtypeStruct((1, d), F32),
                   jax.ShapeDtypeStruct((d, d), F32), jax.ShapeDtypeStruct((d, d), F32),
                   jax.ShapeDtypeStruct((d, d), F32)],
        compiler_params=_params(("arbitrary",)),
    )(ya, yb, g_pre, g_pre, x2d, tgt, w_og, w_osb, w_out, g_f)


def _dproj_pieces(d):
    return [(0, 0, 3), (1, 3, 1), (2, 4, 1), (3, 5, 1), (4, 6, 1), (5, 7, 2)]


def _in_proj_bwd_x(pieces, wg, x2d, g_in, dx2, gw16):
    t, d = x2d.shape
    ncol = wg.shape[2]
    segs = _segments(d, ncol)
    layout = _dproj_pieces(d)
    nsteps = t // TMX

    def body(da_ref, dq_ref, dk_ref, dv_ref, dzb_ref, dg_ref, w_ref, x_ref, g_ref, dx2_ref, gw16_ref,
             gx_ref, gn_ref, recv_ref, send_sems, recv_sems):
        x_pos, y_pos, c_pos, chips = _place()

        def share(k, chunk):
            px, py = chips[k]
            return pltpu.make_async_remote_copy(
                src_ref=gw16_ref.at[:, chunk * ncol:(chunk + 1) * ncol], dst_ref=recv_ref.at[k],
                send_sem=send_sems.at[k], recv_sem=recv_sems.at[k], device_id=(px, py, c_pos), device_id_type=MESH)

        @pl.when(pl.program_id(0) == 0)
        def _():
            gn_ref[...] = jnp.zeros_like(gn_ref)
            for k, (px, py) in enumerate(chips):
                for chunk in range(N_CHIP):
                    @pl.when(2 * px + py == chunk)
                    def _(k=k, chunk=chunk):
                        share(k, chunk).start()

        @pl.when(pl.program_id(0) == nsteps - 1)
        def _():
            for k in range(N_CHIP - 1):
                share(k, 0).wait()

        refs = (da_ref, dq_ref, dk_ref, dv_ref, dzb_ref, dg_ref)
        dh = jnp.zeros((TMX, d), F32)
        for chip, c0, grp, s0, width in segs:
            piece, first, _ = next(p for p in layout if p[1] <= grp < p[1] + p[2])
            off = (grp - first) * d + s0
            dh = dh + lax.dot_general(refs[piece][:, off:off + width], w_ref[chip, :, c0:c0 + width], NT,
                                      preferred_element_type=F32)
        x = x_ref[...]
        r = lax.rsqrt(jnp.mean(x * x, axis=-1, keepdims=True) + EPS)
        n = x * r
        gn_ref[...] += jnp.sum(dh * n, axis=0, keepdims=True)
        dn = dh * g_ref[...]
        gx_ref[...] = dx2_ref[...] + r * (dn - n * jnp.mean(dn * n, axis=-1, keepdims=True))

    row = lambda i: (i, 0)
    return pl.pallas_call(
        body, name="in_proj_bwd_x",
        grid=(nsteps,),
        in_specs=[pl.BlockSpec((TMX, p.shape[1]), row) for p in pieces]
        + [_resident(wg.shape), pl.BlockSpec((TMX, d), row), _const((1, d)), pl.BlockSpec((TMX, d), row), ANY],
        out_specs=[pl.BlockSpec((TMX, d), row), _const((1, d)), ANY],
        out_shape=[jax.ShapeDtypeStruct((t, d), F32), jax.ShapeDtypeStruct((1, d), F32),
                   jax.ShapeDtypeStruct((N_CHIP - 1, gw16.shape[0], ncol), BF16)],
        scratch_shapes=[pltpu.SemaphoreType.DMA((N_CHIP - 1,)), pltpu.SemaphoreType.DMA((N_CHIP - 1,))],
        compiler_params=_params(("arbitrary",)),
    )(*pieces, wg, x2d, g_in, dx2, gw16)


def _in_proj_bwd_w(h_t, pieces, mats16, pack):
    d, t = h_t.shape
    nk = t // TKW
    half = d // 2
    layout = _dproj_pieces(d)
    n_mats = len(mats16)
    n_dev = 2 * N_CHIP
    flips = [(dx, dy, dc) for dx in (0, 1) for dy in (0, 1) for dc in (0, 1)][1:]

    def body(ht_ref, da_ref, dq_ref, dk_ref, dv_ref, dzb_ref, dg_ref, *rest):
        mat_refs, pack_ref = rest[:n_mats], rest[n_mats]
        gw_ref, sib_ref = rest[n_mats + 1:n_mats + 3]
        recv_refs, slots_ref = rest[n_mats + 3:2 * n_mats + 3], rest[2 * n_mats + 3]
        acc, stage, mat_send, mat_recv, pack_send, pack_recv, own_sem, stage_send, sib_recv = rest[2 * n_mats + 4:]
        s = pl.program_id(0)
        i = pl.program_id(1)
        x_pos, y_pos, c_pos, chips = _place()
        me = 4 * x_pos + 2 * y_pos + c_pos

        def to_sibling(slot, group):
            return pltpu.make_async_remote_copy(
                src_ref=stage.at[slot], dst_ref=sib_ref.at[:, group * d:(group + 1) * d],
                send_sem=stage_send.at[slot], recv_sem=sib_recv, device_id=(x_pos, y_pos, 1 - c_pos), device_id_type=MESH)

        def exchanges():
            cps = []
            for k, (px, py) in enumerate(chips):
                for a in range(n_mats):
                    cps.append(pltpu.make_async_remote_copy(
                        src_ref=mat_refs[a].at[2 * px + py], dst_ref=recv_refs[a].at[k],
                        send_sem=mat_send.at[a, k], recv_sem=mat_recv.at[a, k],
                        device_id=(px, py, c_pos), device_id_type=MESH))
            for k, (dx, dy, dc) in enumerate(flips):
                peer = (1 - x_pos if dx else x_pos, 1 - y_pos if dy else y_pos, 1 - c_pos if dc else c_pos)
                cps.append(pltpu.make_async_remote_copy(
                    src_ref=pack_ref, dst_ref=slots_ref.at[me], send_sem=pack_send.at[k], recv_sem=pack_recv.at[k],
                    device_id=peer, device_id_type=MESH))
            return cps, pltpu.make_async_copy(pack_ref, slots_ref.at[me], own_sem)

        @pl.when((s == 0) & (i == 0))
        def _():
            cps, own = exchanges()
            own.start()
            for cp in cps:
                cp.start()

        @pl.when(i == 0)
        def _():
            acc[...] = jnp.zeros_like(acc)

        refs = (da_ref, dq_ref, dk_ref, dv_ref, dzb_ref, dg_ref)
        for piece, first, count in layout:
            @pl.when((s >= first) & (s < first + count))
            def _(piece=piece):
                acc[...] += jnp.dot(ht_ref[...], refs[piece][...], preferred_element_type=F32)

        @pl.when(i == nk - 1)
        def _():
            gw_ref[...] = acc[...]
            for slot in range(2):
                @pl.when(s % 2 == slot)
                def _(slot=slot):
                    @pl.when(s >= 2)
                    def _():
                        to_sibling(slot, 0).wait_send()
                    for other in range(2):
                        @pl.when(c_pos == 1 - other)
                        def _(other=other):
                            stage[slot] = acc[other * half:(other + 1) * half, :].astype(BF16)
                    for group in range(N_SPLIT):
                        @pl.when(s == group)
                        def _(group=group):
                            to_sibling(slot, group).start()

        @pl.when((s == N_SPLIT - 1) & (i == nk - 1))
        def _():
            for slot in range(2):
                to_sibling(slot, 0).wait_send()
            pltpu.make_async_remote_copy(
                src_ref=sib_ref, dst_ref=sib_ref, send_sem=stage_send.at[0], recv_sem=sib_recv,
                device_id=(x_pos, y_pos, c_pos), device_id_type=MESH).wait_recv()
            cps, own = exchanges()
            own.wait()
            for cp in cps:
                cp.wait()

    def piece_spec(p, first, count):
        def index(s, i):
            mine = (s >= first) & (s < first + count)
            return jnp.where(mine, i, 0), jnp.where(mine, s - first, 0)
        return pl.BlockSpec((TKW, d), index)

    col_blk = pl.BlockSpec((d, d), lambda s, i: (0, s))
    outs = pl.pallas_call(
        body, name="in_proj_bwd_w",
        grid=(N_SPLIT, nk),
        in_specs=[pl.BlockSpec((d, TKW), lambda s, i: (0, i))] + [piece_spec(*p) for p in layout] + [ANY] * (n_mats + 1),
        out_specs=[col_blk, ANY] + [ANY] * (n_mats + 1),
        out_shape=[jax.ShapeDtypeStruct((d, N_SPLIT * d), F32), jax.ShapeDtypeStruct((half, N_SPLIT * d), BF16)]
        + [jax.ShapeDtypeStruct((N_CHIP - 1,) + m.shape[1:], BF16) for m in mats16]
        + [jax.ShapeDtypeStruct((n_dev,) + pack.shape, F32)],
        scratch_shapes=[pltpu.VMEM((d, d), F32), pltpu.VMEM((2, half, d), BF16),
                        pltpu.SemaphoreType.DMA((n_mats, N_CHIP - 1)), pltpu.SemaphoreType.DMA((n_mats, N_CHIP - 1)),
                        pltpu.SemaphoreType.DMA((n_dev - 1,)), pltpu.SemaphoreType.DMA((n_dev - 1,)),
                        pltpu.SemaphoreType.DMA, pltpu.SemaphoreType.DMA((2,)), pltpu.SemaphoreType.DMA],
        compiler_params=_params(("arbitrary", "arbitrary")),
    )(h_t, *pieces, *mats16, pack)
    return outs[0], outs[1], outs[2:2 + n_mats], outs[2 + n_mats]


def _local_step(proj, qkv, x2d, tgt2d, bsz, seq, norm_v, w_s, b_s, w_og, w_osb, w_out, norm_final):
    d = x2d.shape[1]
    chunk = w_s.shape[-1]
    causal = jnp.tril(jnp.ones((chunk, chunk), dtype=bool))
    wm = jnp.where(causal[None], w_s, 0.0).astype(BF16)
    wm_t = jnp.swapaxes(wm, 1, 2)
    b_t = b_s.T

    ya = _branch_a_fwd(proj, norm_v, wm, b_t)
    o, yb = _attn_fwd(qkv, proj, bsz, seq)
    dya, dyb, dg, dx2, loss, g_nf, g_wog, g_wosb, g_wout = _out_proj(
        ya, yb, proj, x2d, tgt2d, w_og, w_osb, w_out, norm_final.reshape(1, d))
    dq, dk, dv, dzb = _attn_bwd(qkv, proj, o, dyb, bsz, seq)
    d_a, g_ws, g_bt, g_nv = _branch_a_bwd(proj, dya, norm_v, wm, wm_t, b_t)
    g_ws = jnp.where(causal[None], g_ws, 0.0)
    return loss, (d_a, dq, dk, dv, dzb, dg), dx2, g_nv, g_ws, g_bt.T, g_wog, g_wosb, g_wout, g_nf


def _row_tile(rows):
    return next(r for r in (128, 64, 32, 16, 8) if rows % r == 0)


def _cast_bf16(arrs):
    n = len(arrs)

    def body(*refs):
        for a_ref, o_ref in zip(refs[:n], refs[n:]):
            o_ref[...] = a_ref[...].astype(BF16)

    specs = [pl.BlockSpec((a.shape[0] // CAST_STEPS, a.shape[1]), lambda i: (i, 0)) for a in arrs]
    return pl.pallas_call(
        body, name="cast_bf16", grid=(CAST_STEPS,),
        in_specs=specs, out_specs=specs,
        out_shape=[jax.ShapeDtypeStruct(a.shape, BF16) for a in arrs],
        compiler_params=_params(("arbitrary",)),
    )(*arrs)


def _chip_half(full, from_sibling, core, chip, tile):
    half, n = from_sibling.shape

    def body(where_ref, own_ref, sib_ref, o32_ref, o16_ref):
        total = own_ref[...] + sib_ref[...].astype(F32)
        o16_ref[...] = total.astype(BF16)

        @pl.when(pl.program_id(0) == where_ref[1])
        def _():
            o32_ref[...] = total

    blk = pl.BlockSpec((half, tile), lambda j, where_ref: (0, j))
    return pl.pallas_call(
        body, name="chip_half",
        grid_spec=pltpu.PrefetchScalarGridSpec(
            num_scalar_prefetch=1, grid=(n // tile,),
            in_specs=[pl.BlockSpec((half, tile), lambda j, where_ref: (where_ref[0], j)), blk],
            out_specs=[pl.BlockSpec((half, tile), lambda j, where_ref: (0, 0)), blk]),
        out_shape=[jax.ShapeDtypeStruct((half, tile), F32), jax.ShapeDtypeStruct((half, n), BF16)],
        compiler_params=_params(("arbitrary",)),
    )(jnp.stack([core, chip]).astype(jnp.int32), full, from_sibling)


def _add_received(fulls, recvs, chip, by_cols):
    n = len(fulls)
    _, rows, cols = recvs[0].shape
    tr = _row_tile(rows)
    nb = rows // tr

    def body(chip_ref, *refs):
        for own_ref, recv_ref, o_ref in zip(refs[:n], refs[n:2 * n], refs[2 * n:]):
            s = own_ref[...]
            for k in range(N_CHIP - 1):
                s = s + recv_ref[k].astype(F32)
            o_ref[...] = s

    own_map = (lambda i, chip_ref: (i, chip_ref[0])) if by_cols else (lambda i, chip_ref: (chip_ref[0] * nb + i, 0))
    return pl.pallas_call(
        body, name="add_received",
        grid_spec=pltpu.PrefetchScalarGridSpec(
            num_scalar_prefetch=1, grid=(nb,),
            in_specs=[pl.BlockSpec((tr, cols), own_map)] * n
            + [pl.BlockSpec((N_CHIP - 1, tr, cols), lambda i, chip_ref: (0, i, 0))] * n,
            out_specs=[pl.BlockSpec((tr, cols), lambda i, chip_ref: (i, 0))] * n),
        out_shape=[jax.ShapeDtypeStruct((rows, cols), F32)] * n,
        compiler_params=_params(("arbitrary",)),
    )(chip.reshape(1).astype(jnp.int32), *fulls, *recvs)


def _adamw_math(w, m, v, g):
    new_m = ADAM_B1 * m + (1.0 - ADAM_B1) * g
    new_v = ADAM_B2 * v + (1.0 - ADAM_B2) * (g * g)
    m_hat = new_m / (1.0 - ADAM_B1 ** ADAM_STEP)
    v_hat = new_v / (1.0 - ADAM_B2 ** ADAM_STEP)
    return -ADAM_LR * (m_hat / (jnp.sqrt(v_hat) + ADAM_EPS) + ADAM_WD * w), new_m, new_v


def _adamw_pairs(ws, ms, vs, mines, theirs):
    n = len(ws)
    rows, cols = ws[0].shape
    tr = _row_tile(rows)

    def body(*refs):
        ins, outs = refs[:5 * n], refs[5 * n:]
        for a in range(n):
            w_ref, m_ref, v_ref, mine_ref, theirs_ref = ins[a::n]
            g_ref, d_ref, nm_ref, nv_ref = outs[4 * a:4 * a + 4]
            g = mine_ref[...] + theirs_ref[...]
            g_ref[...] = g
            d_ref[...], nm_ref[...], nv_ref[...] = _adamw_math(w_ref[...], m_ref[...], v_ref[...], g)

    spec = pl.BlockSpec((tr, cols), lambda i: (i, 0))
    out = jax.ShapeDtypeStruct((rows, cols), F32)
    outs = pl.pallas_call(
        body, name="adamw_pairs", grid=(rows // tr,),
        in_specs=[spec] * (5 * n), out_specs=[spec] * (4 * n), out_shape=[out] * (4 * n),
        compiler_params=_params(("arbitrary",)),
    )(*ws, *ms, *vs, *mines, *theirs)
    return [outs[4 * a:4 * a + 4] for a in range(n)]


def _adamw_halves(w, m, v, mine, theirs, core):
    rows, cols = w.shape
    tr = _row_tile(rows // 2)
    per_half = rows // 2 // tr

    def body(core_ref, w_ref, m_ref, v_ref, mine_ref, theirs_ref, g_ref, d_ref, nm_ref, nv_ref):
        is_mine = pl.program_id(0) // per_half == core_ref[0]
        for part, cond in ((mine_ref, is_mine), (theirs_ref, jnp.logical_not(is_mine))):
            @pl.when(cond)
            def _(part=part):
                g = part[...]
                g_ref[...] = g
                d_ref[...], nm_ref[...], nv_ref[...] = _adamw_math(w_ref[...], m_ref[...], v_ref[...], g)

    spec = pl.BlockSpec((tr, cols), lambda i, core_ref: (i, 0))

    def half_spec(own):
        def index(i, core_ref):
            in_core_half = i // per_half == core_ref[0]
            here = in_core_half if own else jnp.logical_not(in_core_half)
            return jnp.where(here, i % per_half, 0), 0
        return pl.BlockSpec((tr, cols), index)

    out = jax.ShapeDtypeStruct(w.shape, F32)
    return pl.pallas_call(
        body, name="adamw_halves",
        grid_spec=pltpu.PrefetchScalarGridSpec(
            num_scalar_prefetch=1, grid=(rows // tr,),
            in_specs=[spec] * 3 + [half_spec(True), half_spec(False)], out_specs=[spec] * 4),
        out_shape=[out] * 4,
        compiler_params=_params(("arbitrary",)),
    )(core.reshape(1).astype(jnp.int32), w, m, v, mine, theirs)


def _adamw_small(w, m, v, slots_head, slots_tail):
    n_dev, p0, _ = slots_head.shape

    def body(w_ref, m_ref, v_ref, head_ref, tail_ref, g_ref, d_ref, nm_ref, nv_ref):
        for ref, rows in ((head_ref, slice(0, p0)), (tail_ref, slice(p0, w.shape[0]))):
            g = ref[0]
            for i in range(1, n_dev):
                g = g + ref[i]
            g_ref[rows, :] = g
            d_ref[rows, :], nm_ref[rows, :], nv_ref[rows, :] = _adamw_math(w_ref[rows, :], m_ref[rows, :], v_ref[rows, :], g)

    vmem = pl.BlockSpec(memory_space=pltpu.VMEM)
    out = jax.ShapeDtypeStruct(w.shape, F32)
    return pl.pallas_call(
        body, name="adamw_small", in_specs=[vmem] * 5, out_specs=[vmem] * 4, out_shape=[out] * 4,
        compiler_params=pltpu.CompilerParams(vmem_limit_bytes=VMEM_LIMIT),
    )(w, m, v, slots_head, slots_tail)


ANY = pl.BlockSpec(memory_space=pl.ANY)


def _place():
    x, y, c = lax.axis_index("x"), lax.axis_index("y"), lax.axis_index("c")
    other_chips = [(1 - x, y), (x, 1 - y), (1 - x, 1 - y)]
    return x, y, c, other_chips


def _swap_and_gather(arrs, pack):
    n = len(arrs)
    n_dev = 2 * N_CHIP
    flips = [(dx, dy, dc) for dx in (0, 1) for dy in (0, 1) for dc in (0, 1)][1:]

    def body(*refs):
        ins, pack_ref = refs[:n], refs[n]
        outs, slots_ref = refs[n + 1:2 * n + 1], refs[2 * n + 1]
        send_sems, recv_sems, pack_send, pack_recv, own_sem = refs[2 * n + 2:]
        x, y, c, _ = _place()
        me = 4 * x + 2 * y + c
        own = pltpu.make_async_copy(pack_ref, slots_ref.at[me], own_sem)
        own.start()
        copies = []
        for k, (dx, dy, dc) in enumerate(flips):
            peer = (1 - x if dx else x, 1 - y if dy else y, 1 - c if dc else c)
            copies.append(pltpu.make_async_remote_copy(
                src_ref=pack_ref, dst_ref=slots_ref.at[me], send_sem=pack_send.at[k], recv_sem=pack_recv.at[k],
                device_id=peer, device_id_type=MESH))
        copies += [pltpu.make_async_remote_copy(
            src_ref=ins[a], dst_ref=outs[a], send_sem=send_sems.at[a], recv_sem=recv_sems.at[a],
            device_id=(x, y, 1 - c), device_id_type=MESH) for a in range(n)]
        for cp in copies:
            cp.start()
        for cp in copies:
            cp.wait()
        own.wait()

    return pl.pallas_call(
        body, name="swap_and_gather",
        in_specs=[ANY] * (n + 1), out_specs=[ANY] * (n + 1),
        out_shape=[jax.ShapeDtypeStruct(a.shape, a.dtype) for a in arrs] + [jax.ShapeDtypeStruct((n_dev,) + pack.shape, F32)],
        scratch_shapes=[pltpu.SemaphoreType.DMA((n,)), pltpu.SemaphoreType.DMA((n,)),
                        pltpu.SemaphoreType.DMA((n_dev - 1,)), pltpu.SemaphoreType.DMA((n_dev - 1,)),
                        pltpu.SemaphoreType.DMA],
    )(*arrs, pack)


SLAB = 8


def _slab(vec, d):
    return jnp.pad(vec.reshape(1, d), ((0, SLAB - 1), (0, 0)))


def _pack_tail(vec_nv, b_s, vec_nf, w_s, scalar=None):
    d = vec_nv.shape[-1]
    extra = jnp.zeros((1, d), F32) if scalar is None else jnp.pad(scalar, ((0, 0), (0, d - 1)))
    return jnp.concatenate([_slab(v, d) for v in (vec_nv, b_s, vec_nf, extra)] + [w_s.reshape(-1, d)], axis=0)


def _pack_small(vec_nin, vec_nv, b_s, vec_nf, w_s):
    return jnp.concatenate([_slab(vec_nin, vec_nin.shape[-1]), _pack_tail(vec_nv, b_s, vec_nf, w_s)], axis=0)


def _unpack_small(pack, w_s_shape, b_s_shape):
    return (pack[0:1], pack[SLAB:SLAB + 1], pack[2 * SLAB].reshape(b_s_shape), pack[3 * SLAB],
            pack[5 * SLAB:].reshape(w_s_shape), pack[4 * SLAB, 0])


def kernel(x, norm_in, w_in, norm_v, w_s, b_s, w_o_gmlp, w_o_sb, w_out, norm_final, loss_target, m_norm_in, m_w_in, m_norm_v, m_w_s, m_b_s, m_w_o_gmlp, m_w_o_sb, m_w_out, m_norm_final, v_norm_in, v_w_in, v_norm_v, v_w_s, v_b_s, v_w_o_gmlp, v_w_o_sb, v_w_out, v_norm_final):
    d = x.shape[-1]
    ncol = w_in.shape[-1]
    nrow = w_o_gmlp.shape[-2]
    chip = 2 * lax.axis_index("x") + lax.axis_index("y")

    bsz, seq, _ = x.shape
    x2d = x.reshape(bsz * seq, d)
    shards = [w_in[0], w_o_gmlp[0], w_o_sb[0], w_out[0]]
    halves = [s16.reshape(2, s16.shape[0] // 2, s16.shape[1]) for s16 in _cast_bf16(shards)]
    proj, qkv, h_t, (wg, w_og, w_osb, w_o) = _in_proj_fwd(x2d, norm_in, halves)
    wg = wg.reshape(N_CHIP, d, ncol)

    loss, pieces, dx2, g_nv, g_ws, g_bs, g_wog, g_wosb, g_wout, g_nf = _local_step(
        proj, qkv, x2d, loss_target.reshape(bsz * seq, d), bsz, seq, norm_v, w_s[0], b_s[0],
        w_og.reshape(d, d), w_osb.reshape(d, d), w_o.reshape(d, d), norm_final)

    mats = [g_wog, g_wosb, g_wout]
    mats16 = [g16.reshape(N_CHIP, nrow, d) for g16 in _cast_bf16(mats)]
    g_win, from_sibling, recv_mats, slots_tail = _in_proj_bwd_w(
        h_t, pieces, mats16, _pack_tail(g_nv, g_bs, g_nf, g_ws, loss))
    core = lax.axis_index("c")
    half_own, half_win16 = _chip_half(g_win, from_sibling, core, chip, ncol)
    grad_x, g_nin, recv_win = _in_proj_bwd_x(pieces, wg, x2d, norm_in, dx2, half_win16)
    grad_x = grad_x.reshape(bsz, seq, d)

    sums = _add_received([half_own], [recv_win], jnp.zeros((), jnp.int32), True) + _add_received(
        mats, recv_mats, chip, False)
    *sibling_sums, slots_head = _swap_and_gather(sums, _slab(g_nin, d))
    stats = [_adamw_halves(w_in[0], m_w_in[0], v_w_in[0], sums[0], sibling_sums[0], core)] + _adamw_pairs(
        shards[1:], [m_w_o_gmlp[0], m_w_o_sb[0], m_w_out[0]], [v_w_o_gmlp[0], v_w_o_sb[0], v_w_out[0]],
        sums[1:], sibling_sums[1:])
    (gw_in, dw_in, nm_in, nv_in), (gw_og, dw_og, nm_og, nv_og), (gw_osb, dw_osb, nm_osb, nv_osb), \
        (gw_out, dw_out, nm_out, nv_out) = stats

    gs, ds, ms, vs = _adamw_small(
        _pack_small(norm_in, norm_v, b_s[0], norm_final, w_s[0]),
        _pack_small(m_norm_in, m_norm_v, m_b_s[0], m_norm_final, m_w_s[0]),
        _pack_small(v_norm_in, v_norm_v, v_b_s[0], v_norm_final, v_w_s[0]), slots_head, slots_tail)

    def small(pack):
        nin, nv, bs, nf, ws, _ = _unpack_small(pack, w_s.shape, b_s.shape)
        return nin, nv, ws, bs, nf

    loss = _unpack_small(gs, w_s.shape, b_s.shape)[-1]
    out = []
    for small_pack, win, wog, wosb, wout in ((gs, gw_in, gw_og, gw_osb, gw_out), (ds, dw_in, dw_og, dw_osb, dw_out),
                                             (ms, nm_in, nm_og, nm_osb, nm_out), (vs, nv_in, nv_og, nv_osb, nv_out)):
        nin, nv, ws, bs, nf = small(small_pack)
        out += [nin, win[None], nv, ws, bs, wog[None], wosb[None], wout[None], nf]
    return (loss, grad_x, *out)
```

```python
import functools
import math

import jax
import jax.numpy as jnp
from jax import lax
from jax.experimental import pallas as pl
from jax.experimental.pallas import tpu as pltpu

F32 = jnp.float32
BF16 = jnp.bfloat16
EPS = 1e-6
HEADS = 8
N_SPLIT = 9
N_CHIP = 4
MESH = pl.DeviceIdType.MESH

ADAM_LR = 0.001
ADAM_B1 = 0.9
ADAM_B2 = 0.999
ADAM_EPS = 1e-08
ADAM_WD = 0.01
ADAM_STEP = 10

VMEM_LIMIT = 56 * 2 ** 20
TM = 256
TMF = 512
TMX = 512
ATT_T = 256
ATT_HP = 4
TKW = 1024
CHUNKS_PER_STEP = 4
CAST_STEPS = 8

NT = (((1,), (1,)), ((), ()))
TN = (((0,), (0,)), ((), ()))


def _params(sem):
    return pltpu.CompilerParams(dimension_semantics=sem, vmem_limit_bytes=VMEM_LIMIT)


def _resident(shape):
    nd = len(shape)
    return pl.BlockSpec(shape, lambda *_: (0,) * nd, pipeline_mode=pl.Buffered(1))


def _const(shape):
    nd = len(shape)
    return pl.BlockSpec(shape, lambda *_: (0,) * nd)


def _segments(d, ncol):
    segs = []
    edges = sorted({j * ncol for j in range(N_CHIP + 1)} | {s * d for s in range(N_SPLIT + 1)})
    for lo, hi in zip(edges[:-1], edges[1:]):
        segs.append((lo // ncol, lo % ncol, lo // d, lo % d, hi - lo))
    return segs


def _sigmoid(x):
    return 0.5 * jnp.tanh(0.5 * x) + 0.5


_GELU_C = math.sqrt(2.0 / math.pi)


_GELU_CA = _GELU_C * 0.044715


def _gelu(x):
    return x * (0.5 * jnp.tanh(x * (_GELU_C + _GELU_CA * (x * x))) + 0.5)


def _gelu_and_grad(x):
    x2 = x * x
    u = 0.5 * jnp.tanh(x * (_GELU_C + _GELU_CA * x2)) + 0.5
    slope = (1.0 - u) * (x * (_GELU_C + (3.0 * _GELU_CA) * x2))
    return x * u, u * (2.0 * slope + 1.0)


def _split_bf16(a):
    hi = a.astype(BF16)
    lo = (a - hi.astype(F32)).astype(BF16)
    return hi, lo


def _in_proj_fwd(x2d, g_in, halves):
    t, d = x2d.shape
    n = len(halves)
    ncol = halves[0].shape[2]
    n_row = t // TMF
    last = n_row - 1
    assert halves[0].shape[1] * 2 == d
    qkv_parts = {j: (max(j * ncol, 3 * d) - j * ncol, max(j * ncol, 3 * d) - 3 * d)
                 for j in range(N_CHIP) if min((j + 1) * ncol, 6 * d) > max(j * ncol, 3 * d)}
    qkv_w = 3 * d // len(qkv_parts)
    assert all(min((j + 1) * ncol, 6 * d) - max(j * ncol, 3 * d) == qkv_w and q0 % qkv_w == 0
               for j, (_, q0) in qkv_parts.items())

    def body(order_ref, qrow_ref, qcol_ref, x_ref, g_ref, *rest):
        ins = rest[:n]
        proj_ref, qkv_ref, ht_ref = rest[n:n + 3]
        outs = rest[n + 3:2 * n + 3]
        wbuf, h_all, send_sems, recv_sems, local_sems, load_sem = rest[2 * n + 3:]
        phase = pl.program_id(0)
        i = pl.program_id(1)
        x_pos, y_pos, c_pos, chips = _place()
        sibling = (x_pos, y_pos, 1 - c_pos)
        me = (x_pos, y_pos, c_pos)
        my_chip = 2 * x_pos + y_pos

        def copy(a, k, block, to, src=None):
            return pltpu.make_async_remote_copy(
                src_ref=outs[a].at[block] if src is None else src, dst_ref=outs[a].at[block],
                send_sem=send_sems.at[a, k], recv_sem=recv_sems.at[a, k], device_id=to, device_id_type=MESH)

        def local(a):
            return pltpu.make_async_copy(ins[a], outs[a].at[pl.ds(2 * my_chip, 2)], local_sems.at[a])

        def load(src, first, slot):
            for half in range(2):
                cp = pltpu.make_async_copy(src.at[first + half], wbuf.at[slot, pl.ds(half * (d // 2), d // 2)], load_sem)
                cp.start()
                cp.wait()

        def relay(a, k, block, piece, to):
            rows = halves[a].shape[1] // 2
            ref = outs[a].at[block, pl.ds(piece * rows, rows)]
            return pltpu.make_async_remote_copy(
                src_ref=ref, dst_ref=ref, send_sem=send_sems.at[a, k], recv_sem=recv_sems.at[a, k],
                device_id=to, device_id_type=MESH)

        x_nbr, y_nbr, diagonal = chips
        first_block = lambda chip_xy: 2 * (2 * chip_xy[0] + chip_xy[1])

        def neighbours_arrived(arrays):
            from_x, from_y = first_block(x_nbr) + c_pos, first_block(y_nbr) + c_pos
            for a in arrays:
                copy(a, 0, from_x, me).wait_recv()
                copy(a, 1, from_y, me).wait_recv()
                relay(a, 2, from_x, 0, (*y_nbr, c_pos)).start()
                relay(a, 3, from_y, 1, (*x_nbr, c_pos)).start()
                copy(a, 4, from_x, sibling).start()
                copy(a, 5, from_y, sibling).start()

        def diagonal_arrived(arrays):
            from_diagonal = first_block(diagonal) + c_pos
            for a in arrays:
                relay(a, 2, from_diagonal, 0, me).wait_recv()
                relay(a, 3, from_diagonal, 1, me).wait_recv()
                copy(a, 6, from_diagonal, sibling).start()

        def from_sibling(a, k, chip_xy):
            copy(a, k, first_block(chip_xy) + 1 - c_pos, me).wait_recv()

        @pl.when((phase == 0) & (i == 0))
        def _():
            for a in range(n):
                local(a).start()
            for k, (px, py) in enumerate((x_nbr, y_nbr)):
                for a in range(n):
                    copy(a, k, 2 * my_chip + c_pos, (px, py, c_pos), src=ins[a].at[c_pos]).start()
            load(ins[0], 0, 0)

        @pl.when((phase == 1) & (i == 0))
        def _():
            neighbours_arrived([0])
            from_sibling(0, 4, x_nbr)
            load(outs[0], first_block(x_nbr), 1)

        @pl.when((phase == 2) & (i == 0))
        def _():
            from_sibling(0, 5, y_nbr)
            load(outs[0], first_block(y_nbr), 0)
            neighbours_arrived(range(1, n))

        @pl.when((phase == 3) & (i == 0))
        def _():
            diagonal_arrived(range(n))
            from_sibling(0, 6, diagonal)
            load(outs[0], first_block(diagonal), 1)

        @pl.when(phase == 0)
        def _():
            x = x_ref[...]
            r = lax.rsqrt(jnp.mean(x * x, axis=-1, keepdims=True) + EPS)
            hf = x * r * g_ref[...]
            h_all[i] = hf.astype(BF16)
            ht_ref[...] = hf.T.astype(BF16)

        for slot in range(2):
            @pl.when(phase % 2 == slot)
            def _(slot=slot):
                proj_ref[...] = jnp.dot(h_all[i], wbuf[slot], preferred_element_type=F32)

        for chunk, (c0, _) in qkv_parts.items():
            @pl.when(order_ref[phase] == chunk)
            def _(c0=c0):
                qkv_ref[...] = proj_ref[:, c0:c0 + qkv_w].astype(BF16)

        @pl.when((phase == N_CHIP - 1) & (i == n_row - 1))
        def _():
            for a in range(1, n):
                for k, chip_xy in ((4, x_nbr), (5, y_nbr), (6, diagonal)):
                    from_sibling(a, k, chip_xy)
            for a in range(n):
                for k in (0, 1, 4, 5, 6):
                    copy(a, k, 0, me).wait_send()
                for k in (2, 3):
                    relay(a, k, 0, 0, me).wait_send()
                local(a).wait()

    x_pos, y_pos = lax.axis_index("x"), lax.axis_index("y")
    order = jnp.stack([2 * x_pos + y_pos, 2 * (1 - x_pos) + y_pos, 2 * x_pos + 1 - y_pos,
                       2 * (1 - x_pos) + 1 - y_pos]).astype(jnp.int32)
    holds = [functools.reduce(jnp.logical_or, [order[p] == j for j in qkv_parts]) for p in range(N_CHIP)]
    col = [sum(jnp.where(order[p] == j, q0 // qkv_w, 0) for j, (_, q0) in qkv_parts.items()) for p in range(N_CHIP)]
    cur = col[-1]
    for p in reversed(range(N_CHIP - 1)):
        cur = jnp.where(holds[p], col[p], cur)
    seen = jnp.bool_(False)
    qrow, qcol = [], []
    for p in range(N_CHIP):
        cur = jnp.where(holds[p], col[p], cur)
        qrow.append(jnp.where(holds[p], -1, jnp.where(seen, last, 0)))
        qcol.append(cur)
        seen = seen | holds[p]
    qrow = jnp.stack(qrow).astype(jnp.int32)
    qcol = jnp.stack(qcol).astype(jnp.int32)

    outs = pl.pallas_call(
        body, name="in_proj_fwd",
        grid_spec=pltpu.PrefetchScalarGridSpec(
            num_scalar_prefetch=3, grid=(N_CHIP, n_row),
            in_specs=[pl.BlockSpec((TMF, d), lambda p, i, order, qrow, qcol: (jnp.where(p == 0, i, last), 0)),
                      pl.BlockSpec((1, d), lambda p, i, order, qrow, qcol: (0, 0))] + [ANY] * n,
            out_specs=[pl.BlockSpec((TMF, ncol), lambda p, i, order, qrow, qcol: (i, order[p])),
                       pl.BlockSpec((TMF, qkv_w),
                                    lambda p, i, order, qrow, qcol: (jnp.where(qrow[p] < 0, i, qrow[p]), qcol[p])),
                       pl.BlockSpec((d, TMF), lambda p, i, order, qrow, qcol: (0, jnp.where(p == 0, i, last)))]
            + [ANY] * n,
            scratch_shapes=[pltpu.VMEM((2, d, ncol), BF16), pltpu.VMEM((n_row, TMF, d), BF16),
                            pltpu.SemaphoreType.DMA((n, 7)), pltpu.SemaphoreType.DMA((n, 7)),
                            pltpu.SemaphoreType.DMA((n,)), pltpu.SemaphoreType.DMA]),
        out_shape=[jax.ShapeDtypeStruct((t, N_CHIP * ncol), F32), jax.ShapeDtypeStruct((t, 3 * d), BF16),
                   jax.ShapeDtypeStruct((d, t), BF16)]
        + [jax.ShapeDtypeStruct((2 * N_CHIP,) + hv.shape[1:], BF16) for hv in halves],
        compiler_params=_params(("arbitrary", "arbitrary")),
    )(order, qrow, qcol, x2d, g_in, *halves)
    return outs[0], outs[1], outs[2], outs[3:]


def _branch_a_fwd(a_pre, g_v, wm, b_t):
    t = a_pre.shape[0]
    d = g_v.shape[1]
    d3 = 3 * d
    ng, chunk, _ = wm.shape
    cw = d // ng

    per_step = CHUNKS_PER_STEP if t % (CHUNKS_PER_STEP * chunk) == 0 else 1

    def body(a_ref, gv_ref, wm_ref, bt_ref, ya_ref):
        for n in range(per_step):
            rows = slice(n * chunk, (n + 1) * chunk)
            ua = _gelu(a_ref[rows, 0:d])
            vg = _gelu(a_ref[rows, d:2 * d])
            za = a_ref[rows, 2 * d:3 * d]
            rv = lax.rsqrt(jnp.mean(vg * vg, axis=-1, keepdims=True) + EPS)
            va = (vg * rv * gv_ref[...]).astype(BF16)
            gate = ua * (za * _sigmoid(za))
            for g in range(ng):
                sl = slice(g * cw, (g + 1) * cw)
                mixed = jnp.dot(wm_ref[g], va[:, sl], preferred_element_type=F32) + bt_ref[:, g:g + 1]
                ya_ref[rows, sl] = (gate[:, sl] * mixed).astype(BF16)

    tile = per_step * chunk
    return pl.pallas_call(
        body, name="branch_a_fwd",
        grid=(t // tile,),
        in_specs=[pl.BlockSpec((tile, d3), lambda i: (i, 0)), _const((1, d)), _const(wm.shape), _const(b_t.shape)],
        out_specs=pl.BlockSpec((tile, d), lambda i: (i, 0)),
        out_shape=jax.ShapeDtypeStruct((t, d), BF16),
        compiler_params=_params(("arbitrary",)),
    )(a_pre, g_v, wm, b_t)


def _branch_a_bwd(a_pre, dya, g_v, wm, wm_t, b_t):
    t = a_pre.shape[0]
    d = g_v.shape[1]
    d3 = 3 * d
    ng, chunk, _ = wm.shape
    cw = d // ng
    nsteps = t // chunk

    def body(a_ref, dya_ref, gv_ref, wm_ref, wmt_ref, bt_ref, da_ref, gws_ref, gbt_ref, gnv_ref, db_acc):
        i = pl.program_id(0)

        @pl.when(i == 0)
        def _():
            gws_ref[...] = jnp.zeros_like(gws_ref)
            gnv_ref[...] = jnp.zeros_like(gnv_ref)
            db_acc[...] = jnp.zeros_like(db_acc)

        ua, dgelu_u = _gelu_and_grad(a_ref[:, 0:d])
        vg, dgelu_v = _gelu_and_grad(a_ref[:, d:2 * d])
        za = a_ref[:, 2 * d:3 * d]
        sig = _sigmoid(za)
        sz = za * sig
        dsz = sig * (1.0 + za * (1.0 - sig))
        rv = lax.rsqrt(jnp.mean(vg * vg, axis=-1, keepdims=True) + EPS)
        nv = vg * rv
        gv = gv_ref[...]
        va = (nv * gv).astype(BF16)
        dya = dya_ref[...]
        dmix = dya * ua * sz
        db_acc[...] += dmix
        dmix_b = dmix.astype(BF16)
        t_gate = dya * sz
        t_z = dya * ua * dsz
        dva_parts = []
        for g in range(ng):
            sl = slice(g * cw, (g + 1) * cw)
            mixed = jnp.dot(wm_ref[g], va[:, sl], preferred_element_type=F32) + bt_ref[:, g:g + 1]
            da_ref[:, sl] = (t_gate[:, sl] * mixed * dgelu_u[:, sl]).astype(BF16)
            da_ref[:, 2 * d + g * cw:2 * d + (g + 1) * cw] = (t_z[:, sl] * mixed).astype(BF16)
            gws_ref[g] += lax.dot_general(dmix_b[:, sl], va[:, sl], NT, preferred_element_type=F32)
            dva_parts.append(jnp.dot(wmt_ref[g], dmix_b[:, sl], preferred_element_type=F32))
        dva = jnp.concatenate(dva_parts, axis=1)
        gnv_ref[...] += jnp.sum(dva * nv, axis=0, keepdims=True)
        dnv = dva * gv
        dvg = rv * (dnv - nv * jnp.mean(dnv * nv, axis=-1, keepdims=True))
        da_ref[:, d:2 * d] = (dvg * dgelu_v).astype(BF16)

        @pl.when(i == nsteps - 1)
        def _():
            acc = db_acc[...]
            for g in range(ng):
                gbt_ref[:, g:g + 1] = jnp.sum(acc[:, g * cw:(g + 1) * cw], axis=1, keepdims=True)

    return pl.pallas_call(
        body, name="branch_a_bwd",
        grid=(nsteps,),
        in_specs=[pl.BlockSpec((chunk, d3), lambda i: (i, 0)), pl.BlockSpec((chunk, d), lambda i: (i, 0)),
                  _const((1, d)), _const(wm.shape), _const(wm_t.shape), _const(b_t.shape)],
        out_specs=[pl.BlockSpec((chunk, d3), lambda i: (i, 0)), _const(wm.shape), _const(b_t.shape), _const((1, d))],
        out_shape=[jax.ShapeDtypeStruct((t, d3), BF16), jax.ShapeDtypeStruct(wm.shape, F32),
                   jax.ShapeDtypeStruct(b_t.shape, F32), jax.ShapeDtypeStruct((1, d), F32)],
        scratch_shapes=[pltpu.VMEM((chunk, d), F32)],
        compiler_params=_params(("arbitrary",)),
    )(a_pre, dya, g_v, wm, wm_t, b_t)


def _below_diagonal(n):
    return lax.broadcasted_iota(jnp.int32, (n, n), 0) > lax.broadcasted_iota(jnp.int32, (n, n), 1)


def _twice(tri):
    t = tri.astype(BF16)
    return jnp.concatenate([t, t], axis=0)


def _cumsum_mm(a, tri2):
    hi, lo = _split_bf16(a)
    return jnp.dot(jnp.concatenate([hi, lo], axis=1), tri2, preferred_element_type=F32)


LOG2E = 1.4426950408889634
_SIGN = 0x80000000


def _sb_block(q, k, scale, upper2, causal):
    z2 = lax.dot_general(q, k, NT, preferred_element_type=F32) * (scale * LOG2E)
    neg_abs = lax.bitcast_convert_type(lax.bitcast_convert_type(z2, jnp.uint32) | jnp.uint32(_SIGN), F32)
    l2 = jnp.log(1.0 + jnp.exp2(neg_abs)) * LOG2E
    log_beta = jnp.minimum(z2, 0.0) - l2
    lom = log_beta - z2
    if causal is not None:
        lom = jnp.where(causal, lom, 0.0)
    sx = _cumsum_mm(lom, upper2)
    return log_beta, sx, sx[:, 0:1] + lom[:, 0:1]


DEAD_LOG2 = -150.0


def _max_carry(carries):
    return jnp.max(functools.reduce(jnp.maximum, carries))


ZB_GROUP, GA_GROUP, GB_GROUP = 6, 7, 8


def _attn_specs(d, seq, nq, heads_per_step):
    hp_w = heads_per_step * (d // HEADS)
    n_hp = d // hp_w
    row_blk = lambda group: pl.BlockSpec((ATT_T, hp_w), lambda b, h, i: (b * nq + i, group * n_hp + h))
    seq_blk = lambda group: pl.BlockSpec((seq, hp_w), lambda b, h, i: (b, group * n_hp + h))
    return row_blk, seq_blk, n_hp


def _attn_fwd(qkv, proj, bsz, seq):
    t, d3 = qkv.shape
    d = d3 // 3
    hd = d // HEADS
    nq = seq // ATT_T
    scale = hd ** -0.5
    n_heads = ATT_HP
    row_blk, seq_blk, n_hp = _attn_specs(d, seq, nq, n_heads)

    def body(q_ref, k_ref, v_ref, zb_ref, o_ref, yb_ref):
        i = pl.program_id(2)
        causal = _below_diagonal(ATT_T)
        upper2 = _twice(causal)

        def step(kb, state, mask):
            rows = pl.ds(pl.multiple_of(kb * ATT_T, ATT_T), ATT_T)
            heads = [slice(h * hd, (h + 1) * hd) for h in range(n_heads)]
            scores = [_sb_block(q_ref[:, cols], k_ref[rows, cols], scale, upper2, mask) for cols in heads]
            new = []
            for cols, (carry, acc), (log_beta, sx, total) in zip(heads, state, scores):
                a = jnp.exp2(log_beta + sx + carry)
                if mask is not None:
                    a = jnp.where(mask, a, 0.0)
                acc = acc + jnp.dot(a.astype(BF16), v_ref[rows, cols], preferred_element_type=F32)
                new.append((carry + total, acc))
            return tuple(new)

        init = tuple((jnp.zeros((ATT_T, 1), F32), jnp.zeros((ATT_T, hd), F32)) for _ in range(n_heads))
        state = step(i, init, causal)
        def more(c):
            new = step(c[0], c[1], None)
            return c[0] - 1, new, _max_carry([s[0] for s in new])

        _, state, _ = lax.while_loop(lambda c: (c[0] >= 0) & (c[2] > DEAD_LOG2), more,
                                     (i - 1, state, _max_carry([s[0] for s in state])))
        for h in range(n_heads):
            cols = slice(h * hd, (h + 1) * hd)
            acc = state[h][1]
            zb = zb_ref[:, cols]
            o_ref[:, cols] = acc
            yb_ref[:, cols] = (acc * (zb * _sigmoid(zb))).astype(BF16)

    return pl.pallas_call(
        body, name="attn_fwd",
        grid=(bsz, n_hp, nq),
        in_specs=[row_blk(0), seq_blk(1), seq_blk(2), row_blk(ZB_GROUP)],
        out_specs=[row_blk(0), row_blk(0)],
        out_shape=[jax.ShapeDtypeStruct((t, d), F32), jax.ShapeDtypeStruct((t, d), BF16)],
        compiler_params=_params(("arbitrary", "arbitrary", "arbitrary")),
    )(qkv, qkv, qkv, proj)


def _attn_bwd(qkv, proj, o, dyb, bsz, seq):
    t, d3 = qkv.shape
    d = d3 // 3
    hd = d // HEADS
    nq = seq // ATT_T
    scale = hd ** -0.5
    row_blk, seq_blk, n_hp = _attn_specs(d, seq, nq, ATT_HP)

    def body(q_ref, k_ref, v_ref, zb_ref, o_ref, dyb_ref, dq_ref, dk_ref, dv_ref, dzb_ref,
             g_s, beta_s, dkt_acc, dvt_acc):
        i = pl.program_id(2)

        @pl.when(i == 0)
        def _():
            dkt_acc[...] = jnp.zeros_like(dkt_acc)
            dvt_acc[...] = jnp.zeros_like(dvt_acc)

        causal = _below_diagonal(ATT_T)
        upper2 = _twice(causal)
        lower2 = _twice(~causal)
        zb = zb_ref[...]
        sig = _sigmoid(zb)
        dyb_t = dyb_ref[...]
        do_f = dyb_t * (zb * sig)
        do = do_f.astype(BF16)
        do_t = do_f.T.astype(BF16)
        q_t = q_ref[...].astype(F32).T.astype(BF16)
        dzb_ref[...] = (dyb_t * o_ref[...] * (sig * (1.0 + zb * (1.0 - sig)))).astype(BF16)

        def sweep(kb, carries, mask):
            rows = pl.ds(pl.multiple_of(kb * ATT_T, ATT_T), ATT_T)
            heads = [slice(h * hd, (h + 1) * hd) for h in range(ATT_HP)]
            scores = [_sb_block(q_ref[:, cols], k_ref[rows, cols], scale, upper2, mask) for cols in heads]
            das = [lax.dot_general(do[:, cols], v_ref[rows, cols], NT, preferred_element_type=F32) for cols in heads]
            new = []
            for h, (cols, carry, (log_beta, sx, total), da) in enumerate(zip(heads, carries, scores, das)):
                a = jnp.exp2(log_beta + sx + carry)
                beta = jnp.exp2(log_beta)
                if mask is not None:
                    a = jnp.where(mask, a, 0.0)
                    beta = jnp.where(mask, beta, 0.0)
                g_s[h, kb] = a * da
                beta_s[h, kb] = beta
                dvt_acc[kb, cols, :] += jnp.dot(do_t[cols, :], a.astype(BF16), preferred_element_type=F32)
                new.append(carry + total)
            return tuple(new)

        carries = sweep(i, tuple(jnp.zeros((ATT_T, 1), F32) for _ in range(ATT_HP)), causal)

        def more(c):
            new = sweep(c[0], c[1], None)
            return c[0] - 1, new, _max_carry(new)

        last, _, _ = lax.while_loop(lambda c: (c[0] >= 0) & (c[2] > DEAD_LOG2), more, (i - 1, carries, _max_carry(carries)))
        first_kb = last + 1

        def back(kb, state):
            rows = pl.ds(pl.multiple_of(kb * ATT_T, ATT_T), ATT_T)
            heads = [slice(h * hd, (h + 1) * hd) for h in range(ATT_HP)]
            sums = [_cumsum_mm(g_s[h, kb], lower2) for h in range(ATT_HP)]
            new = []
            for h, (cols, (p_carry, dq), px) in enumerate(zip(heads, state, sums)):
                dz = ((g_s[h, kb] - (p_carry + px) * beta_s[h, kb]) * scale).astype(BF16)
                dq = dq + jnp.dot(dz, k_ref[rows, cols], preferred_element_type=F32)
                dkt_acc[kb, cols, :] += jnp.dot(q_t[cols, :], dz, preferred_element_type=F32)
                new.append((p_carry + px[:, ATT_T - 1:ATT_T], dq))
            return tuple(new)

        init = tuple((jnp.zeros((ATT_T, 1), F32), jnp.zeros((ATT_T, hd), F32)) for _ in range(ATT_HP))
        state = lax.fori_loop(first_kb, i + 1, back, init)
        for h in range(ATT_HP):
            dq_ref[:, h * hd:(h + 1) * hd] = state[h][1].astype(BF16)

        @pl.when(i == nq - 1)
        def _():
            for kb in range(nq):
                dk_ref[kb * ATT_T:(kb + 1) * ATT_T, :] = dkt_acc[kb].T.astype(BF16)
                dv_ref[kb * ATT_T:(kb + 1) * ATT_T, :] = dvt_acc[kb].T.astype(BF16)

    out = jax.ShapeDtypeStruct((t, d), BF16)
    hp_w = ATT_HP * hd
    return pl.pallas_call(
        body, name="attn_bwd",
        grid=(bsz, n_hp, nq),
        in_specs=[row_blk(0), seq_blk(1), seq_blk(2), row_blk(ZB_GROUP), row_blk(0), row_blk(0)],
        out_specs=[row_blk(0), seq_blk(0), seq_blk(0), row_blk(0)],
        out_shape=[out, out, out, out],
        scratch_shapes=[pltpu.VMEM((ATT_HP, nq, ATT_T, ATT_T), F32), pltpu.VMEM((ATT_HP, nq, ATT_T, ATT_T), F32),
                        pltpu.VMEM((nq, hp_w, ATT_T), F32), pltpu.VMEM((nq, hp_w, ATT_T), F32)],
        compiler_params=_params(("arbitrary", "arbitrary", "arbitrary")),
    )(qkv, qkv, qkv, proj, o, dyb)


def _out_proj(ya, yb, g_pre, x2d, tgt, w_og, w_osb, w_out, g_f):
    t, d = x2d.shape

    def body(ya_ref, yb_ref, ga_ref, gb_ref, x_ref, tgt_ref, wog_ref, wosb_ref, wout_ref, gf_ref,
             dya_ref, dyb_ref, dg_ref, dx2_ref, loss_ref, gnf_ref, gwog_ref, gwosb_ref, gwout_ref):
        @pl.when(pl.program_id(0) == 0)
        def _():
            loss_ref[...] = jnp.zeros_like(loss_ref)
            gnf_ref[...] = jnp.zeros_like(gnf_ref)
            gwog_ref[...] = jnp.zeros_like(gwog_ref)
            gwosb_ref[...] = jnp.zeros_like(gwosb_ref)
            gwout_ref[...] = jnp.zeros_like(gwout_ref)

        ya = ya_ref[...]
        yb = yb_ref[...]
        pa = jnp.dot(ya, wog_ref[...], preferred_element_type=F32)
        pb = jnp.dot(yb, wosb_ref[...], preferred_element_type=F32)
        sga = _sigmoid(ga_ref[...])
        sgb = _sigmoid(gb_ref[...])
        merged = (sga * pa + sgb * pb).astype(BF16)
        x2 = x_ref[...] + jnp.dot(merged, wout_ref[...], preferred_element_type=F32)
        r2 = lax.rsqrt(jnp.mean(x2 * x2, axis=-1, keepdims=True) + EPS)
        n2 = x2 * r2
        gf = gf_ref[...]
        err = n2 * gf - tgt_ref[...]
        loss_ref[...] += 0.5 * jnp.sum(jnp.sum(err * err, axis=-1, keepdims=True), axis=0, keepdims=True) / d
        dy = err * (1.0 / d)
        gnf_ref[...] += jnp.sum(dy * n2, axis=0, keepdims=True)
        dn = dy * gf
        dx2 = r2 * (dn - n2 * jnp.mean(dn * n2, axis=-1, keepdims=True))
        dx2_ref[...] = dx2
        dx2_b = dx2.astype(BF16)
        dmerged = lax.dot_general(dx2_b, wout_ref[...], NT, preferred_element_type=F32)
        gwout_ref[...] += lax.dot_general(merged, dx2_b, TN, preferred_element_type=F32)
        dg_ref[:, 0:d] = (dmerged * pa * (sga * (1.0 - sga))).astype(BF16)
        dg_ref[:, d:2 * d] = (dmerged * pb * (sgb * (1.0 - sgb))).astype(BF16)
        dpa = (dmerged * sga).astype(BF16)
        dpb = (dmerged * sgb).astype(BF16)
        dya_ref[...] = lax.dot_general(dpa, wog_ref[...], NT, preferred_element_type=F32)
        dyb_ref[...] = lax.dot_general(dpb, wosb_ref[...], NT, preferred_element_type=F32)
        gwog_ref[...] += lax.dot_general(ya, dpa, TN, preferred_element_type=F32)
        gwosb_ref[...] += lax.dot_general(yb, dpb, TN, preferred_element_type=F32)

    row = lambda i: (i, 0)
    return pl.pallas_call(
        body, name="out_proj",
        grid=(t // TM,),
        in_specs=[pl.BlockSpec((TM, d), row), pl.BlockSpec((TM, d), row),
                  pl.BlockSpec((TM, d), lambda i: (i, GA_GROUP)), pl.BlockSpec((TM, d), lambda i: (i, GB_GROUP)),
                  pl.BlockSpec((TM, d), row), pl.BlockSpec((TM, d), row),
                  _resident((d, d)), _resident((d, d)), _resident((d, d)), _const((1, d))],
        out_specs=[pl.BlockSpec((TM, d), row), pl.BlockSpec((TM, d), row), pl.BlockSpec((TM, 2 * d), row),
                   pl.BlockSpec((TM, d), row), _const((1, 1)), _const((1, d)),
                   _const((d, d)), _const((d, d)), _const((d, d))],
        out_shape=[jax.ShapeDtypeStruct((t, d), F32), jax.ShapeDtypeStruct((t, d), F32),
                   jax.ShapeDtypeStruct((t, 2 * d), BF16), jax.ShapeDtypeStruct((t, d), F32),
                   jax.ShapeDtypeStruct((1, 1), F32), jax.ShapeDtypeStruct((1, d), F32),
                   jax.ShapeDtypeStruct((d, d), F32), jax.ShapeDtypeStruct((d, d), F32),
                   jax.ShapeDtypeStruct((d, d), F32)],
        compiler_params=_params(("arbitrary",)),
    )(ya, yb, g_pre, g_pre, x2d, tgt, w_og, w_osb, w_out, g_f)


def _dproj_pieces(d):
    return [(0, 0, 3), (1, 3, 1), (2, 4, 1), (3, 5, 1), (4, 6, 1), (5, 7, 2)]


def _in_proj_bwd_x(pieces, wg, x2d, g_in, dx2, gw16):
    t, d = x2d.shape
    ncol = wg.shape[2]
    segs = _segments(d, ncol)
    layout = _dproj_pieces(d)
    nsteps = t // TMX

    def body(da_ref, dq_ref, dk_ref, dv_ref, dzb_ref, dg_ref, w_ref, x_ref, g_ref, dx2_ref, gw16_ref,
             gx_ref, gn_ref, recv_ref, send_sems, recv_sems):
        x_pos, y_pos, c_pos, chips = _place()

        def share(k, chunk):
            px, py = chips[k]
            return pltpu.make_async_remote_copy(
                src_ref=gw16_ref.at[:, chunk * ncol:(chunk + 1) * ncol], dst_ref=recv_ref.at[k],
                send_sem=send_sems.at[k], recv_sem=recv_sems.at[k], device_id=(px, py, c_pos), device_id_type=MESH)

        @pl.when(pl.program_id(0) == 0)
        def _():
            gn_ref[...] = jnp.zeros_like(gn_ref)
            for k, (px, py) in enumerate(chips):
                for chunk in range(N_CHIP):
                    @pl.when(2 * px + py == chunk)
                    def _(k=k, chunk=chunk):
                        share(k, chunk).start()

        @pl.when(pl.program_id(0) == nsteps - 1)
        def _():
            for k in range(N_CHIP - 1):
                share(k, 0).wait()

        refs = (da_ref, dq_ref, dk_ref, dv_ref, dzb_ref, dg_ref)
        dh = jnp.zeros((TMX, d), F32)
        for chip, c0, grp, s0, width in segs:
            piece, first, _ = next(p for p in layout if p[1] <= grp < p[1] + p[2])
            off = (grp - first) * d + s0
            dh = dh + lax.dot_general(refs[piece][:, off:off + width], w_ref[chip, :, c0:c0 + width], NT,
                                      preferred_element_type=F32)
        x = x_ref[...]
        r = lax.rsqrt(jnp.mean(x * x, axis=-1, keepdims=True) + EPS)
        n = x * r
        gn_ref[...] += jnp.sum(dh * n, axis=0, keepdims=True)
        dn = dh * g_ref[...]
        gx_ref[...] = dx2_ref[...] + r * (dn - n * jnp.mean(dn * n, axis=-1, keepdims=True))

    row = lambda i: (i, 0)
    return pl.pallas_call(
        body, name="in_proj_bwd_x",
        grid=(nsteps,),
        in_specs=[pl.BlockSpec((TMX, p.shape[1]), row) for p in pieces]
        + [_resident(wg.shape), pl.BlockSpec((TMX, d), row), _const((1, d)), pl.BlockSpec((TMX, d), row), ANY],
        out_specs=[pl.BlockSpec((TMX, d), row), _const((1, d)), ANY],
        out_shape=[jax.ShapeDtypeStruct((t, d), F32), jax.ShapeDtypeStruct((1, d), F32),
                   jax.ShapeDtypeStruct((N_CHIP - 1, gw16.shape[0], ncol), BF16)],
        scratch_shapes=[pltpu.SemaphoreType.DMA((N_CHIP - 1,)), pltpu.SemaphoreType.DMA((N_CHIP - 1,))],
        compiler_params=_params(("arbitrary",)),
    )(*pieces, wg, x2d, g_in, dx2, gw16)


def _in_proj_bwd_w(h_t, pieces, mats16, pack):
    d, t = h_t.shape
    nk = t // TKW
    half = d // 2
    layout = _dproj_pieces(d)
    n_mats = len(mats16)
    n_dev = 2 * N_CHIP
    flips = [(dx, dy, dc) for dx in (0, 1) for dy in (0, 1) for dc in (0, 1)][1:]

    def body(ht_ref, da_ref, dq_ref, dk_ref, dv_ref, dzb_ref, dg_ref, *rest):
        mat_refs, pack_ref = rest[:n_mats], rest[n_mats]
        gw_ref, sib_ref = rest[n_mats + 1:n_mats + 3]
        recv_refs, slots_ref = rest[n_mats + 3:2 * n_mats + 3], rest[2 * n_mats + 3]
        acc, stage, mat_send, mat_recv, pack_send, pack_recv, own_sem, stage_send, sib_recv = rest[2 * n_mats + 4:]
        s = pl.program_id(0)
        i = pl.program_id(1)
        x_pos, y_pos, c_pos, chips = _place()
        me = 4 * x_pos + 2 * y_pos + c_pos

        def to_sibling(slot, group):
            return pltpu.make_async_remote_copy(
                src_ref=stage.at[slot], dst_ref=sib_ref.at[:, group * d:(group + 1) * d],
                send_sem=stage_send.at[slot], recv_sem=sib_recv, device_id=(x_pos, y_pos, 1 - c_pos), device_id_type=MESH)

        def exchanges():
            cps = []
            for k, (px, py) in enumerate(chips):
                for a in range(n_mats):
                    cps.append(pltpu.make_async_remote_copy(
                        src_ref=mat_refs[a].at[2 * px + py], dst_ref=recv_refs[a].at[k],
                        send_sem=mat_send.at[a, k], recv_sem=mat_recv.at[a, k],
                        device_id=(px, py, c_pos), device_id_type=MESH))
            for k, (dx, dy, dc) in enumerate(flips):
                peer = (1 - x_pos if dx else x_pos, 1 - y_pos if dy else y_pos, 1 - c_pos if dc else c_pos)
                cps.append(pltpu.make_async_remote_copy(
                    src_ref=pack_ref, dst_ref=slots_ref.at[me], send_sem=pack_send.at[k], recv_sem=pack_recv.at[k],
                    device_id=peer, device_id_type=MESH))
            return cps, pltpu.make_async_copy(pack_ref, slots_ref.at[me], own_sem)

        @pl.when((s == 0) & (i == 0))
        def _():
            cps, own = exchanges()
            own.start()
            for cp in cps:
                cp.start()

        @pl.when(i == 0)
        def _():
            acc[...] = jnp.zeros_like(acc)

        refs = (da_ref, dq_ref, dk_ref, dv_ref, dzb_ref, dg_ref)
        for piece, first, count in layout:
            @pl.when((s >= first) & (s < first + count))
            def _(piece=piece):
                acc[...] += jnp.dot(ht_ref[...], refs[piece][...], preferred_element_type=F32)

        @pl.when(i == nk - 1)
        def _():
            gw_ref[...] = acc[...]
            for slot in range(2):
                @pl.when(s % 2 == slot)
                def _(slot=slot):
                    @pl.when(s >= 2)
                    def _():
                        to_sibling(slot, 0).wait_send()
                    for other in range(2):
                        @pl.when(c_pos == 1 - other)
                        def _(other=other):
                            stage[slot] = acc[other * half:(other + 1) * half, :].astype(BF16)
                    for group in range(N_SPLIT):
                        @pl.when(s == group)
                        def _(group=group):
                            to_sibling(slot, group).start()

        @pl.when((s == N_SPLIT - 1) & (i == nk - 1))
        def _():
            for slot in range(2):
                to_sibling(slot, 0).wait_send()
            pltpu.make_async_remote_copy(
                src_ref=sib_ref, dst_ref=sib_ref, send_sem=stage_send.at[0], recv_sem=sib_recv,
                device_id=(x_pos, y_pos, c_pos), device_id_type=MESH).wait_recv()
            cps, own = exchanges()
            own.wait()
            for cp in cps:
                cp.wait()

    def piece_spec(p, first, count):
        def index(s, i):
            mine = (s >= first) & (s < first + count)
            return jnp.where(mine, i, 0), jnp.where(mine, s - first, 0)
        return pl.BlockSpec((TKW, d), index)

    col_blk = pl.BlockSpec((d, d), lambda s, i: (0, s))
    outs = pl.pallas_call(
        body, name="in_proj_bwd_w",
        grid=(N_SPLIT, nk),
        in_specs=[pl.BlockSpec((d, TKW), lambda s, i: (0, i))] + [piece_spec(*p) for p in layout] + [ANY] * (n_mats + 1),
        out_specs=[col_blk, ANY] + [ANY] * (n_mats + 1),
        out_shape=[jax.ShapeDtypeStruct((d, N_SPLIT * d), F32), jax.ShapeDtypeStruct((half, N_SPLIT * d), BF16)]
        + [jax.ShapeDtypeStruct((N_CHIP - 1,) + m.shape[1:], BF16) for m in mats16]
        + [jax.ShapeDtypeStruct((n_dev,) + pack.shape, F32)],
        scratch_shapes=[pltpu.VMEM((d, d), F32), pltpu.VMEM((2, half, d), BF16),
                        pltpu.SemaphoreType.DMA((n_mats, N_CHIP - 1)), pltpu.SemaphoreType.DMA((n_mats, N_CHIP - 1)),
                        pltpu.SemaphoreType.DMA((n_dev - 1,)), pltpu.SemaphoreType.DMA((n_dev - 1,)),
                        pltpu.SemaphoreType.DMA, pltpu.SemaphoreType.DMA((2,)), pltpu.SemaphoreType.DMA],
        compiler_params=_params(("arbitrary", "arbitrary")),
    )(h_t, *pieces, *mats16, pack)
    return outs[0], outs[1], outs[2:2 + n_mats], outs[2 + n_mats]


def _local_step(proj, qkv, x2d, tgt2d, bsz, seq, norm_v, w_s, b_s, w_og, w_osb, w_out, norm_final):
    d = x2d.shape[1]
    chunk = w_s.shape[-1]
    causal = jnp.tril(jnp.ones((chunk, chunk), dtype=bool))
    wm = jnp.where(causal[None], w_s, 0.0).astype(BF16)
    wm_t = jnp.swapaxes(wm, 1, 2)
    b_t = b_s.T

    ya = _branch_a_fwd(proj, norm_v, wm, b_t)
    o, yb = _attn_fwd(qkv, proj, bsz, seq)
    dya, dyb, dg, dx2, loss, g_nf, g_wog, g_wosb, g_wout = _out_proj(
        ya, yb, proj, x2d, tgt2d, w_og, w_osb, w_out, norm_final.reshape(1, d))
    dq, dk, dv, dzb = _attn_bwd(qkv, proj, o, dyb, bsz, seq)
    d_a, g_ws, g_bt, g_nv = _branch_a_bwd(proj, dya, norm_v, wm, wm_t, b_t)
    g_ws = jnp.where(causal[None], g_ws, 0.0)
    return loss, (d_a, dq, dk, dv, dzb, dg), dx2, g_nv, g_ws, g_bt.T, g_wog, g_wosb, g_wout, g_nf


def _row_tile(rows):
    return next(r for r in (128, 64, 32, 16, 8) if rows % r == 0)


def _cast_bf16(arrs):
    n = len(arrs)

    def body(*refs):
        for a_ref, o_ref in zip(refs[:n], refs[n:]):
            o_ref[...] = a_ref[...].astype(BF16)

    specs = [pl.BlockSpec((a.shape[0] // CAST_STEPS, a.shape[1]), lambda i: (i, 0)) for a in arrs]
    return pl.pallas_call(
        body, name="cast_bf16", grid=(CAST_STEPS,),
        in_specs=specs, out_specs=specs,
        out_shape=[jax.ShapeDtypeStruct(a.shape, BF16) for a in arrs],
        compiler_params=_params(("arbitrary",)),
    )(*arrs)


def _chip_half(full, from_sibling, core, chip, tile):
    half, n = from_sibling.shape

    def body(where_ref, own_ref, sib_ref, o32_ref, o16_ref):
        total = own_ref[...] + sib_ref[...].astype(F32)
        o16_ref[...] = total.astype(BF16)

        @pl.when(pl.program_id(0) == where_ref[1])
        def _():
            o32_ref[...] = total

    blk = pl.BlockSpec((half, tile), lambda j, where_ref: (0, j))
    return pl.pallas_call(
        body, name="chip_half",
        grid_spec=pltpu.PrefetchScalarGridSpec(
            num_scalar_prefetch=1, grid=(n // tile,),
            in_specs=[pl.BlockSpec((half, tile), lambda j, where_ref: (where_ref[0], j)), blk],
            out_specs=[pl.BlockSpec((half, tile), lambda j, where_ref: (0, 0)), blk]),
        out_shape=[jax.ShapeDtypeStruct((half, tile), F32), jax.ShapeDtypeStruct((half, n), BF16)],
        compiler_params=_params(("arbitrary",)),
    )(jnp.stack([core, chip]).astype(jnp.int32), full, from_sibling)


def _add_received(fulls, recvs, chip, by_cols):
    n = len(fulls)
    _, rows, cols = recvs[0].shape
    tr = _row_tile(rows)
    nb = rows // tr

    def body(chip_ref, *refs):
        for own_ref, recv_ref, o_ref in zip(refs[:n], refs[n:2 * n], refs[2 * n:]):
            s = own_ref[...]
            for k in range(N_CHIP - 1):
                s = s + recv_ref[k].astype(F32)
            o_ref[...] = s

    own_map = (lambda i, chip_ref: (i, chip_ref[0])) if by_cols else (lambda i, chip_ref: (chip_ref[0] * nb + i, 0))
    return pl.pallas_call(
        body, name="add_received",
        grid_spec=pltpu.PrefetchScalarGridSpec(
            num_scalar_prefetch=1, grid=(nb,),
            in_specs=[pl.BlockSpec((tr, cols), own_map)] * n
            + [pl.BlockSpec((N_CHIP - 1, tr, cols), lambda i, chip_ref: (0, i, 0))] * n,
            out_specs=[pl.BlockSpec((tr, cols), lambda i, chip_ref: (i, 0))] * n),
        out_shape=[jax.ShapeDtypeStruct((rows, cols), F32)] * n,
        compiler_params=_params(("arbitrary",)),
    )(chip.reshape(1).astype(jnp.int32), *fulls, *recvs)


def _adamw_math(w, m, v, g):
    new_m = ADAM_B1 * m + (1.0 - ADAM_B1) * g
    new_v = ADAM_B2 * v + (1.0 - ADAM_B2) * (g * g)
    m_hat = new_m / (1.0 - ADAM_B1 ** ADAM_STEP)
    v_hat = new_v / (1.0 - ADAM_B2 ** ADAM_STEP)
    return -ADAM_LR * (m_hat / (jnp.sqrt(v_hat) + ADAM_EPS) + ADAM_WD * w), new_m, new_v


def _adamw_pairs(ws, ms, vs, mines, theirs):
    n = len(ws)
    rows, cols = ws[0].shape
    tr = _row_tile(rows)

    def body(*refs):
        ins, outs = refs[:5 * n], refs[5 * n:]
        for a in range(n):
            w_ref, m_ref, v_ref, mine_ref, theirs_ref = ins[a::n]
            g_ref, d_ref, nm_ref, nv_ref = outs[4 * a:4 * a + 4]
            g = mine_ref[...] + theirs_ref[...]
            g_ref[...] = g
            d_ref[...], nm_ref[...], nv_ref[...] = _adamw_math(w_ref[...], m_ref[...], v_ref[...], g)

    spec = pl.BlockSpec((tr, cols), lambda i: (i, 0))
    out = jax.ShapeDtypeStruct((rows, cols), F32)
    outs = pl.pallas_call(
        body, name="adamw_pairs", grid=(rows // tr,),
        in_specs=[spec] * (5 * n), out_specs=[spec] * (4 * n), out_shape=[out] * (4 * n),
        compiler_params=_params(("arbitrary",)),
    )(*ws, *ms, *vs, *mines, *theirs)
    return [outs[4 * a:4 * a + 4] for a in range(n)]


def _adamw_halves(w, m, v, mine, theirs, core):
    rows, cols = w.shape
    tr = _row_tile(rows // 2)
    per_half = rows // 2 // tr

    def body(core_ref, w_ref, m_ref, v_ref, mine_ref, theirs_ref, g_ref, d_ref, nm_ref, nv_ref):
        is_mine = pl.program_id(0) // per_half == core_ref[0]
        for part, cond in ((mine_ref, is_mine), (theirs_ref, jnp.logical_not(is_mine))):
            @pl.when(cond)
            def _(part=part):
                g = part[...]
                g_ref[...] = g
                d_ref[...], nm_ref[...], nv_ref[...] = _adamw_math(w_ref[...], m_ref[...], v_ref[...], g)

    spec = pl.BlockSpec((tr, cols), lambda i, core_ref: (i, 0))

    def half_spec(own):
        def index(i, core_ref):
            in_core_half = i // per_half == core_ref[0]
            here = in_core_half if own else jnp.logical_not(in_core_half)
            return jnp.where(here, i % per_half, 0), 0
        return pl.BlockSpec((tr, cols), index)

    out = jax.ShapeDtypeStruct(w.shape, F32)
    return pl.pallas_call(
        body, name="adamw_halves",
        grid_spec=pltpu.PrefetchScalarGridSpec(
            num_scalar_prefetch=1, grid=(rows // tr,),
            in_specs=[spec] * 3 + [half_spec(True), half_spec(False)], out_specs=[spec] * 4),
        out_shape=[out] * 4,
        compiler_params=_params(("arbitrary",)),
    )(core.reshape(1).astype(jnp.int32), w, m, v, mine, theirs)


def _adamw_small(ws, ms, vs, slots_head, slots_tail):
    n_dev = slots_head.shape[0]
    n_par = len(ws)
    groups, chunk, _ = ws[4].shape

    def body(*refs):
        w_refs, m_refs, v_refs = refs[:n_par], refs[n_par:2 * n_par], refs[2 * n_par:3 * n_par]
        head_ref, tail_ref = refs[3 * n_par:3 * n_par + 2]
        out_refs, scalar_ref = refs[3 * n_par + 2:-1], refs[-1]

        def total(ref, rows, cols):
            g = ref[0, rows, cols]
            for i in range(1, n_dev):
                g = g + ref[i, rows, cols]
            return g

        def update(p, g, at=...):
            g_ref, d_ref, nm_ref, nv_ref = out_refs[4 * p:4 * p + 4]
            g_ref[at] = g
            d_ref[at], nm_ref[at], nv_ref[at] = _adamw_math(w_refs[p][at], m_refs[p][at], v_refs[p][at], g)

        every = slice(None)
        update(0, total(head_ref, slice(0, 1), every))
        update(1, total(tail_ref, slice(0, 1), every))
        update(2, total(tail_ref, slice(SLAB, SLAB + groups), slice(0, chunk)))
        update(3, total(tail_ref, slice(2 * SLAB, 2 * SLAB + 1), every))
        scalar_ref[...] = total(tail_ref, slice(3 * SLAB, 3 * SLAB + 1), slice(0, chunk))[:, 0:1]
        for grp in range(groups):
            update(4, total(tail_ref, slice(4 * SLAB, 4 * SLAB + chunk), slice(grp * chunk, (grp + 1) * chunk)), grp)

    vmem = pl.BlockSpec(memory_space=pltpu.VMEM)
    return pl.pallas_call(
        body, name="adamw_small", in_specs=[vmem] * (3 * n_par + 2), out_specs=[vmem] * (4 * n_par + 1),
        out_shape=[jax.ShapeDtypeStruct(w.shape, F32) for w in ws for _ in range(4)]
        + [jax.ShapeDtypeStruct((1, 1), F32)],
        compiler_params=pltpu.CompilerParams(vmem_limit_bytes=VMEM_LIMIT),
    )(*ws, *ms, *vs, slots_head, slots_tail)


ANY = pl.BlockSpec(memory_space=pl.ANY)


def _place():
    x, y, c = lax.axis_index("x"), lax.axis_index("y"), lax.axis_index("c")
    other_chips = [(1 - x, y), (x, 1 - y), (1 - x, 1 - y)]
    return x, y, c, other_chips


def _swap_and_gather(arrs, pack):
    n = len(arrs)
    n_dev = 2 * N_CHIP
    flips = [(dx, dy, dc) for dx in (0, 1) for dy in (0, 1) for dc in (0, 1)][1:]

    def body(*refs):
        ins, pack_ref = refs[:n], refs[n]
        outs, slots_ref = refs[n + 1:2 * n + 1], refs[2 * n + 1]
        send_sems, recv_sems, pack_send, pack_recv, own_sem = refs[2 * n + 2:]
        x, y, c, _ = _place()
        me = 4 * x + 2 * y + c
        own = pltpu.make_async_copy(pack_ref, slots_ref.at[me], own_sem)
        own.start()
        copies = []
        for k, (dx, dy, dc) in enumerate(flips):
            peer = (1 - x if dx else x, 1 - y if dy else y, 1 - c if dc else c)
            copies.append(pltpu.make_async_remote_copy(
                src_ref=pack_ref, dst_ref=slots_ref.at[me], send_sem=pack_send.at[k], recv_sem=pack_recv.at[k],
                device_id=peer, device_id_type=MESH))
        copies += [pltpu.make_async_remote_copy(
            src_ref=ins[a], dst_ref=outs[a], send_sem=send_sems.at[a], recv_sem=recv_sems.at[a],
            device_id=(x, y, 1 - c), device_id_type=MESH) for a in range(n)]
        for cp in copies:
            cp.start()
        for cp in copies:
            cp.wait()
        own.wait()

    return pl.pallas_call(
        body, name="swap_and_gather",
        in_specs=[ANY] * (n + 1), out_specs=[ANY] * (n + 1),
        out_shape=[jax.ShapeDtypeStruct(a.shape, a.dtype) for a in arrs] + [jax.ShapeDtypeStruct((n_dev,) + pack.shape, F32)],
        scratch_shapes=[pltpu.SemaphoreType.DMA((n,)), pltpu.SemaphoreType.DMA((n,)),
                        pltpu.SemaphoreType.DMA((n_dev - 1,)), pltpu.SemaphoreType.DMA((n_dev - 1,)),
                        pltpu.SemaphoreType.DMA],
    )(*arrs, pack)


SLAB = 8


def _slab(vec, d):
    return jnp.pad(vec.reshape(1, d), ((0, SLAB - 1), (0, 0)))


def _pack_tail(vec_nv, b_s, vec_nf, w_s, scalar):
    d = vec_nv.shape[-1]
    groups, chunk, _ = w_s.shape
    assert groups == SLAB and groups * chunk == d
    slabs = [_slab(vec_nv, d), jnp.pad(b_s, ((0, 0), (0, d - chunk))), _slab(vec_nf, d),
             jnp.pad(scalar, ((0, SLAB - 1), (0, d - 1)))]
    return jnp.concatenate(slabs + [jnp.swapaxes(w_s, 0, 1).reshape(chunk, d)], axis=0)


def kernel(x, norm_in, w_in, norm_v, w_s, b_s, w_o_gmlp, w_o_sb, w_out, norm_final, loss_target, m_norm_in, m_w_in, m_norm_v, m_w_s, m_b_s, m_w_o_gmlp, m_w_o_sb, m_w_out, m_norm_final, v_norm_in, v_w_in, v_norm_v, v_w_s, v_b_s, v_w_o_gmlp, v_w_o_sb, v_w_out, v_norm_final):
    d = x.shape[-1]
    ncol = w_in.shape[-1]
    nrow = w_o_gmlp.shape[-2]
    chip = 2 * lax.axis_index("x") + lax.axis_index("y")

    bsz, seq, _ = x.shape
    x2d = x.reshape(bsz * seq, d)
    shards = [w_in[0], w_o_gmlp[0], w_o_sb[0], w_out[0]]
    halves = [s16.reshape(2, s16.shape[0] // 2, s16.shape[1]) for s16 in _cast_bf16(shards)]
    proj, qkv, h_t, (wg, w_og, w_osb, w_o) = _in_proj_fwd(x2d, norm_in, halves)
    wg = wg.reshape(N_CHIP, d, ncol)

    loss, pieces, dx2, g_nv, g_ws, g_bs, g_wog, g_wosb, g_wout, g_nf = _local_step(
        proj, qkv, x2d, loss_target.reshape(bsz * seq, d), bsz, seq, norm_v, w_s[0], b_s[0],
        w_og.reshape(d, d), w_osb.reshape(d, d), w_o.reshape(d, d), norm_final)

    mats = [g_wog, g_wosb, g_wout]
    mats16 = [g16.reshape(N_CHIP, nrow, d) for g16 in _cast_bf16(mats)]
    g_win, from_sibling, recv_mats, slots_tail = _in_proj_bwd_w(
        h_t, pieces, mats16, _pack_tail(g_nv, g_bs, g_nf, g_ws, loss))
    core = lax.axis_index("c")
    half_own, half_win16 = _chip_half(g_win, from_sibling, core, chip, ncol)
    grad_x, g_nin, recv_win = _in_proj_bwd_x(pieces, wg, x2d, norm_in, dx2, half_win16)
    grad_x = grad_x.reshape(bsz, seq, d)

    sums = _add_received([half_own], [recv_win], jnp.zeros((), jnp.int32), True) + _add_received(
        mats, recv_mats, chip, False)
    *sibling_sums, slots_head = _swap_and_gather(sums, _slab(g_nin, d))
    stats = [_adamw_halves(w_in[0], m_w_in[0], v_w_in[0], sums[0], sibling_sums[0], core)] + _adamw_pairs(
        shards[1:], [m_w_o_gmlp[0], m_w_o_sb[0], m_w_out[0]], [v_w_o_gmlp[0], v_w_o_sb[0], v_w_out[0]],
        sums[1:], sibling_sums[1:])

    *small, loss = _adamw_small(
        [norm_in, norm_v, b_s[0], norm_final.reshape(1, d), w_s[0]],
        [m_norm_in, m_norm_v, m_b_s[0], m_norm_final.reshape(1, d), m_w_s[0]],
        [v_norm_in, v_norm_v, v_b_s[0], v_norm_final.reshape(1, d), v_w_s[0]], slots_head, slots_tail)

    out = []
    for kind, (win, wog, wosb, wout) in enumerate(zip(*stats)):
        nin, nv, bs, nf, ws = small[kind::4]
        out += [nin, win[None], nv, ws[None], bs[None], wog[None], wosb[None], wout[None], nf.reshape(d)]
    return (loss.reshape(()), grad_x, *out)
```

```python
import functools
import math

import jax
import jax.numpy as jnp
from jax import lax
from jax.experimental import pallas as pl
from jax.experimental.pallas import tpu as pltpu

F32 = jnp.float32
BF16 = jnp.bfloat16
EPS = 1e-6
HEADS = 8
N_SPLIT = 9
N_CHIP = 4
MESH = pl.DeviceIdType.MESH

ADAM_LR = 0.001
ADAM_B1 = 0.9
ADAM_B2 = 0.999
ADAM_EPS = 1e-08
ADAM_WD = 0.01
ADAM_STEP = 10

VMEM_LIMIT = 56 * 2 ** 20
TM = 256
TMF = 512
TMX = 512
ATT_T = 256
ATT_HP = 4
TKW = 1024
CHUNKS_PER_STEP = 4
CAST_STEPS = 8

NT = (((1,), (1,)), ((), ()))
TN = (((0,), (0,)), ((), ()))


def _params(sem):
    return pltpu.CompilerParams(dimension_semantics=sem, vmem_limit_bytes=VMEM_LIMIT)


def _resident(shape):
    nd = len(shape)
    return pl.BlockSpec(shape, lambda *_: (0,) * nd, pipeline_mode=pl.Buffered(1))


def _const(shape):
    nd = len(shape)
    return pl.BlockSpec(shape, lambda *_: (0,) * nd)


def _segments(d, ncol):
    segs = []
    edges = sorted({j * ncol for j in range(N_CHIP + 1)} | {s * d for s in range(N_SPLIT + 1)})
    for lo, hi in zip(edges[:-1], edges[1:]):
        segs.append((lo // ncol, lo % ncol, lo // d, lo % d, hi - lo))
    return segs


def _sigmoid(x):
    return 0.5 * jnp.tanh(0.5 * x) + 0.5


_GELU_C = math.sqrt(2.0 / math.pi)


_GELU_CA = _GELU_C * 0.044715


def _gelu(x):
    return x * (0.5 * jnp.tanh(x * (_GELU_C + _GELU_CA * (x * x))) + 0.5)


def _gelu_and_grad(x):
    x2 = x * x
    u = 0.5 * jnp.tanh(x * (_GELU_C + _GELU_CA * x2)) + 0.5
    slope = (1.0 - u) * (x * (_GELU_C + (3.0 * _GELU_CA) * x2))
    return x * u, u * (2.0 * slope + 1.0)


def _split_bf16(a):
    hi = a.astype(BF16)
    lo = (a - hi.astype(F32)).astype(BF16)
    return hi, lo


def _in_proj_fwd(x2d, g_in, halves):
    t, d = x2d.shape
    n = len(halves)
    ncol = halves[0].shape[2]
    n_row = t // TMF
    last = n_row - 1
    assert halves[0].shape[1] * 2 == d
    qkv_parts = {j: (max(j * ncol, 3 * d) - j * ncol, max(j * ncol, 3 * d) - 3 * d)
                 for j in range(N_CHIP) if min((j + 1) * ncol, 6 * d) > max(j * ncol, 3 * d)}
    qkv_w = 3 * d // len(qkv_parts)
    assert all(min((j + 1) * ncol, 6 * d) - max(j * ncol, 3 * d) == qkv_w and q0 % qkv_w == 0
               for j, (_, q0) in qkv_parts.items())

    def body(order_ref, qrow_ref, qcol_ref, x_ref, g_ref, *rest):
        ins = rest[:n]
        proj_ref, qkv_ref, ht_ref = rest[n:n + 3]
        outs = rest[n + 3:2 * n + 3]
        wbuf, h_all, send_sems, recv_sems, local_sems, load_sem = rest[2 * n + 3:]
        phase = pl.program_id(0)
        i = pl.program_id(1)
        x_pos, y_pos, c_pos, chips = _place()
        sibling = (x_pos, y_pos, 1 - c_pos)
        me = (x_pos, y_pos, c_pos)
        my_chip = 2 * x_pos + y_pos

        def copy(a, k, block, to, src=None):
            return pltpu.make_async_remote_copy(
                src_ref=outs[a].at[block] if src is None else src, dst_ref=outs[a].at[block],
                send_sem=send_sems.at[a, k], recv_sem=recv_sems.at[a, k], device_id=to, device_id_type=MESH)

        def local(a):
            return pltpu.make_async_copy(ins[a], outs[a].at[pl.ds(2 * my_chip, 2)], local_sems.at[a])

        def load(src, first, slot):
            for half in range(2):
                cp = pltpu.make_async_copy(src.at[first + half], wbuf.at[slot, pl.ds(half * (d // 2), d // 2)], load_sem)
                cp.start()
                cp.wait()

        def relay(a, k, block, piece, to):
            rows = halves[a].shape[1] // 2
            ref = outs[a].at[block, pl.ds(piece * rows, rows)]
            return pltpu.make_async_remote_copy(
                src_ref=ref, dst_ref=ref, send_sem=send_sems.at[a, k], recv_sem=recv_sems.at[a, k],
                device_id=to, device_id_type=MESH)

        x_nbr, y_nbr, diagonal = chips
        first_block = lambda chip_xy: 2 * (2 * chip_xy[0] + chip_xy[1])

        def send_half(k, to_xy):
            for a in range(n):
                copy(a, k, 2 * my_chip + c_pos, (*to_xy, c_pos), src=ins[a].at[c_pos]).start()

        def neighbour_arrived(arrays, k):
            block = first_block(chips[k]) + c_pos
            for a in arrays:
                copy(a, k, block, me).wait_recv()
                relay(a, 2 + k, block, k, (*chips[1 - k], c_pos)).start()
                copy(a, 4 + k, block, sibling).start()

        def diagonal_arrived(arrays):
            from_diagonal = first_block(diagonal) + c_pos
            for a in arrays:
                relay(a, 2, from_diagonal, 0, me).wait_recv()
                relay(a, 3, from_diagonal, 1, me).wait_recv()
                copy(a, 6, from_diagonal, sibling).start()

        def from_sibling(a, k, chip_xy):
            copy(a, k, first_block(chip_xy) + 1 - c_pos, me).wait_recv()

        @pl.when((phase == 0) & (i == 0))
        def _():
            for a in range(n):
                local(a).start()
            send_half(0, x_nbr)
            load(ins[0], 0, 0)

        @pl.when((phase == 0) & (i == n_row // 2))
        def _():
            send_half(1, y_nbr)

        @pl.when((phase == 1) & (i == 0))
        def _():
            neighbour_arrived([0], 0)
            from_sibling(0, 4, x_nbr)
            load(outs[0], first_block(x_nbr), 1)

        @pl.when((phase == 1) & (i == n_row // 2))
        def _():
            neighbour_arrived([0], 1)

        @pl.when((phase == 2) & (i == 0))
        def _():
            from_sibling(0, 5, y_nbr)
            load(outs[0], first_block(y_nbr), 0)
            for k in range(2):
                neighbour_arrived(range(1, n), k)

        @pl.when((phase == 3) & (i == 0))
        def _():
            diagonal_arrived(range(n))
            from_sibling(0, 6, diagonal)
            load(outs[0], first_block(diagonal), 1)

        @pl.when(phase == 0)
        def _():
            x = x_ref[...]
            r = lax.rsqrt(jnp.mean(x * x, axis=-1, keepdims=True) + EPS)
            hf = x * r * g_ref[...]
            h_all[i] = hf.astype(BF16)
            ht_ref[...] = hf.T.astype(BF16)

        for slot in range(2):
            @pl.when(phase % 2 == slot)
            def _(slot=slot):
                proj_ref[...] = jnp.dot(h_all[i], wbuf[slot], preferred_element_type=F32)

        for chunk, (c0, _) in qkv_parts.items():
            @pl.when(order_ref[phase] == chunk)
            def _(c0=c0):
                qkv_ref[...] = proj_ref[:, c0:c0 + qkv_w].astype(BF16)

        @pl.when((phase == N_CHIP - 1) & (i == n_row - 1))
        def _():
            for a in range(1, n):
                for k, chip_xy in ((4, x_nbr), (5, y_nbr), (6, diagonal)):
                    from_sibling(a, k, chip_xy)
            for a in range(n):
                for k in (0, 1, 4, 5, 6):
                    copy(a, k, 0, me).wait_send()
                for k in (2, 3):
                    relay(a, k, 0, 0, me).wait_send()
                local(a).wait()

    x_pos, y_pos = lax.axis_index("x"), lax.axis_index("y")
    order = jnp.stack([2 * x_pos + y_pos, 2 * (1 - x_pos) + y_pos, 2 * x_pos + 1 - y_pos,
                       2 * (1 - x_pos) + 1 - y_pos]).astype(jnp.int32)
    holds = [functools.reduce(jnp.logical_or, [order[p] == j for j in qkv_parts]) for p in range(N_CHIP)]
    col = [sum(jnp.where(order[p] == j, q0 // qkv_w, 0) for j, (_, q0) in qkv_parts.items()) for p in range(N_CHIP)]
    cur = col[-1]
    for p in reversed(range(N_CHIP - 1)):
        cur = jnp.where(holds[p], col[p], cur)
    seen = jnp.bool_(False)
    qrow, qcol = [], []
    for p in range(N_CHIP):
        cur = jnp.where(holds[p], col[p], cur)
        qrow.append(jnp.where(holds[p], -1, jnp.where(seen, last, 0)))
        qcol.append(cur)
        seen = seen | holds[p]
    qrow = jnp.stack(qrow).astype(jnp.int32)
    qcol = jnp.stack(qcol).astype(jnp.int32)

    outs = pl.pallas_call(
        body, name="in_proj_fwd",
        grid_spec=pltpu.PrefetchScalarGridSpec(
            num_scalar_prefetch=3, grid=(N_CHIP, n_row),
            in_specs=[pl.BlockSpec((TMF, d), lambda p, i, order, qrow, qcol: (jnp.where(p == 0, i, last), 0)),
                      pl.BlockSpec((1, d), lambda p, i, order, qrow, qcol: (0, 0))] + [ANY] * n,
            out_specs=[pl.BlockSpec((TMF, ncol), lambda p, i, order, qrow, qcol: (i, order[p])),
                       pl.BlockSpec((TMF, qkv_w),
                                    lambda p, i, order, qrow, qcol: (jnp.where(qrow[p] < 0, i, qrow[p]), qcol[p])),
                       pl.BlockSpec((d, TMF), lambda p, i, order, qrow, qcol: (0, jnp.where(p == 0, i, last)))]
            + [ANY] * n,
            scratch_shapes=[pltpu.VMEM((2, d, ncol), BF16), pltpu.VMEM((n_row, TMF, d), BF16),
                            pltpu.SemaphoreType.DMA((n, 7)), pltpu.SemaphoreType.DMA((n, 7)),
                            pltpu.SemaphoreType.DMA((n,)), pltpu.SemaphoreType.DMA]),
        out_shape=[jax.ShapeDtypeStruct((t, N_CHIP * ncol), F32), jax.ShapeDtypeStruct((t, 3 * d), BF16),
                   jax.ShapeDtypeStruct((d, t), BF16)]
        + [jax.ShapeDtypeStruct((2 * N_CHIP,) + hv.shape[1:], BF16) for hv in halves],
        compiler_params=_params(("arbitrary", "arbitrary")),
    )(order, qrow, qcol, x2d, g_in, *halves)
    return outs[0], outs[1], outs[2], outs[3:]


def _branch_a_fwd(a_pre, g_v, wm, b_t):
    t = a_pre.shape[0]
    d = g_v.shape[1]
    d3 = 3 * d
    ng, chunk, _ = wm.shape
    cw = d // ng

    per_step = CHUNKS_PER_STEP if t % (CHUNKS_PER_STEP * chunk) == 0 else 1

    def body(a_ref, gv_ref, wm_ref, bt_ref, ya_ref):
        for n in range(per_step):
            rows = slice(n * chunk, (n + 1) * chunk)
            ua = _gelu(a_ref[rows, 0:d])
            vg = _gelu(a_ref[rows, d:2 * d])
            za = a_ref[rows, 2 * d:3 * d]
            rv = lax.rsqrt(jnp.mean(vg * vg, axis=-1, keepdims=True) + EPS)
            va = (vg * rv * gv_ref[...]).astype(BF16)
            gate = ua * (za * _sigmoid(za))
            for g in range(ng):
                sl = slice(g * cw, (g + 1) * cw)
                mixed = jnp.dot(wm_ref[g], va[:, sl], preferred_element_type=F32) + bt_ref[:, g:g + 1]
                ya_ref[rows, sl] = (gate[:, sl] * mixed).astype(BF16)

    tile = per_step * chunk
    return pl.pallas_call(
        body, name="branch_a_fwd",
        grid=(t // tile,),
        in_specs=[pl.BlockSpec((tile, d3), lambda i: (i, 0)), _const((1, d)), _const(wm.shape), _const(b_t.shape)],
        out_specs=pl.BlockSpec((tile, d), lambda i: (i, 0)),
        out_shape=jax.ShapeDtypeStruct((t, d), BF16),
        compiler_params=_params(("arbitrary",)),
    )(a_pre, g_v, wm, b_t)


def _branch_a_bwd(a_pre, dya, g_v, wm, wm_t, b_t):
    t = a_pre.shape[0]
    d = g_v.shape[1]
    d3 = 3 * d
    ng, chunk, _ = wm.shape
    cw = d // ng
    nsteps = t // chunk

    def body(a_ref, dya_ref, gv_ref, wm_ref, wmt_ref, bt_ref, da_ref, gws_ref, gbt_ref, gnv_ref, db_acc):
        i = pl.program_id(0)

        @pl.when(i == 0)
        def _():
            gws_ref[...] = jnp.zeros_like(gws_ref)
            gnv_ref[...] = jnp.zeros_like(gnv_ref)
            db_acc[...] = jnp.zeros_like(db_acc)

        ua, dgelu_u = _gelu_and_grad(a_ref[:, 0:d])
        vg, dgelu_v = _gelu_and_grad(a_ref[:, d:2 * d])
        za = a_ref[:, 2 * d:3 * d]
        sig = _sigmoid(za)
        sz = za * sig
        dsz = sig * (1.0 + za * (1.0 - sig))
        rv = lax.rsqrt(jnp.mean(vg * vg, axis=-1, keepdims=True) + EPS)
        nv = vg * rv
        gv = gv_ref[...]
        va = (nv * gv).astype(BF16)
        dya = dya_ref[...]
        dmix = dya * ua * sz
        db_acc[...] += dmix
        dmix_b = dmix.astype(BF16)
        t_gate = dya * sz
        t_z = dya * ua * dsz
        dva_parts = []
        for g in range(ng):
            sl = slice(g * cw, (g + 1) * cw)
            mixed = jnp.dot(wm_ref[g], va[:, sl], preferred_element_type=F32) + bt_ref[:, g:g + 1]
            da_ref[:, sl] = (t_gate[:, sl] * mixed * dgelu_u[:, sl]).astype(BF16)
            da_ref[:, 2 * d + g * cw:2 * d + (g + 1) * cw] = (t_z[:, sl] * mixed).astype(BF16)
            gws_ref[g] += lax.dot_general(dmix_b[:, sl], va[:, sl], NT, preferred_element_type=F32)
            dva_parts.append(jnp.dot(wmt_ref[g], dmix_b[:, sl], preferred_element_type=F32))
        dva = jnp.concatenate(dva_parts, axis=1)
        gnv_ref[...] += jnp.sum(dva * nv, axis=0, keepdims=True)
        dnv = dva * gv
        dvg = rv * (dnv - nv * jnp.mean(dnv * nv, axis=-1, keepdims=True))
        da_ref[:, d:2 * d] = (dvg * dgelu_v).astype(BF16)

        @pl.when(i == nsteps - 1)
        def _():
            acc = db_acc[...]
            for g in range(ng):
                gbt_ref[:, g:g + 1] = jnp.sum(acc[:, g * cw:(g + 1) * cw], axis=1, keepdims=True)

    return pl.pallas_call(
        body, name="branch_a_bwd",
        grid=(nsteps,),
        in_specs=[pl.BlockSpec((chunk, d3), lambda i: (i, 0)), pl.BlockSpec((chunk, d), lambda i: (i, 0)),
                  _const((1, d)), _const(wm.shape), _const(wm_t.shape), _const(b_t.shape)],
        out_specs=[pl.BlockSpec((chunk, d3), lambda i: (i, 0)), _const(wm.shape), _const(b_t.shape), _const((1, d))],
        out_shape=[jax.ShapeDtypeStruct((t, d3), BF16), jax.ShapeDtypeStruct(wm.shape, F32),
                   jax.ShapeDtypeStruct(b_t.shape, F32), jax.ShapeDtypeStruct((1, d), F32)],
        scratch_shapes=[pltpu.VMEM((chunk, d), F32)],
        compiler_params=_params(("arbitrary",)),
    )(a_pre, dya, g_v, wm, wm_t, b_t)


def _below_diagonal(n):
    return lax.broadcasted_iota(jnp.int32, (n, n), 0) > lax.broadcasted_iota(jnp.int32, (n, n), 1)


def _twice(tri):
    t = tri.astype(BF16)
    return jnp.concatenate([t, t], axis=0)


def _cumsum_mm(a, tri2):
    hi, lo = _split_bf16(a)
    return jnp.dot(jnp.concatenate([hi, lo], axis=1), tri2, preferred_element_type=F32)


LOG2E = 1.4426950408889634
_SIGN = 0x80000000


def _sb_block(q, k, scale, upper2, causal):
    z2 = lax.dot_general(q, k, NT, preferred_element_type=F32) * (scale * LOG2E)
    neg_abs = lax.bitcast_convert_type(lax.bitcast_convert_type(z2, jnp.uint32) | jnp.uint32(_SIGN), F32)
    l2 = jnp.log(1.0 + jnp.exp2(neg_abs)) * LOG2E
    log_beta = jnp.minimum(z2, 0.0) - l2
    lom = log_beta - z2
    if causal is not None:
        lom = jnp.where(causal, lom, 0.0)
    sx = _cumsum_mm(lom, upper2)
    return log_beta, sx, sx[:, 0:1] + lom[:, 0:1]


DEAD_LOG2 = -150.0


def _max_carry(carries):
    return jnp.max(functools.reduce(jnp.maximum, carries))


ZB_GROUP, GA_GROUP, GB_GROUP = 6, 7, 8


def _attn_specs(d, seq, nq, heads_per_step):
    hp_w = heads_per_step * (d // HEADS)
    n_hp = d // hp_w
    row_blk = lambda group: pl.BlockSpec((ATT_T, hp_w), lambda b, h, i: (b * nq + i, group * n_hp + h))
    seq_blk = lambda group: pl.BlockSpec((seq, hp_w), lambda b, h, i: (b, group * n_hp + h))
    return row_blk, seq_blk, n_hp


def _attn_fwd(qkv, proj, bsz, seq):
    t, d3 = qkv.shape
    d = d3 // 3
    hd = d // HEADS
    nq = seq // ATT_T
    scale = hd ** -0.5
    n_heads = ATT_HP
    row_blk, seq_blk, n_hp = _attn_specs(d, seq, nq, n_heads)

    def body(q_ref, k_ref, v_ref, zb_ref, o_ref, yb_ref):
        i = pl.program_id(2)
        causal = _below_diagonal(ATT_T)
        upper2 = _twice(causal)

        def step(kb, state, mask):
            rows = pl.ds(pl.multiple_of(kb * ATT_T, ATT_T), ATT_T)
            heads = [slice(h * hd, (h + 1) * hd) for h in range(n_heads)]
            scores = [_sb_block(q_ref[:, cols], k_ref[rows, cols], scale, upper2, mask) for cols in heads]
            new = []
            for cols, (carry, acc), (log_beta, sx, total) in zip(heads, state, scores):
                a = jnp.exp2(log_beta + sx + carry)
                if mask is not None:
                    a = jnp.where(mask, a, 0.0)
                acc = acc + jnp.dot(a.astype(BF16), v_ref[rows, cols], preferred_element_type=F32)
                new.append((carry + total, acc))
            return tuple(new)

        init = tuple((jnp.zeros((ATT_T, 1), F32), jnp.zeros((ATT_T, hd), F32)) for _ in range(n_heads))
        state = step(i, init, causal)
        def more(c):
            new = step(c[0], c[1], None)
            return c[0] - 1, new, _max_carry([s[0] for s in new])

        _, state, _ = lax.while_loop(lambda c: (c[0] >= 0) & (c[2] > DEAD_LOG2), more,
                                     (i - 1, state, _max_carry([s[0] for s in state])))
        for h in range(n_heads):
            cols = slice(h * hd, (h + 1) * hd)
            acc = state[h][1]
            zb = zb_ref[:, cols]
            o_ref[:, cols] = acc
            yb_ref[:, cols] = (acc * (zb * _sigmoid(zb))).astype(BF16)

    return pl.pallas_call(
        body, name="attn_fwd",
        grid=(bsz, n_hp, nq),
        in_specs=[row_blk(0), seq_blk(1), seq_blk(2), row_blk(ZB_GROUP)],
        out_specs=[row_blk(0), row_blk(0)],
        out_shape=[jax.ShapeDtypeStruct((t, d), F32), jax.ShapeDtypeStruct((t, d), BF16)],
        compiler_params=_params(("arbitrary", "arbitrary", "arbitrary")),
    )(qkv, qkv, qkv, proj)


def _attn_bwd(qkv, proj, o, dyb, bsz, seq):
    t, d3 = qkv.shape
    d = d3 // 3
    hd = d // HEADS
    nq = seq // ATT_T
    scale = hd ** -0.5
    row_blk, seq_blk, n_hp = _attn_specs(d, seq, nq, ATT_HP)

    def body(q_ref, k_ref, v_ref, zb_ref, o_ref, dyb_ref, dq_ref, dk_ref, dv_ref, dzb_ref,
             g_s, beta_s, dkt_acc, dvt_acc):
        i = pl.program_id(2)

        @pl.when(i == 0)
        def _():
            dkt_acc[...] = jnp.zeros_like(dkt_acc)
            dvt_acc[...] = jnp.zeros_like(dvt_acc)

        causal = _below_diagonal(ATT_T)
        upper2 = _twice(causal)
        lower2 = _twice(~causal)
        zb = zb_ref[...]
        sig = _sigmoid(zb)
        dyb_t = dyb_ref[...]
        do_f = dyb_t * (zb * sig)
        do = do_f.astype(BF16)
        do_t = do_f.T.astype(BF16)
        q_t = q_ref[...].astype(F32).T.astype(BF16)
        dzb_ref[...] = (dyb_t * o_ref[...] * (sig * (1.0 + zb * (1.0 - sig)))).astype(BF16)

        def sweep(kb, carries, mask):
            rows = pl.ds(pl.multiple_of(kb * ATT_T, ATT_T), ATT_T)
            heads = [slice(h * hd, (h + 1) * hd) for h in range(ATT_HP)]
            scores = [_sb_block(q_ref[:, cols], k_ref[rows, cols], scale, upper2, mask) for cols in heads]
            das = [lax.dot_general(do[:, cols], v_ref[rows, cols], NT, preferred_element_type=F32) for cols in heads]
            new = []
            for h, (cols, carry, (log_beta, sx, total), da) in enumerate(zip(heads, carries, scores, das)):
                a = jnp.exp2(log_beta + sx + carry)
                beta = jnp.exp2(log_beta)
                if mask is not None:
                    a = jnp.where(mask, a, 0.0)
                    beta = jnp.where(mask, beta, 0.0)
                g_s[h, kb] = a * da
                beta_s[h, kb] = beta
                dvt_acc[kb, cols, :] += jnp.dot(do_t[cols, :], a.astype(BF16), preferred_element_type=F32)
                new.append(carry + total)
            return tuple(new)

        carries = sweep(i, tuple(jnp.zeros((ATT_T, 1), F32) for _ in range(ATT_HP)), causal)

        def more(c):
            new = sweep(c[0], c[1], None)
            return c[0] - 1, new, _max_carry(new)

        last, _, _ = lax.while_loop(lambda c: (c[0] >= 0) & (c[2] > DEAD_LOG2), more, (i - 1, carries, _max_carry(carries)))
        first_kb = last + 1

        def back(kb, state):
            rows = pl.ds(pl.multiple_of(kb * ATT_T, ATT_T), ATT_T)
            heads = [slice(h * hd, (h + 1) * hd) for h in range(ATT_HP)]
            sums = [_cumsum_mm(g_s[h, kb], lower2) for h in range(ATT_HP)]
            new = []
            for h, (cols, (p_carry, dq), px) in enumerate(zip(heads, state, sums)):
                dz = ((g_s[h, kb] - (p_carry + px) * beta_s[h, kb]) * scale).astype(BF16)
                dq = dq + jnp.dot(dz, k_ref[rows, cols], preferred_element_type=F32)
                dkt_acc[kb, cols, :] += jnp.dot(q_t[cols, :], dz, preferred_element_type=F32)
                new.append((p_carry + px[:, ATT_T - 1:ATT_T], dq))
            return tuple(new)

        init = tuple((jnp.zeros((ATT_T, 1), F32), jnp.zeros((ATT_T, hd), F32)) for _ in range(ATT_HP))
        state = lax.fori_loop(first_kb, i + 1, back, init)
        for h in range(ATT_HP):
            dq_ref[:, h * hd:(h + 1) * hd] = state[h][1].astype(BF16)

        @pl.when(i == nq - 1)
        def _():
            for kb in range(nq):
                dk_ref[kb * ATT_T:(kb + 1) * ATT_T, :] = dkt_acc[kb].T.astype(BF16)
                dv_ref[kb * ATT_T:(kb + 1) * ATT_T, :] = dvt_acc[kb].T.astype(BF16)

    out = jax.ShapeDtypeStruct((t, d), BF16)
    hp_w = ATT_HP * hd
    return pl.pallas_call(
        body, name="attn_bwd",
        grid=(bsz, n_hp, nq),
        in_specs=[row_blk(0), seq_blk(1), seq_blk(2), row_blk(ZB_GROUP), row_blk(0), row_blk(0)],
        out_specs=[row_blk(0), seq_blk(0), seq_blk(0), row_blk(0)],
        out_shape=[out, out, out, out],
        scratch_shapes=[pltpu.VMEM((ATT_HP, nq, ATT_T, ATT_T), F32), pltpu.VMEM((ATT_HP, nq, ATT_T, ATT_T), F32),
                        pltpu.VMEM((nq, hp_w, ATT_T), F32), pltpu.VMEM((nq, hp_w, ATT_T), F32)],
        compiler_params=_params(("arbitrary", "arbitrary", "arbitrary")),
    )(qkv, qkv, qkv, proj, o, dyb)


def _out_proj(ya, yb, g_pre, x2d, tgt, w_og, w_osb, w_out, g_f):
    t, d = x2d.shape

    def body(ya_ref, yb_ref, ga_ref, gb_ref, x_ref, tgt_ref, wog_ref, wosb_ref, wout_ref, gf_ref,
             dya_ref, dyb_ref, dg_ref, dx2_ref, loss_ref, gnf_ref, gwog_ref, gwosb_ref, gwout_ref):
        @pl.when(pl.program_id(0) == 0)
        def _():
            loss_ref[...] = jnp.zeros_like(loss_ref)
            gnf_ref[...] = jnp.zeros_like(gnf_ref)
            gwog_ref[...] = jnp.zeros_like(gwog_ref)
            gwosb_ref[...] = jnp.zeros_like(gwosb_ref)
            gwout_ref[...] = jnp.zeros_like(gwout_ref)

        ya = ya_ref[...]
        yb = yb_ref[...]
        pa = jnp.dot(ya, wog_ref[...], preferred_element_type=F32)
        pb = jnp.dot(yb, wosb_ref[...], preferred_element_type=F32)
        sga = _sigmoid(ga_ref[...])
        sgb = _sigmoid(gb_ref[...])
        merged = (sga * pa + sgb * pb).astype(BF16)
        x2 = x_ref[...] + jnp.dot(merged, wout_ref[...], preferred_element_type=F32)
        r2 = lax.rsqrt(jnp.mean(x2 * x2, axis=-1, keepdims=True) + EPS)
        n2 = x2 * r2
        gf = gf_ref[...]
        err = n2 * gf - tgt_ref[...]
        loss_ref[...] += 0.5 * jnp.sum(jnp.sum(err * err, axis=-1, keepdims=True), axis=0, keepdims=True) / d
        dy = err * (1.0 / d)
        gnf_ref[...] += jnp.sum(dy * n2, axis=0, keepdims=True)
        dn = dy * gf
        dx2 = r2 * (dn - n2 * jnp.mean(dn * n2, axis=-1, keepdims=True))
        dx2_ref[...] = dx2
        dx2_b = dx2.astype(BF16)
        dmerged = lax.dot_general(dx2_b, wout_ref[...], NT, preferred_element_type=F32)
        gwout_ref[...] += lax.dot_general(merged, dx2_b, TN, preferred_element_type=F32)
        dg_ref[:, 0:d] = (dmerged * pa * (sga * (1.0 - sga))).astype(BF16)
        dg_ref[:, d:2 * d] = (dmerged * pb * (sgb * (1.0 - sgb))).astype(BF16)
        dpa = (dmerged * sga).astype(BF16)
        dpb = (dmerged * sgb).astype(BF16)
        dya_ref[...] = lax.dot_general(dpa, wog_ref[...], NT, preferred_element_type=F32)
        dyb_ref[...] = lax.dot_general(dpb, wosb_ref[...], NT, preferred_element_type=F32)
        gwog_ref[...] += lax.dot_general(ya, dpa, TN, preferred_element_type=F32)
        gwosb_ref[...] += lax.dot_general(yb, dpb, TN, preferred_element_type=F32)

    row = lambda i: (i, 0)
    return pl.pallas_call(
        body, name="out_proj",
        grid=(t // TM,),
        in_specs=[pl.BlockSpec((TM, d), row), pl.BlockSpec((TM, d), row),
                  pl.BlockSpec((TM, d), lambda i: (i, GA_GROUP)), pl.BlockSpec((TM, d), lambda i: (i, GB_GROUP)),
                  pl.BlockSpec((TM, d), row), pl.BlockSpec((TM, d), row),
                  _resident((d, d)), _resident((d, d)), _resident((d, d)), _const((1, d))],
        out_specs=[pl.BlockSpec((TM, d), row), pl.BlockSpec((TM, d), row), pl.BlockSpec((TM, 2 * d), row),
                   pl.BlockSpec((TM, d), row), _const((1, 1)), _const((1, d)),
                   _const((d, d)), _const((d, d)), _const((d, d))],
        out_shape=[jax.ShapeDtypeStruct((t, d), F32), jax.ShapeDtypeStruct((t, d), F32),
                   jax.ShapeDtypeStruct((t, 2 * d), BF16), jax.ShapeDtypeStruct((t, d), F32),
                   jax.ShapeDtypeStruct((1, 1), F32), jax.ShapeDtypeStruct((1, d), F32),
                   jax.ShapeDtypeStruct((d, d), F32), jax.ShapeDtypeStruct((d, d), F32),
                   jax.ShapeDtypeStruct((d, d), F32)],
        compiler_params=_params(("arbitrary",)),
    )(ya, yb, g_pre, g_pre, x2d, tgt, w_og, w_osb, w_out, g_f)


def _dproj_pieces(d):
    return [(0, 0, 3), (1, 3, 1), (2, 4, 1), (3, 5, 1), (4, 6, 1), (5, 7, 2)]


def _in_proj_bwd_x(pieces, wg, x2d, g_in, dx2, gw16):
    t, d = x2d.shape
    ncol = wg.shape[2]
    segs = _segments(d, ncol)
    layout = _dproj_pieces(d)
    nsteps = t // TMX

    def body(da_ref, dq_ref, dk_ref, dv_ref, dzb_ref, dg_ref, w_ref, x_ref, g_ref, dx2_ref, gw16_ref,
             gx_ref, gn_ref, recv_ref, send_sems, recv_sems):
        x_pos, y_pos, c_pos, chips = _place()

        def share(k, chunk):
            px, py = chips[k]
            return pltpu.make_async_remote_copy(
                src_ref=gw16_ref.at[:, chunk * ncol:(chunk + 1) * ncol], dst_ref=recv_ref.at[k],
                send_sem=send_sems.at[k], recv_sem=recv_sems.at[k], device_id=(px, py, c_pos), device_id_type=MESH)

        @pl.when(pl.program_id(0) == 0)
        def _():
            gn_ref[...] = jnp.zeros_like(gn_ref)
            for k, (px, py) in enumerate(chips):
                for chunk in range(N_CHIP):
                    @pl.when(2 * px + py == chunk)
                    def _(k=k, chunk=chunk):
                        share(k, chunk).start()

        @pl.when(pl.program_id(0) == nsteps - 1)
        def _():
            for k in range(N_CHIP - 1):
                share(k, 0).wait()

        refs = (da_ref, dq_ref, dk_ref, dv_ref, dzb_ref, dg_ref)
        dh = jnp.zeros((TMX, d), F32)
        for chip, c0, grp, s0, width in segs:
            piece, first, _ = next(p for p in layout if p[1] <= grp < p[1] + p[2])
            off = (grp - first) * d + s0
            dh = dh + lax.dot_general(refs[piece][:, off:off + width], w_ref[chip, :, c0:c0 + width], NT,
                                      preferred_element_type=F32)
        x = x_ref[...]
        r = lax.rsqrt(jnp.mean(x * x, axis=-1, keepdims=True) + EPS)
        n = x * r
        gn_ref[...] += jnp.sum(dh * n, axis=0, keepdims=True)
        dn = dh * g_ref[...]
        gx_ref[...] = dx2_ref[...] + r * (dn - n * jnp.mean(dn * n, axis=-1, keepdims=True))

    row = lambda i: (i, 0)
    return pl.pallas_call(
        body, name="in_proj_bwd_x",
        grid=(nsteps,),
        in_specs=[pl.BlockSpec((TMX, p.shape[1]), row) for p in pieces]
        + [_resident(wg.shape), pl.BlockSpec((TMX, d), row), _const((1, d)), pl.BlockSpec((TMX, d), row), ANY],
        out_specs=[pl.BlockSpec((TMX, d), row), _const((1, d)), ANY],
        out_shape=[jax.ShapeDtypeStruct((t, d), F32), jax.ShapeDtypeStruct((1, d), F32),
                   jax.ShapeDtypeStruct((N_CHIP - 1, gw16.shape[0], ncol), BF16)],
        scratch_shapes=[pltpu.SemaphoreType.DMA((N_CHIP - 1,)), pltpu.SemaphoreType.DMA((N_CHIP - 1,))],
        compiler_params=_params(("arbitrary",)),
    )(*pieces, wg, x2d, g_in, dx2, gw16)


def _in_proj_bwd_w(h_t, pieces, mats16, pack):
    d, t = h_t.shape
    nk = t // TKW
    half = d // 2
    layout = _dproj_pieces(d)
    n_mats = len(mats16)
    n_dev = 2 * N_CHIP
    flips = [(dx, dy, dc) for dx in (0, 1) for dy in (0, 1) for dc in (0, 1)][1:]

    def body(ht_ref, da_ref, dq_ref, dk_ref, dv_ref, dzb_ref, dg_ref, *rest):
        mat_refs, pack_ref = rest[:n_mats], rest[n_mats]
        gw_ref, sib_ref = rest[n_mats + 1:n_mats + 3]
        recv_refs, slots_ref = rest[n_mats + 3:2 * n_mats + 3], rest[2 * n_mats + 3]
        acc, stage, mat_send, mat_recv, pack_send, pack_recv, own_sem, stage_send, sib_recv = rest[2 * n_mats + 4:]
        s = pl.program_id(0)
        i = pl.program_id(1)
        x_pos, y_pos, c_pos, chips = _place()
        me = 4 * x_pos + 2 * y_pos + c_pos

        def to_sibling(slot, group):
            return pltpu.make_async_remote_copy(
                src_ref=stage.at[slot], dst_ref=sib_ref.at[:, group * d:(group + 1) * d],
                send_sem=stage_send.at[slot], recv_sem=sib_recv, device_id=(x_pos, y_pos, 1 - c_pos), device_id_type=MESH)

        def exchanges():
            cps = []
            for k, (px, py) in enumerate(chips):
                for a in range(n_mats):
                    cps.append(pltpu.make_async_remote_copy(
                        src_ref=mat_refs[a].at[2 * px + py], dst_ref=recv_refs[a].at[k],
                        send_sem=mat_send.at[a, k], recv_sem=mat_recv.at[a, k],
                        device_id=(px, py, c_pos), device_id_type=MESH))
            for k, (dx, dy, dc) in enumerate(flips):
                peer = (1 - x_pos if dx else x_pos, 1 - y_pos if dy else y_pos, 1 - c_pos if dc else c_pos)
                cps.append(pltpu.make_async_remote_copy(
                    src_ref=pack_ref, dst_ref=slots_ref.at[me], send_sem=pack_send.at[k], recv_sem=pack_recv.at[k],
                    device_id=peer, device_id_type=MESH))
            return cps, pltpu.make_async_copy(pack_ref, slots_ref.at[me], own_sem)

        @pl.when((s == 0) & (i == 0))
        def _():
            cps, own = exchanges()
            own.start()
            for cp in cps:
                cp.start()

        @pl.when(i == 0)
        def _():
            acc[...] = jnp.zeros_like(acc)

        refs = (da_ref, dq_ref, dk_ref, dv_ref, dzb_ref, dg_ref)
        for piece, first, count in layout:
            @pl.when((s >= first) & (s < first + count))
            def _(piece=piece):
                acc[...] += jnp.dot(ht_ref[...], refs[piece][...], preferred_element_type=F32)

        @pl.when(i == nk - 1)
        def _():
            gw_ref[...] = acc[...]
            for slot in range(2):
                @pl.when(s % 2 == slot)
                def _(slot=slot):
                    @pl.when(s >= 2)
                    def _():
                        to_sibling(slot, 0).wait_send()
                    for other in range(2):
                        @pl.when(c_pos == 1 - other)
                        def _(other=other):
                            stage[slot] = acc[other * half:(other + 1) * half, :].astype(BF16)
                    for group in range(N_SPLIT):
                        @pl.when(s == group)
                        def _(group=group):
                            to_sibling(slot, group).start()

        @pl.when((s == N_SPLIT - 1) & (i == nk - 1))
        def _():
            for slot in range(2):
                to_sibling(slot, 0).wait_send()
            pltpu.make_async_remote_copy(
                src_ref=sib_ref, dst_ref=sib_ref, send_sem=stage_send.at[0], recv_sem=sib_recv,
                device_id=(x_pos, y_pos, c_pos), device_id_type=MESH).wait_recv()
            cps, own = exchanges()
            own.wait()
            for cp in cps:
                cp.wait()

    def piece_spec(p, first, count):
        def index(s, i):
            mine = (s >= first) & (s < first + count)
            return jnp.where(mine, i, 0), jnp.where(mine, s - first, 0)
        return pl.BlockSpec((TKW, d), index)

    col_blk = pl.BlockSpec((d, d), lambda s, i: (0, s))
    outs = pl.pallas_call(
        body, name="in_proj_bwd_w",
        grid=(N_SPLIT, nk),
        in_specs=[pl.BlockSpec((d, TKW), lambda s, i: (0, i))] + [piece_spec(*p) for p in layout] + [ANY] * (n_mats + 1),
        out_specs=[col_blk, ANY] + [ANY] * (n_mats + 1),
        out_shape=[jax.ShapeDtypeStruct((d, N_SPLIT * d), F32), jax.ShapeDtypeStruct((half, N_SPLIT * d), BF16)]
        + [jax.ShapeDtypeStruct((N_CHIP - 1,) + m.shape[1:], BF16) for m in mats16]
        + [jax.ShapeDtypeStruct((n_dev,) + pack.shape, F32)],
        scratch_shapes=[pltpu.VMEM((d, d), F32), pltpu.VMEM((2, half, d), BF16),
                        pltpu.SemaphoreType.DMA((n_mats, N_CHIP - 1)), pltpu.SemaphoreType.DMA((n_mats, N_CHIP - 1)),
                        pltpu.SemaphoreType.DMA((n_dev - 1,)), pltpu.SemaphoreType.DMA((n_dev - 1,)),
                        pltpu.SemaphoreType.DMA, pltpu.SemaphoreType.DMA((2,)), pltpu.SemaphoreType.DMA],
        compiler_params=_params(("arbitrary", "arbitrary")),
    )(h_t, *pieces, *mats16, pack)
    return outs[0], outs[1], outs[2:2 + n_mats], outs[2 + n_mats]


def _local_step(proj, qkv, x2d, tgt2d, bsz, seq, norm_v, w_s, b_s, w_og, w_osb, w_out, norm_final):
    d = x2d.shape[1]
    chunk = w_s.shape[-1]
    causal = jnp.tril(jnp.ones((chunk, chunk), dtype=bool))
    wm = jnp.where(causal[None], w_s, 0.0).astype(BF16)
    wm_t = jnp.swapaxes(wm, 1, 2)
    b_t = b_s.T

    ya = _branch_a_fwd(proj, norm_v, wm, b_t)
    o, yb = _attn_fwd(qkv, proj, bsz, seq)
    dya, dyb, dg, dx2, loss, g_nf, g_wog, g_wosb, g_wout = _out_proj(
        ya, yb, proj, x2d, tgt2d, w_og, w_osb, w_out, norm_final.reshape(1, d))
    dq, dk, dv, dzb = _attn_bwd(qkv, proj, o, dyb, bsz, seq)
    d_a, g_ws, g_bt, g_nv = _branch_a_bwd(proj, dya, norm_v, wm, wm_t, b_t)
    g_ws = jnp.where(causal[None], g_ws, 0.0)
    return loss, (d_a, dq, dk, dv, dzb, dg), dx2, g_nv, g_ws, g_bt.T, g_wog, g_wosb, g_wout, g_nf


def _row_tile(rows):
    return next(r for r in (128, 64, 32, 16, 8) if rows % r == 0)


def _cast_bf16(arrs):
    n = len(arrs)

    def body(*refs):
        for a_ref, o_ref in zip(refs[:n], refs[n:]):
            o_ref[...] = a_ref[...].astype(BF16)

    specs = [pl.BlockSpec((a.shape[0] // CAST_STEPS, a.shape[1]), lambda i: (i, 0)) for a in arrs]
    return pl.pallas_call(
        body, name="cast_bf16", grid=(CAST_STEPS,),
        in_specs=specs, out_specs=specs,
        out_shape=[jax.ShapeDtypeStruct(a.shape, BF16) for a in arrs],
        compiler_params=_params(("arbitrary",)),
    )(*arrs)


def _chip_half(full, from_sibling, core, chip, tile):
    half, n = from_sibling.shape

    def body(where_ref, own_ref, sib_ref, o32_ref, o16_ref):
        total = own_ref[...] + sib_ref[...].astype(F32)
        o16_ref[...] = total.astype(BF16)

        @pl.when(pl.program_id(0) == where_ref[1])
        def _():
            o32_ref[...] = total

    blk = pl.BlockSpec((half, tile), lambda j, where_ref: (0, j))
    return pl.pallas_call(
        body, name="chip_half",
        grid_spec=pltpu.PrefetchScalarGridSpec(
            num_scalar_prefetch=1, grid=(n // tile,),
            in_specs=[pl.BlockSpec((half, tile), lambda j, where_ref: (where_ref[0], j)), blk],
            out_specs=[pl.BlockSpec((half, tile), lambda j, where_ref: (0, 0)), blk]),
        out_shape=[jax.ShapeDtypeStruct((half, tile), F32), jax.ShapeDtypeStruct((half, n), BF16)],
        compiler_params=_params(("arbitrary",)),
    )(jnp.stack([core, chip]).astype(jnp.int32), full, from_sibling)


def _add_received(fulls, recvs, chip, by_cols):
    n = len(fulls)
    _, rows, cols = recvs[0].shape
    tr = _row_tile(rows)
    nb = rows // tr

    def body(chip_ref, *refs):
        for own_ref, recv_ref, o_ref in zip(refs[:n], refs[n:2 * n], refs[2 * n:]):
            s = own_ref[...]
            for k in range(N_CHIP - 1):
                s = s + recv_ref[k].astype(F32)
            o_ref[...] = s

    own_map = (lambda i, chip_ref: (i, chip_ref[0])) if by_cols else (lambda i, chip_ref: (chip_ref[0] * nb + i, 0))
    return pl.pallas_call(
        body, name="add_received",
        grid_spec=pltpu.PrefetchScalarGridSpec(
            num_scalar_prefetch=1, grid=(nb,),
            in_specs=[pl.BlockSpec((tr, cols), own_map)] * n
            + [pl.BlockSpec((N_CHIP - 1, tr, cols), lambda i, chip_ref: (0, i, 0))] * n,
            out_specs=[pl.BlockSpec((tr, cols), lambda i, chip_ref: (i, 0))] * n),
        out_shape=[jax.ShapeDtypeStruct((rows, cols), F32)] * n,
        compiler_params=_params(("arbitrary",)),
    )(chip.reshape(1).astype(jnp.int32), *fulls, *recvs)


def _adamw_math(w, m, v, g):
    new_m = ADAM_B1 * m + (1.0 - ADAM_B1) * g
    new_v = ADAM_B2 * v + (1.0 - ADAM_B2) * (g * g)
    m_hat = new_m / (1.0 - ADAM_B1 ** ADAM_STEP)
    v_hat = new_v / (1.0 - ADAM_B2 ** ADAM_STEP)
    return -ADAM_LR * (m_hat / (jnp.sqrt(v_hat) + ADAM_EPS) + ADAM_WD * w), new_m, new_v


def _adamw_pairs(ws, ms, vs, mines, theirs):
    n = len(ws)
    rows, cols = ws[0].shape
    tr = _row_tile(rows)

    def body(*refs):
        ins, outs = refs[:5 * n], refs[5 * n:]
        for a in range(n):
            w_ref, m_ref, v_ref, mine_ref, theirs_ref = ins[a::n]
            g_ref, d_ref, nm_ref, nv_ref = outs[4 * a:4 * a + 4]
            g = mine_ref[...] + theirs_ref[...]
            g_ref[...] = g
            d_ref[...], nm_ref[...], nv_ref[...] = _adamw_math(w_ref[...], m_ref[...], v_ref[...], g)

    spec = pl.BlockSpec((tr, cols), lambda i: (i, 0))
    out = jax.ShapeDtypeStruct((rows, cols), F32)
    outs = pl.pallas_call(
        body, name="adamw_pairs", grid=(rows // tr,),
        in_specs=[spec] * (5 * n), out_specs=[spec] * (4 * n), out_shape=[out] * (4 * n),
        compiler_params=_params(("arbitrary",)),
    )(*ws, *ms, *vs, *mines, *theirs)
    return [outs[4 * a:4 * a + 4] for a in range(n)]


def _adamw_halves(w, m, v, mine, theirs, core):
    rows, cols = w.shape
    tr = _row_tile(rows // 2)
    per_half = rows // 2 // tr

    def body(core_ref, w_ref, m_ref, v_ref, mine_ref, theirs_ref, g_ref, d_ref, nm_ref, nv_ref):
        is_mine = pl.program_id(0) // per_half == core_ref[0]
        for part, cond in ((mine_ref, is_mine), (theirs_ref, jnp.logical_not(is_mine))):
            @pl.when(cond)
            def _(part=part):
                g = part[...]
                g_ref[...] = g
                d_ref[...], nm_ref[...], nv_ref[...] = _adamw_math(w_ref[...], m_ref[...], v_ref[...], g)

    spec = pl.BlockSpec((tr, cols), lambda i, core_ref: (i, 0))

    def half_spec(own):
        def index(i, core_ref):
            in_core_half = i // per_half == core_ref[0]
            here = in_core_half if own else jnp.logical_not(in_core_half)
            return jnp.where(here, i % per_half, 0), 0
        return pl.BlockSpec((tr, cols), index)

    out = jax.ShapeDtypeStruct(w.shape, F32)
    return pl.pallas_call(
        body, name="adamw_halves",
        grid_spec=pltpu.PrefetchScalarGridSpec(
            num_scalar_prefetch=1, grid=(rows // tr,),
            in_specs=[spec] * 3 + [half_spec(True), half_spec(False)], out_specs=[spec] * 4),
        out_shape=[out] * 4,
        compiler_params=_params(("arbitrary",)),
    )(core.reshape(1).astype(jnp.int32), w, m, v, mine, theirs)


def _adamw_small(ws, ms, vs, slots_head, slots_tail):
    n_dev = slots_head.shape[0]
    n_par = len(ws)
    groups, chunk, _ = ws[4].shape

    def body(*refs):
        w_refs, m_refs, v_refs = refs[:n_par], refs[n_par:2 * n_par], refs[2 * n_par:3 * n_par]
        head_ref, tail_ref = refs[3 * n_par:3 * n_par + 2]
        out_refs, scalar_ref = refs[3 * n_par + 2:-1], refs[-1]

        def total(ref, rows, cols):
            g = ref[0, rows, cols]
            for i in range(1, n_dev):
                g = g + ref[i, rows, cols]
            return g

        def update(p, g, at=...):
            g_ref, d_ref, nm_ref, nv_ref = out_refs[4 * p:4 * p + 4]
            g_ref[at] = g
            d_ref[at], nm_ref[at], nv_ref[at] = _adamw_math(w_refs[p][at], m_refs[p][at], v_refs[p][at], g)

        every = slice(None)
        update(0, total(head_ref, slice(0, 1), every))
        update(1, total(tail_ref, slice(0, 1), every))
        update(2, total(tail_ref, slice(SLAB, SLAB + groups), slice(0, chunk)))
        update(3, total(tail_ref, slice(2 * SLAB, 2 * SLAB + 1), every))
        scalar_ref[...] = total(tail_ref, slice(3 * SLAB, 3 * SLAB + 1), slice(0, chunk))[:, 0:1]
        for grp in range(groups):
            update(4, total(tail_ref, slice(4 * SLAB, 4 * SLAB + chunk), slice(grp * chunk, (grp + 1) * chunk)), grp)

    vmem = pl.BlockSpec(memory_space=pltpu.VMEM)
    return pl.pallas_call(
        body, name="adamw_small", in_specs=[vmem] * (3 * n_par + 2), out_specs=[vmem] * (4 * n_par + 1),
        out_shape=[jax.ShapeDtypeStruct(w.shape, F32) for w in ws for _ in range(4)]
        + [jax.ShapeDtypeStruct((1, 1), F32)],
        compiler_params=pltpu.CompilerParams(vmem_limit_bytes=VMEM_LIMIT),
    )(*ws, *ms, *vs, slots_head, slots_tail)


ANY = pl.BlockSpec(memory_space=pl.ANY)


def _place():
    x, y, c = lax.axis_index("x"), lax.axis_index("y"), lax.axis_index("c")
    other_chips = [(1 - x, y), (x, 1 - y), (1 - x, 1 - y)]
    return x, y, c, other_chips


def _swap_and_gather(arrs, pack):
    n = len(arrs)
    n_dev = 2 * N_CHIP
    flips = [(dx, dy, dc) for dx in (0, 1) for dy in (0, 1) for dc in (0, 1)][1:]

    def body(*refs):
        ins, pack_ref = refs[:n], refs[n]
        outs, slots_ref = refs[n + 1:2 * n + 1], refs[2 * n + 1]
        send_sems, recv_sems, pack_send, pack_recv, own_sem = refs[2 * n + 2:]
        x, y, c, _ = _place()
        me = 4 * x + 2 * y + c
        own = pltpu.make_async_copy(pack_ref, slots_ref.at[me], own_sem)
        own.start()
        copies = []
        for k, (dx, dy, dc) in enumerate(flips):
            peer = (1 - x if dx else x, 1 - y if dy else y, 1 - c if dc else c)
            copies.append(pltpu.make_async_remote_copy(
                src_ref=pack_ref, dst_ref=slots_ref.at[me], send_sem=pack_send.at[k], recv_sem=pack_recv.at[k],
                device_id=peer, device_id_type=MESH))
        copies += [pltpu.make_async_remote_copy(
            src_ref=ins[a], dst_ref=outs[a], send_sem=send_sems.at[a], recv_sem=recv_sems.at[a],
            device_id=(x, y, 1 - c), device_id_type=MESH) for a in range(n)]
        for cp in copies:
            cp.start()
        for cp in copies:
            cp.wait()
        own.wait()

    return pl.pallas_call(
        body, name="swap_and_gather",
        in_specs=[ANY] * (n + 1), out_specs=[ANY] * (n + 1),
        out_shape=[jax.ShapeDtypeStruct(a.shape, a.dtype) for a in arrs] + [jax.ShapeDtypeStruct((n_dev,) + pack.shape, F32)],
        scratch_shapes=[pltpu.SemaphoreType.DMA((n,)), pltpu.SemaphoreType.DMA((n,)),
                        pltpu.SemaphoreType.DMA((n_dev - 1,)), pltpu.SemaphoreType.DMA((n_dev - 1,)),
                        pltpu.SemaphoreType.DMA],
    )(*arrs, pack)


SLAB = 8


def _slab(vec, d):
    return jnp.pad(vec.reshape(1, d), ((0, SLAB - 1), (0, 0)))


def _pack_tail(vec_nv, b_s, vec_nf, w_s, scalar):
    d = vec_nv.shape[-1]
    groups, chunk, _ = w_s.shape
    assert groups == SLAB and groups * chunk == d
    slabs = [_slab(vec_nv, d), jnp.pad(b_s, ((0, 0), (0, d - chunk))), _slab(vec_nf, d),
             jnp.pad(scalar, ((0, SLAB - 1), (0, d - 1)))]
    return jnp.concatenate(slabs + [jnp.swapaxes(w_s, 0, 1).reshape(chunk, d)], axis=0)


def kernel(x, norm_in, w_in, norm_v, w_s, b_s, w_o_gmlp, w_o_sb, w_out, norm_final, loss_target, m_norm_in, m_w_in, m_norm_v, m_w_s, m_b_s, m_w_o_gmlp, m_w_o_sb, m_w_out, m_norm_final, v_norm_in, v_w_in, v_norm_v, v_w_s, v_b_s, v_w_o_gmlp, v_w_o_sb, v_w_out, v_norm_final):
    d = x.shape[-1]
    ncol = w_in.shape[-1]
    nrow = w_o_gmlp.shape[-2]
    chip = 2 * lax.axis_index("x") + lax.axis_index("y")

    bsz, seq, _ = x.shape
    x2d = x.reshape(bsz * seq, d)
    shards = [w_in[0], w_o_gmlp[0], w_o_sb[0], w_out[0]]
    halves = [s16.reshape(2, s16.shape[0] // 2, s16.shape[1]) for s16 in _cast_bf16(shards)]
    proj, qkv, h_t, (wg, w_og, w_osb, w_o) = _in_proj_fwd(x2d, norm_in, halves)
    wg = wg.reshape(N_CHIP, d, ncol)

    loss, pieces, dx2, g_nv, g_ws, g_bs, g_wog, g_wosb, g_wout, g_nf = _local_step(
        proj, qkv, x2d, loss_target.reshape(bsz * seq, d), bsz, seq, norm_v, w_s[0], b_s[0],
        w_og.reshape(d, d), w_osb.reshape(d, d), w_o.reshape(d, d), norm_final)

    mats = [g_wog, g_wosb, g_wout]
    mats16 = [g16.reshape(N_CHIP, nrow, d) for g16 in _cast_bf16(mats)]
    g_win, from_sibling, recv_mats, slots_tail = _in_proj_bwd_w(
        h_t, pieces, mats16, _pack_tail(g_nv, g_bs, g_nf, g_ws, loss))
    core = lax.axis_index("c")
    half_own, half_win16 = _chip_half(g_win, from_sibling, core, chip, ncol)
    grad_x, g_nin, recv_win = _in_proj_bwd_x(pieces, wg, x2d, norm_in, dx2, half_win16)
    grad_x = grad_x.reshape(bsz, seq, d)

    sums = _add_received([half_own], [recv_win], jnp.zeros((), jnp.int32), True) + _add_received(
        mats, recv_mats, chip, False)
    *sibling_sums, slots_head = _swap_and_gather(sums, _slab(g_nin, d))
    stats = [_adamw_halves(w_in[0], m_w_in[0], v_w_in[0], sums[0], sibling_sums[0], core)] + _adamw_pairs(
        shards[1:], [m_w_o_gmlp[0], m_w_o_sb[0], m_w_out[0]], [v_w_o_gmlp[0], v_w_o_sb[0], v_w_out[0]],
        sums[1:], sibling_sums[1:])

    *small, loss = _adamw_small(
        [norm_in, norm_v, b_s[0], norm_final.reshape(1, d), w_s[0]],
        [m_norm_in, m_norm_v, m_b_s[0], m_norm_final.reshape(1, d), m_w_s[0]],
        [v_norm_in, v_norm_v, v_b_s[0], v_norm_final.reshape(1, d), v_w_s[0]], slots_head, slots_tail)

    out = []
    for kind, (win, wog, wosb, wout) in enumerate(zip(*stats)):
        nin, nv, bs, nf, ws = small[kind::4]
        out += [nin, win[None], nv, ws[None], bs[None], wog[None], wosb[None], wout[None], nf.reshape(d)]
    return (loss.reshape(()), grad_x, *out)
```

```python
import functools
import math

import jax
import jax.numpy as jnp
from jax import lax
from jax.experimental import pallas as pl
from jax.experimental.pallas import tpu as pltpu

F32 = jnp.float32
BF16 = jnp.bfloat16
EPS = 1e-6
HEADS = 8
N_SPLIT = 9
N_CHIP = 4
MESH = pl.DeviceIdType.MESH

ADAM_LR = 0.001
ADAM_B1 = 0.9
ADAM_B2 = 0.999
ADAM_EPS = 1e-08
ADAM_WD = 0.01
ADAM_STEP = 10

VMEM_LIMIT = 56 * 2 ** 20
TM = 256
TMF = 512
TMX = 512
ATT_T = 256
ATT_HP = 4
TKW = 1024
CHUNKS_PER_STEP = 4
CAST_STEPS = 8

NT = (((1,), (1,)), ((), ()))
TN = (((0,), (0,)), ((), ()))


def _params(sem):
    return pltpu.CompilerParams(dimension_semantics=sem, vmem_limit_bytes=VMEM_LIMIT)


def _resident(shape):
    nd = len(shape)
    return pl.BlockSpec(shape, lambda *_: (0,) * nd, pipeline_mode=pl.Buffered(1))


def _const(shape):
    nd = len(shape)
    return pl.BlockSpec(shape, lambda *_: (0,) * nd)


def _segments(d, ncol):
    segs = []
    edges = sorted({j * ncol for j in range(N_CHIP + 1)} | {s * d for s in range(N_SPLIT + 1)})
    for lo, hi in zip(edges[:-1], edges[1:]):
        segs.append((lo // ncol, lo % ncol, lo // d, lo % d, hi - lo))
    return segs


def _sigmoid(x):
    return 0.5 * jnp.tanh(0.5 * x) + 0.5


_GELU_C = math.sqrt(2.0 / math.pi)


_GELU_CA = _GELU_C * 0.044715


def _gelu(x):
    return x * (0.5 * jnp.tanh(x * (_GELU_C + _GELU_CA * (x * x))) + 0.5)


def _gelu_and_grad(x):
    x2 = x * x
    u = 0.5 * jnp.tanh(x * (_GELU_C + _GELU_CA * x2)) + 0.5
    slope = (1.0 - u) * (x * (_GELU_C + (3.0 * _GELU_CA) * x2))
    return x * u, u * (2.0 * slope + 1.0)


def _split_bf16(a):
    hi = a.astype(BF16)
    lo = (a - hi.astype(F32)).astype(BF16)
    return hi, lo


def _in_proj_fwd(x2d, g_in, halves):
    t, d = x2d.shape
    n = len(halves)
    ncol = halves[0].shape[2]
    n_row = t // TMF
    last = n_row - 1
    assert halves[0].shape[1] * 2 == d
    qkv_parts = {j: (max(j * ncol, 3 * d) - j * ncol, max(j * ncol, 3 * d) - 3 * d)
                 for j in range(N_CHIP) if min((j + 1) * ncol, 6 * d) > max(j * ncol, 3 * d)}
    qkv_w = 3 * d // len(qkv_parts)
    assert all(min((j + 1) * ncol, 6 * d) - max(j * ncol, 3 * d) == qkv_w and q0 % qkv_w == 0
               for j, (_, q0) in qkv_parts.items())

    def body(order_ref, qrow_ref, qcol_ref, x_ref, g_ref, *rest):
        ins = rest[:n]
        proj_ref, qkv_ref, ht_ref = rest[n:n + 3]
        outs = rest[n + 3:2 * n + 3]
        wbuf, h_all, send_sems, recv_sems, local_sems, load_sem = rest[2 * n + 3:]
        phase = pl.program_id(0)
        i = pl.program_id(1)
        x_pos, y_pos, c_pos, chips = _place()
        sibling = (x_pos, y_pos, 1 - c_pos)
        me = (x_pos, y_pos, c_pos)
        my_chip = 2 * x_pos + y_pos

        def copy(a, k, block, to, src=None):
            return pltpu.make_async_remote_copy(
                src_ref=outs[a].at[block] if src is None else src, dst_ref=outs[a].at[block],
                send_sem=send_sems.at[a, k], recv_sem=recv_sems.at[a, k], device_id=to, device_id_type=MESH)

        def local(a):
            return pltpu.make_async_copy(ins[a], outs[a].at[pl.ds(2 * my_chip, 2)], local_sems.at[a])

        def load(src, first, slot):
            for half in range(2):
                cp = pltpu.make_async_copy(src.at[first + half], wbuf.at[slot, pl.ds(half * (d // 2), d // 2)], load_sem)
                cp.start()
                cp.wait()

        def relay(a, k, block, piece, to):
            rows = halves[a].shape[1] // 2
            ref = outs[a].at[block, pl.ds(piece * rows, rows)]
            return pltpu.make_async_remote_copy(
                src_ref=ref, dst_ref=ref, send_sem=send_sems.at[a, k], recv_sem=recv_sems.at[a, k],
                device_id=to, device_id_type=MESH)

        x_nbr, y_nbr, diagonal = chips
        first_block = lambda chip_xy: 2 * (2 * chip_xy[0] + chip_xy[1])

        def neighbours_arrived(arrays):
            from_x, from_y = first_block(x_nbr) + c_pos, first_block(y_nbr) + c_pos
            for a in arrays:
                copy(a, 0, from_x, me).wait_recv()
                copy(a, 1, from_y, me).wait_recv()
                relay(a, 2, from_x, 0, (*y_nbr, c_pos)).start()
                relay(a, 3, from_y, 1, (*x_nbr, c_pos)).start()
                copy(a, 4, from_x, sibling).start()
                copy(a, 5, from_y, sibling).start()

        def diagonal_arrived(arrays):
            from_diagonal = first_block(diagonal) + c_pos
            for a in arrays:
                relay(a, 2, from_diagonal, 0, me).wait_recv()
                relay(a, 3, from_diagonal, 1, me).wait_recv()
                copy(a, 6, from_diagonal, sibling).start()

        def from_sibling(a, k, chip_xy):
            copy(a, k, first_block(chip_xy) + 1 - c_pos, me).wait_recv()

        @pl.when((phase == 0) & (i == 0))
        def _():
            for a in range(n):
                local(a).start()
            for k, (px, py) in enumerate((x_nbr, y_nbr)):
                for a in range(n):
                    copy(a, k, 2 * my_chip + c_pos, (px, py, c_pos), src=ins[a].at[c_pos]).start()
            load(ins[0], 0, 0)

        @pl.when((phase == 1) & (i == 0))
        def _():
            neighbours_arrived([0])
            from_sibling(0, 4, x_nbr)
            load(outs[0], first_block(x_nbr), 1)

        @pl.when((phase == 2) & (i == 0))
        def _():
            from_sibling(0, 5, y_nbr)
            load(outs[0], first_block(y_nbr), 0)
            neighbours_arrived(range(1, n))

        @pl.when((phase == 3) & (i == 0))
        def _():
            diagonal_arrived(range(n))
            from_sibling(0, 6, diagonal)
            load(outs[0], first_block(diagonal), 1)

        @pl.when(phase == 0)
        def _():
            x = x_ref[...]
            r = lax.rsqrt(jnp.mean(x * x, axis=-1, keepdims=True) + EPS)
            hf = x * r * g_ref[...]
            h_all[i] = hf.astype(BF16)
            ht_ref[...] = hf.T.astype(BF16)

        for slot in range(2):
            @pl.when(phase % 2 == slot)
            def _(slot=slot):
                proj_ref[...] = jnp.dot(h_all[i], wbuf[slot], preferred_element_type=F32)

        for chunk, (c0, _) in qkv_parts.items():
            @pl.when(order_ref[phase] == chunk)
            def _(c0=c0):
                qkv_ref[...] = proj_ref[:, c0:c0 + qkv_w].astype(BF16)

        @pl.when((phase == N_CHIP - 1) & (i == n_row - 1))
        def _():
            for a in range(1, n):
                for k, chip_xy in ((4, x_nbr), (5, y_nbr), (6, diagonal)):
                    from_sibling(a, k, chip_xy)
            for a in range(n):
                for k in (0, 1, 4, 5, 6):
                    copy(a, k, 0, me).wait_send()
                for k in (2, 3):
                    relay(a, k, 0, 0, me).wait_send()
                local(a).wait()

    x_pos, y_pos = lax.axis_index("x"), lax.axis_index("y")
    order = jnp.stack([2 * x_pos + y_pos, 2 * (1 - x_pos) + y_pos, 2 * x_pos + 1 - y_pos,
                       2 * (1 - x_pos) + 1 - y_pos]).astype(jnp.int32)
    holds = [functools.reduce(jnp.logical_or, [order[p] == j for j in qkv_parts]) for p in range(N_CHIP)]
    col = [sum(jnp.where(order[p] == j, q0 // qkv_w, 0) for j, (_, q0) in qkv_parts.items()) for p in range(N_CHIP)]
    cur = col[-1]
    for p in reversed(range(N_CHIP - 1)):
        cur = jnp.where(holds[p], col[p], cur)
    seen = jnp.bool_(False)
    qrow, qcol = [], []
    for p in range(N_CHIP):
        cur = jnp.where(holds[p], col[p], cur)
        qrow.append(jnp.where(holds[p], -1, jnp.where(seen, last, 0)))
        qcol.append(cur)
        seen = seen | holds[p]
    qrow = jnp.stack(qrow).astype(jnp.int32)
    qcol = jnp.stack(qcol).astype(jnp.int32)

    outs = pl.pallas_call(
        body, name="in_proj_fwd",
        grid_spec=pltpu.PrefetchScalarGridSpec(
            num_scalar_prefetch=3, grid=(N_CHIP, n_row),
            in_specs=[pl.BlockSpec((TMF, d), lambda p, i, order, qrow, qcol: (jnp.where(p == 0, i, last), 0)),
                      pl.BlockSpec((1, d), lambda p, i, order, qrow, qcol: (0, 0))] + [ANY] * n,
            out_specs=[pl.BlockSpec((TMF, ncol), lambda p, i, order, qrow, qcol: (i, order[p])),
                       pl.BlockSpec((TMF, qkv_w),
                                    lambda p, i, order, qrow, qcol: (jnp.where(qrow[p] < 0, i, qrow[p]), qcol[p])),
                       pl.BlockSpec((d, TMF), lambda p, i, order, qrow, qcol: (0, jnp.where(p == 0, i, last)))]
            + [ANY] * n,
            scratch_shapes=[pltpu.VMEM((2, d, ncol), BF16), pltpu.VMEM((n_row, TMF, d), BF16),
                            pltpu.SemaphoreType.DMA((n, 7)), pltpu.SemaphoreType.DMA((n, 7)),
                            pltpu.SemaphoreType.DMA((n,)), pltpu.SemaphoreType.DMA]),
        out_shape=[jax.ShapeDtypeStruct((t, N_CHIP * ncol), F32), jax.ShapeDtypeStruct((t, 3 * d), BF16),
                   jax.ShapeDtypeStruct((d, t), BF16)]
        + [jax.ShapeDtypeStruct((2 * N_CHIP,) + hv.shape[1:], BF16) for hv in halves],
        compiler_params=_params(("arbitrary", "arbitrary")),
    )(order, qrow, qcol, x2d, g_in, *halves)
    return outs[0], outs[1], outs[2], outs[3:]


def _branch_a_fwd(a_pre, g_v, wm, b_t):
    t = a_pre.shape[0]
    d = g_v.shape[1]
    d3 = 3 * d
    ng, chunk, _ = wm.shape
    cw = d // ng

    per_step = CHUNKS_PER_STEP if t % (CHUNKS_PER_STEP * chunk) == 0 else 1

    def body(a_ref, gv_ref, wm_ref, bt_ref, ya_ref):
        for n in range(per_step):
            rows = slice(n * chunk, (n + 1) * chunk)
            ua = _gelu(a_ref[rows, 0:d])
            vg = _gelu(a_ref[rows, d:2 * d])
            za = a_ref[rows, 2 * d:3 * d]
            rv = lax.rsqrt(jnp.mean(vg * vg, axis=-1, keepdims=True) + EPS)
            va = (vg * rv * gv_ref[...]).astype(BF16)
            gate = ua * (za * _sigmoid(za))
            for g in range(ng):
                sl = slice(g * cw, (g + 1) * cw)
                mixed = jnp.dot(wm_ref[g], va[:, sl], preferred_element_type=F32) + bt_ref[:, g:g + 1]
                ya_ref[rows, sl] = (gate[:, sl] * mixed).astype(BF16)

    tile = per_step * chunk
    return pl.pallas_call(
        body, name="branch_a_fwd",
        grid=(t // tile,),
        in_specs=[pl.BlockSpec((tile, d3), lambda i: (i, 0)), _const((1, d)), _const(wm.shape), _const(b_t.shape)],
        out_specs=pl.BlockSpec((tile, d), lambda i: (i, 0)),
        out_shape=jax.ShapeDtypeStruct((t, d), BF16),
        compiler_params=_params(("arbitrary",)),
    )(a_pre, g_v, wm, b_t)


def _branch_a_bwd(a_pre, dya, g_v, wm, wm_t, b_t):
    t = a_pre.shape[0]
    d = g_v.shape[1]
    d3 = 3 * d
    ng, chunk, _ = wm.shape
    cw = d // ng
    tile = CHUNKS_PER_STEP * chunk
    nsteps = t // tile

    def body(a_ref, dya_ref, gv_ref, wm_ref, wmt_ref, bt_ref, da_ref, gws_ref, gbt_ref, gnv_ref, db_acc):
        i = pl.program_id(0)

        @pl.when(i == 0)
        def _():
            gws_ref[...] = jnp.zeros_like(gws_ref)
            gnv_ref[...] = jnp.zeros_like(gnv_ref)
            db_acc[...] = jnp.zeros_like(db_acc)

        def one_chunk(c, carry):
            rows = pl.ds(pl.multiple_of(c * chunk, chunk), chunk)
            ua, dgelu_u = _gelu_and_grad(a_ref[rows, 0:d])
            vg, dgelu_v = _gelu_and_grad(a_ref[rows, d:2 * d])
            za = a_ref[rows, 2 * d:3 * d]
            sig = _sigmoid(za)
            sz = za * sig
            dsz = sig * (1.0 + za * (1.0 - sig))
            rv = lax.rsqrt(jnp.mean(vg * vg, axis=-1, keepdims=True) + EPS)
            nv = vg * rv
            gv = gv_ref[...]
            va = (nv * gv).astype(BF16)
            dya = dya_ref[rows, :]
            dmix = dya * ua * sz
            db_acc[...] += dmix
            dmix_b = dmix.astype(BF16)
            t_gate = dya * sz
            t_z = dya * ua * dsz
            dva_parts = []
            for g in range(ng):
                sl = slice(g * cw, (g + 1) * cw)
                mixed = jnp.dot(wm_ref[g], va[:, sl], preferred_element_type=F32) + bt_ref[:, g:g + 1]
                da_ref[rows, sl] = (t_gate[:, sl] * mixed * dgelu_u[:, sl]).astype(BF16)
                da_ref[rows, 2 * d + g * cw:2 * d + (g + 1) * cw] = (t_z[:, sl] * mixed).astype(BF16)
                gws_ref[g] += lax.dot_general(dmix_b[:, sl], va[:, sl], NT, preferred_element_type=F32)
                dva_parts.append(jnp.dot(wmt_ref[g], dmix_b[:, sl], preferred_element_type=F32))
            dva = jnp.concatenate(dva_parts, axis=1)
            gnv_ref[...] += jnp.sum(dva * nv, axis=0, keepdims=True)
            dnv = dva * gv
            dvg = rv * (dnv - nv * jnp.mean(dnv * nv, axis=-1, keepdims=True))
            da_ref[rows, d:2 * d] = (dvg * dgelu_v).astype(BF16)
            return carry

        lax.fori_loop(0, CHUNKS_PER_STEP, one_chunk, 0)

        @pl.when(i == nsteps - 1)
        def _():
            acc = db_acc[...]
            for g in range(ng):
                gbt_ref[:, g:g + 1] = jnp.sum(acc[:, g * cw:(g + 1) * cw], axis=1, keepdims=True)

    return pl.pallas_call(
        body, name="branch_a_bwd",
        grid=(nsteps,),
        in_specs=[pl.BlockSpec((tile, d3), lambda i: (i, 0)), pl.BlockSpec((tile, d), lambda i: (i, 0)),
                  _const((1, d)), _const(wm.shape), _const(wm_t.shape), _const(b_t.shape)],
        out_specs=[pl.BlockSpec((tile, d3), lambda i: (i, 0)), _const(wm.shape), _const(b_t.shape), _const((1, d))],
        out_shape=[jax.ShapeDtypeStruct((t, d3), BF16), jax.ShapeDtypeStruct(wm.shape, F32),
                   jax.ShapeDtypeStruct(b_t.shape, F32), jax.ShapeDtypeStruct((1, d), F32)],
        scratch_shapes=[pltpu.VMEM((chunk, d), F32)],
        compiler_params=_params(("arbitrary",)),
    )(a_pre, dya, g_v, wm, wm_t, b_t)


def _below_diagonal(n):
    return lax.broadcasted_iota(jnp.int32, (n, n), 0) > lax.broadcasted_iota(jnp.int32, (n, n), 1)


def _twice(tri):
    t = tri.astype(BF16)
    return jnp.concatenate([t, t], axis=0)


def _cumsum_mm(a, tri2):
    hi, lo = _split_bf16(a)
    return jnp.dot(jnp.concatenate([hi, lo], axis=1), tri2, preferred_element_type=F32)


LOG2E = 1.4426950408889634
_SIGN = 0x80000000


def _sb_block(q, k, scale, upper2, causal):
    z2 = lax.dot_general(q, k, NT, preferred_element_type=F32) * (scale * LOG2E)
    neg_abs = lax.bitcast_convert_type(lax.bitcast_convert_type(z2, jnp.uint32) | jnp.uint32(_SIGN), F32)
    l2 = jnp.log(1.0 + jnp.exp2(neg_abs)) * LOG2E
    log_beta = jnp.minimum(z2, 0.0) - l2
    lom = log_beta - z2
    if causal is not None:
        lom = jnp.where(causal, lom, 0.0)
    sx = _cumsum_mm(lom, upper2)
    return log_beta, sx, sx[:, 0:1] + lom[:, 0:1]


DEAD_LOG2 = -150.0


def _max_carry(carries):
    return jnp.max(functools.reduce(jnp.maximum, carries))


ZB_GROUP, GA_GROUP, GB_GROUP = 6, 7, 8


def _attn_specs(d, seq, nq, heads_per_step):
    hp_w = heads_per_step * (d // HEADS)
    n_hp = d // hp_w
    row_blk = lambda group: pl.BlockSpec((ATT_T, hp_w), lambda b, h, i: (b * nq + i, group * n_hp + h))
    seq_blk = lambda group: pl.BlockSpec((seq, hp_w), lambda b, h, i: (b, group * n_hp + h))
    return row_blk, seq_blk, n_hp


def _attn_fwd(qkv, proj, bsz, seq):
    t, d3 = qkv.shape
    d = d3 // 3
    hd = d // HEADS
    nq = seq // ATT_T
    scale = hd ** -0.5
    n_heads = ATT_HP
    row_blk, seq_blk, n_hp = _attn_specs(d, seq, nq, n_heads)

    def body(q_ref, k_ref, v_ref, zb_ref, o_ref, yb_ref):
        i = pl.program_id(2)
        causal = _below_diagonal(ATT_T)
        upper2 = _twice(causal)

        def step(kb, state, mask):
            rows = pl.ds(pl.multiple_of(kb * ATT_T, ATT_T), ATT_T)
            heads = [slice(h * hd, (h + 1) * hd) for h in range(n_heads)]
            scores = [_sb_block(q_ref[:, cols], k_ref[rows, cols], scale, upper2, mask) for cols in heads]
            new = []
            for cols, (carry, acc), (log_beta, sx, total) in zip(heads, state, scores):
                a = jnp.exp2(log_beta + sx + carry)
                if mask is not None:
                    a = jnp.where(mask, a, 0.0)
                acc = acc + jnp.dot(a.astype(BF16), v_ref[rows, cols], preferred_element_type=F32)
                new.append((carry + total, acc))
            return tuple(new)

        init = tuple((jnp.zeros((ATT_T, 1), F32), jnp.zeros((ATT_T, hd), F32)) for _ in range(n_heads))
        state = step(i, init, causal)
        def more(c):
            new = step(c[0], c[1], None)
            return c[0] - 1, new, _max_carry([s[0] for s in new])

        _, state, _ = lax.while_loop(lambda c: (c[0] >= 0) & (c[2] > DEAD_LOG2), more,
                                     (i - 1, state, _max_carry([s[0] for s in state])))
        for h in range(n_heads):
            cols = slice(h * hd, (h + 1) * hd)
            acc = state[h][1]
            zb = zb_ref[:, cols]
            o_ref[:, cols] = acc
            yb_ref[:, cols] = (acc * (zb * _sigmoid(zb))).astype(BF16)

    return pl.pallas_call(
        body, name="attn_fwd",
        grid=(bsz, n_hp, nq),
        in_specs=[row_blk(0), seq_blk(1), seq_blk(2), row_blk(ZB_GROUP)],
        out_specs=[row_blk(0), row_blk(0)],
        out_shape=[jax.ShapeDtypeStruct((t, d), F32), jax.ShapeDtypeStruct((t, d), BF16)],
        compiler_params=_params(("arbitrary", "arbitrary", "arbitrary")),
    )(qkv, qkv, qkv, proj)


def _attn_bwd(qkv, proj, o, dyb, bsz, seq):
    t, d3 = qkv.shape
    d = d3 // 3
    hd = d // HEADS
    nq = seq // ATT_T
    scale = hd ** -0.5
    row_blk, seq_blk, n_hp = _attn_specs(d, seq, nq, ATT_HP)

    def body(q_ref, k_ref, v_ref, zb_ref, o_ref, dyb_ref, dq_ref, dk_ref, dv_ref, dzb_ref,
             g_s, beta_s, dkt_acc, dvt_acc):
        i = pl.program_id(2)

        @pl.when(i == 0)
        def _():
            dkt_acc[...] = jnp.zeros_like(dkt_acc)
            dvt_acc[...] = jnp.zeros_like(dvt_acc)

        causal = _below_diagonal(ATT_T)
        upper2 = _twice(causal)
        lower2 = _twice(~causal)
        zb = zb_ref[...]
        sig = _sigmoid(zb)
        dyb_t = dyb_ref[...]
        do_f = dyb_t * (zb * sig)
        do = do_f.astype(BF16)
        do_t = do_f.T.astype(BF16)
        q_t = q_ref[...].astype(F32).T.astype(BF16)
        dzb_ref[...] = (dyb_t * o_ref[...] * (sig * (1.0 + zb * (1.0 - sig)))).astype(BF16)

        def sweep(kb, carries, mask):
            rows = pl.ds(pl.multiple_of(kb * ATT_T, ATT_T), ATT_T)
            heads = [slice(h * hd, (h + 1) * hd) for h in range(ATT_HP)]
            scores = [_sb_block(q_ref[:, cols], k_ref[rows, cols], scale, upper2, mask) for cols in heads]
            das = [lax.dot_general(do[:, cols], v_ref[rows, cols], NT, preferred_element_type=F32) for cols in heads]
            new = []
            for h, (cols, carry, (log_beta, sx, total), da) in enumerate(zip(heads, carries, scores, das)):
                a = jnp.exp2(log_beta + sx + carry)
                beta = jnp.exp2(log_beta)
                if mask is not None:
                    a = jnp.where(mask, a, 0.0)
                    beta = jnp.where(mask, beta, 0.0)
                g_s[h, kb] = a * da
                beta_s[h, kb] = beta
                dvt_acc[kb, cols, :] += jnp.dot(do_t[cols, :], a.astype(BF16), preferred_element_type=F32)
                new.append(carry + total)
            return tuple(new)

        carries = sweep(i, tuple(jnp.zeros((ATT_T, 1), F32) for _ in range(ATT_HP)), causal)

        def more(c):
            new = sweep(c[0], c[1], None)
            return c[0] - 1, new, _max_carry(new)

        last, _, _ = lax.while_loop(lambda c: (c[0] >= 0) & (c[2] > DEAD_LOG2), more, (i - 1, carries, _max_carry(carries)))
        first_kb = last + 1

        def back(kb, state):
            rows = pl.ds(pl.multiple_of(kb * ATT_T, ATT_T), ATT_T)
            heads = [slice(h * hd, (h + 1) * hd) for h in range(ATT_HP)]
            sums = [_cumsum_mm(g_s[h, kb], lower2) for h in range(ATT_HP)]
            new = []
            for h, (cols, (p_carry, dq), px) in enumerate(zip(heads, state, sums)):
                dz = ((g_s[h, kb] - (p_carry + px) * beta_s[h, kb]) * scale).astype(BF16)
                dq = dq + jnp.dot(dz, k_ref[rows, cols], preferred_element_type=F32)
                dkt_acc[kb, cols, :] += jnp.dot(q_t[cols, :], dz, preferred_element_type=F32)
                new.append((p_carry + px[:, ATT_T - 1:ATT_T], dq))
            return tuple(new)

        init = tuple((jnp.zeros((ATT_T, 1), F32), jnp.zeros((ATT_T, hd), F32)) for _ in range(ATT_HP))
        state = lax.fori_loop(first_kb, i + 1, back, init)
        for h in range(ATT_HP):
            dq_ref[:, h * hd:(h + 1) * hd] = state[h][1].astype(BF16)

        @pl.when(i == nq - 1)
        def _():
            for kb in range(nq):
                dk_ref[kb * ATT_T:(kb + 1) * ATT_T, :] = dkt_acc[kb].T.astype(BF16)
                dv_ref[kb * ATT_T:(kb + 1) * ATT_T, :] = dvt_acc[kb].T.astype(BF16)

    out = jax.ShapeDtypeStruct((t, d), BF16)
    hp_w = ATT_HP * hd
    return pl.pallas_call(
        body, name="attn_bwd",
        grid=(bsz, n_hp, nq),
        in_specs=[row_blk(0), seq_blk(1), seq_blk(2), row_blk(ZB_GROUP), row_blk(0), row_blk(0)],
        out_specs=[row_blk(0), seq_blk(0), seq_blk(0), row_blk(0)],
        out_shape=[out, out, out, out],
        scratch_shapes=[pltpu.VMEM((ATT_HP, nq, ATT_T, ATT_T), F32), pltpu.VMEM((ATT_HP, nq, ATT_T, ATT_T), F32),
                        pltpu.VMEM((nq, hp_w, ATT_T), F32), pltpu.VMEM((nq, hp_w, ATT_T), F32)],
        compiler_params=_params(("arbitrary", "arbitrary", "arbitrary")),
    )(qkv, qkv, qkv, proj, o, dyb)


def _out_proj(ya, yb, g_pre, x2d, tgt, w_og, w_osb, w_out, g_f):
    t, d = x2d.shape

    def body(ya_ref, yb_ref, ga_ref, gb_ref, x_ref, tgt_ref, wog_ref, wosb_ref, wout_ref, gf_ref,
             dya_ref, dyb_ref, dg_ref, dx2_ref, loss_ref, gnf_ref, gwog_ref, gwosb_ref, gwout_ref):
        @pl.when(pl.program_id(0) == 0)
        def _():
            loss_ref[...] = jnp.zeros_like(loss_ref)
            gnf_ref[...] = jnp.zeros_like(gnf_ref)
            gwog_ref[...] = jnp.zeros_like(gwog_ref)
            gwosb_ref[...] = jnp.zeros_like(gwosb_ref)
            gwout_ref[...] = jnp.zeros_like(gwout_ref)

        ya = ya_ref[...]
        yb = yb_ref[...]
        pa = jnp.dot(ya, wog_ref[...], preferred_element_type=F32)
        pb = jnp.dot(yb, wosb_ref[...], preferred_element_type=F32)
        sga = _sigmoid(ga_ref[...])
        sgb = _sigmoid(gb_ref[...])
        merged = (sga * pa + sgb * pb).astype(BF16)
        x2 = x_ref[...] + jnp.dot(merged, wout_ref[...], preferred_element_type=F32)
        r2 = lax.rsqrt(jnp.mean(x2 * x2, axis=-1, keepdims=True) + EPS)
        n2 = x2 * r2
        gf = gf_ref[...]
        err = n2 * gf - tgt_ref[...]
        loss_ref[...] += 0.5 * jnp.sum(jnp.sum(err * err, axis=-1, keepdims=True), axis=0, keepdims=True) / d
        dy = err * (1.0 / d)
        gnf_ref[...] += jnp.sum(dy * n2, axis=0, keepdims=True)
        dn = dy * gf
        dx2 = r2 * (dn - n2 * jnp.mean(dn * n2, axis=-1, keepdims=True))
        dx2_ref[...] = dx2
        dx2_b = dx2.astype(BF16)
        dmerged = lax.dot_general(dx2_b, wout_ref[...], NT, preferred_element_type=F32)
        gwout_ref[...] += lax.dot_general(merged, dx2_b, TN, preferred_element_type=F32)
        dg_ref[:, 0:d] = (dmerged * pa * (sga * (1.0 - sga))).astype(BF16)
        dg_ref[:, d:2 * d] = (dmerged * pb * (sgb * (1.0 - sgb))).astype(BF16)
        dpa = (dmerged * sga).astype(BF16)
        dpb = (dmerged * sgb).astype(BF16)
        dya_ref[...] = lax.dot_general(dpa, wog_ref[...], NT, preferred_element_type=F32)
        dyb_ref[...] = lax.dot_general(dpb, wosb_ref[...], NT, preferred_element_type=F32)
        gwog_ref[...] += lax.dot_general(ya, dpa, TN, preferred_element_type=F32)
        gwosb_ref[...] += lax.dot_general(yb, dpb, TN, preferred_element_type=F32)

    row = lambda i: (i, 0)
    return pl.pallas_call(
        body, name="out_proj",
        grid=(t // TM,),
        in_specs=[pl.BlockSpec((TM, d), row), pl.BlockSpec((TM, d), row),
                  pl.BlockSpec((TM, d), lambda i: (i, GA_GROUP)), pl.BlockSpec((TM, d), lambda i: (i, GB_GROUP)),
                  pl.BlockSpec((TM, d), row), pl.BlockSpec((TM, d), row),
                  _resident((d, d)), _resident((d, d)), _resident((d, d)), _const((1, d))],
        out_specs=[pl.BlockSpec((TM, d), row), pl.BlockSpec((TM, d), row), pl.BlockSpec((TM, 2 * d), row),
                   pl.BlockSpec((TM, d), row), _const((1, 1)), _const((1, d)),
                   _const((d, d)), _const((d, d)), _const((d, d))],
        out_shape=[jax.ShapeDtypeStruct((t, d), F32), jax.ShapeDtypeStruct((t, d), F32),
                   jax.ShapeDtypeStruct((t, 2 * d), BF16), jax.ShapeDtypeStruct((t, d), F32),
                   jax.ShapeDtypeStruct((1, 1), F32), jax.ShapeDtypeStruct((1, d), F32),
                   jax.ShapeDtypeStruct((d, d), F32), jax.ShapeDtypeStruct((d, d), F32),
                   jax.ShapeDtypeStruct((d, d), F32)],
        compiler_params=_params(("arbitrary",)),
    )(ya, yb, g_pre, g_pre, x2d, tgt, w_og, w_osb, w_out, g_f)


def _dproj_pieces(d):
    return [(0, 0, 3), (1, 3, 1), (2, 4, 1), (3, 5, 1), (4, 6, 1), (5, 7, 2)]


def _in_proj_bwd_x(pieces, wg, x2d, g_in, dx2, gw16):
    t, d = x2d.shape
    ncol = wg.shape[2]
    segs = _segments(d, ncol)
    layout = _dproj_pieces(d)
    nsteps = t // TMX

    def body(da_ref, dq_ref, dk_ref, dv_ref, dzb_ref, dg_ref, w_ref, x_ref, g_ref, dx2_ref, gw16_ref,
             gx_ref, gn_ref, recv_ref, send_sems, recv_sems):
        x_pos, y_pos, c_pos, chips = _place()

        def share(k, chunk):
            px, py = chips[k]
            return pltpu.make_async_remote_copy(
                src_ref=gw16_ref.at[:, chunk * ncol:(chunk + 1) * ncol], dst_ref=recv_ref.at[k],
                send_sem=send_sems.at[k], recv_sem=recv_sems.at[k], device_id=(px, py, c_pos), device_id_type=MESH)

        @pl.when(pl.program_id(0) == 0)
        def _():
            gn_ref[...] = jnp.zeros_like(gn_ref)
            for k, (px, py) in enumerate(chips):
                for chunk in range(N_CHIP):
                    @pl.when(2 * px + py == chunk)
                    def _(k=k, chunk=chunk):
                        share(k, chunk).start()

        @pl.when(pl.program_id(0) == nsteps - 1)
        def _():
            for k in range(N_CHIP - 1):
                share(k, 0).wait()

        refs = (da_ref, dq_ref, dk_ref, dv_ref, dzb_ref, dg_ref)
        dh = jnp.zeros((TMX, d), F32)
        for chip, c0, grp, s0, width in segs:
            piece, first, _ = next(p for p in layout if p[1] <= grp < p[1] + p[2])
            off = (grp - first) * d + s0
            dh = dh + lax.dot_general(refs[piece][:, off:off + width], w_ref[chip, :, c0:c0 + width], NT,
                                      preferred_element_type=F32)
        x = x_ref[...]
        r = lax.rsqrt(jnp.mean(x * x, axis=-1, keepdims=True) + EPS)
        n = x * r
        gn_ref[...] += jnp.sum(dh * n, axis=0, keepdims=True)
        dn = dh * g_ref[...]
        gx_ref[...] = dx2_ref[...] + r * (dn - n * jnp.mean(dn * n, axis=-1, keepdims=True))

    row = lambda i: (i, 0)
    return pl.pallas_call(
        body, name="in_proj_bwd_x",
        grid=(nsteps,),
        in_specs=[pl.BlockSpec((TMX, p.shape[1]), row) for p in pieces]
        + [_resident(wg.shape), pl.BlockSpec((TMX, d), row), _const((1, d)), pl.BlockSpec((TMX, d), row), ANY],
        out_specs=[pl.BlockSpec((TMX, d), row), _const((1, d)), ANY],
        out_shape=[jax.ShapeDtypeStruct((t, d), F32), jax.ShapeDtypeStruct((1, d), F32),
                   jax.ShapeDtypeStruct((N_CHIP - 1, gw16.shape[0], ncol), BF16)],
        scratch_shapes=[pltpu.SemaphoreType.DMA((N_CHIP - 1,)), pltpu.SemaphoreType.DMA((N_CHIP - 1,))],
        compiler_params=_params(("arbitrary",)),
    )(*pieces, wg, x2d, g_in, dx2, gw16)


def _in_proj_bwd_w(h_t, pieces, mats16, pack):
    d, t = h_t.shape
    nk = t // TKW
    half = d // 2
    layout = _dproj_pieces(d)
    n_mats = len(mats16)
    n_dev = 2 * N_CHIP
    flips = [(dx, dy, dc) for dx in (0, 1) for dy in (0, 1) for dc in (0, 1)][1:]

    def body(ht_ref, da_ref, dq_ref, dk_ref, dv_ref, dzb_ref, dg_ref, *rest):
        mat_refs, pack_ref = rest[:n_mats], rest[n_mats]
        gw_ref, sib_ref = rest[n_mats + 1:n_mats + 3]
        recv_refs, slots_ref = rest[n_mats + 3:2 * n_mats + 3], rest[2 * n_mats + 3]
        acc, stage, mat_send, mat_recv, pack_send, pack_recv, own_sem, stage_send, sib_recv = rest[2 * n_mats + 4:]
        s = pl.program_id(0)
        i = pl.program_id(1)
        x_pos, y_pos, c_pos, chips = _place()
        me = 4 * x_pos + 2 * y_pos + c_pos

        def to_sibling(slot, group):
            return pltpu.make_async_remote_copy(
                src_ref=stage.at[slot], dst_ref=sib_ref.at[:, group * d:(group + 1) * d],
                send_sem=stage_send.at[slot], recv_sem=sib_recv, device_id=(x_pos, y_pos, 1 - c_pos), device_id_type=MESH)

        def exchanges():
            cps = []
            for k, (px, py) in enumerate(chips):
                for a in range(n_mats):
                    cps.append(pltpu.make_async_remote_copy(
                        src_ref=mat_refs[a].at[2 * px + py], dst_ref=recv_refs[a].at[k],
                        send_sem=mat_send.at[a, k], recv_sem=mat_recv.at[a, k],
                        device_id=(px, py, c_pos), device_id_type=MESH))
            for k, (dx, dy, dc) in enumerate(flips):
                peer = (1 - x_pos if dx else x_pos, 1 - y_pos if dy else y_pos, 1 - c_pos if dc else c_pos)
                cps.append(pltpu.make_async_remote_copy(
                    src_ref=pack_ref, dst_ref=slots_ref.at[me], send_sem=pack_send.at[k], recv_sem=pack_recv.at[k],
                    device_id=peer, device_id_type=MESH))
            return cps, pltpu.make_async_copy(pack_ref, slots_ref.at[me], own_sem)

        @pl.when((s == 0) & (i == 0))
        def _():
            cps, own = exchanges()
            own.start()
            for cp in cps:
                cp.start()

        @pl.when(i == 0)
        def _():
            acc[...] = jnp.zeros_like(acc)

        refs = (da_ref, dq_ref, dk_ref, dv_ref, dzb_ref, dg_ref)
        for piece, first, count in layout:
            @pl.when((s >= first) & (s < first + count))
            def _(piece=piece):
                acc[...] += jnp.dot(ht_ref[...], refs[piece][...], preferred_element_type=F32)

        @pl.when(i == nk - 1)
        def _():
            gw_ref[...] = acc[...]
            for slot in range(2):
                @pl.when(s % 2 == slot)
                def _(slot=slot):
                    @pl.when(s >= 2)
                    def _():
                        to_sibling(slot, 0).wait_send()
                    for other in range(2):
                        @pl.when(c_pos == 1 - other)
                        def _(other=other):
                            stage[slot] = acc[other * half:(other + 1) * half, :].astype(BF16)
                    for group in range(N_SPLIT):
                        @pl.when(s == group)
                        def _(group=group):
                            to_sibling(slot, group).start()

        @pl.when((s == N_SPLIT - 1) & (i == nk - 1))
        def _():
            for slot in range(2):
                to_sibling(slot, 0).wait_send()
            pltpu.make_async_remote_copy(
                src_ref=sib_ref, dst_ref=sib_ref, send_sem=stage_send.at[0], recv_sem=sib_recv,
                device_id=(x_pos, y_pos, c_pos), device_id_type=MESH).wait_recv()
            cps, own = exchanges()
            own.wait()
            for cp in cps:
                cp.wait()

    def piece_spec(p, first, count):
        def index(s, i):
            mine = (s >= first) & (s < first + count)
            return jnp.where(mine, i, 0), jnp.where(mine, s - first, 0)
        return pl.BlockSpec((TKW, d), index)

    col_blk = pl.BlockSpec((d, d), lambda s, i: (0, s))
    outs = pl.pallas_call(
        body, name="in_proj_bwd_w",
        grid=(N_SPLIT, nk),
        in_specs=[pl.BlockSpec((d, TKW), lambda s, i: (0, i))] + [piece_spec(*p) for p in layout] + [ANY] * (n_mats + 1),
        out_specs=[col_blk, ANY] + [ANY] * (n_mats + 1),
        out_shape=[jax.ShapeDtypeStruct((d, N_SPLIT * d), F32), jax.ShapeDtypeStruct((half, N_SPLIT * d), BF16)]
        + [jax.ShapeDtypeStruct((N_CHIP - 1,) + m.shape[1:], BF16) for m in mats16]
        + [jax.ShapeDtypeStruct((n_dev,) + pack.shape, F32)],
        scratch_shapes=[pltpu.VMEM((d, d), F32), pltpu.VMEM((2, half, d), BF16),
                        pltpu.SemaphoreType.DMA((n_mats, N_CHIP - 1)), pltpu.SemaphoreType.DMA((n_mats, N_CHIP - 1)),
                        pltpu.SemaphoreType.DMA((n_dev - 1,)), pltpu.SemaphoreType.DMA((n_dev - 1,)),
                        pltpu.SemaphoreType.DMA, pltpu.SemaphoreType.DMA((2,)), pltpu.SemaphoreType.DMA],
        compiler_params=_params(("arbitrary", "arbitrary")),
    )(h_t, *pieces, *mats16, pack)
    return outs[0], outs[1], outs[2:2 + n_mats], outs[2 + n_mats]


def _local_step(proj, qkv, x2d, tgt2d, bsz, seq, norm_v, w_s, b_s, w_og, w_osb, w_out, norm_final):
    d = x2d.shape[1]
    chunk = w_s.shape[-1]
    causal = jnp.tril(jnp.ones((chunk, chunk), dtype=bool))
    wm = jnp.where(causal[None], w_s, 0.0).astype(BF16)
    wm_t = jnp.swapaxes(wm, 1, 2)
    b_t = b_s.T

    ya = _branch_a_fwd(proj, norm_v, wm, b_t)
    o, yb = _attn_fwd(qkv, proj, bsz, seq)
    dya, dyb, dg, dx2, loss, g_nf, g_wog, g_wosb, g_wout = _out_proj(
        ya, yb, proj, x2d, tgt2d, w_og, w_osb, w_out, norm_final.reshape(1, d))
    dq, dk, dv, dzb = _attn_bwd(qkv, proj, o, dyb, bsz, seq)
    d_a, g_ws, g_bt, g_nv = _branch_a_bwd(proj, dya, norm_v, wm, wm_t, b_t)
    g_ws = jnp.where(causal[None], g_ws, 0.0)
    return loss, (d_a, dq, dk, dv, dzb, dg), dx2, g_nv, g_ws, g_bt.T, g_wog, g_wosb, g_wout, g_nf


def _row_tile(rows):
    return next(r for r in (128, 64, 32, 16, 8) if rows % r == 0)


def _cast_bf16(arrs):
    n = len(arrs)

    def body(*refs):
        for a_ref, o_ref in zip(refs[:n], refs[n:]):
            o_ref[...] = a_ref[...].astype(BF16)

    specs = [pl.BlockSpec((a.shape[0] // CAST_STEPS, a.shape[1]), lambda i: (i, 0)) for a in arrs]
    return pl.pallas_call(
        body, name="cast_bf16", grid=(CAST_STEPS,),
        in_specs=specs, out_specs=specs,
        out_shape=[jax.ShapeDtypeStruct(a.shape, BF16) for a in arrs],
        compiler_params=_params(("arbitrary",)),
    )(*arrs)


def _chip_half(full, from_sibling, core, chip, tile):
    half, n = from_sibling.shape

    def body(where_ref, own_ref, sib_ref, o32_ref, o16_ref):
        total = own_ref[...] + sib_ref[...].astype(F32)
        o16_ref[...] = total.astype(BF16)

        @pl.when(pl.program_id(0) == where_ref[1])
        def _():
            o32_ref[...] = total

    blk = pl.BlockSpec((half, tile), lambda j, where_ref: (0, j))
    return pl.pallas_call(
        body, name="chip_half",
        grid_spec=pltpu.PrefetchScalarGridSpec(
            num_scalar_prefetch=1, grid=(n // tile,),
            in_specs=[pl.BlockSpec((half, tile), lambda j, where_ref: (where_ref[0], j)), blk],
            out_specs=[pl.BlockSpec((half, tile), lambda j, where_ref: (0, 0)), blk]),
        out_shape=[jax.ShapeDtypeStruct((half, tile), F32), jax.ShapeDtypeStruct((half, n), BF16)],
        compiler_params=_params(("arbitrary",)),
    )(jnp.stack([core, chip]).astype(jnp.int32), full, from_sibling)


def _add_received(fulls, recvs, chip, by_cols):
    n = len(fulls)
    _, rows, cols = recvs[0].shape
    tr = _row_tile(rows)
    nb = rows // tr

    def body(chip_ref, *refs):
        for own_ref, recv_ref, o_ref in zip(refs[:n], refs[n:2 * n], refs[2 * n:]):
            s = own_ref[...]
            for k in range(N_CHIP - 1):
                s = s + recv_ref[k].astype(F32)
            o_ref[...] = s

    own_map = (lambda i, chip_ref: (i, chip_ref[0])) if by_cols else (lambda i, chip_ref: (chip_ref[0] * nb + i, 0))
    return pl.pallas_call(
        body, name="add_received",
        grid_spec=pltpu.PrefetchScalarGridSpec(
            num_scalar_prefetch=1, grid=(nb,),
            in_specs=[pl.BlockSpec((tr, cols), own_map)] * n
            + [pl.BlockSpec((N_CHIP - 1, tr, cols), lambda i, chip_ref: (0, i, 0))] * n,
            out_specs=[pl.BlockSpec((tr, cols), lambda i, chip_ref: (i, 0))] * n),
        out_shape=[jax.ShapeDtypeStruct((rows, cols), F32)] * n,
        compiler_params=_params(("arbitrary",)),
    )(chip.reshape(1).astype(jnp.int32), *fulls, *recvs)


def _adamw_math(w, m, v, g):
    new_m = ADAM_B1 * m + (1.0 - ADAM_B1) * g
    new_v = ADAM_B2 * v + (1.0 - ADAM_B2) * (g * g)
    m_hat = new_m / (1.0 - ADAM_B1 ** ADAM_STEP)
    v_hat = new_v / (1.0 - ADAM_B2 ** ADAM_STEP)
    return -ADAM_LR * (m_hat / (jnp.sqrt(v_hat) + ADAM_EPS) + ADAM_WD * w), new_m, new_v


def _adamw_pairs(ws, ms, vs, mines, theirs):
    n = len(ws)
    rows, cols = ws[0].shape
    tr = _row_tile(rows)

    def body(*refs):
        ins, outs = refs[:5 * n], refs[5 * n:]
        for a in range(n):
            w_ref, m_ref, v_ref, mine_ref, theirs_ref = ins[a::n]
            g_ref, d_ref, nm_ref, nv_ref = outs[4 * a:4 * a + 4]
            g = mine_ref[...] + theirs_ref[...]
            g_ref[...] = g
            d_ref[...], nm_ref[...], nv_ref[...] = _adamw_math(w_ref[...], m_ref[...], v_ref[...], g)

    spec = pl.BlockSpec((tr, cols), lambda i: (i, 0))
    out = jax.ShapeDtypeStruct((rows, cols), F32)
    outs = pl.pallas_call(
        body, name="adamw_pairs", grid=(rows // tr,),
        in_specs=[spec] * (5 * n), out_specs=[spec] * (4 * n), out_shape=[out] * (4 * n),
        compiler_params=_params(("arbitrary",)),
    )(*ws, *ms, *vs, *mines, *theirs)
    return [outs[4 * a:4 * a + 4] for a in range(n)]


def _adamw_halves(w, m, v, mine, theirs, core):
    rows, cols = w.shape
    tr = _row_tile(rows // 2)
    per_half = rows // 2 // tr

    def body(core_ref, w_ref, m_ref, v_ref, mine_ref, theirs_ref, g_ref, d_ref, nm_ref, nv_ref):
        is_mine = pl.program_id(0) // per_half == core_ref[0]
        for part, cond in ((mine_ref, is_mine), (theirs_ref, jnp.logical_not(is_mine))):
            @pl.when(cond)
            def _(part=part):
                g = part[...]
                g_ref[...] = g
                d_ref[...], nm_ref[...], nv_ref[...] = _adamw_math(w_ref[...], m_ref[...], v_ref[...], g)

    spec = pl.BlockSpec((tr, cols), lambda i, core_ref: (i, 0))

    def half_spec(own):
        def index(i, core_ref):
            in_core_half = i // per_half == core_ref[0]
            here = in_core_half if own else jnp.logical_not(in_core_half)
            return jnp.where(here, i % per_half, 0), 0
        return pl.BlockSpec((tr, cols), index)

    out = jax.ShapeDtypeStruct(w.shape, F32)
    return pl.pallas_call(
        body, name="adamw_halves",
        grid_spec=pltpu.PrefetchScalarGridSpec(
            num_scalar_prefetch=1, grid=(rows // tr,),
            in_specs=[spec] * 3 + [half_spec(True), half_spec(False)], out_specs=[spec] * 4),
        out_shape=[out] * 4,
        compiler_params=_params(("arbitrary",)),
    )(core.reshape(1).astype(jnp.int32), w, m, v, mine, theirs)


def _adamw_small(ws, ms, vs, slots_head, slots_tail):
    n_dev = slots_head.shape[0]
    n_par = len(ws)
    groups, chunk, _ = ws[4].shape

    def body(*refs):
        w_refs, m_refs, v_refs = refs[:n_par], refs[n_par:2 * n_par], refs[2 * n_par:3 * n_par]
        head_ref, tail_ref = refs[3 * n_par:3 * n_par + 2]
        out_refs, scalar_ref = refs[3 * n_par + 2:-1], refs[-1]

        def total(ref, rows, cols):
            g = ref[0, rows, cols]
            for i in range(1, n_dev):
                g = g + ref[i, rows, cols]
            return g

        def update(p, g, at=...):
            g_ref, d_ref, nm_ref, nv_ref = out_refs[4 * p:4 * p + 4]
            g_ref[at] = g
            d_ref[at], nm_ref[at], nv_ref[at] = _adamw_math(w_refs[p][at], m_refs[p][at], v_refs[p][at], g)

        every = slice(None)
        update(0, total(head_ref, slice(0, 1), every))
        update(1, total(tail_ref, slice(0, 1), every))
        update(2, total(tail_ref, slice(SLAB, SLAB + groups), slice(0, chunk)))
        update(3, total(tail_ref, slice(2 * SLAB, 2 * SLAB + 1), every))
        scalar_ref[...] = total(tail_ref, slice(3 * SLAB, 3 * SLAB + 1), slice(0, chunk))[:, 0:1]
        for grp in range(groups):
            update(4, total(tail_ref, slice(4 * SLAB, 4 * SLAB + chunk), slice(grp * chunk, (grp + 1) * chunk)), grp)

    vmem = pl.BlockSpec(memory_space=pltpu.VMEM)
    return pl.pallas_call(
        body, name="adamw_small", in_specs=[vmem] * (3 * n_par + 2), out_specs=[vmem] * (4 * n_par + 1),
        out_shape=[jax.ShapeDtypeStruct(w.shape, F32) for w in ws for _ in range(4)]
        + [jax.ShapeDtypeStruct((1, 1), F32)],
        compiler_params=pltpu.CompilerParams(vmem_limit_bytes=VMEM_LIMIT),
    )(*ws, *ms, *vs, slots_head, slots_tail)


ANY = pl.BlockSpec(memory_space=pl.ANY)


def _place():
    x, y, c = lax.axis_index("x"), lax.axis_index("y"), lax.axis_index("c")
    other_chips = [(1 - x, y), (x, 1 - y), (1 - x, 1 - y)]
    return x, y, c, other_chips


def _swap_and_gather(arrs, pack):
    n = len(arrs)
    n_dev = 2 * N_CHIP
    flips = [(dx, dy, dc) for dx in (0, 1) for dy in (0, 1) for dc in (0, 1)][1:]

    def body(*refs):
        ins, pack_ref = refs[:n], refs[n]
        outs, slots_ref = refs[n + 1:2 * n + 1], refs[2 * n + 1]
        send_sems, recv_sems, pack_send, pack_recv, own_sem = refs[2 * n + 2:]
        x, y, c, _ = _place()
        me = 4 * x + 2 * y + c
        own = pltpu.make_async_copy(pack_ref, slots_ref.at[me], own_sem)
        own.start()
        copies = []
        for k, (dx, dy, dc) in enumerate(flips):
            peer = (1 - x if dx else x, 1 - y if dy else y, 1 - c if dc else c)
            copies.append(pltpu.make_async_remote_copy(
                src_ref=pack_ref, dst_ref=slots_ref.at[me], send_sem=pack_send.at[k], recv_sem=pack_recv.at[k],
                device_id=peer, device_id_type=MESH))
        copies += [pltpu.make_async_remote_copy(
            src_ref=ins[a], dst_ref=outs[a], send_sem=send_sems.at[a], recv_sem=recv_sems.at[a],
            device_id=(x, y, 1 - c), device_id_type=MESH) for a in range(n)]
        for cp in copies:
            cp.start()
        for cp in copies:
            cp.wait()
        own.wait()

    return pl.pallas_call(
        body, name="swap_and_gather",
        in_specs=[ANY] * (n + 1), out_specs=[ANY] * (n + 1),
        out_shape=[jax.ShapeDtypeStruct(a.shape, a.dtype) for a in arrs] + [jax.ShapeDtypeStruct((n_dev,) + pack.shape, F32)],
        scratch_shapes=[pltpu.SemaphoreType.DMA((n,)), pltpu.SemaphoreType.DMA((n,)),
                        pltpu.SemaphoreType.DMA((n_dev - 1,)), pltpu.SemaphoreType.DMA((n_dev - 1,)),
                        pltpu.SemaphoreType.DMA],
    )(*arrs, pack)


SLAB = 8


def _slab(vec, d):
    return jnp.pad(vec.reshape(1, d), ((0, SLAB - 1), (0, 0)))


def _pack_tail(vec_nv, b_s, vec_nf, w_s, scalar):
    d = vec_nv.shape[-1]
    groups, chunk, _ = w_s.shape
    assert groups == SLAB and groups * chunk == d
    slabs = [_slab(vec_nv, d), jnp.pad(b_s, ((0, 0), (0, d - chunk))), _slab(vec_nf, d),
             jnp.pad(scalar, ((0, SLAB - 1), (0, d - 1)))]
    return jnp.concatenate(slabs + [jnp.swapaxes(w_s, 0, 1).reshape(chunk, d)], axis=0)


def kernel(x, norm_in, w_in, norm_v, w_s, b_s, w_o_gmlp, w_o_sb, w_out, norm_final, loss_target, m_norm_in, m_w_in, m_norm_v, m_w_s, m_b_s, m_w_o_gmlp, m_w_o_sb, m_w_out, m_norm_final, v_norm_in, v_w_in, v_norm_v, v_w_s, v_b_s, v_w_o_gmlp, v_w_o_sb, v_w_out, v_norm_final):
    d = x.shape[-1]
    ncol = w_in.shape[-1]
    nrow = w_o_gmlp.shape[-2]
    chip = 2 * lax.axis_index("x") + lax.axis_index("y")

    bsz, seq, _ = x.shape
    x2d = x.reshape(bsz * seq, d)
    shards = [w_in[0], w_o_gmlp[0], w_o_sb[0], w_out[0]]
    halves = [s16.reshape(2, s16.shape[0] // 2, s16.shape[1]) for s16 in _cast_bf16(shards)]
    proj, qkv, h_t, (wg, w_og, w_osb, w_o) = _in_proj_fwd(x2d, norm_in, halves)
    wg = wg.reshape(N_CHIP, d, ncol)

    loss, pieces, dx2, g_nv, g_ws, g_bs, g_wog, g_wosb, g_wout, g_nf = _local_step(
        proj, qkv, x2d, loss_target.reshape(bsz * seq, d), bsz, seq, norm_v, w_s[0], b_s[0],
        w_og.reshape(d, d), w_osb.reshape(d, d), w_o.reshape(d, d), norm_final)

    mats = [g_wog, g_wosb, g_wout]
    mats16 = [g16.reshape(N_CHIP, nrow, d) for g16 in _cast_bf16(mats)]
    g_win, from_sibling, recv_mats, slots_tail = _in_proj_bwd_w(
        h_t, pieces, mats16, _pack_tail(g_nv, g_bs, g_nf, g_ws, loss))
    core = lax.axis_index("c")
    half_own, half_win16 = _chip_half(g_win, from_sibling, core, chip, ncol)
    grad_x, g_nin, recv_win = _in_proj_bwd_x(pieces, wg, x2d, norm_in, dx2, half_win16)
    grad_x = grad_x.reshape(bsz, seq, d)

    sums = _add_received([half_own], [recv_win], jnp.zeros((), jnp.int32), True) + _add_received(
        mats, recv_mats, chip, False)
    *sibling_sums, slots_head = _swap_and_gather(sums, _slab(g_nin, d))
    stats = [_adamw_halves(w_in[0], m_w_in[0], v_w_in[0], sums[0], sibling_sums[0], core)] + _adamw_pairs(
        shards[1:], [m_w_o_gmlp[0], m_w_o_sb[0], m_w_out[0]], [v_w_o_gmlp[0], v_w_o_sb[0], v_w_out[0]],
        sums[1:], sibling_sums[1:])

    *small, loss = _adamw_small(
        [norm_in, norm_v, b_s[0], norm_final.reshape(1, d), w_s[0]],
        [m_norm_in, m_norm_v, m_b_s[0], m_norm_final.reshape(1, d), m_w_s[0]],
        [v_norm_in, v_norm_v, v_b_s[0], v_norm_final.reshape(1, d), v_w_s[0]], slots_head, slots_tail)

    out = []
    for kind, (win, wog, wosb, wout) in enumerate(zip(*stats)):
        nin, nv, bs, nf, ws = small[kind::4]
        out += [nin, win[None], nv, ws[None], bs[None], wog[None], wosb[None], wout[None], nf.reshape(d)]
    return (loss.reshape(()), grad_x, *out)
```

```python
import functools
import math

import jax
import jax.numpy as jnp
from jax import lax
from jax.experimental import pallas as pl
from jax.experimental.pallas import tpu as pltpu

F32 = jnp.float32
BF16 = jnp.bfloat16
EPS = 1e-6
HEADS = 8
N_SPLIT = 9
N_CHIP = 4
MESH = pl.DeviceIdType.MESH

ADAM_LR = 0.001
ADAM_B1 = 0.9
ADAM_B2 = 0.999
ADAM_EPS = 1e-08
ADAM_WD = 0.01
ADAM_STEP = 10

VMEM_LIMIT = 56 * 2 ** 20
TM = 256
TMF = 512
TMX = 512
ATT_T = 256
ATT_HP = 4
TKW = 1024
CHUNKS_PER_STEP = 4
CAST_STEPS = 8

NT = (((1,), (1,)), ((), ()))
TN = (((0,), (0,)), ((), ()))


def _params(sem):
    return pltpu.CompilerParams(dimension_semantics=sem, vmem_limit_bytes=VMEM_LIMIT)


def _resident(shape):
    nd = len(shape)
    return pl.BlockSpec(shape, lambda *_: (0,) * nd, pipeline_mode=pl.Buffered(1))


def _const(shape):
    nd = len(shape)
    return pl.BlockSpec(shape, lambda *_: (0,) * nd)


def _segments(d, ncol):
    segs = []
    edges = sorted({j * ncol for j in range(N_CHIP + 1)} | {s * d for s in range(N_SPLIT + 1)})
    for lo, hi in zip(edges[:-1], edges[1:]):
        segs.append((lo // ncol, lo % ncol, lo // d, lo % d, hi - lo))
    return segs


def _sigmoid(x):
    return 0.5 * jnp.tanh(0.5 * x) + 0.5


_GELU_C = math.sqrt(2.0 / math.pi)


_GELU_CA = _GELU_C * 0.044715


def _gelu(x):
    return x * (0.5 * jnp.tanh(x * (_GELU_C + _GELU_CA * (x * x))) + 0.5)


def _gelu_and_grad(x):
    x2 = x * x
    u = 0.5 * jnp.tanh(x * (_GELU_C + _GELU_CA * x2)) + 0.5
    slope = (1.0 - u) * (x * (_GELU_C + (3.0 * _GELU_CA) * x2))
    return x * u, u * (2.0 * slope + 1.0)


def _split_bf16(a):
    hi = a.astype(BF16)
    lo = (a - hi.astype(F32)).astype(BF16)
    return hi, lo


def _in_proj_fwd(x2d, g_in, halves):
    t, d = x2d.shape
    n = len(halves)
    ncol = halves[0].shape[2]
    n_row = t // TMF
    last = n_row - 1
    assert halves[0].shape[1] * 2 == d
    qkv_parts = {j: (max(j * ncol, 3 * d) - j * ncol, max(j * ncol, 3 * d) - 3 * d)
                 for j in range(N_CHIP) if min((j + 1) * ncol, 6 * d) > max(j * ncol, 3 * d)}
    qkv_w = 3 * d // len(qkv_parts)
    assert all(min((j + 1) * ncol, 6 * d) - max(j * ncol, 3 * d) == qkv_w and q0 % qkv_w == 0
               for j, (_, q0) in qkv_parts.items())

    def body(order_ref, qrow_ref, qcol_ref, x_ref, g_ref, *rest):
        ins = rest[:n]
        proj_ref, qkv_ref, ht_ref = rest[n:n + 3]
        outs = rest[n + 3:2 * n + 3]
        wbuf, h_all, send_sems, recv_sems, local_sems, load_sem = rest[2 * n + 3:]
        phase = pl.program_id(0)
        i = pl.program_id(1)
        x_pos, y_pos, c_pos, chips = _place()
        sibling = (x_pos, y_pos, 1 - c_pos)
        me = (x_pos, y_pos, c_pos)
        my_chip = 2 * x_pos + y_pos

        def copy(a, k, block, to, src=None):
            return pltpu.make_async_remote_copy(
                src_ref=outs[a].at[block] if src is None else src, dst_ref=outs[a].at[block],
                send_sem=send_sems.at[a, k], recv_sem=recv_sems.at[a, k], device_id=to, device_id_type=MESH)

        def local(a):
            return pltpu.make_async_copy(ins[a], outs[a].at[pl.ds(2 * my_chip, 2)], local_sems.at[a])

        def load(src, first, slot):
            for half in range(2):
                cp = pltpu.make_async_copy(src.at[first + half], wbuf.at[slot, pl.ds(half * (d // 2), d // 2)], load_sem)
                cp.start()
                cp.wait()

        def relay(a, k, block, piece, to):
            rows = halves[a].shape[1] // 2
            ref = outs[a].at[block, pl.ds(piece * rows, rows)]
            return pltpu.make_async_remote_copy(
                src_ref=ref, dst_ref=ref, send_sem=send_sems.at[a, k], recv_sem=recv_sems.at[a, k],
                device_id=to, device_id_type=MESH)

        x_nbr, y_nbr, diagonal = chips
        first_block = lambda chip_xy: 2 * (2 * chip_xy[0] + chip_xy[1])

        def neighbours_arrived(arrays):
            from_x, from_y = first_block(x_nbr) + c_pos, first_block(y_nbr) + c_pos
            for a in arrays:
                copy(a, 0, from_x, me).wait_recv()
                copy(a, 1, from_y, me).wait_recv()
                relay(a, 2, from_x, 0, (*y_nbr, c_pos)).start()
                relay(a, 3, from_y, 1, (*x_nbr, c_pos)).start()
                copy(a, 4, from_x, sibling).start()
                copy(a, 5, from_y, sibling).start()

        def diagonal_arrived(arrays):
            from_diagonal = first_block(diagonal) + c_pos
            for a in arrays:
                relay(a, 2, from_diagonal, 0, me).wait_recv()
                relay(a, 3, from_diagonal, 1, me).wait_recv()
                copy(a, 6, from_diagonal, sibling).start()

        def from_sibling(a, k, chip_xy):
            copy(a, k, first_block(chip_xy) + 1 - c_pos, me).wait_recv()

        @pl.when((phase == 0) & (i == 0))
        def _():
            for a in range(n):
                local(a).start()
            for k, (px, py) in enumerate((x_nbr, y_nbr)):
                for a in range(n):
                    copy(a, k, 2 * my_chip + c_pos, (px, py, c_pos), src=ins[a].at[c_pos]).start()
            load(ins[0], 0, 0)

        @pl.when((phase == 1) & (i == 0))
        def _():
            neighbours_arrived([0])
            from_sibling(0, 4, x_nbr)
            load(outs[0], first_block(x_nbr), 1)

        @pl.when((phase == 2) & (i == 0))
        def _():
            from_sibling(0, 5, y_nbr)
            load(outs[0], first_block(y_nbr), 0)
            neighbours_arrived(range(1, n))

        @pl.when((phase == 3) & (i == 0))
        def _():
            diagonal_arrived(range(n))
            from_sibling(0, 6, diagonal)
            load(outs[0], first_block(diagonal), 1)

        @pl.when(phase == 0)
        def _():
            x = x_ref[...]
            r = lax.rsqrt(jnp.mean(x * x, axis=-1, keepdims=True) + EPS)
            hf = x * r * g_ref[...]
            h_all[i] = hf.astype(BF16)
            ht_ref[...] = hf.T.astype(BF16)

        for slot in range(2):
            @pl.when(phase % 2 == slot)
            def _(slot=slot):
                proj_ref[...] = jnp.dot(h_all[i], wbuf[slot], preferred_element_type=F32)

        for chunk, (c0, _) in qkv_parts.items():
            @pl.when(order_ref[phase] == chunk)
            def _(c0=c0):
                qkv_ref[...] = proj_ref[:, c0:c0 + qkv_w].astype(BF16)

        @pl.when((phase == N_CHIP - 1) & (i == n_row - 1))
        def _():
            for a in range(1, n):
                for k, chip_xy in ((4, x_nbr), (5, y_nbr), (6, diagonal)):
                    from_sibling(a, k, chip_xy)
            for a in range(n):
                for k in (0, 1, 4, 5, 6):
                    copy(a, k, 0, me).wait_send()
                for k in (2, 3):
                    relay(a, k, 0, 0, me).wait_send()
                local(a).wait()

    x_pos, y_pos = lax.axis_index("x"), lax.axis_index("y")
    order = jnp.stack([2 * x_pos + y_pos, 2 * (1 - x_pos) + y_pos, 2 * x_pos + 1 - y_pos,
                       2 * (1 - x_pos) + 1 - y_pos]).astype(jnp.int32)
    holds = [functools.reduce(jnp.logical_or, [order[p] == j for j in qkv_parts]) for p in range(N_CHIP)]
    col = [sum(jnp.where(order[p] == j, q0 // qkv_w, 0) for j, (_, q0) in qkv_parts.items()) for p in range(N_CHIP)]
    cur = col[-1]
    for p in reversed(range(N_CHIP - 1)):
        cur = jnp.where(holds[p], col[p], cur)
    seen = jnp.bool_(False)
    qrow, qcol = [], []
    for p in range(N_CHIP):
        cur = jnp.where(holds[p], col[p], cur)
        qrow.append(jnp.where(holds[p], -1, jnp.where(seen, last, 0)))
        qcol.append(cur)
        seen = seen | holds[p]
    qrow = jnp.stack(qrow).astype(jnp.int32)
    qcol = jnp.stack(qcol).astype(jnp.int32)

    outs = pl.pallas_call(
        body, name="in_proj_fwd",
        grid_spec=pltpu.PrefetchScalarGridSpec(
            num_scalar_prefetch=3, grid=(N_CHIP, n_row),
            in_specs=[pl.BlockSpec((TMF, d), lambda p, i, order, qrow, qcol: (jnp.where(p == 0, i, last), 0)),
                      pl.BlockSpec((1, d), lambda p, i, order, qrow, qcol: (0, 0))] + [ANY] * n,
            out_specs=[pl.BlockSpec((TMF, ncol), lambda p, i, order, qrow, qcol: (i, order[p])),
                       pl.BlockSpec((TMF, qkv_w),
                                    lambda p, i, order, qrow, qcol: (jnp.where(qrow[p] < 0, i, qrow[p]), qcol[p])),
                       pl.BlockSpec((d, TMF), lambda p, i, order, qrow, qcol: (0, jnp.where(p == 0, i, last)))]
            + [ANY] * n,
            scratch_shapes=[pltpu.VMEM((2, d, ncol), BF16), pltpu.VMEM((n_row, TMF, d), BF16),
                            pltpu.SemaphoreType.DMA((n, 7)), pltpu.SemaphoreType.DMA((n, 7)),
                            pltpu.SemaphoreType.DMA((n,)), pltpu.SemaphoreType.DMA]),
        out_shape=[jax.ShapeDtypeStruct((t, N_CHIP * ncol), F32), jax.ShapeDtypeStruct((t, 3 * d), BF16),
                   jax.ShapeDtypeStruct((d, t), BF16)]
        + [jax.ShapeDtypeStruct((2 * N_CHIP,) + hv.shape[1:], BF16) for hv in halves],
        compiler_params=_params(("arbitrary", "arbitrary")),
    )(order, qrow, qcol, x2d, g_in, *halves)
    return outs[0], outs[1], outs[2], outs[3:]


def _branch_a_fwd(a_pre, g_v, wm, b_t):
    t = a_pre.shape[0]
    d = g_v.shape[1]
    d3 = 3 * d
    ng, chunk, _ = wm.shape
    cw = d // ng

    per_step = CHUNKS_PER_STEP if t % (CHUNKS_PER_STEP * chunk) == 0 else 1

    def body(a_ref, gv_ref, wm_ref, bt_ref, ya_ref):
        for n in range(per_step):
            rows = slice(n * chunk, (n + 1) * chunk)
            ua = _gelu(a_ref[rows, 0:d])
            vg = _gelu(a_ref[rows, d:2 * d])
            za = a_ref[rows, 2 * d:3 * d]
            rv = lax.rsqrt(jnp.mean(vg * vg, axis=-1, keepdims=True) + EPS)
            va = (vg * rv * gv_ref[...]).astype(BF16)
            gate = ua * (za * _sigmoid(za))
            for g in range(ng):
                sl = slice(g * cw, (g + 1) * cw)
                mixed = jnp.dot(wm_ref[g], va[:, sl], preferred_element_type=F32) + bt_ref[:, g:g + 1]
                ya_ref[rows, sl] = (gate[:, sl] * mixed).astype(BF16)

    tile = per_step * chunk
    return pl.pallas_call(
        body, name="branch_a_fwd",
        grid=(t // tile,),
        in_specs=[pl.BlockSpec((tile, d3), lambda i: (i, 0)), _const((1, d)), _const(wm.shape), _const(b_t.shape)],
        out_specs=pl.BlockSpec((tile, d), lambda i: (i, 0)),
        out_shape=jax.ShapeDtypeStruct((t, d), BF16),
        compiler_params=_params(("arbitrary",)),
    )(a_pre, g_v, wm, b_t)


def _branch_a_bwd(a_pre, dya, g_v, wm, wm_t, b_t):
    t = a_pre.shape[0]
    d = g_v.shape[1]
    d3 = 3 * d
    ng, chunk, _ = wm.shape
    cw = d // ng
    tile = CHUNKS_PER_STEP * chunk
    nsteps = t // tile

    def body(a_ref, dya_ref, gv_ref, wm_ref, wmt_ref, bt_ref, da_ref, gws_ref, gbt_ref, gnv_ref, db_acc):
        i = pl.program_id(0)

        @pl.when(i == 0)
        def _():
            gws_ref[...] = jnp.zeros_like(gws_ref)
            gnv_ref[...] = jnp.zeros_like(gnv_ref)
            db_acc[...] = jnp.zeros_like(db_acc)

        def one_chunk(c, carry):
            rows = pl.ds(pl.multiple_of(c * chunk, chunk), chunk)
            ua, dgelu_u = _gelu_and_grad(a_ref[rows, 0:d])
            vg, dgelu_v = _gelu_and_grad(a_ref[rows, d:2 * d])
            za = a_ref[rows, 2 * d:3 * d]
            sig = _sigmoid(za)
            sz = za * sig
            dsz = sig * (1.0 + za * (1.0 - sig))
            rv = lax.rsqrt(jnp.mean(vg * vg, axis=-1, keepdims=True) + EPS)
            nv = vg * rv
            gv = gv_ref[...]
            va = (nv * gv).astype(BF16)
            dya = dya_ref[rows, :]
            dmix = dya * ua * sz
            db_acc[...] += dmix
            dmix_b = dmix.astype(BF16)
            t_gate = dya * sz
            t_z = dya * ua * dsz
            dva_parts = []
            for g in range(ng):
                sl = slice(g * cw, (g + 1) * cw)
                mixed = jnp.dot(wm_ref[g], va[:, sl], preferred_element_type=F32) + bt_ref[:, g:g + 1]
                da_ref[rows, sl] = (t_gate[:, sl] * mixed * dgelu_u[:, sl]).astype(BF16)
                da_ref[rows, 2 * d + g * cw:2 * d + (g + 1) * cw] = (t_z[:, sl] * mixed).astype(BF16)
                gws_ref[g] += lax.dot_general(dmix_b[:, sl], va[:, sl], NT, preferred_element_type=F32)
                dva_parts.append(jnp.dot(wmt_ref[g], dmix_b[:, sl], preferred_element_type=F32))
            dva = jnp.concatenate(dva_parts, axis=1)
            gnv_ref[...] += jnp.sum(dva * nv, axis=0, keepdims=True)
            dnv = dva * gv
            dvg = rv * (dnv - nv * jnp.mean(dnv * nv, axis=-1, keepdims=True))
            da_ref[rows, d:2 * d] = (dvg * dgelu_v).astype(BF16)
            return carry

        lax.fori_loop(0, CHUNKS_PER_STEP, one_chunk, 0)

        @pl.when(i == nsteps - 1)
        def _():
            acc = db_acc[...]
            for g in range(ng):
                gbt_ref[:, g:g + 1] = jnp.sum(acc[:, g * cw:(g + 1) * cw], axis=1, keepdims=True)

    return pl.pallas_call(
        body, name="branch_a_bwd",
        grid=(nsteps,),
        in_specs=[pl.BlockSpec((tile, d3), lambda i: (i, 0)), pl.BlockSpec((tile, d), lambda i: (i, 0)),
                  _const((1, d)), _const(wm.shape), _const(wm_t.shape), _const(b_t.shape)],
        out_specs=[pl.BlockSpec((tile, d3), lambda i: (i, 0)), _const(wm.shape), _const(b_t.shape), _const((1, d))],
        out_shape=[jax.ShapeDtypeStruct((t, d3), BF16), jax.ShapeDtypeStruct(wm.shape, F32),
                   jax.ShapeDtypeStruct(b_t.shape, F32), jax.ShapeDtypeStruct((1, d), F32)],
        scratch_shapes=[pltpu.VMEM((chunk, d), F32)],
        compiler_params=_params(("arbitrary",)),
    )(a_pre, dya, g_v, wm, wm_t, b_t)


def _below_diagonal(n):
    return lax.broadcasted_iota(jnp.int32, (n, n), 0) > lax.broadcasted_iota(jnp.int32, (n, n), 1)


def _twice(tri):
    t = tri.astype(BF16)
    return jnp.concatenate([t, t], axis=0)


def _cumsum_mm(a, tri2):
    hi, lo = _split_bf16(a)
    return jnp.dot(jnp.concatenate([hi, lo], axis=1), tri2, preferred_element_type=F32)


LOG2E = 1.4426950408889634
_SIGN = 0x80000000


def _sb_block(q, k, scale, upper2, causal):
    z2 = lax.dot_general(q, k, NT, preferred_element_type=F32) * (scale * LOG2E)
    neg_abs = lax.bitcast_convert_type(lax.bitcast_convert_type(z2, jnp.uint32) | jnp.uint32(_SIGN), F32)
    l2 = jnp.log(1.0 + jnp.exp2(neg_abs)) * LOG2E
    log_beta = jnp.minimum(z2, 0.0) - l2
    lom = log_beta - z2
    if causal is not None:
        lom = jnp.where(causal, lom, 0.0)
    sx = _cumsum_mm(lom, upper2)
    return log_beta, sx, sx[:, 0:1] + lom[:, 0:1]


DEAD_LOG2 = -150.0


def _max_carry(carries):
    return jnp.max(functools.reduce(jnp.maximum, carries))


ZB_GROUP, GA_GROUP, GB_GROUP = 6, 7, 8


def _attn_specs(d, seq, nq, heads_per_step):
    hp_w = heads_per_step * (d // HEADS)
    n_hp = d // hp_w
    row_blk = lambda group: pl.BlockSpec((ATT_T, hp_w), lambda b, h, i: (b * nq + i, group * n_hp + h))
    seq_blk = lambda group: pl.BlockSpec((seq, hp_w), lambda b, h, i: (b, group * n_hp + h))
    return row_blk, seq_blk, n_hp


def _attn_fwd(qkv, proj, bsz, seq):
    t, d3 = qkv.shape
    d = d3 // 3
    hd = d // HEADS
    nq = seq // ATT_T
    scale = hd ** -0.5
    n_heads = ATT_HP
    _, seq_blk, n_hp = _attn_specs(d, seq, nq, n_heads)

    def body(q_ref, k_ref, v_ref, zb_ref, o_ref, yb_ref):
        causal = _below_diagonal(ATT_T)
        upper2 = _twice(causal)
        heads = [slice(h * hd, (h + 1) * hd) for h in range(n_heads)]

        def query_tile(i, done):
            mine = pl.ds(pl.multiple_of(i * ATT_T, ATT_T), ATT_T)

            def step(kb, state, mask):
                rows = pl.ds(pl.multiple_of(kb * ATT_T, ATT_T), ATT_T)
                scores = [_sb_block(q_ref[mine, cols], k_ref[rows, cols], scale, upper2, mask) for cols in heads]
                new = []
                for cols, (carry, acc), (log_beta, sx, total) in zip(heads, state, scores):
                    a = jnp.exp2(log_beta + sx + carry)
                    if mask is not None:
                        a = jnp.where(mask, a, 0.0)
                    acc = acc + jnp.dot(a.astype(BF16), v_ref[rows, cols], preferred_element_type=F32)
                    new.append((carry + total, acc))
                return tuple(new)

            init = tuple((jnp.zeros((ATT_T, 1), F32), jnp.zeros((ATT_T, hd), F32)) for _ in range(n_heads))
            state = step(i, init, causal)
            def more(c):
                new = step(c[0], c[1], None)
                return c[0] - 1, new, _max_carry([s[0] for s in new])

            _, state, _ = lax.while_loop(lambda c: (c[0] >= 0) & (c[2] > DEAD_LOG2), more,
                                         (i - 1, state, _max_carry([s[0] for s in state])))
            for cols, (_, acc) in zip(heads, state):
                zb = zb_ref[mine, cols]
                o_ref[mine, cols] = acc
                yb_ref[mine, cols] = (acc * (zb * _sigmoid(zb))).astype(BF16)
            return done

        lax.fori_loop(0, nq, query_tile, 0)

    return pl.pallas_call(
        body, name="attn_fwd",
        grid=(bsz, n_hp, 1),
        in_specs=[seq_blk(0), seq_blk(1), seq_blk(2), seq_blk(ZB_GROUP)],
        out_specs=[seq_blk(0), seq_blk(0)],
        out_shape=[jax.ShapeDtypeStruct((t, d), F32), jax.ShapeDtypeStruct((t, d), BF16)],
        compiler_params=_params(("arbitrary", "arbitrary", "arbitrary")),
    )(qkv, qkv, qkv, proj)


def _attn_bwd(qkv, proj, o, dyb, bsz, seq):
    t, d3 = qkv.shape
    d = d3 // 3
    hd = d // HEADS
    nq = seq // ATT_T
    scale = hd ** -0.5
    row_blk, seq_blk, n_hp = _attn_specs(d, seq, nq, ATT_HP)

    def body(q_ref, k_ref, v_ref, zb_ref, o_ref, dyb_ref, dq_ref, dk_ref, dv_ref, dzb_ref,
             g_s, beta_s, dkt_acc, dvt_acc):
        i = pl.program_id(2)

        @pl.when(i == 0)
        def _():
            dkt_acc[...] = jnp.zeros_like(dkt_acc)
            dvt_acc[...] = jnp.zeros_like(dvt_acc)

        causal = _below_diagonal(ATT_T)
        upper2 = _twice(causal)
        lower2 = _twice(~causal)
        zb = zb_ref[...]
        sig = _sigmoid(zb)
        dyb_t = dyb_ref[...]
        do_f = dyb_t * (zb * sig)
        do = do_f.astype(BF16)
        do_t = do_f.T.astype(BF16)
        q_t = q_ref[...].astype(F32).T.astype(BF16)
        dzb_ref[...] = (dyb_t * o_ref[...] * (sig * (1.0 + zb * (1.0 - sig)))).astype(BF16)

        def sweep(kb, carries, mask):
            rows = pl.ds(pl.multiple_of(kb * ATT_T, ATT_T), ATT_T)
            heads = [slice(h * hd, (h + 1) * hd) for h in range(ATT_HP)]
            scores = [_sb_block(q_ref[:, cols], k_ref[rows, cols], scale, upper2, mask) for cols in heads]
            das = [lax.dot_general(do[:, cols], v_ref[rows, cols], NT, preferred_element_type=F32) for cols in heads]
            new = []
            for h, (cols, carry, (log_beta, sx, total), da) in enumerate(zip(heads, carries, scores, das)):
                a = jnp.exp2(log_beta + sx + carry)
                beta = jnp.exp2(log_beta)
                if mask is not None:
                    a = jnp.where(mask, a, 0.0)
                    beta = jnp.where(mask, beta, 0.0)
                g_s[h, kb] = a * da
                beta_s[h, kb] = beta
                dvt_acc[kb, cols, :] += jnp.dot(do_t[cols, :], a.astype(BF16), preferred_element_type=F32)
                new.append(carry + total)
            return tuple(new)

        carries = sweep(i, tuple(jnp.zeros((ATT_T, 1), F32) for _ in range(ATT_HP)), causal)

        def more(c):
            new = sweep(c[0], c[1], None)
            return c[0] - 1, new, _max_carry(new)

        last, _, _ = lax.while_loop(lambda c: (c[0] >= 0) & (c[2] > DEAD_LOG2), more, (i - 1, carries, _max_carry(carries)))
        first_kb = last + 1

        def back(kb, state):
            rows = pl.ds(pl.multiple_of(kb * ATT_T, ATT_T), ATT_T)
            heads = [slice(h * hd, (h + 1) * hd) for h in range(ATT_HP)]
            sums = [_cumsum_mm(g_s[h, kb], lower2) for h in range(ATT_HP)]
            new = []
            for h, (cols, (p_carry, dq), px) in enumerate(zip(heads, state, sums)):
                dz = ((g_s[h, kb] - (p_carry + px) * beta_s[h, kb]) * scale).astype(BF16)
                dq = dq + jnp.dot(dz, k_ref[rows, cols], preferred_element_type=F32)
                dkt_acc[kb, cols, :] += jnp.dot(q_t[cols, :], dz, preferred_element_type=F32)
                new.append((p_carry + px[:, ATT_T - 1:ATT_T], dq))
            return tuple(new)

        init = tuple((jnp.zeros((ATT_T, 1), F32), jnp.zeros((ATT_T, hd), F32)) for _ in range(ATT_HP))
        state = lax.fori_loop(first_kb, i + 1, back, init)
        for h in range(ATT_HP):
            dq_ref[:, h * hd:(h + 1) * hd] = state[h][1].astype(BF16)

        @pl.when(i == nq - 1)
        def _():
            for kb in range(nq):
                dk_ref[kb * ATT_T:(kb + 1) * ATT_T, :] = dkt_acc[kb].T.astype(BF16)
                dv_ref[kb * ATT_T:(kb + 1) * ATT_T, :] = dvt_acc[kb].T.astype(BF16)

    out = jax.ShapeDtypeStruct((t, d), BF16)
    hp_w = ATT_HP * hd
    return pl.pallas_call(
        body, name="attn_bwd",
        grid=(bsz, n_hp, nq),
        in_specs=[row_blk(0), seq_blk(1), seq_blk(2), row_blk(ZB_GROUP), row_blk(0), row_blk(0)],
        out_specs=[row_blk(0), seq_blk(0), seq_blk(0), row_blk(0)],
        out_shape=[out, out, out, out],
        scratch_shapes=[pltpu.VMEM((ATT_HP, nq, ATT_T, ATT_T), F32), pltpu.VMEM((ATT_HP, nq, ATT_T, ATT_T), F32),
                        pltpu.VMEM((nq, hp_w, ATT_T), F32), pltpu.VMEM((nq, hp_w, ATT_T), F32)],
        compiler_params=_params(("arbitrary", "arbitrary", "arbitrary")),
    )(qkv, qkv, qkv, proj, o, dyb)


def _out_proj(ya, yb, g_pre, x2d, tgt, w_og, w_osb, w_out, g_f):
    t, d = x2d.shape

    def body(ya_ref, yb_ref, ga_ref, gb_ref, x_ref, tgt_ref, wog_ref, wosb_ref, wout_ref, gf_ref,
             dya_ref, dyb_ref, dg_ref, dx2_ref, loss_ref, gnf_ref, gwog_ref, gwosb_ref, gwout_ref):
        @pl.when(pl.program_id(0) == 0)
        def _():
            loss_ref[...] = jnp.zeros_like(loss_ref)
            gnf_ref[...] = jnp.zeros_like(gnf_ref)
            gwog_ref[...] = jnp.zeros_like(gwog_ref)
            gwosb_ref[...] = jnp.zeros_like(gwosb_ref)
            gwout_ref[...] = jnp.zeros_like(gwout_ref)

        ya = ya_ref[...]
        yb = yb_ref[...]
        pa = jnp.dot(ya, wog_ref[...], preferred_element_type=F32)
        pb = jnp.dot(yb, wosb_ref[...], preferred_element_type=F32)
        sga = _sigmoid(ga_ref[...])
        sgb = _sigmoid(gb_ref[...])
        merged = (sga * pa + sgb * pb).astype(BF16)
        x2 = x_ref[...] + jnp.dot(merged, wout_ref[...], preferred_element_type=F32)
        r2 = lax.rsqrt(jnp.mean(x2 * x2, axis=-1, keepdims=True) + EPS)
        n2 = x2 * r2
        gf = gf_ref[...]
        err = n2 * gf - tgt_ref[...]
        loss_ref[...] += 0.5 * jnp.sum(jnp.sum(err * err, axis=-1, keepdims=True), axis=0, keepdims=True) / d
        dy = err * (1.0 / d)
        gnf_ref[...] += jnp.sum(dy * n2, axis=0, keepdims=True)
        dn = dy * gf
        dx2 = r2 * (dn - n2 * jnp.mean(dn * n2, axis=-1, keepdims=True))
        dx2_ref[...] = dx2
        dx2_b = dx2.astype(BF16)
        dmerged = lax.dot_general(dx2_b, wout_ref[...], NT, preferred_element_type=F32)
        gwout_ref[...] += lax.dot_general(merged, dx2_b, TN, preferred_element_type=F32)
        dg_ref[:, 0:d] = (dmerged * pa * (sga * (1.0 - sga))).astype(BF16)
        dg_ref[:, d:2 * d] = (dmerged * pb * (sgb * (1.0 - sgb))).astype(BF16)
        dpa = (dmerged * sga).astype(BF16)
        dpb = (dmerged * sgb).astype(BF16)
        dya_ref[...] = lax.dot_general(dpa, wog_ref[...], NT, preferred_element_type=F32)
        dyb_ref[...] = lax.dot_general(dpb, wosb_ref[...], NT, preferred_element_type=F32)
        gwog_ref[...] += lax.dot_general(ya, dpa, TN, preferred_element_type=F32)
        gwosb_ref[...] += lax.dot_general(yb, dpb, TN, preferred_element_type=F32)

    row = lambda i: (i, 0)
    return pl.pallas_call(
        body, name="out_proj",
        grid=(t // TM,),
        in_specs=[pl.BlockSpec((TM, d), row), pl.BlockSpec((TM, d), row),
                  pl.BlockSpec((TM, d), lambda i: (i, GA_GROUP)), pl.BlockSpec((TM, d), lambda i: (i, GB_GROUP)),
                  pl.BlockSpec((TM, d), row), pl.BlockSpec((TM, d), row),
                  _resident((d, d)), _resident((d, d)), _resident((d, d)), _const((1, d))],
        out_specs=[pl.BlockSpec((TM, d), row), pl.BlockSpec((TM, d), row), pl.BlockSpec((TM, 2 * d), row),
                   pl.BlockSpec((TM, d), row), _const((1, 1)), _const((1, d)),
                   _const((d, d)), _const((d, d)), _const((d, d))],
        out_shape=[jax.ShapeDtypeStruct((t, d), F32), jax.ShapeDtypeStruct((t, d), F32),
                   jax.ShapeDtypeStruct((t, 2 * d), BF16), jax.ShapeDtypeStruct((t, d), F32),
                   jax.ShapeDtypeStruct((1, 1), F32), jax.ShapeDtypeStruct((1, d), F32),
                   jax.ShapeDtypeStruct((d, d), F32), jax.ShapeDtypeStruct((d, d), F32),
                   jax.ShapeDtypeStruct((d, d), F32)],
        compiler_params=_params(("arbitrary",)),
    )(ya, yb, g_pre, g_pre, x2d, tgt, w_og, w_osb, w_out, g_f)


def _dproj_pieces(d):
    return [(0, 0, 3), (1, 3, 1), (2, 4, 1), (3, 5, 1), (4, 6, 1), (5, 7, 2)]


def _in_proj_bwd_x(pieces, wg, x2d, g_in, dx2, gw16):
    t, d = x2d.shape
    ncol = wg.shape[2]
    segs = _segments(d, ncol)
    layout = _dproj_pieces(d)
    nsteps = t // TMX

    def body(da_ref, dq_ref, dk_ref, dv_ref, dzb_ref, dg_ref, w_ref, x_ref, g_ref, dx2_ref, gw16_ref,
             gx_ref, gn_ref, recv_ref, send_sems, recv_sems):
        x_pos, y_pos, c_pos, chips = _place()

        def share(k, chunk):
            px, py = chips[k]
            return pltpu.make_async_remote_copy(
                src_ref=gw16_ref.at[:, chunk * ncol:(chunk + 1) * ncol], dst_ref=recv_ref.at[k],
                send_sem=send_sems.at[k], recv_sem=recv_sems.at[k], device_id=(px, py, c_pos), device_id_type=MESH)

        @pl.when(pl.program_id(0) == 0)
        def _():
            gn_ref[...] = jnp.zeros_like(gn_ref)
            for k, (px, py) in enumerate(chips):
                for chunk in range(N_CHIP):
                    @pl.when(2 * px + py == chunk)
                    def _(k=k, chunk=chunk):
                        share(k, chunk).start()

        @pl.when(pl.program_id(0) == nsteps - 1)
        def _():
            for k in range(N_CHIP - 1):
                share(k, 0).wait()

        refs = (da_ref, dq_ref, dk_ref, dv_ref, dzb_ref, dg_ref)
        dh = jnp.zeros((TMX, d), F32)
        for chip, c0, grp, s0, width in segs:
            piece, first, _ = next(p for p in layout if p[1] <= grp < p[1] + p[2])
            off = (grp - first) * d + s0
            dh = dh + lax.dot_general(refs[piece][:, off:off + width], w_ref[chip, :, c0:c0 + width], NT,
                                      preferred_element_type=F32)
        x = x_ref[...]
        r = lax.rsqrt(jnp.mean(x * x, axis=-1, keepdims=True) + EPS)
        n = x * r
        gn_ref[...] += jnp.sum(dh * n, axis=0, keepdims=True)
        dn = dh * g_ref[...]
        gx_ref[...] = dx2_ref[...] + r * (dn - n * jnp.mean(dn * n, axis=-1, keepdims=True))

    row = lambda i: (i, 0)
    return pl.pallas_call(
        body, name="in_proj_bwd_x",
        grid=(nsteps,),
        in_specs=[pl.BlockSpec((TMX, p.shape[1]), row) for p in pieces]
        + [_resident(wg.shape), pl.BlockSpec((TMX, d), row), _const((1, d)), pl.BlockSpec((TMX, d), row), ANY],
        out_specs=[pl.BlockSpec((TMX, d), row), _const((1, d)), ANY],
        out_shape=[jax.ShapeDtypeStruct((t, d), F32), jax.ShapeDtypeStruct((1, d), F32),
                   jax.ShapeDtypeStruct((N_CHIP - 1, gw16.shape[0], ncol), BF16)],
        scratch_shapes=[pltpu.SemaphoreType.DMA((N_CHIP - 1,)), pltpu.SemaphoreType.DMA((N_CHIP - 1,))],
        compiler_params=_params(("arbitrary",)),
    )(*pieces, wg, x2d, g_in, dx2, gw16)


def _in_proj_bwd_w(h_t, pieces, mats16, pack):
    d, t = h_t.shape
    nk = t // TKW
    half = d // 2
    layout = _dproj_pieces(d)
    n_mats = len(mats16)
    n_dev = 2 * N_CHIP
    flips = [(dx, dy, dc) for dx in (0, 1) for dy in (0, 1) for dc in (0, 1)][1:]

    def body(ht_ref, da_ref, dq_ref, dk_ref, dv_ref, dzb_ref, dg_ref, *rest):
        mat_refs, pack_ref = rest[:n_mats], rest[n_mats]
        gw_ref, sib_ref = rest[n_mats + 1:n_mats + 3]
        recv_refs, slots_ref = rest[n_mats + 3:2 * n_mats + 3], rest[2 * n_mats + 3]
        acc, stage, mat_send, mat_recv, pack_send, pack_recv, own_sem, stage_send, sib_recv = rest[2 * n_mats + 4:]
        s = pl.program_id(0)
        i = pl.program_id(1)
        x_pos, y_pos, c_pos, chips = _place()
        me = 4 * x_pos + 2 * y_pos + c_pos

        def to_sibling(slot, group):
            return pltpu.make_async_remote_copy(
                src_ref=stage.at[slot], dst_ref=sib_ref.at[:, group * d:(group + 1) * d],
                send_sem=stage_send.at[slot], recv_sem=sib_recv, device_id=(x_pos, y_pos, 1 - c_pos), device_id_type=MESH)

        def exchanges():
            cps = []
            for k, (px, py) in enumerate(chips):
                for a in range(n_mats):
                    cps.append(pltpu.make_async_remote_copy(
                        src_ref=mat_refs[a].at[2 * px + py], dst_ref=recv_refs[a].at[k],
                        send_sem=mat_send.at[a, k], recv_sem=mat_recv.at[a, k],
                        device_id=(px, py, c_pos), device_id_type=MESH))
            for k, (dx, dy, dc) in enumerate(flips):
                peer = (1 - x_pos if dx else x_pos, 1 - y_pos if dy else y_pos, 1 - c_pos if dc else c_pos)
                cps.append(pltpu.make_async_remote_copy(
                    src_ref=pack_ref, dst_ref=slots_ref.at[me], send_sem=pack_send.at[k], recv_sem=pack_recv.at[k],
                    device_id=peer, device_id_type=MESH))
            return cps, pltpu.make_async_copy(pack_ref, slots_ref.at[me], own_sem)

        @pl.when((s == 0) & (i == 0))
        def _():
            cps, own = exchanges()
            own.start()
            for cp in cps:
                cp.start()

        @pl.when(i == 0)
        def _():
            acc[...] = jnp.zeros_like(acc)

        refs = (da_ref, dq_ref, dk_ref, dv_ref, dzb_ref, dg_ref)
        for piece, first, count in layout:
            @pl.when((s >= first) & (s < first + count))
            def _(piece=piece):
                acc[...] += jnp.dot(ht_ref[...], refs[piece][...], preferred_element_type=F32)

        @pl.when(i == nk - 1)
        def _():
            gw_ref[...] = acc[...]
            for slot in range(2):
                @pl.when(s % 2 == slot)
                def _(slot=slot):
                    @pl.when(s >= 2)
                    def _():
                        to_sibling(slot, 0).wait_send()
                    for other in range(2):
                        @pl.when(c_pos == 1 - other)
                        def _(other=other):
                            stage[slot] = acc[other * half:(other + 1) * half, :].astype(BF16)
                    for group in range(N_SPLIT):
                        @pl.when(s == group)
                        def _(group=group):
                            to_sibling(slot, group).start()

        @pl.when((s == N_SPLIT - 1) & (i == nk - 1))
        def _():
            for slot in range(2):
                to_sibling(slot, 0).wait_send()
            pltpu.make_async_remote_copy(
                src_ref=sib_ref, dst_ref=sib_ref, send_sem=stage_send.at[0], recv_sem=sib_recv,
                device_id=(x_pos, y_pos, c_pos), device_id_type=MESH).wait_recv()
            cps, own = exchanges()
            own.wait()
            for cp in cps:
                cp.wait()

    def piece_spec(p, first, count):
        def index(s, i):
            mine = (s >= first) & (s < first + count)
            return jnp.where(mine, i, 0), jnp.where(mine, s - first, 0)
        return pl.BlockSpec((TKW, d), index)

    col_blk = pl.BlockSpec((d, d), lambda s, i: (0, s))
    outs = pl.pallas_call(
        body, name="in_proj_bwd_w",
        grid=(N_SPLIT, nk),
        in_specs=[pl.BlockSpec((d, TKW), lambda s, i: (0, i))] + [piece_spec(*p) for p in layout] + [ANY] * (n_mats + 1),
        out_specs=[col_blk, ANY] + [ANY] * (n_mats + 1),
        out_shape=[jax.ShapeDtypeStruct((d, N_SPLIT * d), F32), jax.ShapeDtypeStruct((half, N_SPLIT * d), BF16)]
        + [jax.ShapeDtypeStruct((N_CHIP - 1,) + m.shape[1:], BF16) for m in mats16]
        + [jax.ShapeDtypeStruct((n_dev,) + pack.shape, F32)],
        scratch_shapes=[pltpu.VMEM((d, d), F32), pltpu.VMEM((2, half, d), BF16),
                        pltpu.SemaphoreType.DMA((n_mats, N_CHIP - 1)), pltpu.SemaphoreType.DMA((n_mats, N_CHIP - 1)),
                        pltpu.SemaphoreType.DMA((n_dev - 1,)), pltpu.SemaphoreType.DMA((n_dev - 1,)),
                        pltpu.SemaphoreType.DMA, pltpu.SemaphoreType.DMA((2,)), pltpu.SemaphoreType.DMA],
        compiler_params=_params(("arbitrary", "arbitrary")),
    )(h_t, *pieces, *mats16, pack)
    return outs[0], outs[1], outs[2:2 + n_mats], outs[2 + n_mats]


def _local_step(proj, qkv, x2d, tgt2d, bsz, seq, norm_v, w_s, b_s, w_og, w_osb, w_out, norm_final):
    d = x2d.shape[1]
    chunk = w_s.shape[-1]
    causal = jnp.tril(jnp.ones((chunk, chunk), dtype=bool))
    wm = jnp.where(causal[None], w_s, 0.0).astype(BF16)
    wm_t = jnp.swapaxes(wm, 1, 2)
    b_t = b_s.T

    ya = _branch_a_fwd(proj, norm_v, wm, b_t)
    o, yb = _attn_fwd(qkv, proj, bsz, seq)
    dya, dyb, dg, dx2, loss, g_nf, g_wog, g_wosb, g_wout = _out_proj(
        ya, yb, proj, x2d, tgt2d, w_og, w_osb, w_out, norm_final.reshape(1, d))
    dq, dk, dv, dzb = _attn_bwd(qkv, proj, o, dyb, bsz, seq)
    d_a, g_ws, g_bt, g_nv = _branch_a_bwd(proj, dya, norm_v, wm, wm_t, b_t)
    g_ws = jnp.where(causal[None], g_ws, 0.0)
    return loss, (d_a, dq, dk, dv, dzb, dg), dx2, g_nv, g_ws, g_bt.T, g_wog, g_wosb, g_wout, g_nf


def _row_tile(rows):
    return next(r for r in (128, 64, 32, 16, 8) if rows % r == 0)


def _cast_bf16(arrs):
    n = len(arrs)

    def body(*refs):
        for a_ref, o_ref in zip(refs[:n], refs[n:]):
            o_ref[...] = a_ref[...].astype(BF16)

    specs = [pl.BlockSpec((a.shape[0] // CAST_STEPS, a.shape[1]), lambda i: (i, 0)) for a in arrs]
    return pl.pallas_call(
        body, name="cast_bf16", grid=(CAST_STEPS,),
        in_specs=specs, out_specs=specs,
        out_shape=[jax.ShapeDtypeStruct(a.shape, BF16) for a in arrs],
        compiler_params=_params(("arbitrary",)),
    )(*arrs)


def _chip_half(full, from_sibling, core, chip, tile):
    half, n = from_sibling.shape

    def body(where_ref, own_ref, sib_ref, o32_ref, o16_ref):
        total = own_ref[...] + sib_ref[...].astype(F32)
        o16_ref[...] = total.astype(BF16)

        @pl.when(pl.program_id(0) == where_ref[1])
        def _():
            o32_ref[...] = total

    blk = pl.BlockSpec((half, tile), lambda j, where_ref: (0, j))
    return pl.pallas_call(
        body, name="chip_half",
        grid_spec=pltpu.PrefetchScalarGridSpec(
            num_scalar_prefetch=1, grid=(n // tile,),
            in_specs=[pl.BlockSpec((half, tile), lambda j, where_ref: (where_ref[0], j)), blk],
            out_specs=[pl.BlockSpec((half, tile), lambda j, where_ref: (0, 0)), blk]),
        out_shape=[jax.ShapeDtypeStruct((half, tile), F32), jax.ShapeDtypeStruct((half, n), BF16)],
        compiler_params=_params(("arbitrary",)),
    )(jnp.stack([core, chip]).astype(jnp.int32), full, from_sibling)


def _add_received(fulls, recvs, chip, by_cols):
    n = len(fulls)
    _, rows, cols = recvs[0].shape
    tr = _row_tile(rows)
    nb = rows // tr

    def body(chip_ref, *refs):
        for own_ref, recv_ref, o_ref in zip(refs[:n], refs[n:2 * n], refs[2 * n:]):
            s = own_ref[...]
            for k in range(N_CHIP - 1):
                s = s + recv_ref[k].astype(F32)
            o_ref[...] = s

    own_map = (lambda i, chip_ref: (i, chip_ref[0])) if by_cols else (lambda i, chip_ref: (chip_ref[0] * nb + i, 0))
    return pl.pallas_call(
        body, name="add_received",
        grid_spec=pltpu.PrefetchScalarGridSpec(
            num_scalar_prefetch=1, grid=(nb,),
            in_specs=[pl.BlockSpec((tr, cols), own_map)] * n
            + [pl.BlockSpec((N_CHIP - 1, tr, cols), lambda i, chip_ref: (0, i, 0))] * n,
            out_specs=[pl.BlockSpec((tr, cols), lambda i, chip_ref: (i, 0))] * n),
        out_shape=[jax.ShapeDtypeStruct((rows, cols), F32)] * n,
        compiler_params=_params(("arbitrary",)),
    )(chip.reshape(1).astype(jnp.int32), *fulls, *recvs)


def _adamw_math(w, m, v, g):
    new_m = ADAM_B1 * m + (1.0 - ADAM_B1) * g
    new_v = ADAM_B2 * v + (1.0 - ADAM_B2) * (g * g)
    m_hat = new_m / (1.0 - ADAM_B1 ** ADAM_STEP)
    v_hat = new_v / (1.0 - ADAM_B2 ** ADAM_STEP)
    return -ADAM_LR * (m_hat / (jnp.sqrt(v_hat) + ADAM_EPS) + ADAM_WD * w), new_m, new_v


def _adamw_pairs(ws, ms, vs, mines, theirs):
    n = len(ws)
    rows, cols = ws[0].shape
    tr = _row_tile(rows)

    def body(*refs):
        ins, outs = refs[:5 * n], refs[5 * n:]
        for a in range(n):
            w_ref, m_ref, v_ref, mine_ref, theirs_ref = ins[a::n]
            g_ref, d_ref, nm_ref, nv_ref = outs[4 * a:4 * a + 4]
            g = mine_ref[...] + theirs_ref[...]
            g_ref[...] = g
            d_ref[...], nm_ref[...], nv_ref[...] = _adamw_math(w_ref[...], m_ref[...], v_ref[...], g)

    spec = pl.BlockSpec((tr, cols), lambda i: (i, 0))
    out = jax.ShapeDtypeStruct((rows, cols), F32)
    outs = pl.pallas_call(
        body, name="adamw_pairs", grid=(rows // tr,),
        in_specs=[spec] * (5 * n), out_specs=[spec] * (4 * n), out_shape=[out] * (4 * n),
        compiler_params=_params(("arbitrary",)),
    )(*ws, *ms, *vs, *mines, *theirs)
    return [outs[4 * a:4 * a + 4] for a in range(n)]


def _adamw_halves(w, m, v, mine, theirs, core):
    rows, cols = w.shape
    tr = _row_tile(rows // 2)
    per_half = rows // 2 // tr

    def body(core_ref, w_ref, m_ref, v_ref, mine_ref, theirs_ref, g_ref, d_ref, nm_ref, nv_ref):
        is_mine = pl.program_id(0) // per_half == core_ref[0]
        for part, cond in ((mine_ref, is_mine), (theirs_ref, jnp.logical_not(is_mine))):
            @pl.when(cond)
            def _(part=part):
                g = part[...]
                g_ref[...] = g
                d_ref[...], nm_ref[...], nv_ref[...] = _adamw_math(w_ref[...], m_ref[...], v_ref[...], g)

    spec = pl.BlockSpec((tr, cols), lambda i, core_ref: (i, 0))

    def half_spec(own):
        def index(i, core_ref):
            in_core_half = i // per_half == core_ref[0]
            here = in_core_half if own else jnp.logical_not(in_core_half)
            return jnp.where(here, i % per_half, 0), 0
        return pl.BlockSpec((tr, cols), index)

    out = jax.ShapeDtypeStruct(w.shape, F32)
    return pl.pallas_call(
        body, name="adamw_halves",
        grid_spec=pltpu.PrefetchScalarGridSpec(
            num_scalar_prefetch=1, grid=(rows // tr,),
            in_specs=[spec] * 3 + [half_spec(True), half_spec(False)], out_specs=[spec] * 4),
        out_shape=[out] * 4,
        compiler_params=_params(("arbitrary",)),
    )(core.reshape(1).astype(jnp.int32), w, m, v, mine, theirs)


def _adamw_small(ws, ms, vs, slots_head, slots_tail):
    n_dev = slots_head.shape[0]
    n_par = len(ws)
    groups, chunk, _ = ws[4].shape

    def body(*refs):
        w_refs, m_refs, v_refs = refs[:n_par], refs[n_par:2 * n_par], refs[2 * n_par:3 * n_par]
        head_ref, tail_ref = refs[3 * n_par:3 * n_par + 2]
        out_refs, scalar_ref = refs[3 * n_par + 2:-1], refs[-1]

        def total(ref, rows, cols):
            g = ref[0, rows, cols]
            for i in range(1, n_dev):
                g = g + ref[i, rows, cols]
            return g

        def update(p, g, at=...):
            g_ref, d_ref, nm_ref, nv_ref = out_refs[4 * p:4 * p + 4]
            g_ref[at] = g
            d_ref[at], nm_ref[at], nv_ref[at] = _adamw_math(w_refs[p][at], m_refs[p][at], v_refs[p][at], g)

        every = slice(None)
        update(0, total(head_ref, slice(0, 1), every))
        update(1, total(tail_ref, slice(0, 1), every))
        update(2, total(tail_ref, slice(SLAB, SLAB + groups), slice(0, chunk)))
        update(3, total(tail_ref, slice(2 * SLAB, 2 * SLAB + 1), every))
        scalar_ref[...] = total(tail_ref, slice(3 * SLAB, 3 * SLAB + 1), slice(0, chunk))[:, 0:1]
        for grp in range(groups):
            update(4, total(tail_ref, slice(4 * SLAB, 4 * SLAB + chunk), slice(grp * chunk, (grp + 1) * chunk)), grp)

    vmem = pl.BlockSpec(memory_space=pltpu.VMEM)
    return pl.pallas_call(
        body, name="adamw_small", in_specs=[vmem] * (3 * n_par + 2), out_specs=[vmem] * (4 * n_par + 1),
        out_shape=[jax.ShapeDtypeStruct(w.shape, F32) for w in ws for _ in range(4)]
        + [jax.ShapeDtypeStruct((1, 1), F32)],
        compiler_params=pltpu.CompilerParams(vmem_limit_bytes=VMEM_LIMIT),
    )(*ws, *ms, *vs, slots_head, slots_tail)


ANY = pl.BlockSpec(memory_space=pl.ANY)


def _place():
    x, y, c = lax.axis_index("x"), lax.axis_index("y"), lax.axis_index("c")
    other_chips = [(1 - x, y), (x, 1 - y), (1 - x, 1 - y)]
    return x, y, c, other_chips


def _swap_and_gather(arrs, pack):
    n = len(arrs)
    n_dev = 2 * N_CHIP
    flips = [(dx, dy, dc) for dx in (0, 1) for dy in (0, 1) for dc in (0, 1)][1:]

    def body(*refs):
        ins, pack_ref = refs[:n], refs[n]
        outs, slots_ref = refs[n + 1:2 * n + 1], refs[2 * n + 1]
        send_sems, recv_sems, pack_send, pack_recv, own_sem = refs[2 * n + 2:]
        x, y, c, _ = _place()
        me = 4 * x + 2 * y + c
        own = pltpu.make_async_copy(pack_ref, slots_ref.at[me], own_sem)
        own.start()
        copies = []
        for k, (dx, dy, dc) in enumerate(flips):
            peer = (1 - x if dx else x, 1 - y if dy else y, 1 - c if dc else c)
            copies.append(pltpu.make_async_remote_copy(
                src_ref=pack_ref, dst_ref=slots_ref.at[me], send_sem=pack_send.at[k], recv_sem=pack_recv.at[k],
                device_id=peer, device_id_type=MESH))
        copies += [pltpu.make_async_remote_copy(
            src_ref=ins[a], dst_ref=outs[a], send_sem=send_sems.at[a], recv_sem=recv_sems.at[a],
            device_id=(x, y, 1 - c), device_id_type=MESH) for a in range(n)]
        for cp in copies:
            cp.start()
        for cp in copies:
            cp.wait()
        own.wait()

    return pl.pallas_call(
        body, name="swap_and_gather",
        in_specs=[ANY] * (n + 1), out_specs=[ANY] * (n + 1),
        out_shape=[jax.ShapeDtypeStruct(a.shape, a.dtype) for a in arrs] + [jax.ShapeDtypeStruct((n_dev,) + pack.shape, F32)],
        scratch_shapes=[pltpu.SemaphoreType.DMA((n,)), pltpu.SemaphoreType.DMA((n,)),
                        pltpu.SemaphoreType.DMA((n_dev - 1,)), pltpu.SemaphoreType.DMA((n_dev - 1,)),
                        pltpu.SemaphoreType.DMA],
    )(*arrs, pack)


SLAB = 8


def _slab(vec, d):
    return jnp.pad(vec.reshape(1, d), ((0, SLAB - 1), (0, 0)))


def _pack_tail(vec_nv, b_s, vec_nf, w_s, scalar):
    d = vec_nv.shape[-1]
    groups, chunk, _ = w_s.shape
    assert groups == SLAB and groups * chunk == d
    slabs = [_slab(vec_nv, d), jnp.pad(b_s, ((0, 0), (0, d - chunk))), _slab(vec_nf, d),
             jnp.pad(scalar, ((0, SLAB - 1), (0, d - 1)))]
    return jnp.concatenate(slabs + [jnp.swapaxes(w_s, 0, 1).reshape(chunk, d)], axis=0)


def kernel(x, norm_in, w_in, norm_v, w_s, b_s, w_o_gmlp, w_o_sb, w_out, norm_final, loss_target, m_norm_in, m_w_in, m_norm_v, m_w_s, m_b_s, m_w_o_gmlp, m_w_o_sb, m_w_out, m_norm_final, v_norm_in, v_w_in, v_norm_v, v_w_s, v_b_s, v_w_o_gmlp, v_w_o_sb, v_w_out, v_norm_final):
    d = x.shape[-1]
    ncol = w_in.shape[-1]
    nrow = w_o_gmlp.shape[-2]
    chip = 2 * lax.axis_index("x") + lax.axis_index("y")

    bsz, seq, _ = x.shape
    x2d = x.reshape(bsz * seq, d)
    shards = [w_in[0], w_o_gmlp[0], w_o_sb[0], w_out[0]]
    halves = [s16.reshape(2, s16.shape[0] // 2, s16.shape[1]) for s16 in _cast_bf16(shards)]
    proj, qkv, h_t, (wg, w_og, w_osb, w_o) = _in_proj_fwd(x2d, norm_in, halves)
    wg = wg.reshape(N_CHIP, d, ncol)

    loss, pieces, dx2, g_nv, g_ws, g_bs, g_wog, g_wosb, g_wout, g_nf = _local_step(
        proj, qkv, x2d, loss_target.reshape(bsz * seq, d), bsz, seq, norm_v, w_s[0], b_s[0],
        w_og.reshape(d, d), w_osb.reshape(d, d), w_o.reshape(d, d), norm_final)

    mats = [g_wog, g_wosb, g_wout]
    mats16 = [g16.reshape(N_CHIP, nrow, d) for g16 in _cast_bf16(mats)]
    g_win, from_sibling, recv_mats, slots_tail = _in_proj_bwd_w(
        h_t, pieces, mats16, _pack_tail(g_nv, g_bs, g_nf, g_ws, loss))
    core = lax.axis_index("c")
    half_own, half_win16 = _chip_half(g_win, from_sibling, core, chip, ncol)
    grad_x, g_nin, recv_win = _in_proj_bwd_x(pieces, wg, x2d, norm_in, dx2, half_win16)
    grad_x = grad_x.reshape(bsz, seq, d)

    sums = _add_received([half_own], [recv_win], jnp.zeros((), jnp.int32), True) + _add_received(
        mats, recv_mats, chip, False)
    *sibling_sums, slots_head = _swap_and_gather(sums, _slab(g_nin, d))
    stats = [_adamw_halves(w_in[0], m_w_in[0], v_w_in[0], sums[0], sibling_sums[0], core)] + _adamw_pairs(
        shards[1:], [m_w_o_gmlp[0], m_w_o_sb[0], m_w_out[0]], [v_w_o_gmlp[0], v_w_o_sb[0], v_w_out[0]],
        sums[1:], sibling_sums[1:])

    *small, loss = _adamw_small(
        [norm_in, norm_v, b_s[0], norm_final.reshape(1, d), w_s[0]],
        [m_norm_in, m_norm_v, m_b_s[0], m_norm_final.reshape(1, d), m_w_s[0]],
        [v_norm_in, v_norm_v, v_b_s[0], v_norm_final.reshape(1, d), v_w_s[0]], slots_head, slots_tail)

    out = []
    for kind, (win, wog, wosb, wout) in enumerate(zip(*stats)):
        nin, nv, bs, nf, ws = small[kind::4]
        out += [nin, win[None], nv, ws[None], bs[None], wog[None], wosb[None], wout[None], nf.reshape(d)]
    return (loss.reshape(()), grad_x, *out)
```

```python
import functools
import math

import jax
import jax.numpy as jnp
from jax import lax
from jax.experimental import pallas as pl
from jax.experimental.pallas import tpu as pltpu

F32 = jnp.float32
BF16 = jnp.bfloat16
EPS = 1e-6
HEADS = 8
N_SPLIT = 9
N_CHIP = 4
MESH = pl.DeviceIdType.MESH

ADAM_LR = 0.001
ADAM_B1 = 0.9
ADAM_B2 = 0.999
ADAM_EPS = 1e-08
ADAM_WD = 0.01
ADAM_STEP = 10

VMEM_LIMIT = 56 * 2 ** 20
TM = 256
TMF = 512
TMX = 512
ATT_T = 256
ATT_HP = 4
TKW = 1024
CHUNKS_PER_STEP = 4
CAST_STEPS = 8

NT = (((1,), (1,)), ((), ()))
TN = (((0,), (0,)), ((), ()))


def _params(sem):
    return pltpu.CompilerParams(dimension_semantics=sem, vmem_limit_bytes=VMEM_LIMIT)


def _resident(shape):
    nd = len(shape)
    return pl.BlockSpec(shape, lambda *_: (0,) * nd, pipeline_mode=pl.Buffered(1))


def _const(shape):
    nd = len(shape)
    return pl.BlockSpec(shape, lambda *_: (0,) * nd)


def _segments(d, ncol):
    segs = []
    edges = sorted({j * ncol for j in range(N_CHIP + 1)} | {s * d for s in range(N_SPLIT + 1)})
    for lo, hi in zip(edges[:-1], edges[1:]):
        segs.append((lo // ncol, lo % ncol, lo // d, lo % d, hi - lo))
    return segs


def _sigmoid(x):
    return 0.5 * jnp.tanh(0.5 * x) + 0.5


_GELU_C = math.sqrt(2.0 / math.pi)


_GELU_CA = _GELU_C * 0.044715


def _gelu(x):
    return x * (0.5 * jnp.tanh(x * (_GELU_C + _GELU_CA * (x * x))) + 0.5)


def _gelu_and_grad(x):
    x2 = x * x
    u = 0.5 * jnp.tanh(x * (_GELU_C + _GELU_CA * x2)) + 0.5
    slope = (1.0 - u) * (x * (_GELU_C + (3.0 * _GELU_CA) * x2))
    return x * u, u * (2.0 * slope + 1.0)


def _split_bf16(a):
    hi = a.astype(BF16)
    lo = (a - hi.astype(F32)).astype(BF16)
    return hi, lo


def _in_proj_fwd(x2d, g_in, halves):
    t, d = x2d.shape
    n = len(halves)
    ncol = halves[0].shape[2]
    n_row = t // TMF
    last = n_row - 1
    assert halves[0].shape[1] * 2 == d
    qkv_parts = {j: (max(j * ncol, 3 * d) - j * ncol, max(j * ncol, 3 * d) - 3 * d)
                 for j in range(N_CHIP) if min((j + 1) * ncol, 6 * d) > max(j * ncol, 3 * d)}
    qkv_w = 3 * d // len(qkv_parts)
    assert all(min((j + 1) * ncol, 6 * d) - max(j * ncol, 3 * d) == qkv_w and q0 % qkv_w == 0
               for j, (_, q0) in qkv_parts.items())

    def body(order_ref, qrow_ref, qcol_ref, x_ref, g_ref, *rest):
        ins = rest[:n]
        proj_ref, qkv_ref, ht_ref = rest[n:n + 3]
        outs = rest[n + 3:2 * n + 3]
        wbuf, h_all, send_sems, recv_sems, local_sems, load_sem = rest[2 * n + 3:]
        phase = pl.program_id(0)
        i = pl.program_id(1)
        x_pos, y_pos, c_pos, chips = _place()
        sibling = (x_pos, y_pos, 1 - c_pos)
        me = (x_pos, y_pos, c_pos)
        my_chip = 2 * x_pos + y_pos

        def copy(a, k, block, to, src=None):
            return pltpu.make_async_remote_copy(
                src_ref=outs[a].at[block] if src is None else src, dst_ref=outs[a].at[block],
                send_sem=send_sems.at[a, k], recv_sem=recv_sems.at[a, k], device_id=to, device_id_type=MESH)

        def local(a):
            return pltpu.make_async_copy(ins[a], outs[a].at[pl.ds(2 * my_chip, 2)], local_sems.at[a])

        def load(src, first, slot):
            for half in range(2):
                cp = pltpu.make_async_copy(src.at[first + half], wbuf.at[slot, pl.ds(half * (d // 2), d // 2)], load_sem)
                cp.start()
                cp.wait()

        def relay(a, k, block, piece, to):
            rows = halves[a].shape[1] // 2
            ref = outs[a].at[block, pl.ds(piece * rows, rows)]
            return pltpu.make_async_remote_copy(
                src_ref=ref, dst_ref=ref, send_sem=send_sems.at[a, k], recv_sem=recv_sems.at[a, k],
                device_id=to, device_id_type=MESH)

        x_nbr, y_nbr, diagonal = chips
        first_block = lambda chip_xy: 2 * (2 * chip_xy[0] + chip_xy[1])

        def neighbours_arrived(arrays):
            from_x, from_y = first_block(x_nbr) + c_pos, first_block(y_nbr) + c_pos
            for a in arrays:
                copy(a, 0, from_x, me).wait_recv()
                copy(a, 1, from_y, me).wait_recv()
                relay(a, 2, from_x, 0, (*y_nbr, c_pos)).start()
                relay(a, 3, from_y, 1, (*x_nbr, c_pos)).start()
                copy(a, 4, from_x, sibling).start()
                copy(a, 5, from_y, sibling).start()

        def diagonal_arrived(arrays):
            from_diagonal = first_block(diagonal) + c_pos
            for a in arrays:
                relay(a, 2, from_diagonal, 0, me).wait_recv()
                relay(a, 3, from_diagonal, 1, me).wait_recv()
                copy(a, 6, from_diagonal, sibling).start()

        def from_sibling(a, k, chip_xy):
            copy(a, k, first_block(chip_xy) + 1 - c_pos, me).wait_recv()

        @pl.when((phase == 0) & (i == 0))
        def _():
            for a in range(n):
                local(a).start()
            for k, (px, py) in enumerate((x_nbr, y_nbr)):
                for a in range(n):
                    copy(a, k, 2 * my_chip + c_pos, (px, py, c_pos), src=ins[a].at[c_pos]).start()
            load(ins[0], 0, 0)

        @pl.when((phase == 1) & (i == 0))
        def _():
            neighbours_arrived([0])
            from_sibling(0, 4, x_nbr)
            load(outs[0], first_block(x_nbr), 1)

        @pl.when((phase == 2) & (i == 0))
        def _():
            from_sibling(0, 5, y_nbr)
            load(outs[0], first_block(y_nbr), 0)
            neighbours_arrived(range(1, n))

        @pl.when((phase == 3) & (i == 0))
        def _():
            diagonal_arrived(range(n))
            from_sibling(0, 6, diagonal)
            load(outs[0], first_block(diagonal), 1)

        @pl.when(phase == 0)
        def _():
            x = x_ref[...]
            r = lax.rsqrt(jnp.mean(x * x, axis=-1, keepdims=True) + EPS)
            hf = x * r * g_ref[...]
            h_all[i] = hf.astype(BF16)
            ht_ref[0] = hf.T.astype(BF16)

        for slot in range(2):
            @pl.when(phase % 2 == slot)
            def _(slot=slot):
                proj_ref[...] = jnp.dot(h_all[i], wbuf[slot], preferred_element_type=F32)

        for chunk, (c0, _) in qkv_parts.items():
            @pl.when(order_ref[phase] == chunk)
            def _(c0=c0):
                qkv_ref[...] = proj_ref[:, c0:c0 + qkv_w].astype(BF16)

        @pl.when((phase == N_CHIP - 1) & (i == n_row - 1))
        def _():
            for a in range(1, n):
                for k, chip_xy in ((4, x_nbr), (5, y_nbr), (6, diagonal)):
                    from_sibling(a, k, chip_xy)
            for a in range(n):
                for k in (0, 1, 4, 5, 6):
                    copy(a, k, 0, me).wait_send()
                for k in (2, 3):
                    relay(a, k, 0, 0, me).wait_send()
                local(a).wait()

    x_pos, y_pos = lax.axis_index("x"), lax.axis_index("y")
    order = jnp.stack([2 * x_pos + y_pos, 2 * (1 - x_pos) + y_pos, 2 * x_pos + 1 - y_pos,
                       2 * (1 - x_pos) + 1 - y_pos]).astype(jnp.int32)
    holds = [functools.reduce(jnp.logical_or, [order[p] == j for j in qkv_parts]) for p in range(N_CHIP)]
    col = [sum(jnp.where(order[p] == j, q0 // qkv_w, 0) for j, (_, q0) in qkv_parts.items()) for p in range(N_CHIP)]
    cur = col[-1]
    for p in reversed(range(N_CHIP - 1)):
        cur = jnp.where(holds[p], col[p], cur)
    seen = jnp.bool_(False)
    qrow, qcol = [], []
    for p in range(N_CHIP):
        cur = jnp.where(holds[p], col[p], cur)
        qrow.append(jnp.where(holds[p], -1, jnp.where(seen, last, 0)))
        qcol.append(cur)
        seen = seen | holds[p]
    qrow = jnp.stack(qrow).astype(jnp.int32)
    qcol = jnp.stack(qcol).astype(jnp.int32)

    outs = pl.pallas_call(
        body, name="in_proj_fwd",
        grid_spec=pltpu.PrefetchScalarGridSpec(
            num_scalar_prefetch=3, grid=(N_CHIP, n_row),
            in_specs=[pl.BlockSpec((TMF, d), lambda p, i, order, qrow, qcol: (jnp.where(p == 0, i, last), 0)),
                      pl.BlockSpec((1, d), lambda p, i, order, qrow, qcol: (0, 0))] + [ANY] * n,
            out_specs=[pl.BlockSpec((TMF, ncol), lambda p, i, order, qrow, qcol: (i, order[p])),
                       pl.BlockSpec((TMF, qkv_w),
                                    lambda p, i, order, qrow, qcol: (jnp.where(qrow[p] < 0, i, qrow[p]), qcol[p])),
                       pl.BlockSpec((1, d, TMF), lambda p, i, order, qrow, qcol: (jnp.where(p == 0, i, last), 0, 0))]
            + [ANY] * n,
            scratch_shapes=[pltpu.VMEM((2, d, ncol), BF16), pltpu.VMEM((n_row, TMF, d), BF16),
                            pltpu.SemaphoreType.DMA((n, 7)), pltpu.SemaphoreType.DMA((n, 7)),
                            pltpu.SemaphoreType.DMA((n,)), pltpu.SemaphoreType.DMA]),
        out_shape=[jax.ShapeDtypeStruct((t, N_CHIP * ncol), F32), jax.ShapeDtypeStruct((t, 3 * d), BF16),
                   jax.ShapeDtypeStruct((n_row, d, TMF), BF16)]
        + [jax.ShapeDtypeStruct((2 * N_CHIP,) + hv.shape[1:], BF16) for hv in halves],
        compiler_params=_params(("arbitrary", "arbitrary")),
    )(order, qrow, qcol, x2d, g_in, *halves)
    return outs[0], outs[1], outs[2], outs[3:]


def _branch_a_fwd(a_pre, g_v, wm, b_t):
    t = a_pre.shape[0]
    d = g_v.shape[1]
    d3 = 3 * d
    ng, chunk, _ = wm.shape
    cw = d // ng

    per_step = CHUNKS_PER_STEP if t % (CHUNKS_PER_STEP * chunk) == 0 else 1

    def body(a_ref, gv_ref, wm_ref, bt_ref, ya_ref):
        for n in range(per_step):
            rows = slice(n * chunk, (n + 1) * chunk)
            ua = _gelu(a_ref[rows, 0:d])
            vg = _gelu(a_ref[rows, d:2 * d])
            za = a_ref[rows, 2 * d:3 * d]
            rv = lax.rsqrt(jnp.mean(vg * vg, axis=-1, keepdims=True) + EPS)
            va = (vg * rv * gv_ref[...]).astype(BF16)
            gate = ua * (za * _sigmoid(za))
            for g in range(ng):
                sl = slice(g * cw, (g + 1) * cw)
                mixed = jnp.dot(wm_ref[g], va[:, sl], preferred_element_type=F32) + bt_ref[:, g:g + 1]
                ya_ref[rows, sl] = (gate[:, sl] * mixed).astype(BF16)

    tile = per_step * chunk
    return pl.pallas_call(
        body, name="branch_a_fwd",
        grid=(t // tile,),
        in_specs=[pl.BlockSpec((tile, d3), lambda i: (i, 0)), _const((1, d)), _const(wm.shape), _const(b_t.shape)],
        out_specs=pl.BlockSpec((tile, d), lambda i: (i, 0)),
        out_shape=jax.ShapeDtypeStruct((t, d), BF16),
        compiler_params=_params(("arbitrary",)),
    )(a_pre, g_v, wm, b_t)


def _branch_a_bwd(a_pre, dya, g_v, wm, wm_t, b_t):
    t = a_pre.shape[0]
    d = g_v.shape[1]
    d3 = 3 * d
    ng, chunk, _ = wm.shape
    cw = d // ng
    tile = CHUNKS_PER_STEP * chunk
    nsteps = t // tile

    def body(a_ref, dya_ref, gv_ref, wm_ref, wmt_ref, bt_ref, da_ref, gws_ref, gbt_ref, gnv_ref, db_acc):
        i = pl.program_id(0)

        @pl.when(i == 0)
        def _():
            gws_ref[...] = jnp.zeros_like(gws_ref)
            gnv_ref[...] = jnp.zeros_like(gnv_ref)
            db_acc[...] = jnp.zeros_like(db_acc)

        def one_chunk(c, carry):
            rows = pl.ds(pl.multiple_of(c * chunk, chunk), chunk)
            ua, dgelu_u = _gelu_and_grad(a_ref[rows, 0:d])
            vg, dgelu_v = _gelu_and_grad(a_ref[rows, d:2 * d])
            za = a_ref[rows, 2 * d:3 * d]
            sig = _sigmoid(za)
            sz = za * sig
            dsz = sig * (1.0 + za * (1.0 - sig))
            rv = lax.rsqrt(jnp.mean(vg * vg, axis=-1, keepdims=True) + EPS)
            nv = vg * rv
            gv = gv_ref[...]
            va = (nv * gv).astype(BF16)
            dya = dya_ref[rows, :]
            dmix = dya * ua * sz
            db_acc[...] += dmix
            dmix_b = dmix.astype(BF16)
            t_gate = dya * sz
            t_z = dya * ua * dsz
            dva_parts = []
            for g in range(ng):
                sl = slice(g * cw, (g + 1) * cw)
                mixed = jnp.dot(wm_ref[g], va[:, sl], preferred_element_type=F32) + bt_ref[:, g:g + 1]
                da_ref[rows, sl] = (t_gate[:, sl] * mixed * dgelu_u[:, sl]).astype(BF16)
                da_ref[rows, 2 * d + g * cw:2 * d + (g + 1) * cw] = (t_z[:, sl] * mixed).astype(BF16)
                gws_ref[g] += lax.dot_general(dmix_b[:, sl], va[:, sl], NT, preferred_element_type=F32)
                dva_parts.append(jnp.dot(wmt_ref[g], dmix_b[:, sl], preferred_element_type=F32))
            dva = jnp.concatenate(dva_parts, axis=1)
            gnv_ref[...] += jnp.sum(dva * nv, axis=0, keepdims=True)
            dnv = dva * gv
            dvg = rv * (dnv - nv * jnp.mean(dnv * nv, axis=-1, keepdims=True))
            da_ref[rows, d:2 * d] = (dvg * dgelu_v).astype(BF16)
            return carry

        lax.fori_loop(0, CHUNKS_PER_STEP, one_chunk, 0)

        @pl.when(i == nsteps - 1)
        def _():
            acc = db_acc[...]
            for g in range(ng):
                gbt_ref[:, g:g + 1] = jnp.sum(acc[:, g * cw:(g + 1) * cw], axis=1, keepdims=True)

    return pl.pallas_call(
        body, name="branch_a_bwd",
        grid=(nsteps,),
        in_specs=[pl.BlockSpec((tile, d3), lambda i: (i, 0)), pl.BlockSpec((tile, d), lambda i: (i, 0)),
                  _const((1, d)), _const(wm.shape), _const(wm_t.shape), _const(b_t.shape)],
        out_specs=[pl.BlockSpec((tile, d3), lambda i: (i, 0)), _const(wm.shape), _const(b_t.shape), _const((1, d))],
        out_shape=[jax.ShapeDtypeStruct((t, d3), BF16), jax.ShapeDtypeStruct(wm.shape, F32),
                   jax.ShapeDtypeStruct(b_t.shape, F32), jax.ShapeDtypeStruct((1, d), F32)],
        scratch_shapes=[pltpu.VMEM((chunk, d), F32)],
        compiler_params=_params(("arbitrary",)),
    )(a_pre, dya, g_v, wm, wm_t, b_t)


def _below_diagonal(n):
    return lax.broadcasted_iota(jnp.int32, (n, n), 0) > lax.broadcasted_iota(jnp.int32, (n, n), 1)


def _twice(tri):
    t = tri.astype(BF16)
    return jnp.concatenate([t, t], axis=0)


def _cumsum_mm(a, tri2):
    hi, lo = _split_bf16(a)
    return jnp.dot(jnp.concatenate([hi, lo], axis=1), tri2, preferred_element_type=F32)


LOG2E = 1.4426950408889634
_SIGN = 0x80000000


def _sb_block(q, k, scale, upper2, causal):
    z2 = lax.dot_general(q, k, NT, preferred_element_type=F32) * (scale * LOG2E)
    neg_abs = lax.bitcast_convert_type(lax.bitcast_convert_type(z2, jnp.uint32) | jnp.uint32(_SIGN), F32)
    l2 = jnp.log(1.0 + jnp.exp2(neg_abs)) * LOG2E
    log_beta = jnp.minimum(z2, 0.0) - l2
    lom = log_beta - z2
    if causal is not None:
        lom = jnp.where(causal, lom, 0.0)
    sx = _cumsum_mm(lom, upper2)
    return log_beta, sx, sx[:, 0:1] + lom[:, 0:1]


DEAD_LOG2 = -150.0


def _max_carry(carries):
    return jnp.max(functools.reduce(jnp.maximum, carries))


ZB_GROUP, GA_GROUP, GB_GROUP = 6, 7, 8


def _attn_specs(d, seq, nq, heads_per_step):
    hp_w = heads_per_step * (d // HEADS)
    n_hp = d // hp_w
    row_blk = lambda group: pl.BlockSpec((ATT_T, hp_w), lambda b, h, i: (b * nq + i, group * n_hp + h))
    seq_blk = lambda group: pl.BlockSpec((seq, hp_w), lambda b, h, i: (b, group * n_hp + h))
    return row_blk, seq_blk, n_hp


def _attn_fwd(qkv, proj, bsz, seq):
    t, d3 = qkv.shape
    d = d3 // 3
    hd = d // HEADS
    nq = seq // ATT_T
    scale = hd ** -0.5
    n_heads = ATT_HP
    row_blk, seq_blk, n_hp = _attn_specs(d, seq, nq, n_heads)

    def body(q_ref, k_ref, v_ref, zb_ref, o_ref, yb_ref):
        i = pl.program_id(2)
        causal = _below_diagonal(ATT_T)
        upper2 = _twice(causal)

        def step(kb, state, mask):
            rows = pl.ds(pl.multiple_of(kb * ATT_T, ATT_T), ATT_T)
            heads = [slice(h * hd, (h + 1) * hd) for h in range(n_heads)]
            scores = [_sb_block(q_ref[:, cols], k_ref[rows, cols], scale, upper2, mask) for cols in heads]
            new = []
            for cols, (carry, acc), (log_beta, sx, total) in zip(heads, state, scores):
                a = jnp.exp2(log_beta + sx + carry)
                if mask is not None:
                    a = jnp.where(mask, a, 0.0)
                acc = acc + jnp.dot(a.astype(BF16), v_ref[rows, cols], preferred_element_type=F32)
                new.append((carry + total, acc))
            return tuple(new)

        init = tuple((jnp.zeros((ATT_T, 1), F32), jnp.zeros((ATT_T, hd), F32)) for _ in range(n_heads))
        state = step(i, init, causal)
        def more(c):
            new = step(c[0], c[1], None)
            return c[0] - 1, new, _max_carry([s[0] for s in new])

        _, state, _ = lax.while_loop(lambda c: (c[0] >= 0) & (c[2] > DEAD_LOG2), more,
                                     (i - 1, state, _max_carry([s[0] for s in state])))
        for h in range(n_heads):
            cols = slice(h * hd, (h + 1) * hd)
            acc = state[h][1]
            zb = zb_ref[:, cols]
            o_ref[:, cols] = acc
            yb_ref[:, cols] = (acc * (zb * _sigmoid(zb))).astype(BF16)

    return pl.pallas_call(
        body, name="attn_fwd",
        grid=(bsz, n_hp, nq),
        in_specs=[row_blk(0), seq_blk(1), seq_blk(2), row_blk(ZB_GROUP)],
        out_specs=[row_blk(0), row_blk(0)],
        out_shape=[jax.ShapeDtypeStruct((t, d), F32), jax.ShapeDtypeStruct((t, d), BF16)],
        compiler_params=_params(("arbitrary", "arbitrary", "arbitrary")),
    )(qkv, qkv, qkv, proj)


def _attn_bwd(qkv, proj, o, dyb, bsz, seq):
    t, d3 = qkv.shape
    d = d3 // 3
    hd = d // HEADS
    nq = seq // ATT_T
    scale = hd ** -0.5
    row_blk, seq_blk, n_hp = _attn_specs(d, seq, nq, ATT_HP)

    def body(q_ref, k_ref, v_ref, zb_ref, o_ref, dyb_ref, dq_ref, dk_ref, dv_ref, dzb_ref,
             g_s, beta_s, dkt_acc, dvt_acc):
        i = pl.program_id(2)

        @pl.when(i == 0)
        def _():
            dkt_acc[...] = jnp.zeros_like(dkt_acc)
            dvt_acc[...] = jnp.zeros_like(dvt_acc)

        causal = _below_diagonal(ATT_T)
        upper2 = _twice(causal)
        lower2 = _twice(~causal)
        zb = zb_ref[...]
        sig = _sigmoid(zb)
        dyb_t = dyb_ref[...]
        do_f = dyb_t * (zb * sig)
        do = do_f.astype(BF16)
        do_t = do_f.T.astype(BF16)
        q_t = q_ref[...].astype(F32).T.astype(BF16)
        dzb_ref[...] = (dyb_t * o_ref[...] * (sig * (1.0 + zb * (1.0 - sig)))).astype(BF16)

        def sweep(kb, carries, mask):
            rows = pl.ds(pl.multiple_of(kb * ATT_T, ATT_T), ATT_T)
            heads = [slice(h * hd, (h + 1) * hd) for h in range(ATT_HP)]
            scores = [_sb_block(q_ref[:, cols], k_ref[rows, cols], scale, upper2, mask) for cols in heads]
            das = [lax.dot_general(do[:, cols], v_ref[rows, cols], NT, preferred_element_type=F32) for cols in heads]
            new = []
            for h, (cols, carry, (log_beta, sx, total), da) in enumerate(zip(heads, carries, scores, das)):
                a = jnp.exp2(log_beta + sx + carry)
                beta = jnp.exp2(log_beta)
                if mask is not None:
                    a = jnp.where(mask, a, 0.0)
                    beta = jnp.where(mask, beta, 0.0)
                g_s[h, kb] = a * da
                beta_s[h, kb] = beta
                dvt_acc[kb, cols, :] += jnp.dot(do_t[cols, :], a.astype(BF16), preferred_element_type=F32)
                new.append(carry + total)
            return tuple(new)

        carries = sweep(i, tuple(jnp.zeros((ATT_T, 1), F32) for _ in range(ATT_HP)), causal)

        def more(c):
            new = sweep(c[0], c[1], None)
            return c[0] - 1, new, _max_carry(new)

        last, _, _ = lax.while_loop(lambda c: (c[0] >= 0) & (c[2] > DEAD_LOG2), more, (i - 1, carries, _max_carry(carries)))
        first_kb = last + 1

        def back(kb, state):
            rows = pl.ds(pl.multiple_of(kb * ATT_T, ATT_T), ATT_T)
            heads = [slice(h * hd, (h + 1) * hd) for h in range(ATT_HP)]
            sums = [_cumsum_mm(g_s[h, kb], lower2) for h in range(ATT_HP)]
            new = []
            for h, (cols, (p_carry, dq), px) in enumerate(zip(heads, state, sums)):
                dz = ((g_s[h, kb] - (p_carry + px) * beta_s[h, kb]) * scale).astype(BF16)
                dq = dq + jnp.dot(dz, k_ref[rows, cols], preferred_element_type=F32)
                dkt_acc[kb, cols, :] += jnp.dot(q_t[cols, :], dz, preferred_element_type=F32)
                new.append((p_carry + px[:, ATT_T - 1:ATT_T], dq))
            return tuple(new)

        init = tuple((jnp.zeros((ATT_T, 1), F32), jnp.zeros((ATT_T, hd), F32)) for _ in range(ATT_HP))
        state = lax.fori_loop(first_kb, i + 1, back, init)
        for h in range(ATT_HP):
            dq_ref[:, h * hd:(h + 1) * hd] = state[h][1].astype(BF16)

        @pl.when(i == nq - 1)
        def _():
            for kb in range(nq):
                dk_ref[kb * ATT_T:(kb + 1) * ATT_T, :] = dkt_acc[kb].T.astype(BF16)
                dv_ref[kb * ATT_T:(kb + 1) * ATT_T, :] = dvt_acc[kb].T.astype(BF16)

    out = jax.ShapeDtypeStruct((t, d), BF16)
    hp_w = ATT_HP * hd
    return pl.pallas_call(
        body, name="attn_bwd",
        grid=(bsz, n_hp, nq),
        in_specs=[row_blk(0), seq_blk(1), seq_blk(2), row_blk(ZB_GROUP), row_blk(0), row_blk(0)],
        out_specs=[row_blk(0), seq_blk(0), seq_blk(0), row_blk(0)],
        out_shape=[out, out, out, out],
        scratch_shapes=[pltpu.VMEM((ATT_HP, nq, ATT_T, ATT_T), F32), pltpu.VMEM((ATT_HP, nq, ATT_T, ATT_T), F32),
                        pltpu.VMEM((nq, hp_w, ATT_T), F32), pltpu.VMEM((nq, hp_w, ATT_T), F32)],
        compiler_params=_params(("arbitrary", "arbitrary", "arbitrary")),
    )(qkv, qkv, qkv, proj, o, dyb)


def _out_proj(ya, yb, g_pre, x2d, tgt, w_og, w_osb, w_out, g_f):
    t, d = x2d.shape

    def body(ya_ref, yb_ref, ga_ref, gb_ref, x_ref, tgt_ref, wog_ref, wosb_ref, wout_ref, gf_ref,
             dya_ref, dyb_ref, dg_ref, dx2_ref, loss_ref, gnf_ref, gwog_ref, gwosb_ref, gwout_ref):
        @pl.when(pl.program_id(0) == 0)
        def _():
            loss_ref[...] = jnp.zeros_like(loss_ref)
            gnf_ref[...] = jnp.zeros_like(gnf_ref)
            gwog_ref[...] = jnp.zeros_like(gwog_ref)
            gwosb_ref[...] = jnp.zeros_like(gwosb_ref)
            gwout_ref[...] = jnp.zeros_like(gwout_ref)

        ya = ya_ref[...]
        yb = yb_ref[...]
        pa = jnp.dot(ya, wog_ref[...], preferred_element_type=F32)
        pb = jnp.dot(yb, wosb_ref[...], preferred_element_type=F32)
        sga = _sigmoid(ga_ref[...])
        sgb = _sigmoid(gb_ref[...])
        merged = (sga * pa + sgb * pb).astype(BF16)
        x2 = x_ref[...] + jnp.dot(merged, wout_ref[...], preferred_element_type=F32)
        r2 = lax.rsqrt(jnp.mean(x2 * x2, axis=-1, keepdims=True) + EPS)
        n2 = x2 * r2
        gf = gf_ref[...]
        err = n2 * gf - tgt_ref[...]
        loss_ref[...] += 0.5 * jnp.sum(jnp.sum(err * err, axis=-1, keepdims=True), axis=0, keepdims=True) / d
        dy = err * (1.0 / d)
        gnf_ref[...] += jnp.sum(dy * n2, axis=0, keepdims=True)
        dn = dy * gf
        dx2 = r2 * (dn - n2 * jnp.mean(dn * n2, axis=-1, keepdims=True))
        dx2_ref[...] = dx2
        dx2_b = dx2.astype(BF16)
        dmerged = lax.dot_general(dx2_b, wout_ref[...], NT, preferred_element_type=F32)
        gwout_ref[...] += lax.dot_general(merged, dx2_b, TN, preferred_element_type=F32)
        dg_ref[:, 0:d] = (dmerged * pa * (sga * (1.0 - sga))).astype(BF16)
        dg_ref[:, d:2 * d] = (dmerged * pb * (sgb * (1.0 - sgb))).astype(BF16)
        dpa = (dmerged * sga).astype(BF16)
        dpb = (dmerged * sgb).astype(BF16)
        dya_ref[...] = lax.dot_general(dpa, wog_ref[...], NT, preferred_element_type=F32)
        dyb_ref[...] = lax.dot_general(dpb, wosb_ref[...], NT, preferred_element_type=F32)
        gwog_ref[...] += lax.dot_general(ya, dpa, TN, preferred_element_type=F32)
        gwosb_ref[...] += lax.dot_general(yb, dpb, TN, preferred_element_type=F32)

    row = lambda i: (i, 0)
    return pl.pallas_call(
        body, name="out_proj",
        grid=(t // TM,),
        in_specs=[pl.BlockSpec((TM, d), row), pl.BlockSpec((TM, d), row),
                  pl.BlockSpec((TM, d), lambda i: (i, GA_GROUP)), pl.BlockSpec((TM, d), lambda i: (i, GB_GROUP)),
                  pl.BlockSpec((TM, d), row), pl.BlockSpec((TM, d), row),
                  _resident((d, d)), _resident((d, d)), _resident((d, d)), _const((1, d))],
        out_specs=[pl.BlockSpec((TM, d), row), pl.BlockSpec((TM, d), row), pl.BlockSpec((TM, 2 * d), row),
                   pl.BlockSpec((TM, d), row), _const((1, 1)), _const((1, d)),
                   _const((d, d)), _const((d, d)), _const((d, d))],
        out_shape=[jax.ShapeDtypeStruct((t, d), F32), jax.ShapeDtypeStruct((t, d), F32),
                   jax.ShapeDtypeStruct((t, 2 * d), BF16), jax.ShapeDtypeStruct((t, d), F32),
                   jax.ShapeDtypeStruct((1, 1), F32), jax.ShapeDtypeStruct((1, d), F32),
                   jax.ShapeDtypeStruct((d, d), F32), jax.ShapeDtypeStruct((d, d), F32),
                   jax.ShapeDtypeStruct((d, d), F32)],
        compiler_params=_params(("arbitrary",)),
    )(ya, yb, g_pre, g_pre, x2d, tgt, w_og, w_osb, w_out, g_f)


def _dproj_pieces(d):
    return [(0, 0, 3), (1, 3, 1), (2, 4, 1), (3, 5, 1), (4, 6, 1), (5, 7, 2)]


def _in_proj_bwd_x(pieces, wg, x2d, g_in, dx2, gw16):
    t, d = x2d.shape
    ncol = wg.shape[2]
    segs = _segments(d, ncol)
    layout = _dproj_pieces(d)
    nsteps = t // TMX

    def body(da_ref, dq_ref, dk_ref, dv_ref, dzb_ref, dg_ref, w_ref, x_ref, g_ref, dx2_ref, gw16_ref,
             gx_ref, gn_ref, recv_ref, send_sems, recv_sems):
        x_pos, y_pos, c_pos, chips = _place()

        def share(k, chunk):
            px, py = chips[k]
            return pltpu.make_async_remote_copy(
                src_ref=gw16_ref.at[:, chunk * ncol:(chunk + 1) * ncol], dst_ref=recv_ref.at[k],
                send_sem=send_sems.at[k], recv_sem=recv_sems.at[k], device_id=(px, py, c_pos), device_id_type=MESH)

        @pl.when(pl.program_id(0) == 0)
        def _():
            gn_ref[...] = jnp.zeros_like(gn_ref)
            for k, (px, py) in enumerate(chips):
                for chunk in range(N_CHIP):
                    @pl.when(2 * px + py == chunk)
                    def _(k=k, chunk=chunk):
                        share(k, chunk).start()

        @pl.when(pl.program_id(0) == nsteps - 1)
        def _():
            for k in range(N_CHIP - 1):
                share(k, 0).wait()

        refs = (da_ref, dq_ref, dk_ref, dv_ref, dzb_ref, dg_ref)
        dh = jnp.zeros((TMX, d), F32)
        for chip, c0, grp, s0, width in segs:
            piece, first, _ = next(p for p in layout if p[1] <= grp < p[1] + p[2])
            off = (grp - first) * d + s0
            dh = dh + lax.dot_general(refs[piece][:, off:off + width], w_ref[chip, :, c0:c0 + width], NT,
                                      preferred_element_type=F32)
        x = x_ref[...]
        r = lax.rsqrt(jnp.mean(x * x, axis=-1, keepdims=True) + EPS)
        n = x * r
        gn_ref[...] += jnp.sum(dh * n, axis=0, keepdims=True)
        dn = dh * g_ref[...]
        gx_ref[...] = dx2_ref[...] + r * (dn - n * jnp.mean(dn * n, axis=-1, keepdims=True))

    row = lambda i: (i, 0)
    return pl.pallas_call(
        body, name="in_proj_bwd_x",
        grid=(nsteps,),
        in_specs=[pl.BlockSpec((TMX, p.shape[1]), row) for p in pieces]
        + [_resident(wg.shape), pl.BlockSpec((TMX, d), row), _const((1, d)), pl.BlockSpec((TMX, d), row), ANY],
        out_specs=[pl.BlockSpec((TMX, d), row), _const((1, d)), ANY],
        out_shape=[jax.ShapeDtypeStruct((t, d), F32), jax.ShapeDtypeStruct((1, d), F32),
                   jax.ShapeDtypeStruct((N_CHIP - 1, gw16.shape[0], ncol), BF16)],
        scratch_shapes=[pltpu.SemaphoreType.DMA((N_CHIP - 1,)), pltpu.SemaphoreType.DMA((N_CHIP - 1,))],
        compiler_params=_params(("arbitrary",)),
    )(*pieces, wg, x2d, g_in, dx2, gw16)


def _in_proj_bwd_w(h_t, pieces, mats16, pack):
    n_blk, d, blk = h_t.shape
    per_step = TKW // blk
    nk = n_blk // per_step
    half = d // 2
    layout = _dproj_pieces(d)
    n_mats = len(mats16)
    n_dev = 2 * N_CHIP
    flips = [(dx, dy, dc) for dx in (0, 1) for dy in (0, 1) for dc in (0, 1)][1:]

    def body(ht_ref, da_ref, dq_ref, dk_ref, dv_ref, dzb_ref, dg_ref, *rest):
        mat_refs, pack_ref = rest[:n_mats], rest[n_mats]
        gw_ref, sib_ref = rest[n_mats + 1:n_mats + 3]
        recv_refs, slots_ref = rest[n_mats + 3:2 * n_mats + 3], rest[2 * n_mats + 3]
        acc, stage, mat_send, mat_recv, pack_send, pack_recv, own_sem, stage_send, sib_recv = rest[2 * n_mats + 4:]
        s = pl.program_id(0)
        i = pl.program_id(1)
        x_pos, y_pos, c_pos, chips = _place()
        me = 4 * x_pos + 2 * y_pos + c_pos

        def to_sibling(slot, group):
            return pltpu.make_async_remote_copy(
                src_ref=stage.at[slot], dst_ref=sib_ref.at[:, group * d:(group + 1) * d],
                send_sem=stage_send.at[slot], recv_sem=sib_recv, device_id=(x_pos, y_pos, 1 - c_pos), device_id_type=MESH)

        def exchanges():
            cps = []
            for k, (px, py) in enumerate(chips):
                for a in range(n_mats):
                    cps.append(pltpu.make_async_remote_copy(
                        src_ref=mat_refs[a].at[2 * px + py], dst_ref=recv_refs[a].at[k],
                        send_sem=mat_send.at[a, k], recv_sem=mat_recv.at[a, k],
                        device_id=(px, py, c_pos), device_id_type=MESH))
            for k, (dx, dy, dc) in enumerate(flips):
                peer = (1 - x_pos if dx else x_pos, 1 - y_pos if dy else y_pos, 1 - c_pos if dc else c_pos)
                cps.append(pltpu.make_async_remote_copy(
                    src_ref=pack_ref, dst_ref=slots_ref.at[me], send_sem=pack_send.at[k], recv_sem=pack_recv.at[k],
                    device_id=peer, device_id_type=MESH))
            return cps, pltpu.make_async_copy(pack_ref, slots_ref.at[me], own_sem)

        @pl.when((s == 0) & (i == 0))
        def _():
            cps, own = exchanges()
            own.start()
            for cp in cps:
                cp.start()

        @pl.when(i == 0)
        def _():
            acc[...] = jnp.zeros_like(acc)

        refs = (da_ref, dq_ref, dk_ref, dv_ref, dzb_ref, dg_ref)
        for piece, first, count in layout:
            @pl.when((s >= first) & (s < first + count))
            def _(piece=piece):
                acc[...] += sum(jnp.dot(ht_ref[j], refs[piece][j * blk:(j + 1) * blk, :], preferred_element_type=F32)
                                for j in range(per_step))

        @pl.when(i == nk - 1)
        def _():
            gw_ref[...] = acc[...]
            for slot in range(2):
                @pl.when(s % 2 == slot)
                def _(slot=slot):
                    @pl.when(s >= 2)
                    def _():
                        to_sibling(slot, 0).wait_send()
                    for other in range(2):
                        @pl.when(c_pos == 1 - other)
                        def _(other=other):
                            stage[slot] = acc[other * half:(other + 1) * half, :].astype(BF16)
                    for group in range(N_SPLIT):
                        @pl.when(s == group)
                        def _(group=group):
                            to_sibling(slot, group).start()

        @pl.when((s == N_SPLIT - 1) & (i == nk - 1))
        def _():
            for slot in range(2):
                to_sibling(slot, 0).wait_send()
            pltpu.make_async_remote_copy(
                src_ref=sib_ref, dst_ref=sib_ref, send_sem=stage_send.at[0], recv_sem=sib_recv,
                device_id=(x_pos, y_pos, c_pos), device_id_type=MESH).wait_recv()
            cps, own = exchanges()
            own.wait()
            for cp in cps:
                cp.wait()

    def piece_spec(p, first, count):
        def index(s, i):
            mine = (s >= first) & (s < first + count)
            return jnp.where(mine, i, 0), jnp.where(mine, s - first, 0)
        return pl.BlockSpec((TKW, d), index)

    col_blk = pl.BlockSpec((d, d), lambda s, i: (0, s))
    outs = pl.pallas_call(
        body, name="in_proj_bwd_w",
        grid=(N_SPLIT, nk),
        in_specs=[pl.BlockSpec((per_step, d, blk), lambda s, i: (i, 0, 0))] + [piece_spec(*p) for p in layout]
        + [ANY] * (n_mats + 1),
        out_specs=[col_blk, ANY] + [ANY] * (n_mats + 1),
        out_shape=[jax.ShapeDtypeStruct((d, N_SPLIT * d), F32), jax.ShapeDtypeStruct((half, N_SPLIT * d), BF16)]
        + [jax.ShapeDtypeStruct((N_CHIP - 1,) + m.shape[1:], BF16) for m in mats16]
        + [jax.ShapeDtypeStruct((n_dev,) + pack.shape, F32)],
        scratch_shapes=[pltpu.VMEM((d, d), F32), pltpu.VMEM((2, half, d), BF16),
                        pltpu.SemaphoreType.DMA((n_mats, N_CHIP - 1)), pltpu.SemaphoreType.DMA((n_mats, N_CHIP - 1)),
                        pltpu.SemaphoreType.DMA((n_dev - 1,)), pltpu.SemaphoreType.DMA((n_dev - 1,)),
                        pltpu.SemaphoreType.DMA, pltpu.SemaphoreType.DMA((2,)), pltpu.SemaphoreType.DMA],
        compiler_params=_params(("arbitrary", "arbitrary")),
    )(h_t, *pieces, *mats16, pack)
    return outs[0], outs[1], outs[2:2 + n_mats], outs[2 + n_mats]


def _local_step(proj, qkv, x2d, tgt2d, bsz, seq, norm_v, w_s, b_s, w_og, w_osb, w_out, norm_final):
    d = x2d.shape[1]
    chunk = w_s.shape[-1]
    causal = jnp.tril(jnp.ones((chunk, chunk), dtype=bool))
    wm = jnp.where(causal[None], w_s, 0.0).astype(BF16)
    wm_t = jnp.swapaxes(wm, 1, 2)
    b_t = b_s.T

    ya = _branch_a_fwd(proj, norm_v, wm, b_t)
    o, yb = _attn_fwd(qkv, proj, bsz, seq)
    dya, dyb, dg, dx2, loss, g_nf, g_wog, g_wosb, g_wout = _out_proj(
        ya, yb, proj, x2d, tgt2d, w_og, w_osb, w_out, norm_final.reshape(1, d))
    dq, dk, dv, dzb = _attn_bwd(qkv, proj, o, dyb, bsz, seq)
    d_a, g_ws, g_bt, g_nv = _branch_a_bwd(proj, dya, norm_v, wm, wm_t, b_t)
    g_ws = jnp.where(causal[None], g_ws, 0.0)
    return loss, (d_a, dq, dk, dv, dzb, dg), dx2, g_nv, g_ws, g_bt.T, g_wog, g_wosb, g_wout, g_nf


def _row_tile(rows):
    return next(r for r in (128, 64, 32, 16, 8) if rows % r == 0)


def _cast_bf16(arrs):
    n = len(arrs)

    def body(*refs):
        for a_ref, o_ref in zip(refs[:n], refs[n:]):
            o_ref[...] = a_ref[...].astype(BF16)

    specs = [pl.BlockSpec((a.shape[0] // CAST_STEPS, a.shape[1]), lambda i: (i, 0)) for a in arrs]
    return pl.pallas_call(
        body, name="cast_bf16", grid=(CAST_STEPS,),
        in_specs=specs, out_specs=specs,
        out_shape=[jax.ShapeDtypeStruct(a.shape, BF16) for a in arrs],
        compiler_params=_params(("arbitrary",)),
    )(*arrs)


def _chip_half(full, from_sibling, core, chip, tile):
    half, n = from_sibling.shape

    def body(where_ref, own_ref, sib_ref, o32_ref, o16_ref):
        total = own_ref[...] + sib_ref[...].astype(F32)
        o16_ref[...] = total.astype(BF16)

        @pl.when(pl.program_id(0) == where_ref[1])
        def _():
            o32_ref[...] = total

    blk = pl.BlockSpec((half, tile), lambda j, where_ref: (0, j))
    return pl.pallas_call(
        body, name="chip_half",
        grid_spec=pltpu.PrefetchScalarGridSpec(
            num_scalar_prefetch=1, grid=(n // tile,),
            in_specs=[pl.BlockSpec((half, tile), lambda j, where_ref: (where_ref[0], j)), blk],
            out_specs=[pl.BlockSpec((half, tile), lambda j, where_ref: (0, 0)), blk]),
        out_shape=[jax.ShapeDtypeStruct((half, tile), F32), jax.ShapeDtypeStruct((half, n), BF16)],
        compiler_params=_params(("arbitrary",)),
    )(jnp.stack([core, chip]).astype(jnp.int32), full, from_sibling)


def _add_received(fulls, recvs, chip, by_cols):
    n = len(fulls)
    _, rows, cols = recvs[0].shape
    tr = _row_tile(rows)
    nb = rows // tr

    def body(chip_ref, *refs):
        for own_ref, recv_ref, o_ref in zip(refs[:n], refs[n:2 * n], refs[2 * n:]):
            s = own_ref[...]
            for k in range(N_CHIP - 1):
                s = s + recv_ref[k].astype(F32)
            o_ref[...] = s

    own_map = (lambda i, chip_ref: (i, chip_ref[0])) if by_cols else (lambda i, chip_ref: (chip_ref[0] * nb + i, 0))
    return pl.pallas_call(
        body, name="add_received",
        grid_spec=pltpu.PrefetchScalarGridSpec(
            num_scalar_prefetch=1, grid=(nb,),
            in_specs=[pl.BlockSpec((tr, cols), own_map)] * n
            + [pl.BlockSpec((N_CHIP - 1, tr, cols), lambda i, chip_ref: (0, i, 0))] * n,
            out_specs=[pl.BlockSpec((tr, cols), lambda i, chip_ref: (i, 0))] * n),
        out_shape=[jax.ShapeDtypeStruct((rows, cols), F32)] * n,
        compiler_params=_params(("arbitrary",)),
    )(chip.reshape(1).astype(jnp.int32), *fulls, *recvs)


def _adamw_math(w, m, v, g):
    new_m = ADAM_B1 * m + (1.0 - ADAM_B1) * g
    new_v = ADAM_B2 * v + (1.0 - ADAM_B2) * (g * g)
    m_hat = new_m / (1.0 - ADAM_B1 ** ADAM_STEP)
    v_hat = new_v / (1.0 - ADAM_B2 ** ADAM_STEP)
    return -ADAM_LR * (m_hat / (jnp.sqrt(v_hat) + ADAM_EPS) + ADAM_WD * w), new_m, new_v


def _adamw_pairs(ws, ms, vs, mines, theirs):
    n = len(ws)
    rows, cols = ws[0].shape
    tr = _row_tile(rows)

    def body(*refs):
        ins, outs = refs[:5 * n], refs[5 * n:]
        for a in range(n):
            w_ref, m_ref, v_ref, mine_ref, theirs_ref = ins[a::n]
            g_ref, d_ref, nm_ref, nv_ref = outs[4 * a:4 * a + 4]
            g = mine_ref[...] + theirs_ref[...]
            g_ref[...] = g
            d_ref[...], nm_ref[...], nv_ref[...] = _adamw_math(w_ref[...], m_ref[...], v_ref[...], g)

    spec = pl.BlockSpec((tr, cols), lambda i: (i, 0))
    out = jax.ShapeDtypeStruct((rows, cols), F32)
    outs = pl.pallas_call(
        body, name="adamw_pairs", grid=(rows // tr,),
        in_specs=[spec] * (5 * n), out_specs=[spec] * (4 * n), out_shape=[out] * (4 * n),
        compiler_params=_params(("arbitrary",)),
    )(*ws, *ms, *vs, *mines, *theirs)
    return [outs[4 * a:4 * a + 4] for a in range(n)]


def _adamw_halves(w, m, v, mine, theirs, core):
    rows, cols = w.shape
    tr = _row_tile(rows // 2)
    per_half = rows // 2 // tr

    def body(core_ref, w_ref, m_ref, v_ref, mine_ref, theirs_ref, g_ref, d_ref, nm_ref, nv_ref):
        is_mine = pl.program_id(0) // per_half == core_ref[0]
        for part, cond in ((mine_ref, is_mine), (theirs_ref, jnp.logical_not(is_mine))):
            @pl.when(cond)
            def _(part=part):
                g = part[...]
                g_ref[...] = g
                d_ref[...], nm_ref[...], nv_ref[...] = _adamw_math(w_ref[...], m_ref[...], v_ref[...], g)

    spec = pl.BlockSpec((tr, cols), lambda i, core_ref: (i, 0))

    def half_spec(own):
        def index(i, core_ref):
            in_core_half = i // per_half == core_ref[0]
            here = in_core_half if own else jnp.logical_not(in_core_half)
            return jnp.where(here, i % per_half, 0), 0
        return pl.BlockSpec((tr, cols), index)

    out = jax.ShapeDtypeStruct(w.shape, F32)
    return pl.pallas_call(
        body, name="adamw_halves",
        grid_spec=pltpu.PrefetchScalarGridSpec(
            num_scalar_prefetch=1, grid=(rows // tr,),
            in_specs=[spec] * 3 + [half_spec(True), half_spec(False)], out_specs=[spec] * 4),
        out_shape=[out] * 4,
        compiler_params=_params(("arbitrary",)),
    )(core.reshape(1).astype(jnp.int32), w, m, v, mine, theirs)


def _adamw_small(ws, ms, vs, slots_head, slots_tail):
    n_dev = slots_head.shape[0]
    n_par = len(ws)
    groups, chunk, _ = ws[4].shape

    def body(*refs):
        w_refs, m_refs, v_refs = refs[:n_par], refs[n_par:2 * n_par], refs[2 * n_par:3 * n_par]
        head_ref, tail_ref = refs[3 * n_par:3 * n_par + 2]
        out_refs, scalar_ref = refs[3 * n_par + 2:-1], refs[-1]

        def total(ref, rows, cols):
            g = ref[0, rows, cols]
            for i in range(1, n_dev):
                g = g + ref[i, rows, cols]
            return g

        def update(p, g, at=...):
            g_ref, d_ref, nm_ref, nv_ref = out_refs[4 * p:4 * p + 4]
            g_ref[at] = g
            d_ref[at], nm_ref[at], nv_ref[at] = _adamw_math(w_refs[p][at], m_refs[p][at], v_refs[p][at], g)

        every = slice(None)
        update(0, total(head_ref, slice(0, 1), every))
        update(1, total(tail_ref, slice(0, 1), every))
        update(2, total(tail_ref, slice(SLAB, SLAB + groups), slice(0, chunk)))
        update(3, total(tail_ref, slice(2 * SLAB, 2 * SLAB + 1), every))
        scalar_ref[...] = total(tail_ref, slice(3 * SLAB, 3 * SLAB + 1), slice(0, chunk))[:, 0:1]
        for grp in range(groups):
            update(4, total(tail_ref, slice(4 * SLAB, 4 * SLAB + chunk), slice(grp * chunk, (grp + 1) * chunk)), grp)

    vmem = pl.BlockSpec(memory_space=pltpu.VMEM)
    return pl.pallas_call(
        body, name="adamw_small", in_specs=[vmem] * (3 * n_par + 2), out_specs=[vmem] * (4 * n_par + 1),
        out_shape=[jax.ShapeDtypeStruct(w.shape, F32) for w in ws for _ in range(4)]
        + [jax.ShapeDtypeStruct((1, 1), F32)],
        compiler_params=pltpu.CompilerParams(vmem_limit_bytes=VMEM_LIMIT),
    )(*ws, *ms, *vs, slots_head, slots_tail)


ANY = pl.BlockSpec(memory_space=pl.ANY)


def _place():
    x, y, c = lax.axis_index("x"), lax.axis_index("y"), lax.axis_index("c")
    other_chips = [(1 - x, y), (x, 1 - y), (1 - x, 1 - y)]
    return x, y, c, other_chips


def _swap_and_gather(arrs, pack):
    n = len(arrs)
    n_dev = 2 * N_CHIP
    flips = [(dx, dy, dc) for dx in (0, 1) for dy in (0, 1) for dc in (0, 1)][1:]

    def body(*refs):
        ins, pack_ref = refs[:n], refs[n]
        outs, slots_ref = refs[n + 1:2 * n + 1], refs[2 * n + 1]
        send_sems, recv_sems, pack_send, pack_recv, own_sem = refs[2 * n + 2:]
        x, y, c, _ = _place()
        me = 4 * x + 2 * y + c
        own = pltpu.make_async_copy(pack_ref, slots_ref.at[me], own_sem)
        own.start()
        copies = []
        for k, (dx, dy, dc) in enumerate(flips):
            peer = (1 - x if dx else x, 1 - y if dy else y, 1 - c if dc else c)
            copies.append(pltpu.make_async_remote_copy(
                src_ref=pack_ref, dst_ref=slots_ref.at[me], send_sem=pack_send.at[k], recv_sem=pack_recv.at[k],
                device_id=peer, device_id_type=MESH))
        copies += [pltpu.make_async_remote_copy(
            src_ref=ins[a], dst_ref=outs[a], send_sem=send_sems.at[a], recv_sem=recv_sems.at[a],
            device_id=(x, y, 1 - c), device_id_type=MESH) for a in range(n)]
        for cp in copies:
            cp.start()
        for cp in copies:
            cp.wait()
        own.wait()

    return pl.pallas_call(
        body, name="swap_and_gather",
        in_specs=[ANY] * (n + 1), out_specs=[ANY] * (n + 1),
        out_shape=[jax.ShapeDtypeStruct(a.shape, a.dtype) for a in arrs] + [jax.ShapeDtypeStruct((n_dev,) + pack.shape, F32)],
        scratch_shapes=[pltpu.SemaphoreType.DMA((n,)), pltpu.SemaphoreType.DMA((n,)),
                        pltpu.SemaphoreType.DMA((n_dev - 1,)), pltpu.SemaphoreType.DMA((n_dev - 1,)),
                        pltpu.SemaphoreType.DMA],
    )(*arrs, pack)


SLAB = 8


def _slab(vec, d):
    return jnp.pad(vec.reshape(1, d), ((0, SLAB - 1), (0, 0)))


def _pack_tail(vec_nv, b_s, vec_nf, w_s, scalar):
    d = vec_nv.shape[-1]
    groups, chunk, _ = w_s.shape
    assert groups == SLAB and groups * chunk == d
    slabs = [_slab(vec_nv, d), jnp.pad(b_s, ((0, 0), (0, d - chunk))), _slab(vec_nf, d),
             jnp.pad(scalar, ((0, SLAB - 1), (0, d - 1)))]
    return jnp.concatenate(slabs + [jnp.swapaxes(w_s, 0, 1).reshape(chunk, d)], axis=0)


def kernel(x, norm_in, w_in, norm_v, w_s, b_s, w_o_gmlp, w_o_sb, w_out, norm_final, loss_target, m_norm_in, m_w_in, m_norm_v, m_w_s, m_b_s, m_w_o_gmlp, m_w_o_sb, m_w_out, m_norm_final, v_norm_in, v_w_in, v_norm_v, v_w_s, v_b_s, v_w_o_gmlp, v_w_o_sb, v_w_out, v_norm_final):
    d = x.shape[-1]
    ncol = w_in.shape[-1]
    nrow = w_o_gmlp.shape[-2]
    chip = 2 * lax.axis_index("x") + lax.axis_index("y")

    bsz, seq, _ = x.shape
    x2d = x.reshape(bsz * seq, d)
    shards = [w_in[0], w_o_gmlp[0], w_o_sb[0], w_out[0]]
    halves = [s16.reshape(2, s16.shape[0] // 2, s16.shape[1]) for s16 in _cast_bf16(shards)]
    proj, qkv, h_t, (wg, w_og, w_osb, w_o) = _in_proj_fwd(x2d, norm_in, halves)
    wg = wg.reshape(N_CHIP, d, ncol)

    loss, pieces, dx2, g_nv, g_ws, g_bs, g_wog, g_wosb, g_wout, g_nf = _local_step(
        proj, qkv, x2d, loss_target.reshape(bsz * seq, d), bsz, seq, norm_v, w_s[0], b_s[0],
        w_og.reshape(d, d), w_osb.reshape(d, d), w_o.reshape(d, d), norm_final)

    mats = [g_wog, g_wosb, g_wout]
    mats16 = [g16.reshape(N_CHIP, nrow, d) for g16 in _cast_bf16(mats)]
    g_win, from_sibling, recv_mats, slots_tail = _in_proj_bwd_w(
        h_t, pieces, mats16, _pack_tail(g_nv, g_bs, g_nf, g_ws, loss))
    core = lax.axis_index("c")
    half_own, half_win16 = _chip_half(g_win, from_sibling, core, chip, ncol)
    grad_x, g_nin, recv_win = _in_proj_bwd_x(pieces, wg, x2d, norm_in, dx2, half_win16)
    grad_x = grad_x.reshape(bsz, seq, d)

    sums = _add_received([half_own], [recv_win], jnp.zeros((), jnp.int32), True) + _add_received(
        mats, recv_mats, chip, False)
    *sibling_sums, slots_head = _swap_and_gather(sums, _slab(g_nin, d))
    stats = [_adamw_halves(w_in[0], m_w_in[0], v_w_in[0], sums[0], sibling_sums[0], core)] + _adamw_pairs(
        shards[1:], [m_w_o_gmlp[0], m_w_o_sb[0], m_w_out[0]], [v_w_o_gmlp[0], v_w_o_sb[0], v_w_out[0]],
        sums[1:], sibling_sums[1:])

    *small, loss = _adamw_small(
        [norm_in, norm_v, b_s[0], norm_final.reshape(1, d), w_s[0]],
        [m_norm_in, m_norm_v, m_b_s[0], m_norm_final.reshape(1, d), m_w_s[0]],
        [v_norm_in, v_norm_v, v_b_s[0], v_norm_final.reshape(1, d), v_w_s[0]], slots_head, slots_tail)

    out = []
    for kind, (win, wog, wosb, wout) in enumerate(zip(*stats)):
        nin, nv, bs, nf, ws = small[kind::4]
        out += [nin, win[None], nv, ws[None], bs[None], wog[None], wosb[None], wout[None], nf.reshape(d)]
    return (loss.reshape(()), grad_x, *out)
```

```python
import functools
import math

import jax
import jax.numpy as jnp
from jax import lax
from jax.experimental import pallas as pl
from jax.experimental.pallas import tpu as pltpu

F32 = jnp.float32
BF16 = jnp.bfloat16
EPS = 1e-6
HEADS = 8
N_SPLIT = 9
N_CHIP = 4
MESH = pl.DeviceIdType.MESH

ADAM_LR = 0.001
ADAM_B1 = 0.9
ADAM_B2 = 0.999
ADAM_EPS = 1e-08
ADAM_WD = 0.01
ADAM_STEP = 10

VMEM_LIMIT = 56 * 2 ** 20
TM = 256
TMF = 512
TMX = 512
ATT_T = 256
ATT_HP = 4
TKW = 1024
CHUNKS_PER_STEP = 4
CAST_STEPS = 8
HT_SLOTS = 3

NT = (((1,), (1,)), ((), ()))
TN = (((0,), (0,)), ((), ()))


def _params(sem):
    return pltpu.CompilerParams(dimension_semantics=sem, vmem_limit_bytes=VMEM_LIMIT)


def _resident(shape):
    nd = len(shape)
    return pl.BlockSpec(shape, lambda *_: (0,) * nd, pipeline_mode=pl.Buffered(1))


def _const(shape):
    nd = len(shape)
    return pl.BlockSpec(shape, lambda *_: (0,) * nd)


def _segments(d, ncol):
    segs = []
    edges = sorted({j * ncol for j in range(N_CHIP + 1)} | {s * d for s in range(N_SPLIT + 1)})
    for lo, hi in zip(edges[:-1], edges[1:]):
        segs.append((lo // ncol, lo % ncol, lo // d, lo % d, hi - lo))
    return segs


def _sigmoid(x):
    return 0.5 * jnp.tanh(0.5 * x) + 0.5


_GELU_C = math.sqrt(2.0 / math.pi)


_GELU_CA = _GELU_C * 0.044715


def _gelu(x):
    return x * (0.5 * jnp.tanh(x * (_GELU_C + _GELU_CA * (x * x))) + 0.5)


def _gelu_and_grad(x):
    x2 = x * x
    u = 0.5 * jnp.tanh(x * (_GELU_C + _GELU_CA * x2)) + 0.5
    slope = (1.0 - u) * (x * (_GELU_C + (3.0 * _GELU_CA) * x2))
    return x * u, u * (2.0 * slope + 1.0)


def _split_bf16(a):
    hi = a.astype(BF16)
    lo = (a - hi.astype(F32)).astype(BF16)
    return hi, lo


def _in_proj_fwd(x2d, g_in, halves):
    t, d = x2d.shape
    n = len(halves)
    ncol = halves[0].shape[2]
    n_row = t // TMF
    last = n_row - 1
    assert halves[0].shape[1] * 2 == d
    qkv_parts = {j: (max(j * ncol, 3 * d) - j * ncol, max(j * ncol, 3 * d) - 3 * d)
                 for j in range(N_CHIP) if min((j + 1) * ncol, 6 * d) > max(j * ncol, 3 * d)}
    qkv_w = 3 * d // len(qkv_parts)
    assert all(min((j + 1) * ncol, 6 * d) - max(j * ncol, 3 * d) == qkv_w and q0 % qkv_w == 0
               for j, (_, q0) in qkv_parts.items())

    def body(order_ref, qrow_ref, qcol_ref, x_ref, g_ref, *rest):
        ins = rest[:n]
        proj_ref, qkv_ref, ht_ref = rest[n:n + 3]
        outs = rest[n + 3:2 * n + 3]
        wbuf, h_all, send_sems, recv_sems, local_sems, load_sem = rest[2 * n + 3:]
        phase = pl.program_id(0)
        i = pl.program_id(1)
        x_pos, y_pos, c_pos, chips = _place()
        sibling = (x_pos, y_pos, 1 - c_pos)
        me = (x_pos, y_pos, c_pos)
        my_chip = 2 * x_pos + y_pos

        def copy(a, k, block, to, src=None):
            return pltpu.make_async_remote_copy(
                src_ref=outs[a].at[block] if src is None else src, dst_ref=outs[a].at[block],
                send_sem=send_sems.at[a, k], recv_sem=recv_sems.at[a, k], device_id=to, device_id_type=MESH)

        def local(a):
            return pltpu.make_async_copy(ins[a], outs[a].at[pl.ds(2 * my_chip, 2)], local_sems.at[a])

        def load(src, first, slot):
            for half in range(2):
                cp = pltpu.make_async_copy(src.at[first + half], wbuf.at[slot, pl.ds(half * (d // 2), d // 2)], load_sem)
                cp.start()
                cp.wait()

        def relay(a, k, block, piece, to):
            rows = halves[a].shape[1] // 2
            ref = outs[a].at[block, pl.ds(piece * rows, rows)]
            return pltpu.make_async_remote_copy(
                src_ref=ref, dst_ref=ref, send_sem=send_sems.at[a, k], recv_sem=recv_sems.at[a, k],
                device_id=to, device_id_type=MESH)

        x_nbr, y_nbr, diagonal = chips
        first_block = lambda chip_xy: 2 * (2 * chip_xy[0] + chip_xy[1])

        def neighbours_arrived(arrays):
            from_x, from_y = first_block(x_nbr) + c_pos, first_block(y_nbr) + c_pos
            for a in arrays:
                copy(a, 0, from_x, me).wait_recv()
                copy(a, 1, from_y, me).wait_recv()
                relay(a, 2, from_x, 0, (*y_nbr, c_pos)).start()
                relay(a, 3, from_y, 1, (*x_nbr, c_pos)).start()
                copy(a, 4, from_x, sibling).start()
                copy(a, 5, from_y, sibling).start()

        def diagonal_arrived(arrays):
            from_diagonal = first_block(diagonal) + c_pos
            for a in arrays:
                relay(a, 2, from_diagonal, 0, me).wait_recv()
                relay(a, 3, from_diagonal, 1, me).wait_recv()
                copy(a, 6, from_diagonal, sibling).start()

        def from_sibling(a, k, chip_xy):
            copy(a, k, first_block(chip_xy) + 1 - c_pos, me).wait_recv()

        @pl.when((phase == 0) & (i == 0))
        def _():
            for a in range(n):
                local(a).start()
            for k, (px, py) in enumerate((x_nbr, y_nbr)):
                for a in range(n):
                    copy(a, k, 2 * my_chip + c_pos, (px, py, c_pos), src=ins[a].at[c_pos]).start()
            load(ins[0], 0, 0)

        @pl.when((phase == 1) & (i == 0))
        def _():
            neighbours_arrived([0])
            from_sibling(0, 4, x_nbr)
            load(outs[0], first_block(x_nbr), 1)

        @pl.when((phase == 2) & (i == 0))
        def _():
            from_sibling(0, 5, y_nbr)
            load(outs[0], first_block(y_nbr), 0)
            neighbours_arrived(range(1, n))

        @pl.when((phase == 3) & (i == 0))
        def _():
            diagonal_arrived(range(n))
            from_sibling(0, 6, diagonal)
            load(outs[0], first_block(diagonal), 1)

        @pl.when(phase == 0)
        def _():
            x = x_ref[...]
            r = lax.rsqrt(jnp.mean(x * x, axis=-1, keepdims=True) + EPS)
            hf = x * r * g_ref[...]
            h_all[i] = hf.astype(BF16)
            ht_ref[...] = hf.T.astype(BF16)

        for slot in range(2):
            @pl.when(phase % 2 == slot)
            def _(slot=slot):
                proj_ref[...] = jnp.dot(h_all[i], wbuf[slot], preferred_element_type=F32)

        for chunk, (c0, _) in qkv_parts.items():
            @pl.when(order_ref[phase] == chunk)
            def _(c0=c0):
                qkv_ref[...] = proj_ref[:, c0:c0 + qkv_w].astype(BF16)

        @pl.when((phase == N_CHIP - 1) & (i == n_row - 1))
        def _():
            for a in range(1, n):
                for k, chip_xy in ((4, x_nbr), (5, y_nbr), (6, diagonal)):
                    from_sibling(a, k, chip_xy)
            for a in range(n):
                for k in (0, 1, 4, 5, 6):
                    copy(a, k, 0, me).wait_send()
                for k in (2, 3):
                    relay(a, k, 0, 0, me).wait_send()
                local(a).wait()

    x_pos, y_pos = lax.axis_index("x"), lax.axis_index("y")
    order = jnp.stack([2 * x_pos + y_pos, 2 * (1 - x_pos) + y_pos, 2 * x_pos + 1 - y_pos,
                       2 * (1 - x_pos) + 1 - y_pos]).astype(jnp.int32)
    holds = [functools.reduce(jnp.logical_or, [order[p] == j for j in qkv_parts]) for p in range(N_CHIP)]
    col = [sum(jnp.where(order[p] == j, q0 // qkv_w, 0) for j, (_, q0) in qkv_parts.items()) for p in range(N_CHIP)]
    cur = col[-1]
    for p in reversed(range(N_CHIP - 1)):
        cur = jnp.where(holds[p], col[p], cur)
    seen = jnp.bool_(False)
    qrow, qcol = [], []
    for p in range(N_CHIP):
        cur = jnp.where(holds[p], col[p], cur)
        qrow.append(jnp.where(holds[p], -1, jnp.where(seen, last, 0)))
        qcol.append(cur)
        seen = seen | holds[p]
    qrow = jnp.stack(qrow).astype(jnp.int32)
    qcol = jnp.stack(qcol).astype(jnp.int32)

    outs = pl.pallas_call(
        body, name="in_proj_fwd",
        grid_spec=pltpu.PrefetchScalarGridSpec(
            num_scalar_prefetch=3, grid=(N_CHIP, n_row),
            in_specs=[pl.BlockSpec((TMF, d), lambda p, i, order, qrow, qcol: (jnp.where(p == 0, i, last), 0)),
                      pl.BlockSpec((1, d), lambda p, i, order, qrow, qcol: (0, 0))] + [ANY] * n,
            out_specs=[pl.BlockSpec((TMF, ncol), lambda p, i, order, qrow, qcol: (i, order[p])),
                       pl.BlockSpec((TMF, qkv_w),
                                    lambda p, i, order, qrow, qcol: (jnp.where(qrow[p] < 0, i, qrow[p]), qcol[p])),
                       pl.BlockSpec((d, TMF), lambda p, i, order, qrow, qcol: (0, jnp.where(p == 0, i, last)))]
            + [ANY] * n,
            scratch_shapes=[pltpu.VMEM((2, d, ncol), BF16), pltpu.VMEM((n_row, TMF, d), BF16),
                            pltpu.SemaphoreType.DMA((n, 7)), pltpu.SemaphoreType.DMA((n, 7)),
                            pltpu.SemaphoreType.DMA((n,)), pltpu.SemaphoreType.DMA]),
        out_shape=[jax.ShapeDtypeStruct((t, N_CHIP * ncol), F32), jax.ShapeDtypeStruct((t, 3 * d), BF16),
                   jax.ShapeDtypeStruct((d, t), BF16)]
        + [jax.ShapeDtypeStruct((2 * N_CHIP,) + hv.shape[1:], BF16) for hv in halves],
        compiler_params=_params(("arbitrary", "arbitrary")),
    )(order, qrow, qcol, x2d, g_in, *halves)
    return outs[0], outs[1], outs[2], outs[3:]


def _branch_a_fwd(a_pre, g_v, wm, b_t):
    t = a_pre.shape[0]
    d = g_v.shape[1]
    d3 = 3 * d
    ng, chunk, _ = wm.shape
    cw = d // ng

    per_step = CHUNKS_PER_STEP if t % (CHUNKS_PER_STEP * chunk) == 0 else 1

    def body(a_ref, gv_ref, wm_ref, bt_ref, ya_ref):
        for n in range(per_step):
            rows = slice(n * chunk, (n + 1) * chunk)
            ua = _gelu(a_ref[rows, 0:d])
            vg = _gelu(a_ref[rows, d:2 * d])
            za = a_ref[rows, 2 * d:3 * d]
            rv = lax.rsqrt(jnp.mean(vg * vg, axis=-1, keepdims=True) + EPS)
            va = (vg * rv * gv_ref[...]).astype(BF16)
            gate = ua * (za * _sigmoid(za))
            for g in range(ng):
                sl = slice(g * cw, (g + 1) * cw)
                mixed = jnp.dot(wm_ref[g], va[:, sl], preferred_element_type=F32) + bt_ref[:, g:g + 1]
                ya_ref[rows, sl] = (gate[:, sl] * mixed).astype(BF16)

    tile = per_step * chunk
    return pl.pallas_call(
        body, name="branch_a_fwd",
        grid=(t // tile,),
        in_specs=[pl.BlockSpec((tile, d3), lambda i: (i, 0)), _const((1, d)), _const(wm.shape), _const(b_t.shape)],
        out_specs=pl.BlockSpec((tile, d), lambda i: (i, 0)),
        out_shape=jax.ShapeDtypeStruct((t, d), BF16),
        compiler_params=_params(("arbitrary",)),
    )(a_pre, g_v, wm, b_t)


def _branch_a_bwd(a_pre, dya, g_v, wm, wm_t, b_t):
    t = a_pre.shape[0]
    d = g_v.shape[1]
    d3 = 3 * d
    ng, chunk, _ = wm.shape
    cw = d // ng
    tile = CHUNKS_PER_STEP * chunk
    nsteps = t // tile

    def body(a_ref, dya_ref, gv_ref, wm_ref, wmt_ref, bt_ref, da_ref, gws_ref, gbt_ref, gnv_ref, db_acc):
        i = pl.program_id(0)

        @pl.when(i == 0)
        def _():
            gws_ref[...] = jnp.zeros_like(gws_ref)
            gnv_ref[...] = jnp.zeros_like(gnv_ref)
            db_acc[...] = jnp.zeros_like(db_acc)

        def one_chunk(c, carry):
            rows = pl.ds(pl.multiple_of(c * chunk, chunk), chunk)
            ua, dgelu_u = _gelu_and_grad(a_ref[rows, 0:d])
            vg, dgelu_v = _gelu_and_grad(a_ref[rows, d:2 * d])
            za = a_ref[rows, 2 * d:3 * d]
            sig = _sigmoid(za)
            sz = za * sig
            dsz = sig * (1.0 + za * (1.0 - sig))
            rv = lax.rsqrt(jnp.mean(vg * vg, axis=-1, keepdims=True) + EPS)
            nv = vg * rv
            gv = gv_ref[...]
            va = (nv * gv).astype(BF16)
            dya = dya_ref[rows, :]
            dmix = dya * ua * sz
            db_acc[...] += dmix
            dmix_b = dmix.astype(BF16)
            t_gate = dya * sz
            t_z = dya * ua * dsz
            dva_parts = []
            for g in range(ng):
                sl = slice(g * cw, (g + 1) * cw)
                mixed = jnp.dot(wm_ref[g], va[:, sl], preferred_element_type=F32) + bt_ref[:, g:g + 1]
                da_ref[rows, sl] = (t_gate[:, sl] * mixed * dgelu_u[:, sl]).astype(BF16)
                da_ref[rows, 2 * d + g * cw:2 * d + (g + 1) * cw] = (t_z[:, sl] * mixed).astype(BF16)
                gws_ref[g] += lax.dot_general(dmix_b[:, sl], va[:, sl], NT, preferred_element_type=F32)
                dva_parts.append(jnp.dot(wmt_ref[g], dmix_b[:, sl], preferred_element_type=F32))
            dva = jnp.concatenate(dva_parts, axis=1)
            gnv_ref[...] += jnp.sum(dva * nv, axis=0, keepdims=True)
            dnv = dva * gv
            dvg = rv * (dnv - nv * jnp.mean(dnv * nv, axis=-1, keepdims=True))
            da_ref[rows, d:2 * d] = (dvg * dgelu_v).astype(BF16)
            return carry

        lax.fori_loop(0, CHUNKS_PER_STEP, one_chunk, 0)

        @pl.when(i == nsteps - 1)
        def _():
            acc = db_acc[...]
            for g in range(ng):
                gbt_ref[:, g:g + 1] = jnp.sum(acc[:, g * cw:(g + 1) * cw], axis=1, keepdims=True)

    return pl.pallas_call(
        body, name="branch_a_bwd",
        grid=(nsteps,),
        in_specs=[pl.BlockSpec((tile, d3), lambda i: (i, 0)), pl.BlockSpec((tile, d), lambda i: (i, 0)),
                  _const((1, d)), _const(wm.shape), _const(wm_t.shape), _const(b_t.shape)],
        out_specs=[pl.BlockSpec((tile, d3), lambda i: (i, 0)), _const(wm.shape), _const(b_t.shape), _const((1, d))],
        out_shape=[jax.ShapeDtypeStruct((t, d3), BF16), jax.ShapeDtypeStruct(wm.shape, F32),
                   jax.ShapeDtypeStruct(b_t.shape, F32), jax.ShapeDtypeStruct((1, d), F32)],
        scratch_shapes=[pltpu.VMEM((chunk, d), F32)],
        compiler_params=_params(("arbitrary",)),
    )(a_pre, dya, g_v, wm, wm_t, b_t)


def _below_diagonal(n):
    return lax.broadcasted_iota(jnp.int32, (n, n), 0) > lax.broadcasted_iota(jnp.int32, (n, n), 1)


def _twice(tri):
    t = tri.astype(BF16)
    return jnp.concatenate([t, t], axis=0)


def _cumsum_mm(a, tri2):
    hi, lo = _split_bf16(a)
    return jnp.dot(jnp.concatenate([hi, lo], axis=1), tri2, preferred_element_type=F32)


LOG2E = 1.4426950408889634
_SIGN = 0x80000000


def _sb_block(q, k, scale, upper2, causal):
    z2 = lax.dot_general(q, k, NT, preferred_element_type=F32) * (scale * LOG2E)
    neg_abs = lax.bitcast_convert_type(lax.bitcast_convert_type(z2, jnp.uint32) | jnp.uint32(_SIGN), F32)
    l2 = jnp.log(1.0 + jnp.exp2(neg_abs)) * LOG2E
    log_beta = jnp.minimum(z2, 0.0) - l2
    lom = log_beta - z2
    if causal is not None:
        lom = jnp.where(causal, lom, 0.0)
    sx = _cumsum_mm(lom, upper2)
    return log_beta, sx, sx[:, 0:1] + lom[:, 0:1]


DEAD_LOG2 = -150.0


def _max_carry(carries):
    return jnp.max(functools.reduce(jnp.maximum, carries))


ZB_GROUP, GA_GROUP, GB_GROUP = 6, 7, 8


def _attn_specs(d, seq, nq, heads_per_step):
    hp_w = heads_per_step * (d // HEADS)
    n_hp = d // hp_w
    row_blk = lambda group: pl.BlockSpec((ATT_T, hp_w), lambda b, h, i: (b * nq + i, group * n_hp + h))
    seq_blk = lambda group: pl.BlockSpec((seq, hp_w), lambda b, h, i: (b, group * n_hp + h))
    return row_blk, seq_blk, n_hp


def _attn_fwd(qkv, proj, bsz, seq):
    t, d3 = qkv.shape
    d = d3 // 3
    hd = d // HEADS
    nq = seq // ATT_T
    scale = hd ** -0.5
    n_heads = ATT_HP
    row_blk, seq_blk, n_hp = _attn_specs(d, seq, nq, n_heads)

    def body(q_ref, k_ref, v_ref, zb_ref, o_ref, yb_ref):
        i = pl.program_id(2)
        causal = _below_diagonal(ATT_T)
        upper2 = _twice(causal)

        def step(kb, state, mask):
            rows = pl.ds(pl.multiple_of(kb * ATT_T, ATT_T), ATT_T)
            heads = [slice(h * hd, (h + 1) * hd) for h in range(n_heads)]
            scores = [_sb_block(q_ref[:, cols], k_ref[rows, cols], scale, upper2, mask) for cols in heads]
            new = []
            for cols, (carry, acc), (log_beta, sx, total) in zip(heads, state, scores):
                a = jnp.exp2(log_beta + sx + carry)
                if mask is not None:
                    a = jnp.where(mask, a, 0.0)
                acc = acc + jnp.dot(a.astype(BF16), v_ref[rows, cols], preferred_element_type=F32)
                new.append((carry + total, acc))
            return tuple(new)

        init = tuple((jnp.zeros((ATT_T, 1), F32), jnp.zeros((ATT_T, hd), F32)) for _ in range(n_heads))
        state = step(i, init, causal)
        def more(c):
            new = step(c[0], c[1], None)
            return c[0] - 1, new, _max_carry([s[0] for s in new])

        _, state, _ = lax.while_loop(lambda c: (c[0] >= 0) & (c[2] > DEAD_LOG2), more,
                                     (i - 1, state, _max_carry([s[0] for s in state])))
        for h in range(n_heads):
            cols = slice(h * hd, (h + 1) * hd)
            acc = state[h][1]
            zb = zb_ref[:, cols]
            o_ref[:, cols] = acc
            yb_ref[:, cols] = (acc * (zb * _sigmoid(zb))).astype(BF16)

    return pl.pallas_call(
        body, name="attn_fwd",
        grid=(bsz, n_hp, nq),
        in_specs=[row_blk(0), seq_blk(1), seq_blk(2), row_blk(ZB_GROUP)],
        out_specs=[row_blk(0), row_blk(0)],
        out_shape=[jax.ShapeDtypeStruct((t, d), F32), jax.ShapeDtypeStruct((t, d), BF16)],
        compiler_params=_params(("arbitrary", "arbitrary", "arbitrary")),
    )(qkv, qkv, qkv, proj)


def _attn_bwd(qkv, proj, o, dyb, bsz, seq):
    t, d3 = qkv.shape
    d = d3 // 3
    hd = d // HEADS
    nq = seq // ATT_T
    scale = hd ** -0.5
    row_blk, seq_blk, n_hp = _attn_specs(d, seq, nq, ATT_HP)

    def body(q_ref, k_ref, v_ref, zb_ref, o_ref, dyb_ref, dq_ref, dk_ref, dv_ref, dzb_ref,
             g_s, beta_s, dkt_acc, dvt_acc):
        i = pl.program_id(2)

        @pl.when(i == 0)
        def _():
            dkt_acc[...] = jnp.zeros_like(dkt_acc)
            dvt_acc[...] = jnp.zeros_like(dvt_acc)

        causal = _below_diagonal(ATT_T)
        upper2 = _twice(causal)
        lower2 = _twice(~causal)
        zb = zb_ref[...]
        sig = _sigmoid(zb)
        dyb_t = dyb_ref[...]
        do_f = dyb_t * (zb * sig)
        do = do_f.astype(BF16)
        do_t = do_f.T.astype(BF16)
        q_t = q_ref[...].astype(F32).T.astype(BF16)
        dzb_ref[...] = (dyb_t * o_ref[...] * (sig * (1.0 + zb * (1.0 - sig)))).astype(BF16)

        def sweep(kb, carries, mask):
            rows = pl.ds(pl.multiple_of(kb * ATT_T, ATT_T), ATT_T)
            heads = [slice(h * hd, (h + 1) * hd) for h in range(ATT_HP)]
            scores = [_sb_block(q_ref[:, cols], k_ref[rows, cols], scale, upper2, mask) for cols in heads]
            das = [lax.dot_general(do[:, cols], v_ref[rows, cols], NT, preferred_element_type=F32) for cols in heads]
            new = []
            for h, (cols, carry, (log_beta, sx, total), da) in enumerate(zip(heads, carries, scores, das)):
                a = jnp.exp2(log_beta + sx + carry)
                beta = jnp.exp2(log_beta)
                if mask is not None:
                    a = jnp.where(mask, a, 0.0)
                    beta = jnp.where(mask, beta, 0.0)
                g_s[h, kb] = a * da
                beta_s[h, kb] = beta
                dvt_acc[kb, cols, :] += jnp.dot(do_t[cols, :], a.astype(BF16), preferred_element_type=F32)
                new.append(carry + total)
            return tuple(new)

        carries = sweep(i, tuple(jnp.zeros((ATT_T, 1), F32) for _ in range(ATT_HP)), causal)

        def more(c):
            new = sweep(c[0], c[1], None)
            return c[0] - 1, new, _max_carry(new)

        last, _, _ = lax.while_loop(lambda c: (c[0] >= 0) & (c[2] > DEAD_LOG2), more, (i - 1, carries, _max_carry(carries)))
        first_kb = last + 1

        def back(kb, state):
            rows = pl.ds(pl.multiple_of(kb * ATT_T, ATT_T), ATT_T)
            heads = [slice(h * hd, (h + 1) * hd) for h in range(ATT_HP)]
            sums = [_cumsum_mm(g_s[h, kb], lower2) for h in range(ATT_HP)]
            new = []
            for h, (cols, (p_carry, dq), px) in enumerate(zip(heads, state, sums)):
                dz = ((g_s[h, kb] - (p_carry + px) * beta_s[h, kb]) * scale).astype(BF16)
                dq = dq + jnp.dot(dz, k_ref[rows, cols], preferred_element_type=F32)
                dkt_acc[kb, cols, :] += jnp.dot(q_t[cols, :], dz, preferred_element_type=F32)
                new.append((p_carry + px[:, ATT_T - 1:ATT_T], dq))
            return tuple(new)

        init = tuple((jnp.zeros((ATT_T, 1), F32), jnp.zeros((ATT_T, hd), F32)) for _ in range(ATT_HP))
        state = lax.fori_loop(first_kb, i + 1, back, init)
        for h in range(ATT_HP):
            dq_ref[:, h * hd:(h + 1) * hd] = state[h][1].astype(BF16)

        @pl.when(i == nq - 1)
        def _():
            for kb in range(nq):
                dk_ref[kb * ATT_T:(kb + 1) * ATT_T, :] = dkt_acc[kb].T.astype(BF16)
                dv_ref[kb * ATT_T:(kb + 1) * ATT_T, :] = dvt_acc[kb].T.astype(BF16)

    out = jax.ShapeDtypeStruct((t, d), BF16)
    hp_w = ATT_HP * hd
    return pl.pallas_call(
        body, name="attn_bwd",
        grid=(bsz, n_hp, nq),
        in_specs=[row_blk(0), seq_blk(1), seq_blk(2), row_blk(ZB_GROUP), row_blk(0), row_blk(0)],
        out_specs=[row_blk(0), seq_blk(0), seq_blk(0), row_blk(0)],
        out_shape=[out, out, out, out],
        scratch_shapes=[pltpu.VMEM((ATT_HP, nq, ATT_T, ATT_T), F32), pltpu.VMEM((ATT_HP, nq, ATT_T, ATT_T), F32),
                        pltpu.VMEM((nq, hp_w, ATT_T), F32), pltpu.VMEM((nq, hp_w, ATT_T), F32)],
        compiler_params=_params(("arbitrary", "arbitrary", "arbitrary")),
    )(qkv, qkv, qkv, proj, o, dyb)


def _out_proj(ya, yb, g_pre, x2d, tgt, w_og, w_osb, w_out, g_f):
    t, d = x2d.shape

    def body(ya_ref, yb_ref, ga_ref, gb_ref, x_ref, tgt_ref, wog_ref, wosb_ref, wout_ref, gf_ref,
             dya_ref, dyb_ref, dg_ref, dx2_ref, loss_ref, gnf_ref, gwog_ref, gwosb_ref, gwout_ref):
        @pl.when(pl.program_id(0) == 0)
        def _():
            loss_ref[...] = jnp.zeros_like(loss_ref)
            gnf_ref[...] = jnp.zeros_like(gnf_ref)
            gwog_ref[...] = jnp.zeros_like(gwog_ref)
            gwosb_ref[...] = jnp.zeros_like(gwosb_ref)
            gwout_ref[...] = jnp.zeros_like(gwout_ref)

        ya = ya_ref[...]
        yb = yb_ref[...]
        pa = jnp.dot(ya, wog_ref[...], preferred_element_type=F32)
        pb = jnp.dot(yb, wosb_ref[...], preferred_element_type=F32)
        sga = _sigmoid(ga_ref[...])
        sgb = _sigmoid(gb_ref[...])
        merged = (sga * pa + sgb * pb).astype(BF16)
        x2 = x_ref[...] + jnp.dot(merged, wout_ref[...], preferred_element_type=F32)
        r2 = lax.rsqrt(jnp.mean(x2 * x2, axis=-1, keepdims=True) + EPS)
        n2 = x2 * r2
        gf = gf_ref[...]
        err = n2 * gf - tgt_ref[...]
        loss_ref[...] += 0.5 * jnp.sum(jnp.sum(err * err, axis=-1, keepdims=True), axis=0, keepdims=True) / d
        dy = err * (1.0 / d)
        gnf_ref[...] += jnp.sum(dy * n2, axis=0, keepdims=True)
        dn = dy * gf
        dx2 = r2 * (dn - n2 * jnp.mean(dn * n2, axis=-1, keepdims=True))
        dx2_ref[...] = dx2
        dx2_b = dx2.astype(BF16)
        dmerged = lax.dot_general(dx2_b, wout_ref[...], NT, preferred_element_type=F32)
        gwout_ref[...] += lax.dot_general(merged, dx2_b, TN, preferred_element_type=F32)
        dg_ref[:, 0:d] = (dmerged * pa * (sga * (1.0 - sga))).astype(BF16)
        dg_ref[:, d:2 * d] = (dmerged * pb * (sgb * (1.0 - sgb))).astype(BF16)
        dpa = (dmerged * sga).astype(BF16)
        dpb = (dmerged * sgb).astype(BF16)
        dya_ref[...] = lax.dot_general(dpa, wog_ref[...], NT, preferred_element_type=F32)
        dyb_ref[...] = lax.dot_general(dpb, wosb_ref[...], NT, preferred_element_type=F32)
        gwog_ref[...] += lax.dot_general(ya, dpa, TN, preferred_element_type=F32)
        gwosb_ref[...] += lax.dot_general(yb, dpb, TN, preferred_element_type=F32)

    row = lambda i: (i, 0)
    return pl.pallas_call(
        body, name="out_proj",
        grid=(t // TM,),
        in_specs=[pl.BlockSpec((TM, d), row), pl.BlockSpec((TM, d), row),
                  pl.BlockSpec((TM, d), lambda i: (i, GA_GROUP)), pl.BlockSpec((TM, d), lambda i: (i, GB_GROUP)),
                  pl.BlockSpec((TM, d), row), pl.BlockSpec((TM, d), row),
                  _resident((d, d)), _resident((d, d)), _resident((d, d)), _const((1, d))],
        out_specs=[pl.BlockSpec((TM, d), row), pl.BlockSpec((TM, d), row), pl.BlockSpec((TM, 2 * d), row),
                   pl.BlockSpec((TM, d), row), _const((1, 1)), _const((1, d)),
                   _const((d, d)), _const((d, d)), _const((d, d))],
        out_shape=[jax.ShapeDtypeStruct((t, d), F32), jax.ShapeDtypeStruct((t, d), F32),
                   jax.ShapeDtypeStruct((t, 2 * d), BF16), jax.ShapeDtypeStruct((t, d), F32),
                   jax.ShapeDtypeStruct((1, 1), F32), jax.ShapeDtypeStruct((1, d), F32),
                   jax.ShapeDtypeStruct((d, d), F32), jax.ShapeDtypeStruct((d, d), F32),
                   jax.ShapeDtypeStruct((d, d), F32)],
        compiler_params=_params(("arbitrary",)),
    )(ya, yb, g_pre, g_pre, x2d, tgt, w_og, w_osb, w_out, g_f)


def _dproj_pieces(d):
    return [(0, 0, 3), (1, 3, 1), (2, 4, 1), (3, 5, 1), (4, 6, 1), (5, 7, 2)]


def _in_proj_bwd_x(pieces, wg, x2d, g_in, dx2, gw16):
    t, d = x2d.shape
    ncol = wg.shape[2]
    segs = _segments(d, ncol)
    layout = _dproj_pieces(d)
    nsteps = t // TMX

    def body(da_ref, dq_ref, dk_ref, dv_ref, dzb_ref, dg_ref, w_ref, x_ref, g_ref, dx2_ref, gw16_ref,
             gx_ref, gn_ref, recv_ref, send_sems, recv_sems):
        x_pos, y_pos, c_pos, chips = _place()

        def share(k, chunk):
            px, py = chips[k]
            return pltpu.make_async_remote_copy(
                src_ref=gw16_ref.at[:, chunk * ncol:(chunk + 1) * ncol], dst_ref=recv_ref.at[k],
                send_sem=send_sems.at[k], recv_sem=recv_sems.at[k], device_id=(px, py, c_pos), device_id_type=MESH)

        @pl.when(pl.program_id(0) == 0)
        def _():
            gn_ref[...] = jnp.zeros_like(gn_ref)
            for k, (px, py) in enumerate(chips):
                for chunk in range(N_CHIP):
                    @pl.when(2 * px + py == chunk)
                    def _(k=k, chunk=chunk):
                        share(k, chunk).start()

        @pl.when(pl.program_id(0) == nsteps - 1)
        def _():
            for k in range(N_CHIP - 1):
                share(k, 0).wait()

        refs = (da_ref, dq_ref, dk_ref, dv_ref, dzb_ref, dg_ref)
        dh = jnp.zeros((TMX, d), F32)
        for chip, c0, grp, s0, width in segs:
            piece, first, _ = next(p for p in layout if p[1] <= grp < p[1] + p[2])
            off = (grp - first) * d + s0
            dh = dh + lax.dot_general(refs[piece][:, off:off + width], w_ref[chip, :, c0:c0 + width], NT,
                                      preferred_element_type=F32)
        x = x_ref[...]
        r = lax.rsqrt(jnp.mean(x * x, axis=-1, keepdims=True) + EPS)
        n = x * r
        gn_ref[...] += jnp.sum(dh * n, axis=0, keepdims=True)
        dn = dh * g_ref[...]
        gx_ref[...] = dx2_ref[...] + r * (dn - n * jnp.mean(dn * n, axis=-1, keepdims=True))

    row = lambda i: (i, 0)
    return pl.pallas_call(
        body, name="in_proj_bwd_x",
        grid=(nsteps,),
        in_specs=[pl.BlockSpec((TMX, p.shape[1]), row) for p in pieces]
        + [_resident(wg.shape), pl.BlockSpec((TMX, d), row), _const((1, d)), pl.BlockSpec((TMX, d), row), ANY],
        out_specs=[pl.BlockSpec((TMX, d), row), _const((1, d)), ANY],
        out_shape=[jax.ShapeDtypeStruct((t, d), F32), jax.ShapeDtypeStruct((1, d), F32),
                   jax.ShapeDtypeStruct((N_CHIP - 1, gw16.shape[0], ncol), BF16)],
        scratch_shapes=[pltpu.SemaphoreType.DMA((N_CHIP - 1,)), pltpu.SemaphoreType.DMA((N_CHIP - 1,))],
        compiler_params=_params(("arbitrary",)),
    )(*pieces, wg, x2d, g_in, dx2, gw16)


def _in_proj_bwd_w(h_t, pieces, mats16, pack):
    d, t = h_t.shape
    nk = t // TKW
    half = d // 2
    layout = _dproj_pieces(d)
    n_mats = len(mats16)
    n_dev = 2 * N_CHIP
    flips = [(dx, dy, dc) for dx in (0, 1) for dy in (0, 1) for dc in (0, 1)][1:]

    def body(ht_ref, da_ref, dq_ref, dk_ref, dv_ref, dzb_ref, dg_ref, *rest):
        mat_refs, pack_ref = rest[:n_mats], rest[n_mats]
        gw_ref, sib_ref = rest[n_mats + 1:n_mats + 3]
        recv_refs, slots_ref = rest[n_mats + 3:2 * n_mats + 3], rest[2 * n_mats + 3]
        acc, stage, ring, ring_sems, mat_send, mat_recv, pack_send, pack_recv, own_sem, stage_send, sib_recv = \
            rest[2 * n_mats + 4:]
        s = pl.program_id(0)
        i = pl.program_id(1)
        x_pos, y_pos, c_pos, chips = _place()
        me = 4 * x_pos + 2 * y_pos + c_pos

        step = s * nk + i

        def fetch(at):
            first = pl.multiple_of((at % nk) * TKW, TKW)
            return pltpu.make_async_copy(ht_ref.at[:, pl.ds(first, TKW)], ring.at[at % HT_SLOTS], ring_sems.at[at % HT_SLOTS])

        @pl.when(step == 0)
        def _():
            for at in range(HT_SLOTS - 1):
                fetch(jnp.int32(at)).start()

        @pl.when(step + HT_SLOTS - 1 < N_SPLIT * nk)
        def _():
            fetch(step + HT_SLOTS - 1).start()

        fetch(step).wait()

        def to_sibling(slot, group):
            return pltpu.make_async_remote_copy(
                src_ref=stage.at[slot], dst_ref=sib_ref.at[:, group * d:(group + 1) * d],
                send_sem=stage_send.at[slot], recv_sem=sib_recv, device_id=(x_pos, y_pos, 1 - c_pos), device_id_type=MESH)

        def exchanges():
            cps = []
            for k, (px, py) in enumerate(chips):
                for a in range(n_mats):
                    cps.append(pltpu.make_async_remote_copy(
                        src_ref=mat_refs[a].at[2 * px + py], dst_ref=recv_refs[a].at[k],
                        send_sem=mat_send.at[a, k], recv_sem=mat_recv.at[a, k],
                        device_id=(px, py, c_pos), device_id_type=MESH))
            for k, (dx, dy, dc) in enumerate(flips):
                peer = (1 - x_pos if dx else x_pos, 1 - y_pos if dy else y_pos, 1 - c_pos if dc else c_pos)
                cps.append(pltpu.make_async_remote_copy(
                    src_ref=pack_ref, dst_ref=slots_ref.at[me], send_sem=pack_send.at[k], recv_sem=pack_recv.at[k],
                    device_id=peer, device_id_type=MESH))
            return cps, pltpu.make_async_copy(pack_ref, slots_ref.at[me], own_sem)

        @pl.when((s == 0) & (i == 0))
        def _():
            cps, own = exchanges()
            own.start()
            for cp in cps:
                cp.start()

        @pl.when(i == 0)
        def _():
            acc[...] = jnp.zeros_like(acc)

        refs = (da_ref, dq_ref, dk_ref, dv_ref, dzb_ref, dg_ref)
        for piece, first, count in layout:
            @pl.when((s >= first) & (s < first + count))
            def _(piece=piece):
                acc[...] += jnp.dot(ring[step % HT_SLOTS], refs[piece][...], preferred_element_type=F32)

        @pl.when(i == nk - 1)
        def _():
            gw_ref[...] = acc[...]
            for slot in range(2):
                @pl.when(s % 2 == slot)
                def _(slot=slot):
                    @pl.when(s >= 2)
                    def _():
                        to_sibling(slot, 0).wait_send()
                    for other in range(2):
                        @pl.when(c_pos == 1 - other)
                        def _(other=other):
                            stage[slot] = acc[other * half:(other + 1) * half, :].astype(BF16)
                    for group in range(N_SPLIT):
                        @pl.when(s == group)
                        def _(group=group):
                            to_sibling(slot, group).start()

        @pl.when((s == N_SPLIT - 1) & (i == nk - 1))
        def _():
            for slot in range(2):
                to_sibling(slot, 0).wait_send()
            pltpu.make_async_remote_copy(
                src_ref=sib_ref, dst_ref=sib_ref, send_sem=stage_send.at[0], recv_sem=sib_recv,
                device_id=(x_pos, y_pos, c_pos), device_id_type=MESH).wait_recv()
            cps, own = exchanges()
            own.wait()
            for cp in cps:
                cp.wait()

    def piece_spec(p, first, count):
        def index(s, i):
            mine = (s >= first) & (s < first + count)
            return jnp.where(mine, i, 0), jnp.where(mine, s - first, 0)
        return pl.BlockSpec((TKW, d), index)

    col_blk = pl.BlockSpec((d, d), lambda s, i: (0, s))
    outs = pl.pallas_call(
        body, name="in_proj_bwd_w",
        grid=(N_SPLIT, nk),
        in_specs=[ANY] + [piece_spec(*p) for p in layout] + [ANY] * (n_mats + 1),
        out_specs=[col_blk, ANY] + [ANY] * (n_mats + 1),
        out_shape=[jax.ShapeDtypeStruct((d, N_SPLIT * d), F32), jax.ShapeDtypeStruct((half, N_SPLIT * d), BF16)]
        + [jax.ShapeDtypeStruct((N_CHIP - 1,) + m.shape[1:], BF16) for m in mats16]
        + [jax.ShapeDtypeStruct((n_dev,) + pack.shape, F32)],
        scratch_shapes=[pltpu.VMEM((d, d), F32), pltpu.VMEM((2, half, d), BF16),
                        pltpu.VMEM((HT_SLOTS, d, TKW), BF16), pltpu.SemaphoreType.DMA((HT_SLOTS,)),
                        pltpu.SemaphoreType.DMA((n_mats, N_CHIP - 1)), pltpu.SemaphoreType.DMA((n_mats, N_CHIP - 1)),
                        pltpu.SemaphoreType.DMA((n_dev - 1,)), pltpu.SemaphoreType.DMA((n_dev - 1,)),
                        pltpu.SemaphoreType.DMA, pltpu.SemaphoreType.DMA((2,)), pltpu.SemaphoreType.DMA],
        compiler_params=_params(("arbitrary", "arbitrary")),
    )(h_t, *pieces, *mats16, pack)
    return outs[0], outs[1], outs[2:2 + n_mats], outs[2 + n_mats]


def _local_step(proj, qkv, x2d, tgt2d, bsz, seq, norm_v, w_s, b_s, w_og, w_osb, w_out, norm_final):
    d = x2d.shape[1]
    chunk = w_s.shape[-1]
    causal = jnp.tril(jnp.ones((chunk, chunk), dtype=bool))
    wm = jnp.where(causal[None], w_s, 0.0).astype(BF16)
    wm_t = jnp.swapaxes(wm, 1, 2)
    b_t = b_s.T

    ya = _branch_a_fwd(proj, norm_v, wm, b_t)
    o, yb = _attn_fwd(qkv, proj, bsz, seq)
    dya, dyb, dg, dx2, loss, g_nf, g_wog, g_wosb, g_wout = _out_proj(
        ya, yb, proj, x2d, tgt2d, w_og, w_osb, w_out, norm_final.reshape(1, d))
    dq, dk, dv, dzb = _attn_bwd(qkv, proj, o, dyb, bsz, seq)
    d_a, g_ws, g_bt, g_nv = _branch_a_bwd(proj, dya, norm_v, wm, wm_t, b_t)
    g_ws = jnp.where(causal[None], g_ws, 0.0)
    return loss, (d_a, dq, dk, dv, dzb, dg), dx2, g_nv, g_ws, g_bt.T, g_wog, g_wosb, g_wout, g_nf


def _row_tile(rows):
    return next(r for r in (128, 64, 32, 16, 8) if rows % r == 0)


def _cast_bf16(arrs):
    n = len(arrs)

    def body(*refs):
        for a_ref, o_ref in zip(refs[:n], refs[n:]):
            o_ref[...] = a_ref[...].astype(BF16)

    specs = [pl.BlockSpec((a.shape[0] // CAST_STEPS, a.shape[1]), lambda i: (i, 0)) for a in arrs]
    return pl.pallas_call(
        body, name="cast_bf16", grid=(CAST_STEPS,),
        in_specs=specs, out_specs=specs,
        out_shape=[jax.ShapeDtypeStruct(a.shape, BF16) for a in arrs],
        compiler_params=_params(("arbitrary",)),
    )(*arrs)


def _chip_half(full, from_sibling, core, chip, tile):
    half, n = from_sibling.shape

    def body(where_ref, own_ref, sib_ref, o32_ref, o16_ref):
        total = own_ref[...] + sib_ref[...].astype(F32)
        o16_ref[...] = total.astype(BF16)

        @pl.when(pl.program_id(0) == where_ref[1])
        def _():
            o32_ref[...] = total

    blk = pl.BlockSpec((half, tile), lambda j, where_ref: (0, j))
    return pl.pallas_call(
        body, name="chip_half",
        grid_spec=pltpu.PrefetchScalarGridSpec(
            num_scalar_prefetch=1, grid=(n // tile,),
            in_specs=[pl.BlockSpec((half, tile), lambda j, where_ref: (where_ref[0], j)), blk],
            out_specs=[pl.BlockSpec((half, tile), lambda j, where_ref: (0, 0)), blk]),
        out_shape=[jax.ShapeDtypeStruct((half, tile), F32), jax.ShapeDtypeStruct((half, n), BF16)],
        compiler_params=_params(("arbitrary",)),
    )(jnp.stack([core, chip]).astype(jnp.int32), full, from_sibling)


def _add_received(fulls, recvs, chip, by_cols):
    n = len(fulls)
    _, rows, cols = recvs[0].shape
    tr = _row_tile(rows)
    nb = rows // tr

    def body(chip_ref, *refs):
        for own_ref, recv_ref, o_ref in zip(refs[:n], refs[n:2 * n], refs[2 * n:]):
            s = own_ref[...]
            for k in range(N_CHIP - 1):
                s = s + recv_ref[k].astype(F32)
            o_ref[...] = s

    own_map = (lambda i, chip_ref: (i, chip_ref[0])) if by_cols else (lambda i, chip_ref: (chip_ref[0] * nb + i, 0))
    return pl.pallas_call(
        body, name="add_received",
        grid_spec=pltpu.PrefetchScalarGridSpec(
            num_scalar_prefetch=1, grid=(nb,),
            in_specs=[pl.BlockSpec((tr, cols), own_map)] * n
            + [pl.BlockSpec((N_CHIP - 1, tr, cols), lambda i, chip_ref: (0, i, 0))] * n,
            out_specs=[pl.BlockSpec((tr, cols), lambda i, chip_ref: (i, 0))] * n),
        out_shape=[jax.ShapeDtypeStruct((rows, cols), F32)] * n,
        compiler_params=_params(("arbitrary",)),
    )(chip.reshape(1).astype(jnp.int32), *fulls, *recvs)


def _adamw_math(w, m, v, g):
    new_m = ADAM_B1 * m + (1.0 - ADAM_B1) * g
    new_v = ADAM_B2 * v + (1.0 - ADAM_B2) * (g * g)
    m_hat = new_m / (1.0 - ADAM_B1 ** ADAM_STEP)
    v_hat = new_v / (1.0 - ADAM_B2 ** ADAM_STEP)
    return -ADAM_LR * (m_hat / (jnp.sqrt(v_hat) + ADAM_EPS) + ADAM_WD * w), new_m, new_v


def _adamw_pairs(ws, ms, vs, mines, theirs):
    n = len(ws)
    rows, cols = ws[0].shape
    tr = _row_tile(rows)

    def body(*refs):
        ins, outs = refs[:5 * n], refs[5 * n:]
        for a in range(n):
            w_ref, m_ref, v_ref, mine_ref, theirs_ref = ins[a::n]
            g_ref, d_ref, nm_ref, nv_ref = outs[4 * a:4 * a + 4]
            g = mine_ref[...] + theirs_ref[...]
            g_ref[...] = g
            d_ref[...], nm_ref[...], nv_ref[...] = _adamw_math(w_ref[...], m_ref[...], v_ref[...], g)

    spec = pl.BlockSpec((tr, cols), lambda i: (i, 0))
    out = jax.ShapeDtypeStruct((rows, cols), F32)
    outs = pl.pallas_call(
        body, name="adamw_pairs", grid=(rows // tr,),
        in_specs=[spec] * (5 * n), out_specs=[spec] * (4 * n), out_shape=[out] * (4 * n),
        compiler_params=_params(("arbitrary",)),
    )(*ws, *ms, *vs, *mines, *theirs)
    return [outs[4 * a:4 * a + 4] for a in range(n)]


def _adamw_halves(w, m, v, mine, theirs, core):
    rows, cols = w.shape
    tr = _row_tile(rows // 2)
    per_half = rows // 2 // tr

    def body(core_ref, w_ref, m_ref, v_ref, mine_ref, theirs_ref, g_ref, d_ref, nm_ref, nv_ref):
        is_mine = pl.program_id(0) // per_half == core_ref[0]
        for part, cond in ((mine_ref, is_mine), (theirs_ref, jnp.logical_not(is_mine))):
            @pl.when(cond)
            def _(part=part):
                g = part[...]
                g_ref[...] = g
                d_ref[...], nm_ref[...], nv_ref[...] = _adamw_math(w_ref[...], m_ref[...], v_ref[...], g)

    spec = pl.BlockSpec((tr, cols), lambda i, core_ref: (i, 0))

    def half_spec(own):
        def index(i, core_ref):
            in_core_half = i // per_half == core_ref[0]
            here = in_core_half if own else jnp.logical_not(in_core_half)
            return jnp.where(here, i % per_half, 0), 0
        return pl.BlockSpec((tr, cols), index)

    out = jax.ShapeDtypeStruct(w.shape, F32)
    return pl.pallas_call(
        body, name="adamw_halves",
        grid_spec=pltpu.PrefetchScalarGridSpec(
            num_scalar_prefetch=1, grid=(rows // tr,),
            in_specs=[spec] * 3 + [half_spec(True), half_spec(False)], out_specs=[spec] * 4),
        out_shape=[out] * 4,
        compiler_params=_params(("arbitrary",)),
    )(core.reshape(1).astype(jnp.int32), w, m, v, mine, theirs)


def _adamw_small(ws, ms, vs, slots_head, slots_tail):
    n_dev = slots_head.shape[0]
    n_par = len(ws)
    groups, chunk, _ = ws[4].shape

    def body(*refs):
        w_refs, m_refs, v_refs = refs[:n_par], refs[n_par:2 * n_par], refs[2 * n_par:3 * n_par]
        head_ref, tail_ref = refs[3 * n_par:3 * n_par + 2]
        out_refs, scalar_ref = refs[3 * n_par + 2:-1], refs[-1]

        def total(ref, rows, cols):
            g = ref[0, rows, cols]
            for i in range(1, n_dev):
                g = g + ref[i, rows, cols]
            return g

        def update(p, g, at=...):
            g_ref, d_ref, nm_ref, nv_ref = out_refs[4 * p:4 * p + 4]
            g_ref[at] = g
            d_ref[at], nm_ref[at], nv_ref[at] = _adamw_math(w_refs[p][at], m_refs[p][at], v_refs[p][at], g)

        every = slice(None)
        update(0, total(head_ref, slice(0, 1), every))
        update(1, total(tail_ref, slice(0, 1), every))
        update(2, total(tail_ref, slice(SLAB, SLAB + groups), slice(0, chunk)))
        update(3, total(tail_ref, slice(2 * SLAB, 2 * SLAB + 1), every))
        scalar_ref[...] = total(tail_ref, slice(3 * SLAB, 3 * SLAB + 1), slice(0, chunk))[:, 0:1]
        for grp in range(groups):
            update(4, total(tail_ref, slice(4 * SLAB, 4 * SLAB + chunk), slice(grp * chunk, (grp + 1) * chunk)), grp)

    vmem = pl.BlockSpec(memory_space=pltpu.VMEM)
    return pl.pallas_call(
        body, name="adamw_small", in_specs=[vmem] * (3 * n_par + 2), out_specs=[vmem] * (4 * n_par + 1),
        out_shape=[jax.ShapeDtypeStruct(w.shape, F32) for w in ws for _ in range(4)]
        + [jax.ShapeDtypeStruct((1, 1), F32)],
        compiler_params=pltpu.CompilerParams(vmem_limit_bytes=VMEM_LIMIT),
    )(*ws, *ms, *vs, slots_head, slots_tail)


ANY = pl.BlockSpec(memory_space=pl.ANY)


def _place():
    x, y, c = lax.axis_index("x"), lax.axis_index("y"), lax.axis_index("c")
    other_chips = [(1 - x, y), (x, 1 - y), (1 - x, 1 - y)]
    return x, y, c, other_chips


def _swap_and_gather(arrs, pack):
    n = len(arrs)
    n_dev = 2 * N_CHIP
    flips = [(dx, dy, dc) for dx in (0, 1) for dy in (0, 1) for dc in (0, 1)][1:]

    def body(*refs):
        ins, pack_ref = refs[:n], refs[n]
        outs, slots_ref = refs[n + 1:2 * n + 1], refs[2 * n + 1]
        send_sems, recv_sems, pack_send, pack_recv, own_sem = refs[2 * n + 2:]
        x, y, c, _ = _place()
        me = 4 * x + 2 * y + c
        own = pltpu.make_async_copy(pack_ref, slots_ref.at[me], own_sem)
        own.start()
        copies = []
        for k, (dx, dy, dc) in enumerate(flips):
            peer = (1 - x if dx else x, 1 - y if dy else y, 1 - c if dc else c)
            copies.append(pltpu.make_async_remote_copy(
                src_ref=pack_ref, dst_ref=slots_ref.at[me], send_sem=pack_send.at[k], recv_sem=pack_recv.at[k],
                device_id=peer, device_id_type=MESH))
        copies += [pltpu.make_async_remote_copy(
            src_ref=ins[a], dst_ref=outs[a], send_sem=send_sems.at[a], recv_sem=recv_sems.at[a],
            device_id=(x, y, 1 - c), device_id_type=MESH) for a in range(n)]
        for cp in copies:
            cp.start()
        for cp in copies:
            cp.wait()
        own.wait()

    return pl.pallas_call(
        body, name="swap_and_gather",
        in_specs=[ANY] * (n + 1), out_specs=[ANY] * (n + 1),
        out_shape=[jax.ShapeDtypeStruct(a.shape, a.dtype) for a in arrs] + [jax.ShapeDtypeStruct((n_dev,) + pack.shape, F32)],
        scratch_shapes=[pltpu.SemaphoreType.DMA((n,)), pltpu.SemaphoreType.DMA((n,)),
                        pltpu.SemaphoreType.DMA((n_dev - 1,)), pltpu.SemaphoreType.DMA((n_dev - 1,)),
                        pltpu.SemaphoreType.DMA],
    )(*arrs, pack)


SLAB = 8


def _slab(vec, d):
    return jnp.pad(vec.reshape(1, d), ((0, SLAB - 1), (0, 0)))


def _pack_tail(vec_nv, b_s, vec_nf, w_s, scalar):
    d = vec_nv.shape[-1]
    groups, chunk, _ = w_s.shape
    assert groups == SLAB and groups * chunk == d
    slabs = [_slab(vec_nv, d), jnp.pad(b_s, ((0, 0), (0, d - chunk))), _slab(vec_nf, d),
             jnp.pad(scalar, ((0, SLAB - 1), (0, d - 1)))]
    return jnp.concatenate(slabs + [jnp.swapaxes(w_s, 0, 1).reshape(chunk, d)], axis=0)


def kernel(x, norm_in, w_in, norm_v, w_s, b_s, w_o_gmlp, w_o_sb, w_out, norm_final, loss_target, m_norm_in, m_w_in, m_norm_v, m_w_s, m_b_s, m_w_o_gmlp, m_w_o_sb, m_w_out, m_norm_final, v_norm_in, v_w_in, v_norm_v, v_w_s, v_b_s, v_w_o_gmlp, v_w_o_sb, v_w_out, v_norm_final):
    d = x.shape[-1]
    ncol = w_in.shape[-1]
    nrow = w_o_gmlp.shape[-2]
    chip = 2 * lax.axis_index("x") + lax.axis_index("y")

    bsz, seq, _ = x.shape
    x2d = x.reshape(bsz * seq, d)
    shards = [w_in[0], w_o_gmlp[0], w_o_sb[0], w_out[0]]
    halves = [s16.reshape(2, s16.shape[0] // 2, s16.shape[1]) for s16 in _cast_bf16(shards)]
    proj, qkv, h_t, (wg, w_og, w_osb, w_o) = _in_proj_fwd(x2d, norm_in, halves)
    wg = wg.reshape(N_CHIP, d, ncol)

    loss, pieces, dx2, g_nv, g_ws, g_bs, g_wog, g_wosb, g_wout, g_nf = _local_step(
        proj, qkv, x2d, loss_target.reshape(bsz * seq, d), bsz, seq, norm_v, w_s[0], b_s[0],
        w_og.reshape(d, d), w_osb.reshape(d, d), w_o.reshape(d, d), norm_final)

    mats = [g_wog, g_wosb, g_wout]
    mats16 = [g16.reshape(N_CHIP, nrow, d) for g16 in _cast_bf16(mats)]
    g_win, from_sibling, recv_mats, slots_tail = _in_proj_bwd_w(
        h_t, pieces, mats16, _pack_tail(g_nv, g_bs, g_nf, g_ws, loss))
    core = lax.axis_index("c")
    half_own, half_win16 = _chip_half(g_win, from_sibling, core, chip, ncol)
    grad_x, g_nin, recv_win = _in_proj_bwd_x(pieces, wg, x2d, norm_in, dx2, half_win16)
    grad_x = grad_x.reshape(bsz, seq, d)

    sums = _add_received([half_own], [recv_win], jnp.zeros((), jnp.int32), True) + _add_received(
        mats, recv_mats, chip, False)
    *sibling_sums, slots_head = _swap_and_gather(sums, _slab(g_nin, d))
    stats = [_adamw_halves(w_in[0], m_w_in[0], v_w_in[0], sums[0], sibling_sums[0], core)] + _adamw_pairs(
        shards[1:], [m_w_o_gmlp[0], m_w_o_sb[0], m_w_out[0]], [v_w_o_gmlp[0], v_w_o_sb[0], v_w_out[0]],
        sums[1:], sibling_sums[1:])

    *small, loss = _adamw_small(
        [norm_in, norm_v, b_s[0], norm_final.reshape(1, d), w_s[0]],
        [m_norm_in, m_norm_v, m_b_s[0], m_norm_final.reshape(1, d), m_w_s[0]],
        [v_norm_in, v_norm_v, v_b_s[0], v_norm_final.reshape(1, d), v_w_s[0]], slots_head, slots_tail)

    out = []
    for kind, (win, wog, wosb, wout) in enumerate(zip(*stats)):
        nin, nv, bs, nf, ws = small[kind::4]
        out += [nin, win[None], nv, ws[None], bs[None], wog[None], wosb[None], wout[None], nf.reshape(d)]
    return (loss.reshape(()), grad_x, *out)
```

```python
import functools
import math

import jax
import jax.numpy as jnp
from jax import lax
from jax.experimental import pallas as pl
from jax.experimental.pallas import tpu as pltpu

F32 = jnp.float32
BF16 = jnp.bfloat16
EPS = 1e-6
HEADS = 8
N_SPLIT = 9
N_CHIP = 4
MESH = pl.DeviceIdType.MESH

ADAM_LR = 0.001
ADAM_B1 = 0.9
ADAM_B2 = 0.999
ADAM_EPS = 1e-08
ADAM_WD = 0.01
ADAM_STEP = 10

VMEM_LIMIT = 56 * 2 ** 20
TM = 256
TMF = 512
TMX = 512
ATT_T = 256
ATT_HP = 4
TKW = 2048
CHUNKS_PER_STEP = 4
CAST_STEPS = 8
HT_SLOTS = 3

NT = (((1,), (1,)), ((), ()))
TN = (((0,), (0,)), ((), ()))


def _params(sem):
    return pltpu.CompilerParams(dimension_semantics=sem, vmem_limit_bytes=VMEM_LIMIT)


def _resident(shape):
    nd = len(shape)
    return pl.BlockSpec(shape, lambda *_: (0,) * nd, pipeline_mode=pl.Buffered(1))


def _const(shape):
    nd = len(shape)
    return pl.BlockSpec(shape, lambda *_: (0,) * nd)


def _segments(d, ncol):
    segs = []
    edges = sorted({j * ncol for j in range(N_CHIP + 1)} | {s * d for s in range(N_SPLIT + 1)})
    for lo, hi in zip(edges[:-1], edges[1:]):
        segs.append((lo // ncol, lo % ncol, lo // d, lo % d, hi - lo))
    return segs


def _sigmoid(x):
    return 0.5 * jnp.tanh(0.5 * x) + 0.5


_GELU_C = math.sqrt(2.0 / math.pi)


_GELU_CA = _GELU_C * 0.044715


def _gelu(x):
    return x * (0.5 * jnp.tanh(x * (_GELU_C + _GELU_CA * (x * x))) + 0.5)


def _gelu_and_grad(x):
    x2 = x * x
    u = 0.5 * jnp.tanh(x * (_GELU_C + _GELU_CA * x2)) + 0.5
    slope = (1.0 - u) * (x * (_GELU_C + (3.0 * _GELU_CA) * x2))
    return x * u, u * (2.0 * slope + 1.0)


def _split_bf16(a):
    hi = a.astype(BF16)
    lo = (a - hi.astype(F32)).astype(BF16)
    return hi, lo


def _in_proj_fwd(x2d, g_in, halves):
    t, d = x2d.shape
    n = len(halves)
    ncol = halves[0].shape[2]
    n_row = t // TMF
    last = n_row - 1
    assert halves[0].shape[1] * 2 == d
    qkv_parts = {j: (max(j * ncol, 3 * d) - j * ncol, max(j * ncol, 3 * d) - 3 * d)
                 for j in range(N_CHIP) if min((j + 1) * ncol, 6 * d) > max(j * ncol, 3 * d)}
    qkv_w = 3 * d // len(qkv_parts)
    assert all(min((j + 1) * ncol, 6 * d) - max(j * ncol, 3 * d) == qkv_w and q0 % qkv_w == 0
               for j, (_, q0) in qkv_parts.items())

    def body(order_ref, qrow_ref, qcol_ref, x_ref, g_ref, *rest):
        ins = rest[:n]
        proj_ref, qkv_ref, ht_ref = rest[n:n + 3]
        outs = rest[n + 3:2 * n + 3]
        wbuf, h_all, send_sems, recv_sems, local_sems, load_sem = rest[2 * n + 3:]
        phase = pl.program_id(0)
        i = pl.program_id(1)
        x_pos, y_pos, c_pos, chips = _place()
        sibling = (x_pos, y_pos, 1 - c_pos)
        me = (x_pos, y_pos, c_pos)
        my_chip = 2 * x_pos + y_pos

        def copy(a, k, block, to, src=None):
            return pltpu.make_async_remote_copy(
                src_ref=outs[a].at[block] if src is None else src, dst_ref=outs[a].at[block],
                send_sem=send_sems.at[a, k], recv_sem=recv_sems.at[a, k], device_id=to, device_id_type=MESH)

        def local(a):
            return pltpu.make_async_copy(ins[a], outs[a].at[pl.ds(2 * my_chip, 2)], local_sems.at[a])

        def load(src, first, slot):
            for half in range(2):
                cp = pltpu.make_async_copy(src.at[first + half], wbuf.at[slot, pl.ds(half * (d // 2), d // 2)], load_sem)
                cp.start()
                cp.wait()

        def relay(a, k, block, piece, to):
            rows = halves[a].shape[1] // 2
            ref = outs[a].at[block, pl.ds(piece * rows, rows)]
            return pltpu.make_async_remote_copy(
                src_ref=ref, dst_ref=ref, send_sem=send_sems.at[a, k], recv_sem=recv_sems.at[a, k],
                device_id=to, device_id_type=MESH)

        x_nbr, y_nbr, diagonal = chips
        first_block = lambda chip_xy: 2 * (2 * chip_xy[0] + chip_xy[1])

        def neighbours_arrived(arrays):
            from_x, from_y = first_block(x_nbr) + c_pos, first_block(y_nbr) + c_pos
            for a in arrays:
                copy(a, 0, from_x, me).wait_recv()
                copy(a, 1, from_y, me).wait_recv()
                relay(a, 2, from_x, 0, (*y_nbr, c_pos)).start()
                relay(a, 3, from_y, 1, (*x_nbr, c_pos)).start()
                copy(a, 4, from_x, sibling).start()
                copy(a, 5, from_y, sibling).start()

        def diagonal_arrived(arrays):
            from_diagonal = first_block(diagonal) + c_pos
            for a in arrays:
                relay(a, 2, from_diagonal, 0, me).wait_recv()
                relay(a, 3, from_diagonal, 1, me).wait_recv()
                copy(a, 6, from_diagonal, sibling).start()

        def from_sibling(a, k, chip_xy):
            copy(a, k, first_block(chip_xy) + 1 - c_pos, me).wait_recv()

        @pl.when((phase == 0) & (i == 0))
        def _():
            for a in range(n):
                local(a).start()
            for k, (px, py) in enumerate((x_nbr, y_nbr)):
                for a in range(n):
                    copy(a, k, 2 * my_chip + c_pos, (px, py, c_pos), src=ins[a].at[c_pos]).start()
            load(ins[0], 0, 0)

        @pl.when((phase == 1) & (i == 0))
        def _():
            neighbours_arrived([0])
            from_sibling(0, 4, x_nbr)
            load(outs[0], first_block(x_nbr), 1)

        @pl.when((phase == 2) & (i == 0))
        def _():
            from_sibling(0, 5, y_nbr)
            load(outs[0], first_block(y_nbr), 0)
            neighbours_arrived(range(1, n))

        @pl.when((phase == 3) & (i == 0))
        def _():
            diagonal_arrived(range(n))
            from_sibling(0, 6, diagonal)
            load(outs[0], first_block(diagonal), 1)

        @pl.when(phase == 0)
        def _():
            x = x_ref[...]
            r = lax.rsqrt(jnp.mean(x * x, axis=-1, keepdims=True) + EPS)
            hf = x * r * g_ref[...]
            h_all[i] = hf.astype(BF16)
            ht_ref[...] = hf.T.astype(BF16)

        for slot in range(2):
            @pl.when(phase % 2 == slot)
            def _(slot=slot):
                proj_ref[...] = jnp.dot(h_all[i], wbuf[slot], preferred_element_type=F32)

        for chunk, (c0, _) in qkv_parts.items():
            @pl.when(order_ref[phase] == chunk)
            def _(c0=c0):
                qkv_ref[...] = proj_ref[:, c0:c0 + qkv_w].astype(BF16)

        @pl.when((phase == N_CHIP - 1) & (i == n_row - 1))
        def _():
            for a in range(1, n):
                for k, chip_xy in ((4, x_nbr), (5, y_nbr), (6, diagonal)):
                    from_sibling(a, k, chip_xy)
            for a in range(n):
                for k in (0, 1, 4, 5, 6):
                    copy(a, k, 0, me).wait_send()
                for k in (2, 3):
                    relay(a, k, 0, 0, me).wait_send()
                local(a).wait()

    x_pos, y_pos = lax.axis_index("x"), lax.axis_index("y")
    order = jnp.stack([2 * x_pos + y_pos, 2 * (1 - x_pos) + y_pos, 2 * x_pos + 1 - y_pos,
                       2 * (1 - x_pos) + 1 - y_pos]).astype(jnp.int32)
    holds = [functools.reduce(jnp.logical_or, [order[p] == j for j in qkv_parts]) for p in range(N_CHIP)]
    col = [sum(jnp.where(order[p] == j, q0 // qkv_w, 0) for j, (_, q0) in qkv_parts.items()) for p in range(N_CHIP)]
    cur = col[-1]
    for p in reversed(range(N_CHIP - 1)):
        cur = jnp.where(holds[p], col[p], cur)
    seen = jnp.bool_(False)
    qrow, qcol = [], []
    for p in range(N_CHIP):
        cur = jnp.where(holds[p], col[p], cur)
        qrow.append(jnp.where(holds[p], -1, jnp.where(seen, last, 0)))
        qcol.append(cur)
        seen = seen | holds[p]
    qrow = jnp.stack(qrow).astype(jnp.int32)
    qcol = jnp.stack(qcol).astype(jnp.int32)

    outs = pl.pallas_call(
        body, name="in_proj_fwd",
        grid_spec=pltpu.PrefetchScalarGridSpec(
            num_scalar_prefetch=3, grid=(N_CHIP, n_row),
            in_specs=[pl.BlockSpec((TMF, d), lambda p, i, order, qrow, qcol: (jnp.where(p == 0, i, last), 0)),
                      pl.BlockSpec((1, d), lambda p, i, order, qrow, qcol: (0, 0))] + [ANY] * n,
            out_specs=[pl.BlockSpec((TMF, ncol), lambda p, i, order, qrow, qcol: (i, order[p])),
                       pl.BlockSpec((TMF, qkv_w),
                                    lambda p, i, order, qrow, qcol: (jnp.where(qrow[p] < 0, i, qrow[p]), qcol[p])),
                       pl.BlockSpec((d, TMF), lambda p, i, order, qrow, qcol: (0, jnp.where(p == 0, i, last)))]
            + [ANY] * n,
            scratch_shapes=[pltpu.VMEM((2, d, ncol), BF16), pltpu.VMEM((n_row, TMF, d), BF16),
                            pltpu.SemaphoreType.DMA((n, 7)), pltpu.SemaphoreType.DMA((n, 7)),
                            pltpu.SemaphoreType.DMA((n,)), pltpu.SemaphoreType.DMA]),
        out_shape=[jax.ShapeDtypeStruct((t, N_CHIP * ncol), F32), jax.ShapeDtypeStruct((t, 3 * d), BF16),
                   jax.ShapeDtypeStruct((d, t), BF16)]
        + [jax.ShapeDtypeStruct((2 * N_CHIP,) + hv.shape[1:], BF16) for hv in halves],
        compiler_params=_params(("arbitrary", "arbitrary")),
    )(order, qrow, qcol, x2d, g_in, *halves)
    return outs[0], outs[1], outs[2], outs[3:]


def _branch_a_fwd(a_pre, g_v, wm, b_t):
    t = a_pre.shape[0]
    d = g_v.shape[1]
    d3 = 3 * d
    ng, chunk, _ = wm.shape
    cw = d // ng

    per_step = CHUNKS_PER_STEP if t % (CHUNKS_PER_STEP * chunk) == 0 else 1

    def body(a_ref, gv_ref, wm_ref, bt_ref, ya_ref):
        for n in range(per_step):
            rows = slice(n * chunk, (n + 1) * chunk)
            ua = _gelu(a_ref[rows, 0:d])
            vg = _gelu(a_ref[rows, d:2 * d])
            za = a_ref[rows, 2 * d:3 * d]
            rv = lax.rsqrt(jnp.mean(vg * vg, axis=-1, keepdims=True) + EPS)
            va = (vg * rv * gv_ref[...]).astype(BF16)
            gate = ua * (za * _sigmoid(za))
            for g in range(ng):
                sl = slice(g * cw, (g + 1) * cw)
                mixed = jnp.dot(wm_ref[g], va[:, sl], preferred_element_type=F32) + bt_ref[:, g:g + 1]
                ya_ref[rows, sl] = (gate[:, sl] * mixed).astype(BF16)

    tile = per_step * chunk
    return pl.pallas_call(
        body, name="branch_a_fwd",
        grid=(t // tile,),
        in_specs=[pl.BlockSpec((tile, d3), lambda i: (i, 0)), _const((1, d)), _const(wm.shape), _const(b_t.shape)],
        out_specs=pl.BlockSpec((tile, d), lambda i: (i, 0)),
        out_shape=jax.ShapeDtypeStruct((t, d), BF16),
        compiler_params=_params(("arbitrary",)),
    )(a_pre, g_v, wm, b_t)


def _branch_a_bwd(a_pre, dya, g_v, wm, wm_t, b_t):
    t = a_pre.shape[0]
    d = g_v.shape[1]
    d3 = 3 * d
    ng, chunk, _ = wm.shape
    cw = d // ng
    tile = CHUNKS_PER_STEP * chunk
    nsteps = t // tile

    def body(a_ref, dya_ref, gv_ref, wm_ref, wmt_ref, bt_ref, da_ref, gws_ref, gbt_ref, gnv_ref, db_acc):
        i = pl.program_id(0)

        @pl.when(i == 0)
        def _():
            gws_ref[...] = jnp.zeros_like(gws_ref)
            gnv_ref[...] = jnp.zeros_like(gnv_ref)
            db_acc[...] = jnp.zeros_like(db_acc)

        def one_chunk(c, carry):
            rows = pl.ds(pl.multiple_of(c * chunk, chunk), chunk)
            ua, dgelu_u = _gelu_and_grad(a_ref[rows, 0:d])
            vg, dgelu_v = _gelu_and_grad(a_ref[rows, d:2 * d])
            za = a_ref[rows, 2 * d:3 * d]
            sig = _sigmoid(za)
            sz = za * sig
            dsz = sig * (1.0 + za * (1.0 - sig))
            rv = lax.rsqrt(jnp.mean(vg * vg, axis=-1, keepdims=True) + EPS)
            nv = vg * rv
            gv = gv_ref[...]
            va = (nv * gv).astype(BF16)
            dya = dya_ref[rows, :]
            dmix = dya * ua * sz
            db_acc[...] += dmix
            dmix_b = dmix.astype(BF16)
            t_gate = dya * sz
            t_z = dya * ua * dsz
            dva_parts = []
            for g in range(ng):
                sl = slice(g * cw, (g + 1) * cw)
                mixed = jnp.dot(wm_ref[g], va[:, sl], preferred_element_type=F32) + bt_ref[:, g:g + 1]
                da_ref[rows, sl] = (t_gate[:, sl] * mixed * dgelu_u[:, sl]).astype(BF16)
                da_ref[rows, 2 * d + g * cw:2 * d + (g + 1) * cw] = (t_z[:, sl] * mixed).astype(BF16)
                gws_ref[g] += lax.dot_general(dmix_b[:, sl], va[:, sl], NT, preferred_element_type=F32)
                dva_parts.append(jnp.dot(wmt_ref[g], dmix_b[:, sl], preferred_element_type=F32))
            dva = jnp.concatenate(dva_parts, axis=1)
            gnv_ref[...] += jnp.sum(dva * nv, axis=0, keepdims=True)
            dnv = dva * gv
            dvg = rv * (dnv - nv * jnp.mean(dnv * nv, axis=-1, keepdims=True))
            da_ref[rows, d:2 * d] = (dvg * dgelu_v).astype(BF16)
            return carry

        lax.fori_loop(0, CHUNKS_PER_STEP, one_chunk, 0)

        @pl.when(i == nsteps - 1)
        def _():
            acc = db_acc[...]
            for g in range(ng):
                gbt_ref[:, g:g + 1] = jnp.sum(acc[:, g * cw:(g + 1) * cw], axis=1, keepdims=True)

    return pl.pallas_call(
        body, name="branch_a_bwd",
        grid=(nsteps,),
        in_specs=[pl.BlockSpec((tile, d3), lambda i: (i, 0)), pl.BlockSpec((tile, d), lambda i: (i, 0)),
                  _const((1, d)), _const(wm.shape), _const(wm_t.shape), _const(b_t.shape)],
        out_specs=[pl.BlockSpec((tile, d3), lambda i: (i, 0)), _const(wm.shape), _const(b_t.shape), _const((1, d))],
        out_shape=[jax.ShapeDtypeStruct((t, d3), BF16), jax.ShapeDtypeStruct(wm.shape, F32),
                   jax.ShapeDtypeStruct(b_t.shape, F32), jax.ShapeDtypeStruct((1, d), F32)],
        scratch_shapes=[pltpu.VMEM((chunk, d), F32)],
        compiler_params=_params(("arbitrary",)),
    )(a_pre, dya, g_v, wm, wm_t, b_t)


def _below_diagonal(n):
    return lax.broadcasted_iota(jnp.int32, (n, n), 0) > lax.broadcasted_iota(jnp.int32, (n, n), 1)


def _twice(tri):
    t = tri.astype(BF16)
    return jnp.concatenate([t, t], axis=0)


def _cumsum_mm(a, tri2):
    hi, lo = _split_bf16(a)
    return jnp.dot(jnp.concatenate([hi, lo], axis=1), tri2, preferred_element_type=F32)


LOG2E = 1.4426950408889634
_SIGN = 0x80000000


def _sb_block(q, k, scale, upper2, causal):
    z2 = lax.dot_general(q, k, NT, preferred_element_type=F32) * (scale * LOG2E)
    neg_abs = lax.bitcast_convert_type(lax.bitcast_convert_type(z2, jnp.uint32) | jnp.uint32(_SIGN), F32)
    l2 = jnp.log(1.0 + jnp.exp2(neg_abs)) * LOG2E
    log_beta = jnp.minimum(z2, 0.0) - l2
    lom = log_beta - z2
    if causal is not None:
        lom = jnp.where(causal, lom, 0.0)
    sx = _cumsum_mm(lom, upper2)
    return log_beta, sx, sx[:, 0:1] + lom[:, 0:1]


DEAD_LOG2 = -150.0


def _max_carry(carries):
    return jnp.max(functools.reduce(jnp.maximum, carries))


ZB_GROUP, GA_GROUP, GB_GROUP = 6, 7, 8


def _attn_specs(d, seq, nq, heads_per_step):
    hp_w = heads_per_step * (d // HEADS)
    n_hp = d // hp_w
    row_blk = lambda group: pl.BlockSpec((ATT_T, hp_w), lambda b, h, i: (b * nq + i, group * n_hp + h))
    seq_blk = lambda group: pl.BlockSpec((seq, hp_w), lambda b, h, i: (b, group * n_hp + h))
    return row_blk, seq_blk, n_hp


def _attn_fwd(qkv, proj, bsz, seq):
    t, d3 = qkv.shape
    d = d3 // 3
    hd = d // HEADS
    nq = seq // ATT_T
    scale = hd ** -0.5
    n_heads = ATT_HP
    row_blk, seq_blk, n_hp = _attn_specs(d, seq, nq, n_heads)

    def body(q_ref, k_ref, v_ref, zb_ref, o_ref, yb_ref):
        i = pl.program_id(2)
        causal = _below_diagonal(ATT_T)
        upper2 = _twice(causal)

        def step(kb, state, mask):
            rows = pl.ds(pl.multiple_of(kb * ATT_T, ATT_T), ATT_T)
            heads = [slice(h * hd, (h + 1) * hd) for h in range(n_heads)]
            scores = [_sb_block(q_ref[:, cols], k_ref[rows, cols], scale, upper2, mask) for cols in heads]
            new = []
            for cols, (carry, acc), (log_beta, sx, total) in zip(heads, state, scores):
                a = jnp.exp2(log_beta + sx + carry)
                if mask is not None:
                    a = jnp.where(mask, a, 0.0)
                acc = acc + jnp.dot(a.astype(BF16), v_ref[rows, cols], preferred_element_type=F32)
                new.append((carry + total, acc))
            return tuple(new)

        init = tuple((jnp.zeros((ATT_T, 1), F32), jnp.zeros((ATT_T, hd), F32)) for _ in range(n_heads))
        state = step(i, init, causal)
        def more(c):
            new = step(c[0], c[1], None)
            return c[0] - 1, new, _max_carry([s[0] for s in new])

        _, state, _ = lax.while_loop(lambda c: (c[0] >= 0) & (c[2] > DEAD_LOG2), more,
                                     (i - 1, state, _max_carry([s[0] for s in state])))
        for h in range(n_heads):
            cols = slice(h * hd, (h + 1) * hd)
            acc = state[h][1]
            zb = zb_ref[:, cols]
            o_ref[:, cols] = acc
            yb_ref[:, cols] = (acc * (zb * _sigmoid(zb))).astype(BF16)

    return pl.pallas_call(
        body, name="attn_fwd",
        grid=(bsz, n_hp, nq),
        in_specs=[row_blk(0), seq_blk(1), seq_blk(2), row_blk(ZB_GROUP)],
        out_specs=[row_blk(0), row_blk(0)],
        out_shape=[jax.ShapeDtypeStruct((t, d), F32), jax.ShapeDtypeStruct((t, d), BF16)],
        compiler_params=_params(("arbitrary", "arbitrary", "arbitrary")),
    )(qkv, qkv, qkv, proj)


def _attn_bwd(qkv, proj, o, dyb, bsz, seq):
    t, d3 = qkv.shape
    d = d3 // 3
    hd = d // HEADS
    nq = seq // ATT_T
    scale = hd ** -0.5
    row_blk, seq_blk, n_hp = _attn_specs(d, seq, nq, ATT_HP)

    def body(q_ref, k_ref, v_ref, zb_ref, o_ref, dyb_ref, dq_ref, dk_ref, dv_ref, dzb_ref,
             g_s, beta_s, dkt_acc, dvt_acc):
        i = pl.program_id(2)

        @pl.when(i == 0)
        def _():
            dkt_acc[...] = jnp.zeros_like(dkt_acc)
            dvt_acc[...] = jnp.zeros_like(dvt_acc)

        causal = _below_diagonal(ATT_T)
        upper2 = _twice(causal)
        lower2 = _twice(~causal)
        zb = zb_ref[...]
        sig = _sigmoid(zb)
        dyb_t = dyb_ref[...]
        do_f = dyb_t * (zb * sig)
        do = do_f.astype(BF16)
        do_t = do_f.T.astype(BF16)
        q_t = q_ref[...].astype(F32).T.astype(BF16)
        dzb_ref[...] = (dyb_t * o_ref[...] * (sig * (1.0 + zb * (1.0 - sig)))).astype(BF16)

        def sweep(kb, carries, mask):
            rows = pl.ds(pl.multiple_of(kb * ATT_T, ATT_T), ATT_T)
            heads = [slice(h * hd, (h + 1) * hd) for h in range(ATT_HP)]
            scores = [_sb_block(q_ref[:, cols], k_ref[rows, cols], scale, upper2, mask) for cols in heads]
            das = [lax.dot_general(do[:, cols], v_ref[rows, cols], NT, preferred_element_type=F32) for cols in heads]
            new = []
            for h, (cols, carry, (log_beta, sx, total), da) in enumerate(zip(heads, carries, scores, das)):
                a = jnp.exp2(log_beta + sx + carry)
                beta = jnp.exp2(log_beta)
                if mask is not None:
                    a = jnp.where(mask, a, 0.0)
                    beta = jnp.where(mask, beta, 0.0)
                g_s[h, kb] = a * da
                beta_s[h, kb] = beta
                dvt_acc[kb, cols, :] += jnp.dot(do_t[cols, :], a.astype(BF16), preferred_element_type=F32)
                new.append(carry + total)
            return tuple(new)

        carries = sweep(i, tuple(jnp.zeros((ATT_T, 1), F32) for _ in range(ATT_HP)), causal)

        def more(c):
            new = sweep(c[0], c[1], None)
            return c[0] - 1, new, _max_carry(new)

        last, _, _ = lax.while_loop(lambda c: (c[0] >= 0) & (c[2] > DEAD_LOG2), more, (i - 1, carries, _max_carry(carries)))
        first_kb = last + 1

        def back(kb, state):
            rows = pl.ds(pl.multiple_of(kb * ATT_T, ATT_T), ATT_T)
            heads = [slice(h * hd, (h + 1) * hd) for h in range(ATT_HP)]
            sums = [_cumsum_mm(g_s[h, kb], lower2) for h in range(ATT_HP)]
            new = []
            for h, (cols, (p_carry, dq), px) in enumerate(zip(heads, state, sums)):
                dz = ((g_s[h, kb] - (p_carry + px) * beta_s[h, kb]) * scale).astype(BF16)
                dq = dq + jnp.dot(dz, k_ref[rows, cols], preferred_element_type=F32)
                dkt_acc[kb, cols, :] += jnp.dot(q_t[cols, :], dz, preferred_element_type=F32)
                new.append((p_carry + px[:, ATT_T - 1:ATT_T], dq))
            return tuple(new)

        init = tuple((jnp.zeros((ATT_T, 1), F32), jnp.zeros((ATT_T, hd), F32)) for _ in range(ATT_HP))
        state = lax.fori_loop(first_kb, i + 1, back, init)
        for h in range(ATT_HP):
            dq_ref[:, h * hd:(h + 1) * hd] = state[h][1].astype(BF16)

        @pl.when(i == nq - 1)
        def _():
            for kb in range(nq):
                dk_ref[kb * ATT_T:(kb + 1) * ATT_T, :] = dkt_acc[kb].T.astype(BF16)
                dv_ref[kb * ATT_T:(kb + 1) * ATT_T, :] = dvt_acc[kb].T.astype(BF16)

    out = jax.ShapeDtypeStruct((t, d), BF16)
    hp_w = ATT_HP * hd
    return pl.pallas_call(
        body, name="attn_bwd",
        grid=(bsz, n_hp, nq),
        in_specs=[row_blk(0), seq_blk(1), seq_blk(2), row_blk(ZB_GROUP), row_blk(0), row_blk(0)],
        out_specs=[row_blk(0), seq_blk(0), seq_blk(0), row_blk(0)],
        out_shape=[out, out, out, out],
        scratch_shapes=[pltpu.VMEM((ATT_HP, nq, ATT_T, ATT_T), F32), pltpu.VMEM((ATT_HP, nq, ATT_T, ATT_T), F32),
                        pltpu.VMEM((nq, hp_w, ATT_T), F32), pltpu.VMEM((nq, hp_w, ATT_T), F32)],
        compiler_params=_params(("arbitrary", "arbitrary", "arbitrary")),
    )(qkv, qkv, qkv, proj, o, dyb)


def _out_proj(ya, yb, g_pre, x2d, tgt, w_og, w_osb, w_out, g_f):
    t, d = x2d.shape

    def body(ya_ref, yb_ref, ga_ref, gb_ref, x_ref, tgt_ref, wog_ref, wosb_ref, wout_ref, gf_ref,
             dya_ref, dyb_ref, dg_ref, dx2_ref, loss_ref, gnf_ref, gwog_ref, gwosb_ref, gwout_ref):
        @pl.when(pl.program_id(0) == 0)
        def _():
            loss_ref[...] = jnp.zeros_like(loss_ref)
            gnf_ref[...] = jnp.zeros_like(gnf_ref)
            gwog_ref[...] = jnp.zeros_like(gwog_ref)
            gwosb_ref[...] = jnp.zeros_like(gwosb_ref)
            gwout_ref[...] = jnp.zeros_like(gwout_ref)

        ya = ya_ref[...]
        yb = yb_ref[...]
        pa = jnp.dot(ya, wog_ref[...], preferred_element_type=F32)
        pb = jnp.dot(yb, wosb_ref[...], preferred_element_type=F32)
        sga = _sigmoid(ga_ref[...])
        sgb = _sigmoid(gb_ref[...])
        merged = (sga * pa + sgb * pb).astype(BF16)
        x2 = x_ref[...] + jnp.dot(merged, wout_ref[...], preferred_element_type=F32)
        r2 = lax.rsqrt(jnp.mean(x2 * x2, axis=-1, keepdims=True) + EPS)
        n2 = x2 * r2
        gf = gf_ref[...]
        err = n2 * gf - tgt_ref[...]
        loss_ref[...] += 0.5 * jnp.sum(jnp.sum(err * err, axis=-1, keepdims=True), axis=0, keepdims=True) / d
        dy = err * (1.0 / d)
        gnf_ref[...] += jnp.sum(dy * n2, axis=0, keepdims=True)
        dn = dy * gf
        dx2 = r2 * (dn - n2 * jnp.mean(dn * n2, axis=-1, keepdims=True))
        dx2_ref[...] = dx2
        dx2_b = dx2.astype(BF16)
        dmerged = lax.dot_general(dx2_b, wout_ref[...], NT, preferred_element_type=F32)
        gwout_ref[...] += lax.dot_general(merged, dx2_b, TN, preferred_element_type=F32)
        dg_ref[:, 0:d] = (dmerged * pa * (sga * (1.0 - sga))).astype(BF16)
        dg_ref[:, d:2 * d] = (dmerged * pb * (sgb * (1.0 - sgb))).astype(BF16)
        dpa = (dmerged * sga).astype(BF16)
        dpb = (dmerged * sgb).astype(BF16)
        dya_ref[...] = lax.dot_general(dpa, wog_ref[...], NT, preferred_element_type=F32)
        dyb_ref[...] = lax.dot_general(dpb, wosb_ref[...], NT, preferred_element_type=F32)
        gwog_ref[...] += lax.dot_general(ya, dpa, TN, preferred_element_type=F32)
        gwosb_ref[...] += lax.dot_general(yb, dpb, TN, preferred_element_type=F32)

    row = lambda i: (i, 0)
    return pl.pallas_call(
        body, name="out_proj",
        grid=(t // TM,),
        in_specs=[pl.BlockSpec((TM, d), row), pl.BlockSpec((TM, d), row),
                  pl.BlockSpec((TM, d), lambda i: (i, GA_GROUP)), pl.BlockSpec((TM, d), lambda i: (i, GB_GROUP)),
                  pl.BlockSpec((TM, d), row), pl.BlockSpec((TM, d), row),
                  _resident((d, d)), _resident((d, d)), _resident((d, d)), _const((1, d))],
        out_specs=[pl.BlockSpec((TM, d), row), pl.BlockSpec((TM, d), row), pl.BlockSpec((TM, 2 * d), row),
                   pl.BlockSpec((TM, d), row), _const((1, 1)), _const((1, d)),
                   _const((d, d)), _const((d, d)), _const((d, d))],
        out_shape=[jax.ShapeDtypeStruct((t, d), F32), jax.ShapeDtypeStruct((t, d), F32),
                   jax.ShapeDtypeStruct((t, 2 * d), BF16), jax.ShapeDtypeStruct((t, d), F32),
                   jax.ShapeDtypeStruct((1, 1), F32), jax.ShapeDtypeStruct((1, d), F32),
                   jax.ShapeDtypeStruct((d, d), F32), jax.ShapeDtypeStruct((d, d), F32),
                   jax.ShapeDtypeStruct((d, d), F32)],
        compiler_params=_params(("arbitrary",)),
    )(ya, yb, g_pre, g_pre, x2d, tgt, w_og, w_osb, w_out, g_f)


def _dproj_pieces(d):
    return [(0, 0, 3), (1, 3, 1), (2, 4, 1), (3, 5, 1), (4, 6, 1), (5, 7, 2)]


def _in_proj_bwd_x(pieces, wg, x2d, g_in, dx2, gw16):
    t, d = x2d.shape
    ncol = wg.shape[2]
    segs = _segments(d, ncol)
    layout = _dproj_pieces(d)
    nsteps = t // TMX

    def body(da_ref, dq_ref, dk_ref, dv_ref, dzb_ref, dg_ref, w_ref, x_ref, g_ref, dx2_ref, gw16_ref,
             gx_ref, gn_ref, recv_ref, send_sems, recv_sems):
        x_pos, y_pos, c_pos, chips = _place()

        def share(k, chunk):
            px, py = chips[k]
            return pltpu.make_async_remote_copy(
                src_ref=gw16_ref.at[:, chunk * ncol:(chunk + 1) * ncol], dst_ref=recv_ref.at[k],
                send_sem=send_sems.at[k], recv_sem=recv_sems.at[k], device_id=(px, py, c_pos), device_id_type=MESH)

        @pl.when(pl.program_id(0) == 0)
        def _():
            gn_ref[...] = jnp.zeros_like(gn_ref)
            for k, (px, py) in enumerate(chips):
                for chunk in range(N_CHIP):
                    @pl.when(2 * px + py == chunk)
                    def _(k=k, chunk=chunk):
                        share(k, chunk).start()

        @pl.when(pl.program_id(0) == nsteps - 1)
        def _():
            for k in range(N_CHIP - 1):
                share(k, 0).wait()

        refs = (da_ref, dq_ref, dk_ref, dv_ref, dzb_ref, dg_ref)
        dh = jnp.zeros((TMX, d), F32)
        for chip, c0, grp, s0, width in segs:
            piece, first, _ = next(p for p in layout if p[1] <= grp < p[1] + p[2])
            off = (grp - first) * d + s0
            dh = dh + lax.dot_general(refs[piece][:, off:off + width], w_ref[chip, :, c0:c0 + width], NT,
                                      preferred_element_type=F32)
        x = x_ref[...]
        r = lax.rsqrt(jnp.mean(x * x, axis=-1, keepdims=True) + EPS)
        n = x * r
        gn_ref[...] += jnp.sum(dh * n, axis=0, keepdims=True)
        dn = dh * g_ref[...]
        gx_ref[...] = dx2_ref[...] + r * (dn - n * jnp.mean(dn * n, axis=-1, keepdims=True))

    row = lambda i: (i, 0)
    return pl.pallas_call(
        body, name="in_proj_bwd_x",
        grid=(nsteps,),
        in_specs=[pl.BlockSpec((TMX, p.shape[1]), row) for p in pieces]
        + [_resident(wg.shape), pl.BlockSpec((TMX, d), row), _const((1, d)), pl.BlockSpec((TMX, d), row), ANY],
        out_specs=[pl.BlockSpec((TMX, d), row), _const((1, d)), ANY],
        out_shape=[jax.ShapeDtypeStruct((t, d), F32), jax.ShapeDtypeStruct((1, d), F32),
                   jax.ShapeDtypeStruct((N_CHIP - 1, gw16.shape[0], ncol), BF16)],
        scratch_shapes=[pltpu.SemaphoreType.DMA((N_CHIP - 1,)), pltpu.SemaphoreType.DMA((N_CHIP - 1,))],
        compiler_params=_params(("arbitrary",)),
    )(*pieces, wg, x2d, g_in, dx2, gw16)


def _in_proj_bwd_w(h_t, pieces, mats16, pack):
    d, t = h_t.shape
    nk = t // TKW
    half = d // 2
    layout = _dproj_pieces(d)
    n_mats = len(mats16)
    n_dev = 2 * N_CHIP
    flips = [(dx, dy, dc) for dx in (0, 1) for dy in (0, 1) for dc in (0, 1)][1:]

    def body(ht_ref, da_ref, dq_ref, dk_ref, dv_ref, dzb_ref, dg_ref, *rest):
        mat_refs, pack_ref = rest[:n_mats], rest[n_mats]
        gw_ref, sib_ref = rest[n_mats + 1:n_mats + 3]
        recv_refs, slots_ref = rest[n_mats + 3:2 * n_mats + 3], rest[2 * n_mats + 3]
        acc, stage, ring, ring_sems, pring, piece_sems, mat_send, mat_recv, pack_send, pack_recv, own_sem, stage_send, sib_recv = \
            rest[2 * n_mats + 4:]
        s = pl.program_id(0)
        i = pl.program_id(1)
        x_pos, y_pos, c_pos, chips = _place()
        me = 4 * x_pos + 2 * y_pos + c_pos

        refs = (da_ref, dq_ref, dk_ref, dv_ref, dzb_ref, dg_ref)
        step = s * nk + i

        def fetch(at):
            rows = pl.ds(pl.multiple_of((at % nk) * TKW, TKW), TKW)
            return pltpu.make_async_copy(ht_ref.at[:, rows], ring.at[at % HT_SLOTS], ring_sems.at[at % HT_SLOTS])

        def piece_block(at, piece, g):
            rows = pl.ds(pl.multiple_of((at % nk) * TKW, TKW), TKW)
            return pltpu.make_async_copy(refs[piece].at[rows, g * d:(g + 1) * d], pring.at[at % HT_SLOTS],
                                         piece_sems.at[at % HT_SLOTS])

        def start_both(at):
            fetch(at).start()
            for piece, first, count in layout:
                for g in range(count):
                    @pl.when(at // nk == first + g)
                    def _(piece=piece, g=g):
                        piece_block(at, piece, g).start()

        @pl.when(step == 0)
        def _():
            for at in range(HT_SLOTS - 1):
                start_both(jnp.int32(at))

        @pl.when(step + HT_SLOTS - 1 < N_SPLIT * nk)
        def _():
            start_both(step + HT_SLOTS - 1)

        fetch(step).wait()
        piece_block(step, 1, 0).wait()

        def to_sibling(slot, group):
            return pltpu.make_async_remote_copy(
                src_ref=stage.at[slot], dst_ref=sib_ref.at[:, group * d:(group + 1) * d],
                send_sem=stage_send.at[slot], recv_sem=sib_recv, device_id=(x_pos, y_pos, 1 - c_pos), device_id_type=MESH)

        def exchanges():
            cps = []
            for k, (px, py) in enumerate(chips):
                for a in range(n_mats):
                    cps.append(pltpu.make_async_remote_copy(
                        src_ref=mat_refs[a].at[2 * px + py], dst_ref=recv_refs[a].at[k],
                        send_sem=mat_send.at[a, k], recv_sem=mat_recv.at[a, k],
                        device_id=(px, py, c_pos), device_id_type=MESH))
            for k, (dx, dy, dc) in enumerate(flips):
                peer = (1 - x_pos if dx else x_pos, 1 - y_pos if dy else y_pos, 1 - c_pos if dc else c_pos)
                cps.append(pltpu.make_async_remote_copy(
                    src_ref=pack_ref, dst_ref=slots_ref.at[me], send_sem=pack_send.at[k], recv_sem=pack_recv.at[k],
                    device_id=peer, device_id_type=MESH))
            return cps, pltpu.make_async_copy(pack_ref, slots_ref.at[me], own_sem)

        @pl.when((s == 0) & (i == 0))
        def _():
            cps, own = exchanges()
            own.start()
            for cp in cps:
                cp.start()

        @pl.when(i == 0)
        def _():
            acc[...] = jnp.zeros_like(acc)

        acc[...] += jnp.dot(ring[step % HT_SLOTS], pring[step % HT_SLOTS], preferred_element_type=F32)

        @pl.when(i == nk - 1)
        def _():
            gw_ref[...] = acc[...]
            for slot in range(2):
                @pl.when(s % 2 == slot)
                def _(slot=slot):
                    @pl.when(s >= 2)
                    def _():
                        to_sibling(slot, 0).wait_send()
                    for other in range(2):
                        @pl.when(c_pos == 1 - other)
                        def _(other=other):
                            stage[slot] = acc[other * half:(other + 1) * half, :].astype(BF16)
                    for group in range(N_SPLIT):
                        @pl.when(s == group)
                        def _(group=group):
                            to_sibling(slot, group).start()

        @pl.when((s == N_SPLIT - 1) & (i == nk - 1))
        def _():
            for slot in range(2):
                to_sibling(slot, 0).wait_send()
            pltpu.make_async_remote_copy(
                src_ref=sib_ref, dst_ref=sib_ref, send_sem=stage_send.at[0], recv_sem=sib_recv,
                device_id=(x_pos, y_pos, c_pos), device_id_type=MESH).wait_recv()
            cps, own = exchanges()
            own.wait()
            for cp in cps:
                cp.wait()

    col_blk = pl.BlockSpec((d, d), lambda s, i: (0, s))
    outs = pl.pallas_call(
        body, name="in_proj_bwd_w",
        grid=(N_SPLIT, nk),
        in_specs=[ANY] * (1 + len(layout) + n_mats + 1),
        out_specs=[col_blk, ANY] + [ANY] * (n_mats + 1),
        out_shape=[jax.ShapeDtypeStruct((d, N_SPLIT * d), F32), jax.ShapeDtypeStruct((half, N_SPLIT * d), BF16)]
        + [jax.ShapeDtypeStruct((N_CHIP - 1,) + m.shape[1:], BF16) for m in mats16]
        + [jax.ShapeDtypeStruct((n_dev,) + pack.shape, F32)],
        scratch_shapes=[pltpu.VMEM((d, d), F32), pltpu.VMEM((2, half, d), BF16),
                        pltpu.VMEM((HT_SLOTS, d, TKW), BF16), pltpu.SemaphoreType.DMA((HT_SLOTS,)),
                        pltpu.VMEM((HT_SLOTS, TKW, d), BF16), pltpu.SemaphoreType.DMA((HT_SLOTS,)),
                        pltpu.SemaphoreType.DMA((n_mats, N_CHIP - 1)), pltpu.SemaphoreType.DMA((n_mats, N_CHIP - 1)),
                        pltpu.SemaphoreType.DMA((n_dev - 1,)), pltpu.SemaphoreType.DMA((n_dev - 1,)),
                        pltpu.SemaphoreType.DMA, pltpu.SemaphoreType.DMA((2,)), pltpu.SemaphoreType.DMA],
        compiler_params=_params(("arbitrary", "arbitrary")),
    )(h_t, *pieces, *mats16, pack)
    return outs[0], outs[1], outs[2:2 + n_mats], outs[2 + n_mats]


def _local_step(proj, qkv, x2d, tgt2d, bsz, seq, norm_v, w_s, b_s, w_og, w_osb, w_out, norm_final):
    d = x2d.shape[1]
    chunk = w_s.shape[-1]
    causal = jnp.tril(jnp.ones((chunk, chunk), dtype=bool))
    wm = jnp.where(causal[None], w_s, 0.0).astype(BF16)
    wm_t = jnp.swapaxes(wm, 1, 2)
    b_t = b_s.T

    ya = _branch_a_fwd(proj, norm_v, wm, b_t)
    o, yb = _attn_fwd(qkv, proj, bsz, seq)
    dya, dyb, dg, dx2, loss, g_nf, g_wog, g_wosb, g_wout = _out_proj(
        ya, yb, proj, x2d, tgt2d, w_og, w_osb, w_out, norm_final.reshape(1, d))
    dq, dk, dv, dzb = _attn_bwd(qkv, proj, o, dyb, bsz, seq)
    d_a, g_ws, g_bt, g_nv = _branch_a_bwd(proj, dya, norm_v, wm, wm_t, b_t)
    g_ws = jnp.where(causal[None], g_ws, 0.0)
    return loss, (d_a, dq, dk, dv, dzb, dg), dx2, g_nv, g_ws, g_bt.T, g_wog, g_wosb, g_wout, g_nf


def _row_tile(rows):
    return next(r for r in (128, 64, 32, 16, 8) if rows % r == 0)


def _cast_bf16(arrs):
    n = len(arrs)

    def body(*refs):
        for a_ref, o_ref in zip(refs[:n], refs[n:]):
            o_ref[...] = a_ref[...].astype(BF16)

    specs = [pl.BlockSpec((a.shape[0] // CAST_STEPS, a.shape[1]), lambda i: (i, 0)) for a in arrs]
    return pl.pallas_call(
        body, name="cast_bf16", grid=(CAST_STEPS,),
        in_specs=specs, out_specs=specs,
        out_shape=[jax.ShapeDtypeStruct(a.shape, BF16) for a in arrs],
        compiler_params=_params(("arbitrary",)),
    )(*arrs)


def _chip_half(full, from_sibling, core, chip, tile):
    half, n = from_sibling.shape

    def body(where_ref, own_ref, sib_ref, o32_ref, o16_ref):
        total = own_ref[...] + sib_ref[...].astype(F32)
        o16_ref[...] = total.astype(BF16)

        @pl.when(pl.program_id(0) == where_ref[1])
        def _():
            o32_ref[...] = total

    blk = pl.BlockSpec((half, tile), lambda j, where_ref: (0, j))
    return pl.pallas_call(
        body, name="chip_half",
        grid_spec=pltpu.PrefetchScalarGridSpec(
            num_scalar_prefetch=1, grid=(n // tile,),
            in_specs=[pl.BlockSpec((half, tile), lambda j, where_ref: (where_ref[0], j)), blk],
            out_specs=[pl.BlockSpec((half, tile), lambda j, where_ref: (0, 0)), blk]),
        out_shape=[jax.ShapeDtypeStruct((half, tile), F32), jax.ShapeDtypeStruct((half, n), BF16)],
        compiler_params=_params(("arbitrary",)),
    )(jnp.stack([core, chip]).astype(jnp.int32), full, from_sibling)


def _add_received(fulls, recvs, chip, by_cols):
    n = len(fulls)
    _, rows, cols = recvs[0].shape
    tr = _row_tile(rows)
    nb = rows // tr

    def body(chip_ref, *refs):
        for own_ref, recv_ref, o_ref in zip(refs[:n], refs[n:2 * n], refs[2 * n:]):
            s = own_ref[...]
            for k in range(N_CHIP - 1):
                s = s + recv_ref[k].astype(F32)
            o_ref[...] = s

    own_map = (lambda i, chip_ref: (i, chip_ref[0])) if by_cols else (lambda i, chip_ref: (chip_ref[0] * nb + i, 0))
    return pl.pallas_call(
        body, name="add_received",
        grid_spec=pltpu.PrefetchScalarGridSpec(
            num_scalar_prefetch=1, grid=(nb,),
            in_specs=[pl.BlockSpec((tr, cols), own_map)] * n
            + [pl.BlockSpec((N_CHIP - 1, tr, cols), lambda i, chip_ref: (0, i, 0))] * n,
            out_specs=[pl.BlockSpec((tr, cols), lambda i, chip_ref: (i, 0))] * n),
        out_shape=[jax.ShapeDtypeStruct((rows, cols), F32)] * n,
        compiler_params=_params(("arbitrary",)),
    )(chip.reshape(1).astype(jnp.int32), *fulls, *recvs)


def _adamw_math(w, m, v, g):
    new_m = ADAM_B1 * m + (1.0 - ADAM_B1) * g
    new_v = ADAM_B2 * v + (1.0 - ADAM_B2) * (g * g)
    m_hat = new_m / (1.0 - ADAM_B1 ** ADAM_STEP)
    v_hat = new_v / (1.0 - ADAM_B2 ** ADAM_STEP)
    return -ADAM_LR * (m_hat / (jnp.sqrt(v_hat) + ADAM_EPS) + ADAM_WD * w), new_m, new_v


def _adamw_pairs(ws, ms, vs, mines, theirs):
    n = len(ws)
    rows, cols = ws[0].shape
    tr = _row_tile(rows)

    def body(*refs):
        ins, outs = refs[:5 * n], refs[5 * n:]
        for a in range(n):
            w_ref, m_ref, v_ref, mine_ref, theirs_ref = ins[a::n]
            g_ref, d_ref, nm_ref, nv_ref = outs[4 * a:4 * a + 4]
            g = mine_ref[...] + theirs_ref[...]
            g_ref[...] = g
            d_ref[...], nm_ref[...], nv_ref[...] = _adamw_math(w_ref[...], m_ref[...], v_ref[...], g)

    spec = pl.BlockSpec((tr, cols), lambda i: (i, 0))
    out = jax.ShapeDtypeStruct((rows, cols), F32)
    outs = pl.pallas_call(
        body, name="adamw_pairs", grid=(rows // tr,),
        in_specs=[spec] * (5 * n), out_specs=[spec] * (4 * n), out_shape=[out] * (4 * n),
        compiler_params=_params(("arbitrary",)),
    )(*ws, *ms, *vs, *mines, *theirs)
    return [outs[4 * a:4 * a + 4] for a in range(n)]


def _adamw_halves(w, m, v, mine, theirs, core):
    rows, cols = w.shape
    tr = _row_tile(rows // 2)
    per_half = rows // 2 // tr

    def body(core_ref, w_ref, m_ref, v_ref, mine_ref, theirs_ref, g_ref, d_ref, nm_ref, nv_ref):
        is_mine = pl.program_id(0) // per_half == core_ref[0]
        for part, cond in ((mine_ref, is_mine), (theirs_ref, jnp.logical_not(is_mine))):
            @pl.when(cond)
            def _(part=part):
                g = part[...]
                g_ref[...] = g
                d_ref[...], nm_ref[...], nv_ref[...] = _adamw_math(w_ref[...], m_ref[...], v_ref[...], g)

    spec = pl.BlockSpec((tr, cols), lambda i, core_ref: (i, 0))

    def half_spec(own):
        def index(i, core_ref):
            in_core_half = i // per_half == core_ref[0]
            here = in_core_half if own else jnp.logical_not(in_core_half)
            return jnp.where(here, i % per_half, 0), 0
        return pl.BlockSpec((tr, cols), index)

    out = jax.ShapeDtypeStruct(w.shape, F32)
    return pl.pallas_call(
        body, name="adamw_halves",
        grid_spec=pltpu.PrefetchScalarGridSpec(
            num_scalar_prefetch=1, grid=(rows // tr,),
            in_specs=[spec] * 3 + [half_spec(True), half_spec(False)], out_specs=[spec] * 4),
        out_shape=[out] * 4,
        compiler_params=_params(("arbitrary",)),
    )(core.reshape(1).astype(jnp.int32), w, m, v, mine, theirs)


def _adamw_small(ws, ms, vs, slots_head, slots_tail):
    n_dev = slots_head.shape[0]
    n_par = len(ws)
    groups, chunk, _ = ws[4].shape

    def body(*refs):
        w_refs, m_refs, v_refs = refs[:n_par], refs[n_par:2 * n_par], refs[2 * n_par:3 * n_par]
        head_ref, tail_ref = refs[3 * n_par:3 * n_par + 2]
        out_refs, scalar_ref = refs[3 * n_par + 2:-1], refs[-1]

        def total(ref, rows, cols):
            g = ref[0, rows, cols]
            for i in range(1, n_dev):
                g = g + ref[i, rows, cols]
            return g

        def update(p, g, at=...):
            g_ref, d_ref, nm_ref, nv_ref = out_refs[4 * p:4 * p + 4]
            g_ref[at] = g
            d_ref[at], nm_ref[at], nv_ref[at] = _adamw_math(w_refs[p][at], m_refs[p][at], v_refs[p][at], g)

        every = slice(None)
        update(0, total(head_ref, slice(0, 1), every))
        update(1, total(tail_ref, slice(0, 1), every))
        update(2, total(tail_ref, slice(SLAB, SLAB + groups), slice(0, chunk)))
        update(3, total(tail_ref, slice(2 * SLAB, 2 * SLAB + 1), every))
        scalar_ref[...] = total(tail_ref, slice(3 * SLAB, 3 * SLAB + 1), slice(0, chunk))[:, 0:1]
        for grp in range(groups):
            update(4, total(tail_ref, slice(4 * SLAB, 4 * SLAB + chunk), slice(grp * chunk, (grp + 1) * chunk)), grp)

    vmem = pl.BlockSpec(memory_space=pltpu.VMEM)
    return pl.pallas_call(
        body, name="adamw_small", in_specs=[vmem] * (3 * n_par + 2), out_specs=[vmem] * (4 * n_par + 1),
        out_shape=[jax.ShapeDtypeStruct(w.shape, F32) for w in ws for _ in range(4)]
        + [jax.ShapeDtypeStruct((1, 1), F32)],
        compiler_params=pltpu.CompilerParams(vmem_limit_bytes=VMEM_LIMIT),
    )(*ws, *ms, *vs, slots_head, slots_tail)


ANY = pl.BlockSpec(memory_space=pl.ANY)


def _place():
    x, y, c = lax.axis_index("x"), lax.axis_index("y"), lax.axis_index("c")
    other_chips = [(1 - x, y), (x, 1 - y), (1 - x, 1 - y)]
    return x, y, c, other_chips


def _swap_and_gather(arrs, pack):
    n = len(arrs)
    n_dev = 2 * N_CHIP
    flips = [(dx, dy, dc) for dx in (0, 1) for dy in (0, 1) for dc in (0, 1)][1:]

    def body(*refs):
        ins, pack_ref = refs[:n], refs[n]
        outs, slots_ref = refs[n + 1:2 * n + 1], refs[2 * n + 1]
        send_sems, recv_sems, pack_send, pack_recv, own_sem = refs[2 * n + 2:]
        x, y, c, _ = _place()
        me = 4 * x + 2 * y + c
        own = pltpu.make_async_copy(pack_ref, slots_ref.at[me], own_sem)
        own.start()
        copies = []
        for k, (dx, dy, dc) in enumerate(flips):
            peer = (1 - x if dx else x, 1 - y if dy else y, 1 - c if dc else c)
            copies.append(pltpu.make_async_remote_copy(
                src_ref=pack_ref, dst_ref=slots_ref.at[me], send_sem=pack_send.at[k], recv_sem=pack_recv.at[k],
                device_id=peer, device_id_type=MESH))
        copies += [pltpu.make_async_remote_copy(
            src_ref=ins[a], dst_ref=outs[a], send_sem=send_sems.at[a], recv_sem=recv_sems.at[a],
            device_id=(x, y, 1 - c), device_id_type=MESH) for a in range(n)]
        for cp in copies:
            cp.start()
        for cp in copies:
            cp.wait()
        own.wait()

    return pl.pallas_call(
        body, name="swap_and_gather",
        in_specs=[ANY] * (n + 1), out_specs=[ANY] * (n + 1),
        out_shape=[jax.ShapeDtypeStruct(a.shape, a.dtype) for a in arrs] + [jax.ShapeDtypeStruct((n_dev,) + pack.shape, F32)],
        scratch_shapes=[pltpu.SemaphoreType.DMA((n,)), pltpu.SemaphoreType.DMA((n,)),
                        pltpu.SemaphoreType.DMA((n_dev - 1,)), pltpu.SemaphoreType.DMA((n_dev - 1,)),
                        pltpu.SemaphoreType.DMA],
    )(*arrs, pack)


SLAB = 8


def _slab(vec, d):
    return jnp.pad(vec.reshape(1, d), ((0, SLAB - 1), (0, 0)))


def _pack_tail(vec_nv, b_s, vec_nf, w_s, scalar):
    d = vec_nv.shape[-1]
    groups, chunk, _ = w_s.shape
    assert groups == SLAB and groups * chunk == d
    slabs = [_slab(vec_nv, d), jnp.pad(b_s, ((0, 0), (0, d - chunk))), _slab(vec_nf, d),
             jnp.pad(scalar, ((0, SLAB - 1), (0, d - 1)))]
    return jnp.concatenate(slabs + [jnp.swapaxes(w_s, 0, 1).reshape(chunk, d)], axis=0)


def kernel(x, norm_in, w_in, norm_v, w_s, b_s, w_o_gmlp, w_o_sb, w_out, norm_final, loss_target, m_norm_in, m_w_in, m_norm_v, m_w_s, m_b_s, m_w_o_gmlp, m_w_o_sb, m_w_out, m_norm_final, v_norm_in, v_w_in, v_norm_v, v_w_s, v_b_s, v_w_o_gmlp, v_w_o_sb, v_w_out, v_norm_final):
    d = x.shape[-1]
    ncol = w_in.shape[-1]
    nrow = w_o_gmlp.shape[-2]
    chip = 2 * lax.axis_index("x") + lax.axis_index("y")

    bsz, seq, _ = x.shape
    x2d = x.reshape(bsz * seq, d)
    shards = [w_in[0], w_o_gmlp[0], w_o_sb[0], w_out[0]]
    halves = [s16.reshape(2, s16.shape[0] // 2, s16.shape[1]) for s16 in _cast_bf16(shards)]
    proj, qkv, h_t, (wg, w_og, w_osb, w_o) = _in_proj_fwd(x2d, norm_in, halves)
    wg = wg.reshape(N_CHIP, d, ncol)

    loss, pieces, dx2, g_nv, g_ws, g_bs, g_wog, g_wosb, g_wout, g_nf = _local_step(
        proj, qkv, x2d, loss_target.reshape(bsz * seq, d), bsz, seq, norm_v, w_s[0], b_s[0],
        w_og.reshape(d, d), w_osb.reshape(d, d), w_o.reshape(d, d), norm_final)

    mats = [g_wog, g_wosb, g_wout]
    mats16 = [g16.reshape(N_CHIP, nrow, d) for g16 in _cast_bf16(mats)]
    g_win, from_sibling, recv_mats, slots_tail = _in_proj_bwd_w(
        h_t, pieces, mats16, _pack_tail(g_nv, g_bs, g_nf, g_ws, loss))
    core = lax.axis_index("c")
    half_own, half_win16 = _chip_half(g_win, from_sibling, core, chip, ncol)
    grad_x, g_nin, recv_win = _in_proj_bwd_x(pieces, wg, x2d, norm_in, dx2, half_win16)
    grad_x = grad_x.reshape(bsz, seq, d)

    sums = _add_received([half_own], [recv_win], jnp.zeros((), jnp.int32), True) + _add_received(
        mats, recv_mats, chip, False)
    *sibling_sums, slots_head = _swap_and_gather(sums, _slab(g_nin, d))
    stats = [_adamw_halves(w_in[0], m_w_in[0], v_w_in[0], sums[0], sibling_sums[0], core)] + _adamw_pairs(
        shards[1:], [m_w_o_gmlp[0], m_w_o_sb[0], m_w_out[0]], [v_w_o_gmlp[0], v_w_o_sb[0], v_w_out[0]],
        sums[1:], sibling_sums[1:])

    *small, loss = _adamw_small(
        [norm_in, norm_v, b_s[0], norm_final.reshape(1, d), w_s[0]],
        [m_norm_in, m_norm_v, m_b_s[0], m_norm_final.reshape(1, d), m_w_s[0]],
        [v_norm_in, v_norm_v, v_b_s[0], v_norm_final.reshape(1, d), v_w_s[0]], slots_head, slots_tail)

    out = []
    for kind, (win, wog, wosb, wout) in enumerate(zip(*stats)):
        nin, nv, bs, nf, ws = small[kind::4]
        out += [nin, win[None], nv, ws[None], bs[None], wog[None], wosb[None], wout[None], nf.reshape(d)]
    return (loss.reshape(()), grad_x, *out)
```

```python
import functools
import math

import jax
import jax.numpy as jnp
from jax import lax
from jax.experimental import pallas as pl
from jax.experimental.pallas import tpu as pltpu

F32 = jnp.float32
BF16 = jnp.bfloat16
EPS = 1e-6
HEADS = 8
N_SPLIT = 9
N_CHIP = 4
MESH = pl.DeviceIdType.MESH

ADAM_LR = 0.001
ADAM_B1 = 0.9
ADAM_B2 = 0.999
ADAM_EPS = 1e-08
ADAM_WD = 0.01
ADAM_STEP = 10

VMEM_LIMIT = 56 * 2 ** 20
TM = 256
TMF = 512
TMX = 512
ATT_T = 256
ATT_HP = 4
TKW = 2048
CHUNKS_PER_STEP = 4
CAST_STEPS = 8
HT_SLOTS = 3

NT = (((1,), (1,)), ((), ()))
TN = (((0,), (0,)), ((), ()))


def _params(sem):
    return pltpu.CompilerParams(dimension_semantics=sem, vmem_limit_bytes=VMEM_LIMIT)


def _resident(shape):
    nd = len(shape)
    return pl.BlockSpec(shape, lambda *_: (0,) * nd, pipeline_mode=pl.Buffered(1))


def _const(shape):
    nd = len(shape)
    return pl.BlockSpec(shape, lambda *_: (0,) * nd)


def _segments(d, ncol):
    segs = []
    edges = sorted({j * ncol for j in range(N_CHIP + 1)} | {s * d for s in range(N_SPLIT + 1)})
    for lo, hi in zip(edges[:-1], edges[1:]):
        segs.append((lo // ncol, lo % ncol, lo // d, lo % d, hi - lo))
    return segs


def _sigmoid(x):
    return 0.5 * jnp.tanh(0.5 * x) + 0.5


_GELU_C = math.sqrt(2.0 / math.pi)


_GELU_CA = _GELU_C * 0.044715


def _gelu(x):
    return x * (0.5 * jnp.tanh(x * (_GELU_C + _GELU_CA * (x * x))) + 0.5)


def _gelu_and_grad(x):
    x2 = x * x
    u = 0.5 * jnp.tanh(x * (_GELU_C + _GELU_CA * x2)) + 0.5
    slope = (1.0 - u) * (x * (_GELU_C + (3.0 * _GELU_CA) * x2))
    return x * u, u * (2.0 * slope + 1.0)


def _split_bf16(a):
    hi = a.astype(BF16)
    lo = (a - hi.astype(F32)).astype(BF16)
    return hi, lo


def _in_proj_fwd(x2d, g_in, halves):
    t, d = x2d.shape
    n = len(halves)
    ncol = halves[0].shape[2]
    n_row = t // TMF
    last = n_row - 1
    assert halves[0].shape[1] * 2 == d
    qkv_parts = {j: (max(j * ncol, 3 * d) - j * ncol, max(j * ncol, 3 * d) - 3 * d)
                 for j in range(N_CHIP) if min((j + 1) * ncol, 6 * d) > max(j * ncol, 3 * d)}
    qkv_w = 3 * d // len(qkv_parts)
    assert all(min((j + 1) * ncol, 6 * d) - max(j * ncol, 3 * d) == qkv_w and q0 % qkv_w == 0
               for j, (_, q0) in qkv_parts.items())

    def body(order_ref, qrow_ref, qcol_ref, x_ref, g_ref, *rest):
        ins = rest[:n]
        proj_ref, qkv_ref, ht_ref = rest[n:n + 3]
        outs = rest[n + 3:2 * n + 3]
        wbuf, h_all, send_sems, recv_sems, local_sems, load_sem = rest[2 * n + 3:]
        phase = pl.program_id(0)
        i = pl.program_id(1)
        x_pos, y_pos, c_pos, chips = _place()
        sibling = (x_pos, y_pos, 1 - c_pos)
        me = (x_pos, y_pos, c_pos)
        my_chip = 2 * x_pos + y_pos

        def copy(a, k, block, to, src=None):
            return pltpu.make_async_remote_copy(
                src_ref=outs[a].at[block] if src is None else src, dst_ref=outs[a].at[block],
                send_sem=send_sems.at[a, k], recv_sem=recv_sems.at[a, k], device_id=to, device_id_type=MESH)

        def local(a):
            return pltpu.make_async_copy(ins[a], outs[a].at[pl.ds(2 * my_chip, 2)], local_sems.at[a])

        def load(src, first, slot):
            for half in range(2):
                cp = pltpu.make_async_copy(src.at[first + half], wbuf.at[slot, pl.ds(half * (d // 2), d // 2)], load_sem)
                cp.start()
                cp.wait()

        def relay(a, k, block, piece, to):
            rows = halves[a].shape[1] // 2
            ref = outs[a].at[block, pl.ds(piece * rows, rows)]
            return pltpu.make_async_remote_copy(
                src_ref=ref, dst_ref=ref, send_sem=send_sems.at[a, k], recv_sem=recv_sems.at[a, k],
                device_id=to, device_id_type=MESH)

        x_nbr, y_nbr, diagonal = chips
        first_block = lambda chip_xy: 2 * (2 * chip_xy[0] + chip_xy[1])

        def neighbours_arrived(arrays):
            from_x, from_y = first_block(x_nbr) + c_pos, first_block(y_nbr) + c_pos
            for a in arrays:
                copy(a, 0, from_x, me).wait_recv()
                copy(a, 1, from_y, me).wait_recv()
                relay(a, 2, from_x, 0, (*y_nbr, c_pos)).start()
                relay(a, 3, from_y, 1, (*x_nbr, c_pos)).start()
                copy(a, 4, from_x, sibling).start()
                copy(a, 5, from_y, sibling).start()

        def diagonal_arrived(arrays):
            from_diagonal = first_block(diagonal) + c_pos
            for a in arrays:
                relay(a, 2, from_diagonal, 0, me).wait_recv()
                relay(a, 3, from_diagonal, 1, me).wait_recv()
                copy(a, 6, from_diagonal, sibling).start()

        def from_sibling(a, k, chip_xy):
            copy(a, k, first_block(chip_xy) + 1 - c_pos, me).wait_recv()

        @pl.when((phase == 0) & (i == 0))
        def _():
            for a in range(n):
                local(a).start()
            for k, (px, py) in enumerate((x_nbr, y_nbr)):
                for a in range(n):
                    copy(a, k, 2 * my_chip + c_pos, (px, py, c_pos), src=ins[a].at[c_pos]).start()
            load(ins[0], 0, 0)

        @pl.when((phase == 1) & (i == 0))
        def _():
            neighbours_arrived([0])
            from_sibling(0, 4, x_nbr)
            load(outs[0], first_block(x_nbr), 1)

        @pl.when((phase == 2) & (i == 0))
        def _():
            from_sibling(0, 5, y_nbr)
            load(outs[0], first_block(y_nbr), 0)
            neighbours_arrived(range(1, n))

        @pl.when((phase == 3) & (i == 0))
        def _():
            diagonal_arrived(range(n))
            from_sibling(0, 6, diagonal)
            load(outs[0], first_block(diagonal), 1)

        @pl.when(phase == 0)
        def _():
            x = x_ref[...]
            r = lax.rsqrt(jnp.mean(x * x, axis=-1, keepdims=True) + EPS)
            hf = x * r * g_ref[...]
            h_all[i] = hf.astype(BF16)
            ht_ref[...] = hf.T.astype(BF16)

        for slot in range(2):
            @pl.when(phase % 2 == slot)
            def _(slot=slot):
                proj_ref[...] = jnp.dot(h_all[i], wbuf[slot], preferred_element_type=F32)

        for chunk, (c0, _) in qkv_parts.items():
            @pl.when(order_ref[phase] == chunk)
            def _(c0=c0):
                qkv_ref[...] = proj_ref[:, c0:c0 + qkv_w].astype(BF16)

        @pl.when((phase == N_CHIP - 1) & (i == n_row - 1))
        def _():
            for a in range(1, n):
                for k, chip_xy in ((4, x_nbr), (5, y_nbr), (6, diagonal)):
                    from_sibling(a, k, chip_xy)
            for a in range(n):
                for k in (0, 1, 4, 5, 6):
                    copy(a, k, 0, me).wait_send()
                for k in (2, 3):
                    relay(a, k, 0, 0, me).wait_send()
                local(a).wait()

    x_pos, y_pos = lax.axis_index("x"), lax.axis_index("y")
    order = jnp.stack([2 * x_pos + y_pos, 2 * (1 - x_pos) + y_pos, 2 * x_pos + 1 - y_pos,
                       2 * (1 - x_pos) + 1 - y_pos]).astype(jnp.int32)
    holds = [functools.reduce(jnp.logical_or, [order[p] == j for j in qkv_parts]) for p in range(N_CHIP)]
    col = [sum(jnp.where(order[p] == j, q0 // qkv_w, 0) for j, (_, q0) in qkv_parts.items()) for p in range(N_CHIP)]
    cur = col[-1]
    for p in reversed(range(N_CHIP - 1)):
        cur = jnp.where(holds[p], col[p], cur)
    seen = jnp.bool_(False)
    qrow, qcol = [], []
    for p in range(N_CHIP):
        cur = jnp.where(holds[p], col[p], cur)
        qrow.append(jnp.where(holds[p], -1, jnp.where(seen, last, 0)))
        qcol.append(cur)
        seen = seen | holds[p]
    qrow = jnp.stack(qrow).astype(jnp.int32)
    qcol = jnp.stack(qcol).astype(jnp.int32)

    outs = pl.pallas_call(
        body, name="in_proj_fwd",
        grid_spec=pltpu.PrefetchScalarGridSpec(
            num_scalar_prefetch=3, grid=(N_CHIP, n_row),
            in_specs=[pl.BlockSpec((TMF, d), lambda p, i, order, qrow, qcol: (jnp.where(p == 0, i, last), 0)),
                      pl.BlockSpec((1, d), lambda p, i, order, qrow, qcol: (0, 0))] + [ANY] * n,
            out_specs=[pl.BlockSpec((TMF, ncol), lambda p, i, order, qrow, qcol: (i, order[p])),
                       pl.BlockSpec((TMF, qkv_w),
                                    lambda p, i, order, qrow, qcol: (jnp.where(qrow[p] < 0, i, qrow[p]), qcol[p])),
                       pl.BlockSpec((d, TMF), lambda p, i, order, qrow, qcol: (0, jnp.where(p == 0, i, last)))]
            + [ANY] * n,
            scratch_shapes=[pltpu.VMEM((2, d, ncol), BF16), pltpu.VMEM((n_row, TMF, d), BF16),
                            pltpu.SemaphoreType.DMA((n, 7)), pltpu.SemaphoreType.DMA((n, 7)),
                            pltpu.SemaphoreType.DMA((n,)), pltpu.SemaphoreType.DMA]),
        out_shape=[jax.ShapeDtypeStruct((t, N_CHIP * ncol), F32), jax.ShapeDtypeStruct((t, 3 * d), BF16),
                   jax.ShapeDtypeStruct((d, t), BF16)]
        + [jax.ShapeDtypeStruct((2 * N_CHIP,) + hv.shape[1:], BF16) for hv in halves],
        compiler_params=_params(("arbitrary", "arbitrary")),
    )(order, qrow, qcol, x2d, g_in, *halves)
    return outs[0], outs[1], outs[2], outs[3:]


def _branch_a_fwd(a_pre, g_v, wm, b_t):
    t = a_pre.shape[0]
    d = g_v.shape[1]
    d3 = 3 * d
    ng, chunk, _ = wm.shape
    cw = d // ng

    per_step = CHUNKS_PER_STEP if t % (CHUNKS_PER_STEP * chunk) == 0 else 1

    def body(a_ref, gv_ref, wm_ref, bt_ref, ya_ref):
        for n in range(per_step):
            rows = slice(n * chunk, (n + 1) * chunk)
            ua = _gelu(a_ref[rows, 0:d])
            vg = _gelu(a_ref[rows, d:2 * d])
            za = a_ref[rows, 2 * d:3 * d]
            rv = lax.rsqrt(jnp.mean(vg * vg, axis=-1, keepdims=True) + EPS)
            va = (vg * rv * gv_ref[...]).astype(BF16)
            gate = ua * (za * _sigmoid(za))
            for g in range(ng):
                sl = slice(g * cw, (g + 1) * cw)
                mixed = jnp.dot(wm_ref[g], va[:, sl], preferred_element_type=F32) + bt_ref[:, g:g + 1]
                ya_ref[rows, sl] = (gate[:, sl] * mixed).astype(BF16)

    tile = per_step * chunk
    return pl.pallas_call(
        body, name="branch_a_fwd",
        grid=(t // tile,),
        in_specs=[pl.BlockSpec((tile, d3), lambda i: (i, 0)), _const((1, d)), _const(wm.shape), _const(b_t.shape)],
        out_specs=pl.BlockSpec((tile, d), lambda i: (i, 0)),
        out_shape=jax.ShapeDtypeStruct((t, d), BF16),
        compiler_params=_params(("arbitrary",)),
    )(a_pre, g_v, wm, b_t)


def _branch_a_bwd(a_pre, dya, g_v, wm, wm_t, b_t):
    t = a_pre.shape[0]
    d = g_v.shape[1]
    d3 = 3 * d
    ng, chunk, _ = wm.shape
    cw = d // ng
    tile = CHUNKS_PER_STEP * chunk
    nsteps = t // tile

    def body(a_ref, dya_ref, gv_ref, wm_ref, wmt_ref, bt_ref, da_ref, gws_ref, gbt_ref, gnv_ref, db_acc):
        i = pl.program_id(0)

        @pl.when(i == 0)
        def _():
            gws_ref[...] = jnp.zeros_like(gws_ref)
            gnv_ref[...] = jnp.zeros_like(gnv_ref)
            db_acc[...] = jnp.zeros_like(db_acc)

        def one_chunk(c, carry):
            rows = pl.ds(pl.multiple_of(c * chunk, chunk), chunk)
            ua, dgelu_u = _gelu_and_grad(a_ref[rows, 0:d])
            vg, dgelu_v = _gelu_and_grad(a_ref[rows, d:2 * d])
            za = a_ref[rows, 2 * d:3 * d]
            sig = _sigmoid(za)
            sz = za * sig
            dsz = sig * (1.0 + za * (1.0 - sig))
            rv = lax.rsqrt(jnp.mean(vg * vg, axis=-1, keepdims=True) + EPS)
            nv = vg * rv
            gv = gv_ref[...]
            va = (nv * gv).astype(BF16)
            dya = dya_ref[rows, :]
            dmix = dya * ua * sz
            db_acc[...] += dmix
            dmix_b = dmix.astype(BF16)
            t_gate = dya * sz
            t_z = dya * ua * dsz
            dva_parts = []
            for g in range(ng):
                sl = slice(g * cw, (g + 1) * cw)
                mixed = jnp.dot(wm_ref[g], va[:, sl], preferred_element_type=F32) + bt_ref[:, g:g + 1]
                da_ref[rows, sl] = (t_gate[:, sl] * mixed * dgelu_u[:, sl]).astype(BF16)
                da_ref[rows, 2 * d + g * cw:2 * d + (g + 1) * cw] = (t_z[:, sl] * mixed).astype(BF16)
                gws_ref[g] += lax.dot_general(dmix_b[:, sl], va[:, sl], NT, preferred_element_type=F32)
                dva_parts.append(jnp.dot(wmt_ref[g], dmix_b[:, sl], preferred_element_type=F32))
            dva = jnp.concatenate(dva_parts, axis=1)
            gnv_ref[...] += jnp.sum(dva * nv, axis=0, keepdims=True)
            dnv = dva * gv
            dvg = rv * (dnv - nv * jnp.mean(dnv * nv, axis=-1, keepdims=True))
            da_ref[rows, d:2 * d] = (dvg * dgelu_v).astype(BF16)
            return carry

        lax.fori_loop(0, CHUNKS_PER_STEP, one_chunk, 0)

        @pl.when(i == nsteps - 1)
        def _():
            acc = db_acc[...]
            for g in range(ng):
                gbt_ref[:, g:g + 1] = jnp.sum(acc[:, g * cw:(g + 1) * cw], axis=1, keepdims=True)

    return pl.pallas_call(
        body, name="branch_a_bwd",
        grid=(nsteps,),
        in_specs=[pl.BlockSpec((tile, d3), lambda i: (i, 0)), pl.BlockSpec((tile, d), lambda i: (i, 0)),
                  _const((1, d)), _const(wm.shape), _const(wm_t.shape), _const(b_t.shape)],
        out_specs=[pl.BlockSpec((tile, d3), lambda i: (i, 0)), _const(wm.shape), _const(b_t.shape), _const((1, d))],
        out_shape=[jax.ShapeDtypeStruct((t, d3), BF16), jax.ShapeDtypeStruct(wm.shape, F32),
                   jax.ShapeDtypeStruct(b_t.shape, F32), jax.ShapeDtypeStruct((1, d), F32)],
        scratch_shapes=[pltpu.VMEM((chunk, d), F32)],
        compiler_params=_params(("arbitrary",)),
    )(a_pre, dya, g_v, wm, wm_t, b_t)


def _below_diagonal(n):
    return lax.broadcasted_iota(jnp.int32, (n, n), 0) > lax.broadcasted_iota(jnp.int32, (n, n), 1)


def _twice(tri):
    t = tri.astype(BF16)
    return jnp.concatenate([t, t], axis=0)


def _cumsum_mm(a, tri2):
    hi, lo = _split_bf16(a)
    return jnp.dot(jnp.concatenate([hi, lo], axis=1), tri2, preferred_element_type=F32)


LOG2E = 1.4426950408889634
_SIGN = 0x80000000


def _sb_block(q, k, scale, upper2, causal):
    z2 = lax.dot_general(q, k, NT, preferred_element_type=F32) * (scale * LOG2E)
    neg_abs = lax.bitcast_convert_type(lax.bitcast_convert_type(z2, jnp.uint32) | jnp.uint32(_SIGN), F32)
    l2 = jnp.log(1.0 + jnp.exp2(neg_abs)) * LOG2E
    log_beta = jnp.minimum(z2, 0.0) - l2
    lom = log_beta - z2
    if causal is not None:
        lom = jnp.where(causal, lom, 0.0)
    sx = _cumsum_mm(lom, upper2)
    return log_beta, sx, sx[:, 0:1] + lom[:, 0:1]


DEAD_LOG2 = -150.0


def _max_carry(carries):
    return jnp.max(functools.reduce(jnp.maximum, carries))


ZB_GROUP, GA_GROUP, GB_GROUP = 6, 7, 8


def _attn_specs(d, seq, nq, heads_per_step):
    hp_w = heads_per_step * (d // HEADS)
    n_hp = d // hp_w
    row_blk = lambda group: pl.BlockSpec((ATT_T, hp_w), lambda b, h, i: (b * nq + i, group * n_hp + h))
    seq_blk = lambda group: pl.BlockSpec((seq, hp_w), lambda b, h, i: (b, group * n_hp + h))
    return row_blk, seq_blk, n_hp


def _attn_fwd(qkv, proj, bsz, seq):
    t, d3 = qkv.shape
    d = d3 // 3
    hd = d // HEADS
    nq = seq // ATT_T
    scale = hd ** -0.5
    n_heads = ATT_HP
    row_blk, seq_blk, n_hp = _attn_specs(d, seq, nq, n_heads)

    def body(q_ref, k_ref, v_ref, zb_ref, o_ref, yb_ref):
        i = pl.program_id(2)
        causal = _below_diagonal(ATT_T)
        upper2 = _twice(causal)

        def step(kb, state, mask):
            rows = pl.ds(pl.multiple_of(kb * ATT_T, ATT_T), ATT_T)
            heads = [slice(h * hd, (h + 1) * hd) for h in range(n_heads)]
            scores = [_sb_block(q_ref[:, cols], k_ref[rows, cols], scale, upper2, mask) for cols in heads]
            new = []
            for cols, (carry, acc), (log_beta, sx, total) in zip(heads, state, scores):
                a = jnp.exp2(log_beta + sx + carry)
                if mask is not None:
                    a = jnp.where(mask, a, 0.0)
                acc = acc + jnp.dot(a.astype(BF16), v_ref[rows, cols], preferred_element_type=F32)
                new.append((carry + total, acc))
            return tuple(new)

        init = tuple((jnp.zeros((ATT_T, 1), F32), jnp.zeros((ATT_T, hd), F32)) for _ in range(n_heads))
        state = step(i, init, causal)
        def more(c):
            new = step(c[0], c[1], None)
            return c[0] - 1, new, _max_carry([s[0] for s in new])

        _, state, _ = lax.while_loop(lambda c: (c[0] >= 0) & (c[2] > DEAD_LOG2), more,
                                     (i - 1, state, _max_carry([s[0] for s in state])))
        for h in range(n_heads):
            cols = slice(h * hd, (h + 1) * hd)
            acc = state[h][1]
            zb = zb_ref[:, cols]
            o_ref[:, cols] = acc
            yb_ref[:, cols] = (acc * (zb * _sigmoid(zb))).astype(BF16)

    return pl.pallas_call(
        body, name="attn_fwd",
        grid=(bsz, n_hp, nq),
        in_specs=[row_blk(0), seq_blk(1), seq_blk(2), row_blk(ZB_GROUP)],
        out_specs=[row_blk(0), row_blk(0)],
        out_shape=[jax.ShapeDtypeStruct((t, d), F32), jax.ShapeDtypeStruct((t, d), BF16)],
        compiler_params=_params(("arbitrary", "arbitrary", "arbitrary")),
    )(qkv, qkv, qkv, proj)


def _attn_bwd(qkv, proj, o, dyb, bsz, seq):
    t, d3 = qkv.shape
    d = d3 // 3
    hd = d // HEADS
    nq = seq // ATT_T
    scale = hd ** -0.5
    row_blk, seq_blk, n_hp = _attn_specs(d, seq, nq, ATT_HP)

    def body(q_ref, k_ref, v_ref, zb_ref, o_ref, dyb_ref, dq_ref, dk_ref, dv_ref, dzb_ref,
             g_s, beta_s, dkt_acc, dvt_acc):
        i = pl.program_id(2)

        @pl.when(i == 0)
        def _():
            dkt_acc[...] = jnp.zeros_like(dkt_acc)
            dvt_acc[...] = jnp.zeros_like(dvt_acc)

        causal = _below_diagonal(ATT_T)
        upper2 = _twice(causal)
        lower2 = _twice(~causal)
        zb = zb_ref[...]
        sig = _sigmoid(zb)
        dyb_t = dyb_ref[...]
        do_f = dyb_t * (zb * sig)
        do = do_f.astype(BF16)
        do_t = do_f.T.astype(BF16)
        q_t = q_ref[...].astype(F32).T.astype(BF16)
        dzb_ref[...] = (dyb_t * o_ref[...] * (sig * (1.0 + zb * (1.0 - sig)))).astype(BF16)

        def sweep(kb, carries, mask):
            rows = pl.ds(pl.multiple_of(kb * ATT_T, ATT_T), ATT_T)
            heads = [slice(h * hd, (h + 1) * hd) for h in range(ATT_HP)]
            scores = [_sb_block(q_ref[:, cols], k_ref[rows, cols], scale, upper2, mask) for cols in heads]
            das = [lax.dot_general(do[:, cols], v_ref[rows, cols], NT, preferred_element_type=F32) for cols in heads]
            new = []
            for h, (cols, carry, (log_beta, sx, total), da) in enumerate(zip(heads, carries, scores, das)):
                a = jnp.exp2(log_beta + sx + carry)
                beta = jnp.exp2(log_beta)
                if mask is not None:
                    a = jnp.where(mask, a, 0.0)
                    beta = jnp.where(mask, beta, 0.0)
                g_s[h, kb] = a * da
                beta_s[h, kb] = beta
                dvt_acc[kb, cols, :] += jnp.dot(do_t[cols, :], a.astype(BF16), preferred_element_type=F32)
                new.append(carry + total)
            return tuple(new)

        carries = sweep(i, tuple(jnp.zeros((ATT_T, 1), F32) for _ in range(ATT_HP)), causal)

        def more(c):
            new = sweep(c[0], c[1], None)
            return c[0] - 1, new, _max_carry(new)

        last, _, _ = lax.while_loop(lambda c: (c[0] >= 0) & (c[2] > DEAD_LOG2), more, (i - 1, carries, _max_carry(carries)))
        first_kb = last + 1

        def back(kb, state):
            rows = pl.ds(pl.multiple_of(kb * ATT_T, ATT_T), ATT_T)
            heads = [slice(h * hd, (h + 1) * hd) for h in range(ATT_HP)]
            sums = [_cumsum_mm(g_s[h, kb], lower2) for h in range(ATT_HP)]
            new = []
            for h, (cols, (p_carry, dq), px) in enumerate(zip(heads, state, sums)):
                dz = ((g_s[h, kb] - (p_carry + px) * beta_s[h, kb]) * scale).astype(BF16)
                dq = dq + jnp.dot(dz, k_ref[rows, cols], preferred_element_type=F32)
                dkt_acc[kb, cols, :] += jnp.dot(q_t[cols, :], dz, preferred_element_type=F32)
                new.append((p_carry + px[:, ATT_T - 1:ATT_T], dq))
            return tuple(new)

        init = tuple((jnp.zeros((ATT_T, 1), F32), jnp.zeros((ATT_T, hd), F32)) for _ in range(ATT_HP))
        state = lax.fori_loop(first_kb, i + 1, back, init)
        for h in range(ATT_HP):
            dq_ref[:, h * hd:(h + 1) * hd] = state[h][1].astype(BF16)

        @pl.when(i == nq - 1)
        def _():
            for kb in range(nq):
                dk_ref[kb * ATT_T:(kb + 1) * ATT_T, :] = dkt_acc[kb].T.astype(BF16)
                dv_ref[kb * ATT_T:(kb + 1) * ATT_T, :] = dvt_acc[kb].T.astype(BF16)

    out = jax.ShapeDtypeStruct((t, d), BF16)
    hp_w = ATT_HP * hd
    return pl.pallas_call(
        body, name="attn_bwd",
        grid=(bsz, n_hp, nq),
        in_specs=[row_blk(0), seq_blk(1), seq_blk(2), row_blk(ZB_GROUP), row_blk(0), row_blk(0)],
        out_specs=[row_blk(0), seq_blk(0), seq_blk(0), row_blk(0)],
        out_shape=[out, out, out, out],
        scratch_shapes=[pltpu.VMEM((ATT_HP, nq, ATT_T, ATT_T), F32), pltpu.VMEM((ATT_HP, nq, ATT_T, ATT_T), F32),
                        pltpu.VMEM((nq, hp_w, ATT_T), F32), pltpu.VMEM((nq, hp_w, ATT_T), F32)],
        compiler_params=_params(("arbitrary", "arbitrary", "arbitrary")),
    )(qkv, qkv, qkv, proj, o, dyb)


def _out_proj(ya, yb, g_pre, x2d, tgt, w_og, w_osb, w_out, g_f):
    t, d = x2d.shape

    def body(ya_ref, yb_ref, ga_ref, gb_ref, x_ref, tgt_ref, wog_ref, wosb_ref, wout_ref, gf_ref,
             dya_ref, dyb_ref, dg_ref, dx2_ref, loss_ref, gnf_ref, gwog_ref, gwosb_ref, gwout_ref):
        @pl.when(pl.program_id(0) == 0)
        def _():
            loss_ref[...] = jnp.zeros_like(loss_ref)
            gnf_ref[...] = jnp.zeros_like(gnf_ref)
            gwog_ref[...] = jnp.zeros_like(gwog_ref)
            gwosb_ref[...] = jnp.zeros_like(gwosb_ref)
            gwout_ref[...] = jnp.zeros_like(gwout_ref)

        ya = ya_ref[...]
        yb = yb_ref[...]
        pa = jnp.dot(ya, wog_ref[...], preferred_element_type=F32)
        pb = jnp.dot(yb, wosb_ref[...], preferred_element_type=F32)
        sga = _sigmoid(ga_ref[...])
        sgb = _sigmoid(gb_ref[...])
        merged = (sga * pa + sgb * pb).astype(BF16)
        x2 = x_ref[...] + jnp.dot(merged, wout_ref[...], preferred_element_type=F32)
        r2 = lax.rsqrt(jnp.mean(x2 * x2, axis=-1, keepdims=True) + EPS)
        n2 = x2 * r2
        gf = gf_ref[...]
        err = n2 * gf - tgt_ref[...]
        loss_ref[...] += 0.5 * jnp.sum(jnp.sum(err * err, axis=-1, keepdims=True), axis=0, keepdims=True) / d
        dy = err * (1.0 / d)
        gnf_ref[...] += jnp.sum(dy * n2, axis=0, keepdims=True)
        dn = dy * gf
        dx2 = r2 * (dn - n2 * jnp.mean(dn * n2, axis=-1, keepdims=True))
        dx2_ref[...] = dx2
        dx2_b = dx2.astype(BF16)
        dmerged = lax.dot_general(dx2_b, wout_ref[...], NT, preferred_element_type=F32)
        gwout_ref[...] += lax.dot_general(merged, dx2_b, TN, preferred_element_type=F32)
        dg_ref[:, 0:d] = (dmerged * pa * (sga * (1.0 - sga))).astype(BF16)
        dg_ref[:, d:2 * d] = (dmerged * pb * (sgb * (1.0 - sgb))).astype(BF16)
        dpa = (dmerged * sga).astype(BF16)
        dpb = (dmerged * sgb).astype(BF16)
        dya_ref[...] = lax.dot_general(dpa, wog_ref[...], NT, preferred_element_type=F32)
        dyb_ref[...] = lax.dot_general(dpb, wosb_ref[...], NT, preferred_element_type=F32)
        gwog_ref[...] += lax.dot_general(ya, dpa, TN, preferred_element_type=F32)
        gwosb_ref[...] += lax.dot_general(yb, dpb, TN, preferred_element_type=F32)

    row = lambda i: (i, 0)
    return pl.pallas_call(
        body, name="out_proj",
        grid=(t // TM,),
        in_specs=[pl.BlockSpec((TM, d), row), pl.BlockSpec((TM, d), row),
                  pl.BlockSpec((TM, d), lambda i: (i, GA_GROUP)), pl.BlockSpec((TM, d), lambda i: (i, GB_GROUP)),
                  pl.BlockSpec((TM, d), row), pl.BlockSpec((TM, d), row),
                  _resident((d, d)), _resident((d, d)), _resident((d, d)), _const((1, d))],
        out_specs=[pl.BlockSpec((TM, d), row), pl.BlockSpec((TM, d), row), pl.BlockSpec((TM, 2 * d), row),
                   pl.BlockSpec((TM, d), row), _const((1, 1)), _const((1, d)),
                   _const((d, d)), _const((d, d)), _const((d, d))],
        out_shape=[jax.ShapeDtypeStruct((t, d), F32), jax.ShapeDtypeStruct((t, d), F32),
                   jax.ShapeDtypeStruct((t, 2 * d), BF16), jax.ShapeDtypeStruct((t, d), F32),
                   jax.ShapeDtypeStruct((1, 1), F32), jax.ShapeDtypeStruct((1, d), F32),
                   jax.ShapeDtypeStruct((d, d), F32), jax.ShapeDtypeStruct((d, d), F32),
                   jax.ShapeDtypeStruct((d, d), F32)],
        compiler_params=_params(("arbitrary",)),
    )(ya, yb, g_pre, g_pre, x2d, tgt, w_og, w_osb, w_out, g_f)


def _dproj_pieces(d):
    return [(0, 0, 3), (1, 3, 1), (2, 4, 1), (3, 5, 1), (4, 6, 1), (5, 7, 2)]


def _in_proj_bwd_x(pieces, wg, x2d, g_in, dx2, gw16):
    t, d = x2d.shape
    ncol = wg.shape[2]
    segs = _segments(d, ncol)
    layout = _dproj_pieces(d)
    nsteps = t // TMX

    def body(da_ref, dq_ref, dk_ref, dv_ref, dzb_ref, dg_ref, w_hbm, x_ref, g_ref, dx2_ref, gw16_ref,
             gx_ref, gn_ref, recv_ref, w_ref, w_sems, send_sems, recv_sems):
        x_pos, y_pos, c_pos, chips = _place()

        def load(chip):
            return pltpu.make_async_copy(w_hbm.at[chip], w_ref.at[chip], w_sems.at[chip])

        @pl.when(pl.program_id(0) == 0)
        def _():
            for chip in range(N_CHIP):
                load(chip).start()

        def share(k, chunk):
            px, py = chips[k]
            return pltpu.make_async_remote_copy(
                src_ref=gw16_ref.at[:, chunk * ncol:(chunk + 1) * ncol], dst_ref=recv_ref.at[k],
                send_sem=send_sems.at[k], recv_sem=recv_sems.at[k], device_id=(px, py, c_pos), device_id_type=MESH)

        @pl.when(pl.program_id(0) == 0)
        def _():
            gn_ref[...] = jnp.zeros_like(gn_ref)
            for k, (px, py) in enumerate(chips):
                for chunk in range(N_CHIP):
                    @pl.when(2 * px + py == chunk)
                    def _(k=k, chunk=chunk):
                        share(k, chunk).start()

        @pl.when(pl.program_id(0) == nsteps - 1)
        def _():
            for k in range(N_CHIP - 1):
                share(k, 0).wait()

        refs = (da_ref, dq_ref, dk_ref, dv_ref, dzb_ref, dg_ref)
        dh = jnp.zeros((TMX, d), F32)
        loaded = set()
        for chip, c0, grp, s0, width in segs:
            if chip not in loaded:
                loaded.add(chip)

                @pl.when(pl.program_id(0) == 0)
                def _(chip=chip):
                    load(chip).wait()
            piece, first, _ = next(p for p in layout if p[1] <= grp < p[1] + p[2])
            off = (grp - first) * d + s0
            dh = dh + lax.dot_general(refs[piece][:, off:off + width], w_ref[chip, :, c0:c0 + width], NT,
                                      preferred_element_type=F32)
        x = x_ref[...]
        r = lax.rsqrt(jnp.mean(x * x, axis=-1, keepdims=True) + EPS)
        n = x * r
        gn_ref[...] += jnp.sum(dh * n, axis=0, keepdims=True)
        dn = dh * g_ref[...]
        gx_ref[...] = dx2_ref[...] + r * (dn - n * jnp.mean(dn * n, axis=-1, keepdims=True))

    row = lambda i: (i, 0)
    return pl.pallas_call(
        body, name="in_proj_bwd_x",
        grid=(nsteps,),
        in_specs=[pl.BlockSpec((TMX, p.shape[1]), row) for p in pieces]
        + [ANY, pl.BlockSpec((TMX, d), row), _const((1, d)), pl.BlockSpec((TMX, d), row), ANY],
        out_specs=[pl.BlockSpec((TMX, d), row), _const((1, d)), ANY],
        out_shape=[jax.ShapeDtypeStruct((t, d), F32), jax.ShapeDtypeStruct((1, d), F32),
                   jax.ShapeDtypeStruct((N_CHIP - 1, gw16.shape[0], ncol), BF16)],
        scratch_shapes=[pltpu.VMEM(wg.shape, BF16), pltpu.SemaphoreType.DMA((N_CHIP,)),
                        pltpu.SemaphoreType.DMA((N_CHIP - 1,)), pltpu.SemaphoreType.DMA((N_CHIP - 1,))],
        compiler_params=_params(("arbitrary",)),
    )(*pieces, wg, x2d, g_in, dx2, gw16)


def _in_proj_bwd_w(h_t, pieces, mats16, pack):
    d, t = h_t.shape
    nk = t // TKW
    half = d // 2
    layout = _dproj_pieces(d)
    n_mats = len(mats16)
    n_dev = 2 * N_CHIP
    flips = [(dx, dy, dc) for dx in (0, 1) for dy in (0, 1) for dc in (0, 1)][1:]

    def body(ht_ref, da_ref, dq_ref, dk_ref, dv_ref, dzb_ref, dg_ref, *rest):
        mat_refs, pack_ref = rest[:n_mats], rest[n_mats]
        gw_ref, sib_ref = rest[n_mats + 1:n_mats + 3]
        recv_refs, slots_ref = rest[n_mats + 3:2 * n_mats + 3], rest[2 * n_mats + 3]
        acc, stage, ring, ring_sems, pring, piece_sems, mat_send, mat_recv, pack_send, pack_recv, own_sem, stage_send, sib_recv = \
            rest[2 * n_mats + 4:]
        s = pl.program_id(0)
        i = pl.program_id(1)
        x_pos, y_pos, c_pos, chips = _place()
        me = 4 * x_pos + 2 * y_pos + c_pos

        refs = (da_ref, dq_ref, dk_ref, dv_ref, dzb_ref, dg_ref)
        step = s * nk + i

        def fetch(at):
            rows = pl.ds(pl.multiple_of((at % nk) * TKW, TKW), TKW)
            return pltpu.make_async_copy(ht_ref.at[:, rows], ring.at[at % HT_SLOTS], ring_sems.at[at % HT_SLOTS])

        def piece_block(at, piece, g):
            rows = pl.ds(pl.multiple_of((at % nk) * TKW, TKW), TKW)
            return pltpu.make_async_copy(refs[piece].at[rows, g * d:(g + 1) * d], pring.at[at % HT_SLOTS],
                                         piece_sems.at[at % HT_SLOTS])

        def start_both(at):
            fetch(at).start()
            for piece, first, count in layout:
                for g in range(count):
                    @pl.when(at // nk == first + g)
                    def _(piece=piece, g=g):
                        piece_block(at, piece, g).start()

        @pl.when(step == 0)
        def _():
            for at in range(HT_SLOTS - 1):
                start_both(jnp.int32(at))

        @pl.when(step + HT_SLOTS - 1 < N_SPLIT * nk)
        def _():
            start_both(step + HT_SLOTS - 1)

        fetch(step).wait()
        piece_block(step, 1, 0).wait()

        def to_sibling(slot, group):
            return pltpu.make_async_remote_copy(
                src_ref=stage.at[slot], dst_ref=sib_ref.at[:, group * d:(group + 1) * d],
                send_sem=stage_send.at[slot], recv_sem=sib_recv, device_id=(x_pos, y_pos, 1 - c_pos), device_id_type=MESH)

        def exchanges():
            cps = []
            for k, (px, py) in enumerate(chips):
                for a in range(n_mats):
                    cps.append(pltpu.make_async_remote_copy(
                        src_ref=mat_refs[a].at[2 * px + py], dst_ref=recv_refs[a].at[k],
                        send_sem=mat_send.at[a, k], recv_sem=mat_recv.at[a, k],
                        device_id=(px, py, c_pos), device_id_type=MESH))
            for k, (dx, dy, dc) in enumerate(flips):
                peer = (1 - x_pos if dx else x_pos, 1 - y_pos if dy else y_pos, 1 - c_pos if dc else c_pos)
                cps.append(pltpu.make_async_remote_copy(
                    src_ref=pack_ref, dst_ref=slots_ref.at[me], send_sem=pack_send.at[k], recv_sem=pack_recv.at[k],
                    device_id=peer, device_id_type=MESH))
            return cps, pltpu.make_async_copy(pack_ref, slots_ref.at[me], own_sem)

        @pl.when((s == 0) & (i == 0))
        def _():
            cps, own = exchanges()
            own.start()
            for cp in cps:
                cp.start()

        @pl.when(i == 0)
        def _():
            acc[...] = jnp.zeros_like(acc)

        acc[...] += jnp.dot(ring[step % HT_SLOTS], pring[step % HT_SLOTS], preferred_element_type=F32)

        @pl.when(i == nk - 1)
        def _():
            gw_ref[...] = acc[...]
            for slot in range(2):
                @pl.when(s % 2 == slot)
                def _(slot=slot):
                    @pl.when(s >= 2)
                    def _():
                        to_sibling(slot, 0).wait_send()
                    for other in range(2):
                        @pl.when(c_pos == 1 - other)
                        def _(other=other):
                            stage[slot] = acc[other * half:(other + 1) * half, :].astype(BF16)
                    for group in range(N_SPLIT):
                        @pl.when(s == group)
                        def _(group=group):
                            to_sibling(slot, group).start()

        @pl.when((s == N_SPLIT - 1) & (i == nk - 1))
        def _():
            for slot in range(2):
                to_sibling(slot, 0).wait_send()
            pltpu.make_async_remote_copy(
                src_ref=sib_ref, dst_ref=sib_ref, send_sem=stage_send.at[0], recv_sem=sib_recv,
                device_id=(x_pos, y_pos, c_pos), device_id_type=MESH).wait_recv()
            cps, own = exchanges()
            own.wait()
            for cp in cps:
                cp.wait()

    col_blk = pl.BlockSpec((d, d), lambda s, i: (0, s))
    outs = pl.pallas_call(
        body, name="in_proj_bwd_w",
        grid=(N_SPLIT, nk),
        in_specs=[ANY] * (1 + len(layout) + n_mats + 1),
        out_specs=[col_blk, ANY] + [ANY] * (n_mats + 1),
        out_shape=[jax.ShapeDtypeStruct((d, N_SPLIT * d), F32), jax.ShapeDtypeStruct((half, N_SPLIT * d), BF16)]
        + [jax.ShapeDtypeStruct((N_CHIP - 1,) + m.shape[1:], BF16) for m in mats16]
        + [jax.ShapeDtypeStruct((n_dev,) + pack.shape, F32)],
        scratch_shapes=[pltpu.VMEM((d, d), F32), pltpu.VMEM((2, half, d), BF16),
                        pltpu.VMEM((HT_SLOTS, d, TKW), BF16), pltpu.SemaphoreType.DMA((HT_SLOTS,)),
                        pltpu.VMEM((HT_SLOTS, TKW, d), BF16), pltpu.SemaphoreType.DMA((HT_SLOTS,)),
                        pltpu.SemaphoreType.DMA((n_mats, N_CHIP - 1)), pltpu.SemaphoreType.DMA((n_mats, N_CHIP - 1)),
                        pltpu.SemaphoreType.DMA((n_dev - 1,)), pltpu.SemaphoreType.DMA((n_dev - 1,)),
                        pltpu.SemaphoreType.DMA, pltpu.SemaphoreType.DMA((2,)), pltpu.SemaphoreType.DMA],
        compiler_params=_params(("arbitrary", "arbitrary")),
    )(h_t, *pieces, *mats16, pack)
    return outs[0], outs[1], outs[2:2 + n_mats], outs[2 + n_mats]


def _local_step(proj, qkv, x2d, tgt2d, bsz, seq, norm_v, w_s, b_s, w_og, w_osb, w_out, norm_final):
    d = x2d.shape[1]
    chunk = w_s.shape[-1]
    causal = jnp.tril(jnp.ones((chunk, chunk), dtype=bool))
    wm = jnp.where(causal[None], w_s, 0.0).astype(BF16)
    wm_t = jnp.swapaxes(wm, 1, 2)
    b_t = b_s.T

    ya = _branch_a_fwd(proj, norm_v, wm, b_t)
    o, yb = _attn_fwd(qkv, proj, bsz, seq)
    dya, dyb, dg, dx2, loss, g_nf, g_wog, g_wosb, g_wout = _out_proj(
        ya, yb, proj, x2d, tgt2d, w_og, w_osb, w_out, norm_final.reshape(1, d))
    dq, dk, dv, dzb = _attn_bwd(qkv, proj, o, dyb, bsz, seq)
    d_a, g_ws, g_bt, g_nv = _branch_a_bwd(proj, dya, norm_v, wm, wm_t, b_t)
    g_ws = jnp.where(causal[None], g_ws, 0.0)
    return loss, (d_a, dq, dk, dv, dzb, dg), dx2, g_nv, g_ws, g_bt.T, g_wog, g_wosb, g_wout, g_nf


def _row_tile(rows):
    return next(r for r in (128, 64, 32, 16, 8) if rows % r == 0)


def _cast_bf16(arrs):
    n = len(arrs)

    def body(*refs):
        for a_ref, o_ref in zip(refs[:n], refs[n:]):
            o_ref[...] = a_ref[...].astype(BF16)

    specs = [pl.BlockSpec((a.shape[0] // CAST_STEPS, a.shape[1]), lambda i: (i, 0)) for a in arrs]
    return pl.pallas_call(
        body, name="cast_bf16", grid=(CAST_STEPS,),
        in_specs=specs, out_specs=specs,
        out_shape=[jax.ShapeDtypeStruct(a.shape, BF16) for a in arrs],
        compiler_params=_params(("arbitrary",)),
    )(*arrs)


def _chip_half(full, from_sibling, core, chip, tile):
    half, n = from_sibling.shape

    def body(where_ref, own_ref, sib_ref, o32_ref, o16_ref):
        total = own_ref[...] + sib_ref[...].astype(F32)
        o16_ref[...] = total.astype(BF16)

        @pl.when(pl.program_id(0) == where_ref[1])
        def _():
            o32_ref[...] = total

    blk = pl.BlockSpec((half, tile), lambda j, where_ref: (0, j))
    return pl.pallas_call(
        body, name="chip_half",
        grid_spec=pltpu.PrefetchScalarGridSpec(
            num_scalar_prefetch=1, grid=(n // tile,),
            in_specs=[pl.BlockSpec((half, tile), lambda j, where_ref: (where_ref[0], j)), blk],
            out_specs=[pl.BlockSpec((half, tile), lambda j, where_ref: (0, 0)), blk]),
        out_shape=[jax.ShapeDtypeStruct((half, tile), F32), jax.ShapeDtypeStruct((half, n), BF16)],
        compiler_params=_params(("arbitrary",)),
    )(jnp.stack([core, chip]).astype(jnp.int32), full, from_sibling)


def _add_received(fulls, recvs, chip, by_cols):
    n = len(fulls)
    _, rows, cols = recvs[0].shape
    tr = _row_tile(rows)
    nb = rows // tr

    def body(chip_ref, *refs):
        for own_ref, recv_ref, o_ref in zip(refs[:n], refs[n:2 * n], refs[2 * n:]):
            s = own_ref[...]
            for k in range(N_CHIP - 1):
                s = s + recv_ref[k].astype(F32)
            o_ref[...] = s

    own_map = (lambda i, chip_ref: (i, chip_ref[0])) if by_cols else (lambda i, chip_ref: (chip_ref[0] * nb + i, 0))
    return pl.pallas_call(
        body, name="add_received",
        grid_spec=pltpu.PrefetchScalarGridSpec(
            num_scalar_prefetch=1, grid=(nb,),
            in_specs=[pl.BlockSpec((tr, cols), own_map)] * n
            + [pl.BlockSpec((N_CHIP - 1, tr, cols), lambda i, chip_ref: (0, i, 0))] * n,
            out_specs=[pl.BlockSpec((tr, cols), lambda i, chip_ref: (i, 0))] * n),
        out_shape=[jax.ShapeDtypeStruct((rows, cols), F32)] * n,
        compiler_params=_params(("arbitrary",)),
    )(chip.reshape(1).astype(jnp.int32), *fulls, *recvs)


def _adamw_math(w, m, v, g):
    new_m = ADAM_B1 * m + (1.0 - ADAM_B1) * g
    new_v = ADAM_B2 * v + (1.0 - ADAM_B2) * (g * g)
    m_hat = new_m / (1.0 - ADAM_B1 ** ADAM_STEP)
    v_hat = new_v / (1.0 - ADAM_B2 ** ADAM_STEP)
    return -ADAM_LR * (m_hat / (jnp.sqrt(v_hat) + ADAM_EPS) + ADAM_WD * w), new_m, new_v


def _adamw_pairs(ws, ms, vs, mines, theirs):
    n = len(ws)
    rows, cols = ws[0].shape
    tr = _row_tile(rows)

    def body(*refs):
        ins, outs = refs[:5 * n], refs[5 * n:]
        for a in range(n):
            w_ref, m_ref, v_ref, mine_ref, theirs_ref = ins[a::n]
            g_ref, d_ref, nm_ref, nv_ref = outs[4 * a:4 * a + 4]
            g = mine_ref[...] + theirs_ref[...]
            g_ref[...] = g
            d_ref[...], nm_ref[...], nv_ref[...] = _adamw_math(w_ref[...], m_ref[...], v_ref[...], g)

    spec = pl.BlockSpec((tr, cols), lambda i: (i, 0))
    out = jax.ShapeDtypeStruct((rows, cols), F32)
    outs = pl.pallas_call(
        body, name="adamw_pairs", grid=(rows // tr,),
        in_specs=[spec] * (5 * n), out_specs=[spec] * (4 * n), out_shape=[out] * (4 * n),
        compiler_params=_params(("arbitrary",)),
    )(*ws, *ms, *vs, *mines, *theirs)
    return [outs[4 * a:4 * a + 4] for a in range(n)]


def _adamw_halves(w, m, v, mine, theirs, core):
    rows, cols = w.shape
    tr = _row_tile(rows // 2)
    per_half = rows // 2 // tr

    def body(core_ref, w_ref, m_ref, v_ref, mine_ref, theirs_ref, g_ref, d_ref, nm_ref, nv_ref):
        is_mine = pl.program_id(0) // per_half == core_ref[0]
        for part, cond in ((mine_ref, is_mine), (theirs_ref, jnp.logical_not(is_mine))):
            @pl.when(cond)
            def _(part=part):
                g = part[...]
                g_ref[...] = g
                d_ref[...], nm_ref[...], nv_ref[...] = _adamw_math(w_ref[...], m_ref[...], v_ref[...], g)

    spec = pl.BlockSpec((tr, cols), lambda i, core_ref: (i, 0))

    def half_spec(own):
        def index(i, core_ref):
            in_core_half = i // per_half == core_ref[0]
            here = in_core_half if own else jnp.logical_not(in_core_half)
            return jnp.where(here, i % per_half, 0), 0
        return pl.BlockSpec((tr, cols), index)

    out = jax.ShapeDtypeStruct(w.shape, F32)
    return pl.pallas_call(
        body, name="adamw_halves",
        grid_spec=pltpu.PrefetchScalarGridSpec(
            num_scalar_prefetch=1, grid=(rows // tr,),
            in_specs=[spec] * 3 + [half_spec(True), half_spec(False)], out_specs=[spec] * 4),
        out_shape=[out] * 4,
        compiler_params=_params(("arbitrary",)),
    )(core.reshape(1).astype(jnp.int32), w, m, v, mine, theirs)


def _adamw_small(ws, ms, vs, slots_head, slots_tail):
    n_dev = slots_head.shape[0]
    n_par = len(ws)
    groups, chunk, _ = ws[4].shape

    def body(*refs):
        w_refs, m_refs, v_refs = refs[:n_par], refs[n_par:2 * n_par], refs[2 * n_par:3 * n_par]
        head_ref, tail_ref = refs[3 * n_par:3 * n_par + 2]
        out_refs, scalar_ref = refs[3 * n_par + 2:-1], refs[-1]

        def total(ref, rows, cols):
            g = ref[0, rows, cols]
            for i in range(1, n_dev):
                g = g + ref[i, rows, cols]
            return g

        def update(p, g, at=...):
            g_ref, d_ref, nm_ref, nv_ref = out_refs[4 * p:4 * p + 4]
            g_ref[at] = g
            d_ref[at], nm_ref[at], nv_ref[at] = _adamw_math(w_refs[p][at], m_refs[p][at], v_refs[p][at], g)

        every = slice(None)
        update(0, total(head_ref, slice(0, 1), every))
        update(1, total(tail_ref, slice(0, 1), every))
        update(2, total(tail_ref, slice(SLAB, SLAB + groups), slice(0, chunk)))
        update(3, total(tail_ref, slice(2 * SLAB, 2 * SLAB + 1), every))
        scalar_ref[...] = total(tail_ref, slice(3 * SLAB, 3 * SLAB + 1), slice(0, chunk))[:, 0:1]
        for grp in range(groups):
            update(4, total(tail_ref, slice(4 * SLAB, 4 * SLAB + chunk), slice(grp * chunk, (grp + 1) * chunk)), grp)

    vmem = pl.BlockSpec(memory_space=pltpu.VMEM)
    return pl.pallas_call(
        body, name="adamw_small", in_specs=[vmem] * (3 * n_par + 2), out_specs=[vmem] * (4 * n_par + 1),
        out_shape=[jax.ShapeDtypeStruct(w.shape, F32) for w in ws for _ in range(4)]
        + [jax.ShapeDtypeStruct((1, 1), F32)],
        compiler_params=pltpu.CompilerParams(vmem_limit_bytes=VMEM_LIMIT),
    )(*ws, *ms, *vs, slots_head, slots_tail)


ANY = pl.BlockSpec(memory_space=pl.ANY)


def _place():
    x, y, c = lax.axis_index("x"), lax.axis_index("y"), lax.axis_index("c")
    other_chips = [(1 - x, y), (x, 1 - y), (1 - x, 1 - y)]
    return x, y, c, other_chips


def _swap_and_gather(arrs, pack):
    n = len(arrs)
    n_dev = 2 * N_CHIP
    flips = [(dx, dy, dc) for dx in (0, 1) for dy in (0, 1) for dc in (0, 1)][1:]

    def body(*refs):
        ins, pack_ref = refs[:n], refs[n]
        outs, slots_ref = refs[n + 1:2 * n + 1], refs[2 * n + 1]
        send_sems, recv_sems, pack_send, pack_recv, own_sem = refs[2 * n + 2:]
        x, y, c, _ = _place()
        me = 4 * x + 2 * y + c
        own = pltpu.make_async_copy(pack_ref, slots_ref.at[me], own_sem)
        own.start()
        copies = []
        for k, (dx, dy, dc) in enumerate(flips):
            peer = (1 - x if dx else x, 1 - y if dy else y, 1 - c if dc else c)
            copies.append(pltpu.make_async_remote_copy(
                src_ref=pack_ref, dst_ref=slots_ref.at[me], send_sem=pack_send.at[k], recv_sem=pack_recv.at[k],
                device_id=peer, device_id_type=MESH))
        copies += [pltpu.make_async_remote_copy(
            src_ref=ins[a], dst_ref=outs[a], send_sem=send_sems.at[a], recv_sem=recv_sems.at[a],
            device_id=(x, y, 1 - c), device_id_type=MESH) for a in range(n)]
        for cp in copies:
            cp.start()
        for cp in copies:
            cp.wait()
        own.wait()

    return pl.pallas_call(
        body, name="swap_and_gather",
        in_specs=[ANY] * (n + 1), out_specs=[ANY] * (n + 1),
        out_shape=[jax.ShapeDtypeStruct(a.shape, a.dtype) for a in arrs] + [jax.ShapeDtypeStruct((n_dev,) + pack.shape, F32)],
        scratch_shapes=[pltpu.SemaphoreType.DMA((n,)), pltpu.SemaphoreType.DMA((n,)),
                        pltpu.SemaphoreType.DMA((n_dev - 1,)), pltpu.SemaphoreType.DMA((n_dev - 1,)),
                        pltpu.SemaphoreType.DMA],
    )(*arrs, pack)


SLAB = 8


def _slab(vec, d):
    return jnp.pad(vec.reshape(1, d), ((0, SLAB - 1), (0, 0)))


def _pack_tail(vec_nv, b_s, vec_nf, w_s, scalar):
    d = vec_nv.shape[-1]
    groups, chunk, _ = w_s.shape
    assert groups == SLAB and groups * chunk == d
    slabs = [_slab(vec_nv, d), jnp.pad(b_s, ((0, 0), (0, d - chunk))), _slab(vec_nf, d),
             jnp.pad(scalar, ((0, SLAB - 1), (0, d - 1)))]
    return jnp.concatenate(slabs + [jnp.swapaxes(w_s, 0, 1).reshape(chunk, d)], axis=0)


def kernel(x, norm_in, w_in, norm_v, w_s, b_s, w_o_gmlp, w_o_sb, w_out, norm_final, loss_target, m_norm_in, m_w_in, m_norm_v, m_w_s, m_b_s, m_w_o_gmlp, m_w_o_sb, m_w_out, m_norm_final, v_norm_in, v_w_in, v_norm_v, v_w_s, v_b_s, v_w_o_gmlp, v_w_o_sb, v_w_out, v_norm_final):
    d = x.shape[-1]
    ncol = w_in.shape[-1]
    nrow = w_o_gmlp.shape[-2]
    chip = 2 * lax.axis_index("x") + lax.axis_index("y")

    bsz, seq, _ = x.shape
    x2d = x.reshape(bsz * seq, d)
    shards = [w_in[0], w_o_gmlp[0], w_o_sb[0], w_out[0]]
    halves = [s16.reshape(2, s16.shape[0] // 2, s16.shape[1]) for s16 in _cast_bf16(shards)]
    proj, qkv, h_t, (wg, w_og, w_osb, w_o) = _in_proj_fwd(x2d, norm_in, halves)
    wg = wg.reshape(N_CHIP, d, ncol)

    loss, pieces, dx2, g_nv, g_ws, g_bs, g_wog, g_wosb, g_wout, g_nf = _local_step(
        proj, qkv, x2d, loss_target.reshape(bsz * seq, d), bsz, seq, norm_v, w_s[0], b_s[0],
        w_og.reshape(d, d), w_osb.reshape(d, d), w_o.reshape(d, d), norm_final)

    mats = [g_wog, g_wosb, g_wout]
    mats16 = [g16.reshape(N_CHIP, nrow, d) for g16 in _cast_bf16(mats)]
    g_win, from_sibling, recv_mats, slots_tail = _in_proj_bwd_w(
        h_t, pieces, mats16, _pack_tail(g_nv, g_bs, g_nf, g_ws, loss))
    core = lax.axis_index("c")
    half_own, half_win16 = _chip_half(g_win, from_sibling, core, chip, ncol)
    grad_x, g_nin, recv_win = _in_proj_bwd_x(pieces, wg, x2d, norm_in, dx2, half_win16)
    grad_x = grad_x.reshape(bsz, seq, d)

    sums = _add_received([half_own], [recv_win], jnp.zeros((), jnp.int32), True) + _add_received(
        mats, recv_mats, chip, False)
    *sibling_sums, slots_head = _swap_and_gather(sums, _slab(g_nin, d))
    stats = [_adamw_halves(w_in[0], m_w_in[0], v_w_in[0], sums[0], sibling_sums[0], core)] + _adamw_pairs(
        shards[1:], [m_w_o_gmlp[0], m_w_o_sb[0], m_w_out[0]], [v_w_o_gmlp[0], v_w_o_sb[0], v_w_out[0]],
        sums[1:], sibling_sums[1:])

    *small, loss = _adamw_small(
        [norm_in, norm_v, b_s[0], norm_final.reshape(1, d), w_s[0]],
        [m_norm_in, m_norm_v, m_b_s[0], m_norm_final.reshape(1, d), m_w_s[0]],
        [v_norm_in, v_norm_v, v_b_s[0], v_norm_final.reshape(1, d), v_w_s[0]], slots_head, slots_tail)

    out = []
    for kind, (win, wog, wosb, wout) in enumerate(zip(*stats)):
        nin, nv, bs, nf, ws = small[kind::4]
        out += [nin, win[None], nv, ws[None], bs[None], wog[None], wosb[None], wout[None], nf.reshape(d)]
    return (loss.reshape(()), grad_x, *out)
```
